```python
import jax, jax.numpy as jnp
from jax import lax
import numpy as np

D_MODEL = 1024
BATCH = 8
SEQ = 16384
DEPTH = 1

CHUNK = 64
Q_BLOCK = 2 * CHUNK
ATTN_WIDTH = D_MODEL // 2
CONV_WIDTH = D_MODEL - ATTN_WIDTH
HEAD_DIM = 64
N_HEADS = ATTN_WIDTH // HEAD_DIM
CONV_KERNEL = 31
D_FF = 4 * D_MODEL
IN_WIDTH = 3 * ATTN_WIDTH + N_HEADS + 2 * CONV_WIDTH
EPS = 1e-6

kernel_name = "hybrid_fox_conformer_conv_adaln_block"


def _rmsnorm(x, g):
    xf = x.astype(jnp.float32)
    y = xf * lax.rsqrt(jnp.mean(xf * xf, axis=-1, keepdims=True) + EPS)
    return (y * g.astype(jnp.float32)).astype(x.dtype)


def _layernorm(x, g, b):
    xf = x.astype(jnp.float32)
    mu = jnp.mean(xf, axis=-1, keepdims=True)
    var = jnp.mean(jnp.square(xf - mu), axis=-1, keepdims=True)
    y = (xf - mu) * lax.rsqrt(var + EPS)
    return (y * g.astype(jnp.float32) + b.astype(jnp.float32)).astype(x.dtype)


def _fox_attention(q, k, v, log_f):
    B, S, H, Dh = q.shape
    nb = S // Q_BLOCK
    F = jnp.cumsum(log_f, axis=1).transpose(0, 2, 1)
    qh = q.transpose(0, 2, 1, 3)
    kh = k.transpose(0, 2, 1, 3)
    vh = v.transpose(0, 2, 1, 3)
    q_blocks = qh.reshape(B, H, nb, Q_BLOCK, Dh).transpose(2, 0, 1, 3, 4)
    F_blocks = F.reshape(B, H, nb, Q_BLOCK).transpose(2, 0, 1, 3)
    k_pos = jnp.arange(S)
    scale = HEAD_DIM ** -0.5

    def block(args):
        qb, Fb, i = args
        logits = jnp.einsum('bhqd,bhkd->bhqk', qb, kh).astype(jnp.float32) * scale
        logits = logits + Fb[..., None] - F[:, :, None, :]
        q_pos = i * Q_BLOCK + jnp.arange(Q_BLOCK)
        mask = k_pos[None, :] <= q_pos[:, None]
        logits = jnp.where(mask[None, None], logits, -jnp.inf)
        p = jax.nn.softmax(logits, axis=-1)
        return jnp.einsum('bhqk,bhkd->bhqd', p.astype(vh.dtype), vh)

    out = lax.map(block, (q_blocks, F_blocks, jnp.arange(nb)))
    return out.transpose(1, 0, 3, 2, 4).reshape(B, S, H * Dh)


def _causal_depthwise_conv(u, w, b):
    K, C = w.shape
    u_pad = jnp.pad(u, ((0, 0), (K - 1, 0), (0, 0)))
    y = lax.conv_general_dilated(u_pad, w[:, None, :], window_strides=(1,), padding='VALID',
                                 dimension_numbers=('NWC', 'WIO', 'NWC'),
                                 feature_group_count=C)
    return y + b


def _fwd_setup_inputs(seed: int = 0) -> dict:
    key = jax.random.key(seed)
    ks = jax.random.split(key, 20)
    f32 = jnp.float32
    L, D, A, Cw, H = DEPTH, D_MODEL, ATTN_WIDTH, CONV_WIDTH, N_HEADS
    nrm = lambda k, shape, s: jax.random.normal(k, shape, f32) * s
    return {
        "x": jax.random.normal(ks[0], (BATCH, SEQ, D), f32),
        "c": jax.random.normal(ks[1], (BATCH, D), f32),
        "w_ada": nrm(ks[2], (L, D, 6 * D), 0.5 * D ** -0.5),
        "b_ada": nrm(ks[3], (L, 6 * D), 0.02),
        "norm1_g": 1.0 + nrm(ks[4], (L, D), 0.02),
        "w_in": nrm(ks[5], (L, D, IN_WIDTH), D ** -0.5),
        "q_norm_g": 1.0 + nrm(ks[6], (L, HEAD_DIM), 0.02),
        "k_norm_g": 1.0 + nrm(ks[7], (L, HEAD_DIM), 0.02),
        "b_f": jax.random.uniform(ks[8], (L, H), f32, minval=1.0, maxval=6.0),
        "conv_w": nrm(ks[9], (L, CONV_KERNEL, Cw), CONV_KERNEL ** -0.5),
        "conv_b": nrm(ks[10], (L, Cw), 0.02),
        "conv_ln_g": 1.0 + nrm(ks[11], (L, Cw), 0.02),
        "conv_ln_b": nrm(ks[12], (L, Cw), 0.02),
        "beta_attn": 1.0 + nrm(ks[13], (L, A), 0.02),
        "beta_conv": 1.0 + nrm(ks[14], (L, Cw), 0.02),
        "w_out": nrm(ks[15], (L, D, D), D ** -0.5),
        "norm2_g": 1.0 + nrm(ks[16], (L, D), 0.02),
        "w_ff1": nrm(ks[17], (L, D, D_FF), D ** -0.5),
        "w_ff2": nrm(ks[18], (L, D_FF, D), D_FF ** -0.5),
    }


def _fwd_reference(x, c, w_ada, b_ada, norm1_g, w_in, q_norm_g, k_norm_g, b_f, conv_w, conv_b,
              conv_ln_g, conv_ln_b, beta_attn, beta_conv, w_out, norm2_g, w_ff1, w_ff2):
    B, S, D = x.shape
    A, H = ATTN_WIDTH, N_HEADS
    split_pts = [A, 2 * A, 3 * A, 3 * A + H]
    for l in range(DEPTH):
        mod = jax.nn.silu(c) @ w_ada[l] + b_ada[l]
        sh1, sc1, g1, sh2, sc2, g2 = [m[:, None, :] for m in jnp.split(mod, 6, axis=-1)]

        h = _rmsnorm(x, norm1_g[l]) * (1 + sc1) + sh1
        z = h @ w_in[l]
        q, k, v, fg, conv_in = jnp.split(z, split_pts, axis=-1)

        q = _rmsnorm(q.reshape(B, S, H, HEAD_DIM), q_norm_g[l])
        k = _rmsnorm(k.reshape(B, S, H, HEAD_DIM), k_norm_g[l])
        v = v.reshape(B, S, H, HEAD_DIM)
        log_f = jax.nn.log_sigmoid(fg.astype(jnp.float32) + b_f[l].astype(jnp.float32))
        attn = _fox_attention(q, k, v, log_f)

        a_lin, a_gate = jnp.split(conv_in, 2, axis=-1)
        u = a_lin * jax.nn.sigmoid(a_gate)
        u = _causal_depthwise_conv(u, conv_w[l], conv_b[l])
        u = jax.nn.silu(_layernorm(u, conv_ln_g[l], conv_ln_b[l]))

        merged = jnp.concatenate([_rmsnorm(attn, beta_attn[l]), _rmsnorm(u, beta_conv[l])], axis=-1)
        x = x + g1 * (merged @ w_out[l])

        h = _rmsnorm(x, norm2_g[l]) * (1 + sc2) + sh2
        x = x + g2 * (jnp.square(jax.nn.relu(h @ w_ff1[l])) @ w_ff2[l])
    return x


import jax as _jax
import jax.numpy as _jnp

TWIN_FORMAT = 'train_step'
FWD_PARAMS = ['x', 'c', 'w_ada', 'b_ada', 'norm1_g', 'w_in', 'q_norm_g', 'k_norm_g', 'b_f', 'conv_w', 'conv_b', 'conv_ln_g', 'conv_ln_b', 'beta_attn', 'beta_conv', 'w_out', 'norm2_g', 'w_ff1', 'w_ff2']
TWIN_WEIGHTS = ['w_ada', 'b_ada', 'norm1_g', 'w_in', 'q_norm_g', 'k_norm_g', 'b_f', 'conv_w', 'conv_b', 'conv_ln_g', 'conv_ln_b', 'beta_attn', 'beta_conv', 'w_out', 'norm2_g', 'w_ff1', 'w_ff2']
TWIN_DIFF_INPUT = 'x'
TWIN_INPUTS = ['x', 'c', 'w_ada', 'b_ada', 'norm1_g', 'w_in', 'q_norm_g', 'k_norm_g', 'b_f', 'conv_w', 'conv_b', 'conv_ln_g', 'conv_ln_b', 'beta_attn', 'beta_conv', 'w_out', 'norm2_g', 'w_ff1', 'w_ff2', 'loss_target', 'm_w_ada', 'm_b_ada', 'm_norm1_g', 'm_w_in', 'm_q_norm_g', 'm_k_norm_g', 'm_b_f', 'm_conv_w', 'm_conv_b', 'm_conv_ln_g', 'm_conv_ln_b', 'm_beta_attn', 'm_beta_conv', 'm_w_out', 'm_norm2_g', 'm_w_ff1', 'm_w_ff2', 'v_w_ada', 'v_b_ada', 'v_norm1_g', 'v_w_in', 'v_q_norm_g', 'v_k_norm_g', 'v_b_f', 'v_conv_w', 'v_conv_b', 'v_conv_ln_g', 'v_conv_ln_b', 'v_beta_attn', 'v_beta_conv', 'v_w_out', 'v_norm2_g', 'v_w_ff1', 'v_w_ff2']
TWIN_OUTPUTS = ['loss', 'grad_x', 'grad_w_ada', 'grad_b_ada', 'grad_norm1_g', 'grad_w_in', 'grad_q_norm_g', 'grad_k_norm_g', 'grad_b_f', 'grad_conv_w', 'grad_conv_b', 'grad_conv_ln_g', 'grad_conv_ln_b', 'grad_beta_attn', 'grad_beta_conv', 'grad_w_out', 'grad_norm2_g', 'grad_w_ff1', 'grad_w_ff2', 'delta_w_ada', 'delta_b_ada', 'delta_norm1_g', 'delta_w_in', 'delta_q_norm_g', 'delta_k_norm_g', 'delta_b_f', 'delta_conv_w', 'delta_conv_b', 'delta_conv_ln_g', 'delta_conv_ln_b', 'delta_beta_attn', 'delta_beta_conv', 'delta_w_out', 'delta_norm2_g', 'delta_w_ff1', 'delta_w_ff2', 'new_m_w_ada', 'new_m_b_ada', 'new_m_norm1_g', 'new_m_w_in', 'new_m_q_norm_g', 'new_m_k_norm_g', 'new_m_b_f', 'new_m_conv_w', 'new_m_conv_b', 'new_m_conv_ln_g', 'new_m_conv_ln_b', 'new_m_beta_attn', 'new_m_beta_conv', 'new_m_w_out', 'new_m_norm2_g', 'new_m_w_ff1', 'new_m_w_ff2', 'new_v_w_ada', 'new_v_b_ada', 'new_v_norm1_g', 'new_v_w_in', 'new_v_q_norm_g', 'new_v_k_norm_g', 'new_v_b_f', 'new_v_conv_w', 'new_v_conv_b', 'new_v_conv_ln_g', 'new_v_conv_ln_b', 'new_v_beta_attn', 'new_v_beta_conv', 'new_v_w_out', 'new_v_norm2_g', 'new_v_w_ff1', 'new_v_w_ff2']
TWIN_LEAF_KINDS = {'loss': 'loss', 'grad_x': 'grad_x', 'grad_w_ada': 'grad_w', 'grad_b_ada': 'grad_w', 'grad_norm1_g': 'grad_w', 'grad_w_in': 'grad_w', 'grad_q_norm_g': 'grad_w', 'grad_k_norm_g': 'grad_w', 'grad_b_f': 'grad_w', 'grad_conv_w': 'grad_w', 'grad_conv_b': 'grad_w', 'grad_conv_ln_g': 'grad_w', 'grad_conv_ln_b': 'grad_w', 'grad_beta_attn': 'grad_w', 'grad_beta_conv': 'grad_w', 'grad_w_out': 'grad_w', 'grad_norm2_g': 'grad_w', 'grad_w_ff1': 'grad_w', 'grad_w_ff2': 'grad_w', 'delta_w_ada': 'delta_w', 'delta_b_ada': 'delta_w', 'delta_norm1_g': 'delta_w', 'delta_w_in': 'delta_w', 'delta_q_norm_g': 'delta_w', 'delta_k_norm_g': 'delta_w', 'delta_b_f': 'delta_w', 'delta_conv_w': 'delta_w', 'delta_conv_b': 'delta_w', 'delta_conv_ln_g': 'delta_w', 'delta_conv_ln_b': 'delta_w', 'delta_beta_attn': 'delta_w', 'delta_beta_conv': 'delta_w', 'delta_w_out': 'delta_w', 'delta_norm2_g': 'delta_w', 'delta_w_ff1': 'delta_w', 'delta_w_ff2': 'delta_w', 'new_m_w_ada': 'new_m', 'new_m_b_ada': 'new_m', 'new_m_norm1_g': 'new_m', 'new_m_w_in': 'new_m', 'new_m_q_norm_g': 'new_m', 'new_m_k_norm_g': 'new_m', 'new_m_b_f': 'new_m', 'new_m_conv_w': 'new_m', 'new_m_conv_b': 'new_m', 'new_m_conv_ln_g': 'new_m', 'new_m_conv_ln_b': 'new_m', 'new_m_beta_attn': 'new_m', 'new_m_beta_conv': 'new_m', 'new_m_w_out': 'new_m', 'new_m_norm2_g': 'new_m', 'new_m_w_ff1': 'new_m', 'new_m_w_ff2': 'new_m', 'new_v_w_ada': 'new_v', 'new_v_b_ada': 'new_v', 'new_v_norm1_g': 'new_v', 'new_v_w_in': 'new_v', 'new_v_q_norm_g': 'new_v', 'new_v_k_norm_g': 'new_v', 'new_v_b_f': 'new_v', 'new_v_conv_w': 'new_v', 'new_v_conv_b': 'new_v', 'new_v_conv_ln_g': 'new_v', 'new_v_conv_ln_b': 'new_v', 'new_v_beta_attn': 'new_v', 'new_v_beta_conv': 'new_v', 'new_v_w_out': 'new_v', 'new_v_norm2_g': 'new_v', 'new_v_w_ff1': 'new_v', 'new_v_w_ff2': 'new_v'}


def _forward(args):
    return _fwd_reference(*[args[k] for k in FWD_PARAMS])


def _output_shape():
    def fwd():
        inp = _fwd_setup_inputs(0)
        return _fwd_reference(*[inp[k] for k in FWD_PARAMS])
    out = _jax.eval_shape(fwd)
    return out.shape, out.dtype

N_MICROBATCH = 1
ADAM_LR = 0.001
ADAM_B1 = 0.9
ADAM_B2 = 0.999
ADAM_EPS = 1e-08
ADAM_WD = 0.01
ADAM_STEP = 10
PER_EXAMPLE_BATCH_AXIS = {'x': 0, 'c': 0, 'loss_target': 0}
SHARED_INPUTS = []
_WEIGHT_DTYPES = {'w_ada': _jnp.float32, 'b_ada': _jnp.float32, 'norm1_g': _jnp.float32, 'w_in': _jnp.float32, 'q_norm_g': _jnp.float32, 'k_norm_g': _jnp.float32, 'b_f': _jnp.float32, 'conv_w': _jnp.float32, 'conv_b': _jnp.float32, 'conv_ln_g': _jnp.float32, 'conv_ln_b': _jnp.float32, 'beta_attn': _jnp.float32, 'beta_conv': _jnp.float32, 'w_out': _jnp.float32, 'norm2_g': _jnp.float32, 'w_ff1': _jnp.float32, 'w_ff2': _jnp.float32}
MOMENT_SCALE = {'w_ada': 1.076182e+01, 'b_ada': 2.729961e+01, 'norm1_g': 2.524763e-01, 'w_in': 1.386954e+00, 'q_norm_g': 3.708534e-01, 'k_norm_g': 3.688860e-01, 'b_f': 2.382757e+00, 'conv_w': 9.473010e-01, 'conv_b': 7.230833e+00, 'conv_ln_g': 3.333732e+00, 'conv_ln_b': 4.865189e+00, 'beta_attn': 1.382811e+01, 'beta_conv': 1.246305e+01, 'w_out': 2.784628e+00, 'norm2_g': 4.837034e+01, 'w_ff1': 1.204310e+00, 'w_ff2': 5.019634e+00}


def _to_microbatches(a, axis):
    t = _jnp.moveaxis(a, axis, 0)
    t = t.reshape((N_MICROBATCH, t.shape[0] // N_MICROBATCH) + t.shape[1:])
    return _jnp.moveaxis(t, 1, axis + 1)


def setup_inputs(seed: int = 0) -> dict:
    inp = _fwd_setup_inputs(seed)
    key = _jax.random.fold_in(_jax.random.key(seed), 7919)
    shape, _ = _output_shape()
    out = dict(inp)
    out["loss_target"] = _jax.random.normal(_jax.random.fold_in(key, 0), shape, _jnp.float32)
    for i, name in enumerate(TWIN_WEIGHTS):
        w = inp[name].astype(_jnp.float32)
        if MOMENT_SCALE is None:
            s = _jnp.sqrt(_jnp.mean(_jnp.square(w)) + 1e-30)
        else:
            s = MOMENT_SCALE[name]
        km, kv = _jax.random.split(_jax.random.fold_in(key, i + 1))
        out[name] = w
        out["m_" + name] = s * _jax.random.normal(km, w.shape, _jnp.float32)
        out["v_" + name] = (s * s) * _jax.random.uniform(kv, w.shape, _jnp.float32, 0.5, 1.5)
    if N_MICROBATCH > 1:
        for name, axis in PER_EXAMPLE_BATCH_AXIS.items():
            out[name] = _to_microbatches(out[name], axis)
    return {'x': out['x'], 'c': out['c'], 'w_ada': out['w_ada'], 'b_ada': out['b_ada'], 'norm1_g': out['norm1_g'], 'w_in': out['w_in'], 'q_norm_g': out['q_norm_g'], 'k_norm_g': out['k_norm_g'], 'b_f': out['b_f'], 'conv_w': out['conv_w'], 'conv_b': out['conv_b'], 'conv_ln_g': out['conv_ln_g'], 'conv_ln_b': out['conv_ln_b'], 'beta_attn': out['beta_attn'], 'beta_conv': out['beta_conv'], 'w_out': out['w_out'], 'norm2_g': out['norm2_g'], 'w_ff1': out['w_ff1'], 'w_ff2': out['w_ff2'], 'loss_target': out['loss_target'], 'm_w_ada': out['m_w_ada'], 'm_b_ada': out['m_b_ada'], 'm_norm1_g': out['m_norm1_g'], 'm_w_in': out['m_w_in'], 'm_q_norm_g': out['m_q_norm_g'], 'm_k_norm_g': out['m_k_norm_g'], 'm_b_f': out['m_b_f'], 'm_conv_w': out['m_conv_w'], 'm_conv_b': out['m_conv_b'], 'm_conv_ln_g': out['m_conv_ln_g'], 'm_conv_ln_b': out['m_conv_ln_b'], 'm_beta_attn': out['m_beta_attn'], 'm_beta_conv': out['m_beta_conv'], 'm_w_out': out['m_w_out'], 'm_norm2_g': out['m_norm2_g'], 'm_w_ff1': out['m_w_ff1'], 'm_w_ff2': out['m_w_ff2'], 'v_w_ada': out['v_w_ada'], 'v_b_ada': out['v_b_ada'], 'v_norm1_g': out['v_norm1_g'], 'v_w_in': out['v_w_in'], 'v_q_norm_g': out['v_q_norm_g'], 'v_k_norm_g': out['v_k_norm_g'], 'v_b_f': out['v_b_f'], 'v_conv_w': out['v_conv_w'], 'v_conv_b': out['v_conv_b'], 'v_conv_ln_g': out['v_conv_ln_g'], 'v_conv_ln_b': out['v_conv_ln_b'], 'v_beta_attn': out['v_beta_attn'], 'v_beta_conv': out['v_beta_conv'], 'v_w_out': out['v_w_out'], 'v_norm2_g': out['v_norm2_g'], 'v_w_ff1': out['v_w_ff1'], 'v_w_ff2': out['v_w_ff2']}


def _loss(weights, diff, rest, loss_target):
    with _jax.named_scope("forward"):
        args = {**rest, TWIN_DIFF_INPUT: diff, **{k: w.astype(_WEIGHT_DTYPES[k]) for k, w in weights.items()}}
        y = _forward(args)
    with _jax.named_scope("loss_head"):
        err = _jnp.square(y.astype(_jnp.float32) - loss_target)
        return 0.5 * _jnp.sum(_jnp.mean(err, axis=-1)) if err.ndim else 0.5 * err


def _adamw(w, g, m, v):
    m = ADAM_B1 * m + (1.0 - ADAM_B1) * g
    v = ADAM_B2 * v + (1.0 - ADAM_B2) * _jnp.square(g)
    m_hat = m / (1.0 - ADAM_B1 ** ADAM_STEP)
    v_hat = v / (1.0 - ADAM_B2 ** ADAM_STEP)
    delta = -ADAM_LR * (m_hat / (_jnp.sqrt(v_hat) + ADAM_EPS) + ADAM_WD * w)
    return delta, m, v


def reference(x, c, w_ada, b_ada, norm1_g, w_in, q_norm_g, k_norm_g, b_f, conv_w, conv_b, conv_ln_g, conv_ln_b, beta_attn, beta_conv, w_out, norm2_g, w_ff1, w_ff2, loss_target, m_w_ada, m_b_ada, m_norm1_g, m_w_in, m_q_norm_g, m_k_norm_g, m_b_f, m_conv_w, m_conv_b, m_conv_ln_g, m_conv_ln_b, m_beta_attn, m_beta_conv, m_w_out, m_norm2_g, m_w_ff1, m_w_ff2, v_w_ada, v_b_ada, v_norm1_g, v_w_in, v_q_norm_g, v_k_norm_g, v_b_f, v_conv_w, v_conv_b, v_conv_ln_g, v_conv_ln_b, v_beta_attn, v_beta_conv, v_w_out, v_norm2_g, v_w_ff1, v_w_ff2):
    given = dict(x=x, c=c, w_ada=w_ada, b_ada=b_ada, norm1_g=norm1_g, w_in=w_in, q_norm_g=q_norm_g, k_norm_g=k_norm_g, b_f=b_f, conv_w=conv_w, conv_b=conv_b, conv_ln_g=conv_ln_g, conv_ln_b=conv_ln_b, beta_attn=beta_attn, beta_conv=beta_conv, w_out=w_out, norm2_g=norm2_g, w_ff1=w_ff1, w_ff2=w_ff2, loss_target=loss_target, m_w_ada=m_w_ada, m_b_ada=m_b_ada, m_norm1_g=m_norm1_g, m_w_in=m_w_in, m_q_norm_g=m_q_norm_g, m_k_norm_g=m_k_norm_g, m_b_f=m_b_f, m_conv_w=m_conv_w, m_conv_b=m_conv_b, m_conv_ln_g=m_conv_ln_g, m_conv_ln_b=m_conv_ln_b, m_beta_attn=m_beta_attn, m_beta_conv=m_beta_conv, m_w_out=m_w_out, m_norm2_g=m_norm2_g, m_w_ff1=m_w_ff1, m_w_ff2=m_w_ff2, v_w_ada=v_w_ada, v_b_ada=v_b_ada, v_norm1_g=v_norm1_g, v_w_in=v_w_in, v_q_norm_g=v_q_norm_g, v_k_norm_g=v_k_norm_g, v_b_f=v_b_f, v_conv_w=v_conv_w, v_conv_b=v_conv_b, v_conv_ln_g=v_conv_ln_g, v_conv_ln_b=v_conv_ln_b, v_beta_attn=v_beta_attn, v_beta_conv=v_beta_conv, v_w_out=v_w_out, v_norm2_g=v_norm2_g, v_w_ff1=v_w_ff1, v_w_ff2=v_w_ff2)
    weights = {n: given[n] for n in TWIN_WEIGHTS}
    shared = {n: given[n] for n in SHARED_INPUTS}
    per_example = {n: given[n] for n in ['x', 'c']}
    grad_fn = _jax.value_and_grad(_loss, argnums=(0, 1))

    def one_microbatch(ex, loss_target):
        ex = dict(ex)
        diff = ex.pop(TWIN_DIFF_INPUT)
        return grad_fn(weights, diff, {**shared, **ex}, loss_target)

    if N_MICROBATCH == 1:
        loss, (grad_w, grad_x) = one_microbatch(per_example, given["loss_target"])
    else:
        def body(carry, xs):
            loss_sum, grad_sum = carry
            l_k, (gw_k, gx_k) = one_microbatch(xs[0], xs[1])
            with _jax.named_scope("update"):
                return (loss_sum + l_k, _jax.tree.map(_jnp.add, grad_sum, gw_k)), gx_k

        init = (_jnp.zeros((), _jnp.float32), _jax.tree.map(_jnp.zeros_like, weights))
        (loss, grad_w), grad_x = _jax.lax.scan(body, init, (per_example, given["loss_target"]))
    with _jax.named_scope("update"):
        delta_w, new_m, new_v = {}, {}, {}
        for n in TWIN_WEIGHTS:
            delta_w[n], new_m[n], new_v[n] = _adamw(weights[n], grad_w[n], given["m_" + n], given["v_" + n])
    return (loss, grad_x, *[grad_w[n] for n in TWIN_WEIGHTS], *[delta_w[n] for n in TWIN_WEIGHTS],
            *[new_m[n] for n in TWIN_WEIGHTS], *[new_v[n] for n in TWIN_WEIGHTS])
```

```python
import functools

import numpy as np
import jax
import jax.numpy as jnp
from jax import lax
from jax.experimental import pallas as pl
from jax.experimental.pallas import tpu as pltpu

F32, BF16 = jnp.float32, jnp.bfloat16
HI = lax.Precision.HIGHEST
D = 1024
AW = 512
CW = 512
NH = 8
DH = 64
KC = 31
DFF = 4096
NP = 2688
EPS = 1e-6
NEG = -1e30
LANES = 128
VMEM_LIMIT = 56 * 2**20
NT = (((1,), (1,)), ((), ()))
TN = (((0,), (0,)), ((), ()))
LR, B1, B2, AEPS, WD, STEP = 0.001, 0.9, 0.999, 1e-08, 0.01, 10
N_DEV = 8
SMALL_IN = 11904
SMALL_OUT = 11136


def _cp(sem=None, vmem=VMEM_LIMIT):
    kw = dict(vmem_limit_bytes=vmem)
    if sem is not None:
        kw["dimension_semantics"] = sem
    return pltpu.CompilerParams(**kw)


def _rows(tm, n):
    return pl.BlockSpec((tm, n), lambda i: (i, 0))


def _const(shape):
    nd = len(shape)
    return pl.BlockSpec(shape, lambda *_: (0,) * nd)


def _sds(shape, dt):
    return jax.ShapeDtypeStruct(shape, dt)


def _lane(shape):
    return lax.broadcasted_iota(jnp.int32, shape, len(shape) - 1)


def _sigmoid(x):
    return 1.0 / (1.0 + jnp.exp(-x))


def _exchange(items, name):
    kinds = [k for _, k in items]
    srcs = [s for s, _ in items]
    out_shapes = []
    for s, k in items:
        if k in ("all8", "chip8"):
            out_shapes.append(_sds((N_DEV,) + s.shape[1:], s.dtype))
        elif k == "bcast8":
            out_shapes.append(_sds((N_DEV,) + s.shape, s.dtype))
        else:
            out_shapes.append(_sds((4,) + s.shape, s.dtype))
    n = len(items)
    masks = {k: ([2, 4, 6] if k == "chip4" else [1, 2, 3, 4, 5, 6, 7]) for k in set(kinds)}
    sem_index = {}
    for t, k in enumerate(kinds):
        for m in masks[k]:
            sem_index[(t, m)] = len(sem_index)
    n_sem = len(sem_index)

    def body(*refs):
        src_refs, dst_refs = refs[:n], refs[n:2 * n]
        send_sems, recv_sems, local_sems = refs[2 * n:]
        x, y, c = lax.axis_index("x"), lax.axis_index("y"), lax.axis_index("c")
        my_id = 4 * x + 2 * y + c
        my_chip = 2 * x + y

        def piece(t, dev_id, chip):
            k = kinds[t]
            if k == "all8":
                return src_refs[t].at[dev_id]
            if k == "chip8":
                return src_refs[t].at[chip]
            return src_refs[t]

        def slot(t):
            return dst_refs[t].at[my_chip if kinds[t] == "chip4" else my_id]

        copies = []
        for t in range(n):
            lc = pltpu.make_async_copy(piece(t, my_id, my_chip), slot(t), local_sems.at[t])
            lc.start()
            copies.append(lc)
            for m in masks[kinds[t]]:
                px = 1 - x if m & 4 else x
                py = 1 - y if m & 2 else y
                pc = 1 - c if m & 1 else c
                s = sem_index[(t, m)]
                rc = pltpu.make_async_remote_copy(
                    src_ref=piece(t, 4 * px + 2 * py + pc, 2 * px + py), dst_ref=slot(t),
                    send_sem=send_sems.at[s], recv_sem=recv_sems.at[s],
                    device_id=(px, py, pc), device_id_type=pl.DeviceIdType.MESH)
                rc.start()
                copies.append(rc)
        for cp in copies:
            cp.wait()

    outs = pl.pallas_call(
        body, name=name, out_shape=tuple(out_shapes),
        in_specs=[pl.BlockSpec(memory_space=pl.ANY)] * n,
        out_specs=tuple(pl.BlockSpec(memory_space=pl.ANY) for _ in range(n)),
        scratch_shapes=[pltpu.SemaphoreType.DMA((n_sem,)), pltpu.SemaphoreType.DMA((n_sem,)),
                        pltpu.SemaphoreType.DMA((n,))],
    )(*srcs)
    return list(outs)


def _mod_shard(c_all, w_ada, b_shard):
    n = w_ada.shape[1]

    def body(c_ref, w_ref, b_ref, o_ref, sc_ref):
        cv = c_ref[...]
        sc = cv * _sigmoid(cv)
        sc_ref[...] = sc
        o_ref[...] = jnp.dot(sc, w_ref[...], precision=HI, preferred_element_type=F32) + b_ref[...]

    bn = 512
    return pl.pallas_call(
        body, name="mod_shard", out_shape=(_sds((N_DEV, n), F32), _sds((N_DEV, D), F32)), grid=(n // bn,),
        in_specs=[_const((N_DEV, D)), pl.BlockSpec((D, bn), lambda j: (0, j)), pl.BlockSpec((1, bn), lambda j: (0, j))],
        out_specs=(pl.BlockSpec((N_DEV, bn), lambda j: (0, j)), _const((N_DEV, D))),
        compiler_params=_cp(("arbitrary",)),
    )(c_all, w_ada, b_shard)


def _head_sum_mats():
    e = np.zeros((AW, LANES), np.float32)
    for h in range(NH):
        e[h * DH:(h + 1) * DH, h] = 1.0
    return jnp.asarray(e), jnp.asarray(e.T.copy())


def _fwd_in(x, mod8, n1g, w_in_p, e512, et512, qg512, kg512, bf128, tm):
    S = x.shape[0]

    def body(x_ref, mod_ref, n1g_ref, w_ref, e_ref, et_ref, qg_ref, kg_ref, bf_ref,
             h1_ref, qh_ref, kh_ref, v_ref, qn_ref, kn_ref, rq_ref, rk_ref, fgb_ref, alin_ref, agate_ref, u0_ref):
        xv = x_ref[...]
        r1 = lax.rsqrt(jnp.mean(xv * xv, axis=-1, keepdims=True) + EPS)
        h = (xv * r1) * (n1g_ref[...] * (1.0 + mod_ref[1:2, :])) + mod_ref[0:1, :]
        hb = h.astype(BF16)
        h1_ref[...] = hb

        def seg(a, b):
            return jnp.dot(hb, w_ref[:, a:b], preferred_element_type=F32)

        def headnorm(t, g_ref, scale, n_ref, r_ref, o_ref):
            ss = jnp.dot(t * t, e_ref[...], precision=HI, preferred_element_type=F32)
            r = lax.rsqrt(ss * (1.0 / DH) + EPS)
            tn = t * jnp.dot(r, et_ref[...], precision=HI, preferred_element_type=F32)
            n_ref[...] = tn.astype(BF16)
            r_ref[...] = r
            o_ref[...] = (tn * (g_ref[...] * scale)).astype(BF16)

        headnorm(seg(0, 512), qg_ref, DH ** -0.5, qn_ref, rq_ref, qh_ref)
        headnorm(seg(512, 1024), kg_ref, 1.0, kn_ref, rk_ref, kh_ref)
        v_ref[...] = seg(1024, 1536).astype(BF16)
        alin = seg(1536, 2048)
        agate = seg(2048, 2560)
        alin_ref[...] = alin.astype(BF16)
        agate_ref[...] = agate.astype(BF16)
        u0_ref[...] = alin * _sigmoid(agate)
        fgb_ref[...] = seg(2560, NP) + bf_ref[...]

    bf = lambda: _sds((S, AW), BF16)
    return pl.pallas_call(
        body, name="fwd_in", grid=(S // tm,),
        out_shape=(_sds((S, D), BF16), bf(), bf(), bf(), bf(), bf(), _sds((S, LANES), F32), _sds((S, LANES), F32),
                   _sds((S, LANES), F32), bf(), bf(), _sds((S, CW), F32)),
        in_specs=[_rows(tm, D), _const((8, D)), _const((1, D)), _const((D, NP)), _const((AW, LANES)), _const((LANES, AW)),
                  _const((1, AW)), _const((1, AW)), _const((1, LANES))],
        out_specs=(_rows(tm, D), _rows(tm, AW), _rows(tm, AW), _rows(tm, AW), _rows(tm, AW), _rows(tm, AW),
                   _rows(tm, LANES), _rows(tm, LANES), _rows(tm, LANES), _rows(tm, AW), _rows(tm, AW), _rows(tm, CW)),
        compiler_params=_cp(("parallel",)),
    )(x, mod8, n1g, w_in_p, e512, et512, qg512, kg512, bf128)


def _split3(f):
    f1 = f.astype(BF16).astype(F32)
    f2 = (f - f1).astype(BF16).astype(F32)
    return f1, f2, f - f1 - f2


def _fwd_decay(fgb, qh, kh, tm):
    S = fgb.shape[0]

    def body(fgb_ref, qh_ref, kh_ref, qa_ref, ka_ref, carry_ref):
        @pl.when(pl.program_id(0) == 0)
        def _():
            carry_ref[...] = jnp.zeros_like(carry_ref)

        fb = fgb_ref[...]
        lf = jnp.minimum(fb, 0.0) - jnp.log1p(jnp.exp(-jnp.abs(fb)))
        tri = (lax.broadcasted_iota(jnp.int32, (tm, tm), 0) >= lax.broadcasted_iota(jnp.int32, (tm, tm), 1)).astype(F32)
        cs = jnp.dot(tri, lf, precision=HI, preferred_element_type=F32) + carry_ref[0:1, :]
        carry_ref[...] = jnp.broadcast_to(cs[tm - 1:tm, :], carry_ref.shape)
        lane = _lane((tm, LANES))
        ones = jnp.where((lane >= 67) & (lane < 70), 1.0, 0.0)
        ones_k = jnp.where((lane >= 64) & (lane < 67), 1.0, 0.0)
        for p in range(NH // 2):
            qp = qh_ref[:, p * LANES:(p + 1) * LANES].astype(F32)
            kp = kh_ref[:, p * LANES:(p + 1) * LANES].astype(F32)
            for hh in range(2):
                h = 2 * p + hh
                f1, f2, f3 = _split3(cs[:, h:h + 1])
                qb = qp if hh == 0 else pltpu.roll(qp, 64, 1)
                kb = kp if hh == 0 else pltpu.roll(kp, 64, 1)
                augq = jnp.where(lane == 64, f1, jnp.where(lane == 65, f2, jnp.where(lane == 66, f3, ones)))
                augk = jnp.where(lane == 67, -f1, jnp.where(lane == 68, -f2, jnp.where(lane == 69, -f3, ones_k)))
                qa_ref[h] = jnp.where(lane < DH, qb, augq).astype(BF16)
                ka_ref[h] = jnp.where(lane < DH, kb, augk).astype(BF16)

    hm = pl.BlockSpec((NH, tm, LANES), lambda i: (0, i, 0))
    return pl.pallas_call(
        body, name="fwd_decay", grid=(S // tm,),
        out_shape=(_sds((NH, S, LANES), BF16), _sds((NH, S, LANES), BF16)),
        in_specs=[_rows(tm, LANES), _rows(tm, AW), _rows(tm, AW)], out_specs=(hm, hm),
        scratch_shapes=[pltpu.VMEM((8, LANES), F32)], compiler_params=_cp(("arbitrary",)),
    )(fgb, qh, kh)


def _causal(t):
    return lax.broadcasted_iota(jnp.int32, (t, t), 0) >= lax.broadcasted_iota(jnp.int32, (t, t), 1)


def _causal_t(t):
    return lax.broadcasted_iota(jnp.int32, (t, t), 0) <= lax.broadcasted_iota(jnp.int32, (t, t), 1)


def _attn_fwd(qa, ka, v, tq):
    S = qa.shape[1]
    nq = S // tq
    rep = tq // LANES

    def body(qa_ref, ka_ref, v_ref, o_ref, lsec_ref, lser_ref, m_ref, l_ref, acc_ref):
        i = pl.program_id(1)
        sel_a = _lane((tq, LANES)) < DH
        m_ref[...] = jnp.full(m_ref.shape, NEG, F32)
        l_ref[...] = jnp.zeros_like(l_ref)
        acc_ref[...] = jnp.zeros_like(acc_ref)

        def kv_step(j, masked):
            start = pl.multiple_of(j * tq, tq)
            vb = v_ref[pl.ds(start, tq), :]
            alphas, outs = [], []
            for hh in range(2):
                s = lax.dot_general(qa_ref[hh], ka_ref[hh, pl.ds(start, tq), :], NT, preferred_element_type=F32)
                if masked:
                    s = jnp.where(_causal(tq), s, NEG)
                m_prev = m_ref[hh]
                m_new = jnp.maximum(m_prev, jnp.max(s, axis=1, keepdims=True))
                alpha = jnp.exp(m_prev - m_new)
                p = jnp.exp(s - jnp.tile(m_new, (1, rep)))
                l_ref[hh] = alpha * l_ref[hh] + jnp.sum(p, axis=1, keepdims=True)
                m_ref[hh] = m_new
                alphas.append(alpha)
                outs.append(jnp.dot(p.astype(BF16), vb, preferred_element_type=F32))
            acc_ref[...] = jnp.where(sel_a, alphas[0], alphas[1]) * acc_ref[...] + jnp.where(sel_a, outs[0], outs[1])

        kv_step(i, True)

        def loop(t, carry):
            kv_step(i - 1 - t, False)
            return carry

        lax.fori_loop(0, i, loop, 0)
        o_ref[...] = acc_ref[...] / jnp.where(sel_a, l_ref[0], l_ref[1])
        lse_a = m_ref[0] + jnp.log(l_ref[0])
        lse_b = m_ref[1] + jnp.log(l_ref[1])
        lsec_ref[0] = lse_a
        lsec_ref[1] = lse_b
        row = lax.broadcasted_iota(jnp.int32, (8, tq), 0)
        lser_ref[0, 0] = jnp.where(row == 0, lse_a.T[0:8, :], lse_b.T[0:8, :])

    return pl.pallas_call(
        body, name="attn_fwd", grid=(NH // 2, nq),
        out_shape=(_sds((S, AW), F32), _sds((NH, S, LANES), F32), _sds((NH // 2, nq, 8, tq), F32)),
        in_specs=[pl.BlockSpec((2, tq, LANES), lambda p, i: (p, i, 0)),
                  pl.BlockSpec((2, S, LANES), lambda p, i: (p, 0, 0)),
                  pl.BlockSpec((S, LANES), lambda p, i: (0, p))],
        out_specs=(pl.BlockSpec((tq, LANES), lambda p, i: (i, p)),
                   pl.BlockSpec((2, tq, LANES), lambda p, i: (p, i, 0)),
                   pl.BlockSpec((1, 1, 8, tq), lambda p, i: (p, i, 0, 0))),
        scratch_shapes=[pltpu.VMEM((2, tq, LANES), F32), pltpu.VMEM((2, tq, LANES), F32), pltpu.VMEM((tq, LANES), F32)],
        compiler_params=_cp(("parallel", "parallel")),
    )(qa, ka, v)


HALO = 32
CHUNK_ROWS = 64


def _halo_prev(tm):
    return pl.BlockSpec((HALO, CW), lambda i: (jnp.maximum(i * (tm // HALO) - 1, 0), 0))


def _fwd_conv(u0, w32, cb, lng, lnb, beta_c, tm):
    S = u0.shape[0]

    def body(cur_ref, prev_ref, w_ref, cb_ref, lng_ref, lnb_ref, beta_ref, u1_ref, mc_ref, ext_ref):
        i = pl.program_id(0)
        ext_ref[0:HALO, :] = jnp.where(i == 0, 0.0, prev_ref[...])
        ext_ref[HALO:, :] = cur_ref[...]
        for r0 in range(0, tm, CHUNK_ROWS):
            acc = jnp.zeros((CHUNK_ROWS, CW), F32) + cb_ref[...]
            for j in range(KC):
                acc = acc + w_ref[j:j + 1, :] * ext_ref[r0 + 2 + j:r0 + 2 + j + CHUNK_ROWS, :]
            u1_ref[r0:r0 + CHUNK_ROWS, :] = acc
        u1 = u1_ref[...]
        mu = jnp.mean(u1, axis=-1, keepdims=True)
        d = u1 - mu
        rstd = lax.rsqrt(jnp.mean(d * d, axis=-1, keepdims=True) + EPS)
        u2 = d * rstd * lng_ref[...] + lnb_ref[...]
        u3 = u2 * _sigmoid(u2)
        rc = lax.rsqrt(jnp.mean(u3 * u3, axis=-1, keepdims=True) + EPS)
        mc_ref[...] = (u3 * rc * beta_ref[...]).astype(BF16)

    return pl.pallas_call(
        body, name="fwd_conv", grid=(S // tm,),
        out_shape=(_sds((S, CW), F32), _sds((S, CW), BF16)),
        in_specs=[_rows(tm, CW), _halo_prev(tm), _const((HALO, CW)), _const((1, CW)), _const((1, CW)), _const((1, CW)),
                  _const((1, CW))],
        out_specs=(_rows(tm, CW), _rows(tm, CW)),
        scratch_shapes=[pltpu.VMEM((tm + HALO, CW), F32)], compiler_params=_cp(("parallel",)),
    )(u0, u0, w32, cb, lng, lnb, beta_c)


def _fwd_out(o_attn, mc, x, mod8, n2g, beta_a, w_out, tm):
    S = x.shape[0]

    def body(o_ref, mc_ref, x_ref, mod_ref, n2g_ref, beta_ref, w_ref, mg_ref, ob_ref, x2_ref, h2_ref):
        ov = o_ref[...]
        ra = lax.rsqrt(jnp.mean(ov * ov, axis=-1, keepdims=True) + EPS)
        ma = (ov * ra * beta_ref[...]).astype(BF16)
        mcv = mc_ref[...]
        mg_ref[:, 0:AW] = ma
        mg_ref[:, AW:D] = mcv
        o = (jnp.dot(ma, w_ref[0:AW, :], preferred_element_type=F32)
             + jnp.dot(mcv, w_ref[AW:D, :], preferred_element_type=F32))
        ob_ref[...] = o.astype(BF16)
        x2 = x_ref[...] + mod_ref[2:3, :] * o
        x2_ref[...] = x2
        r2 = lax.rsqrt(jnp.mean(x2 * x2, axis=-1, keepdims=True) + EPS)
        h2_ref[...] = ((x2 * r2) * (n2g_ref[...] * (1.0 + mod_ref[4:5, :])) + mod_ref[3:4, :]).astype(BF16)

    return pl.pallas_call(
        body, name="fwd_out", grid=(S // tm,),
        out_shape=(_sds((S, D), BF16), _sds((S, D), BF16), _sds((S, D), F32), _sds((S, D), BF16)),
        in_specs=[_rows(tm, AW), _rows(tm, CW), _rows(tm, D), _const((8, D)), _const((1, D)), _const((1, AW)),
                  _const((D, D))],
        out_specs=(_rows(tm, D), _rows(tm, D), _rows(tm, D), _rows(tm, D)),
        compiler_params=_cp(("parallel",)),
    )(o_attn, mc, x, mod8, n2g, beta_a, w_out)


def _fwd_ffn(h2, w1, w2, x2, tgt, mod8, tm):
    S = h2.shape[0]
    nk = w1.shape[0]
    bf = w1.shape[2]

    def body(h2_ref, w1_ref, w2_ref, x2_ref, tgt_ref, mod_ref, r_ref, dy_ref, loss_ref, dg2_ref, acc_ref):
        i, k = pl.program_id(0), pl.program_id(1)

        @pl.when((i == 0) & (k == 0))
        def _():
            loss_ref[...] = jnp.zeros_like(loss_ref)
            dg2_ref[...] = jnp.zeros_like(dg2_ref)

        r = jnp.maximum(jnp.dot(h2_ref[...], w1_ref[0], preferred_element_type=F32), 0.0)
        r_ref[...] = r.astype(BF16)
        part = jnp.dot((r * r).astype(BF16), w2_ref[0], preferred_element_type=F32)

        @pl.when(k == 0)
        def _():
            acc_ref[...] = part

        @pl.when(k > 0)
        def _():
            acc_ref[...] += part

        @pl.when(k == nk - 1)
        def _():
            f2 = acc_ref[...]
            e = x2_ref[...] + mod_ref[5:6, :] * f2 - tgt_ref[...]
            dy = e * (1.0 / D)
            dy_ref[...] = dy
            loss_ref[...] += 0.5 * jnp.sum(jnp.sum(e * dy, axis=1, keepdims=True), axis=0, keepdims=True)
            dg2_ref[...] += jnp.sum((dy * f2).reshape(tm // 8, 8, D), axis=0)

    return pl.pallas_call(
        body, name="fwd_ffn", grid=(S // tm, nk),
        out_shape=(_sds((S, DFF), BF16), _sds((S, D), F32), _sds((8, LANES), F32), _sds((8, D), F32)),
        in_specs=[pl.BlockSpec((tm, D), lambda i, k: (i, 0)), pl.BlockSpec((1, D, bf), lambda i, k: (k, 0, 0)),
                  pl.BlockSpec((1, bf, D), lambda i, k: (k, 0, 0)), pl.BlockSpec((tm, D), lambda i, k: (i, 0)),
                  pl.BlockSpec((tm, D), lambda i, k: (i, 0)), pl.BlockSpec((8, D), lambda i, k: (0, 0))],
        out_specs=(pl.BlockSpec((tm, bf), lambda i, k: (i, k)), pl.BlockSpec((tm, D), lambda i, k: (i, 0)),
                   pl.BlockSpec((8, LANES), lambda i, k: (0, 0)), pl.BlockSpec((8, D), lambda i, k: (0, 0))),
        scratch_shapes=[pltpu.VMEM((tm, D), F32)], compiler_params=_cp(("arbitrary", "arbitrary")),
    )(h2, w1, w2, x2, tgt, mod8)


def _bwd_ffn(dy, mod8, r, w1, w2, tm):
    S = dy.shape[0]
    nk = w1.shape[0]
    bf = w1.shape[2]

    def body(dy_ref, mod_ref, r_ref, w1_ref, w2_ref, df2_ref, df1_ref, dh2_ref):
        k = pl.program_id(1)
        df2 = (dy_ref[...] * mod_ref[5:6, :]).astype(BF16)

        @pl.when(k == 0)
        def _():
            df2_ref[...] = df2

        da = lax.dot_general(df2, w2_ref[0], NT, preferred_element_type=F32)
        df1 = (da * (2.0 * r_ref[...].astype(F32))).astype(BF16)
        df1_ref[...] = df1
        part = lax.dot_general(df1, w1_ref[0], NT, preferred_element_type=F32)

        @pl.when(k == 0)
        def _():
            dh2_ref[...] = part

        @pl.when(k > 0)
        def _():
            dh2_ref[...] += part

    return pl.pallas_call(
        body, name="bwd_ffn", grid=(S // tm, nk),
        out_shape=(_sds((S, D), BF16), _sds((S, DFF), BF16), _sds((S, D), F32)),
        in_specs=[pl.BlockSpec((tm, D), lambda i, k: (i, 0)), pl.BlockSpec((8, D), lambda i, k: (0, 0)),
                  pl.BlockSpec((tm, bf), lambda i, k: (i, k)), pl.BlockSpec((1, D, bf), lambda i, k: (k, 0, 0)),
                  pl.BlockSpec((1, bf, D), lambda i, k: (k, 0, 0))],
        out_specs=(pl.BlockSpec((tm, D), lambda i, k: (i, 0)), pl.BlockSpec((tm, bf), lambda i, k: (i, k)),
                   pl.BlockSpec((tm, D), lambda i, k: (i, 0))),
        compiler_params=_cp(("parallel", "arbitrary")),
    )(dy, mod8, r, w1, w2)


def _wgrad(a, b, name, square_a=False, tk=512, bm=1024, bn=1024):
    S, M = a.shape
    N = b.shape[1]
    bm, bn, tk = min(bm, M), min(bn, N), min(tk, S)

    def body(a_ref, b_ref, o_ref):
        av = a_ref[...]
        if square_a:
            af = av.astype(F32)
            av = (af * af).astype(BF16)
        part = lax.dot_general(av, b_ref[...], TN, preferred_element_type=F32)

        @pl.when(pl.program_id(2) == 0)
        def _():
            o_ref[...] = part

        @pl.when(pl.program_id(2) > 0)
        def _():
            o_ref[...] += part

    return pl.pallas_call(
        body, name=name, grid=(M // bm, N // bn, S // tk), out_shape=_sds((M, N), F32),
        in_specs=[pl.BlockSpec((tk, bm), lambda mi, ni, k: (k, mi)), pl.BlockSpec((tk, bn), lambda mi, ni, k: (k, ni))],
        out_specs=pl.BlockSpec((bm, bn), lambda mi, ni, k: (mi, ni)),
        compiler_params=_cp(("parallel", "parallel", "arbitrary")),
    )(a, b)


def _colsum8(t):
    return jnp.sum(t.reshape(t.shape[0] // 8, 8, t.shape[1]), axis=0)


def _bwd_mid(dh2, dy, x2, ob, o_attn, u1, mod8, n2g, beta_a, beta_c, lng, lnb, w_out, tm):
    S = dy.shape[0]

    def body(dh2_ref, dy_ref, x2_ref, ob_ref, oa_ref, u1_ref, mod_ref, n2g_ref, ba_ref, bc_ref, lng_ref, lnb_ref, w_ref,
             dx2_ref, do_ref, doa_ref, du1_ref, acc_d_ref, acc_h_ref):
        @pl.when(pl.program_id(0) == 0)
        def _():
            acc_d_ref[...] = jnp.zeros_like(acc_d_ref)
            acc_h_ref[...] = jnp.zeros_like(acc_h_ref)

        x2 = x2_ref[...]
        dh2 = dh2_ref[...]
        r2 = lax.rsqrt(jnp.mean(x2 * x2, axis=-1, keepdims=True) + EPS)
        xn2 = x2 * r2
        gain = n2g_ref[...] * (1.0 + mod_ref[4:5, :])
        dxn = dh2 * gain
        dx2 = dy_ref[...] + r2 * (dxn - xn2 * jnp.mean(dxn * xn2, axis=-1, keepdims=True))
        dx2_ref[...] = dx2
        t = dh2 * xn2
        acc_d_ref[0] += _colsum8(dh2)
        acc_d_ref[1] += _colsum8(t * n2g_ref[...])
        acc_d_ref[2] += _colsum8(t * (1.0 + mod_ref[4:5, :]))
        acc_d_ref[3] += _colsum8(dx2 * ob_ref[...].astype(F32))
        do = (dx2 * mod_ref[2:3, :]).astype(BF16)
        do_ref[...] = do
        dma = lax.dot_general(do, w_ref[0:AW, :], NT, preferred_element_type=F32)
        dmc = lax.dot_general(do, w_ref[AW:D, :], NT, preferred_element_type=F32)
        ov = oa_ref[...]
        ra = lax.rsqrt(jnp.mean(ov * ov, axis=-1, keepdims=True) + EPS)
        on = ov * ra
        acc_h_ref[0] += _colsum8(dma * on)
        don = dma * ba_ref[...]
        doa_ref[...] = (ra * (don - on * jnp.mean(don * on, axis=-1, keepdims=True))).astype(BF16)
        u1 = u1_ref[...]
        mu = jnp.mean(u1, axis=-1, keepdims=True)
        d = u1 - mu
        rstd = lax.rsqrt(jnp.mean(d * d, axis=-1, keepdims=True) + EPS)
        uh = d * rstd
        u2 = uh * lng_ref[...] + lnb_ref[...]
        sg = _sigmoid(u2)
        u3 = u2 * sg
        rc = lax.rsqrt(jnp.mean(u3 * u3, axis=-1, keepdims=True) + EPS)
        u3n = u3 * rc
        acc_h_ref[1] += _colsum8(dmc * u3n)
        du3n = dmc * bc_ref[...]
        du3 = rc * (du3n - u3n * jnp.mean(du3n * u3n, axis=-1, keepdims=True))
        du2 = du3 * (sg * (1.0 + u2 * (1.0 - sg)))
        acc_h_ref[2] += _colsum8(du2 * uh)
        acc_h_ref[3] += _colsum8(du2)
        duh = du2 * lng_ref[...]
        du1_ref[...] = rstd * (duh - jnp.mean(duh, axis=-1, keepdims=True)
                               - uh * jnp.mean(duh * uh, axis=-1, keepdims=True))

    return pl.pallas_call(
        body, name="bwd_mid", grid=(S // tm,),
        out_shape=(_sds((S, D), F32), _sds((S, D), BF16), _sds((S, AW), BF16), _sds((S, CW), F32),
                   _sds((4, 8, D), F32), _sds((4, 8, AW), F32)),
        in_specs=[_rows(tm, D), _rows(tm, D), _rows(tm, D), _rows(tm, D), _rows(tm, AW), _rows(tm, CW), _const((8, D)),
                  _const((1, D)), _const((1, AW)), _const((1, CW)), _const((1, CW)), _const((1, CW)), _const((D, D))],
        out_specs=(_rows(tm, D), _rows(tm, D), _rows(tm, AW), _rows(tm, CW), _const((4, 8, D)), _const((4, 8, AW))),
        compiler_params=_cp(("arbitrary",)),
    )(dh2, dy, x2, ob, o_attn, u1, mod8, n2g, beta_a, beta_c, lng, lnb, w_out)


def _attn_bwd_dq(qa, ka, v, do, o_attn, lsec, tq):
    S = qa.shape[1]
    nq = S // tq
    rep = tq // LANES

    def body(qa_ref, ka_ref, v_ref, do_ref, o_ref, lse_ref, dqa_ref, dr_ref, acc_ref):
        i = pl.program_id(1)
        sel_a = _lane((tq, LANES)) < DH
        dov = do_ref[...]
        prod = dov.astype(F32) * o_ref[...]
        zero = jnp.zeros_like(prod)
        deltas = [jnp.sum(jnp.where(sel_a, prod, zero), axis=1, keepdims=True),
                  jnp.sum(jnp.where(sel_a, zero, prod), axis=1, keepdims=True)]
        zb = jnp.zeros_like(dov)
        dos = [jnp.where(sel_a, dov, zb), jnp.where(sel_a, zb, dov)]
        acc_ref[...] = jnp.zeros_like(acc_ref)

        def kv_step(j, masked):
            start = pl.multiple_of(j * tq, tq)
            vb = v_ref[pl.ds(start, tq), :]
            for hh in range(2):
                kb = ka_ref[hh, pl.ds(start, tq), :]
                s = lax.dot_general(qa_ref[hh], kb, NT, preferred_element_type=F32)
                p = jnp.exp(s - jnp.tile(lse_ref[hh], (1, rep)))
                if masked:
                    p = jnp.where(_causal(tq), p, 0.0)
                dp = lax.dot_general(dos[hh], vb, NT, preferred_element_type=F32)
                ds = p * (dp - deltas[hh])
                acc_ref[hh] += jnp.dot(ds.astype(BF16), kb, preferred_element_type=F32)

        kv_step(i, True)

        def loop(t, carry):
            kv_step(i - 1 - t, False)
            return carry

        lax.fori_loop(0, i, loop, 0)
        dqa_ref[...] = acc_ref[...]
        row = lax.broadcasted_iota(jnp.int32, (8, tq), 0)
        da = jnp.broadcast_to(deltas[0], (tq, LANES)).T[0:8, :]
        db = jnp.broadcast_to(deltas[1], (tq, LANES)).T[0:8, :]
        dr_ref[0, 0] = jnp.where(row == 0, da, db)

    return pl.pallas_call(
        body, name="attn_bwd_dq", grid=(NH // 2, nq),
        out_shape=(_sds((NH, S, LANES), F32), _sds((NH // 2, nq, 8, tq), F32)),
        in_specs=[pl.BlockSpec((2, tq, LANES), lambda p, i: (p, i, 0)),
                  pl.BlockSpec((2, S, LANES), lambda p, i: (p, 0, 0)),
                  pl.BlockSpec((S, LANES), lambda p, i: (0, p)),
                  pl.BlockSpec((tq, LANES), lambda p, i: (i, p)),
                  pl.BlockSpec((tq, LANES), lambda p, i: (i, p)),
                  pl.BlockSpec((2, tq, LANES), lambda p, i: (p, i, 0))],
        out_specs=(pl.BlockSpec((2, tq, LANES), lambda p, i: (p, i, 0)),
                   pl.BlockSpec((1, 1, 8, tq), lambda p, i: (p, i, 0, 0))),
        scratch_shapes=[pltpu.VMEM((2, tq, LANES), F32)],
        compiler_params=_cp(("parallel", "parallel")),
    )(qa, ka, v, do, o_attn, lsec)


def _attn_bwd_dkv(qa, ka, v, do, lser, dr, tq):
    S = qa.shape[1]
    nq = S // tq

    def body(ka_ref, v_ref, qa_ref, do_ref, lse_ref, dr_ref, dka_ref, dv_ref, acck_ref, accv_ref):
        j = pl.program_id(1)
        sel_a = _lane((tq, LANES)) < DH
        vv = v_ref[...]
        zb = jnp.zeros_like(vv)
        vs = [jnp.where(sel_a, vv, zb), jnp.where(sel_a, zb, vv)]
        acck_ref[...] = jnp.zeros_like(acck_ref)
        accv_ref[...] = jnp.zeros_like(accv_ref)

        def q_step(i, masked):
            start = pl.multiple_of(i * tq, tq)
            dob = do_ref[pl.ds(start, tq), :]
            lse8 = lse_ref[0, i]
            dr8 = dr_ref[0, i]
            for hh in range(2):
                qb = qa_ref[hh, pl.ds(start, tq), :]
                st = lax.dot_general(ka_ref[hh], qb, NT, preferred_element_type=F32)
                pt = jnp.exp(st - lse8[hh:hh + 1, :])
                if masked:
                    pt = jnp.where(_causal_t(tq), pt, 0.0)
                accv_ref[hh] += jnp.dot(pt.astype(BF16), dob, preferred_element_type=F32)
                dpt = lax.dot_general(vs[hh], dob, NT, preferred_element_type=F32)
                dst = pt * (dpt - dr8[hh:hh + 1, :])
                acck_ref[hh] += jnp.dot(dst.astype(BF16), qb, preferred_element_type=F32)

        q_step(j, True)

        def loop(t, carry):
            q_step(j + 1 + t, False)
            return carry

        lax.fori_loop(0, nq - 1 - j, loop, 0)
        dka_ref[...] = acck_ref[...]
        dv_ref[...] = jnp.where(sel_a, accv_ref[0], accv_ref[1]).astype(BF16)

    return pl.pallas_call(
        body, name="attn_bwd_dkv", grid=(NH // 2, nq),
        out_shape=(_sds((NH, S, LANES), F32), _sds((S, AW), BF16)),
        in_specs=[pl.BlockSpec((2, tq, LANES), lambda p, j: (p, j, 0)),
                  pl.BlockSpec((tq, LANES), lambda p, j: (j, p)),
                  pl.BlockSpec((2, S, LANES), lambda p, j: (p, 0, 0)),
                  pl.BlockSpec((S, LANES), lambda p, j: (0, p)),
                  pl.BlockSpec((1, nq, 8, tq), lambda p, j: (p, 0, 0, 0)),
                  pl.BlockSpec((1, nq, 8, tq), lambda p, j: (p, 0, 0, 0))],
        out_specs=(pl.BlockSpec((2, tq, LANES), lambda p, j: (p, j, 0)),
                   pl.BlockSpec((tq, LANES), lambda p, j: (j, p))),
        scratch_shapes=[pltpu.VMEM((2, tq, LANES), F32), pltpu.VMEM((2, tq, LANES), F32)],
        compiler_params=_cp(("parallel", "parallel")),
    )(ka, v, qa, do, lser, dr)


def _bwd_conv(du1, u0, alin, agate, w32, tm):
    S = du1.shape[0]
    nt = S // tm

    def body(du_ref, dun_ref, u0_ref, u0p_ref, alin_ref, agate_ref, w_ref,
             dalin_ref, dagate_ref, dw_ref, db_ref, extd_ref, extu_ref, du0_ref):
        i = pl.program_id(0)

        @pl.when(i == 0)
        def _():
            dw_ref[...] = jnp.zeros_like(dw_ref)
            db_ref[...] = jnp.zeros_like(db_ref)

        extd_ref[0:tm, :] = du_ref[...]
        extd_ref[tm:, :] = jnp.where(i == nt - 1, 0.0, dun_ref[...])
        extu_ref[0:HALO, :] = jnp.where(i == 0, 0.0, u0p_ref[...])
        extu_ref[HALO:, :] = u0_ref[...]
        db_ref[...] += _colsum8(du_ref[...])
        for r0 in range(0, tm, CHUNK_ROWS):
            duc = du_ref[r0:r0 + CHUNK_ROWS, :]
            acc = jnp.zeros((CHUNK_ROWS, CW), F32)
            for j in range(KC):
                acc = acc + w_ref[j:j + 1, :] * extd_ref[r0 + 30 - j:r0 + 30 - j + CHUNK_ROWS, :]
                dw_ref[j] += _colsum8(duc * extu_ref[r0 + 2 + j:r0 + 2 + j + CHUNK_ROWS, :])
            du0_ref[r0:r0 + CHUNK_ROWS, :] = acc
        du0 = du0_ref[...]
        al = alin_ref[...].astype(F32)
        sg = _sigmoid(agate_ref[...].astype(F32))
        dalin_ref[...] = (du0 * sg).astype(BF16)
        dagate_ref[...] = (du0 * al * sg * (1.0 - sg)).astype(BF16)

    nxt = pl.BlockSpec((HALO, CW), lambda i: (jnp.minimum((i + 1) * (tm // HALO), S // HALO - 1), 0))
    return pl.pallas_call(
        body, name="bwd_conv", grid=(nt,),
        out_shape=(_sds((S, CW), BF16), _sds((S, CW), BF16), _sds((HALO, 8, CW), F32), _sds((8, CW), F32)),
        in_specs=[_rows(tm, CW), nxt, _rows(tm, CW), _halo_prev(tm), _rows(tm, CW), _rows(tm, CW), _const((HALO, CW))],
        out_specs=(_rows(tm, CW), _rows(tm, CW), _const((HALO, 8, CW)), _const((8, CW))),
        scratch_shapes=[pltpu.VMEM((tm + HALO, CW), F32), pltpu.VMEM((tm + HALO, CW), F32), pltpu.VMEM((tm, CW), F32)],
        compiler_params=_cp(("arbitrary",)),
    )(du1, du1, u0, u0, alin, agate, w32)


def _bwd_qk(dqa, dka, qn, kn, rq, rk, fgb, qg512, kg512, e512, et512, tm):
    S = qn.shape[0]
    nt = S // tm

    def body(dqa_ref, dka_ref, qn_ref, kn_ref, rq_ref, rk_ref, fgb_ref, qg_ref, kg_ref, e_ref, et_ref,
             dq_ref, dk_ref, dfg_ref, accg_ref, accb_ref, carry_ref):
        @pl.when(pl.program_id(0) == 0)
        def _():
            carry_ref[...] = jnp.zeros_like(carry_ref)
            accg_ref[...] = jnp.zeros_like(accg_ref)
            accb_ref[...] = jnp.zeros_like(accb_ref)

        lane = _lane((tm, LANES))
        sel_a = lane < DH
        df = jnp.zeros((tm, LANES), F32)
        for h in range(NH):
            col = dqa_ref[h][:, 64:65] - dka_ref[h][:, 67:68]
            df = jnp.where(lane == h, col, df)
        tri = (lax.broadcasted_iota(jnp.int32, (tm, tm), 0) <= lax.broadcasted_iota(jnp.int32, (tm, tm), 1)).astype(F32)
        dlf = jnp.dot(tri, df, precision=HI, preferred_element_type=F32) + carry_ref[0:1, :]
        carry_ref[...] = jnp.broadcast_to(dlf[0:1, :], carry_ref.shape)
        dfg = jnp.where(lane < NH, dlf * _sigmoid(-fgb_ref[...]), 0.0)
        dfg_ref[...] = dfg.astype(BF16)
        accb_ref[...] += _colsum8(dfg)

        def norm_bwd(src_ref, n_ref, r_ref, g_ref, scale, slot):
            pairs = []
            for p in range(NH // 2):
                b = pltpu.roll(src_ref[2 * p + 1], 64, 1)
                pairs.append(jnp.where(sel_a, src_ref[2 * p], b))
            dh = jnp.concatenate(pairs, axis=1) * scale
            tn = n_ref[...].astype(F32)
            accg_ref[slot] += _colsum8(dh * tn)
            dn = dh * g_ref[...]
            mean = jnp.dot(dn * tn, e_ref[...], precision=HI, preferred_element_type=F32) * (1.0 / DH)
            corr = jnp.dot(mean, et_ref[...], precision=HI, preferred_element_type=F32)
            rf = jnp.dot(r_ref[...], et_ref[...], precision=HI, preferred_element_type=F32)
            return (rf * (dn - tn * corr)).astype(BF16)

        dq_ref[...] = norm_bwd(dqa_ref, qn_ref, rq_ref, qg_ref, DH ** -0.5, 0)
        dk_ref[...] = norm_bwd(dka_ref, kn_ref, rk_ref, kg_ref, 1.0, 1)

    rev = lambda n: pl.BlockSpec((tm, n), lambda i: (nt - 1 - i, 0))
    hm = pl.BlockSpec((NH, tm, LANES), lambda i: (0, nt - 1 - i, 0))
    dq, dk, dfg, accg, accb = pl.pallas_call(
        body, name="bwd_qk", grid=(nt,),
        out_shape=(_sds((S, AW), BF16), _sds((S, AW), BF16), _sds((S, LANES), BF16), _sds((2, 8, AW), F32),
                   _sds((8, LANES), F32)),
        in_specs=[hm, hm, rev(AW), rev(AW), rev(LANES), rev(LANES), rev(LANES), _const((1, AW)), _const((1, AW)),
                  _const((AW, LANES)), _const((LANES, AW))],
        out_specs=(rev(AW), rev(AW), rev(LANES), _const((2, 8, AW)), _const((8, LANES))),
        scratch_shapes=[pltpu.VMEM((8, LANES), F32)], compiler_params=_cp(("arbitrary",)),
    )(dqa, dka, qn, kn, rq, rk, fgb, qg512, kg512, e512, et512)
    return dq, dk, dfg, accg, accb


def _bwd_in(dq, dk, dv, dalin, dagate, dfg, w_in_p, x, dx2, mod8, n1g, tm):
    S = x.shape[0]

    def body(dq_ref, dk_ref, dv_ref, dal_ref, dag_ref, dfg_ref, w_ref, x_ref, dx2_ref, mod_ref, n1g_ref,
             dx_ref, acc_ref):
        @pl.when(pl.program_id(0) == 0)
        def _():
            acc_ref[...] = jnp.zeros_like(acc_ref)

        def part(ref, a, b):
            return lax.dot_general(ref[...], w_ref[:, a:b], NT, preferred_element_type=F32)

        dh = (part(dq_ref, 0, 512) + part(dk_ref, 512, 1024) + part(dv_ref, 1024, 1536) + part(dal_ref, 1536, 2048)
              + part(dag_ref, 2048, 2560) + part(dfg_ref, 2560, NP))
        xv = x_ref[...]
        r1 = lax.rsqrt(jnp.mean(xv * xv, axis=-1, keepdims=True) + EPS)
        xn = xv * r1
        gain = n1g_ref[...] * (1.0 + mod_ref[1:2, :])
        t = dh * xn
        acc_ref[0] += _colsum8(dh)
        acc_ref[1] += _colsum8(t * n1g_ref[...])
        acc_ref[2] += _colsum8(t * (1.0 + mod_ref[1:2, :]))
        dxn = dh * gain
        dx_ref[...] = dx2_ref[...] + r1 * (dxn - xn * jnp.mean(dxn * xn, axis=-1, keepdims=True))

    return pl.pallas_call(
        body, name="bwd_in", grid=(S // tm,),
        out_shape=(_sds((S, D), F32), _sds((3, 8, D), F32)),
        in_specs=[_rows(tm, AW), _rows(tm, AW), _rows(tm, AW), _rows(tm, CW), _rows(tm, CW), _rows(tm, LANES),
                  _const((D, NP)), _rows(tm, D), _rows(tm, D), _const((8, D)), _const((1, D))],
        out_specs=(_rows(tm, D), _const((3, 8, D))),
        compiler_params=_cp(("arbitrary",)),
    )(dq, dk, dv, dalin, dagate, dfg, w_in_p, x, dx2, mod8, n1g)


def _adam(w, g, m, v):
    m_new = B1 * m + (1.0 - B1) * g
    v_new = B2 * v + (1.0 - B2) * (g * g)
    m_hat = m_new / (1.0 - B1 ** STEP)
    v_hat = v_new / (1.0 - B2 ** STEP)
    delta = -LR * (m_hat / (jnp.sqrt(v_hat) + AEPS) + WD * w)
    return delta, m_new, v_new


def _reduce_adamw(slots, w, m, v, name, tr=256):
    ns, R, C = slots.shape
    tr = tr if R % tr == 0 else R

    def body(s_ref, w_ref, m_ref, v_ref, g_ref, d_ref, mo_ref, vo_ref):
        g = s_ref[0].astype(F32)
        for k in range(1, ns):
            g = g + s_ref[k].astype(F32)
        g_ref[...] = g
        d_ref[...], mo_ref[...], vo_ref[...] = _adam(w_ref[...], g, m_ref[...], v_ref[...])

    blk = pl.BlockSpec((tr, C), lambda i: (i, 0))
    return pl.pallas_call(
        body, name=name, grid=(R // tr,), out_shape=tuple(_sds((R, C), F32) for _ in range(4)),
        in_specs=[pl.BlockSpec((ns, tr, C), lambda i: (0, i, 0)), blk, blk, blk], out_specs=(blk, blk, blk, blk),
        compiler_params=_cp(("parallel",)),
    )(slots, w, m, v)


def _ada_adamw(sct, dmod, w, m, v):
    R, C = w.shape
    tr, bc = 256, 512

    def body(sct_ref, dm_ref, w_ref, m_ref, v_ref, g_ref, d_ref, mo_ref, vo_ref):
        g = sct_ref[:, 0:1] * dm_ref[0:1, :]
        for b in range(1, N_DEV):
            g = g + sct_ref[:, b:b + 1] * dm_ref[b:b + 1, :]
        g_ref[...] = g
        d_ref[...], mo_ref[...], vo_ref[...] = _adam(w_ref[...], g, m_ref[...], v_ref[...])

    blk = pl.BlockSpec((tr, bc), lambda i, j: (i, j))
    return pl.pallas_call(
        body, name="ada_adamw", grid=(R // tr, C // bc), out_shape=tuple(_sds((R, C), F32) for _ in range(4)),
        in_specs=[pl.BlockSpec((tr, N_DEV), lambda i, j: (i, 0)), pl.BlockSpec((N_DEV, bc), lambda i, j: (0, j)),
                  blk, blk, blk],
        out_specs=(blk, blk, blk, blk), compiler_params=_cp(("parallel", "parallel")),
    )(sct, dmod, w, m, v)


def _small_reduce(slots, fold):
    def body(s_ref, f_ref, o_ref):
        tot = s_ref[0:1, :]
        for k in range(1, N_DEV):
            tot = tot + s_ref[k:k + 1, :]
        o_ref[:, 0:6144] = tot[:, 0:6144]
        o_ref[:, 6144:7168] = tot[:, 6144:7168]
        for t, src in enumerate((8192, 8704)):
            v8 = jnp.broadcast_to(tot[:, src:src + AW], (8, AW))
            o_ref[:, 7168 + t * LANES:7168 + (t + 1) * LANES] = jnp.dot(
                v8, f_ref[...], precision=HI, preferred_element_type=F32)[0:1, :]
        o_ref[:, 7424:7552] = tot[:, 9216:9344]
        o_ref[:, 7552:10112] = tot[:, 9344:11904]
        o_ref[:, 10112:SMALL_OUT] = tot[:, 7168:8192]

    return pl.pallas_call(
        body, name="small_reduce", out_shape=_sds((1, SMALL_OUT), F32),
        in_specs=[pl.BlockSpec(memory_space=pltpu.VMEM), pl.BlockSpec(memory_space=pltpu.VMEM)],
        out_specs=pl.BlockSpec(memory_space=pltpu.VMEM),
    )(slots, fold)


def _perm_in(w):
    pad = jnp.zeros((w.shape[0], NP - 2568), w.dtype)
    return jnp.concatenate([w[:, :1536], w[:, 1544:2568], w[:, 1536:1544], pad], axis=1)


def _pad_lanes(vec, n=LANES):
    return jnp.pad(vec, ((0, 0), (0, n - vec.shape[1])))


def kernel(x, c, w_ada, b_ada, norm1_g, w_in, q_norm_g, k_norm_g, b_f, conv_w, conv_b, conv_ln_g, conv_ln_b, beta_attn, beta_conv, w_out, norm2_g, w_ff1, w_ff2, loss_target, m_w_ada, m_b_ada, m_norm1_g, m_w_in, m_q_norm_g, m_k_norm_g, m_b_f, m_conv_w, m_conv_b, m_conv_ln_g, m_conv_ln_b, m_beta_attn, m_beta_conv, m_w_out, m_norm2_g, m_w_ff1, m_w_ff2, v_w_ada, v_b_ada, v_norm1_g, v_w_in, v_q_norm_g, v_k_norm_g, v_b_f, v_conv_w, v_conv_b, v_conv_ln_g, v_conv_ln_b, v_beta_attn, v_beta_conv, v_w_out, v_norm2_g, v_w_ff1, v_w_ff2):
    S = x.shape[1]
    tm = min(256, S)
    tq = min(512, S // 2)
    xs, tgt = x[0], loss_target[0]
    chip = 2 * lax.axis_index("x") + lax.axis_index("y")
    e512, et512 = _head_sum_mats()

    conv_w32 = jnp.pad(conv_w[0], ((0, 1), (0, 0)))
    c_all, g_in, g_out, g_ff1, g_ff2, g_cw = _exchange(
        [(c, "bcast8"), (w_in[0].astype(BF16), "chip4"), (w_out[0].astype(BF16), "chip4"),
         (w_ff1[0].astype(BF16), "chip4"), (w_ff2[0].astype(BF16), "chip4"), (conv_w32, "chip4")], "gather_weights")
    c_all = c_all.reshape(N_DEV, D)
    w_in_p = _perm_in(jnp.transpose(g_in, (1, 0, 2)).reshape(D, 2568))
    w_out_f = g_out.reshape(D, D)
    w1, w2 = g_ff1, g_ff2
    cw32 = jnp.transpose(g_cw, (1, 0, 2)).reshape(HALO, CW)

    b_shard = lax.dynamic_slice(b_ada, (0, chip * 1536), (1, 1536))
    mod_rows, sc_all = _mod_shard(c_all, w_ada[0], b_shard)
    (mod_slots,) = _exchange([(mod_rows.reshape(N_DEV, 1, 1536), "all8")], "scatter_mod")
    mod = mod_slots.reshape(4, 2, 1536)[:, 0, :].reshape(6, D)
    mod8 = jnp.pad(mod, ((0, 2), (0, 0)))

    qg512 = jnp.tile(q_norm_g, (1, NH))
    kg512 = jnp.tile(k_norm_g, (1, NH))
    bf128 = _pad_lanes(b_f)

    h1, qh, kh, vb, qn, kn, rq, rk, fgb, alin, agate, u0 = _fwd_in(xs, mod8, norm1_g, w_in_p, e512, et512, qg512, kg512,
                                                                   bf128, tm)
    qa, ka = _fwd_decay(fgb, qh, kh, tm)
    o_attn, lsec, lser = _attn_fwd(qa, ka, vb, tq)
    u1, mc = _fwd_conv(u0, cw32, conv_b, conv_ln_g, conv_ln_b, beta_conv, tm)
    merged, ob, x2, h2 = _fwd_out(o_attn, mc, xs, mod8, norm2_g, beta_attn, w_out_f, tm)
    tf = min(512, S)
    r, dy, loss8, dg2 = _fwd_ffn(h2, w1, w2, x2, tgt, mod8, tf)

    df2, df1, dh2 = _bwd_ffn(dy, mod8, r, w1, w2, tf)
    gw_ff2 = _wgrad(r, df2, "wgrad_ff2", square_a=True)
    gw_ff1 = _wgrad(h2, df1, "wgrad_ff1")
    dx2, do, doa, du1, acc_d, acc_h = _bwd_mid(dh2, dy, x2, ob, o_attn, u1, mod8, norm2_g, beta_attn, beta_conv,
                                               conv_ln_g, conv_ln_b, w_out_f, tm)
    gw_out = _wgrad(merged, do, "wgrad_out")
    dqa, dr = _attn_bwd_dq(qa, ka, vb, doa, o_attn, lsec, tq)
    dka, dv = _attn_bwd_dkv(qa, ka, vb, doa, lser, dr, tq)
    dalin, dagate, dcw, dcb = _bwd_conv(du1, u0, alin, agate, cw32, tm)
    dq, dk, dfg, accg, accb = _bwd_qk(dqa, dka, qn, kn, rq, rk, fgb, qg512, kg512, e512, et512, tm)
    grad_x, acc1 = _bwd_in(dq, dk, dv, dalin, dagate, dfg, w_in_p, xs, dx2, mod8, norm1_g, tm)
    gw_in = jnp.concatenate(
        [_wgrad(h1, dq, "wgrad_in_q"), _wgrad(h1, dk, "wgrad_in_k"), _wgrad(h1, dv, "wgrad_in_v"),
         _wgrad(h1, dfg, "wgrad_in_f")[:, :NH], _wgrad(h1, dalin, "wgrad_in_a"), _wgrad(h1, dagate, "wgrad_in_g")],
        axis=1)

    s8 = lambda a: jnp.sum(a, axis=-2)
    a1, ad, ah = s8(acc1), s8(acc_d), s8(acc_h)
    small = jnp.concatenate(
        [a1[0], a1[1], ad[3], ad[0], ad[1], s8(dg2),
         a1[2], ad[2], s8(accg).reshape(-1), s8(accb), s8(dcb), ah[2], ah[3], ah[0], ah[1]]).reshape(1, SMALL_IN)
    gcw = s8(dcw)[:KC]
    small_s, s_in, s_out, s_ff1, s_ff2, s_cw = _exchange(
        [(small, "bcast8"),
         (jnp.transpose(gw_in.reshape(D, 4, 642), (1, 0, 2)), "chip8"),
         (gw_out.reshape(4, 256, D), "chip8"),
         (jnp.transpose(gw_ff1.reshape(D, 4, D), (1, 0, 2)), "chip8"),
         (gw_ff2.reshape(4, D, D), "chip8"),
         (jnp.transpose(gcw.reshape(KC, 4, LANES), (1, 0, 2)), "chip8")], "scatter_grads")
    small_s = small_s.reshape(N_DEV, SMALL_IN)

    g_in_, d_in, nm_in, nv_in = _reduce_adamw(s_in, w_in[0], m_w_in[0], v_w_in[0], "adamw_in")
    g_out_, d_out, nm_out, nv_out = _reduce_adamw(s_out, w_out[0], m_w_out[0], v_w_out[0], "adamw_out")
    g_f1, d_f1, nm_f1, nv_f1 = _reduce_adamw(s_ff1, w_ff1[0], m_w_ff1[0], v_w_ff1[0], "adamw_ff1")
    g_f2, d_f2, nm_f2, nv_f2 = _reduce_adamw(s_ff2, w_ff2[0], m_w_ff2[0], v_w_ff2[0], "adamw_ff2")
    g_cw_, d_cw, nm_cw, nv_cw = _reduce_adamw(s_cw, conv_w[0], m_conv_w[0], v_conv_w[0], "adamw_conv_w")
    dmod_shard = lax.dynamic_slice(small_s[:, :6 * D], (0, chip * 1536), (N_DEV, 1536))
    g_ada, d_ada, nm_ada, nv_ada = _ada_adamw(sc_all.T, dmod_shard, w_ada[0], m_w_ada[0], v_w_ada[0])

    fold = np.zeros((AW, LANES), np.float32)
    fold[np.arange(AW), np.arange(AW) % DH] = 1.0
    g_small = _small_reduce(small_s, jnp.asarray(fold))
    smalls = [b_ada, norm1_g, q_norm_g, k_norm_g, b_f, conv_b, conv_ln_g, conv_ln_b, beta_attn, beta_conv, norm2_g]
    m_smalls = [m_b_ada, m_norm1_g, m_q_norm_g, m_k_norm_g, m_b_f, m_conv_b, m_conv_ln_g, m_conv_ln_b, m_beta_attn,
                m_beta_conv, m_norm2_g]
    v_smalls = [v_b_ada, v_norm1_g, v_q_norm_g, v_k_norm_g, v_b_f, v_conv_b, v_conv_ln_g, v_conv_ln_b, v_beta_attn,
                v_beta_conv, v_norm2_g]
    widths = [a.shape[1] for a in smalls]
    padded = [-(-n // LANES) * LANES for n in widths]
    pack = lambda arrs, fill: jnp.concatenate(
        [jnp.pad(a, ((0, 0), (0, p - a.shape[1])), constant_values=fill) for a, p in zip(arrs, padded)], axis=1)
    outs_small = _reduce_adamw(g_small.reshape(1, 1, SMALL_OUT), pack(smalls, 0.0), pack(m_smalls, 0.0),
                               pack(v_smalls, 1.0), "adamw_small")
    offs = np.concatenate([[0], np.cumsum(padded)])

    def unpack(a):
        return [a[:, int(o):int(o) + n] for o, n in zip(offs[:-1], widths)]

    gs, ds, ms, vs = (unpack(a) for a in outs_small)

    loss = lax.psum(loss8[0, 0], ("x", "y", "c"))
    big = {"w_ada": (g_ada, d_ada, nm_ada, nv_ada), "w_in": (g_in_, d_in, nm_in, nv_in),
           "conv_w": (g_cw_, d_cw, nm_cw, nv_cw), "w_out": (g_out_, d_out, nm_out, nv_out),
           "w_ff1": (g_f1, d_f1, nm_f1, nv_f1), "w_ff2": (g_f2, d_f2, nm_f2, nv_f2)}
    small_names = ["b_ada", "norm1_g", "q_norm_g", "k_norm_g", "b_f", "conv_b", "conv_ln_g", "conv_ln_b", "beta_attn",
                   "beta_conv", "norm2_g"]
    order = ["w_ada", "b_ada", "norm1_g", "w_in", "q_norm_g", "k_norm_g", "b_f", "conv_w", "conv_b", "conv_ln_g",
             "conv_ln_b", "beta_attn", "beta_conv", "w_out", "norm2_g", "w_ff1", "w_ff2"]

    def leaf(name, which):
        if name in big:
            return big[name][which][None]
        return (gs, ds, ms, vs)[which][small_names.index(name)]

    return (loss, grad_x[None], *[leaf(n, 0) for n in order], *[leaf(n, 1) for n in order],
            *[leaf(n, 2) for n in order], *[leaf(n, 3) for n in order])
```

```python
import functools

import numpy as np
import jax
import jax.numpy as jnp
from jax import lax
from jax.experimental import pallas as pl
from jax.experimental.pallas import tpu as pltpu

F32, BF16 = jnp.float32, jnp.bfloat16
HI = lax.Precision.HIGHEST
D = 1024
AW = 512
CW = 512
NH = 8
DH = 64
KC = 31
DFF = 4096
NP = 2688
EPS = 1e-6
NEG = -1e30
LANES = 128
VMEM_LIMIT = 56 * 2**20
NT = (((1,), (1,)), ((), ()))
TN = (((0,), (0,)), ((), ()))
LR, B1, B2, AEPS, WD, STEP = 0.001, 0.9, 0.999, 1e-08, 0.01, 10
N_DEV = 8
SMALL_IN = 11904
SMALL_OUT = 11136


def _cp(sem=None, vmem=VMEM_LIMIT):
    kw = dict(vmem_limit_bytes=vmem)
    if sem is not None:
        kw["dimension_semantics"] = sem
    return pltpu.CompilerParams(**kw)


def _rows(tm, n):
    return pl.BlockSpec((tm, n), lambda i: (i, 0))


def _const(shape):
    nd = len(shape)
    return pl.BlockSpec(shape, lambda *_: (0,) * nd)


def _sds(shape, dt):
    return jax.ShapeDtypeStruct(shape, dt)


def _lane(shape):
    return lax.broadcasted_iota(jnp.int32, shape, len(shape) - 1)


def _sigmoid(x):
    return 1.0 / (1.0 + jnp.exp(-x))


def _exchange(items, name):
    kinds = [k for _, k in items]
    srcs = [s for s, _ in items]
    out_shapes = []
    for s, k in items:
        if k in ("all8", "chip8"):
            out_shapes.append(_sds((N_DEV,) + s.shape[1:], s.dtype))
        elif k == "bcast8":
            out_shapes.append(_sds((N_DEV,) + s.shape, s.dtype))
        elif k == "chip4":
            out_shapes.append(_sds((4,) + s.shape, s.dtype))
        elif k == "chip4p":
            out_shapes.append(_sds((4,) + s.shape[1:], s.dtype))
        else:
            out_shapes.append(_sds((2,) + s.shape, s.dtype))
    n = len(items)
    all_masks = {"chip4": [2, 4, 6], "chip4p": [2, 4, 6], "pair": [1]}
    masks = {k: all_masks.get(k, [1, 2, 3, 4, 5, 6, 7]) for k in set(kinds)}
    sem_index = {}
    for t, k in enumerate(kinds):
        for m in masks[k]:
            sem_index[(t, m)] = len(sem_index)
    n_sem = len(sem_index)

    def body(*refs):
        src_refs, dst_refs = refs[:n], refs[n:2 * n]
        send_sems, recv_sems, local_sems = refs[2 * n:]
        x, y, c = lax.axis_index("x"), lax.axis_index("y"), lax.axis_index("c")
        my_id = 4 * x + 2 * y + c
        my_chip = 2 * x + y

        def piece(t, dev_id, chip):
            k = kinds[t]
            if k == "all8":
                return src_refs[t].at[dev_id]
            if k in ("chip8", "chip4p"):
                return src_refs[t].at[chip]
            return src_refs[t]

        def slot(t):
            k = kinds[t]
            return dst_refs[t].at[my_chip if k in ("chip4", "chip4p") else c if k == "pair" else my_id]

        copies = []
        for t in range(n):
            lc = pltpu.make_async_copy(piece(t, my_id, my_chip), slot(t), local_sems.at[t])
            lc.start()
            copies.append(lc)
            for m in masks[kinds[t]]:
                px = 1 - x if m & 4 else x
                py = 1 - y if m & 2 else y
                pc = 1 - c if m & 1 else c
                s = sem_index[(t, m)]
                rc = pltpu.make_async_remote_copy(
                    src_ref=piece(t, 4 * px + 2 * py + pc, 2 * px + py), dst_ref=slot(t),
                    send_sem=send_sems.at[s], recv_sem=recv_sems.at[s],
                    device_id=(px, py, pc), device_id_type=pl.DeviceIdType.MESH)
                rc.start()
                copies.append(rc)
        for cp in copies:
            cp.wait()

    outs = pl.pallas_call(
        body, name=name, out_shape=tuple(out_shapes),
        in_specs=[pl.BlockSpec(memory_space=pl.ANY)] * n,
        out_specs=tuple(pl.BlockSpec(memory_space=pl.ANY) for _ in range(n)),
        scratch_shapes=[pltpu.SemaphoreType.DMA((n_sem,)), pltpu.SemaphoreType.DMA((n_sem,)),
                        pltpu.SemaphoreType.DMA((n,))],
    )(*srcs)
    return list(outs)


def _mod_shard(c_all, w_ada, b_shard):
    n = w_ada.shape[1]

    def body(c_ref, w_ref, b_ref, o_ref, sc_ref):
        cv = c_ref[...]
        sc = cv * _sigmoid(cv)
        sc_ref[...] = sc
        o_ref[...] = jnp.dot(sc, w_ref[...], precision=HI, preferred_element_type=F32) + b_ref[...]

    bn = 512
    return pl.pallas_call(
        body, name="mod_shard", out_shape=(_sds((N_DEV, n), F32), _sds((N_DEV, D), F32)), grid=(n // bn,),
        in_specs=[_const((N_DEV, D)), pl.BlockSpec((D, bn), lambda j: (0, j)), pl.BlockSpec((1, bn), lambda j: (0, j))],
        out_specs=(pl.BlockSpec((N_DEV, bn), lambda j: (0, j)), _const((N_DEV, D))),
        compiler_params=_cp(("arbitrary",)),
    )(c_all, w_ada, b_shard)


def _head_sum_mats():
    e = np.zeros((AW, LANES), np.float32)
    for h in range(NH):
        e[h * DH:(h + 1) * DH, h] = 1.0
    return jnp.asarray(e), jnp.asarray(e.T.copy())


def _fwd_in(x, mod8, n1g, w_in_p, e512, et512, qg512, kg512, bf128, tm):
    S = x.shape[0]

    def body(x_ref, mod_ref, n1g_ref, w_ref, e_ref, et_ref, qg_ref, kg_ref, bf_ref,
             h1_ref, qh_ref, kh_ref, v_ref, qn_ref, kn_ref, rq_ref, rk_ref, fgb_ref, alin_ref, agate_ref, u0_ref):
        xv = x_ref[...]
        r1 = lax.rsqrt(jnp.mean(xv * xv, axis=-1, keepdims=True) + EPS)
        h = (xv * r1) * (n1g_ref[...] * (1.0 + mod_ref[1:2, :])) + mod_ref[0:1, :]
        hb = h.astype(BF16)
        h1_ref[...] = hb

        def seg(a, b):
            return jnp.dot(hb, w_ref[:, a:b], preferred_element_type=F32)

        def headnorm(t, g_ref, scale, n_ref, r_ref, o_ref):
            ss = jnp.dot(t * t, e_ref[...], precision=HI, preferred_element_type=F32)
            r = lax.rsqrt(ss * (1.0 / DH) + EPS)
            tn = t * jnp.dot(r, et_ref[...], precision=HI, preferred_element_type=F32)
            n_ref[...] = tn.astype(BF16)
            r_ref[...] = r
            o_ref[...] = (tn * (g_ref[...] * scale)).astype(BF16)

        headnorm(seg(0, 512), qg_ref, DH ** -0.5, qn_ref, rq_ref, qh_ref)
        headnorm(seg(512, 1024), kg_ref, 1.0, kn_ref, rk_ref, kh_ref)
        v_ref[...] = seg(1024, 1536).astype(BF16)
        alin = seg(1536, 2048)
        agate = seg(2048, 2560)
        alin_ref[...] = alin.astype(BF16)
        agate_ref[...] = agate.astype(BF16)
        u0_ref[...] = alin * _sigmoid(agate)
        fgb_ref[...] = seg(2560, NP) + bf_ref[...]

    bf = lambda: _sds((S, AW), BF16)
    return pl.pallas_call(
        body, name="fwd_in", grid=(S // tm,),
        out_shape=(_sds((S, D), BF16), bf(), bf(), bf(), bf(), bf(), _sds((S, LANES), F32), _sds((S, LANES), F32),
                   _sds((S, LANES), F32), bf(), bf(), _sds((S, CW), F32)),
        in_specs=[_rows(tm, D), _const((8, D)), _const((1, D)), _const((D, NP)), _const((AW, LANES)), _const((LANES, AW)),
                  _const((1, AW)), _const((1, AW)), _const((1, LANES))],
        out_specs=(_rows(tm, D), _rows(tm, AW), _rows(tm, AW), _rows(tm, AW), _rows(tm, AW), _rows(tm, AW),
                   _rows(tm, LANES), _rows(tm, LANES), _rows(tm, LANES), _rows(tm, AW), _rows(tm, AW), _rows(tm, CW)),
        compiler_params=_cp(("parallel",)),
    )(x, mod8, n1g, w_in_p, e512, et512, qg512, kg512, bf128)


def _split3(f):
    f1 = f.astype(BF16).astype(F32)
    f2 = (f - f1).astype(BF16).astype(F32)
    return f1, f2, f - f1 - f2


def _fwd_decay(fgb, qh, kh, tm):
    S = fgb.shape[0]

    def body(fgb_ref, qh_ref, kh_ref, qa_ref, ka_ref, f_ref, carry_ref):
        @pl.when(pl.program_id(0) == 0)
        def _():
            carry_ref[...] = jnp.zeros_like(carry_ref)

        fb = fgb_ref[...]
        lf = jnp.minimum(fb, 0.0) - jnp.log1p(jnp.exp(-jnp.abs(fb)))
        tri = (lax.broadcasted_iota(jnp.int32, (tm, tm), 0) >= lax.broadcasted_iota(jnp.int32, (tm, tm), 1)).astype(F32)
        cs = jnp.dot(tri, lf, precision=HI, preferred_element_type=F32) + carry_ref[0:1, :]
        f_ref[...] = cs
        carry_ref[...] = jnp.broadcast_to(cs[tm - 1:tm, :], carry_ref.shape)
        lane = _lane((tm, LANES))
        ones = jnp.where((lane >= 67) & (lane < 70), 1.0, 0.0)
        ones_k = jnp.where((lane >= 64) & (lane < 67), 1.0, 0.0)
        for p in range(NH // 2):
            qp = qh_ref[:, p * LANES:(p + 1) * LANES].astype(F32)
            kp = kh_ref[:, p * LANES:(p + 1) * LANES].astype(F32)
            for hh in range(2):
                h = 2 * p + hh
                f1, f2, f3 = _split3(cs[:, h:h + 1])
                qb = qp if hh == 0 else pltpu.roll(qp, 64, 1)
                kb = kp if hh == 0 else pltpu.roll(kp, 64, 1)
                augq = jnp.where(lane == 64, f1, jnp.where(lane == 65, f2, jnp.where(lane == 66, f3, ones)))
                augk = jnp.where(lane == 67, -f1, jnp.where(lane == 68, -f2, jnp.where(lane == 69, -f3, ones_k)))
                qa_ref[h] = jnp.where(lane < DH, qb, augq).astype(BF16)
                ka_ref[h] = jnp.where(lane < DH, kb, augk).astype(BF16)

    hm = pl.BlockSpec((NH, tm, LANES), lambda i: (0, i, 0))
    return pl.pallas_call(
        body, name="fwd_decay", grid=(S // tm,),
        out_shape=(_sds((NH, S, LANES), BF16), _sds((NH, S, LANES), BF16), _sds((S, LANES), F32)),
        in_specs=[_rows(tm, LANES), _rows(tm, AW), _rows(tm, AW)], out_specs=(hm, hm, _rows(tm, LANES)),
        scratch_shapes=[pltpu.VMEM((8, LANES), F32)], compiler_params=_cp(("arbitrary",)),
    )(fgb, qh, kh)


def _causal(t):
    return lax.broadcasted_iota(jnp.int32, (t, t), 0) >= lax.broadcasted_iota(jnp.int32, (t, t), 1)


def _causal_t(t):
    return lax.broadcasted_iota(jnp.int32, (t, t), 0) <= lax.broadcasted_iota(jnp.int32, (t, t), 1)


SKIP = -106.0


def _block_loops(first, step, needed, run):
    def both(j):
        return jnp.logical_and(needed(0, j), needed(1, j))

    def walk(heads):
        def go(j):
            run(j, heads)
            return j + step
        return go

    j = lax.while_loop(both, walk((0, 1)), first)
    lax.while_loop(functools.partial(needed, 0), walk((0,)), j)
    lax.while_loop(functools.partial(needed, 1), walk((1,)), j)


def _skip_tables(f, qg, kg, tq):
    fs = f[0::tq, :NH].T
    fe = f[tq - 1::tq, :NH].T
    bound = 2.0 * 1.03 * DH ** 0.5 * jnp.max(jnp.abs(qg)) * jnp.max(jnp.abs(kg))
    return fs, fe, bound.reshape(1, 1)


SMEM_SPEC = pl.BlockSpec(memory_space=pltpu.SMEM)


def _attn_fwd(qa, ka, v, fs, fe, bd, tq):
    S = qa.shape[1]
    nq = S // tq
    rep = tq // LANES

    def body(fs_ref, fe_ref, bd_ref, qa_ref, ka_ref, v_ref, o_ref, lsec_ref, lser_ref, m_ref, l_ref, acc_ref):
        pr, i = pl.program_id(0), pl.program_id(1)
        sel_a = _lane((tq, LANES)) < DH
        m_ref[...] = jnp.full(m_ref.shape, NEG, F32)
        l_ref[...] = jnp.zeros_like(l_ref)
        acc_ref[...] = jnp.zeros_like(acc_ref)

        def kv_step(j, heads, masked=False):
            start = pl.multiple_of(j * tq, tq)
            vb = v_ref[pl.ds(start, tq), :]
            for hh in heads:
                s = lax.dot_general(qa_ref[hh], ka_ref[hh, pl.ds(start, tq), :], NT, preferred_element_type=F32)
                if masked:
                    s = jnp.where(_causal(tq), s, NEG)
                m_prev = m_ref[hh]
                m_new = jnp.maximum(m_prev, jnp.max(s, axis=1, keepdims=True))
                alpha = jnp.exp(m_prev - m_new)
                p = jnp.exp(s - jnp.tile(m_new, (1, rep)))
                l_ref[hh] = alpha * l_ref[hh] + jnp.sum(p, axis=1, keepdims=True)
                m_ref[hh] = m_new
                acc_ref[hh] = alpha * acc_ref[hh] + jnp.dot(p.astype(BF16), vb, preferred_element_type=F32)

        def needed(hh, j):
            top = fs_ref[2 * pr + hh, i] + bd_ref[0, 0]
            return jnp.logical_and(j >= 0, top - fe_ref[2 * pr + hh, jnp.maximum(j, 0)] >= SKIP)

        kv_step(i, (0, 1), masked=True)
        _block_loops(i - 1, -1, needed, kv_step)
        o_ref[...] = jnp.where(sel_a, acc_ref[0] / l_ref[0], acc_ref[1] / l_ref[1])
        lse_a = m_ref[0] + jnp.log(l_ref[0])
        lse_b = m_ref[1] + jnp.log(l_ref[1])
        lsec_ref[0] = lse_a
        lsec_ref[1] = lse_b
        row = lax.broadcasted_iota(jnp.int32, (8, tq), 0)
        lser_ref[0, 0] = jnp.where(row == 0, lse_a.T[0:8, :], lse_b.T[0:8, :])

    return pl.pallas_call(
        body, name="attn_fwd", grid=(NH // 2, nq),
        out_shape=(_sds((S, AW), F32), _sds((NH, S, LANES), F32), _sds((NH // 2, nq, 8, tq), F32)),
        in_specs=[SMEM_SPEC, SMEM_SPEC, SMEM_SPEC,
                  pl.BlockSpec((2, tq, LANES), lambda p, i: (p, i, 0)),
                  pl.BlockSpec((2, S, LANES), lambda p, i: (p, 0, 0)),
                  pl.BlockSpec((S, LANES), lambda p, i: (0, p))],
        out_specs=(pl.BlockSpec((tq, LANES), lambda p, i: (i, p)),
                   pl.BlockSpec((2, tq, LANES), lambda p, i: (p, i, 0)),
                   pl.BlockSpec((1, 1, 8, tq), lambda p, i: (p, i, 0, 0))),
        scratch_shapes=[pltpu.VMEM((2, tq, LANES), F32), pltpu.VMEM((2, tq, LANES), F32),
                        pltpu.VMEM((2, tq, LANES), F32)],
        compiler_params=_cp(("parallel", "parallel")),
    )(fs, fe, bd, qa, ka, v)


HALO = 32
CHUNK_ROWS = 64


def _halo_prev(tm):
    return pl.BlockSpec((HALO, CW), lambda i: (jnp.maximum(i * (tm // HALO) - 1, 0), 0))


def _fwd_conv(u0, w32, cb, lng, lnb, beta_c, tm):
    S = u0.shape[0]

    def body(cur_ref, prev_ref, w_ref, cb_ref, lng_ref, lnb_ref, beta_ref, u1_ref, mc_ref, ext_ref):
        i = pl.program_id(0)
        ext_ref[0:HALO, :] = jnp.where(i == 0, 0.0, prev_ref[...])
        ext_ref[HALO:, :] = cur_ref[...]
        for r0 in range(0, tm, CHUNK_ROWS):
            acc = jnp.zeros((CHUNK_ROWS, CW), F32) + cb_ref[...]
            for j in range(KC):
                acc = acc + w_ref[j:j + 1, :] * ext_ref[r0 + 2 + j:r0 + 2 + j + CHUNK_ROWS, :]
            u1_ref[r0:r0 + CHUNK_ROWS, :] = acc
        u1 = u1_ref[...]
        mu = jnp.mean(u1, axis=-1, keepdims=True)
        d = u1 - mu
        rstd = lax.rsqrt(jnp.mean(d * d, axis=-1, keepdims=True) + EPS)
        u2 = d * rstd * lng_ref[...] + lnb_ref[...]
        u3 = u2 * _sigmoid(u2)
        rc = lax.rsqrt(jnp.mean(u3 * u3, axis=-1, keepdims=True) + EPS)
        mc_ref[...] = (u3 * rc * beta_ref[...]).astype(BF16)

    return pl.pallas_call(
        body, name="fwd_conv", grid=(S // tm,),
        out_shape=(_sds((S, CW), F32), _sds((S, CW), BF16)),
        in_specs=[_rows(tm, CW), _halo_prev(tm), _const((HALO, CW)), _const((1, CW)), _const((1, CW)), _const((1, CW)),
                  _const((1, CW))],
        out_specs=(_rows(tm, CW), _rows(tm, CW)),
        scratch_shapes=[pltpu.VMEM((tm + HALO, CW), F32)], compiler_params=_cp(("parallel",)),
    )(u0, u0, w32, cb, lng, lnb, beta_c)


def _fwd_out(o_attn, mc, x, mod8, n2g, beta_a, w_out, tm):
    S = x.shape[0]

    def body(o_ref, mc_ref, x_ref, mod_ref, n2g_ref, beta_ref, w_ref, mg_ref, ob_ref, x2_ref, h2_ref):
        ov = o_ref[...]
        ra = lax.rsqrt(jnp.mean(ov * ov, axis=-1, keepdims=True) + EPS)
        ma = (ov * ra * beta_ref[...]).astype(BF16)
        mcv = mc_ref[...]
        mg_ref[:, 0:AW] = ma
        mg_ref[:, AW:D] = mcv
        o = (jnp.dot(ma, w_ref[0:AW, :], preferred_element_type=F32)
             + jnp.dot(mcv, w_ref[AW:D, :], preferred_element_type=F32))
        ob_ref[...] = o.astype(BF16)
        x2 = x_ref[...] + mod_ref[2:3, :] * o
        x2_ref[...] = x2
        r2 = lax.rsqrt(jnp.mean(x2 * x2, axis=-1, keepdims=True) + EPS)
        h2_ref[...] = ((x2 * r2) * (n2g_ref[...] * (1.0 + mod_ref[4:5, :])) + mod_ref[3:4, :]).astype(BF16)

    return pl.pallas_call(
        body, name="fwd_out", grid=(S // tm,),
        out_shape=(_sds((S, D), BF16), _sds((S, D), BF16), _sds((S, D), F32), _sds((S, D), BF16)),
        in_specs=[_rows(tm, AW), _rows(tm, CW), _rows(tm, D), _const((8, D)), _const((1, D)), _const((1, AW)),
                  _const((D, D))],
        out_specs=(_rows(tm, D), _rows(tm, D), _rows(tm, D), _rows(tm, D)),
        compiler_params=_cp(("parallel",)),
    )(o_attn, mc, x, mod8, n2g, beta_a, w_out)


def _fwd_ffn(h2, w1, w2, x2, tgt, mod8, tm):
    S = h2.shape[0]
    nk = w1.shape[0]
    bf = w1.shape[2]

    def body(h2_ref, w1_ref, w2_ref, x2_ref, tgt_ref, mod_ref, r_ref, dy_ref, loss_ref, dg2_ref, acc_ref):
        i, k = pl.program_id(0), pl.program_id(1)

        @pl.when((i == 0) & (k == 0))
        def _():
            loss_ref[...] = jnp.zeros_like(loss_ref)
            dg2_ref[...] = jnp.zeros_like(dg2_ref)

        r = jnp.maximum(jnp.dot(h2_ref[...], w1_ref[0], preferred_element_type=F32), 0.0)
        r_ref[...] = r.astype(BF16)
        part = jnp.dot((r * r).astype(BF16), w2_ref[0], preferred_element_type=F32)

        @pl.when(k == 0)
        def _():
            acc_ref[...] = part

        @pl.when(k > 0)
        def _():
            acc_ref[...] += part

        @pl.when(k == nk - 1)
        def _():
            f2 = acc_ref[...]
            e = x2_ref[...] + mod_ref[5:6, :] * f2 - tgt_ref[...]
            dy = e * (1.0 / D)
            dy_ref[...] = dy
            loss_ref[...] += 0.5 * jnp.sum(jnp.sum(e * dy, axis=1, keepdims=True), axis=0, keepdims=True)
            dg2_ref[...] += jnp.sum((dy * f2).reshape(tm // 8, 8, D), axis=0)

    return pl.pallas_call(
        body, name="fwd_ffn", grid=(S // tm, nk),
        out_shape=(_sds((S, DFF), BF16), _sds((S, D), F32), _sds((8, LANES), F32), _sds((8, D), F32)),
        in_specs=[pl.BlockSpec((tm, D), lambda i, k: (i, 0)), pl.BlockSpec((1, D, bf), lambda i, k: (k, 0, 0)),
                  pl.BlockSpec((1, bf, D), lambda i, k: (k, 0, 0)), pl.BlockSpec((tm, D), lambda i, k: (i, 0)),
                  pl.BlockSpec((tm, D), lambda i, k: (i, 0)), pl.BlockSpec((8, D), lambda i, k: (0, 0))],
        out_specs=(pl.BlockSpec((tm, bf), lambda i, k: (i, k)), pl.BlockSpec((tm, D), lambda i, k: (i, 0)),
                   pl.BlockSpec((8, LANES), lambda i, k: (0, 0)), pl.BlockSpec((8, D), lambda i, k: (0, 0))),
        scratch_shapes=[pltpu.VMEM((tm, D), F32)], compiler_params=_cp(("arbitrary", "arbitrary")),
    )(h2, w1, w2, x2, tgt, mod8)


def _bwd_ffn(dy, mod8, r, w1, w2, tm):
    S = dy.shape[0]
    nk = w1.shape[0]
    bf = w1.shape[2]

    def body(dy_ref, mod_ref, r_ref, w1_ref, w2_ref, df2_ref, df1_ref, dh2_ref):
        k = pl.program_id(1)
        df2 = (dy_ref[...] * mod_ref[5:6, :]).astype(BF16)

        @pl.when(k == 0)
        def _():
            df2_ref[...] = df2

        da = lax.dot_general(df2, w2_ref[0], NT, preferred_element_type=F32)
        df1 = (da * (2.0 * r_ref[...].astype(F32))).astype(BF16)
        df1_ref[...] = df1
        part = lax.dot_general(df1, w1_ref[0], NT, preferred_element_type=F32)

        @pl.when(k == 0)
        def _():
            dh2_ref[...] = part

        @pl.when(k > 0)
        def _():
            dh2_ref[...] += part

    return pl.pallas_call(
        body, name="bwd_ffn", grid=(S // tm, nk),
        out_shape=(_sds((S, D), BF16), _sds((S, DFF), BF16), _sds((S, D), F32)),
        in_specs=[pl.BlockSpec((tm, D), lambda i, k: (i, 0)), pl.BlockSpec((8, D), lambda i, k: (0, 0)),
                  pl.BlockSpec((tm, bf), lambda i, k: (i, k)), pl.BlockSpec((1, D, bf), lambda i, k: (k, 0, 0)),
                  pl.BlockSpec((1, bf, D), lambda i, k: (k, 0, 0))],
        out_specs=(pl.BlockSpec((tm, D), lambda i, k: (i, 0)), pl.BlockSpec((tm, bf), lambda i, k: (i, k)),
                   pl.BlockSpec((tm, D), lambda i, k: (i, 0))),
        compiler_params=_cp(("parallel", "arbitrary")),
    )(dy, mod8, r, w1, w2)


def _wgrad(a, b, name, square_a=False, tk=512, bm=1024, bn=1024):
    S, M = a.shape
    N = b.shape[1]
    bm, bn, tk = min(bm, M), min(bn, N), min(tk, S)

    def body(a_ref, b_ref, o_ref):
        av = a_ref[...]
        if square_a:
            af = av.astype(F32)
            av = (af * af).astype(BF16)
        part = lax.dot_general(av, b_ref[...], TN, preferred_element_type=F32)

        @pl.when(pl.program_id(2) == 0)
        def _():
            o_ref[...] = part

        @pl.when(pl.program_id(2) > 0)
        def _():
            o_ref[...] += part

    return pl.pallas_call(
        body, name=name, grid=(M // bm, N // bn, S // tk), out_shape=_sds((M, N), F32),
        in_specs=[pl.BlockSpec((tk, bm), lambda mi, ni, k: (k, mi)), pl.BlockSpec((tk, bn), lambda mi, ni, k: (k, ni))],
        out_specs=pl.BlockSpec((bm, bn), lambda mi, ni, k: (mi, ni)),
        compiler_params=_cp(("parallel", "parallel", "arbitrary")),
    )(a, b)


def _colsum8(t):
    return jnp.sum(t.reshape(t.shape[0] // 8, 8, t.shape[1]), axis=0)


def _bwd_mid(dh2, dy, x2, ob, o_attn, u1, mod8, n2g, beta_a, beta_c, lng, lnb, w_out, tm):
    S = dy.shape[0]

    def body(dh2_ref, dy_ref, x2_ref, ob_ref, oa_ref, u1_ref, mod_ref, n2g_ref, ba_ref, bc_ref, lng_ref, lnb_ref, w_ref,
             dx2_ref, do_ref, doa_ref, du1_ref, acc_d_ref, acc_h_ref):
        @pl.when(pl.program_id(0) == 0)
        def _():
            acc_d_ref[...] = jnp.zeros_like(acc_d_ref)
            acc_h_ref[...] = jnp.zeros_like(acc_h_ref)

        x2 = x2_ref[...]
        dh2 = dh2_ref[...]
        r2 = lax.rsqrt(jnp.mean(x2 * x2, axis=-1, keepdims=True) + EPS)
        xn2 = x2 * r2
        gain = n2g_ref[...] * (1.0 + mod_ref[4:5, :])
        dxn = dh2 * gain
        dx2 = dy_ref[...] + r2 * (dxn - xn2 * jnp.mean(dxn * xn2, axis=-1, keepdims=True))
        dx2_ref[...] = dx2
        t = dh2 * xn2
        acc_d_ref[0] += _colsum8(dh2)
        acc_d_ref[1] += _colsum8(t * n2g_ref[...])
        acc_d_ref[2] += _colsum8(t * (1.0 + mod_ref[4:5, :]))
        acc_d_ref[3] += _colsum8(dx2 * ob_ref[...].astype(F32))
        do = (dx2 * mod_ref[2:3, :]).astype(BF16)
        do_ref[...] = do
        dma = lax.dot_general(do, w_ref[0:AW, :], NT, preferred_element_type=F32)
        dmc = lax.dot_general(do, w_ref[AW:D, :], NT, preferred_element_type=F32)
        ov = oa_ref[...]
        ra = lax.rsqrt(jnp.mean(ov * ov, axis=-1, keepdims=True) + EPS)
        on = ov * ra
        acc_h_ref[0] += _colsum8(dma * on)
        don = dma * ba_ref[...]
        doa_ref[...] = (ra * (don - on * jnp.mean(don * on, axis=-1, keepdims=True))).astype(BF16)
        u1 = u1_ref[...]
        mu = jnp.mean(u1, axis=-1, keepdims=True)
        d = u1 - mu
        rstd = lax.rsqrt(jnp.mean(d * d, axis=-1, keepdims=True) + EPS)
        uh = d * rstd
        u2 = uh * lng_ref[...] + lnb_ref[...]
        sg = _sigmoid(u2)
        u3 = u2 * sg
        rc = lax.rsqrt(jnp.mean(u3 * u3, axis=-1, keepdims=True) + EPS)
        u3n = u3 * rc
        acc_h_ref[1] += _colsum8(dmc * u3n)
        du3n = dmc * bc_ref[...]
        du3 = rc * (du3n - u3n * jnp.mean(du3n * u3n, axis=-1, keepdims=True))
        du2 = du3 * (sg * (1.0 + u2 * (1.0 - sg)))
        acc_h_ref[2] += _colsum8(du2 * uh)
        acc_h_ref[3] += _colsum8(du2)
        duh = du2 * lng_ref[...]
        du1_ref[...] = rstd * (duh - jnp.mean(duh, axis=-1, keepdims=True)
                               - uh * jnp.mean(duh * uh, axis=-1, keepdims=True))

    return pl.pallas_call(
        body, name="bwd_mid", grid=(S // tm,),
        out_shape=(_sds((S, D), F32), _sds((S, D), BF16), _sds((S, AW), BF16), _sds((S, CW), F32),
                   _sds((4, 8, D), F32), _sds((4, 8, AW), F32)),
        in_specs=[_rows(tm, D), _rows(tm, D), _rows(tm, D), _rows(tm, D), _rows(tm, AW), _rows(tm, CW), _const((8, D)),
                  _const((1, D)), _const((1, AW)), _const((1, CW)), _const((1, CW)), _const((1, CW)), _const((D, D))],
        out_specs=(_rows(tm, D), _rows(tm, D), _rows(tm, AW), _rows(tm, CW), _const((4, 8, D)), _const((4, 8, AW))),
        compiler_params=_cp(("arbitrary",)),
    )(dh2, dy, x2, ob, o_attn, u1, mod8, n2g, beta_a, beta_c, lng, lnb, w_out)


def _attn_bwd_dq(qa, ka, v, do, o_attn, lsec, fs, fe, bd, tq):
    S = qa.shape[1]
    nq = S // tq
    rep = tq // LANES

    def body(fs_ref, fe_ref, bd_ref, qa_ref, ka_ref, v_ref, do_ref, o_ref, lse_ref, dqa_ref, dr_ref, acc_ref):
        pr, i = pl.program_id(0), pl.program_id(1)
        sel_a = _lane((tq, LANES)) < DH
        dov = do_ref[...]
        prod = dov.astype(F32) * o_ref[...]
        zero = jnp.zeros_like(prod)
        deltas = [jnp.sum(jnp.where(sel_a, prod, zero), axis=1, keepdims=True),
                  jnp.sum(jnp.where(sel_a, zero, prod), axis=1, keepdims=True)]
        zb = jnp.zeros_like(dov)
        dos = [jnp.where(sel_a, dov, zb), jnp.where(sel_a, zb, dov)]
        acc_ref[...] = jnp.zeros_like(acc_ref)

        def kv_step(j, heads, masked=False):
            start = pl.multiple_of(j * tq, tq)
            vb = v_ref[pl.ds(start, tq), :]
            for hh in heads:
                kb = ka_ref[hh, pl.ds(start, tq), :]
                s = lax.dot_general(qa_ref[hh], kb, NT, preferred_element_type=F32)
                p = jnp.exp(s - jnp.tile(lse_ref[hh], (1, rep)))
                if masked:
                    p = jnp.where(_causal(tq), p, 0.0)
                dp = lax.dot_general(dos[hh], vb, NT, preferred_element_type=F32)
                ds = p * (dp - deltas[hh])
                acc_ref[hh] += jnp.dot(ds.astype(BF16), kb, preferred_element_type=F32)

        def needed(hh, j):
            top = fs_ref[2 * pr + hh, i] + bd_ref[0, 0]
            return jnp.logical_and(j >= 0, top - fe_ref[2 * pr + hh, jnp.maximum(j, 0)] >= SKIP)

        kv_step(i, (0, 1), masked=True)
        _block_loops(i - 1, -1, needed, kv_step)
        dqa_ref[...] = acc_ref[...]
        row = lax.broadcasted_iota(jnp.int32, (8, tq), 0)
        da = jnp.broadcast_to(deltas[0], (tq, LANES)).T[0:8, :]
        db = jnp.broadcast_to(deltas[1], (tq, LANES)).T[0:8, :]
        dr_ref[0, 0] = jnp.where(row == 0, da, db)

    return pl.pallas_call(
        body, name="attn_bwd_dq", grid=(NH // 2, nq),
        out_shape=(_sds((NH, S, LANES), F32), _sds((NH // 2, nq, 8, tq), F32)),
        in_specs=[SMEM_SPEC, SMEM_SPEC, SMEM_SPEC,
                  pl.BlockSpec((2, tq, LANES), lambda p, i: (p, i, 0)),
                  pl.BlockSpec((2, S, LANES), lambda p, i: (p, 0, 0)),
                  pl.BlockSpec((S, LANES), lambda p, i: (0, p)),
                  pl.BlockSpec((tq, LANES), lambda p, i: (i, p)),
                  pl.BlockSpec((tq, LANES), lambda p, i: (i, p)),
                  pl.BlockSpec((2, tq, LANES), lambda p, i: (p, i, 0))],
        out_specs=(pl.BlockSpec((2, tq, LANES), lambda p, i: (p, i, 0)),
                   pl.BlockSpec((1, 1, 8, tq), lambda p, i: (p, i, 0, 0))),
        scratch_shapes=[pltpu.VMEM((2, tq, LANES), F32)],
        compiler_params=_cp(("parallel", "parallel")),
    )(fs, fe, bd, qa, ka, v, do, o_attn, lsec)


def _attn_bwd_dkv(qa, ka, v, do, lser, dr, fs, fe, bd, tq):
    S = qa.shape[1]
    nq = S // tq

    def body(fs_ref, fe_ref, bd_ref, ka_ref, v_ref, qa_ref, do_ref, lse_ref, dr_ref, dka_ref, dv_ref, acck_ref,
             accv_ref):
        pr, j = pl.program_id(0), pl.program_id(1)
        sel_a = _lane((tq, LANES)) < DH
        vv = v_ref[...]
        zb = jnp.zeros_like(vv)
        vs = [jnp.where(sel_a, vv, zb), jnp.where(sel_a, zb, vv)]
        acck_ref[...] = jnp.zeros_like(acck_ref)
        accv_ref[...] = jnp.zeros_like(accv_ref)

        def q_step(i, heads, masked=False):
            start = pl.multiple_of(i * tq, tq)
            dob = do_ref[pl.ds(start, tq), :]
            lse8 = lse_ref[0, i]
            dr8 = dr_ref[0, i]
            for hh in heads:
                qb = qa_ref[hh, pl.ds(start, tq), :]
                st = lax.dot_general(ka_ref[hh], qb, NT, preferred_element_type=F32)
                pt = jnp.exp(st - lse8[hh:hh + 1, :])
                if masked:
                    pt = jnp.where(_causal_t(tq), pt, 0.0)
                accv_ref[hh] += jnp.dot(pt.astype(BF16), dob, preferred_element_type=F32)
                dpt = lax.dot_general(vs[hh], dob, NT, preferred_element_type=F32)
                dst = pt * (dpt - dr8[hh:hh + 1, :])
                acck_ref[hh] += jnp.dot(dst.astype(BF16), qb, preferred_element_type=F32)

        def needed(hh, i):
            top = fs_ref[2 * pr + hh, jnp.minimum(i, nq - 1)] + bd_ref[0, 0]
            return jnp.logical_and(i < nq, top - fe_ref[2 * pr + hh, j] >= SKIP)

        q_step(j, (0, 1), masked=True)
        _block_loops(j + 1, 1, needed, q_step)
        dka_ref[...] = acck_ref[...]
        dv_ref[...] = jnp.where(sel_a, accv_ref[0], accv_ref[1]).astype(BF16)

    return pl.pallas_call(
        body, name="attn_bwd_dkv", grid=(NH // 2, nq),
        out_shape=(_sds((NH, S, LANES), F32), _sds((S, AW), BF16)),
        in_specs=[SMEM_SPEC, SMEM_SPEC, SMEM_SPEC,
                  pl.BlockSpec((2, tq, LANES), lambda p, j: (p, j, 0)),
                  pl.BlockSpec((tq, LANES), lambda p, j: (j, p)),
                  pl.BlockSpec((2, S, LANES), lambda p, j: (p, 0, 0)),
                  pl.BlockSpec((S, LANES), lambda p, j: (0, p)),
                  pl.BlockSpec((1, nq, 8, tq), lambda p, j: (p, 0, 0, 0)),
                  pl.BlockSpec((1, nq, 8, tq), lambda p, j: (p, 0, 0, 0))],
        out_specs=(pl.BlockSpec((2, tq, LANES), lambda p, j: (p, j, 0)),
                   pl.BlockSpec((tq, LANES), lambda p, j: (j, p))),
        scratch_shapes=[pltpu.VMEM((2, tq, LANES), F32), pltpu.VMEM((2, tq, LANES), F32)],
        compiler_params=_cp(("parallel", "parallel")),
    )(fs, fe, bd, ka, v, qa, do, lser, dr)


def _bwd_conv(du1, u0, alin, agate, w32, tm):
    S = du1.shape[0]
    nt = S // tm

    def body(du_ref, dun_ref, u0_ref, u0p_ref, alin_ref, agate_ref, w_ref,
             dalin_ref, dagate_ref, dw_ref, db_ref, extd_ref, extu_ref, du0_ref):
        i = pl.program_id(0)

        @pl.when(i == 0)
        def _():
            dw_ref[...] = jnp.zeros_like(dw_ref)
            db_ref[...] = jnp.zeros_like(db_ref)

        extd_ref[0:tm, :] = du_ref[...]
        extd_ref[tm:, :] = jnp.where(i == nt - 1, 0.0, dun_ref[...])
        extu_ref[0:HALO, :] = jnp.where(i == 0, 0.0, u0p_ref[...])
        extu_ref[HALO:, :] = u0_ref[...]
        db_ref[...] += _colsum8(du_ref[...])
        for r0 in range(0, tm, CHUNK_ROWS):
            duc = du_ref[r0:r0 + CHUNK_ROWS, :]
            acc = jnp.zeros((CHUNK_ROWS, CW), F32)
            for j in range(KC):
                acc = acc + w_ref[j:j + 1, :] * extd_ref[r0 + 30 - j:r0 + 30 - j + CHUNK_ROWS, :]
                dw_ref[j] += _colsum8(duc * extu_ref[r0 + 2 + j:r0 + 2 + j + CHUNK_ROWS, :])
            du0_ref[r0:r0 + CHUNK_ROWS, :] = acc
        du0 = du0_ref[...]
        al = alin_ref[...].astype(F32)
        sg = _sigmoid(agate_ref[...].astype(F32))
        dalin_ref[...] = (du0 * sg).astype(BF16)
        dagate_ref[...] = (du0 * al * sg * (1.0 - sg)).astype(BF16)

    nxt = pl.BlockSpec((HALO, CW), lambda i: (jnp.minimum((i + 1) * (tm // HALO), S // HALO - 1), 0))
    return pl.pallas_call(
        body, name="bwd_conv", grid=(nt,),
        out_shape=(_sds((S, CW), BF16), _sds((S, CW), BF16), _sds((HALO, 8, CW), F32), _sds((8, CW), F32)),
        in_specs=[_rows(tm, CW), nxt, _rows(tm, CW), _halo_prev(tm), _rows(tm, CW), _rows(tm, CW), _const((HALO, CW))],
        out_specs=(_rows(tm, CW), _rows(tm, CW), _const((HALO, 8, CW)), _const((8, CW))),
        scratch_shapes=[pltpu.VMEM((tm + HALO, CW), F32), pltpu.VMEM((tm + HALO, CW), F32), pltpu.VMEM((tm, CW), F32)],
        compiler_params=_cp(("arbitrary",)),
    )(du1, du1, u0, u0, alin, agate, w32)


def _bwd_qk(dqa, dka, qn, kn, rq, rk, fgb, qg512, kg512, e512, et512, tm):
    S = qn.shape[0]
    nt = S // tm

    def body(dqa_ref, dka_ref, qn_ref, kn_ref, rq_ref, rk_ref, fgb_ref, qg_ref, kg_ref, e_ref, et_ref,
             dq_ref, dk_ref, dfg_ref, accg_ref, accb_ref, carry_ref):
        @pl.when(pl.program_id(0) == 0)
        def _():
            carry_ref[...] = jnp.zeros_like(carry_ref)
            accg_ref[...] = jnp.zeros_like(accg_ref)
            accb_ref[...] = jnp.zeros_like(accb_ref)

        lane = _lane((tm, LANES))
        sel_a = lane < DH
        df = jnp.zeros((tm, LANES), F32)
        for h in range(NH):
            col = dqa_ref[h][:, 64:65] - dka_ref[h][:, 67:68]
            df = jnp.where(lane == h, col, df)
        tri = (lax.broadcasted_iota(jnp.int32, (tm, tm), 0) <= lax.broadcasted_iota(jnp.int32, (tm, tm), 1)).astype(F32)
        dlf = jnp.dot(tri, df, precision=HI, preferred_element_type=F32) + carry_ref[0:1, :]
        carry_ref[...] = jnp.broadcast_to(dlf[0:1, :], carry_ref.shape)
        dfg = jnp.where(lane < NH, dlf * _sigmoid(-fgb_ref[...]), 0.0)
        dfg_ref[...] = dfg.astype(BF16)
        accb_ref[...] += _colsum8(dfg)

        def norm_bwd(src_ref, n_ref, r_ref, g_ref, scale, slot):
            pairs = []
            for p in range(NH // 2):
                b = pltpu.roll(src_ref[2 * p + 1], 64, 1)
                pairs.append(jnp.where(sel_a, src_ref[2 * p], b))
            dh = jnp.concatenate(pairs, axis=1) * scale
            tn = n_ref[...].astype(F32)
            accg_ref[slot] += _colsum8(dh * tn)
            dn = dh * g_ref[...]
            mean = jnp.dot(dn * tn, e_ref[...], precision=HI, preferred_element_type=F32) * (1.0 / DH)
            corr = jnp.dot(mean, et_ref[...], precision=HI, preferred_element_type=F32)
            rf = jnp.dot(r_ref[...], et_ref[...], precision=HI, preferred_element_type=F32)
            return (rf * (dn - tn * corr)).astype(BF16)

        dq_ref[...] = norm_bwd(dqa_ref, qn_ref, rq_ref, qg_ref, DH ** -0.5, 0)
        dk_ref[...] = norm_bwd(dka_ref, kn_ref, rk_ref, kg_ref, 1.0, 1)

    rev = lambda n: pl.BlockSpec((tm, n), lambda i: (nt - 1 - i, 0))
    hm = pl.BlockSpec((NH, tm, LANES), lambda i: (0, nt - 1 - i, 0))
    dq, dk, dfg, accg, accb = pl.pallas_call(
        body, name="bwd_qk", grid=(nt,),
        out_shape=(_sds((S, AW), BF16), _sds((S, AW), BF16), _sds((S, LANES), BF16), _sds((2, 8, AW), F32),
                   _sds((8, LANES), F32)),
        in_specs=[hm, hm, rev(AW), rev(AW), rev(LANES), rev(LANES), rev(LANES), _const((1, AW)), _const((1, AW)),
                  _const((AW, LANES)), _const((LANES, AW))],
        out_specs=(rev(AW), rev(AW), rev(LANES), _const((2, 8, AW)), _const((8, LANES))),
        scratch_shapes=[pltpu.VMEM((8, LANES), F32)], compiler_params=_cp(("arbitrary",)),
    )(dqa, dka, qn, kn, rq, rk, fgb, qg512, kg512, e512, et512)
    return dq, dk, dfg, accg, accb


def _bwd_in(dq, dk, dv, dalin, dagate, dfg, w_in_p, x, dx2, mod8, n1g, tm):
    S = x.shape[0]

    def body(dq_ref, dk_ref, dv_ref, dal_ref, dag_ref, dfg_ref, w_ref, x_ref, dx2_ref, mod_ref, n1g_ref,
             dx_ref, acc_ref):
        @pl.when(pl.program_id(0) == 0)
        def _():
            acc_ref[...] = jnp.zeros_like(acc_ref)

        def part(ref, a, b):
            return lax.dot_general(ref[...], w_ref[:, a:b], NT, preferred_element_type=F32)

        dh = (part(dq_ref, 0, 512) + part(dk_ref, 512, 1024) + part(dv_ref, 1024, 1536) + part(dal_ref, 1536, 2048)
              + part(dag_ref, 2048, 2560) + part(dfg_ref, 2560, NP))
        xv = x_ref[...]
        r1 = lax.rsqrt(jnp.mean(xv * xv, axis=-1, keepdims=True) + EPS)
        xn = xv * r1
        gain = n1g_ref[...] * (1.0 + mod_ref[1:2, :])
        t = dh * xn
        acc_ref[0] += _colsum8(dh)
        acc_ref[1] += _colsum8(t * n1g_ref[...])
        acc_ref[2] += _colsum8(t * (1.0 + mod_ref[1:2, :]))
        dxn = dh * gain
        dx_ref[...] = dx2_ref[...] + r1 * (dxn - xn * jnp.mean(dxn * xn, axis=-1, keepdims=True))

    return pl.pallas_call(
        body, name="bwd_in", grid=(S // tm,),
        out_shape=(_sds((S, D), F32), _sds((3, 8, D), F32)),
        in_specs=[_rows(tm, AW), _rows(tm, AW), _rows(tm, AW), _rows(tm, CW), _rows(tm, CW), _rows(tm, LANES),
                  _const((D, NP)), _rows(tm, D), _rows(tm, D), _const((8, D)), _const((1, D))],
        out_specs=(_rows(tm, D), _const((3, 8, D))),
        compiler_params=_cp(("arbitrary",)),
    )(dq, dk, dv, dalin, dagate, dfg, w_in_p, x, dx2, mod8, n1g)


def _adam(w, g, m, v):
    m_new = B1 * m + (1.0 - B1) * g
    v_new = B2 * v + (1.0 - B2) * (g * g)
    m_hat = m_new / (1.0 - B1 ** STEP)
    v_hat = v_new / (1.0 - B2 ** STEP)
    delta = -LR * (m_hat / (jnp.sqrt(v_hat) + AEPS) + WD * w)
    return delta, m_new, v_new


def _reduce_adamw(slots, w, m, v, name, tr=256):
    ns, R, C = slots.shape
    tr = tr if R % tr == 0 else R

    def body(s_ref, w_ref, m_ref, v_ref, g_ref, d_ref, mo_ref, vo_ref):
        g = s_ref[0].astype(F32)
        for k in range(1, ns):
            g = g + s_ref[k].astype(F32)
        g_ref[...] = g
        d_ref[...], mo_ref[...], vo_ref[...] = _adam(w_ref[...], g, m_ref[...], v_ref[...])

    blk = pl.BlockSpec((tr, C), lambda i: (i, 0))
    return pl.pallas_call(
        body, name=name, grid=(R // tr,), out_shape=tuple(_sds((R, C), F32) for _ in range(4)),
        in_specs=[pl.BlockSpec((ns, tr, C), lambda i: (0, i, 0)), blk, blk, blk], out_specs=(blk, blk, blk, blk),
        compiler_params=_cp(("parallel",)),
    )(slots, w, m, v)


def _sum_slots(slots, name, tr=256):
    ns, R, C = slots.shape
    tr = tr if R % tr == 0 else R

    def body(s_ref, o_ref):
        g = s_ref[0].astype(F32)
        for k in range(1, ns):
            g = g + s_ref[k].astype(F32)
        o_ref[...] = g

    return pl.pallas_call(
        body, name=name, grid=(R // tr,), out_shape=_sds((R, C), F32),
        in_specs=[pl.BlockSpec((ns, tr, C), lambda i: (0, i, 0))], out_specs=pl.BlockSpec((tr, C), lambda i: (i, 0)),
        compiler_params=_cp(("parallel",)),
    )(slots)


def _ada_adamw(sct, dmod, w, m, v):
    R, C = w.shape
    tr, bc = 256, 512

    def body(sct_ref, dm_ref, w_ref, m_ref, v_ref, g_ref, d_ref, mo_ref, vo_ref):
        g = sct_ref[:, 0:1] * dm_ref[0:1, :]
        for b in range(1, N_DEV):
            g = g + sct_ref[:, b:b + 1] * dm_ref[b:b + 1, :]
        g_ref[...] = g
        d_ref[...], mo_ref[...], vo_ref[...] = _adam(w_ref[...], g, m_ref[...], v_ref[...])

    blk = pl.BlockSpec((tr, bc), lambda i, j: (i, j))
    return pl.pallas_call(
        body, name="ada_adamw", grid=(R // tr, C // bc), out_shape=tuple(_sds((R, C), F32) for _ in range(4)),
        in_specs=[pl.BlockSpec((tr, N_DEV), lambda i, j: (i, 0)), pl.BlockSpec((N_DEV, bc), lambda i, j: (0, j)),
                  blk, blk, blk],
        out_specs=(blk, blk, blk, blk), compiler_params=_cp(("parallel", "parallel")),
    )(sct, dmod, w, m, v)


def _small_reduce(slots, fold):
    def body(s_ref, f_ref, o_ref):
        tot = s_ref[0:1, :]
        for k in range(1, N_DEV):
            tot = tot + s_ref[k:k + 1, :]
        o_ref[:, 0:6144] = tot[:, 0:6144]
        o_ref[:, 6144:7168] = tot[:, 6144:7168]
        for t, src in enumerate((8192, 8704)):
            v8 = jnp.broadcast_to(tot[:, src:src + AW], (8, AW))
            o_ref[:, 7168 + t * LANES:7168 + (t + 1) * LANES] = jnp.dot(
                v8, f_ref[...], precision=HI, preferred_element_type=F32)[0:1, :]
        o_ref[:, 7424:7552] = tot[:, 9216:9344]
        o_ref[:, 7552:10112] = tot[:, 9344:11904]
        o_ref[:, 10112:SMALL_OUT] = tot[:, 7168:8192]

    return pl.pallas_call(
        body, name="small_reduce", out_shape=_sds((1, SMALL_OUT), F32),
        in_specs=[pl.BlockSpec(memory_space=pltpu.VMEM), pl.BlockSpec(memory_space=pltpu.VMEM)],
        out_specs=pl.BlockSpec(memory_space=pltpu.VMEM),
    )(slots, fold)


def _perm_in(w):
    pad = jnp.zeros((w.shape[0], NP - 2568), w.dtype)
    return jnp.concatenate([w[:, :1536], w[:, 1544:2568], w[:, 1536:1544], pad], axis=1)


def _pad_lanes(vec, n=LANES):
    return jnp.pad(vec, ((0, 0), (0, n - vec.shape[1])))


def kernel(x, c, w_ada, b_ada, norm1_g, w_in, q_norm_g, k_norm_g, b_f, conv_w, conv_b, conv_ln_g, conv_ln_b, beta_attn, beta_conv, w_out, norm2_g, w_ff1, w_ff2, loss_target, m_w_ada, m_b_ada, m_norm1_g, m_w_in, m_q_norm_g, m_k_norm_g, m_b_f, m_conv_w, m_conv_b, m_conv_ln_g, m_conv_ln_b, m_beta_attn, m_beta_conv, m_w_out, m_norm2_g, m_w_ff1, m_w_ff2, v_w_ada, v_b_ada, v_norm1_g, v_w_in, v_q_norm_g, v_k_norm_g, v_b_f, v_conv_w, v_conv_b, v_conv_ln_g, v_conv_ln_b, v_beta_attn, v_beta_conv, v_w_out, v_norm2_g, v_w_ff1, v_w_ff2):
    S = x.shape[1]
    tm = min(256, S)
    tq = min(512, S // 2)
    xs, tgt = x[0], loss_target[0]
    chip = 2 * lax.axis_index("x") + lax.axis_index("y")
    e512, et512 = _head_sum_mats()

    conv_w32 = jnp.pad(conv_w[0], ((0, 1), (0, 0)))
    c_all, g_in, g_out, g_ff1, g_ff2, g_cw = _exchange(
        [(c, "bcast8"), (w_in[0].astype(BF16), "chip4"), (w_out[0].astype(BF16), "chip4"),
         (w_ff1[0].astype(BF16), "chip4"), (w_ff2[0].astype(BF16), "chip4"), (conv_w32, "chip4")], "gather_weights")
    c_all = c_all.reshape(N_DEV, D)
    w_in_p = _perm_in(jnp.transpose(g_in, (1, 0, 2)).reshape(D, 2568))
    w_out_f = g_out.reshape(D, D)
    w1, w2 = g_ff1, g_ff2
    cw32 = jnp.transpose(g_cw, (1, 0, 2)).reshape(HALO, CW)

    b_shard = lax.dynamic_slice(b_ada, (0, chip * 1536), (1, 1536))
    mod_rows, sc_all = _mod_shard(c_all, w_ada[0], b_shard)
    (mod_slots,) = _exchange([(mod_rows.reshape(N_DEV, 1, 1536), "all8")], "scatter_mod")
    mod = mod_slots.reshape(4, 2, 1536)[:, 0, :].reshape(6, D)
    mod8 = jnp.pad(mod, ((0, 2), (0, 0)))

    qg512 = jnp.tile(q_norm_g, (1, NH))
    kg512 = jnp.tile(k_norm_g, (1, NH))
    bf128 = _pad_lanes(b_f)

    h1, qh, kh, vb, qn, kn, rq, rk, fgb, alin, agate, u0 = _fwd_in(xs, mod8, norm1_g, w_in_p, e512, et512, qg512, kg512,
                                                                   bf128, tm)
    qa, ka, fcum = _fwd_decay(fgb, qh, kh, tm)
    fs, fe, bd = _skip_tables(fcum, q_norm_g, k_norm_g, tq)
    o_attn, lsec, lser = _attn_fwd(qa, ka, vb, fs, fe, bd, tq)
    u1, mc = _fwd_conv(u0, cw32, conv_b, conv_ln_g, conv_ln_b, beta_conv, tm)
    merged, ob, x2, h2 = _fwd_out(o_attn, mc, xs, mod8, norm2_g, beta_attn, w_out_f, tm)
    tf = min(512, S)
    r, dy, loss8, dg2 = _fwd_ffn(h2, w1, w2, x2, tgt, mod8, tf)

    df2, df1, dh2 = _bwd_ffn(dy, mod8, r, w1, w2, tf)
    gw_ff2 = _wgrad(r, df2, "wgrad_ff2", square_a=True)
    gw_ff1 = _wgrad(h2, df1, "wgrad_ff1")
    dx2, do, doa, du1, acc_d, acc_h = _bwd_mid(dh2, dy, x2, ob, o_attn, u1, mod8, norm2_g, beta_attn, beta_conv,
                                               conv_ln_g, conv_ln_b, w_out_f, tm)
    gw_out = _wgrad(merged, do, "wgrad_out")
    dqa, dr = _attn_bwd_dq(qa, ka, vb, doa, o_attn, lsec, fs, fe, bd, tq)
    dka, dv = _attn_bwd_dkv(qa, ka, vb, doa, lser, dr, fs, fe, bd, tq)
    dalin, dagate, dcw, dcb = _bwd_conv(du1, u0, alin, agate, cw32, tm)
    dq, dk, dfg, accg, accb = _bwd_qk(dqa, dka, qn, kn, rq, rk, fgb, qg512, kg512, e512, et512, tm)
    grad_x, acc1 = _bwd_in(dq, dk, dv, dalin, dagate, dfg, w_in_p, xs, dx2, mod8, norm1_g, tm)
    gw_in = jnp.concatenate(
        [_wgrad(h1, dq, "wgrad_in_q"), _wgrad(h1, dk, "wgrad_in_k"), _wgrad(h1, dv, "wgrad_in_v"),
         _wgrad(h1, dfg, "wgrad_in_f")[:, :NH], _wgrad(h1, dalin, "wgrad_in_a"), _wgrad(h1, dagate, "wgrad_in_g")],
        axis=1)

    s8 = lambda a: jnp.sum(a, axis=-2)
    a1, ad, ah = s8(acc1), s8(acc_d), s8(acc_h)
    small = jnp.concatenate(
        [a1[0], a1[1], ad[3], ad[0], ad[1], s8(dg2),
         a1[2], ad[2], s8(accg).reshape(-1), s8(accb), s8(dcb), ah[2], ah[3], ah[0], ah[1]]).reshape(1, SMALL_IN)
    gcw = s8(dcw)[:KC]
    small_s, p_in, p_out, p_ff1, p_ff2, p_cw = _exchange(
        [(small, "bcast8"),
         (jnp.transpose(gw_in.reshape(D, 4, 642), (1, 0, 2)).astype(BF16), "chip4p"),
         (gw_out.reshape(4, 256, D).astype(BF16), "chip4p"),
         (jnp.transpose(gw_ff1.reshape(D, 4, D), (1, 0, 2)).astype(BF16), "chip4p"),
         (gw_ff2.reshape(4, D, D).astype(BF16), "chip4p"),
         (jnp.transpose(gcw.reshape(KC, 4, LANES), (1, 0, 2)), "chip4p")], "scatter_grads")
    small_s = small_s.reshape(N_DEV, SMALL_IN)
    s_in, s_out, s_ff1, s_ff2, s_cw = _exchange(
        [(_sum_slots(p_in, "chipsum_in"), "pair"), (_sum_slots(p_out, "chipsum_out"), "pair"),
         (_sum_slots(p_ff1, "chipsum_ff1"), "pair"), (_sum_slots(p_ff2, "chipsum_ff2"), "pair"),
         (_sum_slots(p_cw, "chipsum_conv_w"), "pair")], "pair_grads")

    g_in_, d_in, nm_in, nv_in = _reduce_adamw(s_in, w_in[0], m_w_in[0], v_w_in[0], "adamw_in")
    g_out_, d_out, nm_out, nv_out = _reduce_adamw(s_out, w_out[0], m_w_out[0], v_w_out[0], "adamw_out")
    g_f1, d_f1, nm_f1, nv_f1 = _reduce_adamw(s_ff1, w_ff1[0], m_w_ff1[0], v_w_ff1[0], "adamw_ff1")
    g_f2, d_f2, nm_f2, nv_f2 = _reduce_adamw(s_ff2, w_ff2[0], m_w_ff2[0], v_w_ff2[0], "adamw_ff2")
    g_cw_, d_cw, nm_cw, nv_cw = _reduce_adamw(s_cw, conv_w[0], m_conv_w[0], v_conv_w[0], "adamw_conv_w")
    dmod_shard = lax.dynamic_slice(small_s[:, :6 * D], (0, chip * 1536), (N_DEV, 1536))
    g_ada, d_ada, nm_ada, nv_ada = _ada_adamw(sc_all.T, dmod_shard, w_ada[0], m_w_ada[0], v_w_ada[0])

    fold = np.zeros((AW, LANES), np.float32)
    fold[np.arange(AW), np.arange(AW) % DH] = 1.0
    g_small = _small_reduce(small_s, jnp.asarray(fold))
    smalls = [b_ada, norm1_g, q_norm_g, k_norm_g, b_f, conv_b, conv_ln_g, conv_ln_b, beta_attn, beta_conv, norm2_g]
    m_smalls = [m_b_ada, m_norm1_g, m_q_norm_g, m_k_norm_g, m_b_f, m_conv_b, m_conv_ln_g, m_conv_ln_b, m_beta_attn,
                m_beta_conv, m_norm2_g]
    v_smalls = [v_b_ada, v_norm1_g, v_q_norm_g, v_k_norm_g, v_b_f, v_conv_b, v_conv_ln_g, v_conv_ln_b, v_beta_attn,
                v_beta_conv, v_norm2_g]
    widths = [a.shape[1] for a in smalls]
    padded = [-(-n // LANES) * LANES for n in widths]
    pack = lambda arrs, fill: jnp.concatenate(
        [jnp.pad(a, ((0, 0), (0, p - a.shape[1])), constant_values=fill) for a, p in zip(arrs, padded)], axis=1)
    outs_small = _reduce_adamw(g_small.reshape(1, 1, SMALL_OUT), pack(smalls, 0.0), pack(m_smalls, 0.0),
                               pack(v_smalls, 1.0), "adamw_small")
    offs = np.concatenate([[0], np.cumsum(padded)])

    def unpack(a):
        return [a[:, int(o):int(o) + n] for o, n in zip(offs[:-1], widths)]

    gs, ds, ms, vs = (unpack(a) for a in outs_small)

    loss = lax.psum(loss8[0, 0], ("x", "y", "c"))
    big = {"w_ada": (g_ada, d_ada, nm_ada, nv_ada), "w_in": (g_in_, d_in, nm_in, nv_in),
           "conv_w": (g_cw_, d_cw, nm_cw, nv_cw), "w_out": (g_out_, d_out, nm_out, nv_out),
           "w_ff1": (g_f1, d_f1, nm_f1, nv_f1), "w_ff2": (g_f2, d_f2, nm_f2, nv_f2)}
    small_names = ["b_ada", "norm1_g", "q_norm_g", "k_norm_g", "b_f", "conv_b", "conv_ln_g", "conv_ln_b", "beta_attn",
                   "beta_conv", "norm2_g"]
    order = ["w_ada", "b_ada", "norm1_g", "w_in", "q_norm_g", "k_norm_g", "b_f", "conv_w", "conv_b", "conv_ln_g",
             "conv_ln_b", "beta_attn", "beta_conv", "w_out", "norm2_g", "w_ff1", "w_ff2"]

    def leaf(name, which):
        if name in big:
            return big[name][which][None]
        return (gs, ds, ms, vs)[which][small_names.index(name)]

    return (loss, grad_x[None], *[leaf(n, 0) for n in order], *[leaf(n, 1) for n in order],
            *[leaf(n, 2) for n in order], *[leaf(n, 3) for n in order])
```

```python
import functools

import numpy as np
import jax
import jax.numpy as jnp
from jax import lax
from jax.experimental import pallas as pl
from jax.experimental.pallas import tpu as pltpu

F32, BF16 = jnp.float32, jnp.bfloat16
HI = lax.Precision.HIGHEST
D = 1024
AW = 512
CW = 512
NH = 8
DH = 64
KC = 31
DFF = 4096
NP = 2688
EPS = 1e-6
NEG = -1e30
LANES = 128
VMEM_LIMIT = 56 * 2**20
NT = (((1,), (1,)), ((), ()))
TN = (((0,), (0,)), ((), ()))
LR, B1, B2, AEPS, WD, STEP = 0.001, 0.9, 0.999, 1e-08, 0.01, 10
N_DEV = 8
SMALL_IN = 11904
SMALL_OUT = 11136


def _cp(sem=None, vmem=VMEM_LIMIT):
    kw = dict(vmem_limit_bytes=vmem)
    if sem is not None:
        kw["dimension_semantics"] = sem
    return pltpu.CompilerParams(**kw)


def _rows(tm, n):
    return pl.BlockSpec((tm, n), lambda i: (i, 0))


def _const(shape):
    nd = len(shape)
    return pl.BlockSpec(shape, lambda *_: (0,) * nd)


def _sds(shape, dt):
    return jax.ShapeDtypeStruct(shape, dt)


def _lane(shape):
    return lax.broadcasted_iota(jnp.int32, shape, len(shape) - 1)


def _sigmoid(x):
    return 1.0 / (1.0 + jnp.exp(-x))


def _exchange(items, name, chunks=1):
    kinds = [k for _, k in items]
    srcs = [s for s, _ in items]
    out_shapes = []
    for s, k in items:
        if k in ("all8", "chip8"):
            out_shapes.append(_sds((N_DEV,) + s.shape[1:], s.dtype))
        elif k == "bcast8":
            out_shapes.append(_sds((N_DEV,) + s.shape, s.dtype))
        elif k == "chip4":
            out_shapes.append(_sds((4,) + s.shape, s.dtype))
        elif k == "chip4p":
            out_shapes.append(_sds((4,) + s.shape[1:], s.dtype))
        else:
            out_shapes.append(_sds((2,) + s.shape, s.dtype))
    n = len(items)
    all_masks = {"chip4": [2, 4, 6], "chip4p": [2, 4, 6], "pair": [1]}
    masks = {k: all_masks.get(k, [1, 2, 3, 4, 5, 6, 7]) for k in set(kinds)}
    piece_rows = [o.shape[1] for o in out_shapes]
    n_chunks = [chunks if r % (8 * chunks) == 0 else 1 for r in piece_rows]
    sem_index = {}
    for t, k in enumerate(kinds):
        for m in masks[k]:
            for q in range(n_chunks[t]):
                sem_index[(t, m, q)] = len(sem_index)
    n_sem = len(sem_index)

    def body(*refs):
        src_refs, dst_refs = refs[:n], refs[n:2 * n]
        send_sems, recv_sems, local_sems = refs[2 * n:]
        x, y, c = lax.axis_index("x"), lax.axis_index("y"), lax.axis_index("c")
        my_id = 4 * x + 2 * y + c
        my_chip = 2 * x + y

        def piece(t, dev_id, chip):
            k = kinds[t]
            if k == "all8":
                return src_refs[t].at[dev_id]
            if k in ("chip8", "chip4p"):
                return src_refs[t].at[chip]
            return src_refs[t]

        def slot(t):
            k = kinds[t]
            return dst_refs[t].at[my_chip if k in ("chip4", "chip4p") else c if k == "pair" else my_id]

        copies = []
        for t in range(n):
            lc = pltpu.make_async_copy(piece(t, my_id, my_chip), slot(t), local_sems.at[t])
            lc.start()
            copies.append(lc)
            for m in masks[kinds[t]]:
                px = 1 - x if m & 4 else x
                py = 1 - y if m & 2 else y
                pc = 1 - c if m & 1 else c
                rows = piece_rows[t] // n_chunks[t]
                for q in range(n_chunks[t]):
                    s = sem_index[(t, m, q)]
                    rc = pltpu.make_async_remote_copy(
                        src_ref=piece(t, 4 * px + 2 * py + pc, 2 * px + py).at[pl.ds(q * rows, rows)],
                        dst_ref=slot(t).at[pl.ds(q * rows, rows)],
                        send_sem=send_sems.at[s], recv_sem=recv_sems.at[s],
                        device_id=(px, py, pc), device_id_type=pl.DeviceIdType.MESH)
                    rc.start()
                    copies.append(rc)
        for cp in copies:
            cp.wait()

    outs = pl.pallas_call(
        body, name=name, out_shape=tuple(out_shapes),
        in_specs=[pl.BlockSpec(memory_space=pl.ANY)] * n,
        out_specs=tuple(pl.BlockSpec(memory_space=pl.ANY) for _ in range(n)),
        scratch_shapes=[pltpu.SemaphoreType.DMA((n_sem,)), pltpu.SemaphoreType.DMA((n_sem,)),
                        pltpu.SemaphoreType.DMA((n,))],
    )(*srcs)
    return list(outs)


def _mod_shard(c_all, w_ada, b_shard):
    n = w_ada.shape[1]

    def body(c_ref, w_ref, b_ref, o_ref, sc_ref):
        cv = c_ref[...]
        sc = cv * _sigmoid(cv)
        sc_ref[...] = sc
        o_ref[...] = jnp.dot(sc, w_ref[...], precision=HI, preferred_element_type=F32) + b_ref[...]

    bn = 512
    return pl.pallas_call(
        body, name="mod_shard", out_shape=(_sds((N_DEV, n), F32), _sds((N_DEV, D), F32)), grid=(n // bn,),
        in_specs=[_const((N_DEV, D)), pl.BlockSpec((D, bn), lambda j: (0, j)), pl.BlockSpec((1, bn), lambda j: (0, j))],
        out_specs=(pl.BlockSpec((N_DEV, bn), lambda j: (0, j)), _const((N_DEV, D))),
        compiler_params=_cp(("arbitrary",)),
    )(c_all, w_ada, b_shard)


def _head_sum_mats():
    e = np.zeros((AW, LANES), np.float32)
    for h in range(NH):
        e[h * DH:(h + 1) * DH, h] = 1.0
    return jnp.asarray(e, BF16), jnp.asarray(e.T.copy(), BF16)


def _dot2(x, w):
    hi = x.astype(BF16)
    lo = (x - hi.astype(F32)).astype(BF16)
    return jnp.dot(hi, w, preferred_element_type=F32) + jnp.dot(lo, w, preferred_element_type=F32)


def _fwd_in(x, mod8, n1g, w_in_p, e512, et512, qg512, kg512, bf128, tm):
    S = x.shape[0]

    def body(x_ref, mod_ref, n1g_ref, w_ref, e_ref, et_ref, qg_ref, kg_ref, bf_ref,
             h1_ref, qh_ref, kh_ref, v_ref, qn_ref, kn_ref, rq_ref, rk_ref, fgb_ref, alin_ref, agate_ref, u0_ref):
        xv = x_ref[...]
        r1 = lax.rsqrt(jnp.mean(xv * xv, axis=-1, keepdims=True) + EPS)
        h = (xv * r1) * (n1g_ref[...] * (1.0 + mod_ref[1:2, :])) + mod_ref[0:1, :]
        hb = h.astype(BF16)
        h1_ref[...] = hb

        def seg(a, b):
            return jnp.dot(hb, w_ref[:, a:b], preferred_element_type=F32)

        def headnorm(t, g_ref, scale, n_ref, r_ref, o_ref):
            ss = _dot2(t * t, e_ref[...])
            r = lax.rsqrt(ss * (1.0 / DH) + EPS)
            tn = t * _dot2(r, et_ref[...])
            n_ref[...] = tn.astype(BF16)
            r_ref[...] = r
            o_ref[...] = (tn * (g_ref[...] * scale)).astype(BF16)

        headnorm(seg(0, 512), qg_ref, DH ** -0.5, qn_ref, rq_ref, qh_ref)
        headnorm(seg(512, 1024), kg_ref, 1.0, kn_ref, rk_ref, kh_ref)
        v_ref[...] = seg(1024, 1536).astype(BF16)
        alin = seg(1536, 2048)
        agate = seg(2048, 2560)
        alin_ref[...] = alin.astype(BF16)
        agate_ref[...] = agate.astype(BF16)
        u0_ref[...] = alin * _sigmoid(agate)
        fgb_ref[...] = seg(2560, NP) + bf_ref[...]

    bf = lambda: _sds((S, AW), BF16)
    return pl.pallas_call(
        body, name="fwd_in", grid=(S // tm,),
        out_shape=(_sds((S, D), BF16), bf(), bf(), bf(), bf(), bf(), _sds((S, LANES), F32), _sds((S, LANES), F32),
                   _sds((S, LANES), F32), bf(), bf(), _sds((S, CW), F32)),
        in_specs=[_rows(tm, D), _const((8, D)), _const((1, D)), _const((D, NP)), _const((AW, LANES)), _const((LANES, AW)),
                  _const((1, AW)), _const((1, AW)), _const((1, LANES))],
        out_specs=(_rows(tm, D), _rows(tm, AW), _rows(tm, AW), _rows(tm, AW), _rows(tm, AW), _rows(tm, AW),
                   _rows(tm, LANES), _rows(tm, LANES), _rows(tm, LANES), _rows(tm, AW), _rows(tm, AW), _rows(tm, CW)),
        compiler_params=_cp(("parallel",)),
    )(x, mod8, n1g, w_in_p, e512, et512, qg512, kg512, bf128)


def _split3(f):
    f1 = f.astype(BF16).astype(F32)
    f2 = (f - f1).astype(BF16).astype(F32)
    return f1, f2, f - f1 - f2


def _dot3(w, x):
    return sum(jnp.dot(w, piece.astype(BF16), preferred_element_type=F32) for piece in _split3(x))


def _fwd_decay(fgb, qh, kh, tm):
    S = fgb.shape[0]

    def body(fgb_ref, qh_ref, kh_ref, qa_ref, ka_ref, f_ref, carry_ref):
        @pl.when(pl.program_id(0) == 0)
        def _():
            carry_ref[...] = jnp.zeros_like(carry_ref)

        fb = fgb_ref[...]
        lf = jnp.minimum(fb, 0.0) - jnp.log1p(jnp.exp(-jnp.abs(fb)))
        tri = (lax.broadcasted_iota(jnp.int32, (tm, tm), 0) >= lax.broadcasted_iota(jnp.int32, (tm, tm), 1)
               ).astype(F32).astype(BF16)
        cs = _dot3(tri, lf) + carry_ref[0:1, :]
        f_ref[...] = cs
        carry_ref[...] = jnp.broadcast_to(cs[tm - 1:tm, :], carry_ref.shape)
        lane = _lane((tm, LANES))
        ones = jnp.where((lane >= 67) & (lane < 70), 1.0, 0.0)
        ones_k = jnp.where((lane >= 64) & (lane < 67), 1.0, 0.0)
        for p in range(NH // 2):
            qp = qh_ref[:, p * LANES:(p + 1) * LANES].astype(F32)
            kp = kh_ref[:, p * LANES:(p + 1) * LANES].astype(F32)
            for hh in range(2):
                h = 2 * p + hh
                f1, f2, f3 = _split3(cs[:, h:h + 1])
                qb = qp if hh == 0 else pltpu.roll(qp, 64, 1)
                kb = kp if hh == 0 else pltpu.roll(kp, 64, 1)
                augq = jnp.where(lane == 64, f1, jnp.where(lane == 65, f2, jnp.where(lane == 66, f3, ones)))
                augk = jnp.where(lane == 67, -f1, jnp.where(lane == 68, -f2, jnp.where(lane == 69, -f3, ones_k)))
                qa_ref[h] = jnp.where(lane < DH, qb, augq).astype(BF16)
                ka_ref[h] = jnp.where(lane < DH, kb, augk).astype(BF16)

    hm = pl.BlockSpec((NH, tm, LANES), lambda i: (0, i, 0))
    return pl.pallas_call(
        body, name="fwd_decay", grid=(S // tm,),
        out_shape=(_sds((NH, S, LANES), BF16), _sds((NH, S, LANES), BF16), _sds((S, LANES), F32)),
        in_specs=[_rows(tm, LANES), _rows(tm, AW), _rows(tm, AW)], out_specs=(hm, hm, _rows(tm, LANES)),
        scratch_shapes=[pltpu.VMEM((8, LANES), F32)], compiler_params=_cp(("arbitrary",)),
    )(fgb, qh, kh)


def _causal(t):
    return lax.broadcasted_iota(jnp.int32, (t, t), 0) >= lax.broadcasted_iota(jnp.int32, (t, t), 1)


def _causal_t(t):
    return lax.broadcasted_iota(jnp.int32, (t, t), 0) <= lax.broadcasted_iota(jnp.int32, (t, t), 1)


SKIP = -106.0


def _block_loops(first, step, needed, run):
    def both(j):
        return jnp.logical_and(needed(0, j), needed(1, j))

    def walk(heads):
        def go(j):
            run(j, heads)
            return j + step
        return go

    j = lax.while_loop(both, walk((0, 1)), first)
    lax.while_loop(functools.partial(needed, 0), walk((0,)), j)
    lax.while_loop(functools.partial(needed, 1), walk((1,)), j)


def _skip_tables(f, qg, kg, tq):
    fs = f[0::tq, :NH].T
    fe = f[tq - 1::tq, :NH].T
    bound = 2.0 * 1.03 * DH ** 0.5 * jnp.max(jnp.abs(qg)) * jnp.max(jnp.abs(kg))
    return fs, fe, bound.reshape(1, 1)


SMEM_SPEC = pl.BlockSpec(memory_space=pltpu.SMEM)


def _causal_rect(rows, cols, col0):
    return (lax.broadcasted_iota(jnp.int32, (rows, cols), 0)
            >= lax.broadcasted_iota(jnp.int32, (rows, cols), 1) + col0)


def _chunk(tq):
    return min(256, tq)


def _attn_fwd(qa, ka, v, fs, fe, bd, tq):
    S = qa.shape[1]
    nq = S // tq
    tc = _chunk(tq)

    def body(fs_ref, fe_ref, bd_ref, qa_ref, ka_ref, v_ref, o_ref, lsec_ref, lser_ref, m_ref, l_ref, acc_ref):
        pr, i = pl.program_id(0), pl.program_id(1)
        sel_a = _lane((tq, LANES)) < DH
        m_ref[...] = jnp.full(m_ref.shape, NEG, F32)
        l_ref[...] = jnp.zeros_like(l_ref)
        acc_ref[...] = jnp.zeros_like(acc_ref)

        def kv_step(j, heads, masked=False):
            for c0 in range(0, tq, tc):
                start = pl.multiple_of(j * tq + c0, tc)
                vb = v_ref[pl.ds(start, tc), :]
                for hh in heads:
                    s = lax.dot_general(qa_ref[hh], ka_ref[hh, pl.ds(start, tc), :], NT, preferred_element_type=F32)
                    if masked:
                        s = jnp.where(_causal_rect(tq, tc, c0), s, NEG)
                    m_prev = m_ref[hh]
                    m_new = jnp.maximum(m_prev, jnp.max(s, axis=1, keepdims=True))
                    alpha = jnp.exp(m_prev - m_new)
                    p = jnp.exp(s - jnp.tile(m_new, (1, tc // LANES)))
                    l_ref[hh] = alpha * l_ref[hh] + jnp.sum(p, axis=1, keepdims=True)
                    m_ref[hh] = m_new
                    acc_ref[hh] = alpha * acc_ref[hh] + jnp.dot(p.astype(BF16), vb, preferred_element_type=F32)

        def needed(hh, j):
            top = fs_ref[2 * pr + hh, i] + bd_ref[0, 0]
            return jnp.logical_and(j >= 0, top - fe_ref[2 * pr + hh, jnp.maximum(j, 0)] >= SKIP)

        kv_step(i, (0, 1), masked=True)
        _block_loops(i - 1, -1, needed, kv_step)
        o_ref[...] = jnp.where(sel_a, acc_ref[0] / l_ref[0], acc_ref[1] / l_ref[1])
        lse_a = m_ref[0] + jnp.log(l_ref[0])
        lse_b = m_ref[1] + jnp.log(l_ref[1])
        lsec_ref[0] = lse_a
        lsec_ref[1] = lse_b
        row = lax.broadcasted_iota(jnp.int32, (8, tq), 0)
        lser_ref[0, 0] = jnp.where(row == 0, lse_a.T[0:8, :], lse_b.T[0:8, :])

    return pl.pallas_call(
        body, name="attn_fwd", grid=(NH // 2, nq),
        out_shape=(_sds((S, AW), F32), _sds((NH, S, LANES), F32), _sds((NH // 2, nq, 8, tq), F32)),
        in_specs=[SMEM_SPEC, SMEM_SPEC, SMEM_SPEC,
                  pl.BlockSpec((2, tq, LANES), lambda p, i: (p, i, 0)),
                  pl.BlockSpec((2, S, LANES), lambda p, i: (p, 0, 0)),
                  pl.BlockSpec((S, LANES), lambda p, i: (0, p))],
        out_specs=(pl.BlockSpec((tq, LANES), lambda p, i: (i, p)),
                   pl.BlockSpec((2, tq, LANES), lambda p, i: (p, i, 0)),
                   pl.BlockSpec((1, 1, 8, tq), lambda p, i: (p, i, 0, 0))),
        scratch_shapes=[pltpu.VMEM((2, tq, LANES), F32), pltpu.VMEM((2, tq, LANES), F32),
                        pltpu.VMEM((2, tq, LANES), F32)],
        compiler_params=_cp(("parallel", "parallel")),
    )(fs, fe, bd, qa, ka, v)


HALO = 32
CHUNK_ROWS = 64


def _halo_prev(tm):
    return pl.BlockSpec((HALO, CW), lambda i: (jnp.maximum(i * (tm // HALO) - 1, 0), 0))


def _fwd_conv(u0, w32, cb, lng, lnb, beta_c, tm):
    S = u0.shape[0]

    def body(cur_ref, prev_ref, w_ref, cb_ref, lng_ref, lnb_ref, beta_ref, u1_ref, mc_ref, ext_ref):
        i = pl.program_id(0)
        ext_ref[0:HALO, :] = jnp.where(i == 0, 0.0, prev_ref[...])
        ext_ref[HALO:, :] = cur_ref[...]
        for r0 in range(0, tm, CHUNK_ROWS):
            acc = jnp.zeros((CHUNK_ROWS, CW), F32) + cb_ref[...]
            for j in range(KC):
                acc = acc + w_ref[j:j + 1, :] * ext_ref[r0 + 2 + j:r0 + 2 + j + CHUNK_ROWS, :]
            u1_ref[r0:r0 + CHUNK_ROWS, :] = acc
        u1 = u1_ref[...]
        mu = jnp.mean(u1, axis=-1, keepdims=True)
        d = u1 - mu
        rstd = lax.rsqrt(jnp.mean(d * d, axis=-1, keepdims=True) + EPS)
        u2 = d * rstd * lng_ref[...] + lnb_ref[...]
        u3 = u2 * _sigmoid(u2)
        rc = lax.rsqrt(jnp.mean(u3 * u3, axis=-1, keepdims=True) + EPS)
        mc_ref[...] = (u3 * rc * beta_ref[...]).astype(BF16)

    return pl.pallas_call(
        body, name="fwd_conv", grid=(S // tm,),
        out_shape=(_sds((S, CW), F32), _sds((S, CW), BF16)),
        in_specs=[_rows(tm, CW), _halo_prev(tm), _const((HALO, CW)), _const((1, CW)), _const((1, CW)), _const((1, CW)),
                  _const((1, CW))],
        out_specs=(_rows(tm, CW), _rows(tm, CW)),
        scratch_shapes=[pltpu.VMEM((tm + HALO, CW), F32)], compiler_params=_cp(("parallel",)),
    )(u0, u0, w32, cb, lng, lnb, beta_c)


def _fwd_out(o_attn, mc, x, mod8, n2g, beta_a, w_out, tm):
    S = x.shape[0]

    def body(o_ref, mc_ref, x_ref, mod_ref, n2g_ref, beta_ref, w_ref, mg_ref, ob_ref, x2_ref, h2_ref):
        ov = o_ref[...]
        ra = lax.rsqrt(jnp.mean(ov * ov, axis=-1, keepdims=True) + EPS)
        ma = (ov * ra * beta_ref[...]).astype(BF16)
        mcv = mc_ref[...]
        mg_ref[:, 0:AW] = ma
        mg_ref[:, AW:D] = mcv
        o = (jnp.dot(ma, w_ref[0:AW, :], preferred_element_type=F32)
             + jnp.dot(mcv, w_ref[AW:D, :], preferred_element_type=F32))
        ob_ref[...] = o.astype(BF16)
        x2 = x_ref[...] + mod_ref[2:3, :] * o
        x2_ref[...] = x2
        r2 = lax.rsqrt(jnp.mean(x2 * x2, axis=-1, keepdims=True) + EPS)
        h2_ref[...] = ((x2 * r2) * (n2g_ref[...] * (1.0 + mod_ref[4:5, :])) + mod_ref[3:4, :]).astype(BF16)

    return pl.pallas_call(
        body, name="fwd_out", grid=(S // tm,),
        out_shape=(_sds((S, D), BF16), _sds((S, D), BF16), _sds((S, D), F32), _sds((S, D), BF16)),
        in_specs=[_rows(tm, AW), _rows(tm, CW), _rows(tm, D), _const((8, D)), _const((1, D)), _const((1, AW)),
                  _const((D, D))],
        out_specs=(_rows(tm, D), _rows(tm, D), _rows(tm, D), _rows(tm, D)),
        compiler_params=_cp(("parallel",)),
    )(o_attn, mc, x, mod8, n2g, beta_a, w_out)


def _fwd_ffn(h2, w1, w2, x2, tgt, mod8, tm):
    S = h2.shape[0]
    nk = w1.shape[0]
    bf = w1.shape[2]

    def body(h2_ref, w1_ref, w2_ref, x2_ref, tgt_ref, mod_ref, r_ref, dy_ref, loss_ref, dg2_ref, acc_ref):
        i, k = pl.program_id(0), pl.program_id(1)

        @pl.when((i == 0) & (k == 0))
        def _():
            loss_ref[...] = jnp.zeros_like(loss_ref)
            dg2_ref[...] = jnp.zeros_like(dg2_ref)

        r = jnp.maximum(jnp.dot(h2_ref[...], w1_ref[0], preferred_element_type=F32), 0.0)
        r_ref[...] = r.astype(BF16)
        part = jnp.dot((r * r).astype(BF16), w2_ref[0], preferred_element_type=F32)

        @pl.when(k == 0)
        def _():
            acc_ref[...] = part

        @pl.when(k > 0)
        def _():
            acc_ref[...] += part

        @pl.when(k == nk - 1)
        def _():
            f2 = acc_ref[...]
            e = x2_ref[...] + mod_ref[5:6, :] * f2 - tgt_ref[...]
            dy = e * (1.0 / D)
            dy_ref[...] = dy
            loss_ref[...] += 0.5 * jnp.sum(jnp.sum(e * dy, axis=1, keepdims=True), axis=0, keepdims=True)
            dg2_ref[...] += jnp.sum((dy * f2).reshape(tm // 8, 8, D), axis=0)

    return pl.pallas_call(
        body, name="fwd_ffn", grid=(S // tm, nk),
        out_shape=(_sds((S, DFF), BF16), _sds((S, D), F32), _sds((8, LANES), F32), _sds((8, D), F32)),
        in_specs=[pl.BlockSpec((tm, D), lambda i, k: (i, 0)), pl.BlockSpec((1, D, bf), lambda i, k: (k, 0, 0)),
                  pl.BlockSpec((1, bf, D), lambda i, k: (k, 0, 0)), pl.BlockSpec((tm, D), lambda i, k: (i, 0)),
                  pl.BlockSpec((tm, D), lambda i, k: (i, 0)), pl.BlockSpec((8, D), lambda i, k: (0, 0))],
        out_specs=(pl.BlockSpec((tm, bf), lambda i, k: (i, k)), pl.BlockSpec((tm, D), lambda i, k: (i, 0)),
                   pl.BlockSpec((8, LANES), lambda i, k: (0, 0)), pl.BlockSpec((8, D), lambda i, k: (0, 0))),
        scratch_shapes=[pltpu.VMEM((tm, D), F32)], compiler_params=_cp(("arbitrary", "arbitrary")),
    )(h2, w1, w2, x2, tgt, mod8)


def _bwd_ffn(dy, mod8, r, w1, w2, tm):
    S = dy.shape[0]
    nk = w1.shape[0]
    bf = w1.shape[2]

    def body(dy_ref, mod_ref, r_ref, w1_ref, w2_ref, df2_ref, df1_ref, dh2_ref):
        k = pl.program_id(1)
        df2 = (dy_ref[...] * mod_ref[5:6, :]).astype(BF16)

        @pl.when(k == 0)
        def _():
            df2_ref[...] = df2

        da = lax.dot_general(df2, w2_ref[0], NT, preferred_element_type=F32)
        df1 = (da * (2.0 * r_ref[...].astype(F32))).astype(BF16)
        df1_ref[...] = df1
        part = lax.dot_general(df1, w1_ref[0], NT, preferred_element_type=F32)

        @pl.when(k == 0)
        def _():
            dh2_ref[...] = part

        @pl.when(k > 0)
        def _():
            dh2_ref[...] += part

    return pl.pallas_call(
        body, name="bwd_ffn", grid=(S // tm, nk),
        out_shape=(_sds((S, D), BF16), _sds((S, DFF), BF16), _sds((S, D), F32)),
        in_specs=[pl.BlockSpec((tm, D), lambda i, k: (i, 0)), pl.BlockSpec((8, D), lambda i, k: (0, 0)),
                  pl.BlockSpec((tm, bf), lambda i, k: (i, k)), pl.BlockSpec((1, D, bf), lambda i, k: (k, 0, 0)),
                  pl.BlockSpec((1, bf, D), lambda i, k: (k, 0, 0))],
        out_specs=(pl.BlockSpec((tm, D), lambda i, k: (i, 0)), pl.BlockSpec((tm, bf), lambda i, k: (i, k)),
                   pl.BlockSpec((tm, D), lambda i, k: (i, 0))),
        compiler_params=_cp(("parallel", "arbitrary")),
    )(dy, mod8, r, w1, w2)


def _wgrad(a, b, name, square_a=False, tk=512, bm=1024, bn=1024):
    S, M = a.shape
    N = b.shape[1]
    bm, bn, tk = min(bm, M), min(bn, N), min(tk, S)

    def body(a_ref, b_ref, o_ref):
        av = a_ref[...]
        if square_a:
            af = av.astype(F32)
            av = (af * af).astype(BF16)
        part = lax.dot_general(av, b_ref[...], TN, preferred_element_type=F32)

        @pl.when(pl.program_id(2) == 0)
        def _():
            o_ref[...] = part

        @pl.when(pl.program_id(2) > 0)
        def _():
            o_ref[...] += part

    return pl.pallas_call(
        body, name=name, grid=(M // bm, N // bn, S // tk), out_shape=_sds((M, N), F32),
        in_specs=[pl.BlockSpec((tk, bm), lambda mi, ni, k: (k, mi)), pl.BlockSpec((tk, bn), lambda mi, ni, k: (k, ni))],
        out_specs=pl.BlockSpec((bm, bn), lambda mi, ni, k: (mi, ni)),
        compiler_params=_cp(("parallel", "parallel", "arbitrary")),
    )(a, b)


def _colsum8(t):
    return jnp.sum(t.reshape(t.shape[0] // 8, 8, t.shape[1]), axis=0)


def _bwd_mid(dh2, dy, x2, ob, o_attn, u1, mod8, n2g, beta_a, beta_c, lng, lnb, w_out, tm):
    S = dy.shape[0]

    def body(dh2_ref, dy_ref, x2_ref, ob_ref, oa_ref, u1_ref, mod_ref, n2g_ref, ba_ref, bc_ref, lng_ref, lnb_ref, w_ref,
             dx2_ref, do_ref, doa_ref, du1_ref, acc_d_ref, acc_h_ref):
        @pl.when(pl.program_id(0) == 0)
        def _():
            acc_d_ref[...] = jnp.zeros_like(acc_d_ref)
            acc_h_ref[...] = jnp.zeros_like(acc_h_ref)

        x2 = x2_ref[...]
        dh2 = dh2_ref[...]
        r2 = lax.rsqrt(jnp.mean(x2 * x2, axis=-1, keepdims=True) + EPS)
        xn2 = x2 * r2
        gain = n2g_ref[...] * (1.0 + mod_ref[4:5, :])
        dxn = dh2 * gain
        dx2 = dy_ref[...] + r2 * (dxn - xn2 * jnp.mean(dxn * xn2, axis=-1, keepdims=True))
        dx2_ref[...] = dx2
        t = dh2 * xn2
        acc_d_ref[0] += _colsum8(dh2)
        acc_d_ref[1] += _colsum8(t * n2g_ref[...])
        acc_d_ref[2] += _colsum8(t * (1.0 + mod_ref[4:5, :]))
        acc_d_ref[3] += _colsum8(dx2 * ob_ref[...].astype(F32))
        do = (dx2 * mod_ref[2:3, :]).astype(BF16)
        do_ref[...] = do
        dma = lax.dot_general(do, w_ref[0:AW, :], NT, preferred_element_type=F32)
        dmc = lax.dot_general(do, w_ref[AW:D, :], NT, preferred_element_type=F32)
        ov = oa_ref[...]
        ra = lax.rsqrt(jnp.mean(ov * ov, axis=-1, keepdims=True) + EPS)
        on = ov * ra
        acc_h_ref[0] += _colsum8(dma * on)
        don = dma * ba_ref[...]
        doa_ref[...] = (ra * (don - on * jnp.mean(don * on, axis=-1, keepdims=True))).astype(BF16)
        u1 = u1_ref[...]
        mu = jnp.mean(u1, axis=-1, keepdims=True)
        d = u1 - mu
        rstd = lax.rsqrt(jnp.mean(d * d, axis=-1, keepdims=True) + EPS)
        uh = d * rstd
        u2 = uh * lng_ref[...] + lnb_ref[...]
        sg = _sigmoid(u2)
        u3 = u2 * sg
        rc = lax.rsqrt(jnp.mean(u3 * u3, axis=-1, keepdims=True) + EPS)
        u3n = u3 * rc
        acc_h_ref[1] += _colsum8(dmc * u3n)
        du3n = dmc * bc_ref[...]
        du3 = rc * (du3n - u3n * jnp.mean(du3n * u3n, axis=-1, keepdims=True))
        du2 = du3 * (sg * (1.0 + u2 * (1.0 - sg)))
        acc_h_ref[2] += _colsum8(du2 * uh)
        acc_h_ref[3] += _colsum8(du2)
        duh = du2 * lng_ref[...]
        du1_ref[...] = rstd * (duh - jnp.mean(duh, axis=-1, keepdims=True)
                               - uh * jnp.mean(duh * uh, axis=-1, keepdims=True))

    return pl.pallas_call(
        body, name="bwd_mid", grid=(S // tm,),
        out_shape=(_sds((S, D), F32), _sds((S, D), BF16), _sds((S, AW), BF16), _sds((S, CW), F32),
                   _sds((4, 8, D), F32), _sds((4, 8, AW), F32)),
        in_specs=[_rows(tm, D), _rows(tm, D), _rows(tm, D), _rows(tm, D), _rows(tm, AW), _rows(tm, CW), _const((8, D)),
                  _const((1, D)), _const((1, AW)), _const((1, CW)), _const((1, CW)), _const((1, CW)), _const((D, D))],
        out_specs=(_rows(tm, D), _rows(tm, D), _rows(tm, AW), _rows(tm, CW), _const((4, 8, D)), _const((4, 8, AW))),
        compiler_params=_cp(("arbitrary",)),
    )(dh2, dy, x2, ob, o_attn, u1, mod8, n2g, beta_a, beta_c, lng, lnb, w_out)


def _attn_bwd_dq(qa, ka, v, do, o_attn, lsec, fs, fe, bd, tq):
    S = qa.shape[1]
    nq = S // tq
    tc = _chunk(tq)

    def body(fs_ref, fe_ref, bd_ref, qa_ref, ka_ref, v_ref, do_ref, o_ref, lse_ref, dqa_ref, dr_ref, acc_ref):
        pr, i = pl.program_id(0), pl.program_id(1)
        sel_a = _lane((tq, LANES)) < DH
        dov = do_ref[...]
        prod = dov.astype(F32) * o_ref[...]
        zero = jnp.zeros_like(prod)
        deltas = [jnp.broadcast_to(jnp.sum(jnp.where(sel_a, prod, zero), axis=1, keepdims=True), (tq, LANES)),
                  jnp.broadcast_to(jnp.sum(jnp.where(sel_a, zero, prod), axis=1, keepdims=True), (tq, LANES))]
        zb = jnp.zeros_like(dov)
        dos = [jnp.where(sel_a, dov, zb), jnp.where(sel_a, zb, dov)]
        acc_ref[...] = jnp.zeros_like(acc_ref)

        def kv_step(j, heads, masked=False):
            for c0 in range(0, tq, tc):
                start = pl.multiple_of(j * tq + c0, tc)
                vb = v_ref[pl.ds(start, tc), :]
                for hh in heads:
                    kb = ka_ref[hh, pl.ds(start, tc), :]
                    s = lax.dot_general(qa_ref[hh], kb, NT, preferred_element_type=F32)
                    p = jnp.exp(s - jnp.tile(lse_ref[hh], (1, tc // LANES)))
                    if masked:
                        p = jnp.where(_causal_rect(tq, tc, c0), p, 0.0)
                    dp = lax.dot_general(dos[hh], vb, NT, preferred_element_type=F32)
                    ds = p * (dp - jnp.tile(deltas[hh], (1, tc // LANES)))
                    acc_ref[hh] += jnp.dot(ds.astype(BF16), kb, preferred_element_type=F32)

        def needed(hh, j):
            top = fs_ref[2 * pr + hh, i] + bd_ref[0, 0]
            return jnp.logical_and(j >= 0, top - fe_ref[2 * pr + hh, jnp.maximum(j, 0)] >= SKIP)

        kv_step(i, (0, 1), masked=True)
        _block_loops(i - 1, -1, needed, kv_step)
        dqa_ref[...] = acc_ref[...]
        row = lax.broadcasted_iota(jnp.int32, (8, tq), 0)
        da = deltas[0].T[0:8, :]
        db = deltas[1].T[0:8, :]
        dr_ref[0, 0] = jnp.where(row == 0, da, db)

    return pl.pallas_call(
        body, name="attn_bwd_dq", grid=(NH // 2, nq),
        out_shape=(_sds((NH, S, LANES), F32), _sds((NH // 2, nq, 8, tq), F32)),
        in_specs=[SMEM_SPEC, SMEM_SPEC, SMEM_SPEC,
                  pl.BlockSpec((2, tq, LANES), lambda p, i: (p, i, 0)),
                  pl.BlockSpec((2, S, LANES), lambda p, i: (p, 0, 0)),
                  pl.BlockSpec((S, LANES), lambda p, i: (0, p)),
                  pl.BlockSpec((tq, LANES), lambda p, i: (i, p)),
                  pl.BlockSpec((tq, LANES), lambda p, i: (i, p)),
                  pl.BlockSpec((2, tq, LANES), lambda p, i: (p, i, 0))],
        out_specs=(pl.BlockSpec((2, tq, LANES), lambda p, i: (p, i, 0)),
                   pl.BlockSpec((1, 1, 8, tq), lambda p, i: (p, i, 0, 0))),
        scratch_shapes=[pltpu.VMEM((2, tq, LANES), F32)],
        compiler_params=_cp(("parallel", "parallel")),
    )(fs, fe, bd, qa, ka, v, do, o_attn, lsec)


def _attn_bwd_dkv(qa, ka, v, do, lser, dr, fs, fe, bd, tq):
    S = qa.shape[1]
    nq = S // tq
    tc = _chunk(tq)

    def body(fs_ref, fe_ref, bd_ref, ka_ref, v_ref, qa_ref, do_ref, lse_ref, dr_ref, dka_ref, dv_ref, acck_ref,
             accv_ref):
        pr, j = pl.program_id(0), pl.program_id(1)
        sel_a = _lane((tq, LANES)) < DH
        vv = v_ref[...]
        zb = jnp.zeros_like(vv)
        vs = [jnp.where(sel_a, vv, zb), jnp.where(sel_a, zb, vv)]
        acck_ref[...] = jnp.zeros_like(acck_ref)
        accv_ref[...] = jnp.zeros_like(accv_ref)

        def q_step(i, heads, masked=False):
            lse8 = lse_ref[0, i]
            dr8 = dr_ref[0, i]
            for c0 in range(0, tq, tc):
                start = pl.multiple_of(i * tq + c0, tc)
                dob = do_ref[pl.ds(start, tc), :]
                for hh in heads:
                    qb = qa_ref[hh, pl.ds(start, tc), :]
                    st = lax.dot_general(ka_ref[hh], qb, NT, preferred_element_type=F32)
                    pt = jnp.exp(st - lse8[hh:hh + 1, c0:c0 + tc])
                    if masked:
                        keep = (lax.broadcasted_iota(jnp.int32, (tq, tc), 0)
                                <= lax.broadcasted_iota(jnp.int32, (tq, tc), 1) + c0)
                        pt = jnp.where(keep, pt, 0.0)
                    accv_ref[hh] += jnp.dot(pt.astype(BF16), dob, preferred_element_type=F32)
                    dpt = lax.dot_general(vs[hh], dob, NT, preferred_element_type=F32)
                    dst = pt * (dpt - dr8[hh:hh + 1, c0:c0 + tc])
                    acck_ref[hh] += jnp.dot(dst.astype(BF16), qb, preferred_element_type=F32)

        def needed(hh, i):
            top = fs_ref[2 * pr + hh, jnp.minimum(i, nq - 1)] + bd_ref[0, 0]
            return jnp.logical_and(i < nq, top - fe_ref[2 * pr + hh, j] >= SKIP)

        q_step(j, (0, 1), masked=True)
        _block_loops(j + 1, 1, needed, q_step)
        dka_ref[...] = acck_ref[...]
        dv_ref[...] = jnp.where(sel_a, accv_ref[0], accv_ref[1]).astype(BF16)

    return pl.pallas_call(
        body, name="attn_bwd_dkv", grid=(NH // 2, nq),
        out_shape=(_sds((NH, S, LANES), F32), _sds((S, AW), BF16)),
        in_specs=[SMEM_SPEC, SMEM_SPEC, SMEM_SPEC,
                  pl.BlockSpec((2, tq, LANES), lambda p, j: (p, j, 0)),
                  pl.BlockSpec((tq, LANES), lambda p, j: (j, p)),
                  pl.BlockSpec((2, S, LANES), lambda p, j: (p, 0, 0)),
                  pl.BlockSpec((S, LANES), lambda p, j: (0, p)),
                  pl.BlockSpec((1, nq, 8, tq), lambda p, j: (p, 0, 0, 0)),
                  pl.BlockSpec((1, nq, 8, tq), lambda p, j: (p, 0, 0, 0))],
        out_specs=(pl.BlockSpec((2, tq, LANES), lambda p, j: (p, j, 0)),
                   pl.BlockSpec((tq, LANES), lambda p, j: (j, p))),
        scratch_shapes=[pltpu.VMEM((2, tq, LANES), F32), pltpu.VMEM((2, tq, LANES), F32)],
        compiler_params=_cp(("parallel", "parallel")),
    )(fs, fe, bd, ka, v, qa, do, lser, dr)


def _bwd_conv(du1, u0, alin, agate, w32, tm):
    S = du1.shape[0]
    nt = S // tm

    def body(du_ref, dun_ref, u0_ref, u0p_ref, alin_ref, agate_ref, w_ref,
             dalin_ref, dagate_ref, dw_ref, db_ref, extd_ref, extu_ref, du0_ref):
        i = pl.program_id(0)

        @pl.when(i == 0)
        def _():
            dw_ref[...] = jnp.zeros_like(dw_ref)
            db_ref[...] = jnp.zeros_like(db_ref)

        extd_ref[0:tm, :] = du_ref[...]
        extd_ref[tm:, :] = jnp.where(i == nt - 1, 0.0, dun_ref[...])
        extu_ref[0:HALO, :] = jnp.where(i == 0, 0.0, u0p_ref[...])
        extu_ref[HALO:, :] = u0_ref[...]
        db_ref[...] += _colsum8(du_ref[...])
        for r0 in range(0, tm, CHUNK_ROWS):
            duc = du_ref[r0:r0 + CHUNK_ROWS, :]
            acc = jnp.zeros((CHUNK_ROWS, CW), F32)
            for j in range(KC):
                acc = acc + w_ref[j:j + 1, :] * extd_ref[r0 + 30 - j:r0 + 30 - j + CHUNK_ROWS, :]
                dw_ref[j] += _colsum8(duc * extu_ref[r0 + 2 + j:r0 + 2 + j + CHUNK_ROWS, :])
            du0_ref[r0:r0 + CHUNK_ROWS, :] = acc
        du0 = du0_ref[...]
        al = alin_ref[...].astype(F32)
        sg = _sigmoid(agate_ref[...].astype(F32))
        dalin_ref[...] = (du0 * sg).astype(BF16)
        dagate_ref[...] = (du0 * al * sg * (1.0 - sg)).astype(BF16)

    nxt = pl.BlockSpec((HALO, CW), lambda i: (jnp.minimum((i + 1) * (tm // HALO), S // HALO - 1), 0))
    return pl.pallas_call(
        body, name="bwd_conv", grid=(nt,),
        out_shape=(_sds((S, CW), BF16), _sds((S, CW), BF16), _sds((HALO, 8, CW), F32), _sds((8, CW), F32)),
        in_specs=[_rows(tm, CW), nxt, _rows(tm, CW), _halo_prev(tm), _rows(tm, CW), _rows(tm, CW), _const((HALO, CW))],
        out_specs=(_rows(tm, CW), _rows(tm, CW), _const((HALO, 8, CW)), _const((8, CW))),
        scratch_shapes=[pltpu.VMEM((tm + HALO, CW), F32), pltpu.VMEM((tm + HALO, CW), F32), pltpu.VMEM((tm, CW), F32)],
        compiler_params=_cp(("arbitrary",)),
    )(du1, du1, u0, u0, alin, agate, w32)


def _bwd_qk(dqa, dka, qn, kn, rq, rk, fgb, qg512, kg512, e512, et512, tm):
    S = qn.shape[0]
    nt = S // tm

    def body(dqa_ref, dka_ref, qn_ref, kn_ref, rq_ref, rk_ref, fgb_ref, qg_ref, kg_ref, e_ref, et_ref,
             dq_ref, dk_ref, dfg_ref, accg_ref, accb_ref, carry_ref):
        @pl.when(pl.program_id(0) == 0)
        def _():
            carry_ref[...] = jnp.zeros_like(carry_ref)
            accg_ref[...] = jnp.zeros_like(accg_ref)
            accb_ref[...] = jnp.zeros_like(accb_ref)

        lane = _lane((tm, LANES))
        sel_a = lane < DH
        df = jnp.zeros((tm, LANES), F32)
        for h in range(NH):
            col = dqa_ref[h][:, 64:65] - dka_ref[h][:, 67:68]
            df = jnp.where(lane == h, col, df)
        tri = (lax.broadcasted_iota(jnp.int32, (tm, tm), 0) <= lax.broadcasted_iota(jnp.int32, (tm, tm), 1)
               ).astype(F32).astype(BF16)
        dlf = _dot3(tri, df) + carry_ref[0:1, :]
        carry_ref[...] = jnp.broadcast_to(dlf[0:1, :], carry_ref.shape)
        dfg = jnp.where(lane < NH, dlf * _sigmoid(-fgb_ref[...]), 0.0)
        dfg_ref[...] = dfg.astype(BF16)
        accb_ref[...] += _colsum8(dfg)

        def norm_bwd(src_ref, n_ref, r_ref, g_ref, scale, slot):
            pairs = []
            for p in range(NH // 2):
                b = pltpu.roll(src_ref[2 * p + 1], 64, 1)
                pairs.append(jnp.where(sel_a, src_ref[2 * p], b))
            dh = jnp.concatenate(pairs, axis=1) * scale
            tn = n_ref[...].astype(F32)
            accg_ref[slot] += _colsum8(dh * tn)
            dn = dh * g_ref[...]
            mean = _dot2(dn * tn, e_ref[...]) * (1.0 / DH)
            corr = _dot2(mean, et_ref[...])
            rf = _dot2(r_ref[...], et_ref[...])
            return (rf * (dn - tn * corr)).astype(BF16)

        dq_ref[...] = norm_bwd(dqa_ref, qn_ref, rq_ref, qg_ref, DH ** -0.5, 0)
        dk_ref[...] = norm_bwd(dka_ref, kn_ref, rk_ref, kg_ref, 1.0, 1)

    rev = lambda n: pl.BlockSpec((tm, n), lambda i: (nt - 1 - i, 0))
    hm = pl.BlockSpec((NH, tm, LANES), lambda i: (0, nt - 1 - i, 0))
    dq, dk, dfg, accg, accb = pl.pallas_call(
        body, name="bwd_qk", grid=(nt,),
        out_shape=(_sds((S, AW), BF16), _sds((S, AW), BF16), _sds((S, LANES), BF16), _sds((2, 8, AW), F32),
                   _sds((8, LANES), F32)),
        in_specs=[hm, hm, rev(AW), rev(AW), rev(LANES), rev(LANES), rev(LANES), _const((1, AW)), _const((1, AW)),
                  _const((AW, LANES)), _const((LANES, AW))],
        out_specs=(rev(AW), rev(AW), rev(LANES), _const((2, 8, AW)), _const((8, LANES))),
        scratch_shapes=[pltpu.VMEM((8, LANES), F32)], compiler_params=_cp(("arbitrary",)),
    )(dqa, dka, qn, kn, rq, rk, fgb, qg512, kg512, e512, et512)
    return dq, dk, dfg, accg, accb


def _bwd_in(dq, dk, dv, dalin, dagate, dfg, w_in_p, x, dx2, mod8, n1g, tm):
    S = x.shape[0]

    def body(dq_ref, dk_ref, dv_ref, dal_ref, dag_ref, dfg_ref, w_ref, x_ref, dx2_ref, mod_ref, n1g_ref,
             dx_ref, acc_ref):
        @pl.when(pl.program_id(0) == 0)
        def _():
            acc_ref[...] = jnp.zeros_like(acc_ref)

        def part(ref, a, b):
            return lax.dot_general(ref[...], w_ref[:, a:b], NT, preferred_element_type=F32)

        dh = (part(dq_ref, 0, 512) + part(dk_ref, 512, 1024) + part(dv_ref, 1024, 1536) + part(dal_ref, 1536, 2048)
              + part(dag_ref, 2048, 2560) + part(dfg_ref, 2560, NP))
        xv = x_ref[...]
        r1 = lax.rsqrt(jnp.mean(xv * xv, axis=-1, keepdims=True) + EPS)
        xn = xv * r1
        gain = n1g_ref[...] * (1.0 + mod_ref[1:2, :])
        t = dh * xn
        acc_ref[0] += _colsum8(dh)
        acc_ref[1] += _colsum8(t * n1g_ref[...])
        acc_ref[2] += _colsum8(t * (1.0 + mod_ref[1:2, :]))
        dxn = dh * gain
        dx_ref[...] = dx2_ref[...] + r1 * (dxn - xn * jnp.mean(dxn * xn, axis=-1, keepdims=True))

    return pl.pallas_call(
        body, name="bwd_in", grid=(S // tm,),
        out_shape=(_sds((S, D), F32), _sds((3, 8, D), F32)),
        in_specs=[_rows(tm, AW), _rows(tm, AW), _rows(tm, AW), _rows(tm, CW), _rows(tm, CW), _rows(tm, LANES),
                  _const((D, NP)), _rows(tm, D), _rows(tm, D), _const((8, D)), _const((1, D))],
        out_specs=(_rows(tm, D), _const((3, 8, D))),
        compiler_params=_cp(("arbitrary",)),
    )(dq, dk, dv, dalin, dagate, dfg, w_in_p, x, dx2, mod8, n1g)


def _adam(w, g, m, v):
    m_new = B1 * m + (1.0 - B1) * g
    v_new = B2 * v + (1.0 - B2) * (g * g)
    m_hat = m_new / (1.0 - B1 ** STEP)
    v_hat = v_new / (1.0 - B2 ** STEP)
    delta = -LR * (m_hat / (jnp.sqrt(v_hat) + AEPS) + WD * w)
    return delta, m_new, v_new


def _reduce_adamw(slots, w, m, v, name, tr=256):
    ns, R, C = slots.shape
    tr = tr if R % tr == 0 else R

    def body(s_ref, w_ref, m_ref, v_ref, g_ref, d_ref, mo_ref, vo_ref):
        g = s_ref[0].astype(F32)
        for k in range(1, ns):
            g = g + s_ref[k].astype(F32)
        g_ref[...] = g
        d_ref[...], mo_ref[...], vo_ref[...] = _adam(w_ref[...], g, m_ref[...], v_ref[...])

    blk = pl.BlockSpec((tr, C), lambda i: (i, 0))
    return pl.pallas_call(
        body, name=name, grid=(R // tr,), out_shape=tuple(_sds((R, C), F32) for _ in range(4)),
        in_specs=[pl.BlockSpec((ns, tr, C), lambda i: (0, i, 0)), blk, blk, blk], out_specs=(blk, blk, blk, blk),
        compiler_params=_cp(("parallel",)),
    )(slots, w, m, v)


def _sum_slots(slots, name, tr=256):
    ns, R, C = slots.shape
    tr = tr if R % tr == 0 else R

    def body(s_ref, o_ref):
        g = s_ref[0].astype(F32)
        for k in range(1, ns):
            g = g + s_ref[k].astype(F32)
        o_ref[...] = g

    return pl.pallas_call(
        body, name=name, grid=(R // tr,), out_shape=_sds((R, C), F32),
        in_specs=[pl.BlockSpec((ns, tr, C), lambda i: (0, i, 0))], out_specs=pl.BlockSpec((tr, C), lambda i: (i, 0)),
        compiler_params=_cp(("parallel",)),
    )(slots)


def _ada_adamw(sct, dmod, w, m, v):
    R, C = w.shape
    tr, bc = 256, 512

    def body(sct_ref, dm_ref, w_ref, m_ref, v_ref, g_ref, d_ref, mo_ref, vo_ref):
        g = sct_ref[:, 0:1] * dm_ref[0:1, :]
        for b in range(1, N_DEV):
            g = g + sct_ref[:, b:b + 1] * dm_ref[b:b + 1, :]
        g_ref[...] = g
        d_ref[...], mo_ref[...], vo_ref[...] = _adam(w_ref[...], g, m_ref[...], v_ref[...])

    blk = pl.BlockSpec((tr, bc), lambda i, j: (i, j))
    return pl.pallas_call(
        body, name="ada_adamw", grid=(R // tr, C // bc), out_shape=tuple(_sds((R, C), F32) for _ in range(4)),
        in_specs=[pl.BlockSpec((tr, N_DEV), lambda i, j: (i, 0)), pl.BlockSpec((N_DEV, bc), lambda i, j: (0, j)),
                  blk, blk, blk],
        out_specs=(blk, blk, blk, blk), compiler_params=_cp(("parallel", "parallel")),
    )(sct, dmod, w, m, v)


def _small_reduce(slots, fold):
    def body(s_ref, f_ref, o_ref):
        tot = s_ref[0:1, :]
        for k in range(1, N_DEV):
            tot = tot + s_ref[k:k + 1, :]
        o_ref[:, 0:6144] = tot[:, 0:6144]
        o_ref[:, 6144:7168] = tot[:, 6144:7168]
        for t, src in enumerate((8192, 8704)):
            v8 = jnp.broadcast_to(tot[:, src:src + AW], (8, AW))
            o_ref[:, 7168 + t * LANES:7168 + (t + 1) * LANES] = jnp.dot(
                v8, f_ref[...], precision=HI, preferred_element_type=F32)[0:1, :]
        o_ref[:, 7424:7552] = tot[:, 9216:9344]
        o_ref[:, 7552:10112] = tot[:, 9344:11904]
        o_ref[:, 10112:SMALL_OUT] = tot[:, 7168:8192]

    return pl.pallas_call(
        body, name="small_reduce", out_shape=_sds((1, SMALL_OUT), F32),
        in_specs=[pl.BlockSpec(memory_space=pltpu.VMEM), pl.BlockSpec(memory_space=pltpu.VMEM)],
        out_specs=pl.BlockSpec(memory_space=pltpu.VMEM),
    )(slots, fold)


def _perm_in(w):
    pad = jnp.zeros((w.shape[0], NP - 2568), w.dtype)
    return jnp.concatenate([w[:, :1536], w[:, 1544:2568], w[:, 1536:1544], pad], axis=1)


def _pad_lanes(vec, n=LANES):
    return jnp.pad(vec, ((0, 0), (0, n - vec.shape[1])))


def kernel(x, c, w_ada, b_ada, norm1_g, w_in, q_norm_g, k_norm_g, b_f, conv_w, conv_b, conv_ln_g, conv_ln_b, beta_attn, beta_conv, w_out, norm2_g, w_ff1, w_ff2, loss_target, m_w_ada, m_b_ada, m_norm1_g, m_w_in, m_q_norm_g, m_k_norm_g, m_b_f, m_conv_w, m_conv_b, m_conv_ln_g, m_conv_ln_b, m_beta_attn, m_beta_conv, m_w_out, m_norm2_g, m_w_ff1, m_w_ff2, v_w_ada, v_b_ada, v_norm1_g, v_w_in, v_q_norm_g, v_k_norm_g, v_b_f, v_conv_w, v_conv_b, v_conv_ln_g, v_conv_ln_b, v_beta_attn, v_beta_conv, v_w_out, v_norm2_g, v_w_ff1, v_w_ff2):
    S = x.shape[1]
    tm = min(256, S)
    tq = min(512, S // 2)
    xs, tgt = x[0], loss_target[0]
    chip = 2 * lax.axis_index("x") + lax.axis_index("y")
    e512, et512 = _head_sum_mats()

    conv_w32 = jnp.pad(conv_w[0], ((0, 1), (0, 0)))
    c_all, g_in, g_out, g_ff1, g_ff2, g_cw = _exchange(
        [(c, "bcast8"), (w_in[0].astype(BF16), "chip4"), (w_out[0].astype(BF16), "chip4"),
         (w_ff1[0].astype(BF16), "chip4"), (w_ff2[0].astype(BF16), "chip4"), (conv_w32, "chip4")], "gather_weights")
    c_all = c_all.reshape(N_DEV, D)
    w_in_p = _perm_in(jnp.transpose(g_in, (1, 0, 2)).reshape(D, 2568))
    w_out_f = g_out.reshape(D, D)
    w1, w2 = g_ff1, g_ff2
    cw32 = jnp.transpose(g_cw, (1, 0, 2)).reshape(HALO, CW)

    b_shard = lax.dynamic_slice(b_ada, (0, chip * 1536), (1, 1536))
    mod_rows, sc_all = _mod_shard(c_all, w_ada[0], b_shard)
    (mod_slots,) = _exchange([(mod_rows.reshape(N_DEV, 1, 1536), "all8")], "scatter_mod")
    mod = mod_slots.reshape(4, 2, 1536)[:, 0, :].reshape(6, D)
    mod8 = jnp.pad(mod, ((0, 2), (0, 0)))

    qg512 = jnp.tile(q_norm_g, (1, NH))
    kg512 = jnp.tile(k_norm_g, (1, NH))
    bf128 = _pad_lanes(b_f)

    h1, qh, kh, vb, qn, kn, rq, rk, fgb, alin, agate, u0 = _fwd_in(xs, mod8, norm1_g, w_in_p, e512, et512, qg512, kg512,
                                                                   bf128, tm)
    qa, ka, fcum = _fwd_decay(fgb, qh, kh, tm)
    fs, fe, bd = _skip_tables(fcum, q_norm_g, k_norm_g, tq)
    o_attn, lsec, lser = _attn_fwd(qa, ka, vb, fs, fe, bd, tq)
    u1, mc = _fwd_conv(u0, cw32, conv_b, conv_ln_g, conv_ln_b, beta_conv, tm)
    merged, ob, x2, h2 = _fwd_out(o_attn, mc, xs, mod8, norm2_g, beta_attn, w_out_f, tm)
    tf = min(512, S)
    r, dy, loss8, dg2 = _fwd_ffn(h2, w1, w2, x2, tgt, mod8, tf)

    df2, df1, dh2 = _bwd_ffn(dy, mod8, r, w1, w2, tf)
    gw_ff2 = _wgrad(r, df2, "wgrad_ff2", square_a=True)
    gw_ff1 = _wgrad(h2, df1, "wgrad_ff1")
    dx2, do, doa, du1, acc_d, acc_h = _bwd_mid(dh2, dy, x2, ob, o_attn, u1, mod8, norm2_g, beta_attn, beta_conv,
                                               conv_ln_g, conv_ln_b, w_out_f, tm)
    gw_out = _wgrad(merged, do, "wgrad_out")
    dqa, dr = _attn_bwd_dq(qa, ka, vb, doa, o_attn, lsec, fs, fe, bd, tq)
    dka, dv = _attn_bwd_dkv(qa, ka, vb, doa, lser, dr, fs, fe, bd, tq)
    dalin, dagate, dcw, dcb = _bwd_conv(du1, u0, alin, agate, cw32, tm)
    dq, dk, dfg, accg, accb = _bwd_qk(dqa, dka, qn, kn, rq, rk, fgb, qg512, kg512, e512, et512, tm)
    grad_x, acc1 = _bwd_in(dq, dk, dv, dalin, dagate, dfg, w_in_p, xs, dx2, mod8, norm1_g, tm)
    gw_in = jnp.concatenate(
        [_wgrad(h1, dq, "wgrad_in_q"), _wgrad(h1, dk, "wgrad_in_k"), _wgrad(h1, dv, "wgrad_in_v"),
         _wgrad(h1, dfg, "wgrad_in_f")[:, :NH], _wgrad(h1, dalin, "wgrad_in_a"), _wgrad(h1, dagate, "wgrad_in_g")],
        axis=1)

    s8 = lambda a: jnp.sum(a, axis=-2)
    a1, ad, ah = s8(acc1), s8(acc_d), s8(acc_h)
    small = jnp.concatenate(
        [a1[0], a1[1], ad[3], ad[0], ad[1], s8(dg2),
         a1[2], ad[2], s8(accg).reshape(-1), s8(accb), s8(dcb), ah[2], ah[3], ah[0], ah[1]]).reshape(1, SMALL_IN)
    gcw = s8(dcw)[:KC]
    small_s, p_in, p_out, p_ff1, p_ff2, p_cw = _exchange(
        [(small, "bcast8"),
         (jnp.transpose(gw_in.reshape(D, 4, 642), (1, 0, 2)).astype(BF16), "chip4p"),
         (gw_out.reshape(4, 256, D).astype(BF16), "chip4p"),
         (jnp.transpose(gw_ff1.reshape(D, 4, D), (1, 0, 2)).astype(BF16), "chip4p"),
         (gw_ff2.reshape(4, D, D).astype(BF16), "chip4p"),
         (jnp.transpose(gcw.reshape(KC, 4, LANES), (1, 0, 2)), "chip4p")], "scatter_grads")
    small_s = small_s.reshape(N_DEV, SMALL_IN)
    s_in, s_out, s_ff1, s_ff2, s_cw = _exchange(
        [(_sum_slots(p_in, "chipsum_in"), "pair"), (_sum_slots(p_out, "chipsum_out"), "pair"),
         (_sum_slots(p_ff1, "chipsum_ff1"), "pair"), (_sum_slots(p_ff2, "chipsum_ff2"), "pair"),
         (_sum_slots(p_cw, "chipsum_conv_w"), "pair")], "pair_grads", chunks=4)

    g_in_, d_in, nm_in, nv_in = _reduce_adamw(s_in, w_in[0], m_w_in[0], v_w_in[0], "adamw_in")
    g_out_, d_out, nm_out, nv_out = _reduce_adamw(s_out, w_out[0], m_w_out[0], v_w_out[0], "adamw_out")
    g_f1, d_f1, nm_f1, nv_f1 = _reduce_adamw(s_ff1, w_ff1[0], m_w_ff1[0], v_w_ff1[0], "adamw_ff1")
    g_f2, d_f2, nm_f2, nv_f2 = _reduce_adamw(s_ff2, w_ff2[0], m_w_ff2[0], v_w_ff2[0], "adamw_ff2")
    g_cw_, d_cw, nm_cw, nv_cw = _reduce_adamw(s_cw, conv_w[0], m_conv_w[0], v_conv_w[0], "adamw_conv_w")
    dmod_shard = lax.dynamic_slice(small_s[:, :6 * D], (0, chip * 1536), (N_DEV, 1536))
    g_ada, d_ada, nm_ada, nv_ada = _ada_adamw(sc_all.T, dmod_shard, w_ada[0], m_w_ada[0], v_w_ada[0])

    fold = np.zeros((AW, LANES), np.float32)
    fold[np.arange(AW), np.arange(AW) % DH] = 1.0
    g_small = _small_reduce(small_s, jnp.asarray(fold))
    smalls = [b_ada, norm1_g, q_norm_g, k_norm_g, b_f, conv_b, conv_ln_g, conv_ln_b, beta_attn, beta_conv, norm2_g]
    m_smalls = [m_b_ada, m_norm1_g, m_q_norm_g, m_k_norm_g, m_b_f, m_conv_b, m_conv_ln_g, m_conv_ln_b, m_beta_attn,
                m_beta_conv, m_norm2_g]
    v_smalls = [v_b_ada, v_norm1_g, v_q_norm_g, v_k_norm_g, v_b_f, v_conv_b, v_conv_ln_g, v_conv_ln_b, v_beta_attn,
                v_beta_conv, v_norm2_g]
    widths = [a.shape[1] for a in smalls]
    padded = [-(-n // LANES) * LANES for n in widths]
    pack = lambda arrs, fill: jnp.concatenate(
        [jnp.pad(a, ((0, 0), (0, p - a.shape[1])), constant_values=fill) for a, p in zip(arrs, padded)], axis=1)
    outs_small = _reduce_adamw(g_small.reshape(1, 1, SMALL_OUT), pack(smalls, 0.0), pack(m_smalls, 0.0),
                               pack(v_smalls, 1.0), "adamw_small")
    offs = np.concatenate([[0], np.cumsum(padded)])

    def unpack(a):
        return [a[:, int(o):int(o) + n] for o, n in zip(offs[:-1], widths)]

    gs, ds, ms, vs = (unpack(a) for a in outs_small)

    loss = lax.psum(loss8[0, 0], ("x", "y", "c"))
    big = {"w_ada": (g_ada, d_ada, nm_ada, nv_ada), "w_in": (g_in_, d_in, nm_in, nv_in),
           "conv_w": (g_cw_, d_cw, nm_cw, nv_cw), "w_out": (g_out_, d_out, nm_out, nv_out),
           "w_ff1": (g_f1, d_f1, nm_f1, nv_f1), "w_ff2": (g_f2, d_f2, nm_f2, nv_f2)}
    small_names = ["b_ada", "norm1_g", "q_norm_g", "k_norm_g", "b_f", "conv_b", "conv_ln_g", "conv_ln_b", "beta_attn",
                   "beta_conv", "norm2_g"]
    order = ["w_ada", "b_ada", "norm1_g", "w_in", "q_norm_g", "k_norm_g", "b_f", "conv_w", "conv_b", "conv_ln_g",
             "conv_ln_b", "beta_attn", "beta_conv", "w_out", "norm2_g", "w_ff1", "w_ff2"]

    def leaf(name, which):
        if name in big:
            return big[name][which][None]
        return (gs, ds, ms, vs)[which][small_names.index(name)]

    return (loss, grad_x[None], *[leaf(n, 0) for n in order], *[leaf(n, 1) for n in order],
            *[leaf(n, 2) for n in order], *[leaf(n, 3) for n in order])
```

```python
import functools

import numpy as np
import jax
import jax.numpy as jnp
from jax import lax
from jax.experimental import pallas as pl
from jax.experimental.pallas import tpu as pltpu

F32, BF16 = jnp.float32, jnp.bfloat16
HI = lax.Precision.HIGHEST
D = 1024
AW = 512
CW = 512
NH = 8
DH = 64
KC = 31
DFF = 4096
NP = 2688
EPS = 1e-6
NEG = -1e30
LANES = 128
VMEM_LIMIT = 56 * 2**20
NT = (((1,), (1,)), ((), ()))
TN = (((0,), (0,)), ((), ()))
LR, B1, B2, AEPS, WD, STEP = 0.001, 0.9, 0.999, 1e-08, 0.01, 10
N_DEV = 8
SMALL_IN = 11904
SMALL_OUT = 11136


def _cp(sem=None, vmem=VMEM_LIMIT):
    kw = dict(vmem_limit_bytes=vmem)
    if sem is not None:
        kw["dimension_semantics"] = sem
    return pltpu.CompilerParams(**kw)


def _rows(tm, n):
    return pl.BlockSpec((tm, n), lambda i: (i, 0))


def _const(shape):
    nd = len(shape)
    return pl.BlockSpec(shape, lambda *_: (0,) * nd)


def _sds(shape, dt):
    return jax.ShapeDtypeStruct(shape, dt)


def _lane(shape):
    return lax.broadcasted_iota(jnp.int32, shape, len(shape) - 1)


def _sigmoid(x):
    return 1.0 / (1.0 + jnp.exp(-x))


def _exchange(items, name, chunks=1):
    kinds = [k for _, k in items]
    srcs = [s for s, _ in items]
    out_shapes = []
    for s, k in items:
        if k in ("all8", "chip8"):
            out_shapes.append(_sds((N_DEV,) + s.shape[1:], s.dtype))
        elif k == "bcast8":
            out_shapes.append(_sds((N_DEV,) + s.shape, s.dtype))
        elif k == "chip4":
            out_shapes.append(_sds((4,) + s.shape, s.dtype))
        elif k == "chip4p":
            out_shapes.append(_sds((4,) + s.shape[1:], s.dtype))
        else:
            out_shapes.append(_sds((2,) + s.shape, s.dtype))
    n = len(items)
    all_masks = {"chip4": [2, 4, 6], "chip4p": [2, 4, 6], "pair": [1]}
    masks = {k: all_masks.get(k, [1, 2, 3, 4, 5, 6, 7]) for k in set(kinds)}
    piece_rows = [o.shape[1] for o in out_shapes]
    n_chunks = [chunks if r % (8 * chunks) == 0 else 1 for r in piece_rows]
    sem_index = {}
    for t, k in enumerate(kinds):
        for m in masks[k]:
            for q in range(n_chunks[t]):
                sem_index[(t, m, q)] = len(sem_index)
    n_sem = len(sem_index)

    def body(*refs):
        src_refs, dst_refs = refs[:n], refs[n:2 * n]
        send_sems, recv_sems, local_sems = refs[2 * n:]
        x, y, c = lax.axis_index("x"), lax.axis_index("y"), lax.axis_index("c")
        my_id = 4 * x + 2 * y + c
        my_chip = 2 * x + y

        def piece(t, dev_id, chip):
            k = kinds[t]
            if k == "all8":
                return src_refs[t].at[dev_id]
            if k in ("chip8", "chip4p"):
                return src_refs[t].at[chip]
            return src_refs[t]

        def slot(t):
            k = kinds[t]
            return dst_refs[t].at[my_chip if k in ("chip4", "chip4p") else c if k == "pair" else my_id]

        copies = []
        for t in range(n):
            lc = pltpu.make_async_copy(piece(t, my_id, my_chip), slot(t), local_sems.at[t])
            lc.start()
            copies.append(lc)
            for m in masks[kinds[t]]:
                px = 1 - x if m & 4 else x
                py = 1 - y if m & 2 else y
                pc = 1 - c if m & 1 else c
                rows = piece_rows[t] // n_chunks[t]
                for q in range(n_chunks[t]):
                    s = sem_index[(t, m, q)]
                    rc = pltpu.make_async_remote_copy(
                        src_ref=piece(t, 4 * px + 2 * py + pc, 2 * px + py).at[pl.ds(q * rows, rows)],
                        dst_ref=slot(t).at[pl.ds(q * rows, rows)],
                        send_sem=send_sems.at[s], recv_sem=recv_sems.at[s],
                        device_id=(px, py, pc), device_id_type=pl.DeviceIdType.MESH)
                    rc.start()
                    copies.append(rc)
        for cp in copies:
            cp.wait()

    outs = pl.pallas_call(
        body, name=name, out_shape=tuple(out_shapes),
        in_specs=[pl.BlockSpec(memory_space=pl.ANY)] * n,
        out_specs=tuple(pl.BlockSpec(memory_space=pl.ANY) for _ in range(n)),
        scratch_shapes=[pltpu.SemaphoreType.DMA((n_sem,)), pltpu.SemaphoreType.DMA((n_sem,)),
                        pltpu.SemaphoreType.DMA((n,))],
    )(*srcs)
    return list(outs)


def _mod_shard(c_all, w_ada, b_shard):
    n = w_ada.shape[1]

    def body(c_ref, w_ref, b_ref, o_ref, sc_ref):
        cv = c_ref[...]
        sc = cv * _sigmoid(cv)
        sc_ref[...] = sc
        o_ref[...] = jnp.dot(sc, w_ref[...], precision=HI, preferred_element_type=F32) + b_ref[...]

    bn = 512
    return pl.pallas_call(
        body, name="mod_shard", out_shape=(_sds((N_DEV, n), F32), _sds((N_DEV, D), F32)), grid=(n // bn,),
        in_specs=[_const((N_DEV, D)), pl.BlockSpec((D, bn), lambda j: (0, j)), pl.BlockSpec((1, bn), lambda j: (0, j))],
        out_specs=(pl.BlockSpec((N_DEV, bn), lambda j: (0, j)), _const((N_DEV, D))),
        compiler_params=_cp(("arbitrary",)),
    )(c_all, w_ada, b_shard)


def _head_sum_mats():
    e = np.zeros((AW, LANES), np.float32)
    for h in range(NH):
        e[h * DH:(h + 1) * DH, h] = 1.0
    return jnp.asarray(e, BF16), jnp.asarray(e.T.copy(), BF16)


def _dot2(x, w):
    hi = x.astype(BF16)
    lo = (x - hi.astype(F32)).astype(BF16)
    return jnp.dot(hi, w, preferred_element_type=F32) + jnp.dot(lo, w, preferred_element_type=F32)


def _fwd_in(x, mod8, n1g, w_in_p, e512, et512, qg512, kg512, bf128, tm):
    S = x.shape[0]

    def body(x_ref, mod_ref, n1g_ref, w_ref, e_ref, et_ref, qg_ref, kg_ref, bf_ref,
             h1_ref, qh_ref, kh_ref, v_ref, qn_ref, kn_ref, rq_ref, rk_ref, fgb_ref, alin_ref, agate_ref, u0_ref):
        xv = x_ref[...]
        r1 = lax.rsqrt(jnp.mean(xv * xv, axis=-1, keepdims=True) + EPS)
        h = (xv * r1) * (n1g_ref[...] * (1.0 + mod_ref[1:2, :])) + mod_ref[0:1, :]
        hb = h.astype(BF16)
        h1_ref[...] = hb

        def seg(a, b):
            return jnp.dot(hb, w_ref[:, a:b], preferred_element_type=F32)

        def headnorm(t, g_ref, scale, n_ref, r_ref, o_ref):
            ss = _dot2(t * t, e_ref[...])
            r = lax.rsqrt(ss * (1.0 / DH) + EPS)
            tn = t * _dot2(r, et_ref[...])
            n_ref[...] = tn.astype(BF16)
            r_ref[...] = r
            o_ref[...] = (tn * (g_ref[...] * scale)).astype(BF16)

        headnorm(seg(0, 512), qg_ref, DH ** -0.5, qn_ref, rq_ref, qh_ref)
        headnorm(seg(512, 1024), kg_ref, 1.0, kn_ref, rk_ref, kh_ref)
        v_ref[...] = seg(1024, 1536).astype(BF16)
        alin = seg(1536, 2048)
        agate = seg(2048, 2560)
        alin_ref[...] = alin.astype(BF16)
        agate_ref[...] = agate.astype(BF16)
        u0_ref[...] = alin * _sigmoid(agate)
        fgb_ref[...] = seg(2560, NP) + bf_ref[...]

    bf = lambda: _sds((S, AW), BF16)
    return pl.pallas_call(
        body, name="fwd_in", grid=(S // tm,),
        out_shape=(_sds((S, D), BF16), bf(), bf(), bf(), bf(), bf(), _sds((S, LANES), F32), _sds((S, LANES), F32),
                   _sds((S, LANES), F32), bf(), bf(), _sds((S, CW), F32)),
        in_specs=[_rows(tm, D), _const((8, D)), _const((1, D)), _const((D, NP)), _const((AW, LANES)), _const((LANES, AW)),
                  _const((1, AW)), _const((1, AW)), _const((1, LANES))],
        out_specs=(_rows(tm, D), _rows(tm, AW), _rows(tm, AW), _rows(tm, AW), _rows(tm, AW), _rows(tm, AW),
                   _rows(tm, LANES), _rows(tm, LANES), _rows(tm, LANES), _rows(tm, AW), _rows(tm, AW), _rows(tm, CW)),
        compiler_params=_cp(("parallel",)),
    )(x, mod8, n1g, w_in_p, e512, et512, qg512, kg512, bf128)


def _split3(f):
    f1 = f.astype(BF16).astype(F32)
    f2 = (f - f1).astype(BF16).astype(F32)
    return f1, f2, f - f1 - f2


def _dot3(w, x):
    return sum(jnp.dot(w, piece.astype(BF16), preferred_element_type=F32) for piece in _split3(x))


def _fwd_decay(fgb, qh, kh, tm):
    S = fgb.shape[0]

    def body(fgb_ref, qh_ref, kh_ref, qa_ref, ka_ref, f_ref, carry_ref):
        @pl.when(pl.program_id(0) == 0)
        def _():
            carry_ref[...] = jnp.zeros_like(carry_ref)

        fb = fgb_ref[...]
        lf = jnp.minimum(fb, 0.0) - jnp.log1p(jnp.exp(-jnp.abs(fb)))
        tri = (lax.broadcasted_iota(jnp.int32, (tm, tm), 0) >= lax.broadcasted_iota(jnp.int32, (tm, tm), 1)
               ).astype(F32).astype(BF16)
        cs = _dot3(tri, lf) + carry_ref[0:1, :]
        f_ref[...] = cs
        carry_ref[...] = jnp.broadcast_to(cs[tm - 1:tm, :], carry_ref.shape)
        lane = _lane((tm, LANES))
        ones = jnp.where((lane >= 67) & (lane < 70), 1.0, 0.0)
        ones_k = jnp.where((lane >= 64) & (lane < 67), 1.0, 0.0)
        for p in range(NH // 2):
            qp = qh_ref[:, p * LANES:(p + 1) * LANES].astype(F32)
            kp = kh_ref[:, p * LANES:(p + 1) * LANES].astype(F32)
            for hh in range(2):
                h = 2 * p + hh
                f1, f2, f3 = _split3(cs[:, h:h + 1])
                qb = qp if hh == 0 else pltpu.roll(qp, 64, 1)
                kb = kp if hh == 0 else pltpu.roll(kp, 64, 1)
                augq = jnp.where(lane == 64, f1, jnp.where(lane == 65, f2, jnp.where(lane == 66, f3, ones)))
                augk = jnp.where(lane == 67, -f1, jnp.where(lane == 68, -f2, jnp.where(lane == 69, -f3, ones_k)))
                qa_ref[h] = jnp.where(lane < DH, qb, augq).astype(BF16)
                ka_ref[h] = jnp.where(lane < DH, kb, augk).astype(BF16)

    hm = pl.BlockSpec((NH, tm, LANES), lambda i: (0, i, 0))
    return pl.pallas_call(
        body, name="fwd_decay", grid=(S // tm,),
        out_shape=(_sds((NH, S, LANES), BF16), _sds((NH, S, LANES), BF16), _sds((S, LANES), F32)),
        in_specs=[_rows(tm, LANES), _rows(tm, AW), _rows(tm, AW)], out_specs=(hm, hm, _rows(tm, LANES)),
        scratch_shapes=[pltpu.VMEM((8, LANES), F32)], compiler_params=_cp(("arbitrary",)),
    )(fgb, qh, kh)


def _causal(t):
    return lax.broadcasted_iota(jnp.int32, (t, t), 0) >= lax.broadcasted_iota(jnp.int32, (t, t), 1)


def _causal_t(t):
    return lax.broadcasted_iota(jnp.int32, (t, t), 0) <= lax.broadcasted_iota(jnp.int32, (t, t), 1)


SKIP = -106.0


def _block_loops(first, step, needed, run):
    def both(j):
        return jnp.logical_and(needed(0, j), needed(1, j))

    def walk(heads):
        def go(j):
            run(j, heads)
            return j + step
        return go

    j = lax.while_loop(both, walk((0, 1)), first)
    lax.while_loop(functools.partial(needed, 0), walk((0,)), j)
    lax.while_loop(functools.partial(needed, 1), walk((1,)), j)


def _skip_tables(f, qg, kg, tq):
    fs = f[0::tq, :NH].T
    fe = f[tq - 1::tq, :NH].T
    bound = 2.0 * 1.03 * DH ** 0.5 * jnp.max(jnp.abs(qg)) * jnp.max(jnp.abs(kg))
    return fs, fe, bound.reshape(1, 1)


SMEM_SPEC = pl.BlockSpec(memory_space=pltpu.SMEM)


def _causal_rect(rows, cols, col0):
    return (lax.broadcasted_iota(jnp.int32, (rows, cols), 0)
            >= lax.broadcasted_iota(jnp.int32, (rows, cols), 1) + col0)


def _chunk(tq):
    return tq


def _attn_fwd(qa, ka, v, fs, fe, bd, tq):
    S = qa.shape[1]
    nq = S // tq
    tc = _chunk(tq)

    def body(fs_ref, fe_ref, bd_ref, qa_ref, ka_ref, v_ref, o_ref, lsec_ref, lser_ref, m_ref, l_ref, acc_ref):
        pr, i = pl.program_id(0), pl.program_id(1)
        sel_a = _lane((tq, LANES)) < DH
        m_ref[...] = jnp.full(m_ref.shape, NEG, F32)
        l_ref[...] = jnp.zeros_like(l_ref)
        acc_ref[...] = jnp.zeros_like(acc_ref)

        def kv_step(j, heads, masked=False):
            for c0 in range(0, tq, tc):
                start = pl.multiple_of(j * tq + c0, tc)
                vb = v_ref[pl.ds(start, tc), :]
                for hh in heads:
                    s = lax.dot_general(qa_ref[hh], ka_ref[hh, pl.ds(start, tc), :], NT, preferred_element_type=F32)
                    if masked:
                        s = jnp.where(_causal_rect(tq, tc, c0), s, NEG)
                    m_prev = m_ref[hh]
                    m_new = jnp.maximum(m_prev, jnp.max(s, axis=1, keepdims=True))
                    alpha = jnp.exp(m_prev - m_new)
                    p = jnp.exp(s - jnp.tile(m_new, (1, tc // LANES)))
                    l_ref[hh] = alpha * l_ref[hh] + jnp.sum(p, axis=1, keepdims=True)
                    m_ref[hh] = m_new
                    acc_ref[hh] = alpha * acc_ref[hh] + jnp.dot(p.astype(BF16), vb, preferred_element_type=F32)

        def needed(hh, j):
            top = fs_ref[2 * pr + hh, i] + bd_ref[0, 0]
            return jnp.logical_and(j >= 0, top - fe_ref[2 * pr + hh, jnp.maximum(j, 0)] >= SKIP)

        kv_step(i, (0, 1), masked=True)
        _block_loops(i - 1, -1, needed, kv_step)
        o_ref[...] = jnp.where(sel_a, acc_ref[0] / l_ref[0], acc_ref[1] / l_ref[1])
        lse_a = m_ref[0] + jnp.log(l_ref[0])
        lse_b = m_ref[1] + jnp.log(l_ref[1])
        lsec_ref[0] = lse_a
        lsec_ref[1] = lse_b
        row = lax.broadcasted_iota(jnp.int32, (8, tq), 0)
        lser_ref[0, 0] = jnp.where(row == 0, lse_a.T[0:8, :], lse_b.T[0:8, :])

    return pl.pallas_call(
        body, name="attn_fwd", grid=(NH // 2, nq),
        out_shape=(_sds((S, AW), F32), _sds((NH, S, LANES), F32), _sds((NH // 2, nq, 8, tq), F32)),
        in_specs=[SMEM_SPEC, SMEM_SPEC, SMEM_SPEC,
                  pl.BlockSpec((2, tq, LANES), lambda p, i: (p, i, 0)),
                  pl.BlockSpec((2, S, LANES), lambda p, i: (p, 0, 0)),
                  pl.BlockSpec((S, LANES), lambda p, i: (0, p))],
        out_specs=(pl.BlockSpec((tq, LANES), lambda p, i: (i, p)),
                   pl.BlockSpec((2, tq, LANES), lambda p, i: (p, i, 0)),
                   pl.BlockSpec((1, 1, 8, tq), lambda p, i: (p, i, 0, 0))),
        scratch_shapes=[pltpu.VMEM((2, tq, LANES), F32), pltpu.VMEM((2, tq, LANES), F32),
                        pltpu.VMEM((2, tq, LANES), F32)],
        compiler_params=_cp(("parallel", "parallel")),
    )(fs, fe, bd, qa, ka, v)


HALO = 32
CHUNK_ROWS = 64


def _halo_prev(tm):
    return pl.BlockSpec((HALO, CW), lambda i: (jnp.maximum(i * (tm // HALO) - 1, 0), 0))


def _fwd_conv(u0, w32, cb, lng, lnb, beta_c, tm):
    S = u0.shape[0]

    def body(cur_ref, prev_ref, w_ref, cb_ref, lng_ref, lnb_ref, beta_ref, u1_ref, mc_ref, ext_ref):
        i = pl.program_id(0)
        ext_ref[0:HALO, :] = jnp.where(i == 0, 0.0, prev_ref[...])
        ext_ref[HALO:, :] = cur_ref[...]
        for r0 in range(0, tm, CHUNK_ROWS):
            acc = jnp.zeros((CHUNK_ROWS, CW), F32) + cb_ref[...]
            for j in range(KC):
                acc = acc + w_ref[j:j + 1, :] * ext_ref[r0 + 2 + j:r0 + 2 + j + CHUNK_ROWS, :]
            u1_ref[r0:r0 + CHUNK_ROWS, :] = acc
        u1 = u1_ref[...]
        mu = jnp.mean(u1, axis=-1, keepdims=True)
        d = u1 - mu
        rstd = lax.rsqrt(jnp.mean(d * d, axis=-1, keepdims=True) + EPS)
        u2 = d * rstd * lng_ref[...] + lnb_ref[...]
        u3 = u2 * _sigmoid(u2)
        rc = lax.rsqrt(jnp.mean(u3 * u3, axis=-1, keepdims=True) + EPS)
        mc_ref[...] = (u3 * rc * beta_ref[...]).astype(BF16)

    return pl.pallas_call(
        body, name="fwd_conv", grid=(S // tm,),
        out_shape=(_sds((S, CW), F32), _sds((S, CW), BF16)),
        in_specs=[_rows(tm, CW), _halo_prev(tm), _const((HALO, CW)), _const((1, CW)), _const((1, CW)), _const((1, CW)),
                  _const((1, CW))],
        out_specs=(_rows(tm, CW), _rows(tm, CW)),
        scratch_shapes=[pltpu.VMEM((tm + HALO, CW), F32)], compiler_params=_cp(("parallel",)),
    )(u0, u0, w32, cb, lng, lnb, beta_c)


def _fwd_out(o_attn, mc, x, mod8, n2g, beta_a, w_out, tm):
    S = x.shape[0]

    def body(o_ref, mc_ref, x_ref, mod_ref, n2g_ref, beta_ref, w_ref, mg_ref, ob_ref, x2_ref, h2_ref):
        ov = o_ref[...]
        ra = lax.rsqrt(jnp.mean(ov * ov, axis=-1, keepdims=True) + EPS)
        ma = (ov * ra * beta_ref[...]).astype(BF16)
        mcv = mc_ref[...]
        mg_ref[:, 0:AW] = ma
        mg_ref[:, AW:D] = mcv
        o = (jnp.dot(ma, w_ref[0:AW, :], preferred_element_type=F32)
             + jnp.dot(mcv, w_ref[AW:D, :], preferred_element_type=F32))
        ob_ref[...] = o.astype(BF16)
        x2 = x_ref[...] + mod_ref[2:3, :] * o
        x2_ref[...] = x2
        r2 = lax.rsqrt(jnp.mean(x2 * x2, axis=-1, keepdims=True) + EPS)
        h2_ref[...] = ((x2 * r2) * (n2g_ref[...] * (1.0 + mod_ref[4:5, :])) + mod_ref[3:4, :]).astype(BF16)

    return pl.pallas_call(
        body, name="fwd_out", grid=(S // tm,),
        out_shape=(_sds((S, D), BF16), _sds((S, D), BF16), _sds((S, D), F32), _sds((S, D), BF16)),
        in_specs=[_rows(tm, AW), _rows(tm, CW), _rows(tm, D), _const((8, D)), _const((1, D)), _const((1, AW)),
                  _const((D, D))],
        out_specs=(_rows(tm, D), _rows(tm, D), _rows(tm, D), _rows(tm, D)),
        compiler_params=_cp(("parallel",)),
    )(o_attn, mc, x, mod8, n2g, beta_a, w_out)


def _fwd_ffn(h2, w1, w2, x2, tgt, mod8, tm):
    S = h2.shape[0]
    nk = w1.shape[0]
    bf = w1.shape[2]

    def body(h2_ref, w1_ref, w2_ref, x2_ref, tgt_ref, mod_ref, r_ref, dy_ref, loss_ref, dg2_ref, acc_ref):
        i, k = pl.program_id(0), pl.program_id(1)

        @pl.when((i == 0) & (k == 0))
        def _():
            loss_ref[...] = jnp.zeros_like(loss_ref)
            dg2_ref[...] = jnp.zeros_like(dg2_ref)

        r = jnp.maximum(jnp.dot(h2_ref[...], w1_ref[0], preferred_element_type=F32), 0.0)
        r_ref[...] = r.astype(BF16)
        part = jnp.dot((r * r).astype(BF16), w2_ref[0], preferred_element_type=F32)

        @pl.when(k == 0)
        def _():
            acc_ref[...] = part

        @pl.when(k > 0)
        def _():
            acc_ref[...] += part

        @pl.when(k == nk - 1)
        def _():
            f2 = acc_ref[...]
            e = x2_ref[...] + mod_ref[5:6, :] * f2 - tgt_ref[...]
            dy = e * (1.0 / D)
            dy_ref[...] = dy
            loss_ref[...] += 0.5 * jnp.sum(jnp.sum(e * dy, axis=1, keepdims=True), axis=0, keepdims=True)
            dg2_ref[...] += jnp.sum((dy * f2).reshape(tm // 8, 8, D), axis=0)

    return pl.pallas_call(
        body, name="fwd_ffn", grid=(S // tm, nk),
        out_shape=(_sds((S, DFF), BF16), _sds((S, D), F32), _sds((8, LANES), F32), _sds((8, D), F32)),
        in_specs=[pl.BlockSpec((tm, D), lambda i, k: (i, 0)), pl.BlockSpec((1, D, bf), lambda i, k: (k, 0, 0)),
                  pl.BlockSpec((1, bf, D), lambda i, k: (k, 0, 0)), pl.BlockSpec((tm, D), lambda i, k: (i, 0)),
                  pl.BlockSpec((tm, D), lambda i, k: (i, 0)), pl.BlockSpec((8, D), lambda i, k: (0, 0))],
        out_specs=(pl.BlockSpec((tm, bf), lambda i, k: (i, k)), pl.BlockSpec((tm, D), lambda i, k: (i, 0)),
                   pl.BlockSpec((8, LANES), lambda i, k: (0, 0)), pl.BlockSpec((8, D), lambda i, k: (0, 0))),
        scratch_shapes=[pltpu.VMEM((tm, D), F32)], compiler_params=_cp(("arbitrary", "arbitrary")),
    )(h2, w1, w2, x2, tgt, mod8)


def _bwd_ffn(dy, mod8, r, w1, w2, tm):
    S = dy.shape[0]
    nk = w1.shape[0]
    bf = w1.shape[2]

    def body(dy_ref, mod_ref, r_ref, w1_ref, w2_ref, df2_ref, df1_ref, dh2_ref):
        k = pl.program_id(1)
        df2 = (dy_ref[...] * mod_ref[5:6, :]).astype(BF16)

        @pl.when(k == 0)
        def _():
            df2_ref[...] = df2

        da = lax.dot_general(df2, w2_ref[0], NT, preferred_element_type=F32)
        df1 = (da * (2.0 * r_ref[...].astype(F32))).astype(BF16)
        df1_ref[...] = df1
        part = lax.dot_general(df1, w1_ref[0], NT, preferred_element_type=F32)

        @pl.when(k == 0)
        def _():
            dh2_ref[...] = part

        @pl.when(k > 0)
        def _():
            dh2_ref[...] += part

    return pl.pallas_call(
        body, name="bwd_ffn", grid=(S // tm, nk),
        out_shape=(_sds((S, D), BF16), _sds((S, DFF), BF16), _sds((S, D), F32)),
        in_specs=[pl.BlockSpec((tm, D), lambda i, k: (i, 0)), pl.BlockSpec((8, D), lambda i, k: (0, 0)),
                  pl.BlockSpec((tm, bf), lambda i, k: (i, k)), pl.BlockSpec((1, D, bf), lambda i, k: (k, 0, 0)),
                  pl.BlockSpec((1, bf, D), lambda i, k: (k, 0, 0))],
        out_specs=(pl.BlockSpec((tm, D), lambda i, k: (i, 0)), pl.BlockSpec((tm, bf), lambda i, k: (i, k)),
                   pl.BlockSpec((tm, D), lambda i, k: (i, 0))),
        compiler_params=_cp(("parallel", "arbitrary")),
    )(dy, mod8, r, w1, w2)


def _wgrad(a, b, name, square_a=False, tk=512, bm=1024, bn=1024):
    S, M = a.shape
    N = b.shape[1]
    bm, bn, tk = min(bm, M), min(bn, N), min(tk, S)

    def body(a_ref, b_ref, o_ref):
        av = a_ref[...]
        if square_a:
            af = av.astype(F32)
            av = (af * af).astype(BF16)
        part = lax.dot_general(av, b_ref[...], TN, preferred_element_type=F32)

        @pl.when(pl.program_id(2) == 0)
        def _():
            o_ref[...] = part

        @pl.when(pl.program_id(2) > 0)
        def _():
            o_ref[...] += part

    return pl.pallas_call(
        body, name=name, grid=(M // bm, N // bn, S // tk), out_shape=_sds((M, N), F32),
        in_specs=[pl.BlockSpec((tk, bm), lambda mi, ni, k: (k, mi)), pl.BlockSpec((tk, bn), lambda mi, ni, k: (k, ni))],
        out_specs=pl.BlockSpec((bm, bn), lambda mi, ni, k: (mi, ni)),
        compiler_params=_cp(("parallel", "parallel", "arbitrary")),
    )(a, b)


def _colsum8(t):
    return jnp.sum(t.reshape(t.shape[0] // 8, 8, t.shape[1]), axis=0)


def _bwd_mid(dh2, dy, x2, ob, o_attn, u1, mod8, n2g, beta_a, beta_c, lng, lnb, w_out, tm):
    S = dy.shape[0]

    def body(dh2_ref, dy_ref, x2_ref, ob_ref, oa_ref, u1_ref, mod_ref, n2g_ref, ba_ref, bc_ref, lng_ref, lnb_ref, w_ref,
             dx2_ref, do_ref, doa_ref, du1_ref, acc_d_ref, acc_h_ref):
        @pl.when(pl.program_id(0) == 0)
        def _():
            acc_d_ref[...] = jnp.zeros_like(acc_d_ref)
            acc_h_ref[...] = jnp.zeros_like(acc_h_ref)

        x2 = x2_ref[...]
        dh2 = dh2_ref[...]
        r2 = lax.rsqrt(jnp.mean(x2 * x2, axis=-1, keepdims=True) + EPS)
        xn2 = x2 * r2
        gain = n2g_ref[...] * (1.0 + mod_ref[4:5, :])
        dxn = dh2 * gain
        dx2 = dy_ref[...] + r2 * (dxn - xn2 * jnp.mean(dxn * xn2, axis=-1, keepdims=True))
        dx2_ref[...] = dx2
        t = dh2 * xn2
        acc_d_ref[0] += _colsum8(dh2)
        acc_d_ref[1] += _colsum8(t * n2g_ref[...])
        acc_d_ref[2] += _colsum8(t * (1.0 + mod_ref[4:5, :]))
        acc_d_ref[3] += _colsum8(dx2 * ob_ref[...].astype(F32))
        do = (dx2 * mod_ref[2:3, :]).astype(BF16)
        do_ref[...] = do
        dma = lax.dot_general(do, w_ref[0:AW, :], NT, preferred_element_type=F32)
        dmc = lax.dot_general(do, w_ref[AW:D, :], NT, preferred_element_type=F32)
        ov = oa_ref[...]
        ra = lax.rsqrt(jnp.mean(ov * ov, axis=-1, keepdims=True) + EPS)
        on = ov * ra
        acc_h_ref[0] += _colsum8(dma * on)
        don = dma * ba_ref[...]
        doa_ref[...] = (ra * (don - on * jnp.mean(don * on, axis=-1, keepdims=True))).astype(BF16)
        u1 = u1_ref[...]
        mu = jnp.mean(u1, axis=-1, keepdims=True)
        d = u1 - mu
        rstd = lax.rsqrt(jnp.mean(d * d, axis=-1, keepdims=True) + EPS)
        uh = d * rstd
        u2 = uh * lng_ref[...] + lnb_ref[...]
        sg = _sigmoid(u2)
        u3 = u2 * sg
        rc = lax.rsqrt(jnp.mean(u3 * u3, axis=-1, keepdims=True) + EPS)
        u3n = u3 * rc
        acc_h_ref[1] += _colsum8(dmc * u3n)
        du3n = dmc * bc_ref[...]
        du3 = rc * (du3n - u3n * jnp.mean(du3n * u3n, axis=-1, keepdims=True))
        du2 = du3 * (sg * (1.0 + u2 * (1.0 - sg)))
        acc_h_ref[2] += _colsum8(du2 * uh)
        acc_h_ref[3] += _colsum8(du2)
        duh = du2 * lng_ref[...]
        du1_ref[...] = rstd * (duh - jnp.mean(duh, axis=-1, keepdims=True)
                               - uh * jnp.mean(duh * uh, axis=-1, keepdims=True))

    return pl.pallas_call(
        body, name="bwd_mid", grid=(S // tm,),
        out_shape=(_sds((S, D), F32), _sds((S, D), BF16), _sds((S, AW), BF16), _sds((S, CW), F32),
                   _sds((4, 8, D), F32), _sds((4, 8, AW), F32)),
        in_specs=[_rows(tm, D), _rows(tm, D), _rows(tm, D), _rows(tm, D), _rows(tm, AW), _rows(tm, CW), _const((8, D)),
                  _const((1, D)), _const((1, AW)), _const((1, CW)), _const((1, CW)), _const((1, CW)), _const((D, D))],
        out_specs=(_rows(tm, D), _rows(tm, D), _rows(tm, AW), _rows(tm, CW), _const((4, 8, D)), _const((4, 8, AW))),
        compiler_params=_cp(("arbitrary",)),
    )(dh2, dy, x2, ob, o_attn, u1, mod8, n2g, beta_a, beta_c, lng, lnb, w_out)


def _attn_bwd_dq(qa, ka, v, do, o_attn, lsec, fs, fe, bd, tq):
    S = qa.shape[1]
    nq = S // tq
    tc = _chunk(tq)

    def body(fs_ref, fe_ref, bd_ref, qa_ref, ka_ref, v_ref, do_ref, o_ref, lse_ref, dqa_ref, dr_ref, acc_ref):
        pr, i = pl.program_id(0), pl.program_id(1)
        sel_a = _lane((tq, LANES)) < DH
        dov = do_ref[...]
        prod = dov.astype(F32) * o_ref[...]
        zero = jnp.zeros_like(prod)
        deltas = [jnp.broadcast_to(jnp.sum(jnp.where(sel_a, prod, zero), axis=1, keepdims=True), (tq, LANES)),
                  jnp.broadcast_to(jnp.sum(jnp.where(sel_a, zero, prod), axis=1, keepdims=True), (tq, LANES))]
        zb = jnp.zeros_like(dov)
        dos = [jnp.where(sel_a, dov, zb), jnp.where(sel_a, zb, dov)]
        acc_ref[...] = jnp.zeros_like(acc_ref)

        def kv_step(j, heads, masked=False):
            for c0 in range(0, tq, tc):
                start = pl.multiple_of(j * tq + c0, tc)
                vb = v_ref[pl.ds(start, tc), :]
                for hh in heads:
                    kb = ka_ref[hh, pl.ds(start, tc), :]
                    s = lax.dot_general(qa_ref[hh], kb, NT, preferred_element_type=F32)
                    p = jnp.exp(s - jnp.tile(lse_ref[hh], (1, tc // LANES)))
                    if masked:
                        p = jnp.where(_causal_rect(tq, tc, c0), p, 0.0)
                    dp = lax.dot_general(dos[hh], vb, NT, preferred_element_type=F32)
                    ds = p * (dp - jnp.tile(deltas[hh], (1, tc // LANES)))
                    acc_ref[hh] += jnp.dot(ds.astype(BF16), kb, preferred_element_type=F32)

        def needed(hh, j):
            top = fs_ref[2 * pr + hh, i] + bd_ref[0, 0]
            return jnp.logical_and(j >= 0, top - fe_ref[2 * pr + hh, jnp.maximum(j, 0)] >= SKIP)

        kv_step(i, (0, 1), masked=True)
        _block_loops(i - 1, -1, needed, kv_step)
        dqa_ref[...] = acc_ref[...]
        row = lax.broadcasted_iota(jnp.int32, (8, tq), 0)
        da = deltas[0].T[0:8, :]
        db = deltas[1].T[0:8, :]
        dr_ref[0, 0] = jnp.where(row == 0, da, db)

    return pl.pallas_call(
        body, name="attn_bwd_dq", grid=(NH // 2, nq),
        out_shape=(_sds((NH, S, LANES), F32), _sds((NH // 2, nq, 8, tq), F32)),
        in_specs=[SMEM_SPEC, SMEM_SPEC, SMEM_SPEC,
                  pl.BlockSpec((2, tq, LANES), lambda p, i: (p, i, 0)),
                  pl.BlockSpec((2, S, LANES), lambda p, i: (p, 0, 0)),
                  pl.BlockSpec((S, LANES), lambda p, i: (0, p)),
                  pl.BlockSpec((tq, LANES), lambda p, i: (i, p)),
                  pl.BlockSpec((tq, LANES), lambda p, i: (i, p)),
                  pl.BlockSpec((2, tq, LANES), lambda p, i: (p, i, 0))],
        out_specs=(pl.BlockSpec((2, tq, LANES), lambda p, i: (p, i, 0)),
                   pl.BlockSpec((1, 1, 8, tq), lambda p, i: (p, i, 0, 0))),
        scratch_shapes=[pltpu.VMEM((2, tq, LANES), F32)],
        compiler_params=_cp(("parallel", "parallel")),
    )(fs, fe, bd, qa, ka, v, do, o_attn, lsec)


def _attn_bwd_dkv(qa, ka, v, do, lser, dr, fs, fe, bd, tq):
    S = qa.shape[1]
    nq = S // tq
    tc = _chunk(tq)

    def body(fs_ref, fe_ref, bd_ref, ka_ref, v_ref, qa_ref, do_ref, lse_ref, dr_ref, dka_ref, dv_ref, acck_ref,
             accv_ref):
        pr, j = pl.program_id(0), pl.program_id(1)
        sel_a = _lane((tq, LANES)) < DH
        vv = v_ref[...]
        zb = jnp.zeros_like(vv)
        vs = [jnp.where(sel_a, vv, zb), jnp.where(sel_a, zb, vv)]
        acck_ref[...] = jnp.zeros_like(acck_ref)
        accv_ref[...] = jnp.zeros_like(accv_ref)

        def q_step(i, heads, masked=False):
            lse8 = lse_ref[0, i]
            dr8 = dr_ref[0, i]
            for c0 in range(0, tq, tc):
                start = pl.multiple_of(i * tq + c0, tc)
                dob = do_ref[pl.ds(start, tc), :]
                for hh in heads:
                    qb = qa_ref[hh, pl.ds(start, tc), :]
                    st = lax.dot_general(ka_ref[hh], qb, NT, preferred_element_type=F32)
                    pt = jnp.exp(st - lse8[hh:hh + 1, c0:c0 + tc])
                    if masked:
                        keep = (lax.broadcasted_iota(jnp.int32, (tq, tc), 0)
                                <= lax.broadcasted_iota(jnp.int32, (tq, tc), 1) + c0)
                        pt = jnp.where(keep, pt, 0.0)
                    accv_ref[hh] += jnp.dot(pt.astype(BF16), dob, preferred_element_type=F32)
                    dpt = lax.dot_general(vs[hh], dob, NT, preferred_element_type=F32)
                    dst = pt * (dpt - dr8[hh:hh + 1, c0:c0 + tc])
                    acck_ref[hh] += jnp.dot(dst.astype(BF16), qb, preferred_element_type=F32)

        def needed(hh, i):
            top = fs_ref[2 * pr + hh, jnp.minimum(i, nq - 1)] + bd_ref[0, 0]
            return jnp.logical_and(i < nq, top - fe_ref[2 * pr + hh, j] >= SKIP)

        q_step(j, (0, 1), masked=True)
        _block_loops(j + 1, 1, needed, q_step)
        dka_ref[...] = acck_ref[...]
        dv_ref[...] = jnp.where(sel_a, accv_ref[0], accv_ref[1]).astype(BF16)

    return pl.pallas_call(
        body, name="attn_bwd_dkv", grid=(NH // 2, nq),
        out_shape=(_sds((NH, S, LANES), F32), _sds((S, AW), BF16)),
        in_specs=[SMEM_SPEC, SMEM_SPEC, SMEM_SPEC,
                  pl.BlockSpec((2, tq, LANES), lambda p, j: (p, j, 0)),
                  pl.BlockSpec((tq, LANES), lambda p, j: (j, p)),
                  pl.BlockSpec((2, S, LANES), lambda p, j: (p, 0, 0)),
                  pl.BlockSpec((S, LANES), lambda p, j: (0, p)),
                  pl.BlockSpec((1, nq, 8, tq), lambda p, j: (p, 0, 0, 0)),
                  pl.BlockSpec((1, nq, 8, tq), lambda p, j: (p, 0, 0, 0))],
        out_specs=(pl.BlockSpec((2, tq, LANES), lambda p, j: (p, j, 0)),
                   pl.BlockSpec((tq, LANES), lambda p, j: (j, p))),
        scratch_shapes=[pltpu.VMEM((2, tq, LANES), F32), pltpu.VMEM((2, tq, LANES), F32)],
        compiler_params=_cp(("parallel", "parallel")),
    )(fs, fe, bd, ka, v, qa, do, lser, dr)


def _bwd_conv(du1, u0, alin, agate, w32, tm):
    S = du1.shape[0]
    nt = S // tm

    def body(du_ref, dun_ref, u0_ref, u0p_ref, alin_ref, agate_ref, w_ref,
             dalin_ref, dagate_ref, dw_ref, db_ref, extd_ref, extu_ref, du0_ref):
        i = pl.program_id(0)

        @pl.when(i == 0)
        def _():
            dw_ref[...] = jnp.zeros_like(dw_ref)
            db_ref[...] = jnp.zeros_like(db_ref)

        extd_ref[0:tm, :] = du_ref[...]
        extd_ref[tm:, :] = jnp.where(i == nt - 1, 0.0, dun_ref[...])
        extu_ref[0:HALO, :] = jnp.where(i == 0, 0.0, u0p_ref[...])
        extu_ref[HALO:, :] = u0_ref[...]
        db_ref[...] += _colsum8(du_ref[...])
        for r0 in range(0, tm, CHUNK_ROWS):
            duc = du_ref[r0:r0 + CHUNK_ROWS, :]
            acc = jnp.zeros((CHUNK_ROWS, CW), F32)
            for j in range(KC):
                acc = acc + w_ref[j:j + 1, :] * extd_ref[r0 + 30 - j:r0 + 30 - j + CHUNK_ROWS, :]
                dw_ref[j] += _colsum8(duc * extu_ref[r0 + 2 + j:r0 + 2 + j + CHUNK_ROWS, :])
            du0_ref[r0:r0 + CHUNK_ROWS, :] = acc
        du0 = du0_ref[...]
        al = alin_ref[...].astype(F32)
        sg = _sigmoid(agate_ref[...].astype(F32))
        dalin_ref[...] = (du0 * sg).astype(BF16)
        dagate_ref[...] = (du0 * al * sg * (1.0 - sg)).astype(BF16)

    nxt = pl.BlockSpec((HALO, CW), lambda i: (jnp.minimum((i + 1) * (tm // HALO), S // HALO - 1), 0))
    return pl.pallas_call(
        body, name="bwd_conv", grid=(nt,),
        out_shape=(_sds((S, CW), BF16), _sds((S, CW), BF16), _sds((HALO, 8, CW), F32), _sds((8, CW), F32)),
        in_specs=[_rows(tm, CW), nxt, _rows(tm, CW), _halo_prev(tm), _rows(tm, CW), _rows(tm, CW), _const((HALO, CW))],
        out_specs=(_rows(tm, CW), _rows(tm, CW), _const((HALO, 8, CW)), _const((8, CW))),
        scratch_shapes=[pltpu.VMEM((tm + HALO, CW), F32), pltpu.VMEM((tm + HALO, CW), F32), pltpu.VMEM((tm, CW), F32)],
        compiler_params=_cp(("arbitrary",)),
    )(du1, du1, u0, u0, alin, agate, w32)


def _bwd_qk(dqa, dka, qn, kn, rq, rk, fgb, qg512, kg512, e512, et512, tm):
    S = qn.shape[0]
    nt = S // tm

    def body(dqa_ref, dka_ref, qn_ref, kn_ref, rq_ref, rk_ref, fgb_ref, qg_ref, kg_ref, e_ref, et_ref,
             dq_ref, dk_ref, dfg_ref, accg_ref, accb_ref, carry_ref):
        @pl.when(pl.program_id(0) == 0)
        def _():
            carry_ref[...] = jnp.zeros_like(carry_ref)
            accg_ref[...] = jnp.zeros_like(accg_ref)
            accb_ref[...] = jnp.zeros_like(accb_ref)

        lane = _lane((tm, LANES))
        sel_a = lane < DH
        df = jnp.zeros((tm, LANES), F32)
        for h in range(NH):
            col = dqa_ref[h][:, 64:65] - dka_ref[h][:, 67:68]
            df = jnp.where(lane == h, col, df)
        tri = (lax.broadcasted_iota(jnp.int32, (tm, tm), 0) <= lax.broadcasted_iota(jnp.int32, (tm, tm), 1)
               ).astype(F32).astype(BF16)
        dlf = _dot3(tri, df) + carry_ref[0:1, :]
        carry_ref[...] = jnp.broadcast_to(dlf[0:1, :], carry_ref.shape)
        dfg = jnp.where(lane < NH, dlf * _sigmoid(-fgb_ref[...]), 0.0)
        dfg_ref[...] = dfg.astype(BF16)
        accb_ref[...] += _colsum8(dfg)

        def norm_bwd(src_ref, n_ref, r_ref, g_ref, scale, slot):
            pairs = []
            for p in range(NH // 2):
                b = pltpu.roll(src_ref[2 * p + 1], 64, 1)
                pairs.append(jnp.where(sel_a, src_ref[2 * p], b))
            dh = jnp.concatenate(pairs, axis=1) * scale
            tn = n_ref[...].astype(F32)
            accg_ref[slot] += _colsum8(dh * tn)
            dn = dh * g_ref[...]
            mean = _dot2(dn * tn, e_ref[...]) * (1.0 / DH)
            corr = _dot2(mean, et_ref[...])
            rf = _dot2(r_ref[...], et_ref[...])
            return (rf * (dn - tn * corr)).astype(BF16)

        dq_ref[...] = norm_bwd(dqa_ref, qn_ref, rq_ref, qg_ref, DH ** -0.5, 0)
        dk_ref[...] = norm_bwd(dka_ref, kn_ref, rk_ref, kg_ref, 1.0, 1)

    rev = lambda n: pl.BlockSpec((tm, n), lambda i: (nt - 1 - i, 0))
    hm = pl.BlockSpec((NH, tm, LANES), lambda i: (0, nt - 1 - i, 0))
    dq, dk, dfg, accg, accb = pl.pallas_call(
        body, name="bwd_qk", grid=(nt,),
        out_shape=(_sds((S, AW), BF16), _sds((S, AW), BF16), _sds((S, LANES), BF16), _sds((2, 8, AW), F32),
                   _sds((8, LANES), F32)),
        in_specs=[hm, hm, rev(AW), rev(AW), rev(LANES), rev(LANES), rev(LANES), _const((1, AW)), _const((1, AW)),
                  _const((AW, LANES)), _const((LANES, AW))],
        out_specs=(rev(AW), rev(AW), rev(LANES), _const((2, 8, AW)), _const((8, LANES))),
        scratch_shapes=[pltpu.VMEM((8, LANES), F32)], compiler_params=_cp(("arbitrary",)),
    )(dqa, dka, qn, kn, rq, rk, fgb, qg512, kg512, e512, et512)
    return dq, dk, dfg, accg, accb


def _bwd_in(dq, dk, dv, dalin, dagate, dfg, w_in_p, x, dx2, mod8, n1g, tm):
    S = x.shape[0]

    def body(dq_ref, dk_ref, dv_ref, dal_ref, dag_ref, dfg_ref, w_ref, x_ref, dx2_ref, mod_ref, n1g_ref,
             dx_ref, acc_ref):
        @pl.when(pl.program_id(0) == 0)
        def _():
            acc_ref[...] = jnp.zeros_like(acc_ref)

        def part(ref, a, b):
            return lax.dot_general(ref[...], w_ref[:, a:b], NT, preferred_element_type=F32)

        dh = (part(dq_ref, 0, 512) + part(dk_ref, 512, 1024) + part(dv_ref, 1024, 1536) + part(dal_ref, 1536, 2048)
              + part(dag_ref, 2048, 2560) + part(dfg_ref, 2560, NP))
        xv = x_ref[...]
        r1 = lax.rsqrt(jnp.mean(xv * xv, axis=-1, keepdims=True) + EPS)
        xn = xv * r1
        gain = n1g_ref[...] * (1.0 + mod_ref[1:2, :])
        t = dh * xn
        acc_ref[0] += _colsum8(dh)
        acc_ref[1] += _colsum8(t * n1g_ref[...])
        acc_ref[2] += _colsum8(t * (1.0 + mod_ref[1:2, :]))
        dxn = dh * gain
        dx_ref[...] = dx2_ref[...] + r1 * (dxn - xn * jnp.mean(dxn * xn, axis=-1, keepdims=True))

    return pl.pallas_call(
        body, name="bwd_in", grid=(S // tm,),
        out_shape=(_sds((S, D), F32), _sds((3, 8, D), F32)),
        in_specs=[_rows(tm, AW), _rows(tm, AW), _rows(tm, AW), _rows(tm, CW), _rows(tm, CW), _rows(tm, LANES),
                  _const((D, NP)), _rows(tm, D), _rows(tm, D), _const((8, D)), _const((1, D))],
        out_specs=(_rows(tm, D), _const((3, 8, D))),
        compiler_params=_cp(("arbitrary",)),
    )(dq, dk, dv, dalin, dagate, dfg, w_in_p, x, dx2, mod8, n1g)


def _adam(w, g, m, v):
    m_new = B1 * m + (1.0 - B1) * g
    v_new = B2 * v + (1.0 - B2) * (g * g)
    m_hat = m_new / (1.0 - B1 ** STEP)
    v_hat = v_new / (1.0 - B2 ** STEP)
    delta = -LR * (m_hat / (jnp.sqrt(v_hat) + AEPS) + WD * w)
    return delta, m_new, v_new


def _reduce_adamw(slots, w, m, v, name, tr=256):
    ns, R, C = slots.shape
    tr = tr if R % tr == 0 else R

    def body(s_ref, w_ref, m_ref, v_ref, g_ref, d_ref, mo_ref, vo_ref):
        g = s_ref[0].astype(F32)
        for k in range(1, ns):
            g = g + s_ref[k].astype(F32)
        g_ref[...] = g
        d_ref[...], mo_ref[...], vo_ref[...] = _adam(w_ref[...], g, m_ref[...], v_ref[...])

    blk = pl.BlockSpec((tr, C), lambda i: (i, 0))
    return pl.pallas_call(
        body, name=name, grid=(R // tr,), out_shape=tuple(_sds((R, C), F32) for _ in range(4)),
        in_specs=[pl.BlockSpec((ns, tr, C), lambda i: (0, i, 0)), blk, blk, blk], out_specs=(blk, blk, blk, blk),
        compiler_params=_cp(("parallel",)),
    )(slots, w, m, v)


def _pair_adamw(slots, w, m, v, name, tr=256):
    ns, R, C = slots.shape
    tr = tr if R % tr == 0 else R
    nt = R // tr

    def body(s_ref, w_ref, m_ref, v_ref, g_ref, d_ref, mo_ref, vo_ref, mine_ref, theirs_ref, send_sems, recv_sems):
        i = pl.program_id(0)
        part = s_ref[0].astype(F32)
        for k in range(1, ns):
            part = part + s_ref[k].astype(F32)
        mine_ref[i] = part
        swap = pltpu.make_async_remote_copy(
            src_ref=mine_ref.at[i], dst_ref=theirs_ref.at[i], send_sem=send_sems.at[i], recv_sem=recv_sems.at[i],
            device_id=(lax.axis_index("x"), lax.axis_index("y"), 1 - lax.axis_index("c")),
            device_id_type=pl.DeviceIdType.MESH)
        swap.start()
        swap.wait()
        g = part + theirs_ref[i]
        g_ref[...] = g
        d_ref[...], mo_ref[...], vo_ref[...] = _adam(w_ref[...], g, m_ref[...], v_ref[...])

    blk = pl.BlockSpec((tr, C), lambda i: (i, 0))
    return pl.pallas_call(
        body, name=name, grid=(nt,), out_shape=tuple(_sds((R, C), F32) for _ in range(4)),
        in_specs=[pl.BlockSpec((ns, tr, C), lambda i: (0, i, 0)), blk, blk, blk], out_specs=(blk, blk, blk, blk),
        scratch_shapes=[pltpu.VMEM((nt, tr, C), F32), pltpu.VMEM((nt, tr, C), F32),
                        pltpu.SemaphoreType.DMA((nt,)), pltpu.SemaphoreType.DMA((nt,))],
        compiler_params=_cp(("arbitrary",)),
    )(slots, w, m, v)


def _ada_adamw(sct, dmod, w, m, v):
    R, C = w.shape
    tr, bc = 256, 512

    def body(sct_ref, dm_ref, w_ref, m_ref, v_ref, g_ref, d_ref, mo_ref, vo_ref):
        g = sct_ref[:, 0:1] * dm_ref[0:1, :]
        for b in range(1, N_DEV):
            g = g + sct_ref[:, b:b + 1] * dm_ref[b:b + 1, :]
        g_ref[...] = g
        d_ref[...], mo_ref[...], vo_ref[...] = _adam(w_ref[...], g, m_ref[...], v_ref[...])

    blk = pl.BlockSpec((tr, bc), lambda i, j: (i, j))
    return pl.pallas_call(
        body, name="ada_adamw", grid=(R // tr, C // bc), out_shape=tuple(_sds((R, C), F32) for _ in range(4)),
        in_specs=[pl.BlockSpec((tr, N_DEV), lambda i, j: (i, 0)), pl.BlockSpec((N_DEV, bc), lambda i, j: (0, j)),
                  blk, blk, blk],
        out_specs=(blk, blk, blk, blk), compiler_params=_cp(("parallel", "parallel")),
    )(sct, dmod, w, m, v)


def _small_reduce(slots, fold):
    def body(s_ref, f_ref, o_ref):
        tot = s_ref[0:1, :]
        for k in range(1, N_DEV):
            tot = tot + s_ref[k:k + 1, :]
        o_ref[:, 0:6144] = tot[:, 0:6144]
        o_ref[:, 6144:7168] = tot[:, 6144:7168]
        for t, src in enumerate((8192, 8704)):
            v8 = jnp.broadcast_to(tot[:, src:src + AW], (8, AW))
            o_ref[:, 7168 + t * LANES:7168 + (t + 1) * LANES] = jnp.dot(
                v8, f_ref[...], precision=HI, preferred_element_type=F32)[0:1, :]
        o_ref[:, 7424:7552] = tot[:, 9216:9344]
        o_ref[:, 7552:10112] = tot[:, 9344:11904]
        o_ref[:, 10112:SMALL_OUT] = tot[:, 7168:8192]

    return pl.pallas_call(
        body, name="small_reduce", out_shape=_sds((1, SMALL_OUT), F32),
        in_specs=[pl.BlockSpec(memory_space=pltpu.VMEM), pl.BlockSpec(memory_space=pltpu.VMEM)],
        out_specs=pl.BlockSpec(memory_space=pltpu.VMEM),
    )(slots, fold)


def _perm_in(w):
    pad = jnp.zeros((w.shape[0], NP - 2568), w.dtype)
    return jnp.concatenate([w[:, :1536], w[:, 1544:2568], w[:, 1536:1544], pad], axis=1)


def _pad_lanes(vec, n=LANES):
    return jnp.pad(vec, ((0, 0), (0, n - vec.shape[1])))


def kernel(x, c, w_ada, b_ada, norm1_g, w_in, q_norm_g, k_norm_g, b_f, conv_w, conv_b, conv_ln_g, conv_ln_b, beta_attn, beta_conv, w_out, norm2_g, w_ff1, w_ff2, loss_target, m_w_ada, m_b_ada, m_norm1_g, m_w_in, m_q_norm_g, m_k_norm_g, m_b_f, m_conv_w, m_conv_b, m_conv_ln_g, m_conv_ln_b, m_beta_attn, m_beta_conv, m_w_out, m_norm2_g, m_w_ff1, m_w_ff2, v_w_ada, v_b_ada, v_norm1_g, v_w_in, v_q_norm_g, v_k_norm_g, v_b_f, v_conv_w, v_conv_b, v_conv_ln_g, v_conv_ln_b, v_beta_attn, v_beta_conv, v_w_out, v_norm2_g, v_w_ff1, v_w_ff2):
    S = x.shape[1]
    tm = min(256, S)
    tq = min(512, S // 2)
    xs, tgt = x[0], loss_target[0]
    chip = 2 * lax.axis_index("x") + lax.axis_index("y")
    e512, et512 = _head_sum_mats()

    conv_w32 = jnp.pad(conv_w[0], ((0, 1), (0, 0)))
    c_all, g_in, g_out, g_ff1, g_ff2, g_cw = _exchange(
        [(c, "bcast8"), (w_in[0].astype(BF16), "chip4"), (w_out[0].astype(BF16), "chip4"),
         (w_ff1[0].astype(BF16), "chip4"), (w_ff2[0].astype(BF16), "chip4"), (conv_w32, "chip4")], "gather_weights")
    c_all = c_all.reshape(N_DEV, D)
    w_in_p = _perm_in(jnp.transpose(g_in, (1, 0, 2)).reshape(D, 2568))
    w_out_f = g_out.reshape(D, D)
    w1, w2 = g_ff1, g_ff2
    cw32 = jnp.transpose(g_cw, (1, 0, 2)).reshape(HALO, CW)

    b_shard = lax.dynamic_slice(b_ada, (0, chip * 1536), (1, 1536))
    mod_rows, sc_all = _mod_shard(c_all, w_ada[0], b_shard)
    (mod_slots,) = _exchange([(mod_rows.reshape(N_DEV, 1, 1536), "all8")], "scatter_mod")
    mod = mod_slots.reshape(4, 2, 1536)[:, 0, :].reshape(6, D)
    mod8 = jnp.pad(mod, ((0, 2), (0, 0)))

    qg512 = jnp.tile(q_norm_g, (1, NH))
    kg512 = jnp.tile(k_norm_g, (1, NH))
    bf128 = _pad_lanes(b_f)

    h1, qh, kh, vb, qn, kn, rq, rk, fgb, alin, agate, u0 = _fwd_in(xs, mod8, norm1_g, w_in_p, e512, et512, qg512, kg512,
                                                                   bf128, tm)
    qa, ka, fcum = _fwd_decay(fgb, qh, kh, tm)
    fs, fe, bd = _skip_tables(fcum, q_norm_g, k_norm_g, tq)
    o_attn, lsec, lser = _attn_fwd(qa, ka, vb, fs, fe, bd, tq)
    u1, mc = _fwd_conv(u0, cw32, conv_b, conv_ln_g, conv_ln_b, beta_conv, tm)
    merged, ob, x2, h2 = _fwd_out(o_attn, mc, xs, mod8, norm2_g, beta_attn, w_out_f, tm)
    tf = min(512, S)
    r, dy, loss8, dg2 = _fwd_ffn(h2, w1, w2, x2, tgt, mod8, tf)

    df2, df1, dh2 = _bwd_ffn(dy, mod8, r, w1, w2, tf)
    gw_ff2 = _wgrad(r, df2, "wgrad_ff2", square_a=True)
    gw_ff1 = _wgrad(h2, df1, "wgrad_ff1")
    dx2, do, doa, du1, acc_d, acc_h = _bwd_mid(dh2, dy, x2, ob, o_attn, u1, mod8, norm2_g, beta_attn, beta_conv,
                                               conv_ln_g, conv_ln_b, w_out_f, tm)
    gw_out = _wgrad(merged, do, "wgrad_out")
    dqa, dr = _attn_bwd_dq(qa, ka, vb, doa, o_attn, lsec, fs, fe, bd, tq)
    dka, dv = _attn_bwd_dkv(qa, ka, vb, doa, lser, dr, fs, fe, bd, tq)
    dalin, dagate, dcw, dcb = _bwd_conv(du1, u0, alin, agate, cw32, tm)
    dq, dk, dfg, accg, accb = _bwd_qk(dqa, dka, qn, kn, rq, rk, fgb, qg512, kg512, e512, et512, tm)
    grad_x, acc1 = _bwd_in(dq, dk, dv, dalin, dagate, dfg, w_in_p, xs, dx2, mod8, norm1_g, tm)
    gw_in = jnp.concatenate(
        [_wgrad(h1, dq, "wgrad_in_q"), _wgrad(h1, dk, "wgrad_in_k"), _wgrad(h1, dv, "wgrad_in_v"),
         _wgrad(h1, dfg, "wgrad_in_f")[:, :NH], _wgrad(h1, dalin, "wgrad_in_a"), _wgrad(h1, dagate, "wgrad_in_g")],
        axis=1)

    s8 = lambda a: jnp.sum(a, axis=-2)
    a1, ad, ah = s8(acc1), s8(acc_d), s8(acc_h)
    small = jnp.concatenate(
        [a1[0], a1[1], ad[3], ad[0], ad[1], s8(dg2),
         a1[2], ad[2], s8(accg).reshape(-1), s8(accb), s8(dcb), ah[2], ah[3], ah[0], ah[1]]).reshape(1, SMALL_IN)
    gcw = s8(dcw)
    small_s, p_in, p_out, p_ff1, p_ff2, p_cw = _exchange(
        [(small, "bcast8"),
         (jnp.transpose(gw_in.reshape(D, 4, 642), (1, 0, 2)).astype(BF16), "chip4p"),
         (gw_out.reshape(4, 256, D).astype(BF16), "chip4p"),
         (jnp.transpose(gw_ff1.reshape(D, 4, D), (1, 0, 2)).astype(BF16), "chip4p"),
         (gw_ff2.reshape(4, D, D).astype(BF16), "chip4p"),
         (jnp.transpose(gcw.reshape(HALO, 4, LANES), (1, 0, 2)), "chip4p")], "scatter_grads")
    small_s = small_s.reshape(N_DEV, SMALL_IN)

    g_in_, d_in, nm_in, nv_in = _pair_adamw(p_in, w_in[0], m_w_in[0], v_w_in[0], "adamw_in")
    g_out_, d_out, nm_out, nv_out = _pair_adamw(p_out, w_out[0], m_w_out[0], v_w_out[0], "adamw_out")
    g_f1, d_f1, nm_f1, nv_f1 = _pair_adamw(p_ff1, w_ff1[0], m_w_ff1[0], v_w_ff1[0], "adamw_ff1")
    g_f2, d_f2, nm_f2, nv_f2 = _pair_adamw(p_ff2, w_ff2[0], m_w_ff2[0], v_w_ff2[0], "adamw_ff2")
    pad_row = lambda a, fill: jnp.pad(a[0], ((0, 1), (0, 0)), constant_values=fill)
    g_cw_, d_cw, nm_cw, nv_cw = (a[:KC] for a in _pair_adamw(
        p_cw, pad_row(conv_w, 0.0), pad_row(m_conv_w, 0.0), pad_row(v_conv_w, 1.0), "adamw_conv_w"))
    dmod_shard = lax.dynamic_slice(small_s[:, :6 * D], (0, chip * 1536), (N_DEV, 1536))
    g_ada, d_ada, nm_ada, nv_ada = _ada_adamw(sc_all.T, dmod_shard, w_ada[0], m_w_ada[0], v_w_ada[0])

    fold = np.zeros((AW, LANES), np.float32)
    fold[np.arange(AW), np.arange(AW) % DH] = 1.0
    g_small = _small_reduce(small_s, jnp.asarray(fold))
    smalls = [b_ada, norm1_g, q_norm_g, k_norm_g, b_f, conv_b, conv_ln_g, conv_ln_b, beta_attn, beta_conv, norm2_g]
    m_smalls = [m_b_ada, m_norm1_g, m_q_norm_g, m_k_norm_g, m_b_f, m_conv_b, m_conv_ln_g, m_conv_ln_b, m_beta_attn,
                m_beta_conv, m_norm2_g]
    v_smalls = [v_b_ada, v_norm1_g, v_q_norm_g, v_k_norm_g, v_b_f, v_conv_b, v_conv_ln_g, v_conv_ln_b, v_beta_attn,
                v_beta_conv, v_norm2_g]
    widths = [a.shape[1] for a in smalls]
    padded = [-(-n // LANES) * LANES for n in widths]
    pack = lambda arrs, fill: jnp.concatenate(
        [jnp.pad(a, ((0, 0), (0, p - a.shape[1])), constant_values=fill) for a, p in zip(arrs, padded)], axis=1)
    outs_small = _reduce_adamw(g_small.reshape(1, 1, SMALL_OUT), pack(smalls, 0.0), pack(m_smalls, 0.0),
                               pack(v_smalls, 1.0), "adamw_small")
    offs = np.concatenate([[0], np.cumsum(padded)])

    def unpack(a):
        return [a[:, int(o):int(o) + n] for o, n in zip(offs[:-1], widths)]

    gs, ds, ms, vs = (unpack(a) for a in outs_small)

    loss = lax.psum(loss8[0, 0], ("x", "y", "c"))
    big = {"w_ada": (g_ada, d_ada, nm_ada, nv_ada), "w_in": (g_in_, d_in, nm_in, nv_in),
           "conv_w": (g_cw_, d_cw, nm_cw, nv_cw), "w_out": (g_out_, d_out, nm_out, nv_out),
           "w_ff1": (g_f1, d_f1, nm_f1, nv_f1), "w_ff2": (g_f2, d_f2, nm_f2, nv_f2)}
    small_names = ["b_ada", "norm1_g", "q_norm_g", "k_norm_g", "b_f", "conv_b", "conv_ln_g", "conv_ln_b", "beta_attn",
                   "beta_conv", "norm2_g"]
    order = ["w_ada", "b_ada", "norm1_g", "w_in", "q_norm_g", "k_norm_g", "b_f", "conv_w", "conv_b", "conv_ln_g",
             "conv_ln_b", "beta_attn", "beta_conv", "w_out", "norm2_g", "w_ff1", "w_ff2"]

    def leaf(name, which):
        if name in big:
            return big[name][which][None]
        return (gs, ds, ms, vs)[which][small_names.index(name)]

    return (loss, grad_x[None], *[leaf(n, 0) for n in order], *[leaf(n, 1) for n in order],
            *[leaf(n, 2) for n in order], *[leaf(n, 3) for n in order])
```

```python
import functools

import numpy as np
import jax
import jax.numpy as jnp
from jax import lax
from jax.experimental import pallas as pl
from jax.experimental.pallas import tpu as pltpu

F32, BF16 = jnp.float32, jnp.bfloat16
HI = lax.Precision.HIGHEST
D = 1024
AW = 512
CW = 512
NH = 8
DH = 64
KC = 31
DFF = 4096
NP = 2688
EPS = 1e-6
NEG = -1e30
LANES = 128
VMEM_LIMIT = 56 * 2**20
NT = (((1,), (1,)), ((), ()))
TN = (((0,), (0,)), ((), ()))
LR, B1, B2, AEPS, WD, STEP = 0.001, 0.9, 0.999, 1e-08, 0.01, 10
N_DEV = 8
SMALL_IN = 11904
SMALL_OUT = 11136


def _cp(sem=None, vmem=VMEM_LIMIT):
    kw = dict(vmem_limit_bytes=vmem)
    if sem is not None:
        kw["dimension_semantics"] = sem
    return pltpu.CompilerParams(**kw)


def _rows(tm, n):
    return pl.BlockSpec((tm, n), lambda i: (i, 0))


def _const(shape):
    nd = len(shape)
    return pl.BlockSpec(shape, lambda *_: (0,) * nd)


def _sds(shape, dt):
    return jax.ShapeDtypeStruct(shape, dt)


def _lane(shape):
    return lax.broadcasted_iota(jnp.int32, shape, len(shape) - 1)


def _sigmoid(x):
    return 1.0 / (1.0 + jnp.exp(-x))


class _Exchange:
    MASKS = {"chip4": (2, 4, 6), "chip4p": (2, 4, 6), "all8": (1, 2, 3, 4, 5, 6, 7), "bcast8": (1, 2, 3, 4, 5, 6, 7)}

    def __init__(self, items):
        self.srcs = [s for s, _ in items]
        self.kinds = [k for _, k in items]
        self.n = len(items)
        self.out_shapes = []
        for s, k in items:
            shape = {"all8": (N_DEV,) + s.shape[1:], "bcast8": (N_DEV,) + s.shape, "chip4": (4,) + s.shape,
                     "chip4p": (4,) + s.shape[1:]}[k]
            self.out_shapes.append(_sds(shape, s.dtype))
        self.sem_index = {}
        for t, k in enumerate(self.kinds):
            for m in self.MASKS[k]:
                self.sem_index[(t, m)] = len(self.sem_index)
        n_sem = len(self.sem_index)
        self.scratch = [pltpu.SemaphoreType.DMA((n_sem,)), pltpu.SemaphoreType.DMA((n_sem,)),
                        pltpu.SemaphoreType.DMA((self.n,))]
        self.in_specs = [pl.BlockSpec(memory_space=pl.ANY)] * self.n
        self.out_specs = [pl.BlockSpec(memory_space=pl.ANY)] * self.n

    def copies(self, src_refs, dst_refs, send_sems, recv_sems, local_sems):
        x, y, c = lax.axis_index("x"), lax.axis_index("y"), lax.axis_index("c")
        my_id = 4 * x + 2 * y + c
        my_chip = 2 * x + y

        def piece(t, dev_id, chip):
            k = self.kinds[t]
            return src_refs[t].at[dev_id] if k == "all8" else src_refs[t].at[chip] if k == "chip4p" else src_refs[t]

        out = []
        for t in range(self.n):
            slot = dst_refs[t].at[my_chip if self.kinds[t] in ("chip4", "chip4p") else my_id]
            out.append(pltpu.make_async_copy(piece(t, my_id, my_chip), slot, local_sems.at[t]))
            for m in self.MASKS[self.kinds[t]]:
                px = 1 - x if m & 4 else x
                py = 1 - y if m & 2 else y
                pc = 1 - c if m & 1 else c
                s = self.sem_index[(t, m)]
                out.append(pltpu.make_async_remote_copy(
                    src_ref=piece(t, 4 * px + 2 * py + pc, 2 * px + py), dst_ref=slot,
                    send_sem=send_sems.at[s], recv_sem=recv_sems.at[s],
                    device_id=(px, py, pc), device_id_type=pl.DeviceIdType.MESH))
        return out


def _hosted(body, ex, n_in, n_out, n_scr, n_steps):
    if ex is None:
        return body

    def wrapped(*refs):
        ins, xin = refs[:n_in], refs[n_in:n_in + ex.n]
        o0 = n_in + ex.n
        outs, xout = refs[o0:o0 + n_out], refs[o0 + n_out:o0 + n_out + ex.n]
        s0 = o0 + n_out + ex.n
        scr, sems = refs[s0:s0 + n_scr], refs[s0 + n_scr:]

        @pl.when(pl.program_id(0) == 0)
        def _():
            for cp in ex.copies(xin, xout, *sems):
                cp.start()

        body(*ins, *outs, *scr)

        @pl.when(pl.program_id(0) == n_steps - 1)
        def _():
            for cp in ex.copies(xin, xout, *sems):
                cp.wait()

    return wrapped


def _exchange(items, name):
    ex = _Exchange(items)
    n = ex.n

    def body(*refs):
        copies = ex.copies(refs[:n], refs[n:2 * n], *refs[2 * n:])
        for cp in copies:
            cp.start()
        for cp in copies:
            cp.wait()

    outs = pl.pallas_call(
        body, name=name, out_shape=tuple(ex.out_shapes), in_specs=ex.in_specs, out_specs=tuple(ex.out_specs),
        scratch_shapes=ex.scratch,
    )(*ex.srcs)
    return list(outs)


def _mod_shard(c_all, w_ada, b_shard):
    n = w_ada.shape[1]

    def body(c_ref, w_ref, b_ref, o_ref, sc_ref):
        cv = c_ref[...]
        sc = cv * _sigmoid(cv)
        sc_ref[...] = sc
        o_ref[...] = jnp.dot(sc, w_ref[...], precision=HI, preferred_element_type=F32) + b_ref[...]

    bn = 512
    return pl.pallas_call(
        body, name="mod_shard", out_shape=(_sds((N_DEV, n), F32), _sds((N_DEV, D), F32)), grid=(n // bn,),
        in_specs=[_const((N_DEV, D)), pl.BlockSpec((D, bn), lambda j: (0, j)), pl.BlockSpec((1, bn), lambda j: (0, j))],
        out_specs=(pl.BlockSpec((N_DEV, bn), lambda j: (0, j)), _const((N_DEV, D))),
        compiler_params=_cp(("arbitrary",)),
    )(c_all, w_ada, b_shard)


def _head_sum_mats():
    e = np.zeros((AW, LANES), np.float32)
    for h in range(NH):
        e[h * DH:(h + 1) * DH, h] = 1.0
    return jnp.asarray(e, BF16), jnp.asarray(e.T.copy(), BF16)


def _dot2(x, w):
    hi = x.astype(BF16)
    lo = (x - hi.astype(F32)).astype(BF16)
    return jnp.dot(hi, w, preferred_element_type=F32) + jnp.dot(lo, w, preferred_element_type=F32)


def _fwd_in(x, mod8, n1g, w_in_p, e512, et512, qg512, kg512, bf128, tm, ex=None):
    S = x.shape[0]

    def body(x_ref, mod_ref, n1g_ref, w_ref, e_ref, et_ref, qg_ref, kg_ref, bf_ref,
             h1_ref, qh_ref, kh_ref, v_ref, qn_ref, kn_ref, rq_ref, rk_ref, fgb_ref, alin_ref, agate_ref, u0_ref):
        xv = x_ref[...]
        r1 = lax.rsqrt(jnp.mean(xv * xv, axis=-1, keepdims=True) + EPS)
        h = (xv * r1) * (n1g_ref[...] * (1.0 + mod_ref[1:2, :])) + mod_ref[0:1, :]
        hb = h.astype(BF16)
        h1_ref[...] = hb

        def seg(a, b):
            return jnp.dot(hb, w_ref[:, a:b], preferred_element_type=F32)

        def headnorm(t, g_ref, scale, n_ref, r_ref, o_ref):
            ss = _dot2(t * t, e_ref[...])
            r = lax.rsqrt(ss * (1.0 / DH) + EPS)
            tn = t * _dot2(r, et_ref[...])
            n_ref[...] = tn.astype(BF16)
            r_ref[...] = r
            o_ref[...] = (tn * (g_ref[...] * scale)).astype(BF16)

        headnorm(seg(0, 512), qg_ref, DH ** -0.5, qn_ref, rq_ref, qh_ref)
        headnorm(seg(512, 1024), kg_ref, 1.0, kn_ref, rk_ref, kh_ref)
        v_ref[...] = seg(1024, 1536).astype(BF16)
        alin = seg(1536, 2048)
        agate = seg(2048, 2560)
        alin_ref[...] = alin.astype(BF16)
        agate_ref[...] = agate.astype(BF16)
        u0_ref[...] = alin * _sigmoid(agate)
        fgb_ref[...] = seg(2560, NP) + bf_ref[...]

    bf = lambda: _sds((S, AW), BF16)
    xs = ex.srcs if ex else []
    outs = pl.pallas_call(
        _hosted(body, ex, 9, 12, 0, S // tm), name="fwd_in", grid=(S // tm,),
        out_shape=(_sds((S, D), BF16), bf(), bf(), bf(), bf(), bf(), _sds((S, LANES), F32), _sds((S, LANES), F32),
                   _sds((S, LANES), F32), bf(), bf(), _sds((S, CW), F32), *(ex.out_shapes if ex else [])),
        in_specs=[_rows(tm, D), _const((8, D)), _const((1, D)), _const((D, NP)), _const((AW, LANES)), _const((LANES, AW)),
                  _const((1, AW)), _const((1, AW)), _const((1, LANES)), *(ex.in_specs if ex else [])],
        out_specs=(_rows(tm, D), _rows(tm, AW), _rows(tm, AW), _rows(tm, AW), _rows(tm, AW), _rows(tm, AW),
                   _rows(tm, LANES), _rows(tm, LANES), _rows(tm, LANES), _rows(tm, AW), _rows(tm, AW), _rows(tm, CW),
                   *(ex.out_specs if ex else [])),
        scratch_shapes=ex.scratch if ex else [],
        compiler_params=_cp(("arbitrary",)),
    )(x, mod8, n1g, w_in_p, e512, et512, qg512, kg512, bf128, *xs)
    return outs[:12], list(outs[12:])


def _split3(f):
    f1 = f.astype(BF16).astype(F32)
    f2 = (f - f1).astype(BF16).astype(F32)
    return f1, f2, f - f1 - f2


def _dot3(w, x):
    return sum(jnp.dot(w, piece.astype(BF16), preferred_element_type=F32) for piece in _split3(x))


def _fwd_decay(fgb, qh, kh, tm):
    S = fgb.shape[0]

    def body(fgb_ref, qh_ref, kh_ref, qa_ref, ka_ref, f_ref, carry_ref):
        @pl.when(pl.program_id(0) == 0)
        def _():
            carry_ref[...] = jnp.zeros_like(carry_ref)

        fb = fgb_ref[...]
        lf = jnp.minimum(fb, 0.0) - jnp.log1p(jnp.exp(-jnp.abs(fb)))
        tri = (lax.broadcasted_iota(jnp.int32, (tm, tm), 0) >= lax.broadcasted_iota(jnp.int32, (tm, tm), 1)
               ).astype(F32).astype(BF16)
        cs = _dot3(tri, lf) + carry_ref[0:1, :]
        f_ref[...] = cs
        carry_ref[...] = jnp.broadcast_to(cs[tm - 1:tm, :], carry_ref.shape)
        lane = _lane((tm, LANES))
        ones = jnp.where((lane >= 67) & (lane < 70), 1.0, 0.0)
        ones_k = jnp.where((lane >= 64) & (lane < 67), 1.0, 0.0)
        for p in range(NH // 2):
            qp = qh_ref[:, p * LANES:(p + 1) * LANES].astype(F32)
            kp = kh_ref[:, p * LANES:(p + 1) * LANES].astype(F32)
            for hh in range(2):
                h = 2 * p + hh
                f1, f2, f3 = _split3(cs[:, h:h + 1])
                qb = qp if hh == 0 else pltpu.roll(qp, 64, 1)
                kb = kp if hh == 0 else pltpu.roll(kp, 64, 1)
                augq = jnp.where(lane == 64, f1, jnp.where(lane == 65, f2, jnp.where(lane == 66, f3, ones)))
                augk = jnp.where(lane == 67, -f1, jnp.where(lane == 68, -f2, jnp.where(lane == 69, -f3, ones_k)))
                qa_ref[h] = jnp.where(lane < DH, qb, augq).astype(BF16)
                ka_ref[h] = jnp.where(lane < DH, kb, augk).astype(BF16)

    hm = pl.BlockSpec((NH, tm, LANES), lambda i: (0, i, 0))
    return pl.pallas_call(
        body, name="fwd_decay", grid=(S // tm,),
        out_shape=(_sds((NH, S, LANES), BF16), _sds((NH, S, LANES), BF16), _sds((S, LANES), F32)),
        in_specs=[_rows(tm, LANES), _rows(tm, AW), _rows(tm, AW)], out_specs=(hm, hm, _rows(tm, LANES)),
        scratch_shapes=[pltpu.VMEM((8, LANES), F32)], compiler_params=_cp(("arbitrary",)),
    )(fgb, qh, kh)


def _causal(t):
    return lax.broadcasted_iota(jnp.int32, (t, t), 0) >= lax.broadcasted_iota(jnp.int32, (t, t), 1)


def _causal_t(t):
    return lax.broadcasted_iota(jnp.int32, (t, t), 0) <= lax.broadcasted_iota(jnp.int32, (t, t), 1)


SKIP = -106.0


def _block_loops(first, step, needed, run):
    def both(j):
        return jnp.logical_and(needed(0, j), needed(1, j))

    def walk(heads):
        def go(j):
            run(j, heads)
            return j + step
        return go

    j = lax.while_loop(both, walk((0, 1)), first)
    lax.while_loop(functools.partial(needed, 0), walk((0,)), j)
    lax.while_loop(functools.partial(needed, 1), walk((1,)), j)


def _skip_tables(f, qg, kg, tq):
    fs = f[0::tq, :NH].T
    fe = f[tq - 1::tq, :NH].T
    bound = 2.0 * 1.03 * DH ** 0.5 * jnp.max(jnp.abs(qg)) * jnp.max(jnp.abs(kg))
    return fs, fe, bound.reshape(1, 1)


SMEM_SPEC = pl.BlockSpec(memory_space=pltpu.SMEM)


def _causal_rect(rows, cols, col0):
    return (lax.broadcasted_iota(jnp.int32, (rows, cols), 0)
            >= lax.broadcasted_iota(jnp.int32, (rows, cols), 1) + col0)


def _chunk(tq):
    return tq


def _attn_fwd(qa, ka, v, fs, fe, bd, tq):
    S = qa.shape[1]
    nq = S // tq
    tc = _chunk(tq)

    def body(fs_ref, fe_ref, bd_ref, qa_ref, ka_ref, v_ref, o_ref, lsec_ref, lser_ref, m_ref, l_ref, acc_ref):
        pr, i = pl.program_id(0), pl.program_id(1)
        sel_a = _lane((tq, LANES)) < DH
        m_ref[...] = jnp.full(m_ref.shape, NEG, F32)
        l_ref[...] = jnp.zeros_like(l_ref)
        acc_ref[...] = jnp.zeros_like(acc_ref)

        def kv_step(j, heads, masked=False):
            for c0 in range(0, tq, tc):
                start = pl.multiple_of(j * tq + c0, tc)
                vb = v_ref[pl.ds(start, tc), :]
                for hh in heads:
                    s = lax.dot_general(qa_ref[hh], ka_ref[hh, pl.ds(start, tc), :], NT, preferred_element_type=F32)
                    if masked:
                        s = jnp.where(_causal_rect(tq, tc, c0), s, NEG)
                    m_prev = m_ref[hh]
                    m_new = jnp.maximum(m_prev, jnp.max(s, axis=1, keepdims=True))
                    alpha = jnp.exp(m_prev - m_new)
                    p = jnp.exp(s - jnp.tile(m_new, (1, tc // LANES)))
                    l_ref[hh] = alpha * l_ref[hh] + jnp.sum(p, axis=1, keepdims=True)
                    m_ref[hh] = m_new
                    acc_ref[hh] = alpha * acc_ref[hh] + jnp.dot(p.astype(BF16), vb, preferred_element_type=F32)

        def needed(hh, j):
            top = fs_ref[2 * pr + hh, i] + bd_ref[0, 0]
            return jnp.logical_and(j >= 0, top - fe_ref[2 * pr + hh, jnp.maximum(j, 0)] >= SKIP)

        kv_step(i, (0, 1), masked=True)
        _block_loops(i - 1, -1, needed, kv_step)
        o_ref[...] = jnp.where(sel_a, acc_ref[0] / l_ref[0], acc_ref[1] / l_ref[1])
        lse_a = m_ref[0] + jnp.log(l_ref[0])
        lse_b = m_ref[1] + jnp.log(l_ref[1])
        lsec_ref[0] = lse_a
        lsec_ref[1] = lse_b
        row = lax.broadcasted_iota(jnp.int32, (8, tq), 0)
        lser_ref[0, 0] = jnp.where(row == 0, lse_a.T[0:8, :], lse_b.T[0:8, :])

    return pl.pallas_call(
        body, name="attn_fwd", grid=(NH // 2, nq),
        out_shape=(_sds((S, AW), F32), _sds((NH, S, LANES), F32), _sds((NH // 2, nq, 8, tq), F32)),
        in_specs=[SMEM_SPEC, SMEM_SPEC, SMEM_SPEC,
                  pl.BlockSpec((2, tq, LANES), lambda p, i: (p, i, 0)),
                  pl.BlockSpec((2, S, LANES), lambda p, i: (p, 0, 0)),
                  pl.BlockSpec((S, LANES), lambda p, i: (0, p))],
        out_specs=(pl.BlockSpec((tq, LANES), lambda p, i: (i, p)),
                   pl.BlockSpec((2, tq, LANES), lambda p, i: (p, i, 0)),
                   pl.BlockSpec((1, 1, 8, tq), lambda p, i: (p, i, 0, 0))),
        scratch_shapes=[pltpu.VMEM((2, tq, LANES), F32), pltpu.VMEM((2, tq, LANES), F32),
                        pltpu.VMEM((2, tq, LANES), F32)],
        compiler_params=_cp(("parallel", "parallel")),
    )(fs, fe, bd, qa, ka, v)


HALO = 32
CHUNK_ROWS = 64


def _halo_prev(tm):
    return pl.BlockSpec((HALO, CW), lambda i: (jnp.maximum(i * (tm // HALO) - 1, 0), 0))


def _fwd_conv(u0, w32, cb, lng, lnb, beta_c, tm):
    S = u0.shape[0]

    def body(cur_ref, prev_ref, w_ref, cb_ref, lng_ref, lnb_ref, beta_ref, u1_ref, mc_ref, ext_ref):
        i = pl.program_id(0)
        ext_ref[0:HALO, :] = jnp.where(i == 0, 0.0, prev_ref[...])
        ext_ref[HALO:, :] = cur_ref[...]
        for r0 in range(0, tm, CHUNK_ROWS):
            acc = jnp.zeros((CHUNK_ROWS, CW), F32) + cb_ref[...]
            for j in range(KC):
                acc = acc + w_ref[j:j + 1, :] * ext_ref[r0 + 2 + j:r0 + 2 + j + CHUNK_ROWS, :]
            u1_ref[r0:r0 + CHUNK_ROWS, :] = acc
        u1 = u1_ref[...]
        mu = jnp.mean(u1, axis=-1, keepdims=True)
        d = u1 - mu
        rstd = lax.rsqrt(jnp.mean(d * d, axis=-1, keepdims=True) + EPS)
        u2 = d * rstd * lng_ref[...] + lnb_ref[...]
        u3 = u2 * _sigmoid(u2)
        rc = lax.rsqrt(jnp.mean(u3 * u3, axis=-1, keepdims=True) + EPS)
        mc_ref[...] = (u3 * rc * beta_ref[...]).astype(BF16)

    return pl.pallas_call(
        body, name="fwd_conv", grid=(S // tm,),
        out_shape=(_sds((S, CW), F32), _sds((S, CW), BF16)),
        in_specs=[_rows(tm, CW), _halo_prev(tm), _const((HALO, CW)), _const((1, CW)), _const((1, CW)), _const((1, CW)),
                  _const((1, CW))],
        out_specs=(_rows(tm, CW), _rows(tm, CW)),
        scratch_shapes=[pltpu.VMEM((tm + HALO, CW), F32)], compiler_params=_cp(("parallel",)),
    )(u0, u0, w32, cb, lng, lnb, beta_c)


def _fwd_out(o_attn, mc, x, mod8, n2g, beta_a, w_out, tm):
    S = x.shape[0]

    def body(o_ref, mc_ref, x_ref, mod_ref, n2g_ref, beta_ref, w_ref, mg_ref, ob_ref, x2_ref, h2_ref):
        ov = o_ref[...]
        ra = lax.rsqrt(jnp.mean(ov * ov, axis=-1, keepdims=True) + EPS)
        ma = (ov * ra * beta_ref[...]).astype(BF16)
        mcv = mc_ref[...]
        mg_ref[:, 0:AW] = ma
        mg_ref[:, AW:D] = mcv
        o = (jnp.dot(ma, w_ref[0:AW, :], preferred_element_type=F32)
             + jnp.dot(mcv, w_ref[AW:D, :], preferred_element_type=F32))
        ob_ref[...] = o.astype(BF16)
        x2 = x_ref[...] + mod_ref[2:3, :] * o
        x2_ref[...] = x2
        r2 = lax.rsqrt(jnp.mean(x2 * x2, axis=-1, keepdims=True) + EPS)
        h2_ref[...] = ((x2 * r2) * (n2g_ref[...] * (1.0 + mod_ref[4:5, :])) + mod_ref[3:4, :]).astype(BF16)

    return pl.pallas_call(
        body, name="fwd_out", grid=(S // tm,),
        out_shape=(_sds((S, D), BF16), _sds((S, D), BF16), _sds((S, D), F32), _sds((S, D), BF16)),
        in_specs=[_rows(tm, AW), _rows(tm, CW), _rows(tm, D), _const((8, D)), _const((1, D)), _const((1, AW)),
                  _const((D, D))],
        out_specs=(_rows(tm, D), _rows(tm, D), _rows(tm, D), _rows(tm, D)),
        compiler_params=_cp(("parallel",)),
    )(o_attn, mc, x, mod8, n2g, beta_a, w_out)


def _fwd_ffn(h2, w1, w2, x2, tgt, mod8, tm):
    S = h2.shape[0]
    nk = w1.shape[0]
    bf = w1.shape[2]

    def body(h2_ref, w1_ref, w2_ref, x2_ref, tgt_ref, mod_ref, r_ref, dy_ref, loss_ref, dg2_ref, acc_ref):
        i, k = pl.program_id(0), pl.program_id(1)

        @pl.when((i == 0) & (k == 0))
        def _():
            loss_ref[...] = jnp.zeros_like(loss_ref)
            dg2_ref[...] = jnp.zeros_like(dg2_ref)

        r = jnp.maximum(jnp.dot(h2_ref[...], w1_ref[0], preferred_element_type=F32), 0.0)
        r_ref[...] = r.astype(BF16)
        part = jnp.dot((r * r).astype(BF16), w2_ref[0], preferred_element_type=F32)

        @pl.when(k == 0)
        def _():
            acc_ref[...] = part

        @pl.when(k > 0)
        def _():
            acc_ref[...] += part

        @pl.when(k == nk - 1)
        def _():
            f2 = acc_ref[...]
            e = x2_ref[...] + mod_ref[5:6, :] * f2 - tgt_ref[...]
            dy = e * (1.0 / D)
            dy_ref[...] = dy
            loss_ref[...] += 0.5 * jnp.sum(jnp.sum(e * dy, axis=1, keepdims=True), axis=0, keepdims=True)
            dg2_ref[...] += jnp.sum((dy * f2).reshape(tm // 8, 8, D), axis=0)

    return pl.pallas_call(
        body, name="fwd_ffn", grid=(S // tm, nk),
        out_shape=(_sds((S, DFF), BF16), _sds((S, D), F32), _sds((8, LANES), F32), _sds((8, D), F32)),
        in_specs=[pl.BlockSpec((tm, D), lambda i, k: (i, 0)), pl.BlockSpec((1, D, bf), lambda i, k: (k, 0, 0)),
                  pl.BlockSpec((1, bf, D), lambda i, k: (k, 0, 0)), pl.BlockSpec((tm, D), lambda i, k: (i, 0)),
                  pl.BlockSpec((tm, D), lambda i, k: (i, 0)), pl.BlockSpec((8, D), lambda i, k: (0, 0))],
        out_specs=(pl.BlockSpec((tm, bf), lambda i, k: (i, k)), pl.BlockSpec((tm, D), lambda i, k: (i, 0)),
                   pl.BlockSpec((8, LANES), lambda i, k: (0, 0)), pl.BlockSpec((8, D), lambda i, k: (0, 0))),
        scratch_shapes=[pltpu.VMEM((tm, D), F32)], compiler_params=_cp(("arbitrary", "arbitrary")),
    )(h2, w1, w2, x2, tgt, mod8)


def _bwd_ffn(dy, mod8, r, w1, w2, tm):
    S = dy.shape[0]
    nk = w1.shape[0]
    bf = w1.shape[2]

    def body(dy_ref, mod_ref, r_ref, w1_ref, w2_ref, df2_ref, df1_ref, dh2_ref):
        k = pl.program_id(1)
        df2 = (dy_ref[...] * mod_ref[5:6, :]).astype(BF16)

        @pl.when(k == 0)
        def _():
            df2_ref[...] = df2

        da = lax.dot_general(df2, w2_ref[0], NT, preferred_element_type=F32)
        df1 = (da * (2.0 * r_ref[...].astype(F32))).astype(BF16)
        df1_ref[...] = df1
        part = lax.dot_general(df1, w1_ref[0], NT, preferred_element_type=F32)

        @pl.when(k == 0)
        def _():
            dh2_ref[...] = part

        @pl.when(k > 0)
        def _():
            dh2_ref[...] += part

    return pl.pallas_call(
        body, name="bwd_ffn", grid=(S // tm, nk),
        out_shape=(_sds((S, D), BF16), _sds((S, DFF), BF16), _sds((S, D), F32)),
        in_specs=[pl.BlockSpec((tm, D), lambda i, k: (i, 0)), pl.BlockSpec((8, D), lambda i, k: (0, 0)),
                  pl.BlockSpec((tm, bf), lambda i, k: (i, k)), pl.BlockSpec((1, D, bf), lambda i, k: (k, 0, 0)),
                  pl.BlockSpec((1, bf, D), lambda i, k: (k, 0, 0))],
        out_specs=(pl.BlockSpec((tm, D), lambda i, k: (i, 0)), pl.BlockSpec((tm, bf), lambda i, k: (i, k)),
                   pl.BlockSpec((tm, D), lambda i, k: (i, 0))),
        compiler_params=_cp(("parallel", "arbitrary")),
    )(dy, mod8, r, w1, w2)


def _wgrad(a, b, name, square_a=False, tk=512, bm=1024, bn=1024):
    S, M = a.shape
    N = b.shape[1]
    bm, bn, tk = min(bm, M), min(bn, N), min(tk, S)

    def body(a_ref, b_ref, o_ref):
        av = a_ref[...]
        if square_a:
            af = av.astype(F32)
            av = (af * af).astype(BF16)
        part = lax.dot_general(av, b_ref[...], TN, preferred_element_type=F32)

        @pl.when(pl.program_id(2) == 0)
        def _():
            o_ref[...] = part

        @pl.when(pl.program_id(2) > 0)
        def _():
            o_ref[...] += part

    return pl.pallas_call(
        body, name=name, grid=(M // bm, N // bn, S // tk), out_shape=_sds((M, N), F32),
        in_specs=[pl.BlockSpec((tk, bm), lambda mi, ni, k: (k, mi)), pl.BlockSpec((tk, bn), lambda mi, ni, k: (k, ni))],
        out_specs=pl.BlockSpec((bm, bn), lambda mi, ni, k: (mi, ni)),
        compiler_params=_cp(("parallel", "parallel", "arbitrary")),
    )(a, b)


def _colsum8(t):
    return jnp.sum(t.reshape(t.shape[0] // 8, 8, t.shape[1]), axis=0)


def _bwd_mid(dh2, dy, x2, ob, o_attn, u1, mod8, n2g, beta_a, beta_c, lng, lnb, w_out, tm, ex=None):
    S = dy.shape[0]

    def body(dh2_ref, dy_ref, x2_ref, ob_ref, oa_ref, u1_ref, mod_ref, n2g_ref, ba_ref, bc_ref, lng_ref, lnb_ref, w_ref,
             dx2_ref, do_ref, doa_ref, du1_ref, acc_d_ref, acc_h_ref):
        @pl.when(pl.program_id(0) == 0)
        def _():
            acc_d_ref[...] = jnp.zeros_like(acc_d_ref)
            acc_h_ref[...] = jnp.zeros_like(acc_h_ref)

        x2 = x2_ref[...]
        dh2 = dh2_ref[...]
        r2 = lax.rsqrt(jnp.mean(x2 * x2, axis=-1, keepdims=True) + EPS)
        xn2 = x2 * r2
        gain = n2g_ref[...] * (1.0 + mod_ref[4:5, :])
        dxn = dh2 * gain
        dx2 = dy_ref[...] + r2 * (dxn - xn2 * jnp.mean(dxn * xn2, axis=-1, keepdims=True))
        dx2_ref[...] = dx2
        t = dh2 * xn2
        acc_d_ref[0] += _colsum8(dh2)
        acc_d_ref[1] += _colsum8(t * n2g_ref[...])
        acc_d_ref[2] += _colsum8(t * (1.0 + mod_ref[4:5, :]))
        acc_d_ref[3] += _colsum8(dx2 * ob_ref[...].astype(F32))
        do = (dx2 * mod_ref[2:3, :]).astype(BF16)
        do_ref[...] = do
        dma = lax.dot_general(do, w_ref[0:AW, :], NT, preferred_element_type=F32)
        dmc = lax.dot_general(do, w_ref[AW:D, :], NT, preferred_element_type=F32)
        ov = oa_ref[...]
        ra = lax.rsqrt(jnp.mean(ov * ov, axis=-1, keepdims=True) + EPS)
        on = ov * ra
        acc_h_ref[0] += _colsum8(dma * on)
        don = dma * ba_ref[...]
        doa_ref[...] = (ra * (don - on * jnp.mean(don * on, axis=-1, keepdims=True))).astype(BF16)
        u1 = u1_ref[...]
        mu = jnp.mean(u1, axis=-1, keepdims=True)
        d = u1 - mu
        rstd = lax.rsqrt(jnp.mean(d * d, axis=-1, keepdims=True) + EPS)
        uh = d * rstd
        u2 = uh * lng_ref[...] + lnb_ref[...]
        sg = _sigmoid(u2)
        u3 = u2 * sg
        rc = lax.rsqrt(jnp.mean(u3 * u3, axis=-1, keepdims=True) + EPS)
        u3n = u3 * rc
        acc_h_ref[1] += _colsum8(dmc * u3n)
        du3n = dmc * bc_ref[...]
        du3 = rc * (du3n - u3n * jnp.mean(du3n * u3n, axis=-1, keepdims=True))
        du2 = du3 * (sg * (1.0 + u2 * (1.0 - sg)))
        acc_h_ref[2] += _colsum8(du2 * uh)
        acc_h_ref[3] += _colsum8(du2)
        duh = du2 * lng_ref[...]
        du1_ref[...] = rstd * (duh - jnp.mean(duh, axis=-1, keepdims=True)
                               - uh * jnp.mean(duh * uh, axis=-1, keepdims=True))

    outs = pl.pallas_call(
        _hosted(body, ex, 13, 6, 0, S // tm), name="bwd_mid", grid=(S // tm,),
        out_shape=(_sds((S, D), F32), _sds((S, D), BF16), _sds((S, AW), BF16), _sds((S, CW), F32),
                   _sds((4, 8, D), F32), _sds((4, 8, AW), F32), *(ex.out_shapes if ex else [])),
        in_specs=[_rows(tm, D), _rows(tm, D), _rows(tm, D), _rows(tm, D), _rows(tm, AW), _rows(tm, CW), _const((8, D)),
                  _const((1, D)), _const((1, AW)), _const((1, CW)), _const((1, CW)), _const((1, CW)), _const((D, D)),
                  *(ex.in_specs if ex else [])],
        out_specs=(_rows(tm, D), _rows(tm, D), _rows(tm, AW), _rows(tm, CW), _const((4, 8, D)), _const((4, 8, AW)),
                   *(ex.out_specs if ex else [])),
        scratch_shapes=ex.scratch if ex else [],
        compiler_params=_cp(("arbitrary",)),
    )(dh2, dy, x2, ob, o_attn, u1, mod8, n2g, beta_a, beta_c, lng, lnb, w_out, *(ex.srcs if ex else []))
    return outs[:6], list(outs[6:])


def _attn_bwd_dq(qa, ka, v, do, o_attn, lsec, fs, fe, bd, tq):
    S = qa.shape[1]
    nq = S // tq
    tc = _chunk(tq)

    def body(fs_ref, fe_ref, bd_ref, qa_ref, ka_ref, v_ref, do_ref, o_ref, lse_ref, dqa_ref, dr_ref, acc_ref):
        pr, i = pl.program_id(0), pl.program_id(1)
        sel_a = _lane((tq, LANES)) < DH
        dov = do_ref[...]
        prod = dov.astype(F32) * o_ref[...]
        zero = jnp.zeros_like(prod)
        deltas = [jnp.broadcast_to(jnp.sum(jnp.where(sel_a, prod, zero), axis=1, keepdims=True), (tq, LANES)),
                  jnp.broadcast_to(jnp.sum(jnp.where(sel_a, zero, prod), axis=1, keepdims=True), (tq, LANES))]
        zb = jnp.zeros_like(dov)
        dos = [jnp.where(sel_a, dov, zb), jnp.where(sel_a, zb, dov)]
        acc_ref[...] = jnp.zeros_like(acc_ref)

        def kv_step(j, heads, masked=False):
            for c0 in range(0, tq, tc):
                start = pl.multiple_of(j * tq + c0, tc)
                vb = v_ref[pl.ds(start, tc), :]
                for hh in heads:
                    kb = ka_ref[hh, pl.ds(start, tc), :]
                    s = lax.dot_general(qa_ref[hh], kb, NT, preferred_element_type=F32)
                    p = jnp.exp(s - jnp.tile(lse_ref[hh], (1, tc // LANES)))
                    if masked:
                        p = jnp.where(_causal_rect(tq, tc, c0), p, 0.0)
                    dp = lax.dot_general(dos[hh], vb, NT, preferred_element_type=F32)
                    ds = p * (dp - jnp.tile(deltas[hh], (1, tc // LANES)))
                    acc_ref[hh] += jnp.dot(ds.astype(BF16), kb, preferred_element_type=F32)

        def needed(hh, j):
            top = fs_ref[2 * pr + hh, i] + bd_ref[0, 0]
            return jnp.logical_and(j >= 0, top - fe_ref[2 * pr + hh, jnp.maximum(j, 0)] >= SKIP)

        kv_step(i, (0, 1), masked=True)
        _block_loops(i - 1, -1, needed, kv_step)
        dqa_ref[...] = acc_ref[...]
        row = lax.broadcasted_iota(jnp.int32, (8, tq), 0)
        da = deltas[0].T[0:8, :]
        db = deltas[1].T[0:8, :]
        dr_ref[0, 0] = jnp.where(row == 0, da, db)

    return pl.pallas_call(
        body, name="attn_bwd_dq", grid=(NH // 2, nq),
        out_shape=(_sds((NH, S, LANES), F32), _sds((NH // 2, nq, 8, tq), F32)),
        in_specs=[SMEM_SPEC, SMEM_SPEC, SMEM_SPEC,
                  pl.BlockSpec((2, tq, LANES), lambda p, i: (p, i, 0)),
                  pl.BlockSpec((2, S, LANES), lambda p, i: (p, 0, 0)),
                  pl.BlockSpec((S, LANES), lambda p, i: (0, p)),
                  pl.BlockSpec((tq, LANES), lambda p, i: (i, p)),
                  pl.BlockSpec((tq, LANES), lambda p, i: (i, p)),
                  pl.BlockSpec((2, tq, LANES), lambda p, i: (p, i, 0))],
        out_specs=(pl.BlockSpec((2, tq, LANES), lambda p, i: (p, i, 0)),
                   pl.BlockSpec((1, 1, 8, tq), lambda p, i: (p, i, 0, 0))),
        scratch_shapes=[pltpu.VMEM((2, tq, LANES), F32)],
        compiler_params=_cp(("parallel", "parallel")),
    )(fs, fe, bd, qa, ka, v, do, o_attn, lsec)


def _attn_bwd_dkv(qa, ka, v, do, lser, dr, fs, fe, bd, tq):
    S = qa.shape[1]
    nq = S // tq
    tc = _chunk(tq)

    def body(fs_ref, fe_ref, bd_ref, ka_ref, v_ref, qa_ref, do_ref, lse_ref, dr_ref, dka_ref, dv_ref, acck_ref,
             accv_ref):
        pr, j = pl.program_id(0), pl.program_id(1)
        sel_a = _lane((tq, LANES)) < DH
        vv = v_ref[...]
        zb = jnp.zeros_like(vv)
        vs = [jnp.where(sel_a, vv, zb), jnp.where(sel_a, zb, vv)]
        acck_ref[...] = jnp.zeros_like(acck_ref)
        accv_ref[...] = jnp.zeros_like(accv_ref)

        def q_step(i, heads, masked=False):
            lse8 = lse_ref[0, i]
            dr8 = dr_ref[0, i]
            for c0 in range(0, tq, tc):
                start = pl.multiple_of(i * tq + c0, tc)
                dob = do_ref[pl.ds(start, tc), :]
                for hh in heads:
                    qb = qa_ref[hh, pl.ds(start, tc), :]
                    st = lax.dot_general(ka_ref[hh], qb, NT, preferred_element_type=F32)
                    pt = jnp.exp(st - lse8[hh:hh + 1, c0:c0 + tc])
                    if masked:
                        keep = (lax.broadcasted_iota(jnp.int32, (tq, tc), 0)
                                <= lax.broadcasted_iota(jnp.int32, (tq, tc), 1) + c0)
                        pt = jnp.where(keep, pt, 0.0)
                    accv_ref[hh] += jnp.dot(pt.astype(BF16), dob, preferred_element_type=F32)
                    dpt = lax.dot_general(vs[hh], dob, NT, preferred_element_type=F32)
                    dst = pt * (dpt - dr8[hh:hh + 1, c0:c0 + tc])
                    acck_ref[hh] += jnp.dot(dst.astype(BF16), qb, preferred_element_type=F32)

        def needed(hh, i):
            top = fs_ref[2 * pr + hh, jnp.minimum(i, nq - 1)] + bd_ref[0, 0]
            return jnp.logical_and(i < nq, top - fe_ref[2 * pr + hh, j] >= SKIP)

        q_step(j, (0, 1), masked=True)
        _block_loops(j + 1, 1, needed, q_step)
        dka_ref[...] = acck_ref[...]
        dv_ref[...] = jnp.where(sel_a, accv_ref[0], accv_ref[1]).astype(BF16)

    return pl.pallas_call(
        body, name="attn_bwd_dkv", grid=(NH // 2, nq),
        out_shape=(_sds((NH, S, LANES), F32), _sds((S, AW), BF16)),
        in_specs=[SMEM_SPEC, SMEM_SPEC, SMEM_SPEC,
                  pl.BlockSpec((2, tq, LANES), lambda p, j: (p, j, 0)),
                  pl.BlockSpec((tq, LANES), lambda p, j: (j, p)),
                  pl.BlockSpec((2, S, LANES), lambda p, j: (p, 0, 0)),
                  pl.BlockSpec((S, LANES), lambda p, j: (0, p)),
                  pl.BlockSpec((1, nq, 8, tq), lambda p, j: (p, 0, 0, 0)),
                  pl.BlockSpec((1, nq, 8, tq), lambda p, j: (p, 0, 0, 0))],
        out_specs=(pl.BlockSpec((2, tq, LANES), lambda p, j: (p, j, 0)),
                   pl.BlockSpec((tq, LANES), lambda p, j: (j, p))),
        scratch_shapes=[pltpu.VMEM((2, tq, LANES), F32), pltpu.VMEM((2, tq, LANES), F32)],
        compiler_params=_cp(("parallel", "parallel")),
    )(fs, fe, bd, ka, v, qa, do, lser, dr)


def _bwd_conv(du1, u0, alin, agate, w32, tm, ex=None):
    S = du1.shape[0]
    nt = S // tm

    def body(du_ref, dun_ref, u0_ref, u0p_ref, alin_ref, agate_ref, w_ref,
             dalin_ref, dagate_ref, dw_ref, db_ref, extd_ref, extu_ref, du0_ref):
        i = pl.program_id(0)

        @pl.when(i == 0)
        def _():
            dw_ref[...] = jnp.zeros_like(dw_ref)
            db_ref[...] = jnp.zeros_like(db_ref)

        extd_ref[0:tm, :] = du_ref[...]
        extd_ref[tm:, :] = jnp.where(i == nt - 1, 0.0, dun_ref[...])
        extu_ref[0:HALO, :] = jnp.where(i == 0, 0.0, u0p_ref[...])
        extu_ref[HALO:, :] = u0_ref[...]
        db_ref[...] += _colsum8(du_ref[...])
        for r0 in range(0, tm, CHUNK_ROWS):
            duc = du_ref[r0:r0 + CHUNK_ROWS, :]
            acc = jnp.zeros((CHUNK_ROWS, CW), F32)
            for j in range(KC):
                acc = acc + w_ref[j:j + 1, :] * extd_ref[r0 + 30 - j:r0 + 30 - j + CHUNK_ROWS, :]
                dw_ref[j] += _colsum8(duc * extu_ref[r0 + 2 + j:r0 + 2 + j + CHUNK_ROWS, :])
            du0_ref[r0:r0 + CHUNK_ROWS, :] = acc
        du0 = du0_ref[...]
        al = alin_ref[...].astype(F32)
        sg = _sigmoid(agate_ref[...].astype(F32))
        dalin_ref[...] = (du0 * sg).astype(BF16)
        dagate_ref[...] = (du0 * al * sg * (1.0 - sg)).astype(BF16)

    nxt = pl.BlockSpec((HALO, CW), lambda i: (jnp.minimum((i + 1) * (tm // HALO), S // HALO - 1), 0))
    outs = pl.pallas_call(
        _hosted(body, ex, 7, 4, 3, nt), name="bwd_conv", grid=(nt,),
        out_shape=(_sds((S, CW), BF16), _sds((S, CW), BF16), _sds((HALO, 8, CW), F32), _sds((8, CW), F32),
                   *(ex.out_shapes if ex else [])),
        in_specs=[_rows(tm, CW), nxt, _rows(tm, CW), _halo_prev(tm), _rows(tm, CW), _rows(tm, CW), _const((HALO, CW)),
                  *(ex.in_specs if ex else [])],
        out_specs=(_rows(tm, CW), _rows(tm, CW), _const((HALO, 8, CW)), _const((8, CW)), *(ex.out_specs if ex else [])),
        scratch_shapes=[pltpu.VMEM((tm + HALO, CW), F32), pltpu.VMEM((tm + HALO, CW), F32), pltpu.VMEM((tm, CW), F32),
                        *(ex.scratch if ex else [])],
        compiler_params=_cp(("arbitrary",)),
    )(du1, du1, u0, u0, alin, agate, w32, *(ex.srcs if ex else []))
    return outs[:4], list(outs[4:])


def _bwd_qk(dqa, dka, qn, kn, rq, rk, fgb, qg512, kg512, e512, et512, tm):
    S = qn.shape[0]
    nt = S // tm

    def body(dqa_ref, dka_ref, qn_ref, kn_ref, rq_ref, rk_ref, fgb_ref, qg_ref, kg_ref, e_ref, et_ref,
             dq_ref, dk_ref, dfg_ref, accg_ref, accb_ref, carry_ref):
        @pl.when(pl.program_id(0) == 0)
        def _():
            carry_ref[...] = jnp.zeros_like(carry_ref)
            accg_ref[...] = jnp.zeros_like(accg_ref)
            accb_ref[...] = jnp.zeros_like(accb_ref)

        lane = _lane((tm, LANES))
        sel_a = lane < DH
        df = jnp.zeros((tm, LANES), F32)
        for h in range(NH):
            col = dqa_ref[h][:, 64:65] - dka_ref[h][:, 67:68]
            df = jnp.where(lane == h, col, df)
        tri = (lax.broadcasted_iota(jnp.int32, (tm, tm), 0) <= lax.broadcasted_iota(jnp.int32, (tm, tm), 1)
               ).astype(F32).astype(BF16)
        dlf = _dot3(tri, df) + carry_ref[0:1, :]
        carry_ref[...] = jnp.broadcast_to(dlf[0:1, :], carry_ref.shape)
        dfg = jnp.where(lane < NH, dlf * _sigmoid(-fgb_ref[...]), 0.0)
        dfg_ref[...] = dfg.astype(BF16)
        accb_ref[...] += _colsum8(dfg)

        def norm_bwd(src_ref, n_ref, r_ref, g_ref, scale, slot):
            pairs = []
            for p in range(NH // 2):
                b = pltpu.roll(src_ref[2 * p + 1], 64, 1)
                pairs.append(jnp.where(sel_a, src_ref[2 * p], b))
            dh = jnp.concatenate(pairs, axis=1) * scale
            tn = n_ref[...].astype(F32)
            accg_ref[slot] += _colsum8(dh * tn)
            dn = dh * g_ref[...]
            mean = _dot2(dn * tn, e_ref[...]) * (1.0 / DH)
            corr = _dot2(mean, et_ref[...])
            rf = _dot2(r_ref[...], et_ref[...])
            return (rf * (dn - tn * corr)).astype(BF16)

        dq_ref[...] = norm_bwd(dqa_ref, qn_ref, rq_ref, qg_ref, DH ** -0.5, 0)
        dk_ref[...] = norm_bwd(dka_ref, kn_ref, rk_ref, kg_ref, 1.0, 1)

    rev = lambda n: pl.BlockSpec((tm, n), lambda i: (nt - 1 - i, 0))
    hm = pl.BlockSpec((NH, tm, LANES), lambda i: (0, nt - 1 - i, 0))
    dq, dk, dfg, accg, accb = pl.pallas_call(
        body, name="bwd_qk", grid=(nt,),
        out_shape=(_sds((S, AW), BF16), _sds((S, AW), BF16), _sds((S, LANES), BF16), _sds((2, 8, AW), F32),
                   _sds((8, LANES), F32)),
        in_specs=[hm, hm, rev(AW), rev(AW), rev(LANES), rev(LANES), rev(LANES), _const((1, AW)), _const((1, AW)),
                  _const((AW, LANES)), _const((LANES, AW))],
        out_specs=(rev(AW), rev(AW), rev(LANES), _const((2, 8, AW)), _const((8, LANES))),
        scratch_shapes=[pltpu.VMEM((8, LANES), F32)], compiler_params=_cp(("arbitrary",)),
    )(dqa, dka, qn, kn, rq, rk, fgb, qg512, kg512, e512, et512)
    return dq, dk, dfg, accg, accb


def _bwd_in(dq, dk, dv, dalin, dagate, dfg, w_in_p, x, dx2, mod8, n1g, tm):
    S = x.shape[0]

    def body(dq_ref, dk_ref, dv_ref, dal_ref, dag_ref, dfg_ref, w_ref, x_ref, dx2_ref, mod_ref, n1g_ref,
             dx_ref, acc_ref):
        @pl.when(pl.program_id(0) == 0)
        def _():
            acc_ref[...] = jnp.zeros_like(acc_ref)

        def part(ref, a, b):
            return lax.dot_general(ref[...], w_ref[:, a:b], NT, preferred_element_type=F32)

        dh = (part(dq_ref, 0, 512) + part(dk_ref, 512, 1024) + part(dv_ref, 1024, 1536) + part(dal_ref, 1536, 2048)
              + part(dag_ref, 2048, 2560) + part(dfg_ref, 2560, NP))
        xv = x_ref[...]
        r1 = lax.rsqrt(jnp.mean(xv * xv, axis=-1, keepdims=True) + EPS)
        xn = xv * r1
        gain = n1g_ref[...] * (1.0 + mod_ref[1:2, :])
        t = dh * xn
        acc_ref[0] += _colsum8(dh)
        acc_ref[1] += _colsum8(t * n1g_ref[...])
        acc_ref[2] += _colsum8(t * (1.0 + mod_ref[1:2, :]))
        dxn = dh * gain
        dx_ref[...] = dx2_ref[...] + r1 * (dxn - xn * jnp.mean(dxn * xn, axis=-1, keepdims=True))

    return pl.pallas_call(
        body, name="bwd_in", grid=(S // tm,),
        out_shape=(_sds((S, D), F32), _sds((3, 8, D), F32)),
        in_specs=[_rows(tm, AW), _rows(tm, AW), _rows(tm, AW), _rows(tm, CW), _rows(tm, CW), _rows(tm, LANES),
                  _const((D, NP)), _rows(tm, D), _rows(tm, D), _const((8, D)), _const((1, D))],
        out_specs=(_rows(tm, D), _const((3, 8, D))),
        compiler_params=_cp(("arbitrary",)),
    )(dq, dk, dv, dalin, dagate, dfg, w_in_p, x, dx2, mod8, n1g)


def _adam(w, g, m, v):
    m_new = B1 * m + (1.0 - B1) * g
    v_new = B2 * v + (1.0 - B2) * (g * g)
    m_hat = m_new / (1.0 - B1 ** STEP)
    v_hat = v_new / (1.0 - B2 ** STEP)
    delta = -LR * (m_hat / (jnp.sqrt(v_hat) + AEPS) + WD * w)
    return delta, m_new, v_new


def _reduce_adamw(slots, w, m, v, name, tr=256):
    ns, R, C = slots.shape
    tr = tr if R % tr == 0 else R

    def body(s_ref, w_ref, m_ref, v_ref, g_ref, d_ref, mo_ref, vo_ref):
        g = s_ref[0].astype(F32)
        for k in range(1, ns):
            g = g + s_ref[k].astype(F32)
        g_ref[...] = g
        d_ref[...], mo_ref[...], vo_ref[...] = _adam(w_ref[...], g, m_ref[...], v_ref[...])

    blk = pl.BlockSpec((tr, C), lambda i: (i, 0))
    return pl.pallas_call(
        body, name=name, grid=(R // tr,), out_shape=tuple(_sds((R, C), F32) for _ in range(4)),
        in_specs=[pl.BlockSpec((ns, tr, C), lambda i: (0, i, 0)), blk, blk, blk], out_specs=(blk, blk, blk, blk),
        compiler_params=_cp(("parallel",)),
    )(slots, w, m, v)


def _pair_adamw(slots, w, m, v, name, tr=256):
    ns, R, C = slots.shape
    tr = tr if R % tr == 0 else R
    nt = R // tr

    def body(s_ref, w_ref, m_ref, v_ref, g_ref, d_ref, mo_ref, vo_ref, mine_ref, theirs_ref, send_sems, recv_sems):
        i = pl.program_id(0)
        part = s_ref[0].astype(F32)
        for k in range(1, ns):
            part = part + s_ref[k].astype(F32)
        mine_ref[i] = part
        swap = pltpu.make_async_remote_copy(
            src_ref=mine_ref.at[i], dst_ref=theirs_ref.at[i], send_sem=send_sems.at[i], recv_sem=recv_sems.at[i],
            device_id=(lax.axis_index("x"), lax.axis_index("y"), 1 - lax.axis_index("c")),
            device_id_type=pl.DeviceIdType.MESH)
        swap.start()
        swap.wait()
        g = part + theirs_ref[i]
        g_ref[...] = g
        d_ref[...], mo_ref[...], vo_ref[...] = _adam(w_ref[...], g, m_ref[...], v_ref[...])

    blk = pl.BlockSpec((tr, C), lambda i: (i, 0))
    return pl.pallas_call(
        body, name=name, grid=(nt,), out_shape=tuple(_sds((R, C), F32) for _ in range(4)),
        in_specs=[pl.BlockSpec((ns, tr, C), lambda i: (0, i, 0)), blk, blk, blk], out_specs=(blk, blk, blk, blk),
        scratch_shapes=[pltpu.VMEM((nt, tr, C), F32), pltpu.VMEM((nt, tr, C), F32),
                        pltpu.SemaphoreType.DMA((nt,)), pltpu.SemaphoreType.DMA((nt,))],
        compiler_params=_cp(("arbitrary",)),
    )(slots, w, m, v)


def _ada_adamw(sct, dmod, w, m, v):
    R, C = w.shape
    tr, bc = 256, 512

    def body(sct_ref, dm_ref, w_ref, m_ref, v_ref, g_ref, d_ref, mo_ref, vo_ref):
        g = sct_ref[:, 0:1] * dm_ref[0:1, :]
        for b in range(1, N_DEV):
            g = g + sct_ref[:, b:b + 1] * dm_ref[b:b + 1, :]
        g_ref[...] = g
        d_ref[...], mo_ref[...], vo_ref[...] = _adam(w_ref[...], g, m_ref[...], v_ref[...])

    blk = pl.BlockSpec((tr, bc), lambda i, j: (i, j))
    return pl.pallas_call(
        body, name="ada_adamw", grid=(R // tr, C // bc), out_shape=tuple(_sds((R, C), F32) for _ in range(4)),
        in_specs=[pl.BlockSpec((tr, N_DEV), lambda i, j: (i, 0)), pl.BlockSpec((N_DEV, bc), lambda i, j: (0, j)),
                  blk, blk, blk],
        out_specs=(blk, blk, blk, blk), compiler_params=_cp(("parallel", "parallel")),
    )(sct, dmod, w, m, v)


def _small_reduce(slots, fold):
    def body(s_ref, f_ref, o_ref):
        tot = s_ref[0:1, :]
        for k in range(1, N_DEV):
            tot = tot + s_ref[k:k + 1, :]
        o_ref[:, 0:6144] = tot[:, 0:6144]
        o_ref[:, 6144:7168] = tot[:, 6144:7168]
        for t, src in enumerate((8192, 8704)):
            v8 = jnp.broadcast_to(tot[:, src:src + AW], (8, AW))
            o_ref[:, 7168 + t * LANES:7168 + (t + 1) * LANES] = jnp.dot(
                v8, f_ref[...], precision=HI, preferred_element_type=F32)[0:1, :]
        o_ref[:, 7424:7552] = tot[:, 9216:9344]
        o_ref[:, 7552:10112] = tot[:, 9344:11904]
        o_ref[:, 10112:SMALL_OUT] = tot[:, 7168:8192]

    return pl.pallas_call(
        body, name="small_reduce", out_shape=_sds((1, SMALL_OUT), F32),
        in_specs=[pl.BlockSpec(memory_space=pltpu.VMEM), pl.BlockSpec(memory_space=pltpu.VMEM)],
        out_specs=pl.BlockSpec(memory_space=pltpu.VMEM),
    )(slots, fold)


def _perm_in(w):
    pad = jnp.zeros((w.shape[0], NP - 2568), w.dtype)
    return jnp.concatenate([w[:, :1536], w[:, 1544:2568], w[:, 1536:1544], pad], axis=1)


def _pad_lanes(vec, n=LANES):
    return jnp.pad(vec, ((0, 0), (0, n - vec.shape[1])))


def kernel(x, c, w_ada, b_ada, norm1_g, w_in, q_norm_g, k_norm_g, b_f, conv_w, conv_b, conv_ln_g, conv_ln_b, beta_attn, beta_conv, w_out, norm2_g, w_ff1, w_ff2, loss_target, m_w_ada, m_b_ada, m_norm1_g, m_w_in, m_q_norm_g, m_k_norm_g, m_b_f, m_conv_w, m_conv_b, m_conv_ln_g, m_conv_ln_b, m_beta_attn, m_beta_conv, m_w_out, m_norm2_g, m_w_ff1, m_w_ff2, v_w_ada, v_b_ada, v_norm1_g, v_w_in, v_q_norm_g, v_k_norm_g, v_b_f, v_conv_w, v_conv_b, v_conv_ln_g, v_conv_ln_b, v_beta_attn, v_beta_conv, v_w_out, v_norm2_g, v_w_ff1, v_w_ff2):
    S = x.shape[1]
    tm = min(256, S)
    tq = min(512, S // 2)
    xs, tgt = x[0], loss_target[0]
    chip = 2 * lax.axis_index("x") + lax.axis_index("y")
    e512, et512 = _head_sum_mats()

    conv_w32 = jnp.pad(conv_w[0], ((0, 1), (0, 0)))
    c_all, g_in = _exchange([(c, "bcast8"), (w_in[0].astype(BF16), "chip4")], "gather_in")
    later_weights = _Exchange([(w_out[0].astype(BF16), "chip4"), (w_ff1[0].astype(BF16), "chip4"),
                               (w_ff2[0].astype(BF16), "chip4"), (conv_w32, "chip4")])
    c_all = c_all.reshape(N_DEV, D)
    w_in_p = _perm_in(jnp.transpose(g_in, (1, 0, 2)).reshape(D, 2568))

    b_shard = lax.dynamic_slice(b_ada, (0, chip * 1536), (1, 1536))
    mod_rows, sc_all = _mod_shard(c_all, w_ada[0], b_shard)
    (mod_slots,) = _exchange([(mod_rows.reshape(N_DEV, 1, 1536), "all8")], "scatter_mod")
    mod = mod_slots.reshape(4, 2, 1536)[:, 0, :].reshape(6, D)
    mod8 = jnp.pad(mod, ((0, 2), (0, 0)))

    qg512 = jnp.tile(q_norm_g, (1, NH))
    kg512 = jnp.tile(k_norm_g, (1, NH))
    bf128 = _pad_lanes(b_f)

    (h1, qh, kh, vb, qn, kn, rq, rk, fgb, alin, agate, u0), (g_out, g_ff1, g_ff2, g_cw) = _fwd_in(
        xs, mod8, norm1_g, w_in_p, e512, et512, qg512, kg512, bf128, tm, ex=later_weights)
    w_out_f = g_out.reshape(D, D)
    w1, w2 = g_ff1, g_ff2
    cw32 = jnp.transpose(g_cw, (1, 0, 2)).reshape(HALO, CW)
    qa, ka, fcum = _fwd_decay(fgb, qh, kh, tm)
    fs, fe, bd = _skip_tables(fcum, q_norm_g, k_norm_g, tq)
    o_attn, lsec, lser = _attn_fwd(qa, ka, vb, fs, fe, bd, tq)
    u1, mc = _fwd_conv(u0, cw32, conv_b, conv_ln_g, conv_ln_b, beta_conv, tm)
    merged, ob, x2, h2 = _fwd_out(o_attn, mc, xs, mod8, norm2_g, beta_attn, w_out_f, tm)
    tf = min(512, S)
    r, dy, loss8, dg2 = _fwd_ffn(h2, w1, w2, x2, tgt, mod8, tf)

    df2, df1, dh2 = _bwd_ffn(dy, mod8, r, w1, w2, tf)
    gw_ff2 = _wgrad(r, df2, "wgrad_ff2", square_a=True)
    gw_ff1 = _wgrad(h2, df1, "wgrad_ff1")
    ff_grads = _Exchange([(jnp.transpose(gw_ff1.reshape(D, 4, D), (1, 0, 2)).astype(BF16), "chip4p"),
                          (gw_ff2.reshape(4, D, D).astype(BF16), "chip4p")])
    (dx2, do, doa, du1, acc_d, acc_h), (p_ff1, p_ff2) = _bwd_mid(
        dh2, dy, x2, ob, o_attn, u1, mod8, norm2_g, beta_attn, beta_conv, conv_ln_g, conv_ln_b, w_out_f, tm, ex=ff_grads)
    gw_out = _wgrad(merged, do, "wgrad_out")
    dqa, dr = _attn_bwd_dq(qa, ka, vb, doa, o_attn, lsec, fs, fe, bd, tq)
    dka, dv = _attn_bwd_dkv(qa, ka, vb, doa, lser, dr, fs, fe, bd, tq)
    out_grads = _Exchange([(gw_out.reshape(4, 256, D).astype(BF16), "chip4p")])
    (dalin, dagate, dcw, dcb), (p_out,) = _bwd_conv(du1, u0, alin, agate, cw32, tm, ex=out_grads)
    dq, dk, dfg, accg, accb = _bwd_qk(dqa, dka, qn, kn, rq, rk, fgb, qg512, kg512, e512, et512, tm)
    grad_x, acc1 = _bwd_in(dq, dk, dv, dalin, dagate, dfg, w_in_p, xs, dx2, mod8, norm1_g, tm)
    gw_in = jnp.concatenate(
        [_wgrad(h1, dq, "wgrad_in_q"), _wgrad(h1, dk, "wgrad_in_k"), _wgrad(h1, dv, "wgrad_in_v"),
         _wgrad(h1, dfg, "wgrad_in_f")[:, :NH], _wgrad(h1, dalin, "wgrad_in_a"), _wgrad(h1, dagate, "wgrad_in_g")],
        axis=1)

    s8 = lambda a: jnp.sum(a, axis=-2)
    a1, ad, ah = s8(acc1), s8(acc_d), s8(acc_h)
    small = jnp.concatenate(
        [a1[0], a1[1], ad[3], ad[0], ad[1], s8(dg2),
         a1[2], ad[2], s8(accg).reshape(-1), s8(accb), s8(dcb), ah[2], ah[3], ah[0], ah[1]]).reshape(1, SMALL_IN)
    gcw = s8(dcw)
    small_s, p_in, p_cw = _exchange(
        [(small, "bcast8"),
         (jnp.transpose(gw_in.reshape(D, 4, 642), (1, 0, 2)).astype(BF16), "chip4p"),
         (jnp.transpose(gcw.reshape(HALO, 4, LANES), (1, 0, 2)), "chip4p")], "scatter_grads")
    small_s = small_s.reshape(N_DEV, SMALL_IN)

    g_in_, d_in, nm_in, nv_in = _pair_adamw(p_in, w_in[0], m_w_in[0], v_w_in[0], "adamw_in")
    g_out_, d_out, nm_out, nv_out = _pair_adamw(p_out, w_out[0], m_w_out[0], v_w_out[0], "adamw_out")
    g_f1, d_f1, nm_f1, nv_f1 = _pair_adamw(p_ff1, w_ff1[0], m_w_ff1[0], v_w_ff1[0], "adamw_ff1")
    g_f2, d_f2, nm_f2, nv_f2 = _pair_adamw(p_ff2, w_ff2[0], m_w_ff2[0], v_w_ff2[0], "adamw_ff2")
    pad_row = lambda a, fill: jnp.pad(a[0], ((0, 1), (0, 0)), constant_values=fill)
    g_cw_, d_cw, nm_cw, nv_cw = (a[:KC] for a in _pair_adamw(
        p_cw, pad_row(conv_w, 0.0), pad_row(m_conv_w, 0.0), pad_row(v_conv_w, 1.0), "adamw_conv_w"))
    dmod_shard = lax.dynamic_slice(small_s[:, :6 * D], (0, chip * 1536), (N_DEV, 1536))
    g_ada, d_ada, nm_ada, nv_ada = _ada_adamw(sc_all.T, dmod_shard, w_ada[0], m_w_ada[0], v_w_ada[0])

    fold = np.zeros((AW, LANES), np.float32)
    fold[np.arange(AW), np.arange(AW) % DH] = 1.0
    g_small = _small_reduce(small_s, jnp.asarray(fold))
    smalls = [b_ada, norm1_g, q_norm_g, k_norm_g, b_f, conv_b, conv_ln_g, conv_ln_b, beta_attn, beta_conv, norm2_g]
    m_smalls = [m_b_ada, m_norm1_g, m_q_norm_g, m_k_norm_g, m_b_f, m_conv_b, m_conv_ln_g, m_conv_ln_b, m_beta_attn,
                m_beta_conv, m_norm2_g]
    v_smalls = [v_b_ada, v_norm1_g, v_q_norm_g, v_k_norm_g, v_b_f, v_conv_b, v_conv_ln_g, v_conv_ln_b, v_beta_attn,
                v_beta_conv, v_norm2_g]
    widths = [a.shape[1] for a in smalls]
    padded = [-(-n // LANES) * LANES for n in widths]
    pack = lambda arrs, fill: jnp.concatenate(
        [jnp.pad(a, ((0, 0), (0, p - a.shape[1])), constant_values=fill) for a, p in zip(arrs, padded)], axis=1)
    outs_small = _reduce_adamw(g_small.reshape(1, 1, SMALL_OUT), pack(smalls, 0.0), pack(m_smalls, 0.0),
                               pack(v_smalls, 1.0), "adamw_small")
    offs = np.concatenate([[0], np.cumsum(padded)])

    def unpack(a):
        return [a[:, int(o):int(o) + n] for o, n in zip(offs[:-1], widths)]

    gs, ds, ms, vs = (unpack(a) for a in outs_small)

    loss = lax.psum(loss8[0, 0], ("x", "y", "c"))
    big = {"w_ada": (g_ada, d_ada, nm_ada, nv_ada), "w_in": (g_in_, d_in, nm_in, nv_in),
           "conv_w": (g_cw_, d_cw, nm_cw, nv_cw), "w_out": (g_out_, d_out, nm_out, nv_out),
           "w_ff1": (g_f1, d_f1, nm_f1, nv_f1), "w_ff2": (g_f2, d_f2, nm_f2, nv_f2)}
    small_names = ["b_ada", "norm1_g", "q_norm_g", "k_norm_g", "b_f", "conv_b", "conv_ln_g", "conv_ln_b", "beta_attn",
                   "beta_conv", "norm2_g"]
    order = ["w_ada", "b_ada", "norm1_g", "w_in", "q_norm_g", "k_norm_g", "b_f", "conv_w", "conv_b", "conv_ln_g",
             "conv_ln_b", "beta_attn", "beta_conv", "w_out", "norm2_g", "w_ff1", "w_ff2"]

    def leaf(name, which):
        if name in big:
            return big[name][which][None]
        return (gs, ds, ms, vs)[which][small_names.index(name)]

    return (loss, grad_x[None], *[leaf(n, 0) for n in order], *[leaf(n, 1) for n in order],
            *[leaf(n, 2) for n in order], *[leaf(n, 3) for n in order])
```

```python
import functools

import numpy as np
import jax
import jax.numpy as jnp
from jax import lax
from jax.experimental import pallas as pl
from jax.experimental.pallas import tpu as pltpu

F32, BF16 = jnp.float32, jnp.bfloat16
HI = lax.Precision.HIGHEST
D = 1024
AW = 512
CW = 512
NH = 8
DH = 64
KC = 31
DFF = 4096
NP = 2688
EPS = 1e-6
NEG = -1e30
LANES = 128
VMEM_LIMIT = 56 * 2**20
NT = (((1,), (1,)), ((), ()))
TN = (((0,), (0,)), ((), ()))
LR, B1, B2, AEPS, WD, STEP = 0.001, 0.9, 0.999, 1e-08, 0.01, 10
N_DEV = 8
SMALL_IN = 11904
SMALL_OUT = 11136


def _cp(sem=None, vmem=VMEM_LIMIT):
    kw = dict(vmem_limit_bytes=vmem)
    if sem is not None:
        kw["dimension_semantics"] = sem
    return pltpu.CompilerParams(**kw)


def _rows(tm, n):
    return pl.BlockSpec((tm, n), lambda i: (i, 0))


def _const(shape):
    nd = len(shape)
    return pl.BlockSpec(shape, lambda *_: (0,) * nd)


def _sds(shape, dt):
    return jax.ShapeDtypeStruct(shape, dt)


def _lane(shape):
    return lax.broadcasted_iota(jnp.int32, shape, len(shape) - 1)


def _sigmoid(x):
    return 1.0 / (1.0 + jnp.exp(-x))


class _Exchange:
    MASKS = {"chip4": (2, 4, 6), "chip4p": (2, 4, 6), "all8": (1, 2, 3, 4, 5, 6, 7), "bcast8": (1, 2, 3, 4, 5, 6, 7)}

    def __init__(self, items):
        self.srcs = [s for s, _ in items]
        self.kinds = [k for _, k in items]
        self.n = len(items)
        self.out_shapes = []
        for s, k in items:
            shape = {"all8": (N_DEV,) + s.shape[1:], "bcast8": (N_DEV,) + s.shape, "chip4": (4,) + s.shape,
                     "chip4p": (4,) + s.shape[1:]}[k]
            self.out_shapes.append(_sds(shape, s.dtype))
        self.sem_index = {}
        for t, k in enumerate(self.kinds):
            for m in self.MASKS[k]:
                self.sem_index[(t, m)] = len(self.sem_index)
        n_sem = len(self.sem_index)
        self.scratch = [pltpu.SemaphoreType.DMA((n_sem,)), pltpu.SemaphoreType.DMA((n_sem,)),
                        pltpu.SemaphoreType.DMA((self.n,))]
        self.in_specs = [pl.BlockSpec(memory_space=pl.ANY)] * self.n
        self.out_specs = [pl.BlockSpec(memory_space=pl.ANY)] * self.n

    def copies(self, src_refs, dst_refs, send_sems, recv_sems, local_sems):
        x, y, c = lax.axis_index("x"), lax.axis_index("y"), lax.axis_index("c")
        my_id = 4 * x + 2 * y + c
        my_chip = 2 * x + y

        def piece(t, dev_id, chip):
            k = self.kinds[t]
            return src_refs[t].at[dev_id] if k == "all8" else src_refs[t].at[chip] if k == "chip4p" else src_refs[t]

        out = []
        for t in range(self.n):
            slot = dst_refs[t].at[my_chip if self.kinds[t] in ("chip4", "chip4p") else my_id]
            out.append(pltpu.make_async_copy(piece(t, my_id, my_chip), slot, local_sems.at[t]))
            for m in self.MASKS[self.kinds[t]]:
                px = 1 - x if m & 4 else x
                py = 1 - y if m & 2 else y
                pc = 1 - c if m & 1 else c
                s = self.sem_index[(t, m)]
                out.append(pltpu.make_async_remote_copy(
                    src_ref=piece(t, 4 * px + 2 * py + pc, 2 * px + py), dst_ref=slot,
                    send_sem=send_sems.at[s], recv_sem=recv_sems.at[s],
                    device_id=(px, py, pc), device_id_type=pl.DeviceIdType.MESH))
        return out


def _hosted(body, ex, n_in, n_out, n_scr, n_steps):
    if ex is None:
        return body

    def wrapped(*refs):
        ins, xin = refs[:n_in], refs[n_in:n_in + ex.n]
        o0 = n_in + ex.n
        outs, xout = refs[o0:o0 + n_out], refs[o0 + n_out:o0 + n_out + ex.n]
        s0 = o0 + n_out + ex.n
        scr, sems = refs[s0:s0 + n_scr], refs[s0 + n_scr:]

        @pl.when(pl.program_id(0) == 0)
        def _():
            for cp in ex.copies(xin, xout, *sems):
                cp.start()

        body(*ins, *outs, *scr)

        @pl.when(pl.program_id(0) == n_steps - 1)
        def _():
            for cp in ex.copies(xin, xout, *sems):
                cp.wait()

    return wrapped


def _exchange(items, name):
    ex = _Exchange(items)
    n = ex.n

    def body(*refs):
        copies = ex.copies(refs[:n], refs[n:2 * n], *refs[2 * n:])
        for cp in copies:
            cp.start()
        for cp in copies:
            cp.wait()

    outs = pl.pallas_call(
        body, name=name, out_shape=tuple(ex.out_shapes), in_specs=ex.in_specs, out_specs=tuple(ex.out_specs),
        scratch_shapes=ex.scratch,
    )(*ex.srcs)
    return list(outs)


def _mod_shard(c_all, w_ada, b_shard):
    n = w_ada.shape[1]

    def body(c_ref, w_ref, b_ref, o_ref, sc_ref):
        cv = c_ref[...]
        sc = cv * _sigmoid(cv)
        sc_ref[...] = sc
        o_ref[...] = jnp.dot(sc, w_ref[...], precision=HI, preferred_element_type=F32) + b_ref[...]

    bn = 512
    return pl.pallas_call(
        body, name="mod_shard", out_shape=(_sds((N_DEV, n), F32), _sds((N_DEV, D), F32)), grid=(n // bn,),
        in_specs=[_const((N_DEV, D)), pl.BlockSpec((D, bn), lambda j: (0, j)), pl.BlockSpec((1, bn), lambda j: (0, j))],
        out_specs=(pl.BlockSpec((N_DEV, bn), lambda j: (0, j)), _const((N_DEV, D))),
        compiler_params=_cp(("arbitrary",)),
    )(c_all, w_ada, b_shard)


def _head_sum_mats():
    e = np.zeros((AW, LANES), np.float32)
    for h in range(NH):
        e[h * DH:(h + 1) * DH, h] = 1.0
    return jnp.asarray(e, BF16), jnp.asarray(e.T.copy(), BF16)


def _dot2(x, w):
    hi = x.astype(BF16)
    lo = (x - hi.astype(F32)).astype(BF16)
    return jnp.dot(hi, w, preferred_element_type=F32) + jnp.dot(lo, w, preferred_element_type=F32)


def _fwd_in(x, mod8, n1g, w_in_p, e512, et512, qg512, kg512, bf128, tm, ex=None):
    S = x.shape[0]

    def body(x_ref, mod_ref, n1g_ref, w_ref, e_ref, et_ref, qg_ref, kg_ref, bf_ref,
             h1_ref, qh_ref, kh_ref, v_ref, qn_ref, kn_ref, rq_ref, rk_ref, fgb_ref, alin_ref, agate_ref, u0_ref):
        xv = x_ref[...]
        r1 = lax.rsqrt(jnp.mean(xv * xv, axis=-1, keepdims=True) + EPS)
        h = (xv * r1) * (n1g_ref[...] * (1.0 + mod_ref[1:2, :])) + mod_ref[0:1, :]
        hb = h.astype(BF16)
        h1_ref[...] = hb

        def seg(a, b):
            return jnp.dot(hb, w_ref[:, a:b], preferred_element_type=F32)

        def headnorm(t, g_ref, scale, n_ref, r_ref, o_ref):
            ss = _dot2(t * t, e_ref[...])
            r = lax.rsqrt(ss * (1.0 / DH) + EPS)
            tn = t * _dot2(r, et_ref[...])
            n_ref[...] = tn.astype(BF16)
            r_ref[...] = r
            o_ref[...] = (tn * (g_ref[...] * scale)).astype(BF16)

        headnorm(seg(0, 512), qg_ref, DH ** -0.5, qn_ref, rq_ref, qh_ref)
        headnorm(seg(512, 1024), kg_ref, 1.0, kn_ref, rk_ref, kh_ref)
        v_ref[...] = seg(1024, 1536).astype(BF16)
        alin = seg(1536, 2048)
        agate = seg(2048, 2560)
        alin_ref[...] = alin.astype(BF16)
        agate_ref[...] = agate.astype(BF16)
        u0_ref[...] = alin * _sigmoid(agate)
        fgb_ref[...] = seg(2560, NP) + bf_ref[...]

    bf = lambda: _sds((S, AW), BF16)
    xs = ex.srcs if ex else []
    outs = pl.pallas_call(
        _hosted(body, ex, 9, 12, 0, S // tm), name="fwd_in", grid=(S // tm,),
        out_shape=(_sds((S, D), BF16), bf(), bf(), bf(), bf(), bf(), _sds((S, LANES), F32), _sds((S, LANES), F32),
                   _sds((S, LANES), F32), bf(), bf(), _sds((S, CW), F32), *(ex.out_shapes if ex else [])),
        in_specs=[_rows(tm, D), _const((8, D)), _const((1, D)), _const((D, NP)), _const((AW, LANES)), _const((LANES, AW)),
                  _const((1, AW)), _const((1, AW)), _const((1, LANES)), *(ex.in_specs if ex else [])],
        out_specs=(_rows(tm, D), _rows(tm, AW), _rows(tm, AW), _rows(tm, AW), _rows(tm, AW), _rows(tm, AW),
                   _rows(tm, LANES), _rows(tm, LANES), _rows(tm, LANES), _rows(tm, AW), _rows(tm, AW), _rows(tm, CW),
                   *(ex.out_specs if ex else [])),
        scratch_shapes=ex.scratch if ex else [],
        compiler_params=_cp(("arbitrary",)),
    )(x, mod8, n1g, w_in_p, e512, et512, qg512, kg512, bf128, *xs)
    return outs[:12], list(outs[12:])


def _split3(f):
    f1 = f.astype(BF16).astype(F32)
    f2 = (f - f1).astype(BF16).astype(F32)
    return f1, f2, f - f1 - f2


def _dot3(w, x):
    return sum(jnp.dot(w, piece.astype(BF16), preferred_element_type=F32) for piece in _split3(x))


def _fwd_decay(fgb, qh, kh, vb, shift, tm):
    S = fgb.shape[0]

    def body(shift_ref, fgb_ref, qh_ref, kh_ref, vb_ref, qa_ref, ka_ref, va_ref, f_ref, carry_ref):
        @pl.when(pl.program_id(0) == 0)
        def _():
            carry_ref[...] = jnp.zeros_like(carry_ref)

        fb = fgb_ref[...]
        lf = jnp.minimum(fb, 0.0) - jnp.log1p(jnp.exp(-jnp.abs(fb)))
        tri = (lax.broadcasted_iota(jnp.int32, (tm, tm), 0) >= lax.broadcasted_iota(jnp.int32, (tm, tm), 1)
               ).astype(F32).astype(BF16)
        cs = _dot3(tri, lf) + carry_ref[0:1, :]
        f_ref[...] = cs
        carry_ref[...] = jnp.broadcast_to(cs[tm - 1:tm, :], carry_ref.shape)
        lane = _lane((tm, LANES))
        s1, s2, s3 = _split3(jnp.zeros((tm, LANES), F32) - shift_ref[0, 0])
        tail_q = jnp.where((lane >= 67) & (lane < 70), 1.0,
                           jnp.where(lane == 70, s1, jnp.where(lane == 71, s2, jnp.where(lane == 72, s3, 0.0))))
        tail_k = jnp.where(((lane >= 64) & (lane < 67)) | ((lane >= 70) & (lane < 73)), 1.0, 0.0)
        tail_v = jnp.where(lane == DH, 1.0, 0.0)
        for p in range(NH // 2):
            qp = qh_ref[:, p * LANES:(p + 1) * LANES].astype(F32)
            kp = kh_ref[:, p * LANES:(p + 1) * LANES].astype(F32)
            vp = vb_ref[:, p * LANES:(p + 1) * LANES].astype(F32)
            for hh in range(2):
                h = 2 * p + hh
                f1, f2, f3 = _split3(cs[:, h:h + 1])
                qb = qp if hh == 0 else pltpu.roll(qp, 64, 1)
                kb = kp if hh == 0 else pltpu.roll(kp, 64, 1)
                vh = vp if hh == 0 else pltpu.roll(vp, 64, 1)
                augq = jnp.where(lane == 64, f1, jnp.where(lane == 65, f2, jnp.where(lane == 66, f3, tail_q)))
                augk = jnp.where(lane == 67, -f1, jnp.where(lane == 68, -f2, jnp.where(lane == 69, -f3, tail_k)))
                qa_ref[h] = jnp.where(lane < DH, qb, augq).astype(BF16)
                ka_ref[h] = jnp.where(lane < DH, kb, augk).astype(BF16)
                va_ref[h] = jnp.where(lane < DH, vh, tail_v).astype(BF16)

    hm = pl.BlockSpec((NH, tm, LANES), lambda i: (0, i, 0))
    hms = _sds((NH, S, LANES), BF16)
    return pl.pallas_call(
        body, name="fwd_decay", grid=(S // tm,),
        out_shape=(hms, hms, hms, _sds((S, LANES), F32)),
        in_specs=[SMEM_SPEC, _rows(tm, LANES), _rows(tm, AW), _rows(tm, AW), _rows(tm, AW)],
        out_specs=(hm, hm, hm, _rows(tm, LANES)),
        scratch_shapes=[pltpu.VMEM((8, LANES), F32)], compiler_params=_cp(("arbitrary",)),
    )(shift, fgb, qh, kh, vb)


SKIP = -106.0


def _block_loops(first, step, needed, run):
    def both(j):
        return jnp.logical_and(needed(0, j), needed(1, j))

    def walk(heads):
        def go(j):
            run(j, heads)
            return j + step
        return go

    j = lax.while_loop(both, walk((0, 1)), first)
    lax.while_loop(functools.partial(needed, 0), walk((0,)), j)
    lax.while_loop(functools.partial(needed, 1), walk((1,)), j)


def _logit_bound(qg, kg):
    return (2.0 * 1.03 * DH ** 0.5 * jnp.max(jnp.abs(qg)) * jnp.max(jnp.abs(kg))).reshape(1, 1)


def _skip_tables(f, tq):
    return f[0::tq, :NH].T, f[tq - 1::tq, :NH].T


SMEM_SPEC = pl.BlockSpec(memory_space=pltpu.SMEM)


def _causal_rect(rows, cols, col0):
    return (lax.broadcasted_iota(jnp.int32, (rows, cols), 0)
            >= lax.broadcasted_iota(jnp.int32, (rows, cols), 1) + col0)


def _chunk(tq):
    return tq


SHIFT_MAX = 60.0


def _shift(bd):
    return jnp.where(bd <= SHIFT_MAX, 0.5 * bd, 0.0)


def _attn_fwd(qa, ka, va, fs, fe, bd, tq):
    S = qa.shape[1]
    nq = S // tq

    def body(fs_ref, fe_ref, bd_ref, qa_ref, ka_ref, va_ref, o_ref, lsec_ref, lser_ref, m_ref, acc_ref):
        pr, i = pl.program_id(0), pl.program_id(1)
        sel_a = _lane((tq, LANES)) < DH
        acc_ref[...] = jnp.zeros_like(acc_ref)

        def logits(j, hh, masked):
            start = pl.multiple_of(j * tq, tq)
            s = lax.dot_general(qa_ref[hh], ka_ref[hh, pl.ds(start, tq), :], NT, preferred_element_type=F32)
            if masked:
                s = jnp.where(_causal_rect(tq, tq, 0), s, NEG)
            return s, va_ref[hh, pl.ds(start, tq), :]

        def shifted_step(j, heads, masked=False):
            for hh in heads:
                s, vb = logits(j, hh, masked)
                acc_ref[hh] += jnp.dot(jnp.exp(s).astype(BF16), vb, preferred_element_type=F32)

        def online_step(j, heads, masked=False):
            for hh in heads:
                s, vb = logits(j, hh, masked)
                m_prev = m_ref[hh]
                m_new = jnp.maximum(m_prev, jnp.max(s, axis=1, keepdims=True))
                p = jnp.exp(s - jnp.tile(m_new, (1, tq // LANES)))
                m_ref[hh] = m_new
                acc_ref[hh] = jnp.exp(m_prev - m_new) * acc_ref[hh] + jnp.dot(p.astype(BF16), vb,
                                                                              preferred_element_type=F32)

        def needed(hh, j):
            top = fs_ref[2 * pr + hh, i] + bd_ref[0, 0]
            return jnp.logical_and(j >= 0, top - fe_ref[2 * pr + hh, jnp.maximum(j, 0)] >= SKIP)

        @pl.when(bd_ref[0, 0] <= SHIFT_MAX)
        def _():
            m_ref[...] = jnp.zeros_like(m_ref)
            shifted_step(i, (0, 1), masked=True)
            _block_loops(i - 1, -1, needed, shifted_step)

        @pl.when(bd_ref[0, 0] > SHIFT_MAX)
        def _():
            m_ref[...] = jnp.full(m_ref.shape, NEG, F32)
            online_step(i, (0, 1), masked=True)
            _block_loops(i - 1, -1, needed, online_step)

        outs, lses = [], []
        for hh in range(2):
            acc = acc_ref[hh]
            row_sum = jnp.broadcast_to(acc[:, DH:DH + 1], (tq, LANES))
            outs.append(acc / row_sum)
            lses.append(m_ref[hh] + jnp.log(row_sum))
            lsec_ref[hh] = lses[hh]
        o_ref[...] = jnp.where(sel_a, outs[0], pltpu.roll(outs[1], 64, 1))
        row = lax.broadcasted_iota(jnp.int32, (8, tq), 0)
        lser_ref[0, 0] = jnp.where(row == 0, lses[0].T[0:8, :], lses[1].T[0:8, :])

    return pl.pallas_call(
        body, name="attn_fwd", grid=(NH // 2, nq),
        out_shape=(_sds((S, AW), F32), _sds((NH, S, LANES), F32), _sds((NH // 2, nq, 8, tq), F32)),
        in_specs=[SMEM_SPEC, SMEM_SPEC, SMEM_SPEC,
                  pl.BlockSpec((2, tq, LANES), lambda p, i: (p, i, 0)),
                  pl.BlockSpec((2, S, LANES), lambda p, i: (p, 0, 0)),
                  pl.BlockSpec((2, S, LANES), lambda p, i: (p, 0, 0))],
        out_specs=(pl.BlockSpec((tq, LANES), lambda p, i: (i, p)),
                   pl.BlockSpec((2, tq, LANES), lambda p, i: (p, i, 0)),
                   pl.BlockSpec((1, 1, 8, tq), lambda p, i: (p, i, 0, 0))),
        scratch_shapes=[pltpu.VMEM((2, tq, LANES), F32), pltpu.VMEM((2, tq, LANES), F32)],
        compiler_params=_cp(("parallel", "parallel")),
    )(fs, fe, bd, qa, ka, va)


HALO = 32
CHUNK_ROWS = 64


def _halo_prev(tm):
    return pl.BlockSpec((HALO, CW), lambda i: (jnp.maximum(i * (tm // HALO) - 1, 0), 0))


def _fwd_conv(u0, w32, cb, lng, lnb, beta_c, tm):
    S = u0.shape[0]

    def body(cur_ref, prev_ref, w_ref, cb_ref, lng_ref, lnb_ref, beta_ref, u1_ref, mc_ref, ext_ref):
        i = pl.program_id(0)
        ext_ref[0:HALO, :] = jnp.where(i == 0, 0.0, prev_ref[...])
        ext_ref[HALO:, :] = cur_ref[...]
        for r0 in range(0, tm, CHUNK_ROWS):
            acc = jnp.zeros((CHUNK_ROWS, CW), F32) + cb_ref[...]
            for j in range(KC):
                acc = acc + w_ref[j:j + 1, :] * ext_ref[r0 + 2 + j:r0 + 2 + j + CHUNK_ROWS, :]
            u1_ref[r0:r0 + CHUNK_ROWS, :] = acc
        u1 = u1_ref[...]
        mu = jnp.mean(u1, axis=-1, keepdims=True)
        d = u1 - mu
        rstd = lax.rsqrt(jnp.mean(d * d, axis=-1, keepdims=True) + EPS)
        u2 = d * rstd * lng_ref[...] + lnb_ref[...]
        u3 = u2 * _sigmoid(u2)
        rc = lax.rsqrt(jnp.mean(u3 * u3, axis=-1, keepdims=True) + EPS)
        mc_ref[...] = (u3 * rc * beta_ref[...]).astype(BF16)

    return pl.pallas_call(
        body, name="fwd_conv", grid=(S // tm,),
        out_shape=(_sds((S, CW), F32), _sds((S, CW), BF16)),
        in_specs=[_rows(tm, CW), _halo_prev(tm), _const((HALO, CW)), _const((1, CW)), _const((1, CW)), _const((1, CW)),
                  _const((1, CW))],
        out_specs=(_rows(tm, CW), _rows(tm, CW)),
        scratch_shapes=[pltpu.VMEM((tm + HALO, CW), F32)], compiler_params=_cp(("parallel",)),
    )(u0, u0, w32, cb, lng, lnb, beta_c)


def _fwd_out(o_attn, mc, x, mod8, n2g, beta_a, w_out, tm):
    S = x.shape[0]

    def body(o_ref, mc_ref, x_ref, mod_ref, n2g_ref, beta_ref, w_ref, mg_ref, ob_ref, x2_ref, h2_ref):
        ov = o_ref[...]
        ra = lax.rsqrt(jnp.mean(ov * ov, axis=-1, keepdims=True) + EPS)
        ma = (ov * ra * beta_ref[...]).astype(BF16)
        mcv = mc_ref[...]
        mg_ref[:, 0:AW] = ma
        mg_ref[:, AW:D] = mcv
        o = (jnp.dot(ma, w_ref[0:AW, :], preferred_element_type=F32)
             + jnp.dot(mcv, w_ref[AW:D, :], preferred_element_type=F32))
        ob_ref[...] = o.astype(BF16)
        x2 = x_ref[...] + mod_ref[2:3, :] * o
        x2_ref[...] = x2
        r2 = lax.rsqrt(jnp.mean(x2 * x2, axis=-1, keepdims=True) + EPS)
        h2_ref[...] = ((x2 * r2) * (n2g_ref[...] * (1.0 + mod_ref[4:5, :])) + mod_ref[3:4, :]).astype(BF16)

    return pl.pallas_call(
        body, name="fwd_out", grid=(S // tm,),
        out_shape=(_sds((S, D), BF16), _sds((S, D), BF16), _sds((S, D), F32), _sds((S, D), BF16)),
        in_specs=[_rows(tm, AW), _rows(tm, CW), _rows(tm, D), _const((8, D)), _const((1, D)), _const((1, AW)),
                  _const((D, D))],
        out_specs=(_rows(tm, D), _rows(tm, D), _rows(tm, D), _rows(tm, D)),
        compiler_params=_cp(("parallel",)),
    )(o_attn, mc, x, mod8, n2g, beta_a, w_out)


def _fwd_ffn(h2, w1, w2, x2, tgt, mod8, tm):
    S = h2.shape[0]
    nk = w1.shape[0]
    bf = w1.shape[2]

    def body(h2_ref, w1_ref, w2_ref, x2_ref, tgt_ref, mod_ref, r_ref, dy_ref, loss_ref, dg2_ref, acc_ref):
        i, k = pl.program_id(0), pl.program_id(1)

        @pl.when((i == 0) & (k == 0))
        def _():
            loss_ref[...] = jnp.zeros_like(loss_ref)
            dg2_ref[...] = jnp.zeros_like(dg2_ref)

        r = jnp.maximum(jnp.dot(h2_ref[...], w1_ref[0], preferred_element_type=F32), 0.0)
        r_ref[...] = r.astype(BF16)
        part = jnp.dot((r * r).astype(BF16), w2_ref[0], preferred_element_type=F32)

        @pl.when(k == 0)
        def _():
            acc_ref[...] = part

        @pl.when(k > 0)
        def _():
            acc_ref[...] += part

        @pl.when(k == nk - 1)
        def _():
            f2 = acc_ref[...]
            e = x2_ref[...] + mod_ref[5:6, :] * f2 - tgt_ref[...]
            dy = e * (1.0 / D)
            dy_ref[...] = dy
            loss_ref[...] += 0.5 * jnp.sum(jnp.sum(e * dy, axis=1, keepdims=True), axis=0, keepdims=True)
            dg2_ref[...] += jnp.sum((dy * f2).reshape(tm // 8, 8, D), axis=0)

    return pl.pallas_call(
        body, name="fwd_ffn", grid=(S // tm, nk),
        out_shape=(_sds((S, DFF), BF16), _sds((S, D), F32), _sds((8, LANES), F32), _sds((8, D), F32)),
        in_specs=[pl.BlockSpec((tm, D), lambda i, k: (i, 0)), pl.BlockSpec((1, D, bf), lambda i, k: (k, 0, 0)),
                  pl.BlockSpec((1, bf, D), lambda i, k: (k, 0, 0)), pl.BlockSpec((tm, D), lambda i, k: (i, 0)),
                  pl.BlockSpec((tm, D), lambda i, k: (i, 0)), pl.BlockSpec((8, D), lambda i, k: (0, 0))],
        out_specs=(pl.BlockSpec((tm, bf), lambda i, k: (i, k)), pl.BlockSpec((tm, D), lambda i, k: (i, 0)),
                   pl.BlockSpec((8, LANES), lambda i, k: (0, 0)), pl.BlockSpec((8, D), lambda i, k: (0, 0))),
        scratch_shapes=[pltpu.VMEM((tm, D), F32)], compiler_params=_cp(("arbitrary", "arbitrary")),
    )(h2, w1, w2, x2, tgt, mod8)


def _bwd_ffn(dy, mod8, r, w1, w2, tm):
    S = dy.shape[0]
    nk = w1.shape[0]
    bf = w1.shape[2]

    def body(dy_ref, mod_ref, r_ref, w1_ref, w2_ref, df2_ref, df1_ref, dh2_ref):
        k = pl.program_id(1)
        df2 = (dy_ref[...] * mod_ref[5:6, :]).astype(BF16)

        @pl.when(k == 0)
        def _():
            df2_ref[...] = df2

        da = lax.dot_general(df2, w2_ref[0], NT, preferred_element_type=F32)
        df1 = (da * (2.0 * r_ref[...].astype(F32))).astype(BF16)
        df1_ref[...] = df1
        part = lax.dot_general(df1, w1_ref[0], NT, preferred_element_type=F32)

        @pl.when(k == 0)
        def _():
            dh2_ref[...] = part

        @pl.when(k > 0)
        def _():
            dh2_ref[...] += part

    return pl.pallas_call(
        body, name="bwd_ffn", grid=(S // tm, nk),
        out_shape=(_sds((S, D), BF16), _sds((S, DFF), BF16), _sds((S, D), F32)),
        in_specs=[pl.BlockSpec((tm, D), lambda i, k: (i, 0)), pl.BlockSpec((8, D), lambda i, k: (0, 0)),
                  pl.BlockSpec((tm, bf), lambda i, k: (i, k)), pl.BlockSpec((1, D, bf), lambda i, k: (k, 0, 0)),
                  pl.BlockSpec((1, bf, D), lambda i, k: (k, 0, 0))],
        out_specs=(pl.BlockSpec((tm, D), lambda i, k: (i, 0)), pl.BlockSpec((tm, bf), lambda i, k: (i, k)),
                   pl.BlockSpec((tm, D), lambda i, k: (i, 0))),
        compiler_params=_cp(("parallel", "arbitrary")),
    )(dy, mod8, r, w1, w2)


def _wgrad(a, b, name, square_a=False, tk=512, bm=1024, bn=1024):
    S, M = a.shape
    N = b.shape[1]
    bm, bn, tk = min(bm, M), min(bn, N), min(tk, S)

    def body(a_ref, b_ref, o_ref):
        av = a_ref[...]
        if square_a:
            af = av.astype(F32)
            av = (af * af).astype(BF16)
        part = lax.dot_general(av, b_ref[...], TN, preferred_element_type=F32)

        @pl.when(pl.program_id(2) == 0)
        def _():
            o_ref[...] = part

        @pl.when(pl.program_id(2) > 0)
        def _():
            o_ref[...] += part

    return pl.pallas_call(
        body, name=name, grid=(M // bm, N // bn, S // tk), out_shape=_sds((M, N), F32),
        in_specs=[pl.BlockSpec((tk, bm), lambda mi, ni, k: (k, mi)), pl.BlockSpec((tk, bn), lambda mi, ni, k: (k, ni))],
        out_specs=pl.BlockSpec((bm, bn), lambda mi, ni, k: (mi, ni)),
        compiler_params=_cp(("parallel", "parallel", "arbitrary")),
    )(a, b)


def _wgrad_in(h1, pieces, tk=512):
    S = h1.shape[0]
    tk = min(tk, S)
    widths = [p.shape[1] for p in pieces]
    offs = [sum(widths[:t]) for t in range(len(widths))]

    def body(a_ref, *refs):
        o_ref = refs[-1]

        @pl.when(pl.program_id(0) == 0)
        def _():
            o_ref[...] = jnp.zeros_like(o_ref)

        for b_ref, off, w in zip(refs[:-1], offs, widths):
            o_ref[:, off:off + w] += lax.dot_general(a_ref[...], b_ref[...], TN, preferred_element_type=F32)

    return pl.pallas_call(
        body, name="wgrad_in", grid=(S // tk,), out_shape=_sds((D, NP), F32),
        in_specs=[_rows(tk, D)] + [_rows(tk, w) for w in widths], out_specs=_const((D, NP)),
        compiler_params=_cp(("arbitrary",)),
    )(h1, *pieces)


def _colsum8(t):
    return jnp.sum(t.reshape(t.shape[0] // 8, 8, t.shape[1]), axis=0)


def _bwd_mid(dh2, dy, x2, ob, o_attn, u1, mod8, n2g, beta_a, beta_c, lng, lnb, w_out, tm, ex=None):
    S = dy.shape[0]

    def body(dh2_ref, dy_ref, x2_ref, ob_ref, oa_ref, u1_ref, mod_ref, n2g_ref, ba_ref, bc_ref, lng_ref, lnb_ref, w_ref,
             dx2_ref, do_ref, doa_ref, du1_ref, acc_d_ref, acc_h_ref):
        @pl.when(pl.program_id(0) == 0)
        def _():
            acc_d_ref[...] = jnp.zeros_like(acc_d_ref)
            acc_h_ref[...] = jnp.zeros_like(acc_h_ref)

        x2 = x2_ref[...]
        dh2 = dh2_ref[...]
        r2 = lax.rsqrt(jnp.mean(x2 * x2, axis=-1, keepdims=True) + EPS)
        xn2 = x2 * r2
        gain = n2g_ref[...] * (1.0 + mod_ref[4:5, :])
        dxn = dh2 * gain
        dx2 = dy_ref[...] + r2 * (dxn - xn2 * jnp.mean(dxn * xn2, axis=-1, keepdims=True))
        dx2_ref[...] = dx2
        t = dh2 * xn2
        acc_d_ref[0] += _colsum8(dh2)
        acc_d_ref[1] += _colsum8(t * n2g_ref[...])
        acc_d_ref[2] += _colsum8(t * (1.0 + mod_ref[4:5, :]))
        acc_d_ref[3] += _colsum8(dx2 * ob_ref[...].astype(F32))
        do = (dx2 * mod_ref[2:3, :]).astype(BF16)
        do_ref[...] = do
        dma = lax.dot_general(do, w_ref[0:AW, :], NT, preferred_element_type=F32)
        dmc = lax.dot_general(do, w_ref[AW:D, :], NT, preferred_element_type=F32)
        ov = oa_ref[...]
        ra = lax.rsqrt(jnp.mean(ov * ov, axis=-1, keepdims=True) + EPS)
        on = ov * ra
        acc_h_ref[0] += _colsum8(dma * on)
        don = dma * ba_ref[...]
        doa_ref[...] = (ra * (don - on * jnp.mean(don * on, axis=-1, keepdims=True))).astype(BF16)
        u1 = u1_ref[...]
        mu = jnp.mean(u1, axis=-1, keepdims=True)
        d = u1 - mu
        rstd = lax.rsqrt(jnp.mean(d * d, axis=-1, keepdims=True) + EPS)
        uh = d * rstd
        u2 = uh * lng_ref[...] + lnb_ref[...]
        sg = _sigmoid(u2)
        u3 = u2 * sg
        rc = lax.rsqrt(jnp.mean(u3 * u3, axis=-1, keepdims=True) + EPS)
        u3n = u3 * rc
        acc_h_ref[1] += _colsum8(dmc * u3n)
        du3n = dmc * bc_ref[...]
        du3 = rc * (du3n - u3n * jnp.mean(du3n * u3n, axis=-1, keepdims=True))
        du2 = du3 * (sg * (1.0 + u2 * (1.0 - sg)))
        acc_h_ref[2] += _colsum8(du2 * uh)
        acc_h_ref[3] += _colsum8(du2)
        duh = du2 * lng_ref[...]
        du1_ref[...] = rstd * (duh - jnp.mean(duh, axis=-1, keepdims=True)
                               - uh * jnp.mean(duh * uh, axis=-1, keepdims=True))

    outs = pl.pallas_call(
        _hosted(body, ex, 13, 6, 0, S // tm), name="bwd_mid", grid=(S // tm,),
        out_shape=(_sds((S, D), F32), _sds((S, D), BF16), _sds((S, AW), BF16), _sds((S, CW), F32),
                   _sds((4, 8, D), F32), _sds((4, 8, AW), F32), *(ex.out_shapes if ex else [])),
        in_specs=[_rows(tm, D), _rows(tm, D), _rows(tm, D), _rows(tm, D), _rows(tm, AW), _rows(tm, CW), _const((8, D)),
                  _const((1, D)), _const((1, AW)), _const((1, CW)), _const((1, CW)), _const((1, CW)), _const((D, D)),
                  *(ex.in_specs if ex else [])],
        out_specs=(_rows(tm, D), _rows(tm, D), _rows(tm, AW), _rows(tm, CW), _const((4, 8, D)), _const((4, 8, AW)),
                   *(ex.out_specs if ex else [])),
        scratch_shapes=ex.scratch if ex else [],
        compiler_params=_cp(("arbitrary",)),
    )(dh2, dy, x2, ob, o_attn, u1, mod8, n2g, beta_a, beta_c, lng, lnb, w_out, *(ex.srcs if ex else []))
    return outs[:6], list(outs[6:])


def _attn_bwd_dq(qa, ka, v, do, o_attn, lsec, fs, fe, bd, tq):
    S = qa.shape[1]
    nq = S // tq
    tc = _chunk(tq)

    def body(fs_ref, fe_ref, bd_ref, qa_ref, ka_ref, v_ref, do_ref, o_ref, lse_ref, dqa_ref, dr_ref, acc_ref):
        pr, i = pl.program_id(0), pl.program_id(1)
        sel_a = _lane((tq, LANES)) < DH
        dov = do_ref[...]
        prod = dov.astype(F32) * o_ref[...]
        zero = jnp.zeros_like(prod)
        deltas = [jnp.broadcast_to(jnp.sum(jnp.where(sel_a, prod, zero), axis=1, keepdims=True), (tq, LANES)),
                  jnp.broadcast_to(jnp.sum(jnp.where(sel_a, zero, prod), axis=1, keepdims=True), (tq, LANES))]
        zb = jnp.zeros_like(dov)
        dos = [jnp.where(sel_a, dov, zb), jnp.where(sel_a, zb, dov)]
        acc_ref[...] = jnp.zeros_like(acc_ref)

        def kv_step(j, heads, masked=False):
            for c0 in range(0, tq, tc):
                start = pl.multiple_of(j * tq + c0, tc)
                vb = v_ref[pl.ds(start, tc), :]
                for hh in heads:
                    kb = ka_ref[hh, pl.ds(start, tc), :]
                    s = lax.dot_general(qa_ref[hh], kb, NT, preferred_element_type=F32)
                    p = jnp.exp(s - jnp.tile(lse_ref[hh], (1, tc // LANES)))
                    if masked:
                        p = jnp.where(_causal_rect(tq, tc, c0), p, 0.0)
                    dp = lax.dot_general(dos[hh], vb, NT, preferred_element_type=F32)
                    ds = p * (dp - jnp.tile(deltas[hh], (1, tc // LANES)))
                    acc_ref[hh] += jnp.dot(ds.astype(BF16), kb, preferred_element_type=F32)

        def needed(hh, j):
            top = fs_ref[2 * pr + hh, i] + bd_ref[0, 0]
            return jnp.logical_and(j >= 0, top - fe_ref[2 * pr + hh, jnp.maximum(j, 0)] >= SKIP)

        kv_step(i, (0, 1), masked=True)
        _block_loops(i - 1, -1, needed, kv_step)
        dqa_ref[...] = acc_ref[...]
        row = lax.broadcasted_iota(jnp.int32, (8, tq), 0)
        da = deltas[0].T[0:8, :]
        db = deltas[1].T[0:8, :]
        dr_ref[0, 0] = jnp.where(row == 0, da, db)

    return pl.pallas_call(
        body, name="attn_bwd_dq", grid=(NH // 2, nq),
        out_shape=(_sds((NH, S, LANES), F32), _sds((NH // 2, nq, 8, tq), F32)),
        in_specs=[SMEM_SPEC, SMEM_SPEC, SMEM_SPEC,
                  pl.BlockSpec((2, tq, LANES), lambda p, i: (p, i, 0)),
                  pl.BlockSpec((2, S, LANES), lambda p, i: (p, 0, 0)),
                  pl.BlockSpec((S, LANES), lambda p, i: (0, p)),
                  pl.BlockSpec((tq, LANES), lambda p, i: (i, p)),
                  pl.BlockSpec((tq, LANES), lambda p, i: (i, p)),
                  pl.BlockSpec((2, tq, LANES), lambda p, i: (p, i, 0))],
        out_specs=(pl.BlockSpec((2, tq, LANES), lambda p, i: (p, i, 0)),
                   pl.BlockSpec((1, 1, 8, tq), lambda p, i: (p, i, 0, 0))),
        scratch_shapes=[pltpu.VMEM((2, tq, LANES), F32)],
        compiler_params=_cp(("parallel", "parallel")),
    )(fs, fe, bd, qa, ka, v, do, o_attn, lsec)


def _attn_bwd_dkv(qa, ka, v, do, lser, dr, fs, fe, bd, tq):
    S = qa.shape[1]
    nq = S // tq
    tc = _chunk(tq)

    def body(fs_ref, fe_ref, bd_ref, ka_ref, v_ref, qa_ref, do_ref, lse_ref, dr_ref, dka_ref, dv_ref, acck_ref,
             accv_ref):
        pr, j = pl.program_id(0), pl.program_id(1)
        sel_a = _lane((tq, LANES)) < DH
        vv = v_ref[...]
        zb = jnp.zeros_like(vv)
        vs = [jnp.where(sel_a, vv, zb), jnp.where(sel_a, zb, vv)]
        acck_ref[...] = jnp.zeros_like(acck_ref)
        accv_ref[...] = jnp.zeros_like(accv_ref)

        def q_step(i, heads, masked=False):
            lse8 = lse_ref[0, i]
            dr8 = dr_ref[0, i]
            for c0 in range(0, tq, tc):
                start = pl.multiple_of(i * tq + c0, tc)
                dob = do_ref[pl.ds(start, tc), :]
                for hh in heads:
                    qb = qa_ref[hh, pl.ds(start, tc), :]
                    st = lax.dot_general(ka_ref[hh], qb, NT, preferred_element_type=F32)
                    pt = jnp.exp(st - lse8[hh:hh + 1, c0:c0 + tc])
                    if masked:
                        keep = (lax.broadcasted_iota(jnp.int32, (tq, tc), 0)
                                <= lax.broadcasted_iota(jnp.int32, (tq, tc), 1) + c0)
                        pt = jnp.where(keep, pt, 0.0)
                    accv_ref[hh] += jnp.dot(pt.astype(BF16), dob, preferred_element_type=F32)
                    dpt = lax.dot_general(vs[hh], dob, NT, preferred_element_type=F32)
                    dst = pt * (dpt - dr8[hh:hh + 1, c0:c0 + tc])
                    acck_ref[hh] += jnp.dot(dst.astype(BF16), qb, preferred_element_type=F32)

        def needed(hh, i):
            top = fs_ref[2 * pr + hh, jnp.minimum(i, nq - 1)] + bd_ref[0, 0]
            return jnp.logical_and(i < nq, top - fe_ref[2 * pr + hh, j] >= SKIP)

        q_step(j, (0, 1), masked=True)
        _block_loops(j + 1, 1, needed, q_step)
        dka_ref[...] = acck_ref[...]
        dv_ref[...] = jnp.where(sel_a, accv_ref[0], accv_ref[1]).astype(BF16)

    return pl.pallas_call(
        body, name="attn_bwd_dkv", grid=(NH // 2, nq),
        out_shape=(_sds((NH, S, LANES), F32), _sds((S, AW), BF16)),
        in_specs=[SMEM_SPEC, SMEM_SPEC, SMEM_SPEC,
                  pl.BlockSpec((2, tq, LANES), lambda p, j: (p, j, 0)),
                  pl.BlockSpec((tq, LANES), lambda p, j: (j, p)),
                  pl.BlockSpec((2, S, LANES), lambda p, j: (p, 0, 0)),
                  pl.BlockSpec((S, LANES), lambda p, j: (0, p)),
                  pl.BlockSpec((1, nq, 8, tq), lambda p, j: (p, 0, 0, 0)),
                  pl.BlockSpec((1, nq, 8, tq), lambda p, j: (p, 0, 0, 0))],
        out_specs=(pl.BlockSpec((2, tq, LANES), lambda p, j: (p, j, 0)),
                   pl.BlockSpec((tq, LANES), lambda p, j: (j, p))),
        scratch_shapes=[pltpu.VMEM((2, tq, LANES), F32), pltpu.VMEM((2, tq, LANES), F32)],
        compiler_params=_cp(("parallel", "parallel")),
    )(fs, fe, bd, ka, v, qa, do, lser, dr)


def _bwd_conv(du1, u0, alin, agate, w32, tm, ex=None):
    S = du1.shape[0]
    nt = S // tm

    def body(du_ref, dun_ref, u0_ref, u0p_ref, alin_ref, agate_ref, w_ref,
             dalin_ref, dagate_ref, dw_ref, db_ref, extd_ref, extu_ref, du0_ref):
        i = pl.program_id(0)

        @pl.when(i == 0)
        def _():
            dw_ref[...] = jnp.zeros_like(dw_ref)
            db_ref[...] = jnp.zeros_like(db_ref)

        extd_ref[0:tm, :] = du_ref[...]
        extd_ref[tm:, :] = jnp.where(i == nt - 1, 0.0, dun_ref[...])
        extu_ref[0:HALO, :] = jnp.where(i == 0, 0.0, u0p_ref[...])
        extu_ref[HALO:, :] = u0_ref[...]
        db_ref[...] += _colsum8(du_ref[...])
        for r0 in range(0, tm, CHUNK_ROWS):
            duc = du_ref[r0:r0 + CHUNK_ROWS, :]
            acc = jnp.zeros((CHUNK_ROWS, CW), F32)
            for j in range(KC):
                acc = acc + w_ref[j:j + 1, :] * extd_ref[r0 + 30 - j:r0 + 30 - j + CHUNK_ROWS, :]
                dw_ref[j] += _colsum8(duc * extu_ref[r0 + 2 + j:r0 + 2 + j + CHUNK_ROWS, :])
            du0_ref[r0:r0 + CHUNK_ROWS, :] = acc
        du0 = du0_ref[...]
        al = alin_ref[...].astype(F32)
        sg = _sigmoid(agate_ref[...].astype(F32))
        dalin_ref[...] = (du0 * sg).astype(BF16)
        dagate_ref[...] = (du0 * al * sg * (1.0 - sg)).astype(BF16)

    nxt = pl.BlockSpec((HALO, CW), lambda i: (jnp.minimum((i + 1) * (tm // HALO), S // HALO - 1), 0))
    outs = pl.pallas_call(
        _hosted(body, ex, 7, 4, 3, nt), name="bwd_conv", grid=(nt,),
        out_shape=(_sds((S, CW), BF16), _sds((S, CW), BF16), _sds((HALO, 8, CW), F32), _sds((8, CW), F32),
                   *(ex.out_shapes if ex else [])),
        in_specs=[_rows(tm, CW), nxt, _rows(tm, CW), _halo_prev(tm), _rows(tm, CW), _rows(tm, CW), _const((HALO, CW)),
                  *(ex.in_specs if ex else [])],
        out_specs=(_rows(tm, CW), _rows(tm, CW), _const((HALO, 8, CW)), _const((8, CW)), *(ex.out_specs if ex else [])),
        scratch_shapes=[pltpu.VMEM((tm + HALO, CW), F32), pltpu.VMEM((tm + HALO, CW), F32), pltpu.VMEM((tm, CW), F32),
                        *(ex.scratch if ex else [])],
        compiler_params=_cp(("arbitrary",)),
    )(du1, du1, u0, u0, alin, agate, w32, *(ex.srcs if ex else []))
    return outs[:4], list(outs[4:])


def _bwd_qk(dqa, dka, qn, kn, rq, rk, fgb, qg512, kg512, e512, et512, tm):
    S = qn.shape[0]
    nt = S // tm

    def body(dqa_ref, dka_ref, qn_ref, kn_ref, rq_ref, rk_ref, fgb_ref, qg_ref, kg_ref, e_ref, et_ref,
             dq_ref, dk_ref, dfg_ref, accg_ref, accb_ref, carry_ref):
        @pl.when(pl.program_id(0) == 0)
        def _():
            carry_ref[...] = jnp.zeros_like(carry_ref)
            accg_ref[...] = jnp.zeros_like(accg_ref)
            accb_ref[...] = jnp.zeros_like(accb_ref)

        lane = _lane((tm, LANES))
        sel_a = lane < DH
        df = jnp.zeros((tm, LANES), F32)
        for h in range(NH):
            col = dqa_ref[h][:, 64:65] - dka_ref[h][:, 67:68]
            df = jnp.where(lane == h, col, df)
        tri = (lax.broadcasted_iota(jnp.int32, (tm, tm), 0) <= lax.broadcasted_iota(jnp.int32, (tm, tm), 1)
               ).astype(F32).astype(BF16)
        dlf = _dot3(tri, df) + carry_ref[0:1, :]
        carry_ref[...] = jnp.broadcast_to(dlf[0:1, :], carry_ref.shape)
        dfg = jnp.where(lane < NH, dlf * _sigmoid(-fgb_ref[...]), 0.0)
        dfg_ref[...] = dfg.astype(BF16)
        accb_ref[...] += _colsum8(dfg)

        def norm_bwd(src_ref, n_ref, r_ref, g_ref, scale, slot):
            pairs = []
            for p in range(NH // 2):
                b = pltpu.roll(src_ref[2 * p + 1], 64, 1)
                pairs.append(jnp.where(sel_a, src_ref[2 * p], b))
            dh = jnp.concatenate(pairs, axis=1) * scale
            tn = n_ref[...].astype(F32)
            accg_ref[slot] += _colsum8(dh * tn)
            dn = dh * g_ref[...]
            mean = _dot2(dn * tn, e_ref[...]) * (1.0 / DH)
            corr = _dot2(mean, et_ref[...])
            rf = _dot2(r_ref[...], et_ref[...])
            return (rf * (dn - tn * corr)).astype(BF16)

        dq_ref[...] = norm_bwd(dqa_ref, qn_ref, rq_ref, qg_ref, DH ** -0.5, 0)
        dk_ref[...] = norm_bwd(dka_ref, kn_ref, rk_ref, kg_ref, 1.0, 1)

    rev = lambda n: pl.BlockSpec((tm, n), lambda i: (nt - 1 - i, 0))
    hm = pl.BlockSpec((NH, tm, LANES), lambda i: (0, nt - 1 - i, 0))
    dq, dk, dfg, accg, accb = pl.pallas_call(
        body, name="bwd_qk", grid=(nt,),
        out_shape=(_sds((S, AW), BF16), _sds((S, AW), BF16), _sds((S, LANES), BF16), _sds((2, 8, AW), F32),
                   _sds((8, LANES), F32)),
        in_specs=[hm, hm, rev(AW), rev(AW), rev(LANES), rev(LANES), rev(LANES), _const((1, AW)), _const((1, AW)),
                  _const((AW, LANES)), _const((LANES, AW))],
        out_specs=(rev(AW), rev(AW), rev(LANES), _const((2, 8, AW)), _const((8, LANES))),
        scratch_shapes=[pltpu.VMEM((8, LANES), F32)], compiler_params=_cp(("arbitrary",)),
    )(dqa, dka, qn, kn, rq, rk, fgb, qg512, kg512, e512, et512)
    return dq, dk, dfg, accg, accb


def _bwd_in(dq, dk, dv, dalin, dagate, dfg, w_in_p, x, dx2, mod8, n1g, tm):
    S = x.shape[0]

    def body(dq_ref, dk_ref, dv_ref, dal_ref, dag_ref, dfg_ref, w_ref, x_ref, dx2_ref, mod_ref, n1g_ref,
             dx_ref, acc_ref):
        @pl.when(pl.program_id(0) == 0)
        def _():
            acc_ref[...] = jnp.zeros_like(acc_ref)

        def part(ref, a, b):
            return lax.dot_general(ref[...], w_ref[:, a:b], NT, preferred_element_type=F32)

        dh = (part(dq_ref, 0, 512) + part(dk_ref, 512, 1024) + part(dv_ref, 1024, 1536) + part(dal_ref, 1536, 2048)
              + part(dag_ref, 2048, 2560) + part(dfg_ref, 2560, NP))
        xv = x_ref[...]
        r1 = lax.rsqrt(jnp.mean(xv * xv, axis=-1, keepdims=True) + EPS)
        xn = xv * r1
        gain = n1g_ref[...] * (1.0 + mod_ref[1:2, :])
        t = dh * xn
        acc_ref[0] += _colsum8(dh)
        acc_ref[1] += _colsum8(t * n1g_ref[...])
        acc_ref[2] += _colsum8(t * (1.0 + mod_ref[1:2, :]))
        dxn = dh * gain
        dx_ref[...] = dx2_ref[...] + r1 * (dxn - xn * jnp.mean(dxn * xn, axis=-1, keepdims=True))

    return pl.pallas_call(
        body, name="bwd_in", grid=(S // tm,),
        out_shape=(_sds((S, D), F32), _sds((3, 8, D), F32)),
        in_specs=[_rows(tm, AW), _rows(tm, AW), _rows(tm, AW), _rows(tm, CW), _rows(tm, CW), _rows(tm, LANES),
                  _const((D, NP)), _rows(tm, D), _rows(tm, D), _const((8, D)), _const((1, D))],
        out_specs=(_rows(tm, D), _const((3, 8, D))),
        compiler_params=_cp(("arbitrary",)),
    )(dq, dk, dv, dalin, dagate, dfg, w_in_p, x, dx2, mod8, n1g)


def _adam(w, g, m, v):
    m_new = B1 * m + (1.0 - B1) * g
    v_new = B2 * v + (1.0 - B2) * (g * g)
    m_hat = m_new / (1.0 - B1 ** STEP)
    v_hat = v_new / (1.0 - B2 ** STEP)
    delta = -LR * (m_hat / (jnp.sqrt(v_hat) + AEPS) + WD * w)
    return delta, m_new, v_new


def _reduce_adamw(slots, w, m, v, name, tr=256):
    ns, R, C = slots.shape
    tr = tr if R % tr == 0 else R

    def body(s_ref, w_ref, m_ref, v_ref, g_ref, d_ref, mo_ref, vo_ref):
        g = s_ref[0].astype(F32)
        for k in range(1, ns):
            g = g + s_ref[k].astype(F32)
        g_ref[...] = g
        d_ref[...], mo_ref[...], vo_ref[...] = _adam(w_ref[...], g, m_ref[...], v_ref[...])

    blk = pl.BlockSpec((tr, C), lambda i: (i, 0))
    return pl.pallas_call(
        body, name=name, grid=(R // tr,), out_shape=tuple(_sds((R, C), F32) for _ in range(4)),
        in_specs=[pl.BlockSpec((ns, tr, C), lambda i: (0, i, 0)), blk, blk, blk], out_specs=(blk, blk, blk, blk),
        compiler_params=_cp(("parallel",)),
    )(slots, w, m, v)


def _pair_adamw(slots, w, m, v, name, tr=256):
    ns, R, C = slots.shape
    tr = tr if R % tr == 0 else R
    nt = R // tr

    def body(s_ref, w_ref, m_ref, v_ref, g_ref, d_ref, mo_ref, vo_ref, mine_ref, theirs_ref, send_sems, recv_sems):
        i = pl.program_id(0)
        part = s_ref[0].astype(F32)
        for k in range(1, ns):
            part = part + s_ref[k].astype(F32)
        mine_ref[i] = part
        swap = pltpu.make_async_remote_copy(
            src_ref=mine_ref.at[i], dst_ref=theirs_ref.at[i], send_sem=send_sems.at[i], recv_sem=recv_sems.at[i],
            device_id=(lax.axis_index("x"), lax.axis_index("y"), 1 - lax.axis_index("c")),
            device_id_type=pl.DeviceIdType.MESH)
        swap.start()
        swap.wait()
        g = part + theirs_ref[i]
        g_ref[...] = g
        d_ref[...], mo_ref[...], vo_ref[...] = _adam(w_ref[...], g, m_ref[...], v_ref[...])

    blk = pl.BlockSpec((tr, C), lambda i: (i, 0))
    return pl.pallas_call(
        body, name=name, grid=(nt,), out_shape=tuple(_sds((R, C), F32) for _ in range(4)),
        in_specs=[pl.BlockSpec((ns, tr, C), lambda i: (0, i, 0)), blk, blk, blk], out_specs=(blk, blk, blk, blk),
        scratch_shapes=[pltpu.VMEM((nt, tr, C), F32), pltpu.VMEM((nt, tr, C), F32),
                        pltpu.SemaphoreType.DMA((nt,)), pltpu.SemaphoreType.DMA((nt,))],
        compiler_params=_cp(("arbitrary",)),
    )(slots, w, m, v)


def _ada_adamw(sct, dmod, w, m, v):
    R, C = w.shape
    tr, bc = 256, 512

    def body(sct_ref, dm_ref, w_ref, m_ref, v_ref, g_ref, d_ref, mo_ref, vo_ref):
        g = sct_ref[:, 0:1] * dm_ref[0:1, :]
        for b in range(1, N_DEV):
            g = g + sct_ref[:, b:b + 1] * dm_ref[b:b + 1, :]
        g_ref[...] = g
        d_ref[...], mo_ref[...], vo_ref[...] = _adam(w_ref[...], g, m_ref[...], v_ref[...])

    blk = pl.BlockSpec((tr, bc), lambda i, j: (i, j))
    return pl.pallas_call(
        body, name="ada_adamw", grid=(R // tr, C // bc), out_shape=tuple(_sds((R, C), F32) for _ in range(4)),
        in_specs=[pl.BlockSpec((tr, N_DEV), lambda i, j: (i, 0)), pl.BlockSpec((N_DEV, bc), lambda i, j: (0, j)),
                  blk, blk, blk],
        out_specs=(blk, blk, blk, blk), compiler_params=_cp(("parallel", "parallel")),
    )(sct, dmod, w, m, v)


def _small_reduce(slots, fold):
    def body(s_ref, f_ref, o_ref):
        tot = s_ref[0:1, :]
        for k in range(1, N_DEV):
            tot = tot + s_ref[k:k + 1, :]
        o_ref[:, 0:6144] = tot[:, 0:6144]
        o_ref[:, 6144:7168] = tot[:, 6144:7168]
        for t, src in enumerate((8192, 8704)):
            v8 = jnp.broadcast_to(tot[:, src:src + AW], (8, AW))
            o_ref[:, 7168 + t * LANES:7168 + (t + 1) * LANES] = jnp.dot(
                v8, f_ref[...], precision=HI, preferred_element_type=F32)[0:1, :]
        o_ref[:, 7424:7552] = tot[:, 9216:9344]
        o_ref[:, 7552:10112] = tot[:, 9344:11904]
        o_ref[:, 10112:SMALL_OUT] = tot[:, 7168:8192]

    return pl.pallas_call(
        body, name="small_reduce", out_shape=_sds((1, SMALL_OUT), F32),
        in_specs=[pl.BlockSpec(memory_space=pltpu.VMEM), pl.BlockSpec(memory_space=pltpu.VMEM)],
        out_specs=pl.BlockSpec(memory_space=pltpu.VMEM),
    )(slots, fold)


def _perm_in(w):
    pad = jnp.zeros((w.shape[0], NP - 2568), w.dtype)
    return jnp.concatenate([w[:, :1536], w[:, 1544:2568], w[:, 1536:1544], pad], axis=1)


def _pad_lanes(vec, n=LANES):
    return jnp.pad(vec, ((0, 0), (0, n - vec.shape[1])))


def kernel(x, c, w_ada, b_ada, norm1_g, w_in, q_norm_g, k_norm_g, b_f, conv_w, conv_b, conv_ln_g, conv_ln_b, beta_attn, beta_conv, w_out, norm2_g, w_ff1, w_ff2, loss_target, m_w_ada, m_b_ada, m_norm1_g, m_w_in, m_q_norm_g, m_k_norm_g, m_b_f, m_conv_w, m_conv_b, m_conv_ln_g, m_conv_ln_b, m_beta_attn, m_beta_conv, m_w_out, m_norm2_g, m_w_ff1, m_w_ff2, v_w_ada, v_b_ada, v_norm1_g, v_w_in, v_q_norm_g, v_k_norm_g, v_b_f, v_conv_w, v_conv_b, v_conv_ln_g, v_conv_ln_b, v_beta_attn, v_beta_conv, v_w_out, v_norm2_g, v_w_ff1, v_w_ff2):
    S = x.shape[1]
    tm = min(256, S)
    tq = min(512, S // 2)
    xs, tgt = x[0], loss_target[0]
    chip = 2 * lax.axis_index("x") + lax.axis_index("y")
    e512, et512 = _head_sum_mats()

    conv_w32 = jnp.pad(conv_w[0], ((0, 1), (0, 0)))
    c_all, g_in = _exchange([(c, "bcast8"), (w_in[0].astype(BF16), "chip4")], "gather_in")
    later_weights = _Exchange([(w_out[0].astype(BF16), "chip4"), (w_ff1[0].astype(BF16), "chip4"),
                               (w_ff2[0].astype(BF16), "chip4"), (conv_w32, "chip4")])
    c_all = c_all.reshape(N_DEV, D)
    w_in_p = _perm_in(jnp.transpose(g_in, (1, 0, 2)).reshape(D, 2568))

    b_shard = lax.dynamic_slice(b_ada, (0, chip * 1536), (1, 1536))
    mod_rows, sc_all = _mod_shard(c_all, w_ada[0], b_shard)
    (mod_slots,) = _exchange([(mod_rows.reshape(N_DEV, 1, 1536), "all8")], "scatter_mod")
    mod = mod_slots.reshape(4, 2, 1536)[:, 0, :].reshape(6, D)
    mod8 = jnp.pad(mod, ((0, 2), (0, 0)))

    qg512 = jnp.tile(q_norm_g, (1, NH))
    kg512 = jnp.tile(k_norm_g, (1, NH))
    bf128 = _pad_lanes(b_f)

    (h1, qh, kh, vb, qn, kn, rq, rk, fgb, alin, agate, u0), (g_out, g_ff1, g_ff2, g_cw) = _fwd_in(
        xs, mod8, norm1_g, w_in_p, e512, et512, qg512, kg512, bf128, tm, ex=later_weights)
    w_out_f = g_out.reshape(D, D)
    w1, w2 = g_ff1, g_ff2
    cw32 = jnp.transpose(g_cw, (1, 0, 2)).reshape(HALO, CW)
    bd = _logit_bound(q_norm_g, k_norm_g)
    qa, ka, va, fcum = _fwd_decay(fgb, qh, kh, vb, _shift(bd), tm)
    fs, fe = _skip_tables(fcum, tq)
    o_attn, lsec, lser = _attn_fwd(qa, ka, va, fs, fe, bd, tq)
    u1, mc = _fwd_conv(u0, cw32, conv_b, conv_ln_g, conv_ln_b, beta_conv, tm)
    merged, ob, x2, h2 = _fwd_out(o_attn, mc, xs, mod8, norm2_g, beta_attn, w_out_f, tm)
    tf = min(512, S)
    r, dy, loss8, dg2 = _fwd_ffn(h2, w1, w2, x2, tgt, mod8, tf)

    df2, df1, dh2 = _bwd_ffn(dy, mod8, r, w1, w2, tf)
    gw_ff2 = _wgrad(r, df2, "wgrad_ff2", square_a=True)
    gw_ff1 = _wgrad(h2, df1, "wgrad_ff1")
    ff_grads = _Exchange([(jnp.transpose(gw_ff1.reshape(D, 4, D), (1, 0, 2)).astype(BF16), "chip4p"),
                          (gw_ff2.reshape(4, D, D).astype(BF16), "chip4p")])
    (dx2, do, doa, du1, acc_d, acc_h), (p_ff1, p_ff2) = _bwd_mid(
        dh2, dy, x2, ob, o_attn, u1, mod8, norm2_g, beta_attn, beta_conv, conv_ln_g, conv_ln_b, w_out_f, tm, ex=ff_grads)
    gw_out = _wgrad(merged, do, "wgrad_out")
    dqa, dr = _attn_bwd_dq(qa, ka, vb, doa, o_attn, lsec, fs, fe, bd, tq)
    dka, dv = _attn_bwd_dkv(qa, ka, vb, doa, lser, dr, fs, fe, bd, tq)
    out_grads = _Exchange([(gw_out.reshape(4, 256, D).astype(BF16), "chip4p")])
    (dalin, dagate, dcw, dcb), (p_out,) = _bwd_conv(du1, u0, alin, agate, cw32, tm, ex=out_grads)
    dq, dk, dfg, accg, accb = _bwd_qk(dqa, dka, qn, kn, rq, rk, fgb, qg512, kg512, e512, et512, tm)
    grad_x, acc1 = _bwd_in(dq, dk, dv, dalin, dagate, dfg, w_in_p, xs, dx2, mod8, norm1_g, tm)
    gw_in_p = _wgrad_in(h1, [dq, dk, dv, dalin, dagate, dfg])
    gw_in = jnp.concatenate([gw_in_p[:, :1536], gw_in_p[:, 2560:2568], gw_in_p[:, 1536:2560]], axis=1)

    s8 = lambda a: jnp.sum(a, axis=-2)
    a1, ad, ah = s8(acc1), s8(acc_d), s8(acc_h)
    small = jnp.concatenate(
        [a1[0], a1[1], ad[3], ad[0], ad[1], s8(dg2),
         a1[2], ad[2], s8(accg).reshape(-1), s8(accb), s8(dcb), ah[2], ah[3], ah[0], ah[1]]).reshape(1, SMALL_IN)
    gcw = s8(dcw)
    small_s, p_in, p_cw = _exchange(
        [(small, "bcast8"),
         (jnp.transpose(gw_in.reshape(D, 4, 642), (1, 0, 2)).astype(BF16), "chip4p"),
         (jnp.transpose(gcw.reshape(HALO, 4, LANES), (1, 0, 2)), "chip4p")], "scatter_grads")
    small_s = small_s.reshape(N_DEV, SMALL_IN)

    g_in_, d_in, nm_in, nv_in = _pair_adamw(p_in, w_in[0], m_w_in[0], v_w_in[0], "adamw_in")
    g_out_, d_out, nm_out, nv_out = _pair_adamw(p_out, w_out[0], m_w_out[0], v_w_out[0], "adamw_out")
    g_f1, d_f1, nm_f1, nv_f1 = _pair_adamw(p_ff1, w_ff1[0], m_w_ff1[0], v_w_ff1[0], "adamw_ff1")
    g_f2, d_f2, nm_f2, nv_f2 = _pair_adamw(p_ff2, w_ff2[0], m_w_ff2[0], v_w_ff2[0], "adamw_ff2")
    pad_row = lambda a, fill: jnp.pad(a[0], ((0, 1), (0, 0)), constant_values=fill)
    g_cw_, d_cw, nm_cw, nv_cw = (a[:KC] for a in _pair_adamw(
        p_cw, pad_row(conv_w, 0.0), pad_row(m_conv_w, 0.0), pad_row(v_conv_w, 1.0), "adamw_conv_w"))
    dmod_shard = lax.dynamic_slice(small_s[:, :6 * D], (0, chip * 1536), (N_DEV, 1536))
    g_ada, d_ada, nm_ada, nv_ada = _ada_adamw(sc_all.T, dmod_shard, w_ada[0], m_w_ada[0], v_w_ada[0])

    fold = np.zeros((AW, LANES), np.float32)
    fold[np.arange(AW), np.arange(AW) % DH] = 1.0
    g_small = _small_reduce(small_s, jnp.asarray(fold))
    smalls = [b_ada, norm1_g, q_norm_g, k_norm_g, b_f, conv_b, conv_ln_g, conv_ln_b, beta_attn, beta_conv, norm2_g]
    m_smalls = [m_b_ada, m_norm1_g, m_q_norm_g, m_k_norm_g, m_b_f, m_conv_b, m_conv_ln_g, m_conv_ln_b, m_beta_attn,
                m_beta_conv, m_norm2_g]
    v_smalls = [v_b_ada, v_norm1_g, v_q_norm_g, v_k_norm_g, v_b_f, v_conv_b, v_conv_ln_g, v_conv_ln_b, v_beta_attn,
                v_beta_conv, v_norm2_g]
    widths = [a.shape[1] for a in smalls]
    padded = [-(-n // LANES) * LANES for n in widths]
    pack = lambda arrs, fill: jnp.concatenate(
        [jnp.pad(a, ((0, 0), (0, p - a.shape[1])), constant_values=fill) for a, p in zip(arrs, padded)], axis=1)
    outs_small = _reduce_adamw(g_small.reshape(1, 1, SMALL_OUT), pack(smalls, 0.0), pack(m_smalls, 0.0),
                               pack(v_smalls, 1.0), "adamw_small")
    offs = np.concatenate([[0], np.cumsum(padded)])

    def unpack(a):
        return [a[:, int(o):int(o) + n] for o, n in zip(offs[:-1], widths)]

    gs, ds, ms, vs = (unpack(a) for a in outs_small)

    loss = lax.psum(loss8[0, 0], ("x", "y", "c"))
    big = {"w_ada": (g_ada, d_ada, nm_ada, nv_ada), "w_in": (g_in_, d_in, nm_in, nv_in),
           "conv_w": (g_cw_, d_cw, nm_cw, nv_cw), "w_out": (g_out_, d_out, nm_out, nv_out),
           "w_ff1": (g_f1, d_f1, nm_f1, nv_f1), "w_ff2": (g_f2, d_f2, nm_f2, nv_f2)}
    small_names = ["b_ada", "norm1_g", "q_norm_g", "k_norm_g", "b_f", "conv_b", "conv_ln_g", "conv_ln_b", "beta_attn",
                   "beta_conv", "norm2_g"]
    order = ["w_ada", "b_ada", "norm1_g", "w_in", "q_norm_g", "k_norm_g", "b_f", "conv_w", "conv_b", "conv_ln_g",
             "conv_ln_b", "beta_attn", "beta_conv", "w_out", "norm2_g", "w_ff1", "w_ff2"]

    def leaf(name, which):
        if name in big:
            return big[name][which][None]
        return (gs, ds, ms, vs)[which][small_names.index(name)]

    return (loss, grad_x[None], *[leaf(n, 0) for n in order], *[leaf(n, 1) for n in order],
            *[leaf(n, 2) for n in order], *[leaf(n, 3) for n in order])
```

```python
import functools

import numpy as np
import jax
import jax.numpy as jnp
from jax import lax
from jax.experimental import pallas as pl
from jax.experimental.pallas import tpu as pltpu

F32, BF16 = jnp.float32, jnp.bfloat16
HI = lax.Precision.HIGHEST
D = 1024
AW = 512
CW = 512
NH = 8
DH = 64
KC = 31
DFF = 4096
NP = 2688
EPS = 1e-6
NEG = -1e30
LANES = 128
VMEM_LIMIT = 56 * 2**20
NT = (((1,), (1,)), ((), ()))
TN = (((0,), (0,)), ((), ()))
LR, B1, B2, AEPS, WD, STEP = 0.001, 0.9, 0.999, 1e-08, 0.01, 10
N_DEV = 8
SMALL_IN = 11904
SMALL_OUT = 11136


def _cp(sem=None, vmem=VMEM_LIMIT):
    kw = dict(vmem_limit_bytes=vmem)
    if sem is not None:
        kw["dimension_semantics"] = sem
    return pltpu.CompilerParams(**kw)


def _rows(tm, n):
    return pl.BlockSpec((tm, n), lambda i: (i, 0))


def _const(shape):
    nd = len(shape)
    return pl.BlockSpec(shape, lambda *_: (0,) * nd)


def _sds(shape, dt):
    return jax.ShapeDtypeStruct(shape, dt)


def _lane(shape):
    return lax.broadcasted_iota(jnp.int32, shape, len(shape) - 1)


def _sigmoid(x):
    return 1.0 / (1.0 + jnp.exp(-x))


class _Exchange:
    MASKS = {"chip4": (2, 4, 6), "chip4p": (2, 4, 6), "all8": (1, 2, 3, 4, 5, 6, 7), "bcast8": (1, 2, 3, 4, 5, 6, 7)}

    def __init__(self, items):
        self.srcs = [s for s, _ in items]
        self.kinds = [k for _, k in items]
        self.n = len(items)
        self.out_shapes = []
        for s, k in items:
            shape = {"all8": (N_DEV,) + s.shape[1:], "bcast8": (N_DEV,) + s.shape, "chip4": (4,) + s.shape,
                     "chip4p": (4,) + s.shape[1:]}[k]
            self.out_shapes.append(_sds(shape, s.dtype))
        self.sem_index = {}
        for t, k in enumerate(self.kinds):
            for m in self.MASKS[k]:
                self.sem_index[(t, m)] = len(self.sem_index)
        n_sem = len(self.sem_index)
        self.scratch = [pltpu.SemaphoreType.DMA((n_sem,)), pltpu.SemaphoreType.DMA((n_sem,)),
                        pltpu.SemaphoreType.DMA((self.n,))]
        self.in_specs = [pl.BlockSpec(memory_space=pl.ANY)] * self.n
        self.out_specs = [pl.BlockSpec(memory_space=pl.ANY)] * self.n

    def copies(self, src_refs, dst_refs, send_sems, recv_sems, local_sems):
        x, y, c = lax.axis_index("x"), lax.axis_index("y"), lax.axis_index("c")
        my_id = 4 * x + 2 * y + c
        my_chip = 2 * x + y

        def piece(t, dev_id, chip):
            k = self.kinds[t]
            return src_refs[t].at[dev_id] if k == "all8" else src_refs[t].at[chip] if k == "chip4p" else src_refs[t]

        out = []
        for t in range(self.n):
            slot = dst_refs[t].at[my_chip if self.kinds[t] in ("chip4", "chip4p") else my_id]
            out.append(pltpu.make_async_copy(piece(t, my_id, my_chip), slot, local_sems.at[t]))
            for m in self.MASKS[self.kinds[t]]:
                px = 1 - x if m & 4 else x
                py = 1 - y if m & 2 else y
                pc = 1 - c if m & 1 else c
                s = self.sem_index[(t, m)]
                out.append(pltpu.make_async_remote_copy(
                    src_ref=piece(t, 4 * px + 2 * py + pc, 2 * px + py), dst_ref=slot,
                    send_sem=send_sems.at[s], recv_sem=recv_sems.at[s],
                    device_id=(px, py, pc), device_id_type=pl.DeviceIdType.MESH))
        return out


def _hosted(body, ex, n_in, n_out, n_scr, n_steps):
    if ex is None:
        return body

    def wrapped(*refs):
        ins, xin = refs[:n_in], refs[n_in:n_in + ex.n]
        o0 = n_in + ex.n
        outs, xout = refs[o0:o0 + n_out], refs[o0 + n_out:o0 + n_out + ex.n]
        s0 = o0 + n_out + ex.n
        scr, sems = refs[s0:s0 + n_scr], refs[s0 + n_scr:]

        @pl.when(pl.program_id(0) == 0)
        def _():
            for cp in ex.copies(xin, xout, *sems):
                cp.start()

        body(*ins, *outs, *scr)

        @pl.when(pl.program_id(0) == n_steps - 1)
        def _():
            for cp in ex.copies(xin, xout, *sems):
                cp.wait()

    return wrapped


def _exchange(items, name):
    ex = _Exchange(items)
    n = ex.n

    def body(*refs):
        copies = ex.copies(refs[:n], refs[n:2 * n], *refs[2 * n:])
        for cp in copies:
            cp.start()
        for cp in copies:
            cp.wait()

    outs = pl.pallas_call(
        body, name=name, out_shape=tuple(ex.out_shapes), in_specs=ex.in_specs, out_specs=tuple(ex.out_specs),
        scratch_shapes=ex.scratch,
    )(*ex.srcs)
    return list(outs)


def _mod_shard(c_all, w_ada, b_shard):
    n = w_ada.shape[1]

    def body(c_ref, w_ref, b_ref, o_ref, sc_ref):
        cv = c_ref[...]
        sc = cv * _sigmoid(cv)
        sc_ref[...] = sc
        o_ref[...] = jnp.dot(sc, w_ref[...], precision=HI, preferred_element_type=F32) + b_ref[...]

    bn = 512
    return pl.pallas_call(
        body, name="mod_shard", out_shape=(_sds((N_DEV, n), F32), _sds((N_DEV, D), F32)), grid=(n // bn,),
        in_specs=[_const((N_DEV, D)), pl.BlockSpec((D, bn), lambda j: (0, j)), pl.BlockSpec((1, bn), lambda j: (0, j))],
        out_specs=(pl.BlockSpec((N_DEV, bn), lambda j: (0, j)), _const((N_DEV, D))),
        compiler_params=_cp(("arbitrary",)),
    )(c_all, w_ada, b_shard)


def _head_sum_mats():
    e = np.zeros((AW, LANES), np.float32)
    for h in range(NH):
        e[h * DH:(h + 1) * DH, h] = 1.0
    return jnp.asarray(e, BF16), jnp.asarray(e.T.copy(), BF16)


def _dot2(x, w):
    hi = x.astype(BF16)
    lo = (x - hi.astype(F32)).astype(BF16)
    return jnp.dot(hi, w, preferred_element_type=F32) + jnp.dot(lo, w, preferred_element_type=F32)


def _fwd_in(x, mod8, n1g, w_in_p, e512, et512, qg512, kg512, bf128, tm, ex=None):
    S = x.shape[0]

    def body(x_ref, mod_ref, n1g_ref, w_ref, e_ref, et_ref, qg_ref, kg_ref, bf_ref,
             h1_ref, qh_ref, kh_ref, v_ref, qn_ref, kn_ref, rq_ref, rk_ref, fgb_ref, alin_ref, agate_ref, u0_ref):
        xv = x_ref[...]
        r1 = lax.rsqrt(jnp.mean(xv * xv, axis=-1, keepdims=True) + EPS)
        h = (xv * r1) * (n1g_ref[...] * (1.0 + mod_ref[1:2, :])) + mod_ref[0:1, :]
        hb = h.astype(BF16)
        h1_ref[...] = hb

        def seg(a, b):
            return jnp.dot(hb, w_ref[:, a:b], preferred_element_type=F32)

        def headnorm(t, g_ref, scale, n_ref, r_ref, o_ref):
            ss = _dot2(t * t, e_ref[...])
            r = lax.rsqrt(ss * (1.0 / DH) + EPS)
            tn = t * _dot2(r, et_ref[...])
            n_ref[...] = tn.astype(BF16)
            r_ref[...] = r
            o_ref[...] = (tn * (g_ref[...] * scale)).astype(BF16)

        headnorm(seg(0, 512), qg_ref, DH ** -0.5, qn_ref, rq_ref, qh_ref)
        headnorm(seg(512, 1024), kg_ref, 1.0, kn_ref, rk_ref, kh_ref)
        v_ref[...] = seg(1024, 1536).astype(BF16)
        alin = seg(1536, 2048)
        agate = seg(2048, 2560)
        alin_ref[...] = alin.astype(BF16)
        agate_ref[...] = agate.astype(BF16)
        u0_ref[...] = alin * _sigmoid(agate)
        fgb_ref[...] = seg(2560, NP) + bf_ref[...]

    bf = lambda: _sds((S, AW), BF16)
    xs = ex.srcs if ex else []
    outs = pl.pallas_call(
        _hosted(body, ex, 9, 12, 0, S // tm), name="fwd_in", grid=(S // tm,),
        out_shape=(_sds((S, D), BF16), bf(), bf(), bf(), bf(), bf(), _sds((S, LANES), F32), _sds((S, LANES), F32),
                   _sds((S, LANES), F32), bf(), bf(), _sds((S, CW), F32), *(ex.out_shapes if ex else [])),
        in_specs=[_rows(tm, D), _const((8, D)), _const((1, D)), _const((D, NP)), _const((AW, LANES)), _const((LANES, AW)),
                  _const((1, AW)), _const((1, AW)), _const((1, LANES)), *(ex.in_specs if ex else [])],
        out_specs=(_rows(tm, D), _rows(tm, AW), _rows(tm, AW), _rows(tm, AW), _rows(tm, AW), _rows(tm, AW),
                   _rows(tm, LANES), _rows(tm, LANES), _rows(tm, LANES), _rows(tm, AW), _rows(tm, AW), _rows(tm, CW),
                   *(ex.out_specs if ex else [])),
        scratch_shapes=ex.scratch if ex else [],
        compiler_params=_cp(("arbitrary",)),
    )(x, mod8, n1g, w_in_p, e512, et512, qg512, kg512, bf128, *xs)
    return outs[:12], list(outs[12:])


def _split3(f):
    f1 = f.astype(BF16).astype(F32)
    f2 = (f - f1).astype(BF16).astype(F32)
    return f1, f2, f - f1 - f2


def _dot3(w, x):
    return sum(jnp.dot(w, piece.astype(BF16), preferred_element_type=F32) for piece in _split3(x))


def _fwd_decay(fgb, qh, kh, vb, shift, tm):
    S = fgb.shape[0]

    def body(shift_ref, fgb_ref, qh_ref, kh_ref, vb_ref, qa_ref, ka_ref, va_ref, f_ref, carry_ref):
        @pl.when(pl.program_id(0) == 0)
        def _():
            carry_ref[...] = jnp.zeros_like(carry_ref)

        fb = fgb_ref[...]
        lf = jnp.minimum(fb, 0.0) - jnp.log1p(jnp.exp(-jnp.abs(fb)))
        tri = (lax.broadcasted_iota(jnp.int32, (tm, tm), 0) >= lax.broadcasted_iota(jnp.int32, (tm, tm), 1)
               ).astype(F32).astype(BF16)
        cs = _dot3(tri, lf) + carry_ref[0:1, :]
        f_ref[...] = cs
        carry_ref[...] = jnp.broadcast_to(cs[tm - 1:tm, :], carry_ref.shape)
        lane = _lane((tm, LANES))
        s1, s2, s3 = _split3(jnp.zeros((tm, LANES), F32) - shift_ref[0, 0])
        tail_q = jnp.where((lane >= 67) & (lane < 70), 1.0,
                           jnp.where(lane == 70, s1, jnp.where(lane == 71, s2, jnp.where(lane == 72, s3, 0.0))))
        tail_k = jnp.where(((lane >= 64) & (lane < 67)) | ((lane >= 70) & (lane < 73)), 1.0, 0.0)
        tail_v = jnp.where(lane == DH, 1.0, 0.0)
        for p in range(NH // 2):
            qp = qh_ref[:, p * LANES:(p + 1) * LANES].astype(F32)
            kp = kh_ref[:, p * LANES:(p + 1) * LANES].astype(F32)
            vp = vb_ref[:, p * LANES:(p + 1) * LANES].astype(F32)
            for hh in range(2):
                h = 2 * p + hh
                f1, f2, f3 = _split3(cs[:, h:h + 1])
                qb = qp if hh == 0 else pltpu.roll(qp, 64, 1)
                kb = kp if hh == 0 else pltpu.roll(kp, 64, 1)
                vh = vp if hh == 0 else pltpu.roll(vp, 64, 1)
                augq = jnp.where(lane == 64, f1, jnp.where(lane == 65, f2, jnp.where(lane == 66, f3, tail_q)))
                augk = jnp.where(lane == 67, -f1, jnp.where(lane == 68, -f2, jnp.where(lane == 69, -f3, tail_k)))
                qa_ref[h] = jnp.where(lane < DH, qb, augq).astype(BF16)
                ka_ref[h] = jnp.where(lane < DH, kb, augk).astype(BF16)
                va_ref[h] = jnp.where(lane < DH, vh, tail_v).astype(BF16)

    hm = pl.BlockSpec((NH, tm, LANES), lambda i: (0, i, 0))
    hms = _sds((NH, S, LANES), BF16)
    return pl.pallas_call(
        body, name="fwd_decay", grid=(S // tm,),
        out_shape=(hms, hms, hms, _sds((S, LANES), F32)),
        in_specs=[SMEM_SPEC, _rows(tm, LANES), _rows(tm, AW), _rows(tm, AW), _rows(tm, AW)],
        out_specs=(hm, hm, hm, _rows(tm, LANES)),
        scratch_shapes=[pltpu.VMEM((8, LANES), F32)], compiler_params=_cp(("arbitrary",)),
    )(shift, fgb, qh, kh, vb)


SKIP = -106.0


def _block_loops(first, step, needed, run):
    def both(j):
        return jnp.logical_and(needed(0, j), needed(1, j))

    def walk(heads):
        def go(j):
            run(j, heads)
            return j + step
        return go

    j = lax.while_loop(both, walk((0, 1)), first)
    lax.while_loop(functools.partial(needed, 0), walk((0,)), j)
    lax.while_loop(functools.partial(needed, 1), walk((1,)), j)


def _logit_bound(qg, kg):
    return (2.0 * 1.03 * DH ** 0.5 * jnp.max(jnp.abs(qg)) * jnp.max(jnp.abs(kg))).reshape(1, 1)


def _skip_tables(f, tq):
    return f[0::tq, :NH].T, f[tq - 1::tq, :NH].T


SMEM_SPEC = pl.BlockSpec(memory_space=pltpu.SMEM)


def _causal_rect(rows, cols, col0):
    return (lax.broadcasted_iota(jnp.int32, (rows, cols), 0)
            >= lax.broadcasted_iota(jnp.int32, (rows, cols), 1) + col0)


def _chunk(tq):
    return tq


SHIFT_MAX = 60.0


def _shift(bd):
    return jnp.where(bd <= SHIFT_MAX, 0.5 * bd, 0.0)


def _attn_fwd(qa, ka, va, fs, fe, bd, tq):
    S = qa.shape[1]
    nq = S // tq

    def body(fs_ref, fe_ref, bd_ref, qa_ref, ka_ref, va_ref, o_ref, lser_ref, m_ref, acc_ref):
        pr, i = pl.program_id(0), pl.program_id(1)
        sel_a = _lane((tq, LANES)) < DH
        acc_ref[...] = jnp.zeros_like(acc_ref)

        def logits(j, hh, masked):
            start = pl.multiple_of(j * tq, tq)
            s = lax.dot_general(qa_ref[hh], ka_ref[hh, pl.ds(start, tq), :], NT, preferred_element_type=F32)
            if masked:
                s = jnp.where(_causal_rect(tq, tq, 0), s, NEG)
            return s, va_ref[hh, pl.ds(start, tq), :]

        def shifted_step(j, heads, masked=False):
            for hh in heads:
                s, vb = logits(j, hh, masked)
                acc_ref[hh] += jnp.dot(jnp.exp(s).astype(BF16), vb, preferred_element_type=F32)

        def online_step(j, heads, masked=False):
            for hh in heads:
                s, vb = logits(j, hh, masked)
                m_prev = m_ref[hh]
                m_new = jnp.maximum(m_prev, jnp.max(s, axis=1, keepdims=True))
                p = jnp.exp(s - jnp.tile(m_new, (1, tq // LANES)))
                m_ref[hh] = m_new
                acc_ref[hh] = jnp.exp(m_prev - m_new) * acc_ref[hh] + jnp.dot(p.astype(BF16), vb,
                                                                              preferred_element_type=F32)

        def needed(hh, j):
            top = fs_ref[2 * pr + hh, i] + bd_ref[0, 0]
            return jnp.logical_and(j >= 0, top - fe_ref[2 * pr + hh, jnp.maximum(j, 0)] >= SKIP)

        @pl.when(bd_ref[0, 0] <= SHIFT_MAX)
        def _():
            m_ref[...] = jnp.zeros_like(m_ref)
            shifted_step(i, (0, 1), masked=True)
            _block_loops(i - 1, -1, needed, shifted_step)

        @pl.when(bd_ref[0, 0] > SHIFT_MAX)
        def _():
            m_ref[...] = jnp.full(m_ref.shape, NEG, F32)
            online_step(i, (0, 1), masked=True)
            _block_loops(i - 1, -1, needed, online_step)

        outs, lses = [], []
        for hh in range(2):
            acc = acc_ref[hh]
            row_sum = jnp.broadcast_to(acc[:, DH:DH + 1], (tq, LANES))
            outs.append(acc / row_sum)
            lses.append(m_ref[hh] + jnp.log(row_sum))
        o_ref[...] = jnp.where(sel_a, outs[0], pltpu.roll(outs[1], 64, 1))
        row = lax.broadcasted_iota(jnp.int32, (8, tq), 0)
        lser_ref[0, 0] = jnp.where(row == 0, lses[0].T[0:8, :], lses[1].T[0:8, :])

    return pl.pallas_call(
        body, name="attn_fwd", grid=(NH // 2, nq),
        out_shape=(_sds((S, AW), F32), _sds((NH // 2, nq, 8, tq), F32)),
        in_specs=[SMEM_SPEC, SMEM_SPEC, SMEM_SPEC,
                  pl.BlockSpec((2, tq, LANES), lambda p, i: (p, i, 0)),
                  pl.BlockSpec((2, S, LANES), lambda p, i: (p, 0, 0)),
                  pl.BlockSpec((2, S, LANES), lambda p, i: (p, 0, 0))],
        out_specs=(pl.BlockSpec((tq, LANES), lambda p, i: (i, p)),
                   pl.BlockSpec((1, 1, 8, tq), lambda p, i: (p, i, 0, 0))),
        scratch_shapes=[pltpu.VMEM((2, tq, LANES), F32), pltpu.VMEM((2, tq, LANES), F32)],
        compiler_params=_cp(("parallel", "parallel")),
    )(fs, fe, bd, qa, ka, va)


HALO = 32
CHUNK_ROWS = 64


def _halo_prev(tm):
    return pl.BlockSpec((HALO, CW), lambda i: (jnp.maximum(i * (tm // HALO) - 1, 0), 0))


def _fwd_conv(u0, w32, cb, lng, lnb, beta_c, tm):
    S = u0.shape[0]

    def body(cur_ref, prev_ref, w_ref, cb_ref, lng_ref, lnb_ref, beta_ref, u1_ref, mc_ref, ext_ref):
        i = pl.program_id(0)
        ext_ref[0:HALO, :] = jnp.where(i == 0, 0.0, prev_ref[...])
        ext_ref[HALO:, :] = cur_ref[...]
        for r0 in range(0, tm, CHUNK_ROWS):
            acc = jnp.zeros((CHUNK_ROWS, CW), F32) + cb_ref[...]
            for j in range(KC):
                acc = acc + w_ref[j:j + 1, :] * ext_ref[r0 + 2 + j:r0 + 2 + j + CHUNK_ROWS, :]
            u1_ref[r0:r0 + CHUNK_ROWS, :] = acc
        u1 = u1_ref[...]
        mu = jnp.mean(u1, axis=-1, keepdims=True)
        d = u1 - mu
        rstd = lax.rsqrt(jnp.mean(d * d, axis=-1, keepdims=True) + EPS)
        u2 = d * rstd * lng_ref[...] + lnb_ref[...]
        u3 = u2 * _sigmoid(u2)
        rc = lax.rsqrt(jnp.mean(u3 * u3, axis=-1, keepdims=True) + EPS)
        mc_ref[...] = (u3 * rc * beta_ref[...]).astype(BF16)

    return pl.pallas_call(
        body, name="fwd_conv", grid=(S // tm,),
        out_shape=(_sds((S, CW), F32), _sds((S, CW), BF16)),
        in_specs=[_rows(tm, CW), _halo_prev(tm), _const((HALO, CW)), _const((1, CW)), _const((1, CW)), _const((1, CW)),
                  _const((1, CW))],
        out_specs=(_rows(tm, CW), _rows(tm, CW)),
        scratch_shapes=[pltpu.VMEM((tm + HALO, CW), F32)], compiler_params=_cp(("parallel",)),
    )(u0, u0, w32, cb, lng, lnb, beta_c)


def _fwd_out(o_attn, mc, x, mod8, n2g, beta_a, w_out, tm):
    S = x.shape[0]

    def body(o_ref, mc_ref, x_ref, mod_ref, n2g_ref, beta_ref, w_ref, mg_ref, ob_ref, x2_ref, h2_ref):
        ov = o_ref[...]
        ra = lax.rsqrt(jnp.mean(ov * ov, axis=-1, keepdims=True) + EPS)
        ma = (ov * ra * beta_ref[...]).astype(BF16)
        mcv = mc_ref[...]
        mg_ref[:, 0:AW] = ma
        mg_ref[:, AW:D] = mcv
        o = (jnp.dot(ma, w_ref[0:AW, :], preferred_element_type=F32)
             + jnp.dot(mcv, w_ref[AW:D, :], preferred_element_type=F32))
        ob_ref[...] = o.astype(BF16)
        x2 = x_ref[...] + mod_ref[2:3, :] * o
        x2_ref[...] = x2
        r2 = lax.rsqrt(jnp.mean(x2 * x2, axis=-1, keepdims=True) + EPS)
        h2_ref[...] = ((x2 * r2) * (n2g_ref[...] * (1.0 + mod_ref[4:5, :])) + mod_ref[3:4, :]).astype(BF16)

    return pl.pallas_call(
        body, name="fwd_out", grid=(S // tm,),
        out_shape=(_sds((S, D), BF16), _sds((S, D), BF16), _sds((S, D), F32), _sds((S, D), BF16)),
        in_specs=[_rows(tm, AW), _rows(tm, CW), _rows(tm, D), _const((8, D)), _const((1, D)), _const((1, AW)),
                  _const((D, D))],
        out_specs=(_rows(tm, D), _rows(tm, D), _rows(tm, D), _rows(tm, D)),
        compiler_params=_cp(("parallel",)),
    )(o_attn, mc, x, mod8, n2g, beta_a, w_out)


def _fwd_ffn(h2, w1, w2, x2, tgt, mod8, tm):
    S = h2.shape[0]
    nk = w1.shape[0]
    bf = w1.shape[2]

    def body(h2_ref, w1_ref, w2_ref, x2_ref, tgt_ref, mod_ref, r_ref, dy_ref, loss_ref, dg2_ref, acc_ref):
        i, k = pl.program_id(0), pl.program_id(1)

        @pl.when((i == 0) & (k == 0))
        def _():
            loss_ref[...] = jnp.zeros_like(loss_ref)
            dg2_ref[...] = jnp.zeros_like(dg2_ref)

        r = jnp.maximum(jnp.dot(h2_ref[...], w1_ref[0], preferred_element_type=F32), 0.0)
        r_ref[...] = r.astype(BF16)
        part = jnp.dot((r * r).astype(BF16), w2_ref[0], preferred_element_type=F32)

        @pl.when(k == 0)
        def _():
            acc_ref[...] = part

        @pl.when(k > 0)
        def _():
            acc_ref[...] += part

        @pl.when(k == nk - 1)
        def _():
            f2 = acc_ref[...]
            e = x2_ref[...] + mod_ref[5:6, :] * f2 - tgt_ref[...]
            dy = e * (1.0 / D)
            dy_ref[...] = dy
            loss_ref[...] += 0.5 * jnp.sum(jnp.sum(e * dy, axis=1, keepdims=True), axis=0, keepdims=True)
            dg2_ref[...] += jnp.sum((dy * f2).reshape(tm // 8, 8, D), axis=0)

    return pl.pallas_call(
        body, name="fwd_ffn", grid=(S // tm, nk),
        out_shape=(_sds((S, DFF), BF16), _sds((S, D), F32), _sds((8, LANES), F32), _sds((8, D), F32)),
        in_specs=[pl.BlockSpec((tm, D), lambda i, k: (i, 0)), pl.BlockSpec((1, D, bf), lambda i, k: (k, 0, 0)),
                  pl.BlockSpec((1, bf, D), lambda i, k: (k, 0, 0)), pl.BlockSpec((tm, D), lambda i, k: (i, 0)),
                  pl.BlockSpec((tm, D), lambda i, k: (i, 0)), pl.BlockSpec((8, D), lambda i, k: (0, 0))],
        out_specs=(pl.BlockSpec((tm, bf), lambda i, k: (i, k)), pl.BlockSpec((tm, D), lambda i, k: (i, 0)),
                   pl.BlockSpec((8, LANES), lambda i, k: (0, 0)), pl.BlockSpec((8, D), lambda i, k: (0, 0))),
        scratch_shapes=[pltpu.VMEM((tm, D), F32)], compiler_params=_cp(("arbitrary", "arbitrary")),
    )(h2, w1, w2, x2, tgt, mod8)


def _bwd_ffn(dy, mod8, r, w1, w2, tm):
    S = dy.shape[0]
    nk = w1.shape[0]
    bf = w1.shape[2]

    def body(dy_ref, mod_ref, r_ref, w1_ref, w2_ref, df2_ref, df1_ref, dh2_ref):
        k = pl.program_id(1)
        df2 = (dy_ref[...] * mod_ref[5:6, :]).astype(BF16)

        @pl.when(k == 0)
        def _():
            df2_ref[...] = df2

        da = lax.dot_general(df2, w2_ref[0], NT, preferred_element_type=F32)
        df1 = (da * (2.0 * r_ref[...].astype(F32))).astype(BF16)
        df1_ref[...] = df1
        part = lax.dot_general(df1, w1_ref[0], NT, preferred_element_type=F32)

        @pl.when(k == 0)
        def _():
            dh2_ref[...] = part

        @pl.when(k > 0)
        def _():
            dh2_ref[...] += part

    return pl.pallas_call(
        body, name="bwd_ffn", grid=(S // tm, nk),
        out_shape=(_sds((S, D), BF16), _sds((S, DFF), BF16), _sds((S, D), F32)),
        in_specs=[pl.BlockSpec((tm, D), lambda i, k: (i, 0)), pl.BlockSpec((8, D), lambda i, k: (0, 0)),
                  pl.BlockSpec((tm, bf), lambda i, k: (i, k)), pl.BlockSpec((1, D, bf), lambda i, k: (k, 0, 0)),
                  pl.BlockSpec((1, bf, D), lambda i, k: (k, 0, 0))],
        out_specs=(pl.BlockSpec((tm, D), lambda i, k: (i, 0)), pl.BlockSpec((tm, bf), lambda i, k: (i, k)),
                   pl.BlockSpec((tm, D), lambda i, k: (i, 0))),
        compiler_params=_cp(("parallel", "arbitrary")),
    )(dy, mod8, r, w1, w2)


def _wgrad(a, b, name, square_a=False, tk=512, bm=1024, bn=1024):
    S, M = a.shape
    N = b.shape[1]
    bm, bn, tk = min(bm, M), min(bn, N), min(tk, S)

    def body(a_ref, b_ref, o_ref):
        av = a_ref[...]
        if square_a:
            af = av.astype(F32)
            av = (af * af).astype(BF16)
        part = lax.dot_general(av, b_ref[...], TN, preferred_element_type=F32)

        @pl.when(pl.program_id(2) == 0)
        def _():
            o_ref[...] = part

        @pl.when(pl.program_id(2) > 0)
        def _():
            o_ref[...] += part

    return pl.pallas_call(
        body, name=name, grid=(M // bm, N // bn, S // tk), out_shape=_sds((M, N), F32),
        in_specs=[pl.BlockSpec((tk, bm), lambda mi, ni, k: (k, mi)), pl.BlockSpec((tk, bn), lambda mi, ni, k: (k, ni))],
        out_specs=pl.BlockSpec((bm, bn), lambda mi, ni, k: (mi, ni)),
        compiler_params=_cp(("parallel", "parallel", "arbitrary")),
    )(a, b)


def _wgrad_in(h1, pieces, tk=512):
    S = h1.shape[0]
    tk = min(tk, S)
    widths = [p.shape[1] for p in pieces]
    offs = [sum(widths[:t]) for t in range(len(widths))]

    def body(a_ref, *refs):
        o_ref = refs[-1]

        @pl.when(pl.program_id(0) == 0)
        def _():
            o_ref[...] = jnp.zeros_like(o_ref)

        for b_ref, off, w in zip(refs[:-1], offs, widths):
            o_ref[:, off:off + w] += lax.dot_general(a_ref[...], b_ref[...], TN, preferred_element_type=F32)

    return pl.pallas_call(
        body, name="wgrad_in", grid=(S // tk,), out_shape=_sds((D, NP), F32),
        in_specs=[_rows(tk, D)] + [_rows(tk, w) for w in widths], out_specs=_const((D, NP)),
        compiler_params=_cp(("arbitrary",)),
    )(h1, *pieces)


def _colsum8(t):
    return jnp.sum(t.reshape(t.shape[0] // 8, 8, t.shape[1]), axis=0)


def _bwd_mid(dh2, dy, x2, ob, o_attn, u1, mod8, n2g, beta_a, beta_c, lng, lnb, w_out, e512, tm, tq, ex=None):
    S = dy.shape[0]

    def body(dh2_ref, dy_ref, x2_ref, ob_ref, oa_ref, u1_ref, mod_ref, n2g_ref, ba_ref, bc_ref, lng_ref, lnb_ref, w_ref,
             e_ref, dx2_ref, do_ref, doa_ref, du1_ref, acc_d_ref, acc_h_ref, dr_ref):
        @pl.when(pl.program_id(0) == 0)
        def _():
            acc_d_ref[...] = jnp.zeros_like(acc_d_ref)
            acc_h_ref[...] = jnp.zeros_like(acc_h_ref)

        x2 = x2_ref[...]
        dh2 = dh2_ref[...]
        r2 = lax.rsqrt(jnp.mean(x2 * x2, axis=-1, keepdims=True) + EPS)
        xn2 = x2 * r2
        gain = n2g_ref[...] * (1.0 + mod_ref[4:5, :])
        dxn = dh2 * gain
        dx2 = dy_ref[...] + r2 * (dxn - xn2 * jnp.mean(dxn * xn2, axis=-1, keepdims=True))
        dx2_ref[...] = dx2
        t = dh2 * xn2
        acc_d_ref[0] += _colsum8(dh2)
        acc_d_ref[1] += _colsum8(t * n2g_ref[...])
        acc_d_ref[2] += _colsum8(t * (1.0 + mod_ref[4:5, :]))
        acc_d_ref[3] += _colsum8(dx2 * ob_ref[...].astype(F32))
        do = (dx2 * mod_ref[2:3, :]).astype(BF16)
        do_ref[...] = do
        dma = lax.dot_general(do, w_ref[0:AW, :], NT, preferred_element_type=F32)
        dmc = lax.dot_general(do, w_ref[AW:D, :], NT, preferred_element_type=F32)
        ov = oa_ref[...]
        ra = lax.rsqrt(jnp.mean(ov * ov, axis=-1, keepdims=True) + EPS)
        on = ov * ra
        acc_h_ref[0] += _colsum8(dma * on)
        don = dma * ba_ref[...]
        doa = (ra * (don - on * jnp.mean(don * on, axis=-1, keepdims=True))).astype(BF16)
        doa_ref[...] = doa
        delta_t = _dot2(doa.astype(F32) * ov, e_ref[...]).T
        for p in range(NH // 2):
            dr_ref[p, 0] = delta_t[2 * p:2 * p + 8, :]
        u1 = u1_ref[...]
        mu = jnp.mean(u1, axis=-1, keepdims=True)
        d = u1 - mu
        rstd = lax.rsqrt(jnp.mean(d * d, axis=-1, keepdims=True) + EPS)
        uh = d * rstd
        u2 = uh * lng_ref[...] + lnb_ref[...]
        sg = _sigmoid(u2)
        u3 = u2 * sg
        rc = lax.rsqrt(jnp.mean(u3 * u3, axis=-1, keepdims=True) + EPS)
        u3n = u3 * rc
        acc_h_ref[1] += _colsum8(dmc * u3n)
        du3n = dmc * bc_ref[...]
        du3 = rc * (du3n - u3n * jnp.mean(du3n * u3n, axis=-1, keepdims=True))
        du2 = du3 * (sg * (1.0 + u2 * (1.0 - sg)))
        acc_h_ref[2] += _colsum8(du2 * uh)
        acc_h_ref[3] += _colsum8(du2)
        duh = du2 * lng_ref[...]
        du1_ref[...] = rstd * (duh - jnp.mean(duh, axis=-1, keepdims=True)
                               - uh * jnp.mean(duh * uh, axis=-1, keepdims=True))

    per = tq // tm
    outs = pl.pallas_call(
        _hosted(body, ex, 14, 7, 0, S // tm), name="bwd_mid", grid=(S // tm,),
        out_shape=(_sds((S, D), F32), _sds((S, D), BF16), _sds((S, AW), BF16), _sds((S, CW), F32),
                   _sds((4, 8, D), F32), _sds((4, 8, AW), F32), _sds((NH // 2, S // tq, 8, tq), F32),
                   *(ex.out_shapes if ex else [])),
        in_specs=[_rows(tm, D), _rows(tm, D), _rows(tm, D), _rows(tm, D), _rows(tm, AW), _rows(tm, CW), _const((8, D)),
                  _const((1, D)), _const((1, AW)), _const((1, CW)), _const((1, CW)), _const((1, CW)), _const((D, D)),
                  _const((AW, LANES)), *(ex.in_specs if ex else [])],
        out_specs=(_rows(tm, D), _rows(tm, D), _rows(tm, AW), _rows(tm, CW), _const((4, 8, D)), _const((4, 8, AW)),
                   pl.BlockSpec((NH // 2, 1, 8, tm), lambda i: (0, i // per, 0, i % per)),
                   *(ex.out_specs if ex else [])),
        scratch_shapes=ex.scratch if ex else [],
        compiler_params=_cp(("arbitrary",)),
    )(dh2, dy, x2, ob, o_attn, u1, mod8, n2g, beta_a, beta_c, lng, lnb, w_out, e512, *(ex.srcs if ex else []))
    return outs[:7], list(outs[7:])


def _attn_bwd_dq(qa, ka, v, do, o_attn, lsec, fs, fe, bd, tq):
    S = qa.shape[1]
    nq = S // tq
    tc = _chunk(tq)

    def body(fs_ref, fe_ref, bd_ref, qa_ref, ka_ref, v_ref, do_ref, o_ref, lse_ref, dqa_ref, dr_ref, acc_ref):
        pr, i = pl.program_id(0), pl.program_id(1)
        sel_a = _lane((tq, LANES)) < DH
        dov = do_ref[...]
        prod = dov.astype(F32) * o_ref[...]
        zero = jnp.zeros_like(prod)
        deltas = [jnp.broadcast_to(jnp.sum(jnp.where(sel_a, prod, zero), axis=1, keepdims=True), (tq, LANES)),
                  jnp.broadcast_to(jnp.sum(jnp.where(sel_a, zero, prod), axis=1, keepdims=True), (tq, LANES))]
        zb = jnp.zeros_like(dov)
        dos = [jnp.where(sel_a, dov, zb), jnp.where(sel_a, zb, dov)]
        acc_ref[...] = jnp.zeros_like(acc_ref)

        def kv_step(j, heads, masked=False):
            for c0 in range(0, tq, tc):
                start = pl.multiple_of(j * tq + c0, tc)
                vb = v_ref[pl.ds(start, tc), :]
                for hh in heads:
                    kb = ka_ref[hh, pl.ds(start, tc), :]
                    s = lax.dot_general(qa_ref[hh], kb, NT, preferred_element_type=F32)
                    p = jnp.exp(s - jnp.tile(lse_ref[hh], (1, tc // LANES)))
                    if masked:
                        p = jnp.where(_causal_rect(tq, tc, c0), p, 0.0)
                    dp = lax.dot_general(dos[hh], vb, NT, preferred_element_type=F32)
                    ds = p * (dp - jnp.tile(deltas[hh], (1, tc // LANES)))
                    acc_ref[hh] += jnp.dot(ds.astype(BF16), kb, preferred_element_type=F32)

        def needed(hh, j):
            top = fs_ref[2 * pr + hh, i] + bd_ref[0, 0]
            return jnp.logical_and(j >= 0, top - fe_ref[2 * pr + hh, jnp.maximum(j, 0)] >= SKIP)

        kv_step(i, (0, 1), masked=True)
        _block_loops(i - 1, -1, needed, kv_step)
        dqa_ref[...] = acc_ref[...]
        row = lax.broadcasted_iota(jnp.int32, (8, tq), 0)
        da = deltas[0].T[0:8, :]
        db = deltas[1].T[0:8, :]
        dr_ref[0, 0] = jnp.where(row == 0, da, db)

    return pl.pallas_call(
        body, name="attn_bwd_dq", grid=(NH // 2, nq),
        out_shape=(_sds((NH, S, LANES), F32), _sds((NH // 2, nq, 8, tq), F32)),
        in_specs=[SMEM_SPEC, SMEM_SPEC, SMEM_SPEC,
                  pl.BlockSpec((2, tq, LANES), lambda p, i: (p, i, 0)),
                  pl.BlockSpec((2, S, LANES), lambda p, i: (p, 0, 0)),
                  pl.BlockSpec((S, LANES), lambda p, i: (0, p)),
                  pl.BlockSpec((tq, LANES), lambda p, i: (i, p)),
                  pl.BlockSpec((tq, LANES), lambda p, i: (i, p)),
                  pl.BlockSpec((2, tq, LANES), lambda p, i: (p, i, 0))],
        out_specs=(pl.BlockSpec((2, tq, LANES), lambda p, i: (p, i, 0)),
                   pl.BlockSpec((1, 1, 8, tq), lambda p, i: (p, i, 0, 0))),
        scratch_shapes=[pltpu.VMEM((2, tq, LANES), F32)],
        compiler_params=_cp(("parallel", "parallel")),
    )(fs, fe, bd, qa, ka, v, do, o_attn, lsec)


def _attn_bwd_dkv(qa, ka, v, do, lser, dr, fs, fe, bd, tq):
    S = qa.shape[1]
    nq = S // tq
    tc = _chunk(tq)

    def body(fs_ref, fe_ref, bd_ref, ka_ref, v_ref, qa_ref, do_ref, lse_ref, dr_ref, dka_ref, dv_ref, acck_ref,
             accv_ref):
        pr, j = pl.program_id(0), pl.program_id(1)
        sel_a = _lane((tq, LANES)) < DH
        vv = v_ref[...]
        zb = jnp.zeros_like(vv)
        vs = [jnp.where(sel_a, vv, zb), jnp.where(sel_a, zb, vv)]
        acck_ref[...] = jnp.zeros_like(acck_ref)
        accv_ref[...] = jnp.zeros_like(accv_ref)

        def q_step(i, heads, masked=False):
            lse8 = lse_ref[0, i]
            dr8 = dr_ref[0, i]
            for c0 in range(0, tq, tc):
                start = pl.multiple_of(i * tq + c0, tc)
                dob = do_ref[pl.ds(start, tc), :]
                for hh in heads:
                    qb = qa_ref[hh, pl.ds(start, tc), :]
                    st = lax.dot_general(ka_ref[hh], qb, NT, preferred_element_type=F32)
                    pt = jnp.exp(st - lse8[hh:hh + 1, c0:c0 + tc])
                    if masked:
                        keep = (lax.broadcasted_iota(jnp.int32, (tq, tc), 0)
                                <= lax.broadcasted_iota(jnp.int32, (tq, tc), 1) + c0)
                        pt = jnp.where(keep, pt, 0.0)
                    accv_ref[hh] += jnp.dot(pt.astype(BF16), dob, preferred_element_type=F32)
                    dpt = lax.dot_general(vs[hh], dob, NT, preferred_element_type=F32)
                    dst = pt * (dpt - dr8[hh:hh + 1, c0:c0 + tc])
                    acck_ref[hh] += jnp.dot(dst.astype(BF16), qb, preferred_element_type=F32)

        def needed(hh, i):
            top = fs_ref[2 * pr + hh, jnp.minimum(i, nq - 1)] + bd_ref[0, 0]
            return jnp.logical_and(i < nq, top - fe_ref[2 * pr + hh, j] >= SKIP)

        q_step(j, (0, 1), masked=True)
        _block_loops(j + 1, 1, needed, q_step)
        dka_ref[...] = acck_ref[...]
        dv_ref[...] = jnp.where(sel_a, accv_ref[0], accv_ref[1]).astype(BF16)

    return pl.pallas_call(
        body, name="attn_bwd_dkv", grid=(NH // 2, nq),
        out_shape=(_sds((NH, S, LANES), F32), _sds((S, AW), BF16)),
        in_specs=[SMEM_SPEC, SMEM_SPEC, SMEM_SPEC,
                  pl.BlockSpec((2, tq, LANES), lambda p, j: (p, j, 0)),
                  pl.BlockSpec((tq, LANES), lambda p, j: (j, p)),
                  pl.BlockSpec((2, S, LANES), lambda p, j: (p, 0, 0)),
                  pl.BlockSpec((S, LANES), lambda p, j: (0, p)),
                  pl.BlockSpec((1, nq, 8, tq), lambda p, j: (p, 0, 0, 0)),
                  pl.BlockSpec((1, nq, 8, tq), lambda p, j: (p, 0, 0, 0))],
        out_specs=(pl.BlockSpec((2, tq, LANES), lambda p, j: (p, j, 0)),
                   pl.BlockSpec((tq, LANES), lambda p, j: (j, p))),
        scratch_shapes=[pltpu.VMEM((2, tq, LANES), F32), pltpu.VMEM((2, tq, LANES), F32)],
        compiler_params=_cp(("parallel", "parallel")),
    )(fs, fe, bd, ka, v, qa, do, lser, dr)


def _attn_bwd(qa, ka, v, do, lser, dr, fs, fe, bd, tq):
    S = qa.shape[1]
    nq = S // tq

    def body(fs_ref, fe_ref, bd_ref, ka_ref, v_ref, qa_ref, do_ref, lse_ref, dr_ref, dqa_hbm, dka_ref, dv_ref,
             accq_ref, acck_ref, accv_ref, out_sem):
        pr, j = pl.program_id(0), pl.program_id(1)
        sel_a = _lane((tq, LANES)) < DH
        vv = v_ref[...]
        zb = jnp.zeros_like(vv)
        vs = [jnp.where(sel_a, vv, zb), jnp.where(sel_a, zb, vv)]
        acck_ref[...] = jnp.zeros_like(acck_ref)
        accv_ref[...] = jnp.zeros_like(accv_ref)

        @pl.when(j == 0)
        def _():
            accq_ref[...] = jnp.zeros_like(accq_ref)

        def q_step(i, heads, masked=False):
            start = pl.multiple_of(i * tq, tq)
            dob = do_ref[pl.ds(start, tq), :]
            lse8 = lse_ref[0, i]
            dr8 = dr_ref[0, i]
            for hh in heads:
                qb = qa_ref[hh, pl.ds(start, tq), :]
                kb = ka_ref[hh]
                st = lax.dot_general(kb, qb, NT, preferred_element_type=F32)
                pt = jnp.exp(st - lse8[hh:hh + 1, :])
                if masked:
                    keep = (lax.broadcasted_iota(jnp.int32, (tq, tq), 0)
                            <= lax.broadcasted_iota(jnp.int32, (tq, tq), 1))
                    pt = jnp.where(keep, pt, 0.0)
                accv_ref[hh] += jnp.dot(pt.astype(BF16), dob, preferred_element_type=F32)
                dpt = lax.dot_general(vs[hh], dob, NT, preferred_element_type=F32)
                dst = (pt * (dpt - dr8[hh:hh + 1, :])).astype(BF16)
                acck_ref[hh] += jnp.dot(dst, qb, preferred_element_type=F32)
                accq_ref[hh, pl.ds(start, tq), :] += lax.dot_general(dst, kb, TN, preferred_element_type=F32)

        def needed(hh, i):
            top = fs_ref[2 * pr + hh, jnp.minimum(i, nq - 1)] + bd_ref[0, 0]
            return jnp.logical_and(i < nq, top - fe_ref[2 * pr + hh, j] >= SKIP)

        q_step(j, (0, 1), masked=True)
        _block_loops(j + 1, 1, needed, q_step)
        dka_ref[...] = acck_ref[...]
        dv_ref[...] = jnp.where(sel_a, accv_ref[0], accv_ref[1]).astype(BF16)

        @pl.when(j == nq - 1)
        def _():
            out = pltpu.make_async_copy(accq_ref, dqa_hbm.at[pl.ds(2 * pr, 2)], out_sem)
            out.start()
            out.wait()

    once = pl.Buffered(1)
    return pl.pallas_call(
        body, name="attn_bwd", grid=(NH // 2, nq),
        out_shape=(_sds((NH, S, LANES), F32), _sds((NH, S, LANES), F32), _sds((S, AW), BF16)),
        in_specs=[SMEM_SPEC, SMEM_SPEC, SMEM_SPEC,
                  pl.BlockSpec((2, tq, LANES), lambda p, j: (p, j, 0)),
                  pl.BlockSpec((tq, LANES), lambda p, j: (j, p)),
                  pl.BlockSpec((2, S, LANES), lambda p, j: (p, 0, 0), pipeline_mode=once),
                  pl.BlockSpec((S, LANES), lambda p, j: (0, p), pipeline_mode=once),
                  pl.BlockSpec((1, nq, 8, tq), lambda p, j: (p, 0, 0, 0)),
                  pl.BlockSpec((1, nq, 8, tq), lambda p, j: (p, 0, 0, 0))],
        out_specs=(pl.BlockSpec(memory_space=pl.ANY),
                   pl.BlockSpec((2, tq, LANES), lambda p, j: (p, j, 0)),
                   pl.BlockSpec((tq, LANES), lambda p, j: (j, p))),
        scratch_shapes=[pltpu.VMEM((2, S, LANES), F32), pltpu.VMEM((2, tq, LANES), F32),
                        pltpu.VMEM((2, tq, LANES), F32), pltpu.SemaphoreType.DMA],
        compiler_params=_cp(("arbitrary", "arbitrary")),
    )(fs, fe, bd, ka, v, qa, do, lser, dr)


def _bwd_conv(du1, u0, alin, agate, w32, tm, ex=None):
    S = du1.shape[0]
    nt = S // tm

    def body(du_ref, dun_ref, u0_ref, u0p_ref, alin_ref, agate_ref, w_ref,
             dalin_ref, dagate_ref, dw_ref, db_ref, extd_ref, extu_ref, du0_ref):
        i = pl.program_id(0)

        @pl.when(i == 0)
        def _():
            dw_ref[...] = jnp.zeros_like(dw_ref)
            db_ref[...] = jnp.zeros_like(db_ref)

        extd_ref[0:tm, :] = du_ref[...]
        extd_ref[tm:, :] = jnp.where(i == nt - 1, 0.0, dun_ref[...])
        extu_ref[0:HALO, :] = jnp.where(i == 0, 0.0, u0p_ref[...])
        extu_ref[HALO:, :] = u0_ref[...]
        db_ref[...] += _colsum8(du_ref[...])
        for r0 in range(0, tm, CHUNK_ROWS):
            duc = du_ref[r0:r0 + CHUNK_ROWS, :]
            acc = jnp.zeros((CHUNK_ROWS, CW), F32)
            for j in range(KC):
                acc = acc + w_ref[j:j + 1, :] * extd_ref[r0 + 30 - j:r0 + 30 - j + CHUNK_ROWS, :]
                dw_ref[j] += _colsum8(duc * extu_ref[r0 + 2 + j:r0 + 2 + j + CHUNK_ROWS, :])
            du0_ref[r0:r0 + CHUNK_ROWS, :] = acc
        du0 = du0_ref[...]
        al = alin_ref[...].astype(F32)
        sg = _sigmoid(agate_ref[...].astype(F32))
        dalin_ref[...] = (du0 * sg).astype(BF16)
        dagate_ref[...] = (du0 * al * sg * (1.0 - sg)).astype(BF16)

    nxt = pl.BlockSpec((HALO, CW), lambda i: (jnp.minimum((i + 1) * (tm // HALO), S // HALO - 1), 0))
    outs = pl.pallas_call(
        _hosted(body, ex, 7, 4, 3, nt), name="bwd_conv", grid=(nt,),
        out_shape=(_sds((S, CW), BF16), _sds((S, CW), BF16), _sds((HALO, 8, CW), F32), _sds((8, CW), F32),
                   *(ex.out_shapes if ex else [])),
        in_specs=[_rows(tm, CW), nxt, _rows(tm, CW), _halo_prev(tm), _rows(tm, CW), _rows(tm, CW), _const((HALO, CW)),
                  *(ex.in_specs if ex else [])],
        out_specs=(_rows(tm, CW), _rows(tm, CW), _const((HALO, 8, CW)), _const((8, CW)), *(ex.out_specs if ex else [])),
        scratch_shapes=[pltpu.VMEM((tm + HALO, CW), F32), pltpu.VMEM((tm + HALO, CW), F32), pltpu.VMEM((tm, CW), F32),
                        *(ex.scratch if ex else [])],
        compiler_params=_cp(("arbitrary",)),
    )(du1, du1, u0, u0, alin, agate, w32, *(ex.srcs if ex else []))
    return outs[:4], list(outs[4:])


def _bwd_qk(dqa, dka, qn, kn, rq, rk, fgb, qg512, kg512, e512, et512, tm):
    S = qn.shape[0]
    nt = S // tm

    def body(dqa_ref, dka_ref, qn_ref, kn_ref, rq_ref, rk_ref, fgb_ref, qg_ref, kg_ref, e_ref, et_ref,
             dq_ref, dk_ref, dfg_ref, accg_ref, accb_ref, carry_ref):
        @pl.when(pl.program_id(0) == 0)
        def _():
            carry_ref[...] = jnp.zeros_like(carry_ref)
            accg_ref[...] = jnp.zeros_like(accg_ref)
            accb_ref[...] = jnp.zeros_like(accb_ref)

        lane = _lane((tm, LANES))
        sel_a = lane < DH
        df = jnp.zeros((tm, LANES), F32)
        for h in range(NH):
            col = dqa_ref[h][:, 64:65] - dka_ref[h][:, 67:68]
            df = jnp.where(lane == h, col, df)
        tri = (lax.broadcasted_iota(jnp.int32, (tm, tm), 0) <= lax.broadcasted_iota(jnp.int32, (tm, tm), 1)
               ).astype(F32).astype(BF16)
        dlf = _dot3(tri, df) + carry_ref[0:1, :]
        carry_ref[...] = jnp.broadcast_to(dlf[0:1, :], carry_ref.shape)
        dfg = jnp.where(lane < NH, dlf * _sigmoid(-fgb_ref[...]), 0.0)
        dfg_ref[...] = dfg.astype(BF16)
        accb_ref[...] += _colsum8(dfg)

        def norm_bwd(src_ref, n_ref, r_ref, g_ref, scale, slot):
            pairs = []
            for p in range(NH // 2):
                b = pltpu.roll(src_ref[2 * p + 1], 64, 1)
                pairs.append(jnp.where(sel_a, src_ref[2 * p], b))
            dh = jnp.concatenate(pairs, axis=1) * scale
            tn = n_ref[...].astype(F32)
            accg_ref[slot] += _colsum8(dh * tn)
            dn = dh * g_ref[...]
            mean = _dot2(dn * tn, e_ref[...]) * (1.0 / DH)
            corr = _dot2(mean, et_ref[...])
            rf = _dot2(r_ref[...], et_ref[...])
            return (rf * (dn - tn * corr)).astype(BF16)

        dq_ref[...] = norm_bwd(dqa_ref, qn_ref, rq_ref, qg_ref, DH ** -0.5, 0)
        dk_ref[...] = norm_bwd(dka_ref, kn_ref, rk_ref, kg_ref, 1.0, 1)

    rev = lambda n: pl.BlockSpec((tm, n), lambda i: (nt - 1 - i, 0))
    hm = pl.BlockSpec((NH, tm, LANES), lambda i: (0, nt - 1 - i, 0))
    dq, dk, dfg, accg, accb = pl.pallas_call(
        body, name="bwd_qk", grid=(nt,),
        out_shape=(_sds((S, AW), BF16), _sds((S, AW), BF16), _sds((S, LANES), BF16), _sds((2, 8, AW), F32),
                   _sds((8, LANES), F32)),
        in_specs=[hm, hm, rev(AW), rev(AW), rev(LANES), rev(LANES), rev(LANES), _const((1, AW)), _const((1, AW)),
                  _const((AW, LANES)), _const((LANES, AW))],
        out_specs=(rev(AW), rev(AW), rev(LANES), _const((2, 8, AW)), _const((8, LANES))),
        scratch_shapes=[pltpu.VMEM((8, LANES), F32)], compiler_params=_cp(("arbitrary",)),
    )(dqa, dka, qn, kn, rq, rk, fgb, qg512, kg512, e512, et512)
    return dq, dk, dfg, accg, accb


def _bwd_in(dq, dk, dv, dalin, dagate, dfg, w_in_p, x, dx2, mod8, n1g, tm):
    S = x.shape[0]

    def body(dq_ref, dk_ref, dv_ref, dal_ref, dag_ref, dfg_ref, w_ref, x_ref, dx2_ref, mod_ref, n1g_ref,
             dx_ref, acc_ref):
        @pl.when(pl.program_id(0) == 0)
        def _():
            acc_ref[...] = jnp.zeros_like(acc_ref)

        def part(ref, a, b):
            return lax.dot_general(ref[...], w_ref[:, a:b], NT, preferred_element_type=F32)

        dh = (part(dq_ref, 0, 512) + part(dk_ref, 512, 1024) + part(dv_ref, 1024, 1536) + part(dal_ref, 1536, 2048)
              + part(dag_ref, 2048, 2560) + part(dfg_ref, 2560, NP))
        xv = x_ref[...]
        r1 = lax.rsqrt(jnp.mean(xv * xv, axis=-1, keepdims=True) + EPS)
        xn = xv * r1
        gain = n1g_ref[...] * (1.0 + mod_ref[1:2, :])
        t = dh * xn
        acc_ref[0] += _colsum8(dh)
        acc_ref[1] += _colsum8(t * n1g_ref[...])
        acc_ref[2] += _colsum8(t * (1.0 + mod_ref[1:2, :]))
        dxn = dh * gain
        dx_ref[...] = dx2_ref[...] + r1 * (dxn - xn * jnp.mean(dxn * xn, axis=-1, keepdims=True))

    return pl.pallas_call(
        body, name="bwd_in", grid=(S // tm,),
        out_shape=(_sds((S, D), F32), _sds((3, 8, D), F32)),
        in_specs=[_rows(tm, AW), _rows(tm, AW), _rows(tm, AW), _rows(tm, CW), _rows(tm, CW), _rows(tm, LANES),
                  _const((D, NP)), _rows(tm, D), _rows(tm, D), _const((8, D)), _const((1, D))],
        out_specs=(_rows(tm, D), _const((3, 8, D))),
        compiler_params=_cp(("arbitrary",)),
    )(dq, dk, dv, dalin, dagate, dfg, w_in_p, x, dx2, mod8, n1g)


def _adam(w, g, m, v):
    m_new = B1 * m + (1.0 - B1) * g
    v_new = B2 * v + (1.0 - B2) * (g * g)
    m_hat = m_new / (1.0 - B1 ** STEP)
    v_hat = v_new / (1.0 - B2 ** STEP)
    delta = -LR * (m_hat / (jnp.sqrt(v_hat) + AEPS) + WD * w)
    return delta, m_new, v_new


def _reduce_adamw(slots, w, m, v, name, tr=256):
    ns, R, C = slots.shape
    tr = tr if R % tr == 0 else R

    def body(s_ref, w_ref, m_ref, v_ref, g_ref, d_ref, mo_ref, vo_ref):
        g = s_ref[0].astype(F32)
        for k in range(1, ns):
            g = g + s_ref[k].astype(F32)
        g_ref[...] = g
        d_ref[...], mo_ref[...], vo_ref[...] = _adam(w_ref[...], g, m_ref[...], v_ref[...])

    blk = pl.BlockSpec((tr, C), lambda i: (i, 0))
    return pl.pallas_call(
        body, name=name, grid=(R // tr,), out_shape=tuple(_sds((R, C), F32) for _ in range(4)),
        in_specs=[pl.BlockSpec((ns, tr, C), lambda i: (0, i, 0)), blk, blk, blk], out_specs=(blk, blk, blk, blk),
        compiler_params=_cp(("parallel",)),
    )(slots, w, m, v)


def _pair_adamw(slots, w, m, v, name, tr=256):
    ns, R, C = slots.shape
    tr = tr if R % tr == 0 else R
    nt = R // tr

    def body(s_ref, w_ref, m_ref, v_ref, g_ref, d_ref, mo_ref, vo_ref, mine_ref, theirs_ref, send_sems, recv_sems):
        i = pl.program_id(0)
        part = s_ref[0].astype(F32)
        for k in range(1, ns):
            part = part + s_ref[k].astype(F32)
        mine_ref[i] = part
        swap = pltpu.make_async_remote_copy(
            src_ref=mine_ref.at[i], dst_ref=theirs_ref.at[i], send_sem=send_sems.at[i], recv_sem=recv_sems.at[i],
            device_id=(lax.axis_index("x"), lax.axis_index("y"), 1 - lax.axis_index("c")),
            device_id_type=pl.DeviceIdType.MESH)
        swap.start()
        swap.wait()
        g = part + theirs_ref[i]
        g_ref[...] = g
        d_ref[...], mo_ref[...], vo_ref[...] = _adam(w_ref[...], g, m_ref[...], v_ref[...])

    blk = pl.BlockSpec((tr, C), lambda i: (i, 0))
    return pl.pallas_call(
        body, name=name, grid=(nt,), out_shape=tuple(_sds((R, C), F32) for _ in range(4)),
        in_specs=[pl.BlockSpec((ns, tr, C), lambda i: (0, i, 0)), blk, blk, blk], out_specs=(blk, blk, blk, blk),
        scratch_shapes=[pltpu.VMEM((nt, tr, C), F32), pltpu.VMEM((nt, tr, C), F32),
                        pltpu.SemaphoreType.DMA((nt,)), pltpu.SemaphoreType.DMA((nt,))],
        compiler_params=_cp(("arbitrary",)),
    )(slots, w, m, v)


def _ada_adamw(sct, dmod, w, m, v):
    R, C = w.shape
    tr, bc = 256, 512

    def body(sct_ref, dm_ref, w_ref, m_ref, v_ref, g_ref, d_ref, mo_ref, vo_ref):
        g = sct_ref[:, 0:1] * dm_ref[0:1, :]
        for b in range(1, N_DEV):
            g = g + sct_ref[:, b:b + 1] * dm_ref[b:b + 1, :]
        g_ref[...] = g
        d_ref[...], mo_ref[...], vo_ref[...] = _adam(w_ref[...], g, m_ref[...], v_ref[...])

    blk = pl.BlockSpec((tr, bc), lambda i, j: (i, j))
    return pl.pallas_call(
        body, name="ada_adamw", grid=(R // tr, C // bc), out_shape=tuple(_sds((R, C), F32) for _ in range(4)),
        in_specs=[pl.BlockSpec((tr, N_DEV), lambda i, j: (i, 0)), pl.BlockSpec((N_DEV, bc), lambda i, j: (0, j)),
                  blk, blk, blk],
        out_specs=(blk, blk, blk, blk), compiler_params=_cp(("parallel", "parallel")),
    )(sct, dmod, w, m, v)


def _small_reduce(slots, fold):
    def body(s_ref, f_ref, o_ref):
        tot = s_ref[0:1, :]
        for k in range(1, N_DEV):
            tot = tot + s_ref[k:k + 1, :]
        o_ref[:, 0:6144] = tot[:, 0:6144]
        o_ref[:, 6144:7168] = tot[:, 6144:7168]
        for t, src in enumerate((8192, 8704)):
            v8 = jnp.broadcast_to(tot[:, src:src + AW], (8, AW))
            o_ref[:, 7168 + t * LANES:7168 + (t + 1) * LANES] = jnp.dot(
                v8, f_ref[...], precision=HI, preferred_element_type=F32)[0:1, :]
        o_ref[:, 7424:7552] = tot[:, 9216:9344]
        o_ref[:, 7552:10112] = tot[:, 9344:11904]
        o_ref[:, 10112:SMALL_OUT] = tot[:, 7168:8192]

    return pl.pallas_call(
        body, name="small_reduce", out_shape=_sds((1, SMALL_OUT), F32),
        in_specs=[pl.BlockSpec(memory_space=pltpu.VMEM), pl.BlockSpec(memory_space=pltpu.VMEM)],
        out_specs=pl.BlockSpec(memory_space=pltpu.VMEM),
    )(slots, fold)


def _perm_in(w):
    pad = jnp.zeros((w.shape[0], NP - 2568), w.dtype)
    return jnp.concatenate([w[:, :1536], w[:, 1544:2568], w[:, 1536:1544], pad], axis=1)


def _pad_lanes(vec, n=LANES):
    return jnp.pad(vec, ((0, 0), (0, n - vec.shape[1])))


def kernel(x, c, w_ada, b_ada, norm1_g, w_in, q_norm_g, k_norm_g, b_f, conv_w, conv_b, conv_ln_g, conv_ln_b, beta_attn, beta_conv, w_out, norm2_g, w_ff1, w_ff2, loss_target, m_w_ada, m_b_ada, m_norm1_g, m_w_in, m_q_norm_g, m_k_norm_g, m_b_f, m_conv_w, m_conv_b, m_conv_ln_g, m_conv_ln_b, m_beta_attn, m_beta_conv, m_w_out, m_norm2_g, m_w_ff1, m_w_ff2, v_w_ada, v_b_ada, v_norm1_g, v_w_in, v_q_norm_g, v_k_norm_g, v_b_f, v_conv_w, v_conv_b, v_conv_ln_g, v_conv_ln_b, v_beta_attn, v_beta_conv, v_w_out, v_norm2_g, v_w_ff1, v_w_ff2):
    S = x.shape[1]
    tm = min(256, S)
    tq = min(512, S // 2)
    xs, tgt = x[0], loss_target[0]
    chip = 2 * lax.axis_index("x") + lax.axis_index("y")
    e512, et512 = _head_sum_mats()

    conv_w32 = jnp.pad(conv_w[0], ((0, 1), (0, 0)))
    c_all, g_in = _exchange([(c, "bcast8"), (w_in[0].astype(BF16), "chip4")], "gather_in")
    later_weights = _Exchange([(w_out[0].astype(BF16), "chip4"), (w_ff1[0].astype(BF16), "chip4"),
                               (w_ff2[0].astype(BF16), "chip4"), (conv_w32, "chip4")])
    c_all = c_all.reshape(N_DEV, D)
    w_in_p = _perm_in(jnp.transpose(g_in, (1, 0, 2)).reshape(D, 2568))

    b_shard = lax.dynamic_slice(b_ada, (0, chip * 1536), (1, 1536))
    mod_rows, sc_all = _mod_shard(c_all, w_ada[0], b_shard)
    (mod_slots,) = _exchange([(mod_rows.reshape(N_DEV, 1, 1536), "all8")], "scatter_mod")
    mod = mod_slots.reshape(4, 2, 1536)[:, 0, :].reshape(6, D)
    mod8 = jnp.pad(mod, ((0, 2), (0, 0)))

    qg512 = jnp.tile(q_norm_g, (1, NH))
    kg512 = jnp.tile(k_norm_g, (1, NH))
    bf128 = _pad_lanes(b_f)

    (h1, qh, kh, vb, qn, kn, rq, rk, fgb, alin, agate, u0), (g_out, g_ff1, g_ff2, g_cw) = _fwd_in(
        xs, mod8, norm1_g, w_in_p, e512, et512, qg512, kg512, bf128, tm, ex=later_weights)
    w_out_f = g_out.reshape(D, D)
    w1, w2 = g_ff1, g_ff2
    cw32 = jnp.transpose(g_cw, (1, 0, 2)).reshape(HALO, CW)
    bd = _logit_bound(q_norm_g, k_norm_g)
    qa, ka, va, fcum = _fwd_decay(fgb, qh, kh, vb, _shift(bd), tm)
    fs, fe = _skip_tables(fcum, tq)
    o_attn, lser = _attn_fwd(qa, ka, va, fs, fe, bd, tq)
    u1, mc = _fwd_conv(u0, cw32, conv_b, conv_ln_g, conv_ln_b, beta_conv, tm)
    merged, ob, x2, h2 = _fwd_out(o_attn, mc, xs, mod8, norm2_g, beta_attn, w_out_f, tm)
    tf = min(512, S)
    r, dy, loss8, dg2 = _fwd_ffn(h2, w1, w2, x2, tgt, mod8, tf)

    df2, df1, dh2 = _bwd_ffn(dy, mod8, r, w1, w2, tf)
    gw_ff2 = _wgrad(r, df2, "wgrad_ff2", square_a=True)
    gw_ff1 = _wgrad(h2, df1, "wgrad_ff1")
    ff_grads = _Exchange([(jnp.transpose(gw_ff1.reshape(D, 4, D), (1, 0, 2)).astype(BF16), "chip4p"),
                          (gw_ff2.reshape(4, D, D).astype(BF16), "chip4p")])
    (dx2, do, doa, du1, acc_d, acc_h, dr), (p_ff1, p_ff2) = _bwd_mid(
        dh2, dy, x2, ob, o_attn, u1, mod8, norm2_g, beta_attn, beta_conv, conv_ln_g, conv_ln_b, w_out_f, e512, tm, tq,
        ex=ff_grads)
    gw_out = _wgrad(merged, do, "wgrad_out")
    dqa, dka, dv = _attn_bwd(qa, ka, vb, doa, lser, dr, fs, fe, bd, tq)
    out_grads = _Exchange([(gw_out.reshape(4, 256, D).astype(BF16), "chip4p")])
    (dalin, dagate, dcw, dcb), (p_out,) = _bwd_conv(du1, u0, alin, agate, cw32, tm, ex=out_grads)
    dq, dk, dfg, accg, accb = _bwd_qk(dqa, dka, qn, kn, rq, rk, fgb, qg512, kg512, e512, et512, tm)
    grad_x, acc1 = _bwd_in(dq, dk, dv, dalin, dagate, dfg, w_in_p, xs, dx2, mod8, norm1_g, tm)
    gw_in_p = _wgrad_in(h1, [dq, dk, dv, dalin, dagate, dfg])
    gw_in = jnp.concatenate([gw_in_p[:, :1536], gw_in_p[:, 2560:2568], gw_in_p[:, 1536:2560]], axis=1)

    s8 = lambda a: jnp.sum(a, axis=-2)
    a1, ad, ah = s8(acc1), s8(acc_d), s8(acc_h)
    small = jnp.concatenate(
        [a1[0], a1[1], ad[3], ad[0], ad[1], s8(dg2),
         a1[2], ad[2], s8(accg).reshape(-1), s8(accb), s8(dcb), ah[2], ah[3], ah[0], ah[1]]).reshape(1, SMALL_IN)
    gcw = s8(dcw)
    small_s, p_in, p_cw = _exchange(
        [(small, "bcast8"),
         (jnp.transpose(gw_in.reshape(D, 4, 642), (1, 0, 2)).astype(BF16), "chip4p"),
         (jnp.transpose(gcw.reshape(HALO, 4, LANES), (1, 0, 2)), "chip4p")], "scatter_grads")
    small_s = small_s.reshape(N_DEV, SMALL_IN)

    g_in_, d_in, nm_in, nv_in = _pair_adamw(p_in, w_in[0], m_w_in[0], v_w_in[0], "adamw_in")
    g_out_, d_out, nm_out, nv_out = _pair_adamw(p_out, w_out[0], m_w_out[0], v_w_out[0], "adamw_out")
    g_f1, d_f1, nm_f1, nv_f1 = _pair_adamw(p_ff1, w_ff1[0], m_w_ff1[0], v_w_ff1[0], "adamw_ff1")
    g_f2, d_f2, nm_f2, nv_f2 = _pair_adamw(p_ff2, w_ff2[0], m_w_ff2[0], v_w_ff2[0], "adamw_ff2")
    pad_row = lambda a, fill: jnp.pad(a[0], ((0, 1), (0, 0)), constant_values=fill)
    g_cw_, d_cw, nm_cw, nv_cw = (a[:KC] for a in _pair_adamw(
        p_cw, pad_row(conv_w, 0.0), pad_row(m_conv_w, 0.0), pad_row(v_conv_w, 1.0), "adamw_conv_w"))
    dmod_shard = lax.dynamic_slice(small_s[:, :6 * D], (0, chip * 1536), (N_DEV, 1536))
    g_ada, d_ada, nm_ada, nv_ada = _ada_adamw(sc_all.T, dmod_shard, w_ada[0], m_w_ada[0], v_w_ada[0])

    fold = np.zeros((AW, LANES), np.float32)
    fold[np.arange(AW), np.arange(AW) % DH] = 1.0
    g_small = _small_reduce(small_s, jnp.asarray(fold))
    smalls = [b_ada, norm1_g, q_norm_g, k_norm_g, b_f, conv_b, conv_ln_g, conv_ln_b, beta_attn, beta_conv, norm2_g]
    m_smalls = [m_b_ada, m_norm1_g, m_q_norm_g, m_k_norm_g, m_b_f, m_conv_b, m_conv_ln_g, m_conv_ln_b, m_beta_attn,
                m_beta_conv, m_norm2_g]
    v_smalls = [v_b_ada, v_norm1_g, v_q_norm_g, v_k_norm_g, v_b_f, v_conv_b, v_conv_ln_g, v_conv_ln_b, v_beta_attn,
                v_beta_conv, v_norm2_g]
    widths = [a.shape[1] for a in smalls]
    padded = [-(-n // LANES) * LANES for n in widths]
    pack = lambda arrs, fill: jnp.concatenate(
        [jnp.pad(a, ((0, 0), (0, p - a.shape[1])), constant_values=fill) for a, p in zip(arrs, padded)], axis=1)
    outs_small = _reduce_adamw(g_small.reshape(1, 1, SMALL_OUT), pack(smalls, 0.0), pack(m_smalls, 0.0),
                               pack(v_smalls, 1.0), "adamw_small")
    offs = np.concatenate([[0], np.cumsum(padded)])

    def unpack(a):
        return [a[:, int(o):int(o) + n] for o, n in zip(offs[:-1], widths)]

    gs, ds, ms, vs = (unpack(a) for a in outs_small)

    loss = lax.psum(loss8[0, 0], ("x", "y", "c"))
    big = {"w_ada": (g_ada, d_ada, nm_ada, nv_ada), "w_in": (g_in_, d_in, nm_in, nv_in),
           "conv_w": (g_cw_, d_cw, nm_cw, nv_cw), "w_out": (g_out_, d_out, nm_out, nv_out),
           "w_ff1": (g_f1, d_f1, nm_f1, nv_f1), "w_ff2": (g_f2, d_f2, nm_f2, nv_f2)}
    small_names = ["b_ada", "norm1_g", "q_norm_g", "k_norm_g", "b_f", "conv_b", "conv_ln_g", "conv_ln_b", "beta_attn",
                   "beta_conv", "norm2_g"]
    order = ["w_ada", "b_ada", "norm1_g", "w_in", "q_norm_g", "k_norm_g", "b_f", "conv_w", "conv_b", "conv_ln_g",
             "conv_ln_b", "beta_attn", "beta_conv", "w_out", "norm2_g", "w_ff1", "w_ff2"]

    def leaf(name, which):
        if name in big:
            return big[name][which][None]
        return (gs, ds, ms, vs)[which][small_names.index(name)]

    return (loss, grad_x[None], *[leaf(n, 0) for n in order], *[leaf(n, 1) for n in order],
            *[leaf(n, 2) for n in order], *[leaf(n, 3) for n in order])
```

```python
import functools

import numpy as np
import jax
import jax.numpy as jnp
from jax import lax
from jax.experimental import pallas as pl
from jax.experimental.pallas import tpu as pltpu

F32, BF16 = jnp.float32, jnp.bfloat16
HI = lax.Precision.HIGHEST
D = 1024
AW = 512
CW = 512
NH = 8
DH = 64
KC = 31
DFF = 4096
NP = 2688
EPS = 1e-6
NEG = -1e30
LANES = 128
VMEM_LIMIT = 56 * 2**20
NT = (((1,), (1,)), ((), ()))
TN = (((0,), (0,)), ((), ()))
LR, B1, B2, AEPS, WD, STEP = 0.001, 0.9, 0.999, 1e-08, 0.01, 10
N_DEV = 8
SMALL_IN = 11904
SMALL_OUT = 11136


def _cp(sem=None, vmem=VMEM_LIMIT):
    kw = dict(vmem_limit_bytes=vmem)
    if sem is not None:
        kw["dimension_semantics"] = sem
    return pltpu.CompilerParams(**kw)


def _rows(tm, n):
    return pl.BlockSpec((tm, n), lambda i: (i, 0))


def _const(shape):
    nd = len(shape)
    return pl.BlockSpec(shape, lambda *_: (0,) * nd)


def _sds(shape, dt):
    return jax.ShapeDtypeStruct(shape, dt)


def _lane(shape):
    return lax.broadcasted_iota(jnp.int32, shape, len(shape) - 1)


def _sigmoid(x):
    return 1.0 / (1.0 + jnp.exp(-x))


class _Exchange:
    MASKS = {"chip4": (2, 4, 6), "chip4p": (2, 4, 6), "all8": (1, 2, 3, 4, 5, 6, 7), "bcast8": (1, 2, 3, 4, 5, 6, 7)}

    def __init__(self, items):
        self.srcs = [s for s, _ in items]
        self.kinds = [k for _, k in items]
        self.n = len(items)
        self.out_shapes = []
        for s, k in items:
            shape = {"all8": (N_DEV,) + s.shape[1:], "bcast8": (N_DEV,) + s.shape, "chip4": (4,) + s.shape,
                     "chip4p": (4,) + s.shape[1:]}[k]
            self.out_shapes.append(_sds(shape, s.dtype))
        self.sem_index = {}
        for t, k in enumerate(self.kinds):
            for m in self.MASKS[k]:
                self.sem_index[(t, m)] = len(self.sem_index)
        n_sem = len(self.sem_index)
        self.scratch = [pltpu.SemaphoreType.DMA((n_sem,)), pltpu.SemaphoreType.DMA((n_sem,)),
                        pltpu.SemaphoreType.DMA((self.n,))]
        self.in_specs = [pl.BlockSpec(memory_space=pl.ANY)] * self.n
        self.out_specs = [pl.BlockSpec(memory_space=pl.ANY)] * self.n

    def copies(self, src_refs, dst_refs, send_sems, recv_sems, local_sems):
        x, y, c = lax.axis_index("x"), lax.axis_index("y"), lax.axis_index("c")
        my_id = 4 * x + 2 * y + c
        my_chip = 2 * x + y

        def piece(t, dev_id, chip):
            k = self.kinds[t]
            return src_refs[t].at[dev_id] if k == "all8" else src_refs[t].at[chip] if k == "chip4p" else src_refs[t]

        out = []
        for t in range(self.n):
            slot = dst_refs[t].at[my_chip if self.kinds[t] in ("chip4", "chip4p") else my_id]
            out.append(pltpu.make_async_copy(piece(t, my_id, my_chip), slot, local_sems.at[t]))
            for m in self.MASKS[self.kinds[t]]:
                px = 1 - x if m & 4 else x
                py = 1 - y if m & 2 else y
                pc = 1 - c if m & 1 else c
                s = self.sem_index[(t, m)]
                out.append(pltpu.make_async_remote_copy(
                    src_ref=piece(t, 4 * px + 2 * py + pc, 2 * px + py), dst_ref=slot,
                    send_sem=send_sems.at[s], recv_sem=recv_sems.at[s],
                    device_id=(px, py, pc), device_id_type=pl.DeviceIdType.MESH))
        return out


def _hosted(body, ex, n_in, n_out, n_scr, n_steps):
    if ex is None:
        return body

    def wrapped(*refs):
        ins, xin = refs[:n_in], refs[n_in:n_in + ex.n]
        o0 = n_in + ex.n
        outs, xout = refs[o0:o0 + n_out], refs[o0 + n_out:o0 + n_out + ex.n]
        s0 = o0 + n_out + ex.n
        scr, sems = refs[s0:s0 + n_scr], refs[s0 + n_scr:]

        @pl.when(pl.program_id(0) == 0)
        def _():
            for cp in ex.copies(xin, xout, *sems):
                cp.start()

        body(*ins, *outs, *scr)

        @pl.when(pl.program_id(0) == n_steps - 1)
        def _():
            for cp in ex.copies(xin, xout, *sems):
                cp.wait()

    return wrapped


def _exchange(items, name):
    ex = _Exchange(items)
    n = ex.n

    def body(*refs):
        copies = ex.copies(refs[:n], refs[n:2 * n], *refs[2 * n:])
        for cp in copies:
            cp.start()
        for cp in copies:
            cp.wait()

    outs = pl.pallas_call(
        body, name=name, out_shape=tuple(ex.out_shapes), in_specs=ex.in_specs, out_specs=tuple(ex.out_specs),
        scratch_shapes=ex.scratch,
    )(*ex.srcs)
    return list(outs)


def _mod_shard(c_all, w_ada, b_shard):
    n = w_ada.shape[1]

    def body(c_ref, w_ref, b_ref, o_ref, sc_ref):
        cv = c_ref[...]
        sc = cv * _sigmoid(cv)
        sc_ref[...] = sc
        o_ref[...] = jnp.dot(sc, w_ref[...], precision=HI, preferred_element_type=F32) + b_ref[...]

    bn = 512
    return pl.pallas_call(
        body, name="mod_shard", out_shape=(_sds((N_DEV, n), F32), _sds((N_DEV, D), F32)), grid=(n // bn,),
        in_specs=[_const((N_DEV, D)), pl.BlockSpec((D, bn), lambda j: (0, j)), pl.BlockSpec((1, bn), lambda j: (0, j))],
        out_specs=(pl.BlockSpec((N_DEV, bn), lambda j: (0, j)), _const((N_DEV, D))),
        compiler_params=_cp(("arbitrary",)),
    )(c_all, w_ada, b_shard)


def _head_sum_mats():
    e = np.zeros((AW, LANES), np.float32)
    for h in range(NH):
        e[h * DH:(h + 1) * DH, h] = 1.0
    return jnp.asarray(e, BF16), jnp.asarray(e.T.copy(), BF16)


def _dot2(x, w):
    hi = x.astype(BF16)
    lo = (x - hi.astype(F32)).astype(BF16)
    return jnp.dot(hi, w, preferred_element_type=F32) + jnp.dot(lo, w, preferred_element_type=F32)


def _fwd_in(x, mod8, n1g, w_in_p, e512, et512, qg512, kg512, bf128, tm, ex=None):
    S = x.shape[0]

    def body(x_ref, mod_ref, n1g_ref, w_ref, e_ref, et_ref, qg_ref, kg_ref, bf_ref,
             h1_ref, qh_ref, kh_ref, v_ref, qn_ref, kn_ref, rq_ref, rk_ref, fgb_ref, alin_ref, agate_ref, u0_ref):
        xv = x_ref[...]
        r1 = lax.rsqrt(jnp.mean(xv * xv, axis=-1, keepdims=True) + EPS)
        h = (xv * r1) * (n1g_ref[...] * (1.0 + mod_ref[1:2, :])) + mod_ref[0:1, :]
        hb = h.astype(BF16)
        h1_ref[...] = hb

        def seg(a, b):
            return jnp.dot(hb, w_ref[:, a:b], preferred_element_type=F32)

        def headnorm(t, g_ref, scale, n_ref, r_ref, o_ref):
            ss = _dot2(t * t, e_ref[...])
            r = lax.rsqrt(ss * (1.0 / DH) + EPS)
            tn = t * _dot2(r, et_ref[...])
            n_ref[...] = tn.astype(BF16)
            r_ref[...] = r
            o_ref[...] = (tn * (g_ref[...] * scale)).astype(BF16)

        headnorm(seg(0, 512), qg_ref, DH ** -0.5, qn_ref, rq_ref, qh_ref)
        headnorm(seg(512, 1024), kg_ref, 1.0, kn_ref, rk_ref, kh_ref)
        v_ref[...] = seg(1024, 1536).astype(BF16)
        alin = seg(1536, 2048)
        agate = seg(2048, 2560)
        alin_ref[...] = alin.astype(BF16)
        agate_ref[...] = agate.astype(BF16)
        u0_ref[...] = alin * _sigmoid(agate)
        fgb_ref[...] = seg(2560, NP) + bf_ref[...]

    bf = lambda: _sds((S, AW), BF16)
    xs = ex.srcs if ex else []
    outs = pl.pallas_call(
        _hosted(body, ex, 9, 12, 0, S // tm), name="fwd_in", grid=(S // tm,),
        out_shape=(_sds((S, D), BF16), bf(), bf(), bf(), bf(), bf(), _sds((S, LANES), F32), _sds((S, LANES), F32),
                   _sds((S, LANES), F32), bf(), bf(), _sds((S, CW), F32), *(ex.out_shapes if ex else [])),
        in_specs=[_rows(tm, D), _const((8, D)), _const((1, D)), _const((D, NP)), _const((AW, LANES)), _const((LANES, AW)),
                  _const((1, AW)), _const((1, AW)), _const((1, LANES)), *(ex.in_specs if ex else [])],
        out_specs=(_rows(tm, D), _rows(tm, AW), _rows(tm, AW), _rows(tm, AW), _rows(tm, AW), _rows(tm, AW),
                   _rows(tm, LANES), _rows(tm, LANES), _rows(tm, LANES), _rows(tm, AW), _rows(tm, AW), _rows(tm, CW),
                   *(ex.out_specs if ex else [])),
        scratch_shapes=ex.scratch if ex else [],
        compiler_params=_cp(("arbitrary",)),
    )(x, mod8, n1g, w_in_p, e512, et512, qg512, kg512, bf128, *xs)
    return outs[:12], list(outs[12:])


def _split3(f):
    f1 = f.astype(BF16).astype(F32)
    f2 = (f - f1).astype(BF16).astype(F32)
    return f1, f2, f - f1 - f2


def _dot3(w, x):
    return sum(jnp.dot(w, piece.astype(BF16), preferred_element_type=F32) for piece in _split3(x))


def _fwd_decay(fgb, qh, kh, vb, shift, tm):
    S = fgb.shape[0]

    def body(shift_ref, fgb_ref, qh_ref, kh_ref, vb_ref, qa_ref, ka_ref, va_ref, f_ref, carry_ref):
        @pl.when(pl.program_id(0) == 0)
        def _():
            carry_ref[...] = jnp.zeros_like(carry_ref)

        fb = fgb_ref[...]
        lf = jnp.minimum(fb, 0.0) - jnp.log1p(jnp.exp(-jnp.abs(fb)))
        tri = (lax.broadcasted_iota(jnp.int32, (tm, tm), 0) >= lax.broadcasted_iota(jnp.int32, (tm, tm), 1)
               ).astype(F32).astype(BF16)
        cs = _dot3(tri, lf) + carry_ref[0:1, :]
        f_ref[...] = cs
        carry_ref[...] = jnp.broadcast_to(cs[tm - 1:tm, :], carry_ref.shape)
        lane = _lane((tm, LANES))
        s1, s2, s3 = _split3(jnp.zeros((tm, LANES), F32) - shift_ref[0, 0])
        tail_q = jnp.where((lane >= 67) & (lane < 70), 1.0,
                           jnp.where(lane == 70, s1, jnp.where(lane == 71, s2, jnp.where(lane == 72, s3, 0.0))))
        tail_k = jnp.where(((lane >= 64) & (lane < 67)) | ((lane >= 70) & (lane < 73)), 1.0, 0.0)
        tail_v = jnp.where(lane == DH, 1.0, 0.0)
        for p in range(NH // 2):
            qp = qh_ref[:, p * LANES:(p + 1) * LANES].astype(F32)
            kp = kh_ref[:, p * LANES:(p + 1) * LANES].astype(F32)
            vp = vb_ref[:, p * LANES:(p + 1) * LANES].astype(F32)
            for hh in range(2):
                h = 2 * p + hh
                f1, f2, f3 = _split3(cs[:, h:h + 1])
                qb = qp if hh == 0 else pltpu.roll(qp, 64, 1)
                kb = kp if hh == 0 else pltpu.roll(kp, 64, 1)
                vh = vp if hh == 0 else pltpu.roll(vp, 64, 1)
                augq = jnp.where(lane == 64, f1, jnp.where(lane == 65, f2, jnp.where(lane == 66, f3, tail_q)))
                augk = jnp.where(lane == 67, -f1, jnp.where(lane == 68, -f2, jnp.where(lane == 69, -f3, tail_k)))
                qa_ref[h] = jnp.where(lane < DH, qb, augq).astype(BF16)
                ka_ref[h] = jnp.where(lane < DH, kb, augk).astype(BF16)
                va_ref[h] = jnp.where(lane < DH, vh, tail_v).astype(BF16)

    hm = pl.BlockSpec((NH, tm, LANES), lambda i: (0, i, 0))
    hms = _sds((NH, S, LANES), BF16)
    return pl.pallas_call(
        body, name="fwd_decay", grid=(S // tm,),
        out_shape=(hms, hms, hms, _sds((S, LANES), F32)),
        in_specs=[SMEM_SPEC, _rows(tm, LANES), _rows(tm, AW), _rows(tm, AW), _rows(tm, AW)],
        out_specs=(hm, hm, hm, _rows(tm, LANES)),
        scratch_shapes=[pltpu.VMEM((8, LANES), F32)], compiler_params=_cp(("arbitrary",)),
    )(shift, fgb, qh, kh, vb)


SKIP = -106.0


def _block_loops(first, step, needed, run):
    def both(j):
        return jnp.logical_and(needed(0, j), needed(1, j))

    def walk(heads):
        def go(j):
            run(j, heads)
            return j + step
        return go

    j = lax.while_loop(both, walk((0, 1)), first)
    lax.while_loop(functools.partial(needed, 0), walk((0,)), j)
    lax.while_loop(functools.partial(needed, 1), walk((1,)), j)


def _logit_bound(qg, kg):
    return (2.0 * 1.03 * DH ** 0.5 * jnp.max(jnp.abs(qg)) * jnp.max(jnp.abs(kg))).reshape(1, 1)


def _skip_tables(f, tq):
    return f[0::tq, :NH].T, f[tq - 1::tq, :NH].T


SMEM_SPEC = pl.BlockSpec(memory_space=pltpu.SMEM)


def _causal_rect(rows, cols, col0):
    return (lax.broadcasted_iota(jnp.int32, (rows, cols), 0)
            >= lax.broadcasted_iota(jnp.int32, (rows, cols), 1) + col0)


def _chunk(tq):
    return tq


SHIFT_MAX = 60.0


def _shift(bd):
    return jnp.where(bd <= SHIFT_MAX, 0.5 * bd, 0.0)


def _attn_fwd(qa, ka, va, fs, fe, bd, tq):
    S = qa.shape[1]
    nq = S // tq

    def body(fs_ref, fe_ref, bd_ref, qa_ref, ka_ref, va_ref, o_ref, lser_ref, m_ref, acc_ref):
        pr, i = pl.program_id(0), pl.program_id(1)
        sel_a = _lane((tq, LANES)) < DH
        acc_ref[...] = jnp.zeros_like(acc_ref)

        def logits(j, hh, masked):
            start = pl.multiple_of(j * tq, tq)
            s = lax.dot_general(qa_ref[hh], ka_ref[hh, pl.ds(start, tq), :], NT, preferred_element_type=F32)
            if masked:
                s = jnp.where(_causal_rect(tq, tq, 0), s, NEG)
            return s, va_ref[hh, pl.ds(start, tq), :]

        def shifted_step(j, heads, masked=False):
            for hh in heads:
                s, vb = logits(j, hh, masked)
                acc_ref[hh] += jnp.dot(jnp.exp(s).astype(BF16), vb, preferred_element_type=F32)

        def online_step(j, heads, masked=False):
            for hh in heads:
                s, vb = logits(j, hh, masked)
                m_prev = m_ref[hh]
                m_new = jnp.maximum(m_prev, jnp.max(s, axis=1, keepdims=True))
                p = jnp.exp(s - jnp.tile(m_new, (1, tq // LANES)))
                m_ref[hh] = m_new
                acc_ref[hh] = jnp.exp(m_prev - m_new) * acc_ref[hh] + jnp.dot(p.astype(BF16), vb,
                                                                              preferred_element_type=F32)

        def needed(hh, j):
            top = fs_ref[2 * pr + hh, i] + bd_ref[0, 0]
            return jnp.logical_and(j >= 0, top - fe_ref[2 * pr + hh, jnp.maximum(j, 0)] >= SKIP)

        @pl.when(bd_ref[0, 0] <= SHIFT_MAX)
        def _():
            m_ref[...] = jnp.zeros_like(m_ref)
            shifted_step(i, (0, 1), masked=True)
            _block_loops(i - 1, -1, needed, shifted_step)

        @pl.when(bd_ref[0, 0] > SHIFT_MAX)
        def _():
            m_ref[...] = jnp.full(m_ref.shape, NEG, F32)
            online_step(i, (0, 1), masked=True)
            _block_loops(i - 1, -1, needed, online_step)

        outs, lses = [], []
        for hh in range(2):
            acc = acc_ref[hh]
            row_sum = jnp.broadcast_to(acc[:, DH:DH + 1], (tq, LANES))
            outs.append(acc / row_sum)
            lses.append(m_ref[hh] + jnp.log(row_sum))
        o_ref[...] = jnp.where(sel_a, outs[0], pltpu.roll(outs[1], 64, 1))
        row = lax.broadcasted_iota(jnp.int32, (8, tq), 0)
        lser_ref[0, 0] = jnp.where(row == 0, lses[0].T[0:8, :], lses[1].T[0:8, :])

    return pl.pallas_call(
        body, name="attn_fwd", grid=(NH // 2, nq),
        out_shape=(_sds((S, AW), F32), _sds((NH // 2, nq, 8, tq), F32)),
        in_specs=[SMEM_SPEC, SMEM_SPEC, SMEM_SPEC,
                  pl.BlockSpec((2, tq, LANES), lambda p, i: (p, i, 0)),
                  pl.BlockSpec((2, S, LANES), lambda p, i: (p, 0, 0)),
                  pl.BlockSpec((2, S, LANES), lambda p, i: (p, 0, 0))],
        out_specs=(pl.BlockSpec((tq, LANES), lambda p, i: (i, p)),
                   pl.BlockSpec((1, 1, 8, tq), lambda p, i: (p, i, 0, 0))),
        scratch_shapes=[pltpu.VMEM((2, tq, LANES), F32), pltpu.VMEM((2, tq, LANES), F32)],
        compiler_params=_cp(("parallel", "parallel")),
    )(fs, fe, bd, qa, ka, va)


HALO = 32
CHUNK_ROWS = 64


def _halo_prev(tm):
    return pl.BlockSpec((HALO, CW), lambda i: (jnp.maximum(i * (tm // HALO) - 1, 0), 0))


SUBLANES = 8
SHIFT_ROWS = 24


def _shifted_copies(ext_ref, sh_ref, tm):
    for k in range(1, SUBLANES):
        sh_ref[k - 1, 0:tm + SHIFT_ROWS, :] = ext_ref[k:k + tm + SHIFT_ROWS, :]


def _ext_rows(ext_ref, sh_ref, o):
    k = o % SUBLANES
    if k == 0:
        return ext_ref[o:o + CHUNK_ROWS, :]
    return sh_ref[k - 1, o - k:o - k + CHUNK_ROWS, :]


def _fwd_conv(u0, w32, cb, lng, lnb, beta_c, tm):
    S = u0.shape[0]

    def body(cur_ref, prev_ref, w_ref, cb_ref, lng_ref, lnb_ref, beta_ref, u1_ref, mc_ref, ext_ref, sh_ref):
        i = pl.program_id(0)
        ext_ref[0:HALO, :] = jnp.where(i == 0, 0.0, prev_ref[...])
        ext_ref[HALO:, :] = cur_ref[...]
        _shifted_copies(ext_ref, sh_ref, tm)
        for r0 in range(0, tm, CHUNK_ROWS):
            acc = jnp.zeros((CHUNK_ROWS, CW), F32) + cb_ref[...]
            for j in range(KC):
                acc = acc + w_ref[j:j + 1, :] * _ext_rows(ext_ref, sh_ref, r0 + 2 + j)
            u1_ref[r0:r0 + CHUNK_ROWS, :] = acc
        u1 = u1_ref[...]
        mu = jnp.mean(u1, axis=-1, keepdims=True)
        d = u1 - mu
        rstd = lax.rsqrt(jnp.mean(d * d, axis=-1, keepdims=True) + EPS)
        u2 = d * rstd * lng_ref[...] + lnb_ref[...]
        u3 = u2 * _sigmoid(u2)
        rc = lax.rsqrt(jnp.mean(u3 * u3, axis=-1, keepdims=True) + EPS)
        mc_ref[...] = (u3 * rc * beta_ref[...]).astype(BF16)

    return pl.pallas_call(
        body, name="fwd_conv", grid=(S // tm,),
        out_shape=(_sds((S, CW), F32), _sds((S, CW), BF16)),
        in_specs=[_rows(tm, CW), _halo_prev(tm), _const((HALO, CW)), _const((1, CW)), _const((1, CW)), _const((1, CW)),
                  _const((1, CW))],
        out_specs=(_rows(tm, CW), _rows(tm, CW)),
        scratch_shapes=[pltpu.VMEM((tm + HALO, CW), F32), pltpu.VMEM((SUBLANES - 1, tm + HALO, CW), F32)],
        compiler_params=_cp(("parallel",)),
    )(u0, u0, w32, cb, lng, lnb, beta_c)


def _fwd_out(o_attn, mc, x, mod8, n2g, beta_a, w_out, tm):
    S = x.shape[0]

    def body(o_ref, mc_ref, x_ref, mod_ref, n2g_ref, beta_ref, w_ref, mg_ref, ob_ref, x2_ref, h2_ref):
        ov = o_ref[...]
        ra = lax.rsqrt(jnp.mean(ov * ov, axis=-1, keepdims=True) + EPS)
        ma = (ov * ra * beta_ref[...]).astype(BF16)
        mcv = mc_ref[...]
        mg_ref[:, 0:AW] = ma
        mg_ref[:, AW:D] = mcv
        o = (jnp.dot(ma, w_ref[0:AW, :], preferred_element_type=F32)
             + jnp.dot(mcv, w_ref[AW:D, :], preferred_element_type=F32))
        ob_ref[...] = o.astype(BF16)
        x2 = x_ref[...] + mod_ref[2:3, :] * o
        x2_ref[...] = x2
        r2 = lax.rsqrt(jnp.mean(x2 * x2, axis=-1, keepdims=True) + EPS)
        h2_ref[...] = ((x2 * r2) * (n2g_ref[...] * (1.0 + mod_ref[4:5, :])) + mod_ref[3:4, :]).astype(BF16)

    return pl.pallas_call(
        body, name="fwd_out", grid=(S // tm,),
        out_shape=(_sds((S, D), BF16), _sds((S, D), BF16), _sds((S, D), F32), _sds((S, D), BF16)),
        in_specs=[_rows(tm, AW), _rows(tm, CW), _rows(tm, D), _const((8, D)), _const((1, D)), _const((1, AW)),
                  _const((D, D))],
        out_specs=(_rows(tm, D), _rows(tm, D), _rows(tm, D), _rows(tm, D)),
        compiler_params=_cp(("parallel",)),
    )(o_attn, mc, x, mod8, n2g, beta_a, w_out)


def _fwd_ffn(h2, w1, w2, x2, tgt, mod8, tm):
    S = h2.shape[0]
    nk = w1.shape[0]
    bf = w1.shape[2]

    def body(h2_ref, w1_ref, w2_ref, x2_ref, tgt_ref, mod_ref, r_ref, dy_ref, loss_ref, dg2_ref, acc_ref):
        i, k = pl.program_id(0), pl.program_id(1)

        @pl.when((i == 0) & (k == 0))
        def _():
            loss_ref[...] = jnp.zeros_like(loss_ref)
            dg2_ref[...] = jnp.zeros_like(dg2_ref)

        r = jnp.maximum(jnp.dot(h2_ref[...], w1_ref[0], preferred_element_type=F32), 0.0)
        r_ref[...] = r.astype(BF16)
        part = jnp.dot((r * r).astype(BF16), w2_ref[0], preferred_element_type=F32)

        @pl.when(k == 0)
        def _():
            acc_ref[...] = part

        @pl.when(k > 0)
        def _():
            acc_ref[...] += part

        @pl.when(k == nk - 1)
        def _():
            f2 = acc_ref[...]
            e = x2_ref[...] + mod_ref[5:6, :] * f2 - tgt_ref[...]
            dy = e * (1.0 / D)
            dy_ref[...] = dy
            loss_ref[...] += 0.5 * jnp.sum(jnp.sum(e * dy, axis=1, keepdims=True), axis=0, keepdims=True)
            dg2_ref[...] += jnp.sum((dy * f2).reshape(tm // 8, 8, D), axis=0)

    return pl.pallas_call(
        body, name="fwd_ffn", grid=(S // tm, nk),
        out_shape=(_sds((S, DFF), BF16), _sds((S, D), F32), _sds((8, LANES), F32), _sds((8, D), F32)),
        in_specs=[pl.BlockSpec((tm, D), lambda i, k: (i, 0)), pl.BlockSpec((1, D, bf), lambda i, k: (k, 0, 0)),
                  pl.BlockSpec((1, bf, D), lambda i, k: (k, 0, 0)), pl.BlockSpec((tm, D), lambda i, k: (i, 0)),
                  pl.BlockSpec((tm, D), lambda i, k: (i, 0)), pl.BlockSpec((8, D), lambda i, k: (0, 0))],
        out_specs=(pl.BlockSpec((tm, bf), lambda i, k: (i, k)), pl.BlockSpec((tm, D), lambda i, k: (i, 0)),
                   pl.BlockSpec((8, LANES), lambda i, k: (0, 0)), pl.BlockSpec((8, D), lambda i, k: (0, 0))),
        scratch_shapes=[pltpu.VMEM((tm, D), F32)], compiler_params=_cp(("arbitrary", "arbitrary")),
    )(h2, w1, w2, x2, tgt, mod8)


def _bwd_ffn(dy, mod8, r, w1, w2, tm):
    S = dy.shape[0]
    nk = w1.shape[0]
    bf = w1.shape[2]

    def body(dy_ref, mod_ref, r_ref, w1_ref, w2_ref, df2_ref, df1_ref, dh2_ref):
        k = pl.program_id(1)
        df2 = (dy_ref[...] * mod_ref[5:6, :]).astype(BF16)

        @pl.when(k == 0)
        def _():
            df2_ref[...] = df2

        da = lax.dot_general(df2, w2_ref[0], NT, preferred_element_type=F32)
        df1 = (da * (2.0 * r_ref[...].astype(F32))).astype(BF16)
        df1_ref[...] = df1
        part = lax.dot_general(df1, w1_ref[0], NT, preferred_element_type=F32)

        @pl.when(k == 0)
        def _():
            dh2_ref[...] = part

        @pl.when(k > 0)
        def _():
            dh2_ref[...] += part

    return pl.pallas_call(
        body, name="bwd_ffn", grid=(S // tm, nk),
        out_shape=(_sds((S, D), BF16), _sds((S, DFF), BF16), _sds((S, D), F32)),
        in_specs=[pl.BlockSpec((tm, D), lambda i, k: (i, 0)), pl.BlockSpec((8, D), lambda i, k: (0, 0)),
                  pl.BlockSpec((tm, bf), lambda i, k: (i, k)), pl.BlockSpec((1, D, bf), lambda i, k: (k, 0, 0)),
                  pl.BlockSpec((1, bf, D), lambda i, k: (k, 0, 0))],
        out_specs=(pl.BlockSpec((tm, D), lambda i, k: (i, 0)), pl.BlockSpec((tm, bf), lambda i, k: (i, k)),
                   pl.BlockSpec((tm, D), lambda i, k: (i, 0))),
        compiler_params=_cp(("parallel", "arbitrary")),
    )(dy, mod8, r, w1, w2)


def _wgrad(a, b, name, square_a=False, tk=512, bm=1024, bn=1024):
    S, M = a.shape
    N = b.shape[1]
    bm, bn, tk = min(bm, M), min(bn, N), min(tk, S)

    def body(a_ref, b_ref, o_ref):
        av = a_ref[...]
        if square_a:
            af = av.astype(F32)
            av = (af * af).astype(BF16)
        part = lax.dot_general(av, b_ref[...], TN, preferred_element_type=F32)

        @pl.when(pl.program_id(2) == 0)
        def _():
            o_ref[...] = part

        @pl.when(pl.program_id(2) > 0)
        def _():
            o_ref[...] += part

    return pl.pallas_call(
        body, name=name, grid=(M // bm, N // bn, S // tk), out_shape=_sds((M, N), F32),
        in_specs=[pl.BlockSpec((tk, bm), lambda mi, ni, k: (k, mi)), pl.BlockSpec((tk, bn), lambda mi, ni, k: (k, ni))],
        out_specs=pl.BlockSpec((bm, bn), lambda mi, ni, k: (mi, ni)),
        compiler_params=_cp(("parallel", "parallel", "arbitrary")),
    )(a, b)


def _wgrad_in(h1, pieces, tk=512):
    S = h1.shape[0]
    tk = min(tk, S)
    widths = [p.shape[1] for p in pieces]
    offs = [sum(widths[:t]) for t in range(len(widths))]

    def body(a_ref, *refs):
        o_ref = refs[-1]

        @pl.when(pl.program_id(0) == 0)
        def _():
            o_ref[...] = jnp.zeros_like(o_ref)

        for b_ref, off, w in zip(refs[:-1], offs, widths):
            o_ref[:, off:off + w] += lax.dot_general(a_ref[...], b_ref[...], TN, preferred_element_type=F32)

    return pl.pallas_call(
        body, name="wgrad_in", grid=(S // tk,), out_shape=_sds((D, NP), F32),
        in_specs=[_rows(tk, D)] + [_rows(tk, w) for w in widths], out_specs=_const((D, NP)),
        compiler_params=_cp(("arbitrary",)),
    )(h1, *pieces)


def _colsum8(t):
    return jnp.sum(t.reshape(t.shape[0] // 8, 8, t.shape[1]), axis=0)


def _bwd_mid(dh2, dy, x2, ob, o_attn, u1, mod8, n2g, beta_a, beta_c, lng, lnb, w_out, e512, tm, tq, ex=None):
    S = dy.shape[0]

    def body(dh2_ref, dy_ref, x2_ref, ob_ref, oa_ref, u1_ref, mod_ref, n2g_ref, ba_ref, bc_ref, lng_ref, lnb_ref, w_ref,
             e_ref, dx2_ref, do_ref, doa_ref, du1_ref, acc_d_ref, acc_h_ref, dr_ref):
        @pl.when(pl.program_id(0) == 0)
        def _():
            acc_d_ref[...] = jnp.zeros_like(acc_d_ref)
            acc_h_ref[...] = jnp.zeros_like(acc_h_ref)

        x2 = x2_ref[...]
        dh2 = dh2_ref[...]
        r2 = lax.rsqrt(jnp.mean(x2 * x2, axis=-1, keepdims=True) + EPS)
        xn2 = x2 * r2
        gain = n2g_ref[...] * (1.0 + mod_ref[4:5, :])
        dxn = dh2 * gain
        dx2 = dy_ref[...] + r2 * (dxn - xn2 * jnp.mean(dxn * xn2, axis=-1, keepdims=True))
        dx2_ref[...] = dx2
        t = dh2 * xn2
        acc_d_ref[0] += _colsum8(dh2)
        acc_d_ref[1] += _colsum8(t * n2g_ref[...])
        acc_d_ref[2] += _colsum8(t * (1.0 + mod_ref[4:5, :]))
        acc_d_ref[3] += _colsum8(dx2 * ob_ref[...].astype(F32))
        do = (dx2 * mod_ref[2:3, :]).astype(BF16)
        do_ref[...] = do
        dma = lax.dot_general(do, w_ref[0:AW, :], NT, preferred_element_type=F32)
        dmc = lax.dot_general(do, w_ref[AW:D, :], NT, preferred_element_type=F32)
        ov = oa_ref[...]
        ra = lax.rsqrt(jnp.mean(ov * ov, axis=-1, keepdims=True) + EPS)
        on = ov * ra
        acc_h_ref[0] += _colsum8(dma * on)
        don = dma * ba_ref[...]
        doa = (ra * (don - on * jnp.mean(don * on, axis=-1, keepdims=True))).astype(BF16)
        doa_ref[...] = doa
        delta_t = _dot2(doa.astype(F32) * ov, e_ref[...]).T
        for p in range(NH // 2):
            dr_ref[p, 0] = delta_t[2 * p:2 * p + 8, :]
        u1 = u1_ref[...]
        mu = jnp.mean(u1, axis=-1, keepdims=True)
        d = u1 - mu
        rstd = lax.rsqrt(jnp.mean(d * d, axis=-1, keepdims=True) + EPS)
        uh = d * rstd
        u2 = uh * lng_ref[...] + lnb_ref[...]
        sg = _sigmoid(u2)
        u3 = u2 * sg
        rc = lax.rsqrt(jnp.mean(u3 * u3, axis=-1, keepdims=True) + EPS)
        u3n = u3 * rc
        acc_h_ref[1] += _colsum8(dmc * u3n)
        du3n = dmc * bc_ref[...]
        du3 = rc * (du3n - u3n * jnp.mean(du3n * u3n, axis=-1, keepdims=True))
        du2 = du3 * (sg * (1.0 + u2 * (1.0 - sg)))
        acc_h_ref[2] += _colsum8(du2 * uh)
        acc_h_ref[3] += _colsum8(du2)
        duh = du2 * lng_ref[...]
        du1_ref[...] = rstd * (duh - jnp.mean(duh, axis=-1, keepdims=True)
                               - uh * jnp.mean(duh * uh, axis=-1, keepdims=True))

    per = tq // tm
    outs = pl.pallas_call(
        _hosted(body, ex, 14, 7, 0, S // tm), name="bwd_mid", grid=(S // tm,),
        out_shape=(_sds((S, D), F32), _sds((S, D), BF16), _sds((S, AW), BF16), _sds((S, CW), F32),
                   _sds((4, 8, D), F32), _sds((4, 8, AW), F32), _sds((NH // 2, S // tq, 8, tq), F32),
                   *(ex.out_shapes if ex else [])),
        in_specs=[_rows(tm, D), _rows(tm, D), _rows(tm, D), _rows(tm, D), _rows(tm, AW), _rows(tm, CW), _const((8, D)),
                  _const((1, D)), _const((1, AW)), _const((1, CW)), _const((1, CW)), _const((1, CW)), _const((D, D)),
                  _const((AW, LANES)), *(ex.in_specs if ex else [])],
        out_specs=(_rows(tm, D), _rows(tm, D), _rows(tm, AW), _rows(tm, CW), _const((4, 8, D)), _const((4, 8, AW)),
                   pl.BlockSpec((NH // 2, 1, 8, tm), lambda i: (0, i // per, 0, i % per)),
                   *(ex.out_specs if ex else [])),
        scratch_shapes=ex.scratch if ex else [],
        compiler_params=_cp(("arbitrary",)),
    )(dh2, dy, x2, ob, o_attn, u1, mod8, n2g, beta_a, beta_c, lng, lnb, w_out, e512, *(ex.srcs if ex else []))
    return outs[:7], list(outs[7:])


def _attn_bwd_dq(qa, ka, v, do, o_attn, lsec, fs, fe, bd, tq):
    S = qa.shape[1]
    nq = S // tq
    tc = _chunk(tq)

    def body(fs_ref, fe_ref, bd_ref, qa_ref, ka_ref, v_ref, do_ref, o_ref, lse_ref, dqa_ref, dr_ref, acc_ref):
        pr, i = pl.program_id(0), pl.program_id(1)
        sel_a = _lane((tq, LANES)) < DH
        dov = do_ref[...]
        prod = dov.astype(F32) * o_ref[...]
        zero = jnp.zeros_like(prod)
        deltas = [jnp.broadcast_to(jnp.sum(jnp.where(sel_a, prod, zero), axis=1, keepdims=True), (tq, LANES)),
                  jnp.broadcast_to(jnp.sum(jnp.where(sel_a, zero, prod), axis=1, keepdims=True), (tq, LANES))]
        zb = jnp.zeros_like(dov)
        dos = [jnp.where(sel_a, dov, zb), jnp.where(sel_a, zb, dov)]
        acc_ref[...] = jnp.zeros_like(acc_ref)

        def kv_step(j, heads, masked=False):
            for c0 in range(0, tq, tc):
                start = pl.multiple_of(j * tq + c0, tc)
                vb = v_ref[pl.ds(start, tc), :]
                for hh in heads:
                    kb = ka_ref[hh, pl.ds(start, tc), :]
                    s = lax.dot_general(qa_ref[hh], kb, NT, preferred_element_type=F32)
                    p = jnp.exp(s - jnp.tile(lse_ref[hh], (1, tc // LANES)))
                    if masked:
                        p = jnp.where(_causal_rect(tq, tc, c0), p, 0.0)
                    dp = lax.dot_general(dos[hh], vb, NT, preferred_element_type=F32)
                    ds = p * (dp - jnp.tile(deltas[hh], (1, tc // LANES)))
                    acc_ref[hh] += jnp.dot(ds.astype(BF16), kb, preferred_element_type=F32)

        def needed(hh, j):
            top = fs_ref[2 * pr + hh, i] + bd_ref[0, 0]
            return jnp.logical_and(j >= 0, top - fe_ref[2 * pr + hh, jnp.maximum(j, 0)] >= SKIP)

        kv_step(i, (0, 1), masked=True)
        _block_loops(i - 1, -1, needed, kv_step)
        dqa_ref[...] = acc_ref[...]
        row = lax.broadcasted_iota(jnp.int32, (8, tq), 0)
        da = deltas[0].T[0:8, :]
        db = deltas[1].T[0:8, :]
        dr_ref[0, 0] = jnp.where(row == 0, da, db)

    return pl.pallas_call(
        body, name="attn_bwd_dq", grid=(NH // 2, nq),
        out_shape=(_sds((NH, S, LANES), F32), _sds((NH // 2, nq, 8, tq), F32)),
        in_specs=[SMEM_SPEC, SMEM_SPEC, SMEM_SPEC,
                  pl.BlockSpec((2, tq, LANES), lambda p, i: (p, i, 0)),
                  pl.BlockSpec((2, S, LANES), lambda p, i: (p, 0, 0)),
                  pl.BlockSpec((S, LANES), lambda p, i: (0, p)),
                  pl.BlockSpec((tq, LANES), lambda p, i: (i, p)),
                  pl.BlockSpec((tq, LANES), lambda p, i: (i, p)),
                  pl.BlockSpec((2, tq, LANES), lambda p, i: (p, i, 0))],
        out_specs=(pl.BlockSpec((2, tq, LANES), lambda p, i: (p, i, 0)),
                   pl.BlockSpec((1, 1, 8, tq), lambda p, i: (p, i, 0, 0))),
        scratch_shapes=[pltpu.VMEM((2, tq, LANES), F32)],
        compiler_params=_cp(("parallel", "parallel")),
    )(fs, fe, bd, qa, ka, v, do, o_attn, lsec)


def _attn_bwd_dkv(qa, ka, v, do, lser, dr, fs, fe, bd, tq):
    S = qa.shape[1]
    nq = S // tq
    tc = _chunk(tq)

    def body(fs_ref, fe_ref, bd_ref, ka_ref, v_ref, qa_ref, do_ref, lse_ref, dr_ref, dka_ref, dv_ref, acck_ref,
             accv_ref):
        pr, j = pl.program_id(0), pl.program_id(1)
        sel_a = _lane((tq, LANES)) < DH
        vv = v_ref[...]
        zb = jnp.zeros_like(vv)
        vs = [jnp.where(sel_a, vv, zb), jnp.where(sel_a, zb, vv)]
        acck_ref[...] = jnp.zeros_like(acck_ref)
        accv_ref[...] = jnp.zeros_like(accv_ref)

        def q_step(i, heads, masked=False):
            lse8 = lse_ref[0, i]
            dr8 = dr_ref[0, i]
            for c0 in range(0, tq, tc):
                start = pl.multiple_of(i * tq + c0, tc)
                dob = do_ref[pl.ds(start, tc), :]
                for hh in heads:
                    qb = qa_ref[hh, pl.ds(start, tc), :]
                    st = lax.dot_general(ka_ref[hh], qb, NT, preferred_element_type=F32)
                    pt = jnp.exp(st - lse8[hh:hh + 1, c0:c0 + tc])
                    if masked:
                        keep = (lax.broadcasted_iota(jnp.int32, (tq, tc), 0)
                                <= lax.broadcasted_iota(jnp.int32, (tq, tc), 1) + c0)
                        pt = jnp.where(keep, pt, 0.0)
                    accv_ref[hh] += jnp.dot(pt.astype(BF16), dob, preferred_element_type=F32)
                    dpt = lax.dot_general(vs[hh], dob, NT, preferred_element_type=F32)
                    dst = pt * (dpt - dr8[hh:hh + 1, c0:c0 + tc])
                    acck_ref[hh] += jnp.dot(dst.astype(BF16), qb, preferred_element_type=F32)

        def needed(hh, i):
            top = fs_ref[2 * pr + hh, jnp.minimum(i, nq - 1)] + bd_ref[0, 0]
            return jnp.logical_and(i < nq, top - fe_ref[2 * pr + hh, j] >= SKIP)

        q_step(j, (0, 1), masked=True)
        _block_loops(j + 1, 1, needed, q_step)
        dka_ref[...] = acck_ref[...]
        dv_ref[...] = jnp.where(sel_a, accv_ref[0], accv_ref[1]).astype(BF16)

    return pl.pallas_call(
        body, name="attn_bwd_dkv", grid=(NH // 2, nq),
        out_shape=(_sds((NH, S, LANES), F32), _sds((S, AW), BF16)),
        in_specs=[SMEM_SPEC, SMEM_SPEC, SMEM_SPEC,
                  pl.BlockSpec((2, tq, LANES), lambda p, j: (p, j, 0)),
                  pl.BlockSpec((tq, LANES), lambda p, j: (j, p)),
                  pl.BlockSpec((2, S, LANES), lambda p, j: (p, 0, 0)),
                  pl.BlockSpec((S, LANES), lambda p, j: (0, p)),
                  pl.BlockSpec((1, nq, 8, tq), lambda p, j: (p, 0, 0, 0)),
                  pl.BlockSpec((1, nq, 8, tq), lambda p, j: (p, 0, 0, 0))],
        out_specs=(pl.BlockSpec((2, tq, LANES), lambda p, j: (p, j, 0)),
                   pl.BlockSpec((tq, LANES), lambda p, j: (j, p))),
        scratch_shapes=[pltpu.VMEM((2, tq, LANES), F32), pltpu.VMEM((2, tq, LANES), F32)],
        compiler_params=_cp(("parallel", "parallel")),
    )(fs, fe, bd, ka, v, qa, do, lser, dr)


def _attn_bwd(qa, ka, v, do, lser, dr, fs, fe, bd, tq):
    S = qa.shape[1]
    nq = S // tq

    def body(fs_ref, fe_ref, bd_ref, ka_ref, v_ref, qa_ref, do_ref, lse_ref, dr_ref, dqa_hbm, dka_ref, dv_ref,
             accq_ref, acck_ref, accv_ref, out_sem):
        pr, j = pl.program_id(0), pl.program_id(1)
        sel_a = _lane((tq, LANES)) < DH
        vv = v_ref[...]
        zb = jnp.zeros_like(vv)
        vs = [jnp.where(sel_a, vv, zb), jnp.where(sel_a, zb, vv)]
        acck_ref[...] = jnp.zeros_like(acck_ref)
        accv_ref[...] = jnp.zeros_like(accv_ref)

        @pl.when(j == 0)
        def _():
            accq_ref[...] = jnp.zeros_like(accq_ref)

        def q_step(i, heads, masked=False):
            start = pl.multiple_of(i * tq, tq)
            dob = do_ref[pl.ds(start, tq), :]
            lse8 = lse_ref[0, i]
            dr8 = dr_ref[0, i]
            for hh in heads:
                qb = qa_ref[hh, pl.ds(start, tq), :]
                kb = ka_ref[hh]
                st = lax.dot_general(kb, qb, NT, preferred_element_type=F32)
                pt = jnp.exp(st - lse8[hh:hh + 1, :])
                if masked:
                    keep = (lax.broadcasted_iota(jnp.int32, (tq, tq), 0)
                            <= lax.broadcasted_iota(jnp.int32, (tq, tq), 1))
                    pt = jnp.where(keep, pt, 0.0)
                accv_ref[hh] += jnp.dot(pt.astype(BF16), dob, preferred_element_type=F32)
                dpt = lax.dot_general(vs[hh], dob, NT, preferred_element_type=F32)
                dst = (pt * (dpt - dr8[hh:hh + 1, :])).astype(BF16)
                acck_ref[hh] += jnp.dot(dst, qb, preferred_element_type=F32)
                accq_ref[hh, pl.ds(start, tq), :] += lax.dot_general(dst, kb, TN, preferred_element_type=F32)

        def needed(hh, i):
            top = fs_ref[2 * pr + hh, jnp.minimum(i, nq - 1)] + bd_ref[0, 0]
            return jnp.logical_and(i < nq, top - fe_ref[2 * pr + hh, j] >= SKIP)

        q_step(j, (0, 1), masked=True)
        _block_loops(j + 1, 1, needed, q_step)
        dka_ref[...] = acck_ref[...]
        dv_ref[...] = jnp.where(sel_a, accv_ref[0], accv_ref[1]).astype(BF16)

        @pl.when(j == nq - 1)
        def _():
            out = pltpu.make_async_copy(accq_ref, dqa_hbm.at[pl.ds(2 * pr, 2)], out_sem)
            out.start()
            out.wait()

    once = pl.Buffered(1)
    return pl.pallas_call(
        body, name="attn_bwd", grid=(NH // 2, nq),
        out_shape=(_sds((NH, S, LANES), F32), _sds((NH, S, LANES), F32), _sds((S, AW), BF16)),
        in_specs=[SMEM_SPEC, SMEM_SPEC, SMEM_SPEC,
                  pl.BlockSpec((2, tq, LANES), lambda p, j: (p, j, 0)),
                  pl.BlockSpec((tq, LANES), lambda p, j: (j, p)),
                  pl.BlockSpec((2, S, LANES), lambda p, j: (p, 0, 0), pipeline_mode=once),
                  pl.BlockSpec((S, LANES), lambda p, j: (0, p), pipeline_mode=once),
                  pl.BlockSpec((1, nq, 8, tq), lambda p, j: (p, 0, 0, 0)),
                  pl.BlockSpec((1, nq, 8, tq), lambda p, j: (p, 0, 0, 0))],
        out_specs=(pl.BlockSpec(memory_space=pl.ANY),
                   pl.BlockSpec((2, tq, LANES), lambda p, j: (p, j, 0)),
                   pl.BlockSpec((tq, LANES), lambda p, j: (j, p))),
        scratch_shapes=[pltpu.VMEM((2, S, LANES), F32), pltpu.VMEM((2, tq, LANES), F32),
                        pltpu.VMEM((2, tq, LANES), F32), pltpu.SemaphoreType.DMA],
        compiler_params=_cp(("arbitrary", "arbitrary")),
    )(fs, fe, bd, ka, v, qa, do, lser, dr)


def _bwd_conv(du1, u0, alin, agate, w32, tm, ex=None):
    S = du1.shape[0]
    nt = S // tm

    def body(du_ref, dun_ref, u0_ref, u0p_ref, alin_ref, agate_ref, w_ref,
             dalin_ref, dagate_ref, dw_ref, db_ref, extd_ref, extu_ref, du0_ref, shd_ref, shu_ref):
        i = pl.program_id(0)

        @pl.when(i == 0)
        def _():
            dw_ref[...] = jnp.zeros_like(dw_ref)
            db_ref[...] = jnp.zeros_like(db_ref)

        extd_ref[0:tm, :] = du_ref[...]
        extd_ref[tm:, :] = jnp.where(i == nt - 1, 0.0, dun_ref[...])
        extu_ref[0:HALO, :] = jnp.where(i == 0, 0.0, u0p_ref[...])
        extu_ref[HALO:, :] = u0_ref[...]
        _shifted_copies(extd_ref, shd_ref, tm)
        _shifted_copies(extu_ref, shu_ref, tm)
        db_ref[...] += _colsum8(du_ref[...])
        for r0 in range(0, tm, CHUNK_ROWS):
            duc = du_ref[r0:r0 + CHUNK_ROWS, :]
            acc = jnp.zeros((CHUNK_ROWS, CW), F32)
            for j in range(KC):
                acc = acc + w_ref[j:j + 1, :] * _ext_rows(extd_ref, shd_ref, r0 + 30 - j)
                dw_ref[j] += _colsum8(duc * _ext_rows(extu_ref, shu_ref, r0 + 2 + j))
            du0_ref[r0:r0 + CHUNK_ROWS, :] = acc
        du0 = du0_ref[...]
        al = alin_ref[...].astype(F32)
        sg = _sigmoid(agate_ref[...].astype(F32))
        dalin_ref[...] = (du0 * sg).astype(BF16)
        dagate_ref[...] = (du0 * al * sg * (1.0 - sg)).astype(BF16)

    nxt = pl.BlockSpec((HALO, CW), lambda i: (jnp.minimum((i + 1) * (tm // HALO), S // HALO - 1), 0))
    outs = pl.pallas_call(
        _hosted(body, ex, 7, 4, 5, nt), name="bwd_conv", grid=(nt,),
        out_shape=(_sds((S, CW), BF16), _sds((S, CW), BF16), _sds((HALO, 8, CW), F32), _sds((8, CW), F32),
                   *(ex.out_shapes if ex else [])),
        in_specs=[_rows(tm, CW), nxt, _rows(tm, CW), _halo_prev(tm), _rows(tm, CW), _rows(tm, CW), _const((HALO, CW)),
                  *(ex.in_specs if ex else [])],
        out_specs=(_rows(tm, CW), _rows(tm, CW), _const((HALO, 8, CW)), _const((8, CW)), *(ex.out_specs if ex else [])),
        scratch_shapes=[pltpu.VMEM((tm + HALO, CW), F32), pltpu.VMEM((tm + HALO, CW), F32), pltpu.VMEM((tm, CW), F32),
                        pltpu.VMEM((SUBLANES - 1, tm + HALO, CW), F32), pltpu.VMEM((SUBLANES - 1, tm + HALO, CW), F32),
                        *(ex.scratch if ex else [])],
        compiler_params=_cp(("arbitrary",)),
    )(du1, du1, u0, u0, alin, agate, w32, *(ex.srcs if ex else []))
    return outs[:4], list(outs[4:])


def _bwd_qk(dqa, dka, qn, kn, rq, rk, fgb, qg512, kg512, e512, et512, tm):
    S = qn.shape[0]
    nt = S // tm

    def body(dqa_ref, dka_ref, qn_ref, kn_ref, rq_ref, rk_ref, fgb_ref, qg_ref, kg_ref, e_ref, et_ref,
             dq_ref, dk_ref, dfg_ref, accg_ref, accb_ref, carry_ref):
        @pl.when(pl.program_id(0) == 0)
        def _():
            carry_ref[...] = jnp.zeros_like(carry_ref)
            accg_ref[...] = jnp.zeros_like(accg_ref)
            accb_ref[...] = jnp.zeros_like(accb_ref)

        lane = _lane((tm, LANES))
        sel_a = lane < DH
        df = jnp.zeros((tm, LANES), F32)
        for h in range(NH):
            col = dqa_ref[h][:, 64:65] - dka_ref[h][:, 67:68]
            df = jnp.where(lane == h, col, df)
        tri = (lax.broadcasted_iota(jnp.int32, (tm, tm), 0) <= lax.broadcasted_iota(jnp.int32, (tm, tm), 1)
               ).astype(F32).astype(BF16)
        dlf = _dot3(tri, df) + carry_ref[0:1, :]
        carry_ref[...] = jnp.broadcast_to(dlf[0:1, :], carry_ref.shape)
        dfg = jnp.where(lane < NH, dlf * _sigmoid(-fgb_ref[...]), 0.0)
        dfg_ref[...] = dfg.astype(BF16)
        accb_ref[...] += _colsum8(dfg)

        def norm_bwd(src_ref, n_ref, r_ref, g_ref, scale, slot):
            pairs = []
            for p in range(NH // 2):
                b = pltpu.roll(src_ref[2 * p + 1], 64, 1)
                pairs.append(jnp.where(sel_a, src_ref[2 * p], b))
            dh = jnp.concatenate(pairs, axis=1) * scale
            tn = n_ref[...].astype(F32)
            accg_ref[slot] += _colsum8(dh * tn)
            dn = dh * g_ref[...]
            mean = _dot2(dn * tn, e_ref[...]) * (1.0 / DH)
            corr = _dot2(mean, et_ref[...])
            rf = _dot2(r_ref[...], et_ref[...])
            return (rf * (dn - tn * corr)).astype(BF16)

        dq_ref[...] = norm_bwd(dqa_ref, qn_ref, rq_ref, qg_ref, DH ** -0.5, 0)
        dk_ref[...] = norm_bwd(dka_ref, kn_ref, rk_ref, kg_ref, 1.0, 1)

    rev = lambda n: pl.BlockSpec((tm, n), lambda i: (nt - 1 - i, 0))
    hm = pl.BlockSpec((NH, tm, LANES), lambda i: (0, nt - 1 - i, 0))
    dq, dk, dfg, accg, accb = pl.pallas_call(
        body, name="bwd_qk", grid=(nt,),
        out_shape=(_sds((S, AW), BF16), _sds((S, AW), BF16), _sds((S, LANES), BF16), _sds((2, 8, AW), F32),
                   _sds((8, LANES), F32)),
        in_specs=[hm, hm, rev(AW), rev(AW), rev(LANES), rev(LANES), rev(LANES), _const((1, AW)), _const((1, AW)),
                  _const((AW, LANES)), _const((LANES, AW))],
        out_specs=(rev(AW), rev(AW), rev(LANES), _const((2, 8, AW)), _const((8, LANES))),
        scratch_shapes=[pltpu.VMEM((8, LANES), F32)], compiler_params=_cp(("arbitrary",)),
    )(dqa, dka, qn, kn, rq, rk, fgb, qg512, kg512, e512, et512)
    return dq, dk, dfg, accg, accb


def _bwd_in(dq, dk, dv, dalin, dagate, dfg, w_in_p, x, dx2, mod8, n1g, tm, ex=None):
    S = x.shape[0]

    def body(dq_ref, dk_ref, dv_ref, dal_ref, dag_ref, dfg_ref, w_ref, x_ref, dx2_ref, mod_ref, n1g_ref,
             dx_ref, acc_ref):
        @pl.when(pl.program_id(0) == 0)
        def _():
            acc_ref[...] = jnp.zeros_like(acc_ref)

        def part(ref, a, b):
            return lax.dot_general(ref[...], w_ref[:, a:b], NT, preferred_element_type=F32)

        dh = (part(dq_ref, 0, 512) + part(dk_ref, 512, 1024) + part(dv_ref, 1024, 1536) + part(dal_ref, 1536, 2048)
              + part(dag_ref, 2048, 2560) + part(dfg_ref, 2560, NP))
        xv = x_ref[...]
        r1 = lax.rsqrt(jnp.mean(xv * xv, axis=-1, keepdims=True) + EPS)
        xn = xv * r1
        gain = n1g_ref[...] * (1.0 + mod_ref[1:2, :])
        t = dh * xn
        acc_ref[0] += _colsum8(dh)
        acc_ref[1] += _colsum8(t * n1g_ref[...])
        acc_ref[2] += _colsum8(t * (1.0 + mod_ref[1:2, :]))
        dxn = dh * gain
        dx_ref[...] = dx2_ref[...] + r1 * (dxn - xn * jnp.mean(dxn * xn, axis=-1, keepdims=True))

    outs = pl.pallas_call(
        _hosted(body, ex, 11, 2, 0, S // tm), name="bwd_in", grid=(S // tm,),
        out_shape=(_sds((S, D), F32), _sds((3, 8, D), F32), *(ex.out_shapes if ex else [])),
        in_specs=[_rows(tm, AW), _rows(tm, AW), _rows(tm, AW), _rows(tm, CW), _rows(tm, CW), _rows(tm, LANES),
                  _const((D, NP)), _rows(tm, D), _rows(tm, D), _const((8, D)), _const((1, D)),
                  *(ex.in_specs if ex else [])],
        out_specs=(_rows(tm, D), _const((3, 8, D)), *(ex.out_specs if ex else [])),
        scratch_shapes=ex.scratch if ex else [],
        compiler_params=_cp(("arbitrary",)),
    )(dq, dk, dv, dalin, dagate, dfg, w_in_p, x, dx2, mod8, n1g, *(ex.srcs if ex else []))
    return outs[:2], list(outs[2:])


def _adam(w, g, m, v):
    m_new = B1 * m + (1.0 - B1) * g
    v_new = B2 * v + (1.0 - B2) * (g * g)
    m_hat = m_new / (1.0 - B1 ** STEP)
    v_hat = v_new / (1.0 - B2 ** STEP)
    delta = -LR * (m_hat / (jnp.sqrt(v_hat) + AEPS) + WD * w)
    return delta, m_new, v_new


def _reduce_adamw(slots, w, m, v, name, tr=256):
    ns, R, C = slots.shape
    tr = tr if R % tr == 0 else R

    def body(s_ref, w_ref, m_ref, v_ref, g_ref, d_ref, mo_ref, vo_ref):
        g = s_ref[0].astype(F32)
        for k in range(1, ns):
            g = g + s_ref[k].astype(F32)
        g_ref[...] = g
        d_ref[...], mo_ref[...], vo_ref[...] = _adam(w_ref[...], g, m_ref[...], v_ref[...])

    blk = pl.BlockSpec((tr, C), lambda i: (i, 0))
    return pl.pallas_call(
        body, name=name, grid=(R // tr,), out_shape=tuple(_sds((R, C), F32) for _ in range(4)),
        in_specs=[pl.BlockSpec((ns, tr, C), lambda i: (0, i, 0)), blk, blk, blk], out_specs=(blk, blk, blk, blk),
        compiler_params=_cp(("parallel",)),
    )(slots, w, m, v)


def _pair_adamw(slots, w, m, v, name, tr=256):
    ns, R, C = slots.shape
    tr = tr if R % tr == 0 else R
    nt = R // tr

    def body(s_ref, w_ref, m_ref, v_ref, g_ref, d_ref, mo_ref, vo_ref, mine_ref, theirs_ref, send_sems, recv_sems):
        i = pl.program_id(0)
        part = s_ref[0].astype(F32)
        for k in range(1, ns):
            part = part + s_ref[k].astype(F32)
        mine_ref[i] = part
        swap = pltpu.make_async_remote_copy(
            src_ref=mine_ref.at[i], dst_ref=theirs_ref.at[i], send_sem=send_sems.at[i], recv_sem=recv_sems.at[i],
            device_id=(lax.axis_index("x"), lax.axis_index("y"), 1 - lax.axis_index("c")),
            device_id_type=pl.DeviceIdType.MESH)
        swap.start()
        swap.wait()
        g = part + theirs_ref[i]
        g_ref[...] = g
        d_ref[...], mo_ref[...], vo_ref[...] = _adam(w_ref[...], g, m_ref[...], v_ref[...])

    blk = pl.BlockSpec((tr, C), lambda i: (i, 0))
    return pl.pallas_call(
        body, name=name, grid=(nt,), out_shape=tuple(_sds((R, C), F32) for _ in range(4)),
        in_specs=[pl.BlockSpec((ns, tr, C), lambda i: (0, i, 0)), blk, blk, blk], out_specs=(blk, blk, blk, blk),
        scratch_shapes=[pltpu.VMEM((nt, tr, C), F32), pltpu.VMEM((nt, tr, C), F32),
                        pltpu.SemaphoreType.DMA((nt,)), pltpu.SemaphoreType.DMA((nt,))],
        compiler_params=_cp(("arbitrary",)),
    )(slots, w, m, v)


def _ada_adamw(sct, dmod, w, m, v):
    R, C = w.shape
    tr, bc = 256, 512

    def body(sct_ref, dm_ref, w_ref, m_ref, v_ref, g_ref, d_ref, mo_ref, vo_ref):
        g = sct_ref[:, 0:1] * dm_ref[0:1, :]
        for b in range(1, N_DEV):
            g = g + sct_ref[:, b:b + 1] * dm_ref[b:b + 1, :]
        g_ref[...] = g
        d_ref[...], mo_ref[...], vo_ref[...] = _adam(w_ref[...], g, m_ref[...], v_ref[...])

    blk = pl.BlockSpec((tr, bc), lambda i, j: (i, j))
    return pl.pallas_call(
        body, name="ada_adamw", grid=(R // tr, C // bc), out_shape=tuple(_sds((R, C), F32) for _ in range(4)),
        in_specs=[pl.BlockSpec((tr, N_DEV), lambda i, j: (i, 0)), pl.BlockSpec((N_DEV, bc), lambda i, j: (0, j)),
                  blk, blk, blk],
        out_specs=(blk, blk, blk, blk), compiler_params=_cp(("parallel", "parallel")),
    )(sct, dmod, w, m, v)


def _small_reduce(slots, fold):
    def body(s_ref, f_ref, o_ref):
        tot = s_ref[0:1, :]
        for k in range(1, N_DEV):
            tot = tot + s_ref[k:k + 1, :]
        o_ref[:, 0:6144] = tot[:, 0:6144]
        o_ref[:, 6144:7168] = tot[:, 6144:7168]
        for t, src in enumerate((8192, 8704)):
            v8 = jnp.broadcast_to(tot[:, src:src + AW], (8, AW))
            o_ref[:, 7168 + t * LANES:7168 + (t + 1) * LANES] = jnp.dot(
                v8, f_ref[...], precision=HI, preferred_element_type=F32)[0:1, :]
        o_ref[:, 7424:7552] = tot[:, 9216:9344]
        o_ref[:, 7552:10112] = tot[:, 9344:11904]
        o_ref[:, 10112:SMALL_OUT] = tot[:, 7168:8192]

    return pl.pallas_call(
        body, name="small_reduce", out_shape=_sds((1, SMALL_OUT), F32),
        in_specs=[pl.BlockSpec(memory_space=pltpu.VMEM), pl.BlockSpec(memory_space=pltpu.VMEM)],
        out_specs=pl.BlockSpec(memory_space=pltpu.VMEM),
    )(slots, fold)


def _perm_in(w):
    pad = jnp.zeros((w.shape[0], NP - 2568), w.dtype)
    return jnp.concatenate([w[:, :1536], w[:, 1544:2568], w[:, 1536:1544], pad], axis=1)


def _pad_lanes(vec, n=LANES):
    return jnp.pad(vec, ((0, 0), (0, n - vec.shape[1])))


def kernel(x, c, w_ada, b_ada, norm1_g, w_in, q_norm_g, k_norm_g, b_f, conv_w, conv_b, conv_ln_g, conv_ln_b, beta_attn, beta_conv, w_out, norm2_g, w_ff1, w_ff2, loss_target, m_w_ada, m_b_ada, m_norm1_g, m_w_in, m_q_norm_g, m_k_norm_g, m_b_f, m_conv_w, m_conv_b, m_conv_ln_g, m_conv_ln_b, m_beta_attn, m_beta_conv, m_w_out, m_norm2_g, m_w_ff1, m_w_ff2, v_w_ada, v_b_ada, v_norm1_g, v_w_in, v_q_norm_g, v_k_norm_g, v_b_f, v_conv_w, v_conv_b, v_conv_ln_g, v_conv_ln_b, v_beta_attn, v_beta_conv, v_w_out, v_norm2_g, v_w_ff1, v_w_ff2):
    S = x.shape[1]
    tm = min(256, S)
    tq = min(512, S // 2)
    xs, tgt = x[0], loss_target[0]
    chip = 2 * lax.axis_index("x") + lax.axis_index("y")
    e512, et512 = _head_sum_mats()

    conv_w32 = jnp.pad(conv_w[0], ((0, 1), (0, 0)))
    c_all, g_in = _exchange([(c, "bcast8"), (w_in[0].astype(BF16), "chip4")], "gather_in")
    later_weights = _Exchange([(w_out[0].astype(BF16), "chip4"), (w_ff1[0].astype(BF16), "chip4"),
                               (w_ff2[0].astype(BF16), "chip4"), (conv_w32, "chip4")])
    c_all = c_all.reshape(N_DEV, D)
    w_in_p = _perm_in(jnp.transpose(g_in, (1, 0, 2)).reshape(D, 2568))

    b_shard = lax.dynamic_slice(b_ada, (0, chip * 1536), (1, 1536))
    mod_rows, sc_all = _mod_shard(c_all, w_ada[0], b_shard)
    (mod_slots,) = _exchange([(mod_rows.reshape(N_DEV, 1, 1536), "all8")], "scatter_mod")
    mod = mod_slots.reshape(4, 2, 1536)[:, 0, :].reshape(6, D)
    mod8 = jnp.pad(mod, ((0, 2), (0, 0)))

    qg512 = jnp.tile(q_norm_g, (1, NH))
    kg512 = jnp.tile(k_norm_g, (1, NH))
    bf128 = _pad_lanes(b_f)

    (h1, qh, kh, vb, qn, kn, rq, rk, fgb, alin, agate, u0), (g_out, g_ff1, g_ff2, g_cw) = _fwd_in(
        xs, mod8, norm1_g, w_in_p, e512, et512, qg512, kg512, bf128, tm, ex=later_weights)
    w_out_f = g_out.reshape(D, D)
    w1, w2 = g_ff1, g_ff2
    cw32 = jnp.transpose(g_cw, (1, 0, 2)).reshape(HALO, CW)
    bd = _logit_bound(q_norm_g, k_norm_g)
    qa, ka, va, fcum = _fwd_decay(fgb, qh, kh, vb, _shift(bd), tm)
    fs, fe = _skip_tables(fcum, tq)
    o_attn, lser = _attn_fwd(qa, ka, va, fs, fe, bd, tq)
    u1, mc = _fwd_conv(u0, cw32, conv_b, conv_ln_g, conv_ln_b, beta_conv, tm)
    merged, ob, x2, h2 = _fwd_out(o_attn, mc, xs, mod8, norm2_g, beta_attn, w_out_f, tm)
    tf = min(512, S)
    r, dy, loss8, dg2 = _fwd_ffn(h2, w1, w2, x2, tgt, mod8, tf)

    df2, df1, dh2 = _bwd_ffn(dy, mod8, r, w1, w2, tf)
    gw_ff2 = _wgrad(r, df2, "wgrad_ff2", square_a=True)
    gw_ff1 = _wgrad(h2, df1, "wgrad_ff1")
    ff_grads = _Exchange([(jnp.transpose(gw_ff1.reshape(D, 4, D), (1, 0, 2)).astype(BF16), "chip4p"),
                          (gw_ff2.reshape(4, D, D).astype(BF16), "chip4p")])
    (dx2, do, doa, du1, acc_d, acc_h, dr), (p_ff1, p_ff2) = _bwd_mid(
        dh2, dy, x2, ob, o_attn, u1, mod8, norm2_g, beta_attn, beta_conv, conv_ln_g, conv_ln_b, w_out_f, e512, tm, tq,
        ex=ff_grads)
    gw_out = _wgrad(merged, do, "wgrad_out")
    dqa, dka, dv = _attn_bwd(qa, ka, vb, doa, lser, dr, fs, fe, bd, tq)
    out_grads = _Exchange([(gw_out.reshape(4, 256, D).astype(BF16), "chip4p")])
    (dalin, dagate, dcw, dcb), (p_out,) = _bwd_conv(du1, u0, alin, agate, cw32, tm, ex=out_grads)
    dq, dk, dfg, accg, accb = _bwd_qk(dqa, dka, qn, kn, rq, rk, fgb, qg512, kg512, e512, et512, tm)
    gw_in_p = _wgrad_in(h1, [dq, dk, dv, dalin, dagate, dfg])
    gw_in = jnp.concatenate([gw_in_p[:, :1536], gw_in_p[:, 2560:2568], gw_in_p[:, 1536:2560]], axis=1)
    s8 = lambda a: jnp.sum(a, axis=-2)
    gcw = s8(dcw)
    in_grads = _Exchange([(jnp.transpose(gw_in.reshape(D, 4, 642), (1, 0, 2)).astype(BF16), "chip4p"),
                          (jnp.transpose(gcw.reshape(HALO, 4, LANES), (1, 0, 2)), "chip4p")])
    (grad_x, acc1), (p_in, p_cw) = _bwd_in(dq, dk, dv, dalin, dagate, dfg, w_in_p, xs, dx2, mod8, norm1_g, tm,
                                           ex=in_grads)

    a1, ad, ah = s8(acc1), s8(acc_d), s8(acc_h)
    small = jnp.concatenate(
        [a1[0], a1[1], ad[3], ad[0], ad[1], s8(dg2),
         a1[2], ad[2], s8(accg).reshape(-1), s8(accb), s8(dcb), ah[2], ah[3], ah[0], ah[1]]).reshape(1, SMALL_IN)
    (small_s,) = _exchange([(small, "bcast8")], "gather_small")
    small_s = small_s.reshape(N_DEV, SMALL_IN)

    g_in_, d_in, nm_in, nv_in = _pair_adamw(p_in, w_in[0], m_w_in[0], v_w_in[0], "adamw_in")
    g_out_, d_out, nm_out, nv_out = _pair_adamw(p_out, w_out[0], m_w_out[0], v_w_out[0], "adamw_out")
    g_f1, d_f1, nm_f1, nv_f1 = _pair_adamw(p_ff1, w_ff1[0], m_w_ff1[0], v_w_ff1[0], "adamw_ff1")
    g_f2, d_f2, nm_f2, nv_f2 = _pair_adamw(p_ff2, w_ff2[0], m_w_ff2[0], v_w_ff2[0], "adamw_ff2")
    pad_row = lambda a, fill: jnp.pad(a[0], ((0, 1), (0, 0)), constant_values=fill)
    g_cw_, d_cw, nm_cw, nv_cw = (a[:KC] for a in _pair_adamw(
        p_cw, pad_row(conv_w, 0.0), pad_row(m_conv_w, 0.0), pad_row(v_conv_w, 1.0), "adamw_conv_w"))
    dmod_shard = lax.dynamic_slice(small_s[:, :6 * D], (0, chip * 1536), (N_DEV, 1536))
    g_ada, d_ada, nm_ada, nv_ada = _ada_adamw(sc_all.T, dmod_shard, w_ada[0], m_w_ada[0], v_w_ada[0])

    fold = np.zeros((AW, LANES), np.float32)
    fold[np.arange(AW), np.arange(AW) % DH] = 1.0
    g_small = _small_reduce(small_s, jnp.asarray(fold))
    smalls = [b_ada, norm1_g, q_norm_g, k_norm_g, b_f, conv_b, conv_ln_g, conv_ln_b, beta_attn, beta_conv, norm2_g]
    m_smalls = [m_b_ada, m_norm1_g, m_q_norm_g, m_k_norm_g, m_b_f, m_conv_b, m_conv_ln_g, m_conv_ln_b, m_beta_attn,
                m_beta_conv, m_norm2_g]
    v_smalls = [v_b_ada, v_norm1_g, v_q_norm_g, v_k_norm_g, v_b_f, v_conv_b, v_conv_ln_g, v_conv_ln_b, v_beta_attn,
                v_beta_conv, v_norm2_g]
    widths = [a.shape[1] for a in smalls]
    padded = [-(-n // LANES) * LANES for n in widths]
    pack = lambda arrs, fill: jnp.concatenate(
        [jnp.pad(a, ((0, 0), (0, p - a.shape[1])), constant_values=fill) for a, p in zip(arrs, padded)], axis=1)
    outs_small = _reduce_adamw(g_small.reshape(1, 1, SMALL_OUT), pack(smalls, 0.0), pack(m_smalls, 0.0),
                               pack(v_smalls, 1.0), "adamw_small")
    offs = np.concatenate([[0], np.cumsum(padded)])

    def unpack(a):
        return [a[:, int(o):int(o) + n] for o, n in zip(offs[:-1], widths)]

    gs, ds, ms, vs = (unpack(a) for a in outs_small)

    loss = lax.psum(loss8[0, 0], ("x", "y", "c"))
    big = {"w_ada": (g_ada, d_ada, nm_ada, nv_ada), "w_in": (g_in_, d_in, nm_in, nv_in),
           "conv_w": (g_cw_, d_cw, nm_cw, nv_cw), "w_out": (g_out_, d_out, nm_out, nv_out),
           "w_ff1": (g_f1, d_f1, nm_f1, nv_f1), "w_ff2": (g_f2, d_f2, nm_f2, nv_f2)}
    small_names = ["b_ada", "norm1_g", "q_norm_g", "k_norm_g", "b_f", "conv_b", "conv_ln_g", "conv_ln_b", "beta_attn",
                   "beta_conv", "norm2_g"]
    order = ["w_ada", "b_ada", "norm1_g", "w_in", "q_norm_g", "k_norm_g", "b_f", "conv_w", "conv_b", "conv_ln_g",
             "conv_ln_b", "beta_attn", "beta_conv", "w_out", "norm2_g", "w_ff1", "w_ff2"]

    def leaf(name, which):
        if name in big:
            return big[name][which][None]
        return (gs, ds, ms, vs)[which][small_names.index(name)]

    return (loss, grad_x[None], *[leaf(n, 0) for n in order], *[leaf(n, 1) for n in order],
            *[leaf(n, 2) for n in order], *[leaf(n, 3) for n in order])
```

```python
import functools

import numpy as np
import jax
import jax.numpy as jnp
from jax import lax
from jax.experimental import pallas as pl
from jax.experimental.pallas import tpu as pltpu

F32, BF16 = jnp.float32, jnp.bfloat16
HI = lax.Precision.HIGHEST
D = 1024
AW = 512
CW = 512
NH = 8
DH = 64
KC = 31
DFF = 4096
NP = 2688
EPS = 1e-6
NEG = -1e30
LANES = 128
VMEM_LIMIT = 56 * 2**20
NT = (((1,), (1,)), ((), ()))
TN = (((0,), (0,)), ((), ()))
LR, B1, B2, AEPS, WD, STEP = 0.001, 0.9, 0.999, 1e-08, 0.01, 10
N_DEV = 8
SMALL_IN = 11904
SMALL_OUT = 11136


def _cp(sem=None, vmem=VMEM_LIMIT):
    kw = dict(vmem_limit_bytes=vmem)
    if sem is not None:
        kw["dimension_semantics"] = sem
    return pltpu.CompilerParams(**kw)


def _rows(tm, n):
    return pl.BlockSpec((tm, n), lambda i: (i, 0))


def _const(shape):
    nd = len(shape)
    return pl.BlockSpec(shape, lambda *_: (0,) * nd)


def _sds(shape, dt):
    return jax.ShapeDtypeStruct(shape, dt)


def _lane(shape):
    return lax.broadcasted_iota(jnp.int32, shape, len(shape) - 1)


def _sigmoid(x):
    return 1.0 / (1.0 + jnp.exp(-x))


class _Exchange:
    MASKS = {"chip4": (2, 4, 6), "chip4p": (2, 4, 6), "all8": (1, 2, 3, 4, 5, 6, 7), "bcast8": (1, 2, 3, 4, 5, 6, 7)}

    def __init__(self, items):
        self.srcs = [s for s, _ in items]
        self.kinds = [k for _, k in items]
        self.n = len(items)
        self.out_shapes = []
        for s, k in items:
            shape = {"all8": (N_DEV,) + s.shape[1:], "bcast8": (N_DEV,) + s.shape, "chip4": (4,) + s.shape,
                     "chip4p": (4,) + s.shape[1:]}[k]
            self.out_shapes.append(_sds(shape, s.dtype))
        self.sem_index = {}
        for t, k in enumerate(self.kinds):
            for m in self.MASKS[k]:
                self.sem_index[(t, m)] = len(self.sem_index)
        n_sem = len(self.sem_index)
        self.scratch = [pltpu.SemaphoreType.DMA((n_sem,)), pltpu.SemaphoreType.DMA((n_sem,)),
                        pltpu.SemaphoreType.DMA((self.n,))]
        self.in_specs = [pl.BlockSpec(memory_space=pl.ANY)] * self.n
        self.out_specs = [pl.BlockSpec(memory_space=pl.ANY)] * self.n

    def copies(self, src_refs, dst_refs, send_sems, recv_sems, local_sems):
        x, y, c = lax.axis_index("x"), lax.axis_index("y"), lax.axis_index("c")
        my_id = 4 * x + 2 * y + c
        my_chip = 2 * x + y

        def piece(t, dev_id, chip):
            k = self.kinds[t]
            return src_refs[t].at[dev_id] if k == "all8" else src_refs[t].at[chip] if k == "chip4p" else src_refs[t]

        out = []
        for t in range(self.n):
            slot = dst_refs[t].at[my_chip if self.kinds[t] in ("chip4", "chip4p") else my_id]
            out.append(pltpu.make_async_copy(piece(t, my_id, my_chip), slot, local_sems.at[t]))
            for m in self.MASKS[self.kinds[t]]:
                px = 1 - x if m & 4 else x
                py = 1 - y if m & 2 else y
                pc = 1 - c if m & 1 else c
                s = self.sem_index[(t, m)]
                out.append(pltpu.make_async_remote_copy(
                    src_ref=piece(t, 4 * px + 2 * py + pc, 2 * px + py), dst_ref=slot,
                    send_sem=send_sems.at[s], recv_sem=recv_sems.at[s],
                    device_id=(px, py, pc), device_id_type=pl.DeviceIdType.MESH))
        return out


def _hosted(body, ex, n_in, n_out, n_scr, n_steps):
    if ex is None:
        return body

    def wrapped(*refs):
        ins, xin = refs[:n_in], refs[n_in:n_in + ex.n]
        o0 = n_in + ex.n
        outs, xout = refs[o0:o0 + n_out], refs[o0 + n_out:o0 + n_out + ex.n]
        s0 = o0 + n_out + ex.n
        scr, sems = refs[s0:s0 + n_scr], refs[s0 + n_scr:]

        @pl.when(pl.program_id(0) == 0)
        def _():
            for cp in ex.copies(xin, xout, *sems):
                cp.start()

        body(*ins, *outs, *scr)

        @pl.when(pl.program_id(0) == n_steps - 1)
        def _():
            for cp in ex.copies(xin, xout, *sems):
                cp.wait()

    return wrapped


def _exchange(items, name):
    ex = _Exchange(items)
    n = ex.n

    def body(*refs):
        copies = ex.copies(refs[:n], refs[n:2 * n], *refs[2 * n:])
        for cp in copies:
            cp.start()
        for cp in copies:
            cp.wait()

    outs = pl.pallas_call(
        body, name=name, out_shape=tuple(ex.out_shapes), in_specs=ex.in_specs, out_specs=tuple(ex.out_specs),
        scratch_shapes=ex.scratch,
    )(*ex.srcs)
    return list(outs)


def _mod_shard(c_all, w_ada, b_shard):
    n = w_ada.shape[1]

    def body(c_ref, w_ref, b_ref, o_ref, sc_ref):
        cv = c_ref[...]
        sc = cv * _sigmoid(cv)
        sc_ref[...] = sc
        o_ref[...] = jnp.dot(sc, w_ref[...], precision=HI, preferred_element_type=F32) + b_ref[...]

    bn = 512
    return pl.pallas_call(
        body, name="mod_shard", out_shape=(_sds((N_DEV, n), F32), _sds((N_DEV, D), F32)), grid=(n // bn,),
        in_specs=[_const((N_DEV, D)), pl.BlockSpec((D, bn), lambda j: (0, j)), pl.BlockSpec((1, bn), lambda j: (0, j))],
        out_specs=(pl.BlockSpec((N_DEV, bn), lambda j: (0, j)), _const((N_DEV, D))),
        compiler_params=_cp(("arbitrary",)),
    )(c_all, w_ada, b_shard)


def _head_sum_mats():
    e = np.zeros((AW, LANES), np.float32)
    for h in range(NH):
        e[h * DH:(h + 1) * DH, h] = 1.0
    return jnp.asarray(e, BF16), jnp.asarray(e.T.copy(), BF16)


def _dot2(x, w):
    hi = x.astype(BF16)
    lo = (x - hi.astype(F32)).astype(BF16)
    return jnp.dot(hi, w, preferred_element_type=F32) + jnp.dot(lo, w, preferred_element_type=F32)


def _fwd_in(x, mod8, n1g, w_in_p, e512, et512, qg512, kg512, bf128, tm, ex=None):
    S = x.shape[0]

    def body(x_ref, mod_ref, n1g_ref, w_ref, e_ref, et_ref, qg_ref, kg_ref, bf_ref,
             h1_ref, qh_ref, kh_ref, v_ref, qn_ref, kn_ref, rq_ref, rk_ref, fgb_ref, alin_ref, agate_ref, u0_ref):
        xv = x_ref[...]
        r1 = lax.rsqrt(jnp.mean(xv * xv, axis=-1, keepdims=True) + EPS)
        h = (xv * r1) * (n1g_ref[...] * (1.0 + mod_ref[1:2, :])) + mod_ref[0:1, :]
        hb = h.astype(BF16)
        h1_ref[...] = hb

        def seg(a, b):
            return jnp.dot(hb, w_ref[:, a:b], preferred_element_type=F32)

        def headnorm(t, g_ref, scale, n_ref, r_ref, o_ref):
            ss = _dot2(t * t, e_ref[...])
            r = lax.rsqrt(ss * (1.0 / DH) + EPS)
            tn = t * _dot2(r, et_ref[...])
            n_ref[...] = tn.astype(BF16)
            r_ref[...] = r
            o_ref[...] = (tn * (g_ref[...] * scale)).astype(BF16)

        headnorm(seg(0, 512), qg_ref, DH ** -0.5, qn_ref, rq_ref, qh_ref)
        headnorm(seg(512, 1024), kg_ref, 1.0, kn_ref, rk_ref, kh_ref)
        v_ref[...] = seg(1024, 1536).astype(BF16)
        alin = seg(1536, 2048)
        agate = seg(2048, 2560)
        alin_ref[...] = alin.astype(BF16)
        agate_ref[...] = agate.astype(BF16)
        u0_ref[...] = alin * _sigmoid(agate)
        fgb_ref[...] = seg(2560, NP) + bf_ref[...]

    bf = lambda: _sds((S, AW), BF16)
    xs = ex.srcs if ex else []
    outs = pl.pallas_call(
        _hosted(body, ex, 9, 12, 0, S // tm), name="fwd_in", grid=(S // tm,),
        out_shape=(_sds((S, D), BF16), bf(), bf(), bf(), bf(), bf(), _sds((S, LANES), F32), _sds((S, LANES), F32),
                   _sds((S, LANES), F32), bf(), bf(), _sds((S, CW), F32), *(ex.out_shapes if ex else [])),
        in_specs=[_rows(tm, D), _const((8, D)), _const((1, D)), _const((D, NP)), _const((AW, LANES)), _const((LANES, AW)),
                  _const((1, AW)), _const((1, AW)), _const((1, LANES)), *(ex.in_specs if ex else [])],
        out_specs=(_rows(tm, D), _rows(tm, AW), _rows(tm, AW), _rows(tm, AW), _rows(tm, AW), _rows(tm, AW),
                   _rows(tm, LANES), _rows(tm, LANES), _rows(tm, LANES), _rows(tm, AW), _rows(tm, AW), _rows(tm, CW),
                   *(ex.out_specs if ex else [])),
        scratch_shapes=ex.scratch if ex else [],
        compiler_params=_cp(("arbitrary",)),
    )(x, mod8, n1g, w_in_p, e512, et512, qg512, kg512, bf128, *xs)
    return outs[:12], list(outs[12:])


def _split3(f):
    f1 = f.astype(BF16).astype(F32)
    f2 = (f - f1).astype(BF16).astype(F32)
    return f1, f2, f - f1 - f2


def _dot3(w, x):
    return sum(jnp.dot(w, piece.astype(BF16), preferred_element_type=F32) for piece in _split3(x))


def _fwd_decay(fgb, qh, kh, vb, shift, tm):
    S = fgb.shape[0]

    def body(shift_ref, fgb_ref, qh_ref, kh_ref, vb_ref, qa_ref, ka_ref, va_ref, f_ref, carry_ref):
        @pl.when(pl.program_id(0) == 0)
        def _():
            carry_ref[...] = jnp.zeros_like(carry_ref)

        fb = fgb_ref[...]
        lf = jnp.minimum(fb, 0.0) - jnp.log1p(jnp.exp(-jnp.abs(fb)))
        tri = (lax.broadcasted_iota(jnp.int32, (tm, tm), 0) >= lax.broadcasted_iota(jnp.int32, (tm, tm), 1)
               ).astype(F32).astype(BF16)
        cs = _dot3(tri, lf) + carry_ref[0:1, :]
        f_ref[...] = cs
        carry_ref[...] = jnp.broadcast_to(cs[tm - 1:tm, :], carry_ref.shape)
        lane = _lane((tm, LANES))
        s1, s2, s3 = _split3(jnp.zeros((tm, LANES), F32) - shift_ref[0, 0])
        tail_q = jnp.where((lane >= 67) & (lane < 70), 1.0,
                           jnp.where(lane == 70, s1, jnp.where(lane == 71, s2, jnp.where(lane == 72, s3, 0.0))))
        tail_k = jnp.where(((lane >= 64) & (lane < 67)) | ((lane >= 70) & (lane < 73)), 1.0, 0.0)
        tail_v = jnp.where(lane == DH, 1.0, 0.0)
        for p in range(NH // 2):
            qp = qh_ref[:, p * LANES:(p + 1) * LANES].astype(F32)
            kp = kh_ref[:, p * LANES:(p + 1) * LANES].astype(F32)
            vp = vb_ref[:, p * LANES:(p + 1) * LANES].astype(F32)
            for hh in range(2):
                h = 2 * p + hh
                f1, f2, f3 = _split3(cs[:, h:h + 1])
                qb = qp if hh == 0 else pltpu.roll(qp, 64, 1)
                kb = kp if hh == 0 else pltpu.roll(kp, 64, 1)
                vh = vp if hh == 0 else pltpu.roll(vp, 64, 1)
                augq = jnp.where(lane == 64, f1, jnp.where(lane == 65, f2, jnp.where(lane == 66, f3, tail_q)))
                augk = jnp.where(lane == 67, -f1, jnp.where(lane == 68, -f2, jnp.where(lane == 69, -f3, tail_k)))
                qa_ref[h] = jnp.where(lane < DH, qb, augq).astype(BF16)
                ka_ref[h] = jnp.where(lane < DH, kb, augk).astype(BF16)
                va_ref[h] = jnp.where(lane < DH, vh, tail_v).astype(BF16)

    hm = pl.BlockSpec((NH, tm, LANES), lambda i: (0, i, 0))
    hms = _sds((NH, S, LANES), BF16)
    return pl.pallas_call(
        body, name="fwd_decay", grid=(S // tm,),
        out_shape=(hms, hms, hms, _sds((S, LANES), F32)),
        in_specs=[SMEM_SPEC, _rows(tm, LANES), _rows(tm, AW), _rows(tm, AW), _rows(tm, AW)],
        out_specs=(hm, hm, hm, _rows(tm, LANES)),
        scratch_shapes=[pltpu.VMEM((8, LANES), F32)], compiler_params=_cp(("arbitrary",)),
    )(shift, fgb, qh, kh, vb)


SKIP = -106.0


def _block_loops(first, step, needed, run):
    def both(j):
        return jnp.logical_and(needed(0, j), needed(1, j))

    def walk(heads):
        def go(j):
            run(j, heads)
            return j + step
        return go

    j = lax.while_loop(both, walk((0, 1)), first)
    lax.while_loop(functools.partial(needed, 0), walk((0,)), j)
    lax.while_loop(functools.partial(needed, 1), walk((1,)), j)


def _logit_bound(qg, kg):
    return (2.0 * 1.03 * DH ** 0.5 * jnp.max(jnp.abs(qg)) * jnp.max(jnp.abs(kg))).reshape(1, 1)


def _skip_tables(f, tq):
    return f[0::tq, :NH].T, f[tq - 1::tq, :NH].T


SMEM_SPEC = pl.BlockSpec(memory_space=pltpu.SMEM)


def _causal_rect(rows, cols, col0):
    return (lax.broadcasted_iota(jnp.int32, (rows, cols), 0)
            >= lax.broadcasted_iota(jnp.int32, (rows, cols), 1) + col0)


def _chunk(tq):
    return tq


SHIFT_MAX = 60.0


def _shift(bd):
    return jnp.where(bd <= SHIFT_MAX, 0.5 * bd, 0.0)


def _attn_fwd(qa, ka, va, fs, fe, bd, tq):
    S = qa.shape[1]
    nq = S // tq

    def body(fs_ref, fe_ref, bd_ref, qa_ref, ka_ref, va_ref, o_ref, lser_ref, m_ref, acc_ref):
        pr, i = pl.program_id(0), pl.program_id(1)
        sel_a = _lane((tq, LANES)) < DH
        acc_ref[...] = jnp.zeros_like(acc_ref)

        def logits(j, hh, masked):
            start = pl.multiple_of(j * tq, tq)
            s = lax.dot_general(qa_ref[hh], ka_ref[hh, pl.ds(start, tq), :], NT, preferred_element_type=F32)
            if masked:
                s = jnp.where(_causal_rect(tq, tq, 0), s, NEG)
            return s, va_ref[hh, pl.ds(start, tq), :]

        def shifted_step(j, heads, masked=False):
            for hh in heads:
                s, vb = logits(j, hh, masked)
                acc_ref[hh] += jnp.dot(jnp.exp(s).astype(BF16), vb, preferred_element_type=F32)

        def online_step(j, heads, masked=False):
            for hh in heads:
                s, vb = logits(j, hh, masked)
                m_prev = m_ref[hh]
                m_new = jnp.maximum(m_prev, jnp.max(s, axis=1, keepdims=True))
                p = jnp.exp(s - jnp.tile(m_new, (1, tq // LANES)))
                m_ref[hh] = m_new
                acc_ref[hh] = jnp.exp(m_prev - m_new) * acc_ref[hh] + jnp.dot(p.astype(BF16), vb,
                                                                              preferred_element_type=F32)

        def needed(hh, j):
            top = fs_ref[2 * pr + hh, i] + bd_ref[0, 0]
            return jnp.logical_and(j >= 0, top - fe_ref[2 * pr + hh, jnp.maximum(j, 0)] >= SKIP)

        @pl.when(bd_ref[0, 0] <= SHIFT_MAX)
        def _():
            m_ref[...] = jnp.zeros_like(m_ref)
            shifted_step(i, (0, 1), masked=True)
            _block_loops(i - 1, -1, needed, shifted_step)

        @pl.when(bd_ref[0, 0] > SHIFT_MAX)
        def _():
            m_ref[...] = jnp.full(m_ref.shape, NEG, F32)
            online_step(i, (0, 1), masked=True)
            _block_loops(i - 1, -1, needed, online_step)

        outs, lses = [], []
        for hh in range(2):
            acc = acc_ref[hh]
            row_sum = jnp.broadcast_to(acc[:, DH:DH + 1], (tq, LANES))
            outs.append(acc / row_sum)
            lses.append(m_ref[hh] + jnp.log(row_sum))
        o_ref[...] = jnp.where(sel_a, outs[0], pltpu.roll(outs[1], 64, 1))
        row = lax.broadcasted_iota(jnp.int32, (8, tq), 0)
        lser_ref[0, 0] = jnp.where(row == 0, lses[0].T[0:8, :], lses[1].T[0:8, :])

    return pl.pallas_call(
        body, name="attn_fwd", grid=(NH // 2, nq),
        out_shape=(_sds((S, AW), F32), _sds((NH // 2, nq, 8, tq), F32)),
        in_specs=[SMEM_SPEC, SMEM_SPEC, SMEM_SPEC,
                  pl.BlockSpec((2, tq, LANES), lambda p, i: (p, i, 0)),
                  pl.BlockSpec((2, S, LANES), lambda p, i: (p, 0, 0)),
                  pl.BlockSpec((2, S, LANES), lambda p, i: (p, 0, 0))],
        out_specs=(pl.BlockSpec((tq, LANES), lambda p, i: (i, p)),
                   pl.BlockSpec((1, 1, 8, tq), lambda p, i: (p, i, 0, 0))),
        scratch_shapes=[pltpu.VMEM((2, tq, LANES), F32), pltpu.VMEM((2, tq, LANES), F32)],
        compiler_params=_cp(("parallel", "parallel")),
    )(fs, fe, bd, qa, ka, va)


HALO = 32
CHUNK_ROWS = 64


def _halo_prev(tm):
    return pl.BlockSpec((HALO, CW), lambda i: (jnp.maximum(i * (tm // HALO) - 1, 0), 0))


SUBLANES = 8
SHIFT_ROWS = 24


def _shifted_copies(ext_ref, sh_ref, tm):
    for k in range(1, SUBLANES):
        sh_ref[k - 1, 0:tm + SHIFT_ROWS, :] = ext_ref[k:k + tm + SHIFT_ROWS, :]


def _ext_rows(ext_ref, sh_ref, o):
    k = o % SUBLANES
    if k == 0:
        return ext_ref[o:o + CHUNK_ROWS, :]
    return sh_ref[k - 1, o - k:o - k + CHUNK_ROWS, :]


def _fwd_conv(u0, w32, cb, lng, lnb, beta_c, tm):
    S = u0.shape[0]

    def body(cur_ref, prev_ref, w_ref, cb_ref, lng_ref, lnb_ref, beta_ref, u1_ref, mc_ref, ext_ref, sh_ref):
        i = pl.program_id(0)
        ext_ref[0:HALO, :] = jnp.where(i == 0, 0.0, prev_ref[...])
        ext_ref[HALO:, :] = cur_ref[...]
        _shifted_copies(ext_ref, sh_ref, tm)
        for r0 in range(0, tm, CHUNK_ROWS):
            acc = jnp.zeros((CHUNK_ROWS, CW), F32) + cb_ref[...]
            for j in range(KC):
                acc = acc + w_ref[j:j + 1, :] * _ext_rows(ext_ref, sh_ref, r0 + 2 + j)
            u1_ref[r0:r0 + CHUNK_ROWS, :] = acc
        u1 = u1_ref[...]
        mu = jnp.mean(u1, axis=-1, keepdims=True)
        d = u1 - mu
        rstd = lax.rsqrt(jnp.mean(d * d, axis=-1, keepdims=True) + EPS)
        u2 = d * rstd * lng_ref[...] + lnb_ref[...]
        u3 = u2 * _sigmoid(u2)
        rc = lax.rsqrt(jnp.mean(u3 * u3, axis=-1, keepdims=True) + EPS)
        mc_ref[...] = (u3 * rc * beta_ref[...]).astype(BF16)

    return pl.pallas_call(
        body, name="fwd_conv", grid=(S // tm,),
        out_shape=(_sds((S, CW), F32), _sds((S, CW), BF16)),
        in_specs=[_rows(tm, CW), _halo_prev(tm), _const((HALO, CW)), _const((1, CW)), _const((1, CW)), _const((1, CW)),
                  _const((1, CW))],
        out_specs=(_rows(tm, CW), _rows(tm, CW)),
        scratch_shapes=[pltpu.VMEM((tm + HALO, CW), F32), pltpu.VMEM((SUBLANES - 1, tm + HALO, CW), F32)],
        compiler_params=_cp(("parallel",)),
    )(u0, u0, w32, cb, lng, lnb, beta_c)


def _fwd_out(o_attn, mc, x, mod8, n2g, beta_a, w_out, tm):
    S = x.shape[0]

    def body(o_ref, mc_ref, x_ref, mod_ref, n2g_ref, beta_ref, w_ref, mg_ref, ob_ref, x2_ref, h2_ref):
        ov = o_ref[...]
        ra = lax.rsqrt(jnp.mean(ov * ov, axis=-1, keepdims=True) + EPS)
        ma = (ov * ra * beta_ref[...]).astype(BF16)
        mcv = mc_ref[...]
        mg_ref[:, 0:AW] = ma
        mg_ref[:, AW:D] = mcv
        o = (jnp.dot(ma, w_ref[0:AW, :], preferred_element_type=F32)
             + jnp.dot(mcv, w_ref[AW:D, :], preferred_element_type=F32))
        ob_ref[...] = o.astype(BF16)
        x2 = x_ref[...] + mod_ref[2:3, :] * o
        x2_ref[...] = x2
        r2 = lax.rsqrt(jnp.mean(x2 * x2, axis=-1, keepdims=True) + EPS)
        h2_ref[...] = ((x2 * r2) * (n2g_ref[...] * (1.0 + mod_ref[4:5, :])) + mod_ref[3:4, :]).astype(BF16)

    return pl.pallas_call(
        body, name="fwd_out", grid=(S // tm,),
        out_shape=(_sds((S, D), BF16), _sds((S, D), BF16), _sds((S, D), F32), _sds((S, D), BF16)),
        in_specs=[_rows(tm, AW), _rows(tm, CW), _rows(tm, D), _const((8, D)), _const((1, D)), _const((1, AW)),
                  _const((D, D))],
        out_specs=(_rows(tm, D), _rows(tm, D), _rows(tm, D), _rows(tm, D)),
        compiler_params=_cp(("parallel",)),
    )(o_attn, mc, x, mod8, n2g, beta_a, w_out)


def _fwd_ffn(h2, w1, w2, x2, tgt, mod8, tm):
    S = h2.shape[0]
    nk = w1.shape[0]
    bf = w1.shape[2]

    def body(h2_ref, w1_ref, w2_ref, x2_ref, tgt_ref, mod_ref, r_ref, dy_ref, loss_ref, dg2_ref):
        @pl.when(pl.program_id(0) == 0)
        def _():
            loss_ref[...] = jnp.zeros_like(loss_ref)
            dg2_ref[...] = jnp.zeros_like(dg2_ref)

        f2 = None
        for k in range(nk):
            r = jnp.maximum(jnp.dot(h2_ref[...], w1_ref[k], preferred_element_type=F32), 0.0)
            r_ref[:, k * bf:(k + 1) * bf] = r.astype(BF16)
            part = jnp.dot((r * r).astype(BF16), w2_ref[k], preferred_element_type=F32)
            f2 = part if f2 is None else f2 + part
        e = x2_ref[...] + mod_ref[5:6, :] * f2 - tgt_ref[...]
        dy = e * (1.0 / D)
        dy_ref[...] = dy
        loss_ref[...] += 0.5 * jnp.sum(jnp.sum(e * dy, axis=1, keepdims=True), axis=0, keepdims=True)
        dg2_ref[...] += jnp.sum((dy * f2).reshape(tm // 8, 8, D), axis=0)

    once = pl.Buffered(1)
    return pl.pallas_call(
        body, name="fwd_ffn", grid=(S // tm,),
        out_shape=(_sds((S, DFF), BF16), _sds((S, D), F32), _sds((8, LANES), F32), _sds((8, D), F32)),
        in_specs=[_rows(tm, D), pl.BlockSpec((nk, D, bf), lambda i: (0, 0, 0), pipeline_mode=once),
                  pl.BlockSpec((nk, bf, D), lambda i: (0, 0, 0), pipeline_mode=once), _rows(tm, D), _rows(tm, D),
                  _const((8, D))],
        out_specs=(_rows(tm, DFF), _rows(tm, D), _const((8, LANES)), _const((8, D))),
        compiler_params=_cp(("arbitrary",)),
    )(h2, w1, w2, x2, tgt, mod8)


def _bwd_ffn(dy, mod8, r, w1, w2, tm):
    S = dy.shape[0]
    nk = w1.shape[0]
    bf = w1.shape[2]

    def body(dy_ref, mod_ref, r_ref, w1_ref, w2_ref, df2_ref, df1_ref, dh2_ref):
        df2 = (dy_ref[...] * mod_ref[5:6, :]).astype(BF16)
        df2_ref[...] = df2
        dh2 = None
        for k in range(nk):
            da = lax.dot_general(df2, w2_ref[k], NT, preferred_element_type=F32)
            df1 = (da * (2.0 * r_ref[:, k * bf:(k + 1) * bf].astype(F32))).astype(BF16)
            df1_ref[:, k * bf:(k + 1) * bf] = df1
            part = lax.dot_general(df1, w1_ref[k], NT, preferred_element_type=F32)
            dh2 = part if dh2 is None else dh2 + part
        dh2_ref[...] = dh2

    once = pl.Buffered(1)
    return pl.pallas_call(
        body, name="bwd_ffn", grid=(S // tm,),
        out_shape=(_sds((S, D), BF16), _sds((S, DFF), BF16), _sds((S, D), F32)),
        in_specs=[_rows(tm, D), _const((8, D)), _rows(tm, DFF),
                  pl.BlockSpec((nk, D, bf), lambda i: (0, 0, 0), pipeline_mode=once),
                  pl.BlockSpec((nk, bf, D), lambda i: (0, 0, 0), pipeline_mode=once)],
        out_specs=(_rows(tm, D), _rows(tm, DFF), _rows(tm, D)),
        compiler_params=_cp(("parallel",)),
    )(dy, mod8, r, w1, w2)


def _wgrad(a, b, name, square_a=False, tk=1024, bm=1024, bn=1024):
    S, M = a.shape
    N = b.shape[1]
    bm, bn, tk = min(bm, M), min(bn, N), min(tk, S)

    def body(a_ref, b_ref, o_ref):
        av = a_ref[...]
        if square_a:
            af = av.astype(F32)
            av = (af * af).astype(BF16)
        part = lax.dot_general(av, b_ref[...], TN, preferred_element_type=F32)

        @pl.when(pl.program_id(2) == 0)
        def _():
            o_ref[...] = part

        @pl.when(pl.program_id(2) > 0)
        def _():
            o_ref[...] += part

    return pl.pallas_call(
        body, name=name, grid=(M // bm, N // bn, S // tk), out_shape=_sds((M, N), F32),
        in_specs=[pl.BlockSpec((tk, bm), lambda mi, ni, k: (k, mi)), pl.BlockSpec((tk, bn), lambda mi, ni, k: (k, ni))],
        out_specs=pl.BlockSpec((bm, bn), lambda mi, ni, k: (mi, ni)),
        compiler_params=_cp(("parallel", "parallel", "arbitrary")),
    )(a, b)


def _wgrad_in(h1, pieces, tk=1024):
    S = h1.shape[0]
    tk = min(tk, S)
    widths = [p.shape[1] for p in pieces]
    offs = [sum(widths[:t]) for t in range(len(widths))]

    def body(a_ref, *refs):
        o_ref = refs[-1]

        @pl.when(pl.program_id(0) == 0)
        def _():
            o_ref[...] = jnp.zeros_like(o_ref)

        for b_ref, off, w in zip(refs[:-1], offs, widths):
            o_ref[:, off:off + w] += lax.dot_general(a_ref[...], b_ref[...], TN, preferred_element_type=F32)

    return pl.pallas_call(
        body, name="wgrad_in", grid=(S // tk,), out_shape=_sds((D, NP), F32),
        in_specs=[_rows(tk, D)] + [_rows(tk, w) for w in widths], out_specs=_const((D, NP)),
        compiler_params=_cp(("arbitrary",)),
    )(h1, *pieces)


def _colsum8(t):
    return jnp.sum(t.reshape(t.shape[0] // 8, 8, t.shape[1]), axis=0)


def _bwd_mid(dh2, dy, x2, ob, o_attn, u1, mod8, n2g, beta_a, beta_c, lng, lnb, w_out, e512, tm, tq, ex=None):
    S = dy.shape[0]

    def body(dh2_ref, dy_ref, x2_ref, ob_ref, oa_ref, u1_ref, mod_ref, n2g_ref, ba_ref, bc_ref, lng_ref, lnb_ref, w_ref,
             e_ref, dx2_ref, do_ref, doa_ref, du1_ref, acc_d_ref, acc_h_ref, dr_ref):
        @pl.when(pl.program_id(0) == 0)
        def _():
            acc_d_ref[...] = jnp.zeros_like(acc_d_ref)
            acc_h_ref[...] = jnp.zeros_like(acc_h_ref)

        x2 = x2_ref[...]
        dh2 = dh2_ref[...]
        r2 = lax.rsqrt(jnp.mean(x2 * x2, axis=-1, keepdims=True) + EPS)
        xn2 = x2 * r2
        gain = n2g_ref[...] * (1.0 + mod_ref[4:5, :])
        dxn = dh2 * gain
        dx2 = dy_ref[...] + r2 * (dxn - xn2 * jnp.mean(dxn * xn2, axis=-1, keepdims=True))
        dx2_ref[...] = dx2
        t = dh2 * xn2
        acc_d_ref[0] += _colsum8(dh2)
        acc_d_ref[1] += _colsum8(t * n2g_ref[...])
        acc_d_ref[2] += _colsum8(t * (1.0 + mod_ref[4:5, :]))
        acc_d_ref[3] += _colsum8(dx2 * ob_ref[...].astype(F32))
        do = (dx2 * mod_ref[2:3, :]).astype(BF16)
        do_ref[...] = do
        dma = lax.dot_general(do, w_ref[0:AW, :], NT, preferred_element_type=F32)
        dmc = lax.dot_general(do, w_ref[AW:D, :], NT, preferred_element_type=F32)
        ov = oa_ref[...]
        ra = lax.rsqrt(jnp.mean(ov * ov, axis=-1, keepdims=True) + EPS)
        on = ov * ra
        acc_h_ref[0] += _colsum8(dma * on)
        don = dma * ba_ref[...]
        doa = (ra * (don - on * jnp.mean(don * on, axis=-1, keepdims=True))).astype(BF16)
        doa_ref[...] = doa
        delta_t = _dot2(doa.astype(F32) * ov, e_ref[...]).T
        for p in range(NH // 2):
            dr_ref[p, 0] = delta_t[2 * p:2 * p + 8, :]
        u1 = u1_ref[...]
        mu = jnp.mean(u1, axis=-1, keepdims=True)
        d = u1 - mu
        rstd = lax.rsqrt(jnp.mean(d * d, axis=-1, keepdims=True) + EPS)
        uh = d * rstd
        u2 = uh * lng_ref[...] + lnb_ref[...]
        sg = _sigmoid(u2)
        u3 = u2 * sg
        rc = lax.rsqrt(jnp.mean(u3 * u3, axis=-1, keepdims=True) + EPS)
        u3n = u3 * rc
        acc_h_ref[1] += _colsum8(dmc * u3n)
        du3n = dmc * bc_ref[...]
        du3 = rc * (du3n - u3n * jnp.mean(du3n * u3n, axis=-1, keepdims=True))
        du2 = du3 * (sg * (1.0 + u2 * (1.0 - sg)))
        acc_h_ref[2] += _colsum8(du2 * uh)
        acc_h_ref[3] += _colsum8(du2)
        duh = du2 * lng_ref[...]
        du1_ref[...] = rstd * (duh - jnp.mean(duh, axis=-1, keepdims=True)
                               - uh * jnp.mean(duh * uh, axis=-1, keepdims=True))

    per = tq // tm
    outs = pl.pallas_call(
        _hosted(body, ex, 14, 7, 0, S // tm), name="bwd_mid", grid=(S // tm,),
        out_shape=(_sds((S, D), F32), _sds((S, D), BF16), _sds((S, AW), BF16), _sds((S, CW), F32),
                   _sds((4, 8, D), F32), _sds((4, 8, AW), F32), _sds((NH // 2, S // tq, 8, tq), F32),
                   *(ex.out_shapes if ex else [])),
        in_specs=[_rows(tm, D), _rows(tm, D), _rows(tm, D), _rows(tm, D), _rows(tm, AW), _rows(tm, CW), _const((8, D)),
                  _const((1, D)), _const((1, AW)), _const((1, CW)), _const((1, CW)), _const((1, CW)), _const((D, D)),
                  _const((AW, LANES)), *(ex.in_specs if ex else [])],
        out_specs=(_rows(tm, D), _rows(tm, D), _rows(tm, AW), _rows(tm, CW), _const((4, 8, D)), _const((4, 8, AW)),
                   pl.BlockSpec((NH // 2, 1, 8, tm), lambda i: (0, i // per, 0, i % per)),
                   *(ex.out_specs if ex else [])),
        scratch_shapes=ex.scratch if ex else [],
        compiler_params=_cp(("arbitrary",)),
    )(dh2, dy, x2, ob, o_attn, u1, mod8, n2g, beta_a, beta_c, lng, lnb, w_out, e512, *(ex.srcs if ex else []))
    return outs[:7], list(outs[7:])


def _attn_bwd_dq(qa, ka, v, do, o_attn, lsec, fs, fe, bd, tq):
    S = qa.shape[1]
    nq = S // tq
    tc = _chunk(tq)

    def body(fs_ref, fe_ref, bd_ref, qa_ref, ka_ref, v_ref, do_ref, o_ref, lse_ref, dqa_ref, dr_ref, acc_ref):
        pr, i = pl.program_id(0), pl.program_id(1)
        sel_a = _lane((tq, LANES)) < DH
        dov = do_ref[...]
        prod = dov.astype(F32) * o_ref[...]
        zero = jnp.zeros_like(prod)
        deltas = [jnp.broadcast_to(jnp.sum(jnp.where(sel_a, prod, zero), axis=1, keepdims=True), (tq, LANES)),
                  jnp.broadcast_to(jnp.sum(jnp.where(sel_a, zero, prod), axis=1, keepdims=True), (tq, LANES))]
        zb = jnp.zeros_like(dov)
        dos = [jnp.where(sel_a, dov, zb), jnp.where(sel_a, zb, dov)]
        acc_ref[...] = jnp.zeros_like(acc_ref)

        def kv_step(j, heads, masked=False):
            for c0 in range(0, tq, tc):
                start = pl.multiple_of(j * tq + c0, tc)
                vb = v_ref[pl.ds(start, tc), :]
                for hh in heads:
                    kb = ka_ref[hh, pl.ds(start, tc), :]
                    s = lax.dot_general(qa_ref[hh], kb, NT, preferred_element_type=F32)
                    p = jnp.exp(s - jnp.tile(lse_ref[hh], (1, tc // LANES)))
                    if masked:
                        p = jnp.where(_causal_rect(tq, tc, c0), p, 0.0)
                    dp = lax.dot_general(dos[hh], vb, NT, preferred_element_type=F32)
                    ds = p * (dp - jnp.tile(deltas[hh], (1, tc // LANES)))
                    acc_ref[hh] += jnp.dot(ds.astype(BF16), kb, preferred_element_type=F32)

        def needed(hh, j):
            top = fs_ref[2 * pr + hh, i] + bd_ref[0, 0]
            return jnp.logical_and(j >= 0, top - fe_ref[2 * pr + hh, jnp.maximum(j, 0)] >= SKIP)

        kv_step(i, (0, 1), masked=True)
        _block_loops(i - 1, -1, needed, kv_step)
        dqa_ref[...] = acc_ref[...]
        row = lax.broadcasted_iota(jnp.int32, (8, tq), 0)
        da = deltas[0].T[0:8, :]
        db = deltas[1].T[0:8, :]
        dr_ref[0, 0] = jnp.where(row == 0, da, db)

    return pl.pallas_call(
        body, name="attn_bwd_dq", grid=(NH // 2, nq),
        out_shape=(_sds((NH, S, LANES), F32), _sds((NH // 2, nq, 8, tq), F32)),
        in_specs=[SMEM_SPEC, SMEM_SPEC, SMEM_SPEC,
                  pl.BlockSpec((2, tq, LANES), lambda p, i: (p, i, 0)),
                  pl.BlockSpec((2, S, LANES), lambda p, i: (p, 0, 0)),
                  pl.BlockSpec((S, LANES), lambda p, i: (0, p)),
                  pl.BlockSpec((tq, LANES), lambda p, i: (i, p)),
                  pl.BlockSpec((tq, LANES), lambda p, i: (i, p)),
                  pl.BlockSpec((2, tq, LANES), lambda p, i: (p, i, 0))],
        out_specs=(pl.BlockSpec((2, tq, LANES), lambda p, i: (p, i, 0)),
                   pl.BlockSpec((1, 1, 8, tq), lambda p, i: (p, i, 0, 0))),
        scratch_shapes=[pltpu.VMEM((2, tq, LANES), F32)],
        compiler_params=_cp(("parallel", "parallel")),
    )(fs, fe, bd, qa, ka, v, do, o_attn, lsec)


def _attn_bwd_dkv(qa, ka, v, do, lser, dr, fs, fe, bd, tq):
    S = qa.shape[1]
    nq = S // tq
    tc = _chunk(tq)

    def body(fs_ref, fe_ref, bd_ref, ka_ref, v_ref, qa_ref, do_ref, lse_ref, dr_ref, dka_ref, dv_ref, acck_ref,
             accv_ref):
        pr, j = pl.program_id(0), pl.program_id(1)
        sel_a = _lane((tq, LANES)) < DH
        vv = v_ref[...]
        zb = jnp.zeros_like(vv)
        vs = [jnp.where(sel_a, vv, zb), jnp.where(sel_a, zb, vv)]
        acck_ref[...] = jnp.zeros_like(acck_ref)
        accv_ref[...] = jnp.zeros_like(accv_ref)

        def q_step(i, heads, masked=False):
            lse8 = lse_ref[0, i]
            dr8 = dr_ref[0, i]
            for c0 in range(0, tq, tc):
                start = pl.multiple_of(i * tq + c0, tc)
                dob = do_ref[pl.ds(start, tc), :]
                for hh in heads:
                    qb = qa_ref[hh, pl.ds(start, tc), :]
                    st = lax.dot_general(ka_ref[hh], qb, NT, preferred_element_type=F32)
                    pt = jnp.exp(st - lse8[hh:hh + 1, c0:c0 + tc])
                    if masked:
                        keep = (lax.broadcasted_iota(jnp.int32, (tq, tc), 0)
                                <= lax.broadcasted_iota(jnp.int32, (tq, tc), 1) + c0)
                        pt = jnp.where(keep, pt, 0.0)
                    accv_ref[hh] += jnp.dot(pt.astype(BF16), dob, preferred_element_type=F32)
                    dpt = lax.dot_general(vs[hh], dob, NT, preferred_element_type=F32)
                    dst = pt * (dpt - dr8[hh:hh + 1, c0:c0 + tc])
                    acck_ref[hh] += jnp.dot(dst.astype(BF16), qb, preferred_element_type=F32)

        def needed(hh, i):
            top = fs_ref[2 * pr + hh, jnp.minimum(i, nq - 1)] + bd_ref[0, 0]
            return jnp.logical_and(i < nq, top - fe_ref[2 * pr + hh, j] >= SKIP)

        q_step(j, (0, 1), masked=True)
        _block_loops(j + 1, 1, needed, q_step)
        dka_ref[...] = acck_ref[...]
        dv_ref[...] = jnp.where(sel_a, accv_ref[0], accv_ref[1]).astype(BF16)

    return pl.pallas_call(
        body, name="attn_bwd_dkv", grid=(NH // 2, nq),
        out_shape=(_sds((NH, S, LANES), F32), _sds((S, AW), BF16)),
        in_specs=[SMEM_SPEC, SMEM_SPEC, SMEM_SPEC,
                  pl.BlockSpec((2, tq, LANES), lambda p, j: (p, j, 0)),
                  pl.BlockSpec((tq, LANES), lambda p, j: (j, p)),
                  pl.BlockSpec((2, S, LANES), lambda p, j: (p, 0, 0)),
                  pl.BlockSpec((S, LANES), lambda p, j: (0, p)),
                  pl.BlockSpec((1, nq, 8, tq), lambda p, j: (p, 0, 0, 0)),
                  pl.BlockSpec((1, nq, 8, tq), lambda p, j: (p, 0, 0, 0))],
        out_specs=(pl.BlockSpec((2, tq, LANES), lambda p, j: (p, j, 0)),
                   pl.BlockSpec((tq, LANES), lambda p, j: (j, p))),
        scratch_shapes=[pltpu.VMEM((2, tq, LANES), F32), pltpu.VMEM((2, tq, LANES), F32)],
        compiler_params=_cp(("parallel", "parallel")),
    )(fs, fe, bd, ka, v, qa, do, lser, dr)


def _attn_bwd(qa, ka, v, do, lser, dr, fs, fe, bd, tq):
    S = qa.shape[1]
    nq = S // tq

    def body(fs_ref, fe_ref, bd_ref, ka_ref, v_ref, qa_ref, do_ref, lse_ref, dr_ref, dqa_hbm, dka_ref, dv_ref,
             accq_ref, acck_ref, accv_ref, out_sem):
        pr, j = pl.program_id(0), pl.program_id(1)
        sel_a = _lane((tq, LANES)) < DH
        vv = v_ref[...]
        zb = jnp.zeros_like(vv)
        vs = [jnp.where(sel_a, vv, zb), jnp.where(sel_a, zb, vv)]
        acck_ref[...] = jnp.zeros_like(acck_ref)
        accv_ref[...] = jnp.zeros_like(accv_ref)

        @pl.when(j == 0)
        def _():
            accq_ref[...] = jnp.zeros_like(accq_ref)

        def q_step(i, heads, masked=False):
            start = pl.multiple_of(i * tq, tq)
            dob = do_ref[pl.ds(start, tq), :]
            lse8 = lse_ref[0, i]
            dr8 = dr_ref[0, i]
            for hh in heads:
                qb = qa_ref[hh, pl.ds(start, tq), :]
                kb = ka_ref[hh]
                st = lax.dot_general(kb, qb, NT, preferred_element_type=F32)
                pt = jnp.exp(st - lse8[hh:hh + 1, :])
                if masked:
                    keep = (lax.broadcasted_iota(jnp.int32, (tq, tq), 0)
                            <= lax.broadcasted_iota(jnp.int32, (tq, tq), 1))
                    pt = jnp.where(keep, pt, 0.0)
                accv_ref[hh] += jnp.dot(pt.astype(BF16), dob, preferred_element_type=F32)
                dpt = lax.dot_general(vs[hh], dob, NT, preferred_element_type=F32)
                dst = (pt * (dpt - dr8[hh:hh + 1, :])).astype(BF16)
                acck_ref[hh] += jnp.dot(dst, qb, preferred_element_type=F32)
                accq_ref[hh, pl.ds(start, tq), :] += lax.dot_general(dst, kb, TN, preferred_element_type=F32)

        def needed(hh, i):
            top = fs_ref[2 * pr + hh, jnp.minimum(i, nq - 1)] + bd_ref[0, 0]
            return jnp.logical_and(i < nq, top - fe_ref[2 * pr + hh, j] >= SKIP)

        q_step(j, (0, 1), masked=True)
        _block_loops(j + 1, 1, needed, q_step)
        dka_ref[...] = acck_ref[...]
        dv_ref[...] = jnp.where(sel_a, accv_ref[0], accv_ref[1]).astype(BF16)

        @pl.when(j == nq - 1)
        def _():
            out = pltpu.make_async_copy(accq_ref, dqa_hbm.at[pl.ds(2 * pr, 2)], out_sem)
            out.start()
            out.wait()

    once = pl.Buffered(1)
    return pl.pallas_call(
        body, name="attn_bwd", grid=(NH // 2, nq),
        out_shape=(_sds((NH, S, LANES), F32), _sds((NH, S, LANES), F32), _sds((S, AW), BF16)),
        in_specs=[SMEM_SPEC, SMEM_SPEC, SMEM_SPEC,
                  pl.BlockSpec((2, tq, LANES), lambda p, j: (p, j, 0)),
                  pl.BlockSpec((tq, LANES), lambda p, j: (j, p)),
                  pl.BlockSpec((2, S, LANES), lambda p, j: (p, 0, 0), pipeline_mode=once),
                  pl.BlockSpec((S, LANES), lambda p, j: (0, p), pipeline_mode=once),
                  pl.BlockSpec((1, nq, 8, tq), lambda p, j: (p, 0, 0, 0)),
                  pl.BlockSpec((1, nq, 8, tq), lambda p, j: (p, 0, 0, 0))],
        out_specs=(pl.BlockSpec(memory_space=pl.ANY),
                   pl.BlockSpec((2, tq, LANES), lambda p, j: (p, j, 0)),
                   pl.BlockSpec((tq, LANES), lambda p, j: (j, p))),
        scratch_shapes=[pltpu.VMEM((2, S, LANES), F32), pltpu.VMEM((2, tq, LANES), F32),
                        pltpu.VMEM((2, tq, LANES), F32), pltpu.SemaphoreType.DMA],
        compiler_params=_cp(("arbitrary", "arbitrary")),
    )(fs, fe, bd, ka, v, qa, do, lser, dr)


def _bwd_conv(du1, u0, alin, agate, w32, tm, ex=None):
    S = du1.shape[0]
    nt = S // tm

    def body(du_ref, dun_ref, u0_ref, u0p_ref, alin_ref, agate_ref, w_ref,
             dalin_ref, dagate_ref, dw_ref, db_ref, extd_ref, extu_ref, du0_ref, shd_ref, shu_ref):
        i = pl.program_id(0)

        @pl.when(i == 0)
        def _():
            dw_ref[...] = jnp.zeros_like(dw_ref)
            db_ref[...] = jnp.zeros_like(db_ref)

        extd_ref[0:tm, :] = du_ref[...]
        extd_ref[tm:, :] = jnp.where(i == nt - 1, 0.0, dun_ref[...])
        extu_ref[0:HALO, :] = jnp.where(i == 0, 0.0, u0p_ref[...])
        extu_ref[HALO:, :] = u0_ref[...]
        _shifted_copies(extd_ref, shd_ref, tm)
        _shifted_copies(extu_ref, shu_ref, tm)
        db_ref[...] += _colsum8(du_ref[...])
        for r0 in range(0, tm, CHUNK_ROWS):
            duc = du_ref[r0:r0 + CHUNK_ROWS, :]
            acc = jnp.zeros((CHUNK_ROWS, CW), F32)
            for j in range(KC):
                acc = acc + w_ref[j:j + 1, :] * _ext_rows(extd_ref, shd_ref, r0 + 30 - j)
                dw_ref[j] += _colsum8(duc * _ext_rows(extu_ref, shu_ref, r0 + 2 + j))
            du0_ref[r0:r0 + CHUNK_ROWS, :] = acc
        du0 = du0_ref[...]
        al = alin_ref[...].astype(F32)
        sg = _sigmoid(agate_ref[...].astype(F32))
        dalin_ref[...] = (du0 * sg).astype(BF16)
        dagate_ref[...] = (du0 * al * sg * (1.0 - sg)).astype(BF16)

    nxt = pl.BlockSpec((HALO, CW), lambda i: (jnp.minimum((i + 1) * (tm // HALO), S // HALO - 1), 0))
    outs = pl.pallas_call(
        _hosted(body, ex, 7, 4, 5, nt), name="bwd_conv", grid=(nt,),
        out_shape=(_sds((S, CW), BF16), _sds((S, CW), BF16), _sds((HALO, 8, CW), F32), _sds((8, CW), F32),
                   *(ex.out_shapes if ex else [])),
        in_specs=[_rows(tm, CW), nxt, _rows(tm, CW), _halo_prev(tm), _rows(tm, CW), _rows(tm, CW), _const((HALO, CW)),
                  *(ex.in_specs if ex else [])],
        out_specs=(_rows(tm, CW), _rows(tm, CW), _const((HALO, 8, CW)), _const((8, CW)), *(ex.out_specs if ex else [])),
        scratch_shapes=[pltpu.VMEM((tm + HALO, CW), F32), pltpu.VMEM((tm + HALO, CW), F32), pltpu.VMEM((tm, CW), F32),
                        pltpu.VMEM((SUBLANES - 1, tm + HALO, CW), F32), pltpu.VMEM((SUBLANES - 1, tm + HALO, CW), F32),
                        *(ex.scratch if ex else [])],
        compiler_params=_cp(("arbitrary",)),
    )(du1, du1, u0, u0, alin, agate, w32, *(ex.srcs if ex else []))
    return outs[:4], list(outs[4:])


def _bwd_qk(dqa, dka, qn, kn, rq, rk, fgb, qg512, kg512, e512, et512, tm):
    S = qn.shape[0]
    nt = S // tm

    def body(dqa_ref, dka_ref, qn_ref, kn_ref, rq_ref, rk_ref, fgb_ref, qg_ref, kg_ref, e_ref, et_ref,
             dq_ref, dk_ref, dfg_ref, accg_ref, accb_ref, carry_ref):
        @pl.when(pl.program_id(0) == 0)
        def _():
            carry_ref[...] = jnp.zeros_like(carry_ref)
            accg_ref[...] = jnp.zeros_like(accg_ref)
            accb_ref[...] = jnp.zeros_like(accb_ref)

        lane = _lane((tm, LANES))
        sel_a = lane < DH
        df = jnp.zeros((tm, LANES), F32)
        for h in range(NH):
            col = dqa_ref[h][:, 64:65] - dka_ref[h][:, 67:68]
            df = jnp.where(lane == h, col, df)
        tri = (lax.broadcasted_iota(jnp.int32, (tm, tm), 0) <= lax.broadcasted_iota(jnp.int32, (tm, tm), 1)
               ).astype(F32).astype(BF16)
        dlf = _dot3(tri, df) + carry_ref[0:1, :]
        carry_ref[...] = jnp.broadcast_to(dlf[0:1, :], carry_ref.shape)
        dfg = jnp.where(lane < NH, dlf * _sigmoid(-fgb_ref[...]), 0.0)
        dfg_ref[...] = dfg.astype(BF16)
        accb_ref[...] += _colsum8(dfg)

        def norm_bwd(src_ref, n_ref, r_ref, g_ref, scale, slot):
            pairs = []
            for p in range(NH // 2):
                b = pltpu.roll(src_ref[2 * p + 1], 64, 1)
                pairs.append(jnp.where(sel_a, src_ref[2 * p], b))
            dh = jnp.concatenate(pairs, axis=1) * scale
            tn = n_ref[...].astype(F32)
            accg_ref[slot] += _colsum8(dh * tn)
            dn = dh * g_ref[...]
            mean = _dot2(dn * tn, e_ref[...]) * (1.0 / DH)
            corr = _dot2(mean, et_ref[...])
            rf = _dot2(r_ref[...], et_ref[...])
            return (rf * (dn - tn * corr)).astype(BF16)

        dq_ref[...] = norm_bwd(dqa_ref, qn_ref, rq_ref, qg_ref, DH ** -0.5, 0)
        dk_ref[...] = norm_bwd(dka_ref, kn_ref, rk_ref, kg_ref, 1.0, 1)

    rev = lambda n: pl.BlockSpec((tm, n), lambda i: (nt - 1 - i, 0))
    hm = pl.BlockSpec((NH, tm, LANES), lambda i: (0, nt - 1 - i, 0))
    dq, dk, dfg, accg, accb = pl.pallas_call(
        body, name="bwd_qk", grid=(nt,),
        out_shape=(_sds((S, AW), BF16), _sds((S, AW), BF16), _sds((S, LANES), BF16), _sds((2, 8, AW), F32),
                   _sds((8, LANES), F32)),
        in_specs=[hm, hm, rev(AW), rev(AW), rev(LANES), rev(LANES), rev(LANES), _const((1, AW)), _const((1, AW)),
                  _const((AW, LANES)), _const((LANES, AW))],
        out_specs=(rev(AW), rev(AW), rev(LANES), _const((2, 8, AW)), _const((8, LANES))),
        scratch_shapes=[pltpu.VMEM((8, LANES), F32)], compiler_params=_cp(("arbitrary",)),
    )(dqa, dka, qn, kn, rq, rk, fgb, qg512, kg512, e512, et512)
    return dq, dk, dfg, accg, accb


def _bwd_in(dq, dk, dv, dalin, dagate, dfg, w_in_p, x, dx2, mod8, n1g, tm, ex=None):
    S = x.shape[0]

    def body(dq_ref, dk_ref, dv_ref, dal_ref, dag_ref, dfg_ref, w_ref, x_ref, dx2_ref, mod_ref, n1g_ref,
             dx_ref, acc_ref):
        @pl.when(pl.program_id(0) == 0)
        def _():
            acc_ref[...] = jnp.zeros_like(acc_ref)

        def part(ref, a, b):
            return lax.dot_general(ref[...], w_ref[:, a:b], NT, preferred_element_type=F32)

        dh = (part(dq_ref, 0, 512) + part(dk_ref, 512, 1024) + part(dv_ref, 1024, 1536) + part(dal_ref, 1536, 2048)
              + part(dag_ref, 2048, 2560) + part(dfg_ref, 2560, NP))
        xv = x_ref[...]
        r1 = lax.rsqrt(jnp.mean(xv * xv, axis=-1, keepdims=True) + EPS)
        xn = xv * r1
        gain = n1g_ref[...] * (1.0 + mod_ref[1:2, :])
        t = dh * xn
        acc_ref[0] += _colsum8(dh)
        acc_ref[1] += _colsum8(t * n1g_ref[...])
        acc_ref[2] += _colsum8(t * (1.0 + mod_ref[1:2, :]))
        dxn = dh * gain
        dx_ref[...] = dx2_ref[...] + r1 * (dxn - xn * jnp.mean(dxn * xn, axis=-1, keepdims=True))

    outs = pl.pallas_call(
        _hosted(body, ex, 11, 2, 0, S // tm), name="bwd_in", grid=(S // tm,),
        out_shape=(_sds((S, D), F32), _sds((3, 8, D), F32), *(ex.out_shapes if ex else [])),
        in_specs=[_rows(tm, AW), _rows(tm, AW), _rows(tm, AW), _rows(tm, CW), _rows(tm, CW), _rows(tm, LANES),
                  _const((D, NP)), _rows(tm, D), _rows(tm, D), _const((8, D)), _const((1, D)),
                  *(ex.in_specs if ex else [])],
        out_specs=(_rows(tm, D), _const((3, 8, D)), *(ex.out_specs if ex else [])),
        scratch_shapes=ex.scratch if ex else [],
        compiler_params=_cp(("arbitrary",)),
    )(dq, dk, dv, dalin, dagate, dfg, w_in_p, x, dx2, mod8, n1g, *(ex.srcs if ex else []))
    return outs[:2], list(outs[2:])


def _adam(w, g, m, v):
    m_new = B1 * m + (1.0 - B1) * g
    v_new = B2 * v + (1.0 - B2) * (g * g)
    m_hat = m_new / (1.0 - B1 ** STEP)
    v_hat = v_new / (1.0 - B2 ** STEP)
    delta = -LR * (m_hat / (jnp.sqrt(v_hat) + AEPS) + WD * w)
    return delta, m_new, v_new


def _reduce_adamw(slots, w, m, v, name, tr=256):
    ns, R, C = slots.shape
    tr = tr if R % tr == 0 else R

    def body(s_ref, w_ref, m_ref, v_ref, g_ref, d_ref, mo_ref, vo_ref):
        g = s_ref[0].astype(F32)
        for k in range(1, ns):
            g = g + s_ref[k].astype(F32)
        g_ref[...] = g
        d_ref[...], mo_ref[...], vo_ref[...] = _adam(w_ref[...], g, m_ref[...], v_ref[...])

    blk = pl.BlockSpec((tr, C), lambda i: (i, 0))
    return pl.pallas_call(
        body, name=name, grid=(R // tr,), out_shape=tuple(_sds((R, C), F32) for _ in range(4)),
        in_specs=[pl.BlockSpec((ns, tr, C), lambda i: (0, i, 0)), blk, blk, blk], out_specs=(blk, blk, blk, blk),
        compiler_params=_cp(("parallel",)),
    )(slots, w, m, v)


def _pair_adamw(slots, w, m, v, name, tr=256):
    ns, R, C = slots.shape
    tr = tr if R % tr == 0 else R
    nt = R // tr

    def body(s_ref, w_ref, m_ref, v_ref, g_ref, d_ref, mo_ref, vo_ref, mine_ref, theirs_ref, send_sems, recv_sems):
        i = pl.program_id(0)
        part = s_ref[0].astype(F32)
        for k in range(1, ns):
            part = part + s_ref[k].astype(F32)
        mine_ref[i] = part
        swap = pltpu.make_async_remote_copy(
            src_ref=mine_ref.at[i], dst_ref=theirs_ref.at[i], send_sem=send_sems.at[i], recv_sem=recv_sems.at[i],
            device_id=(lax.axis_index("x"), lax.axis_index("y"), 1 - lax.axis_index("c")),
            device_id_type=pl.DeviceIdType.MESH)
        swap.start()
        swap.wait()
        g = part + theirs_ref[i]
        g_ref[...] = g
        d_ref[...], mo_ref[...], vo_ref[...] = _adam(w_ref[...], g, m_ref[...], v_ref[...])

    blk = pl.BlockSpec((tr, C), lambda i: (i, 0))
    return pl.pallas_call(
        body, name=name, grid=(nt,), out_shape=tuple(_sds((R, C), F32) for _ in range(4)),
        in_specs=[pl.BlockSpec((ns, tr, C), lambda i: (0, i, 0)), blk, blk, blk], out_specs=(blk, blk, blk, blk),
        scratch_shapes=[pltpu.VMEM((nt, tr, C), F32), pltpu.VMEM((nt, tr, C), F32),
                        pltpu.SemaphoreType.DMA((nt,)), pltpu.SemaphoreType.DMA((nt,))],
        compiler_params=_cp(("arbitrary",)),
    )(slots, w, m, v)


def _ada_adamw(sct, dmod, w, m, v):
    R, C = w.shape
    tr, bc = 256, 512

    def body(sct_ref, dm_ref, w_ref, m_ref, v_ref, g_ref, d_ref, mo_ref, vo_ref):
        g = sct_ref[:, 0:1] * dm_ref[0:1, :]
        for b in range(1, N_DEV):
            g = g + sct_ref[:, b:b + 1] * dm_ref[b:b + 1, :]
        g_ref[...] = g
        d_ref[...], mo_ref[...], vo_ref[...] = _adam(w_ref[...], g, m_ref[...], v_ref[...])

    blk = pl.BlockSpec((tr, bc), lambda i, j: (i, j))
    return pl.pallas_call(
        body, name="ada_adamw", grid=(R // tr, C // bc), out_shape=tuple(_sds((R, C), F32) for _ in range(4)),
        in_specs=[pl.BlockSpec((tr, N_DEV), lambda i, j: (i, 0)), pl.BlockSpec((N_DEV, bc), lambda i, j: (0, j)),
                  blk, blk, blk],
        out_specs=(blk, blk, blk, blk), compiler_params=_cp(("parallel", "parallel")),
    )(sct, dmod, w, m, v)


def _small_reduce(slots, fold):
    def body(s_ref, f_ref, o_ref):
        tot = s_ref[0:1, :]
        for k in range(1, N_DEV):
            tot = tot + s_ref[k:k + 1, :]
        o_ref[:, 0:6144] = tot[:, 0:6144]
        o_ref[:, 6144:7168] = tot[:, 6144:7168]
        for t, src in enumerate((8192, 8704)):
            v8 = jnp.broadcast_to(tot[:, src:src + AW], (8, AW))
            o_ref[:, 7168 + t * LANES:7168 + (t + 1) * LANES] = jnp.dot(
                v8, f_ref[...], precision=HI, preferred_element_type=F32)[0:1, :]
        o_ref[:, 7424:7552] = tot[:, 9216:9344]
        o_ref[:, 7552:10112] = tot[:, 9344:11904]
        o_ref[:, 10112:SMALL_OUT] = tot[:, 7168:8192]

    return pl.pallas_call(
        body, name="small_reduce", out_shape=_sds((1, SMALL_OUT), F32),
        in_specs=[pl.BlockSpec(memory_space=pltpu.VMEM), pl.BlockSpec(memory_space=pltpu.VMEM)],
        out_specs=pl.BlockSpec(memory_space=pltpu.VMEM),
    )(slots, fold)


def _perm_in(w):
    pad = jnp.zeros((w.shape[0], NP - 2568), w.dtype)
    return jnp.concatenate([w[:, :1536], w[:, 1544:2568], w[:, 1536:1544], pad], axis=1)


def _pad_lanes(vec, n=LANES):
    return jnp.pad(vec, ((0, 0), (0, n - vec.shape[1])))


def kernel(x, c, w_ada, b_ada, norm1_g, w_in, q_norm_g, k_norm_g, b_f, conv_w, conv_b, conv_ln_g, conv_ln_b, beta_attn, beta_conv, w_out, norm2_g, w_ff1, w_ff2, loss_target, m_w_ada, m_b_ada, m_norm1_g, m_w_in, m_q_norm_g, m_k_norm_g, m_b_f, m_conv_w, m_conv_b, m_conv_ln_g, m_conv_ln_b, m_beta_attn, m_beta_conv, m_w_out, m_norm2_g, m_w_ff1, m_w_ff2, v_w_ada, v_b_ada, v_norm1_g, v_w_in, v_q_norm_g, v_k_norm_g, v_b_f, v_conv_w, v_conv_b, v_conv_ln_g, v_conv_ln_b, v_beta_attn, v_beta_conv, v_w_out, v_norm2_g, v_w_ff1, v_w_ff2):
    S = x.shape[1]
    tm = min(256, S)
    tq = min(512, S // 2)
    xs, tgt = x[0], loss_target[0]
    chip = 2 * lax.axis_index("x") + lax.axis_index("y")
    e512, et512 = _head_sum_mats()

    conv_w32 = jnp.pad(conv_w[0], ((0, 1), (0, 0)))
    c_all, g_in = _exchange([(c, "bcast8"), (w_in[0].astype(BF16), "chip4")], "gather_in")
    later_weights = _Exchange([(w_out[0].astype(BF16), "chip4"), (w_ff1[0].astype(BF16), "chip4"),
                               (w_ff2[0].astype(BF16), "chip4"), (conv_w32, "chip4")])
    c_all = c_all.reshape(N_DEV, D)
    w_in_p = _perm_in(jnp.transpose(g_in, (1, 0, 2)).reshape(D, 2568))

    b_shard = lax.dynamic_slice(b_ada, (0, chip * 1536), (1, 1536))
    mod_rows, sc_all = _mod_shard(c_all, w_ada[0], b_shard)
    (mod_slots,) = _exchange([(mod_rows.reshape(N_DEV, 1, 1536), "all8")], "scatter_mod")
    mod = mod_slots.reshape(4, 2, 1536)[:, 0, :].reshape(6, D)
    mod8 = jnp.pad(mod, ((0, 2), (0, 0)))

    qg512 = jnp.tile(q_norm_g, (1, NH))
    kg512 = jnp.tile(k_norm_g, (1, NH))
    bf128 = _pad_lanes(b_f)

    (h1, qh, kh, vb, qn, kn, rq, rk, fgb, alin, agate, u0), (g_out, g_ff1, g_ff2, g_cw) = _fwd_in(
        xs, mod8, norm1_g, w_in_p, e512, et512, qg512, kg512, bf128, tm, ex=later_weights)
    w_out_f = g_out.reshape(D, D)
    w1, w2 = g_ff1, g_ff2
    cw32 = jnp.transpose(g_cw, (1, 0, 2)).reshape(HALO, CW)
    bd = _logit_bound(q_norm_g, k_norm_g)
    qa, ka, va, fcum = _fwd_decay(fgb, qh, kh, vb, _shift(bd), tm)
    fs, fe = _skip_tables(fcum, tq)
    o_attn, lser = _attn_fwd(qa, ka, va, fs, fe, bd, tq)
    u1, mc = _fwd_conv(u0, cw32, conv_b, conv_ln_g, conv_ln_b, beta_conv, tm)
    merged, ob, x2, h2 = _fwd_out(o_attn, mc, xs, mod8, norm2_g, beta_attn, w_out_f, tm)
    tf = min(512, S)
    r, dy, loss8, dg2 = _fwd_ffn(h2, w1, w2, x2, tgt, mod8, tf)

    df2, df1, dh2 = _bwd_ffn(dy, mod8, r, w1, w2, tf)
    gw_ff2 = _wgrad(r, df2, "wgrad_ff2", square_a=True)
    gw_ff1 = _wgrad(h2, df1, "wgrad_ff1")
    ff_grads = _Exchange([(jnp.transpose(gw_ff1.reshape(D, 4, D), (1, 0, 2)).astype(BF16), "chip4p"),
                          (gw_ff2.reshape(4, D, D).astype(BF16), "chip4p")])
    (dx2, do, doa, du1, acc_d, acc_h, dr), (p_ff1, p_ff2) = _bwd_mid(
        dh2, dy, x2, ob, o_attn, u1, mod8, norm2_g, beta_attn, beta_conv, conv_ln_g, conv_ln_b, w_out_f, e512, tm, tq,
        ex=ff_grads)
    gw_out = _wgrad(merged, do, "wgrad_out")
    dqa, dka, dv = _attn_bwd(qa, ka, vb, doa, lser, dr, fs, fe, bd, tq)
    out_grads = _Exchange([(gw_out.reshape(4, 256, D).astype(BF16), "chip4p")])
    (dalin, dagate, dcw, dcb), (p_out,) = _bwd_conv(du1, u0, alin, agate, cw32, tm, ex=out_grads)
    dq, dk, dfg, accg, accb = _bwd_qk(dqa, dka, qn, kn, rq, rk, fgb, qg512, kg512, e512, et512, tm)
    gw_in_p = _wgrad_in(h1, [dq, dk, dv, dalin, dagate, dfg])
    gw_in = jnp.concatenate([gw_in_p[:, :1536], gw_in_p[:, 2560:2568], gw_in_p[:, 1536:2560]], axis=1)
    s8 = lambda a: jnp.sum(a, axis=-2)
    gcw = s8(dcw)
    in_grads = _Exchange([(jnp.transpose(gw_in.reshape(D, 4, 642), (1, 0, 2)).astype(BF16), "chip4p"),
                          (jnp.transpose(gcw.reshape(HALO, 4, LANES), (1, 0, 2)), "chip4p")])
    (grad_x, acc1), (p_in, p_cw) = _bwd_in(dq, dk, dv, dalin, dagate, dfg, w_in_p, xs, dx2, mod8, norm1_g, tm,
                                           ex=in_grads)

    a1, ad, ah = s8(acc1), s8(acc_d), s8(acc_h)
    small = jnp.concatenate(
        [a1[0], a1[1], ad[3], ad[0], ad[1], s8(dg2),
         a1[2], ad[2], s8(accg).reshape(-1), s8(accb), s8(dcb), ah[2], ah[3], ah[0], ah[1]]).reshape(1, SMALL_IN)
    (small_s,) = _exchange([(small, "bcast8")], "gather_small")
    small_s = small_s.reshape(N_DEV, SMALL_IN)

    g_in_, d_in, nm_in, nv_in = _pair_adamw(p_in, w_in[0], m_w_in[0], v_w_in[0], "adamw_in")
    g_out_, d_out, nm_out, nv_out = _pair_adamw(p_out, w_out[0], m_w_out[0], v_w_out[0], "adamw_out")
    g_f1, d_f1, nm_f1, nv_f1 = _pair_adamw(p_ff1, w_ff1[0], m_w_ff1[0], v_w_ff1[0], "adamw_ff1")
    g_f2, d_f2, nm_f2, nv_f2 = _pair_adamw(p_ff2, w_ff2[0], m_w_ff2[0], v_w_ff2[0], "adamw_ff2")
    pad_row = lambda a, fill: jnp.pad(a[0], ((0, 1), (0, 0)), constant_values=fill)
    g_cw_, d_cw, nm_cw, nv_cw = (a[:KC] for a in _pair_adamw(
        p_cw, pad_row(conv_w, 0.0), pad_row(m_conv_w, 0.0), pad_row(v_conv_w, 1.0), "adamw_conv_w"))
    dmod_shard = lax.dynamic_slice(small_s[:, :6 * D], (0, chip * 1536), (N_DEV, 1536))
    g_ada, d_ada, nm_ada, nv_ada = _ada_adamw(sc_all.T, dmod_shard, w_ada[0], m_w_ada[0], v_w_ada[0])

    fold = np.zeros((AW, LANES), np.float32)
    fold[np.arange(AW), np.arange(AW) % DH] = 1.0
    g_small = _small_reduce(small_s, jnp.asarray(fold))
    smalls = [b_ada, norm1_g, q_norm_g, k_norm_g, b_f, conv_b, conv_ln_g, conv_ln_b, beta_attn, beta_conv, norm2_g]
    m_smalls = [m_b_ada, m_norm1_g, m_q_norm_g, m_k_norm_g, m_b_f, m_conv_b, m_conv_ln_g, m_conv_ln_b, m_beta_attn,
                m_beta_conv, m_norm2_g]
    v_smalls = [v_b_ada, v_norm1_g, v_q_norm_g, v_k_norm_g, v_b_f, v_conv_b, v_conv_ln_g, v_conv_ln_b, v_beta_attn,
                v_beta_conv, v_norm2_g]
    widths = [a.shape[1] for a in smalls]
    padded = [-(-n // LANES) * LANES for n in widths]
    pack = lambda arrs, fill: jnp.concatenate(
        [jnp.pad(a, ((0, 0), (0, p - a.shape[1])), constant_values=fill) for a, p in zip(arrs, padded)], axis=1)
    outs_small = _reduce_adamw(g_small.reshape(1, 1, SMALL_OUT), pack(smalls, 0.0), pack(m_smalls, 0.0),
                               pack(v_smalls, 1.0), "adamw_small")
    offs = np.concatenate([[0], np.cumsum(padded)])

    def unpack(a):
        return [a[:, int(o):int(o) + n] for o, n in zip(offs[:-1], widths)]

    gs, ds, ms, vs = (unpack(a) for a in outs_small)

    loss = lax.psum(loss8[0, 0], ("x", "y", "c"))
    big = {"w_ada": (g_ada, d_ada, nm_ada, nv_ada), "w_in": (g_in_, d_in, nm_in, nv_in),
           "conv_w": (g_cw_, d_cw, nm_cw, nv_cw), "w_out": (g_out_, d_out, nm_out, nv_out),
           "w_ff1": (g_f1, d_f1, nm_f1, nv_f1), "w_ff2": (g_f2, d_f2, nm_f2, nv_f2)}
    small_names = ["b_ada", "norm1_g", "q_norm_g", "k_norm_g", "b_f", "conv_b", "conv_ln_g", "conv_ln_b", "beta_attn",
                   "beta_conv", "norm2_g"]
    order = ["w_ada", "b_ada", "norm1_g", "w_in", "q_norm_g", "k_norm_g", "b_f", "conv_w", "conv_b", "conv_ln_g",
             "conv_ln_b", "beta_attn", "beta_conv", "w_out", "norm2_g", "w_ff1", "w_ff2"]

    def leaf(name, which):
        if name in big:
            return big[name][which][None]
        return (gs, ds, ms, vs)[which][small_names.index(name)]

    return (loss, grad_x[None], *[leaf(n, 0) for n in order], *[leaf(n, 1) for n in order],
            *[leaf(n, 2) for n in order], *[leaf(n, 3) for n in order])
```

```python
import functools

import numpy as np
import jax
import jax.numpy as jnp
from jax import lax
from jax.experimental import pallas as pl
from jax.experimental.pallas import tpu as pltpu

F32, BF16 = jnp.float32, jnp.bfloat16
HI = lax.Precision.HIGHEST
D = 1024
AW = 512
CW = 512
NH = 8
DH = 64
KC = 31
DFF = 4096
NP = 2688
EPS = 1e-6
NEG = -1e30
LANES = 128
VMEM_LIMIT = 56 * 2**20
NT = (((1,), (1,)), ((), ()))
TN = (((0,), (0,)), ((), ()))
LR, B1, B2, AEPS, WD, STEP = 0.001, 0.9, 0.999, 1e-08, 0.01, 10
N_DEV = 8
SMALL_IN = 11904
SMALL_OUT = 11136


def _cp(sem=None, vmem=VMEM_LIMIT):
    kw = dict(vmem_limit_bytes=vmem)
    if sem is not None:
        kw["dimension_semantics"] = sem
    return pltpu.CompilerParams(**kw)


def _rows(tm, n):
    return pl.BlockSpec((tm, n), lambda i: (i, 0))


def _const(shape):
    nd = len(shape)
    return pl.BlockSpec(shape, lambda *_: (0,) * nd)


def _sds(shape, dt):
    return jax.ShapeDtypeStruct(shape, dt)


def _lane(shape):
    return lax.broadcasted_iota(jnp.int32, shape, len(shape) - 1)


def _sigmoid(x):
    return 1.0 / (1.0 + jnp.exp(-x))


class _Exchange:
    MASKS = {"chip4": (2, 4, 6), "chip4p": (2, 4, 6), "all8": (1, 2, 3, 4, 5, 6, 7), "bcast8": (1, 2, 3, 4, 5, 6, 7)}

    def __init__(self, items):
        self.srcs = [s for s, _ in items]
        self.kinds = [k for _, k in items]
        self.n = len(items)
        self.out_shapes = []
        for s, k in items:
            shape = {"all8": (N_DEV,) + s.shape[1:], "bcast8": (N_DEV,) + s.shape, "chip4": (4,) + s.shape,
                     "chip4p": (4,) + s.shape[1:]}[k]
            self.out_shapes.append(_sds(shape, s.dtype))
        self.sem_index = {}
        for t, k in enumerate(self.kinds):
            for m in self.MASKS[k]:
                self.sem_index[(t, m)] = len(self.sem_index)
        n_sem = len(self.sem_index)
        self.scratch = [pltpu.SemaphoreType.DMA((n_sem,)), pltpu.SemaphoreType.DMA((n_sem,)),
                        pltpu.SemaphoreType.DMA((self.n,))]
        self.in_specs = [pl.BlockSpec(memory_space=pl.ANY)] * self.n
        self.out_specs = [pl.BlockSpec(memory_space=pl.ANY)] * self.n

    def copies(self, src_refs, dst_refs, send_sems, recv_sems, local_sems):
        x, y, c = lax.axis_index("x"), lax.axis_index("y"), lax.axis_index("c")
        my_id = 4 * x + 2 * y + c
        my_chip = 2 * x + y

        def piece(t, dev_id, chip):
            k = self.kinds[t]
            return src_refs[t].at[dev_id] if k == "all8" else src_refs[t].at[chip] if k == "chip4p" else src_refs[t]

        out = []
        for t in range(self.n):
            slot = dst_refs[t].at[my_chip if self.kinds[t] in ("chip4", "chip4p") else my_id]
            out.append(pltpu.make_async_copy(piece(t, my_id, my_chip), slot, local_sems.at[t]))
            for m in self.MASKS[self.kinds[t]]:
                px = 1 - x if m & 4 else x
                py = 1 - y if m & 2 else y
                pc = 1 - c if m & 1 else c
                s = self.sem_index[(t, m)]
                out.append(pltpu.make_async_remote_copy(
                    src_ref=piece(t, 4 * px + 2 * py + pc, 2 * px + py), dst_ref=slot,
                    send_sem=send_sems.at[s], recv_sem=recv_sems.at[s],
                    device_id=(px, py, pc), device_id_type=pl.DeviceIdType.MESH))
        return out


def _hosted(body, ex, n_in, n_out, n_scr, n_steps):
    if ex is None:
        return body

    def wrapped(*refs):
        ins, xin = refs[:n_in], refs[n_in:n_in + ex.n]
        o0 = n_in + ex.n
        outs, xout = refs[o0:o0 + n_out], refs[o0 + n_out:o0 + n_out + ex.n]
        s0 = o0 + n_out + ex.n
        scr, sems = refs[s0:s0 + n_scr], refs[s0 + n_scr:]

        @pl.when(pl.program_id(0) == 0)
        def _():
            for cp in ex.copies(xin, xout, *sems):
                cp.start()

        body(*ins, *outs, *scr)

        @pl.when(pl.program_id(0) == n_steps - 1)
        def _():
            for cp in ex.copies(xin, xout, *sems):
                cp.wait()

    return wrapped


def _exchange(items, name):
    ex = _Exchange(items)
    n = ex.n

    def body(*refs):
        copies = ex.copies(refs[:n], refs[n:2 * n], *refs[2 * n:])
        for cp in copies:
            cp.start()
        for cp in copies:
            cp.wait()

    outs = pl.pallas_call(
        body, name=name, out_shape=tuple(ex.out_shapes), in_specs=ex.in_specs, out_specs=tuple(ex.out_specs),
        scratch_shapes=ex.scratch,
    )(*ex.srcs)
    return list(outs)


def _mod_shard(c_all, w_ada, b_shard):
    n = w_ada.shape[1]

    def body(c_ref, w_ref, b_ref, o_ref, sc_ref):
        cv = c_ref[...]
        sc = cv * _sigmoid(cv)
        sc_ref[...] = sc
        o_ref[...] = jnp.dot(sc, w_ref[...], precision=HI, preferred_element_type=F32) + b_ref[...]

    bn = 512
    return pl.pallas_call(
        body, name="mod_shard", out_shape=(_sds((N_DEV, n), F32), _sds((N_DEV, D), F32)), grid=(n // bn,),
        in_specs=[_const((N_DEV, D)), pl.BlockSpec((D, bn), lambda j: (0, j)), pl.BlockSpec((1, bn), lambda j: (0, j))],
        out_specs=(pl.BlockSpec((N_DEV, bn), lambda j: (0, j)), _const((N_DEV, D))),
        compiler_params=_cp(("arbitrary",)),
    )(c_all, w_ada, b_shard)


def _head_sum_mats():
    e = np.zeros((AW, LANES), np.float32)
    for h in range(NH):
        e[h * DH:(h + 1) * DH, h] = 1.0
    return jnp.asarray(e, BF16), jnp.asarray(e.T.copy(), BF16)


def _dot2(x, w):
    hi = x.astype(BF16)
    lo = (x - hi.astype(F32)).astype(BF16)
    return jnp.dot(hi, w, preferred_element_type=F32) + jnp.dot(lo, w, preferred_element_type=F32)


def _fwd_in(x, mod8, n1g, w_in_p, e512, et512, qg512, kg512, bf128, tm, ex=None):
    S = x.shape[0]

    def body(x_ref, mod_ref, n1g_ref, w_ref, e_ref, et_ref, qg_ref, kg_ref, bf_ref,
             h1_ref, qh_ref, kh_ref, v_ref, qn_ref, kn_ref, rq_ref, rk_ref, fgb_ref, alin_ref, agate_ref, u0_ref):
        xv = x_ref[...]
        r1 = lax.rsqrt(jnp.mean(xv * xv, axis=-1, keepdims=True) + EPS)
        h = (xv * r1) * (n1g_ref[...] * (1.0 + mod_ref[1:2, :])) + mod_ref[0:1, :]
        hb = h.astype(BF16)
        h1_ref[...] = hb

        def seg(a, b):
            return jnp.dot(hb, w_ref[:, a:b], preferred_element_type=F32)

        def headnorm(t, g_ref, scale, n_ref, r_ref, o_ref):
            ss = _dot2(t * t, e_ref[...])
            r = lax.rsqrt(ss * (1.0 / DH) + EPS)
            tn = t * _dot2(r, et_ref[...])
            n_ref[...] = tn.astype(BF16)
            r_ref[...] = r
            o_ref[...] = (tn * (g_ref[...] * scale)).astype(BF16)

        headnorm(seg(0, 512), qg_ref, DH ** -0.5, qn_ref, rq_ref, qh_ref)
        headnorm(seg(512, 1024), kg_ref, 1.0, kn_ref, rk_ref, kh_ref)
        v_ref[...] = seg(1024, 1536).astype(BF16)
        alin = seg(1536, 2048)
        agate = seg(2048, 2560)
        alin_ref[...] = alin.astype(BF16)
        agate_ref[...] = agate.astype(BF16)
        u0_ref[...] = alin * _sigmoid(agate)
        fgb_ref[...] = seg(2560, NP) + bf_ref[...]

    bf = lambda: _sds((S, AW), BF16)
    xs = ex.srcs if ex else []
    outs = pl.pallas_call(
        _hosted(body, ex, 9, 12, 0, S // tm), name="fwd_in", grid=(S // tm,),
        out_shape=(_sds((S, D), BF16), bf(), bf(), bf(), bf(), bf(), _sds((S, LANES), F32), _sds((S, LANES), F32),
                   _sds((S, LANES), F32), bf(), bf(), _sds((S, CW), F32), *(ex.out_shapes if ex else [])),
        in_specs=[_rows(tm, D), _const((8, D)), _const((1, D)), _const((D, NP)), _const((AW, LANES)), _const((LANES, AW)),
                  _const((1, AW)), _const((1, AW)), _const((1, LANES)), *(ex.in_specs if ex else [])],
        out_specs=(_rows(tm, D), _rows(tm, AW), _rows(tm, AW), _rows(tm, AW), _rows(tm, AW), _rows(tm, AW),
                   _rows(tm, LANES), _rows(tm, LANES), _rows(tm, LANES), _rows(tm, AW), _rows(tm, AW), _rows(tm, CW),
                   *(ex.out_specs if ex else [])),
        scratch_shapes=ex.scratch if ex else [],
        compiler_params=_cp(("arbitrary",)),
    )(x, mod8, n1g, w_in_p, e512, et512, qg512, kg512, bf128, *xs)
    return outs[:12], list(outs[12:])


def _split3(f):
    f1 = f.astype(BF16).astype(F32)
    f2 = (f - f1).astype(BF16).astype(F32)
    return f1, f2, f - f1 - f2


def _dot3(w, x):
    return sum(jnp.dot(w, piece.astype(BF16), preferred_element_type=F32) for piece in _split3(x))


def _fwd_decay(fgb, qh, kh, vb, shift, tm):
    S = fgb.shape[0]

    def body(shift_ref, fgb_ref, qh_ref, kh_ref, vb_ref, qa_ref, ka_ref, va_ref, f_ref, carry_ref):
        @pl.when(pl.program_id(0) == 0)
        def _():
            carry_ref[...] = jnp.zeros_like(carry_ref)

        fb = fgb_ref[...]
        lf = jnp.minimum(fb, 0.0) - jnp.log1p(jnp.exp(-jnp.abs(fb)))
        tri = (lax.broadcasted_iota(jnp.int32, (tm, tm), 0) >= lax.broadcasted_iota(jnp.int32, (tm, tm), 1)
               ).astype(F32).astype(BF16)
        cs = _dot3(tri, lf) + carry_ref[0:1, :]
        f_ref[...] = cs
        carry_ref[...] = jnp.broadcast_to(cs[tm - 1:tm, :], carry_ref.shape)
        lane = _lane((tm, LANES))
        s1, s2, s3 = _split3(jnp.zeros((tm, LANES), F32) - shift_ref[0, 0])
        tail_q = jnp.where((lane >= 67) & (lane < 70), 1.0,
                           jnp.where(lane == 70, s1, jnp.where(lane == 71, s2, jnp.where(lane == 72, s3, 0.0))))
        tail_k = jnp.where(((lane >= 64) & (lane < 67)) | ((lane >= 70) & (lane < 73)), 1.0, 0.0)
        tail_v = jnp.where(lane == DH, 1.0, 0.0)
        for p in range(NH // 2):
            qp = qh_ref[:, p * LANES:(p + 1) * LANES].astype(F32)
            kp = kh_ref[:, p * LANES:(p + 1) * LANES].astype(F32)
            vp = vb_ref[:, p * LANES:(p + 1) * LANES].astype(F32)
            for hh in range(2):
                h = 2 * p + hh
                f1, f2, f3 = _split3(cs[:, h:h + 1])
                qb = qp if hh == 0 else pltpu.roll(qp, 64, 1)
                kb = kp if hh == 0 else pltpu.roll(kp, 64, 1)
                vh = vp if hh == 0 else pltpu.roll(vp, 64, 1)
                augq = jnp.where(lane == 64, f1, jnp.where(lane == 65, f2, jnp.where(lane == 66, f3, tail_q)))
                augk = jnp.where(lane == 67, -f1, jnp.where(lane == 68, -f2, jnp.where(lane == 69, -f3, tail_k)))
                qa_ref[h] = jnp.where(lane < DH, qb, augq).astype(BF16)
                ka_ref[h] = jnp.where(lane < DH, kb, augk).astype(BF16)
                va_ref[h] = jnp.where(lane < DH, vh, tail_v).astype(BF16)

    hm = pl.BlockSpec((NH, tm, LANES), lambda i: (0, i, 0))
    hms = _sds((NH, S, LANES), BF16)
    return pl.pallas_call(
        body, name="fwd_decay", grid=(S // tm,),
        out_shape=(hms, hms, hms, _sds((S, LANES), F32)),
        in_specs=[SMEM_SPEC, _rows(tm, LANES), _rows(tm, AW), _rows(tm, AW), _rows(tm, AW)],
        out_specs=(hm, hm, hm, _rows(tm, LANES)),
        scratch_shapes=[pltpu.VMEM((8, LANES), F32)], compiler_params=_cp(("arbitrary",)),
    )(shift, fgb, qh, kh, vb)


SKIP = -106.0


def _block_loops(first, step, needed, run):
    def both(j):
        return jnp.logical_and(needed(0, j), needed(1, j))

    def walk(heads):
        def go(j):
            run(j, heads)
            return j + step
        return go

    j = lax.while_loop(both, walk((0, 1)), first)
    lax.while_loop(functools.partial(needed, 0), walk((0,)), j)
    lax.while_loop(functools.partial(needed, 1), walk((1,)), j)


def _logit_bound(qg, kg):
    return (2.0 * 1.03 * DH ** 0.5 * jnp.max(jnp.abs(qg)) * jnp.max(jnp.abs(kg))).reshape(1, 1)


def _skip_tables(f, tq):
    return f[0::tq, :NH].T, f[tq - 1::tq, :NH].T


SMEM_SPEC = pl.BlockSpec(memory_space=pltpu.SMEM)


def _causal_rect(rows, cols, col0):
    return (lax.broadcasted_iota(jnp.int32, (rows, cols), 0)
            >= lax.broadcasted_iota(jnp.int32, (rows, cols), 1) + col0)


def _chunk(tq):
    return tq


SHIFT_MAX = 60.0


def _shift(bd):
    return jnp.where(bd <= SHIFT_MAX, 0.5 * bd, 0.0)


def _attn_fwd(qa, ka, va, fs, fe, bd, tq):
    S = qa.shape[1]
    nq = S // tq

    def body(fs_ref, fe_ref, bd_ref, qa_ref, ka_ref, va_ref, o_ref, lser_ref, m_ref, acc_ref):
        pr, i = pl.program_id(0), pl.program_id(1)
        sel_a = _lane((tq, LANES)) < DH
        acc_ref[...] = jnp.zeros_like(acc_ref)

        def logits(j, hh, masked):
            start = pl.multiple_of(j * tq, tq)
            s = lax.dot_general(qa_ref[hh], ka_ref[hh, pl.ds(start, tq), :], NT, preferred_element_type=F32)
            if masked:
                s = jnp.where(_causal_rect(tq, tq, 0), s, NEG)
            return s, va_ref[hh, pl.ds(start, tq), :]

        def shifted_step(j, heads, masked=False):
            for hh in heads:
                s, vb = logits(j, hh, masked)
                acc_ref[hh] += jnp.dot(jnp.exp(s).astype(BF16), vb, preferred_element_type=F32)

        def online_step(j, heads, masked=False):
            for hh in heads:
                s, vb = logits(j, hh, masked)
                m_prev = m_ref[hh]
                m_new = jnp.maximum(m_prev, jnp.max(s, axis=1, keepdims=True))
                p = jnp.exp(s - jnp.tile(m_new, (1, tq // LANES)))
                m_ref[hh] = m_new
                acc_ref[hh] = jnp.exp(m_prev - m_new) * acc_ref[hh] + jnp.dot(p.astype(BF16), vb,
                                                                              preferred_element_type=F32)

        def needed(hh, j):
            top = fs_ref[2 * pr + hh, i] + bd_ref[0, 0]
            return jnp.logical_and(j >= 0, top - fe_ref[2 * pr + hh, jnp.maximum(j, 0)] >= SKIP)

        @pl.when(bd_ref[0, 0] <= SHIFT_MAX)
        def _():
            m_ref[...] = jnp.zeros_like(m_ref)
            shifted_step(i, (0, 1), masked=True)
            _block_loops(i - 1, -1, needed, shifted_step)

        @pl.when(bd_ref[0, 0] > SHIFT_MAX)
        def _():
            m_ref[...] = jnp.full(m_ref.shape, NEG, F32)
            online_step(i, (0, 1), masked=True)
            _block_loops(i - 1, -1, needed, online_step)

        outs, lses = [], []
        for hh in range(2):
            acc = acc_ref[hh]
            row_sum = jnp.broadcast_to(acc[:, DH:DH + 1], (tq, LANES))
            outs.append(acc / row_sum)
            lses.append(m_ref[hh] + jnp.log(row_sum))
        o_ref[...] = jnp.where(sel_a, outs[0], pltpu.roll(outs[1], 64, 1))
        row = lax.broadcasted_iota(jnp.int32, (8, tq), 0)
        lser_ref[0, 0] = jnp.where(row == 0, lses[0].T[0:8, :], lses[1].T[0:8, :])

    return pl.pallas_call(
        body, name="attn_fwd", grid=(NH // 2, nq),
        out_shape=(_sds((S, AW), F32), _sds((NH // 2, nq, 8, tq), F32)),
        in_specs=[SMEM_SPEC, SMEM_SPEC, SMEM_SPEC,
                  pl.BlockSpec((2, tq, LANES), lambda p, i: (p, i, 0)),
                  pl.BlockSpec((2, S, LANES), lambda p, i: (p, 0, 0)),
                  pl.BlockSpec((2, S, LANES), lambda p, i: (p, 0, 0))],
        out_specs=(pl.BlockSpec((tq, LANES), lambda p, i: (i, p)),
                   pl.BlockSpec((1, 1, 8, tq), lambda p, i: (p, i, 0, 0))),
        scratch_shapes=[pltpu.VMEM((2, tq, LANES), F32), pltpu.VMEM((2, tq, LANES), F32)],
        compiler_params=_cp(("parallel", "parallel")),
    )(fs, fe, bd, qa, ka, va)


HALO = 32
CHUNK_ROWS = 64


def _halo_prev(tm):
    return pl.BlockSpec((HALO, CW), lambda i: (jnp.maximum(i * (tm // HALO) - 1, 0), 0))


SUBLANES = 8
SHIFT_ROWS = 24


def _shifted_copies(ext_ref, sh_ref, tm):
    for k in range(1, SUBLANES):
        sh_ref[k - 1, 0:tm + SHIFT_ROWS, :] = ext_ref[k:k + tm + SHIFT_ROWS, :]


def _ext_rows(ext_ref, sh_ref, o):
    k = o % SUBLANES
    if k == 0:
        return ext_ref[o:o + CHUNK_ROWS, :]
    return sh_ref[k - 1, o - k:o - k + CHUNK_ROWS, :]


def _fwd_conv(u0, w32, cb, lng, lnb, beta_c, tm):
    S = u0.shape[0]

    def body(cur_ref, prev_ref, w_ref, cb_ref, lng_ref, lnb_ref, beta_ref, u1_ref, mc_ref, ext_ref, sh_ref):
        i = pl.program_id(0)
        ext_ref[0:HALO, :] = jnp.where(i == 0, 0.0, prev_ref[...])
        ext_ref[HALO:, :] = cur_ref[...]
        _shifted_copies(ext_ref, sh_ref, tm)
        for r0 in range(0, tm, CHUNK_ROWS):
            acc = jnp.zeros((CHUNK_ROWS, CW), F32) + cb_ref[...]
            for j in range(KC):
                acc = acc + w_ref[j:j + 1, :] * _ext_rows(ext_ref, sh_ref, r0 + 2 + j)
            u1_ref[r0:r0 + CHUNK_ROWS, :] = acc
        u1 = u1_ref[...]
        mu = jnp.mean(u1, axis=-1, keepdims=True)
        d = u1 - mu
        rstd = lax.rsqrt(jnp.mean(d * d, axis=-1, keepdims=True) + EPS)
        u2 = d * rstd * lng_ref[...] + lnb_ref[...]
        u3 = u2 * _sigmoid(u2)
        rc = lax.rsqrt(jnp.mean(u3 * u3, axis=-1, keepdims=True) + EPS)
        mc_ref[...] = (u3 * rc * beta_ref[...]).astype(BF16)

    return pl.pallas_call(
        body, name="fwd_conv", grid=(S // tm,),
        out_shape=(_sds((S, CW), F32), _sds((S, CW), BF16)),
        in_specs=[_rows(tm, CW), _halo_prev(tm), _const((HALO, CW)), _const((1, CW)), _const((1, CW)), _const((1, CW)),
                  _const((1, CW))],
        out_specs=(_rows(tm, CW), _rows(tm, CW)),
        scratch_shapes=[pltpu.VMEM((tm + HALO, CW), F32), pltpu.VMEM((SUBLANES - 1, tm + HALO, CW), F32)],
        compiler_params=_cp(("parallel",)),
    )(u0, u0, w32, cb, lng, lnb, beta_c)


def _fwd_out(o_attn, mc, x, mod8, n2g, beta_a, w_out, tm):
    S = x.shape[0]

    def body(o_ref, mc_ref, x_ref, mod_ref, n2g_ref, beta_ref, w_ref, mg_ref, ob_ref, x2_ref, h2_ref):
        ov = o_ref[...]
        ra = lax.rsqrt(jnp.mean(ov * ov, axis=-1, keepdims=True) + EPS)
        ma = (ov * ra * beta_ref[...]).astype(BF16)
        mcv = mc_ref[...]
        mg_ref[:, 0:AW] = ma
        mg_ref[:, AW:D] = mcv
        o = (jnp.dot(ma, w_ref[0:AW, :], preferred_element_type=F32)
             + jnp.dot(mcv, w_ref[AW:D, :], preferred_element_type=F32))
        ob_ref[...] = o.astype(BF16)
        x2 = x_ref[...] + mod_ref[2:3, :] * o
        x2_ref[...] = x2
        r2 = lax.rsqrt(jnp.mean(x2 * x2, axis=-1, keepdims=True) + EPS)
        h2_ref[...] = ((x2 * r2) * (n2g_ref[...] * (1.0 + mod_ref[4:5, :])) + mod_ref[3:4, :]).astype(BF16)

    return pl.pallas_call(
        body, name="fwd_out", grid=(S // tm,),
        out_shape=(_sds((S, D), BF16), _sds((S, D), BF16), _sds((S, D), F32), _sds((S, D), BF16)),
        in_specs=[_rows(tm, AW), _rows(tm, CW), _rows(tm, D), _const((8, D)), _const((1, D)), _const((1, AW)),
                  _const((D, D))],
        out_specs=(_rows(tm, D), _rows(tm, D), _rows(tm, D), _rows(tm, D)),
        compiler_params=_cp(("parallel",)),
    )(o_attn, mc, x, mod8, n2g, beta_a, w_out)


def _fwd_ffn(h2, w1, w2, x2, tgt, mod8, tm):
    S = h2.shape[0]
    nk = w1.shape[0]
    bf = w1.shape[2]

    def body(h2_ref, w1_ref, w2_ref, x2_ref, tgt_ref, mod_ref, r_ref, dy_ref, loss_ref, dg2_ref):
        @pl.when(pl.program_id(0) == 0)
        def _():
            loss_ref[...] = jnp.zeros_like(loss_ref)
            dg2_ref[...] = jnp.zeros_like(dg2_ref)

        f2 = None
        for k in range(nk):
            r = jnp.maximum(jnp.dot(h2_ref[...], w1_ref[k], preferred_element_type=F32), 0.0)
            r_ref[:, k * bf:(k + 1) * bf] = r.astype(BF16)
            part = jnp.dot((r * r).astype(BF16), w2_ref[k], preferred_element_type=F32)
            f2 = part if f2 is None else f2 + part
        e = x2_ref[...] + mod_ref[5:6, :] * f2 - tgt_ref[...]
        dy = e * (1.0 / D)
        dy_ref[...] = dy
        loss_ref[...] += 0.5 * jnp.sum(jnp.sum(e * dy, axis=1, keepdims=True), axis=0, keepdims=True)
        dg2_ref[...] += jnp.sum((dy * f2).reshape(tm // 8, 8, D), axis=0)

    once = pl.Buffered(1)
    return pl.pallas_call(
        body, name="fwd_ffn", grid=(S // tm,),
        out_shape=(_sds((S, DFF), BF16), _sds((S, D), F32), _sds((8, LANES), F32), _sds((8, D), F32)),
        in_specs=[_rows(tm, D), pl.BlockSpec((nk, D, bf), lambda i: (0, 0, 0), pipeline_mode=once),
                  pl.BlockSpec((nk, bf, D), lambda i: (0, 0, 0), pipeline_mode=once), _rows(tm, D), _rows(tm, D),
                  _const((8, D))],
        out_specs=(_rows(tm, DFF), _rows(tm, D), _const((8, LANES)), _const((8, D))),
        compiler_params=_cp(("arbitrary",)),
    )(h2, w1, w2, x2, tgt, mod8)


def _bwd_ffn(dy, mod8, r, w1, w2, tm):
    S = dy.shape[0]
    nk = w1.shape[0]
    bf = w1.shape[2]

    def body(dy_ref, mod_ref, r_ref, w1_ref, w2_ref, df2_ref, df1_ref, dh2_ref):
        df2 = (dy_ref[...] * mod_ref[5:6, :]).astype(BF16)
        df2_ref[...] = df2
        dh2 = None
        for k in range(nk):
            da = lax.dot_general(df2, w2_ref[k], NT, preferred_element_type=F32)
            df1 = (da * (2.0 * r_ref[:, k * bf:(k + 1) * bf].astype(F32))).astype(BF16)
            df1_ref[:, k * bf:(k + 1) * bf] = df1
            part = lax.dot_general(df1, w1_ref[k], NT, preferred_element_type=F32)
            dh2 = part if dh2 is None else dh2 + part
        dh2_ref[...] = dh2

    once = pl.Buffered(1)
    return pl.pallas_call(
        body, name="bwd_ffn", grid=(S // tm,),
        out_shape=(_sds((S, D), BF16), _sds((S, DFF), BF16), _sds((S, D), F32)),
        in_specs=[_rows(tm, D), _const((8, D)), _rows(tm, DFF),
                  pl.BlockSpec((nk, D, bf), lambda i: (0, 0, 0), pipeline_mode=once),
                  pl.BlockSpec((nk, bf, D), lambda i: (0, 0, 0), pipeline_mode=once)],
        out_specs=(_rows(tm, D), _rows(tm, DFF), _rows(tm, D)),
        compiler_params=_cp(("parallel",)),
    )(dy, mod8, r, w1, w2)


def _wgrad(a, b, name, square_a=False, col_pieces=False, tk=1024, bm=1024, bn=1024):
    S, M = a.shape
    N = b.shape[1]
    bm, bn, tk = min(bm, M), min(bn, N), min(tk, S)
    nk = S // tk

    def body(a_ref, b_ref, o_ref, acc_ref):
        av = a_ref[...]
        if square_a:
            af = av.astype(F32)
            av = (af * af).astype(BF16)
        part = lax.dot_general(av, b_ref[...], TN, preferred_element_type=F32)

        @pl.when(pl.program_id(2) == 0)
        def _():
            acc_ref[...] = part

        @pl.when(pl.program_id(2) > 0)
        def _():
            acc_ref[...] += part

        @pl.when(pl.program_id(2) == nk - 1)
        def _():
            if col_pieces:
                o_ref[0] = acc_ref[...].astype(BF16)
            else:
                o_ref[...] = acc_ref[...].astype(BF16)

    if col_pieces:
        out_shape, out_spec = _sds((N // bn, M, bn), BF16), pl.BlockSpec((1, bm, bn), lambda mi, ni, k: (ni, mi, 0))
    else:
        out_shape, out_spec = _sds((M, N), BF16), pl.BlockSpec((bm, bn), lambda mi, ni, k: (mi, ni))
    return pl.pallas_call(
        body, name=name, grid=(M // bm, N // bn, nk), out_shape=out_shape,
        in_specs=[pl.BlockSpec((tk, bm), lambda mi, ni, k: (k, mi)), pl.BlockSpec((tk, bn), lambda mi, ni, k: (k, ni))],
        out_specs=out_spec, scratch_shapes=[pltpu.VMEM((bm, bn), F32)],
        compiler_params=_cp(("parallel", "parallel", "arbitrary")),
    )(a, b)


def _wgrad_in(h1, pieces, tk=1024):
    S = h1.shape[0]
    tk = min(tk, S)
    widths = [p.shape[1] for p in pieces]
    offs = [sum(widths[:t]) for t in range(len(widths))]

    def body(a_ref, *refs):
        o_ref, acc_ref = refs[-2:]

        @pl.when(pl.program_id(0) == 0)
        def _():
            acc_ref[...] = jnp.zeros_like(acc_ref)

        for b_ref, off, w in zip(refs[:-2], offs, widths):
            acc_ref[:, off:off + w] += lax.dot_general(a_ref[...], b_ref[...], TN, preferred_element_type=F32)

        @pl.when(pl.program_id(0) == S // tk - 1)
        def _():
            o_ref[...] = acc_ref[...].astype(BF16)

    return pl.pallas_call(
        body, name="wgrad_in", grid=(S // tk,), out_shape=_sds((D, NP), BF16),
        in_specs=[_rows(tk, D)] + [_rows(tk, w) for w in widths], out_specs=_const((D, NP)),
        scratch_shapes=[pltpu.VMEM((D, NP), F32)], compiler_params=_cp(("arbitrary",)),
    )(h1, *pieces)


def _colsum8(t):
    return jnp.sum(t.reshape(t.shape[0] // 8, 8, t.shape[1]), axis=0)


def _bwd_mid(dh2, dy, x2, ob, o_attn, u1, mod8, n2g, beta_a, beta_c, lng, lnb, w_out, e512, tm, tq, ex=None):
    S = dy.shape[0]

    def body(dh2_ref, dy_ref, x2_ref, ob_ref, oa_ref, u1_ref, mod_ref, n2g_ref, ba_ref, bc_ref, lng_ref, lnb_ref, w_ref,
             e_ref, dx2_ref, do_ref, doa_ref, du1_ref, acc_d_ref, acc_h_ref, dr_ref):
        @pl.when(pl.program_id(0) == 0)
        def _():
            acc_d_ref[...] = jnp.zeros_like(acc_d_ref)
            acc_h_ref[...] = jnp.zeros_like(acc_h_ref)

        x2 = x2_ref[...]
        dh2 = dh2_ref[...]
        r2 = lax.rsqrt(jnp.mean(x2 * x2, axis=-1, keepdims=True) + EPS)
        xn2 = x2 * r2
        gain = n2g_ref[...] * (1.0 + mod_ref[4:5, :])
        dxn = dh2 * gain
        dx2 = dy_ref[...] + r2 * (dxn - xn2 * jnp.mean(dxn * xn2, axis=-1, keepdims=True))
        dx2_ref[...] = dx2
        t = dh2 * xn2
        acc_d_ref[0] += _colsum8(dh2)
        acc_d_ref[1] += _colsum8(t * n2g_ref[...])
        acc_d_ref[2] += _colsum8(t * (1.0 + mod_ref[4:5, :]))
        acc_d_ref[3] += _colsum8(dx2 * ob_ref[...].astype(F32))
        do = (dx2 * mod_ref[2:3, :]).astype(BF16)
        do_ref[...] = do
        dma = lax.dot_general(do, w_ref[0:AW, :], NT, preferred_element_type=F32)
        dmc = lax.dot_general(do, w_ref[AW:D, :], NT, preferred_element_type=F32)
        ov = oa_ref[...]
        ra = lax.rsqrt(jnp.mean(ov * ov, axis=-1, keepdims=True) + EPS)
        on = ov * ra
        acc_h_ref[0] += _colsum8(dma * on)
        don = dma * ba_ref[...]
        doa = (ra * (don - on * jnp.mean(don * on, axis=-1, keepdims=True))).astype(BF16)
        doa_ref[...] = doa
        delta_t = _dot2(doa.astype(F32) * ov, e_ref[...]).T
        for p in range(NH // 2):
            dr_ref[p, 0] = delta_t[2 * p:2 * p + 8, :]
        u1 = u1_ref[...]
        mu = jnp.mean(u1, axis=-1, keepdims=True)
        d = u1 - mu
        rstd = lax.rsqrt(jnp.mean(d * d, axis=-1, keepdims=True) + EPS)
        uh = d * rstd
        u2 = uh * lng_ref[...] + lnb_ref[...]
        sg = _sigmoid(u2)
        u3 = u2 * sg
        rc = lax.rsqrt(jnp.mean(u3 * u3, axis=-1, keepdims=True) + EPS)
        u3n = u3 * rc
        acc_h_ref[1] += _colsum8(dmc * u3n)
        du3n = dmc * bc_ref[...]
        du3 = rc * (du3n - u3n * jnp.mean(du3n * u3n, axis=-1, keepdims=True))
        du2 = du3 * (sg * (1.0 + u2 * (1.0 - sg)))
        acc_h_ref[2] += _colsum8(du2 * uh)
        acc_h_ref[3] += _colsum8(du2)
        duh = du2 * lng_ref[...]
        du1_ref[...] = rstd * (duh - jnp.mean(duh, axis=-1, keepdims=True)
                               - uh * jnp.mean(duh * uh, axis=-1, keepdims=True))

    per = tq // tm
    outs = pl.pallas_call(
        _hosted(body, ex, 14, 7, 0, S // tm), name="bwd_mid", grid=(S // tm,),
        out_shape=(_sds((S, D), F32), _sds((S, D), BF16), _sds((S, AW), BF16), _sds((S, CW), F32),
                   _sds((4, 8, D), F32), _sds((4, 8, AW), F32), _sds((NH // 2, S // tq, 8, tq), F32),
                   *(ex.out_shapes if ex else [])),
        in_specs=[_rows(tm, D), _rows(tm, D), _rows(tm, D), _rows(tm, D), _rows(tm, AW), _rows(tm, CW), _const((8, D)),
                  _const((1, D)), _const((1, AW)), _const((1, CW)), _const((1, CW)), _const((1, CW)), _const((D, D)),
                  _const((AW, LANES)), *(ex.in_specs if ex else [])],
        out_specs=(_rows(tm, D), _rows(tm, D), _rows(tm, AW), _rows(tm, CW), _const((4, 8, D)), _const((4, 8, AW)),
                   pl.BlockSpec((NH // 2, 1, 8, tm), lambda i: (0, i // per, 0, i % per)),
                   *(ex.out_specs if ex else [])),
        scratch_shapes=ex.scratch if ex else [],
        compiler_params=_cp(("arbitrary",)),
    )(dh2, dy, x2, ob, o_attn, u1, mod8, n2g, beta_a, beta_c, lng, lnb, w_out, e512, *(ex.srcs if ex else []))
    return outs[:7], list(outs[7:])


def _attn_bwd_dq(qa, ka, v, do, o_attn, lsec, fs, fe, bd, tq):
    S = qa.shape[1]
    nq = S // tq
    tc = _chunk(tq)

    def body(fs_ref, fe_ref, bd_ref, qa_ref, ka_ref, v_ref, do_ref, o_ref, lse_ref, dqa_ref, dr_ref, acc_ref):
        pr, i = pl.program_id(0), pl.program_id(1)
        sel_a = _lane((tq, LANES)) < DH
        dov = do_ref[...]
        prod = dov.astype(F32) * o_ref[...]
        zero = jnp.zeros_like(prod)
        deltas = [jnp.broadcast_to(jnp.sum(jnp.where(sel_a, prod, zero), axis=1, keepdims=True), (tq, LANES)),
                  jnp.broadcast_to(jnp.sum(jnp.where(sel_a, zero, prod), axis=1, keepdims=True), (tq, LANES))]
        zb = jnp.zeros_like(dov)
        dos = [jnp.where(sel_a, dov, zb), jnp.where(sel_a, zb, dov)]
        acc_ref[...] = jnp.zeros_like(acc_ref)

        def kv_step(j, heads, masked=False):
            for c0 in range(0, tq, tc):
                start = pl.multiple_of(j * tq + c0, tc)
                vb = v_ref[pl.ds(start, tc), :]
                for hh in heads:
                    kb = ka_ref[hh, pl.ds(start, tc), :]
                    s = lax.dot_general(qa_ref[hh], kb, NT, preferred_element_type=F32)
                    p = jnp.exp(s - jnp.tile(lse_ref[hh], (1, tc // LANES)))
                    if masked:
                        p = jnp.where(_causal_rect(tq, tc, c0), p, 0.0)
                    dp = lax.dot_general(dos[hh], vb, NT, preferred_element_type=F32)
                    ds = p * (dp - jnp.tile(deltas[hh], (1, tc // LANES)))
                    acc_ref[hh] += jnp.dot(ds.astype(BF16), kb, preferred_element_type=F32)

        def needed(hh, j):
            top = fs_ref[2 * pr + hh, i] + bd_ref[0, 0]
            return jnp.logical_and(j >= 0, top - fe_ref[2 * pr + hh, jnp.maximum(j, 0)] >= SKIP)

        kv_step(i, (0, 1), masked=True)
        _block_loops(i - 1, -1, needed, kv_step)
        dqa_ref[...] = acc_ref[...]
        row = lax.broadcasted_iota(jnp.int32, (8, tq), 0)
        da = deltas[0].T[0:8, :]
        db = deltas[1].T[0:8, :]
        dr_ref[0, 0] = jnp.where(row == 0, da, db)

    return pl.pallas_call(
        body, name="attn_bwd_dq", grid=(NH // 2, nq),
        out_shape=(_sds((NH, S, LANES), F32), _sds((NH // 2, nq, 8, tq), F32)),
        in_specs=[SMEM_SPEC, SMEM_SPEC, SMEM_SPEC,
                  pl.BlockSpec((2, tq, LANES), lambda p, i: (p, i, 0)),
                  pl.BlockSpec((2, S, LANES), lambda p, i: (p, 0, 0)),
                  pl.BlockSpec((S, LANES), lambda p, i: (0, p)),
                  pl.BlockSpec((tq, LANES), lambda p, i: (i, p)),
                  pl.BlockSpec((tq, LANES), lambda p, i: (i, p)),
                  pl.BlockSpec((2, tq, LANES), lambda p, i: (p, i, 0))],
        out_specs=(pl.BlockSpec((2, tq, LANES), lambda p, i: (p, i, 0)),
                   pl.BlockSpec((1, 1, 8, tq), lambda p, i: (p, i, 0, 0))),
        scratch_shapes=[pltpu.VMEM((2, tq, LANES), F32)],
        compiler_params=_cp(("parallel", "parallel")),
    )(fs, fe, bd, qa, ka, v, do, o_attn, lsec)


def _attn_bwd_dkv(qa, ka, v, do, lser, dr, fs, fe, bd, tq):
    S = qa.shape[1]
    nq = S // tq
    tc = _chunk(tq)

    def body(fs_ref, fe_ref, bd_ref, ka_ref, v_ref, qa_ref, do_ref, lse_ref, dr_ref, dka_ref, dv_ref, acck_ref,
             accv_ref):
        pr, j = pl.program_id(0), pl.program_id(1)
        sel_a = _lane((tq, LANES)) < DH
        vv = v_ref[...]
        zb = jnp.zeros_like(vv)
        vs = [jnp.where(sel_a, vv, zb), jnp.where(sel_a, zb, vv)]
        acck_ref[...] = jnp.zeros_like(acck_ref)
        accv_ref[...] = jnp.zeros_like(accv_ref)

        def q_step(i, heads, masked=False):
            lse8 = lse_ref[0, i]
            dr8 = dr_ref[0, i]
            for c0 in range(0, tq, tc):
                start = pl.multiple_of(i * tq + c0, tc)
                dob = do_ref[pl.ds(start, tc), :]
                for hh in heads:
                    qb = qa_ref[hh, pl.ds(start, tc), :]
                    st = lax.dot_general(ka_ref[hh], qb, NT, preferred_element_type=F32)
                    pt = jnp.exp(st - lse8[hh:hh + 1, c0:c0 + tc])
                    if masked:
                        keep = (lax.broadcasted_iota(jnp.int32, (tq, tc), 0)
                                <= lax.broadcasted_iota(jnp.int32, (tq, tc), 1) + c0)
                        pt = jnp.where(keep, pt, 0.0)
                    accv_ref[hh] += jnp.dot(pt.astype(BF16), dob, preferred_element_type=F32)
                    dpt = lax.dot_general(vs[hh], dob, NT, preferred_element_type=F32)
                    dst = pt * (dpt - dr8[hh:hh + 1, c0:c0 + tc])
                    acck_ref[hh] += jnp.dot(dst.astype(BF16), qb, preferred_element_type=F32)

        def needed(hh, i):
            top = fs_ref[2 * pr + hh, jnp.minimum(i, nq - 1)] + bd_ref[0, 0]
            return jnp.logical_and(i < nq, top - fe_ref[2 * pr + hh, j] >= SKIP)

        q_step(j, (0, 1), masked=True)
        _block_loops(j + 1, 1, needed, q_step)
        dka_ref[...] = acck_ref[...]
        dv_ref[...] = jnp.where(sel_a, accv_ref[0], accv_ref[1]).astype(BF16)

    return pl.pallas_call(
        body, name="attn_bwd_dkv", grid=(NH // 2, nq),
        out_shape=(_sds((NH, S, LANES), F32), _sds((S, AW), BF16)),
        in_specs=[SMEM_SPEC, SMEM_SPEC, SMEM_SPEC,
                  pl.BlockSpec((2, tq, LANES), lambda p, j: (p, j, 0)),
                  pl.BlockSpec((tq, LANES), lambda p, j: (j, p)),
                  pl.BlockSpec((2, S, LANES), lambda p, j: (p, 0, 0)),
                  pl.BlockSpec((S, LANES), lambda p, j: (0, p)),
                  pl.BlockSpec((1, nq, 8, tq), lambda p, j: (p, 0, 0, 0)),
                  pl.BlockSpec((1, nq, 8, tq), lambda p, j: (p, 0, 0, 0))],
        out_specs=(pl.BlockSpec((2, tq, LANES), lambda p, j: (p, j, 0)),
                   pl.BlockSpec((tq, LANES), lambda p, j: (j, p))),
        scratch_shapes=[pltpu.VMEM((2, tq, LANES), F32), pltpu.VMEM((2, tq, LANES), F32)],
        compiler_params=_cp(("parallel", "parallel")),
    )(fs, fe, bd, ka, v, qa, do, lser, dr)


def _attn_bwd(qa, ka, v, do, lser, dr, fs, fe, bd, tq):
    S = qa.shape[1]
    nq = S // tq

    def body(fs_ref, fe_ref, bd_ref, ka_ref, v_ref, qa_ref, do_ref, lse_ref, dr_ref, dqa_hbm, dka_ref, dv_ref,
             accq_ref, acck_ref, accv_ref, out_sem):
        pr, j = pl.program_id(0), pl.program_id(1)
        sel_a = _lane((tq, LANES)) < DH
        vv = v_ref[...]
        zb = jnp.zeros_like(vv)
        vs = [jnp.where(sel_a, vv, zb), jnp.where(sel_a, zb, vv)]
        acck_ref[...] = jnp.zeros_like(acck_ref)
        accv_ref[...] = jnp.zeros_like(accv_ref)

        @pl.when(j == 0)
        def _():
            accq_ref[...] = jnp.zeros_like(accq_ref)

        def q_step(i, heads, masked=False):
            start = pl.multiple_of(i * tq, tq)
            dob = do_ref[pl.ds(start, tq), :]
            lse8 = lse_ref[0, i]
            dr8 = dr_ref[0, i]
            for hh in heads:
                qb = qa_ref[hh, pl.ds(start, tq), :]
                kb = ka_ref[hh]
                st = lax.dot_general(kb, qb, NT, preferred_element_type=F32)
                pt = jnp.exp(st - lse8[hh:hh + 1, :])
                if masked:
                    keep = (lax.broadcasted_iota(jnp.int32, (tq, tq), 0)
                            <= lax.broadcasted_iota(jnp.int32, (tq, tq), 1))
                    pt = jnp.where(keep, pt, 0.0)
                accv_ref[hh] += jnp.dot(pt.astype(BF16), dob, preferred_element_type=F32)
                dpt = lax.dot_general(vs[hh], dob, NT, preferred_element_type=F32)
                dst = (pt * (dpt - dr8[hh:hh + 1, :])).astype(BF16)
                acck_ref[hh] += jnp.dot(dst, qb, preferred_element_type=F32)
                accq_ref[hh, pl.ds(start, tq), :] += lax.dot_general(dst, kb, TN, preferred_element_type=F32)

        def needed(hh, i):
            top = fs_ref[2 * pr + hh, jnp.minimum(i, nq - 1)] + bd_ref[0, 0]
            return jnp.logical_and(i < nq, top - fe_ref[2 * pr + hh, j] >= SKIP)

        q_step(j, (0, 1), masked=True)
        _block_loops(j + 1, 1, needed, q_step)
        dka_ref[...] = acck_ref[...]
        dv_ref[...] = jnp.where(sel_a, accv_ref[0], accv_ref[1]).astype(BF16)

        @pl.when(j == nq - 1)
        def _():
            out = pltpu.make_async_copy(accq_ref, dqa_hbm.at[pl.ds(2 * pr, 2)], out_sem)
            out.start()
            out.wait()

    once = pl.Buffered(1)
    return pl.pallas_call(
        body, name="attn_bwd", grid=(NH // 2, nq),
        out_shape=(_sds((NH, S, LANES), F32), _sds((NH, S, LANES), F32), _sds((S, AW), BF16)),
        in_specs=[SMEM_SPEC, SMEM_SPEC, SMEM_SPEC,
                  pl.BlockSpec((2, tq, LANES), lambda p, j: (p, j, 0)),
                  pl.BlockSpec((tq, LANES), lambda p, j: (j, p)),
                  pl.BlockSpec((2, S, LANES), lambda p, j: (p, 0, 0), pipeline_mode=once),
                  pl.BlockSpec((S, LANES), lambda p, j: (0, p), pipeline_mode=once),
                  pl.BlockSpec((1, nq, 8, tq), lambda p, j: (p, 0, 0, 0)),
                  pl.BlockSpec((1, nq, 8, tq), lambda p, j: (p, 0, 0, 0))],
        out_specs=(pl.BlockSpec(memory_space=pl.ANY),
                   pl.BlockSpec((2, tq, LANES), lambda p, j: (p, j, 0)),
                   pl.BlockSpec((tq, LANES), lambda p, j: (j, p))),
        scratch_shapes=[pltpu.VMEM((2, S, LANES), F32), pltpu.VMEM((2, tq, LANES), F32),
                        pltpu.VMEM((2, tq, LANES), F32), pltpu.SemaphoreType.DMA],
        compiler_params=_cp(("arbitrary", "arbitrary")),
    )(fs, fe, bd, ka, v, qa, do, lser, dr)


def _bwd_conv(du1, u0, alin, agate, w32, tm, ex=None):
    S = du1.shape[0]
    nt = S // tm

    def body(du_ref, dun_ref, u0_ref, u0p_ref, alin_ref, agate_ref, w_ref,
             dalin_ref, dagate_ref, dw_ref, db_ref, extd_ref, extu_ref, du0_ref, shd_ref, shu_ref):
        i = pl.program_id(0)

        @pl.when(i == 0)
        def _():
            dw_ref[...] = jnp.zeros_like(dw_ref)
            db_ref[...] = jnp.zeros_like(db_ref)

        extd_ref[0:tm, :] = du_ref[...]
        extd_ref[tm:, :] = jnp.where(i == nt - 1, 0.0, dun_ref[...])
        extu_ref[0:HALO, :] = jnp.where(i == 0, 0.0, u0p_ref[...])
        extu_ref[HALO:, :] = u0_ref[...]
        _shifted_copies(extd_ref, shd_ref, tm)
        _shifted_copies(extu_ref, shu_ref, tm)
        db_ref[...] += _colsum8(du_ref[...])
        for r0 in range(0, tm, CHUNK_ROWS):
            duc = du_ref[r0:r0 + CHUNK_ROWS, :]
            acc = jnp.zeros((CHUNK_ROWS, CW), F32)
            for j in range(KC):
                acc = acc + w_ref[j:j + 1, :] * _ext_rows(extd_ref, shd_ref, r0 + 30 - j)
                dw_ref[j] += _colsum8(duc * _ext_rows(extu_ref, shu_ref, r0 + 2 + j))
            du0_ref[r0:r0 + CHUNK_ROWS, :] = acc
        du0 = du0_ref[...]
        al = alin_ref[...].astype(F32)
        sg = _sigmoid(agate_ref[...].astype(F32))
        dalin_ref[...] = (du0 * sg).astype(BF16)
        dagate_ref[...] = (du0 * al * sg * (1.0 - sg)).astype(BF16)

    nxt = pl.BlockSpec((HALO, CW), lambda i: (jnp.minimum((i + 1) * (tm // HALO), S // HALO - 1), 0))
    outs = pl.pallas_call(
        _hosted(body, ex, 7, 4, 5, nt), name="bwd_conv", grid=(nt,),
        out_shape=(_sds((S, CW), BF16), _sds((S, CW), BF16), _sds((HALO, 8, CW), F32), _sds((8, CW), F32),
                   *(ex.out_shapes if ex else [])),
        in_specs=[_rows(tm, CW), nxt, _rows(tm, CW), _halo_prev(tm), _rows(tm, CW), _rows(tm, CW), _const((HALO, CW)),
                  *(ex.in_specs if ex else [])],
        out_specs=(_rows(tm, CW), _rows(tm, CW), _const((HALO, 8, CW)), _const((8, CW)), *(ex.out_specs if ex else [])),
        scratch_shapes=[pltpu.VMEM((tm + HALO, CW), F32), pltpu.VMEM((tm + HALO, CW), F32), pltpu.VMEM((tm, CW), F32),
                        pltpu.VMEM((SUBLANES - 1, tm + HALO, CW), F32), pltpu.VMEM((SUBLANES - 1, tm + HALO, CW), F32),
                        *(ex.scratch if ex else [])],
        compiler_params=_cp(("arbitrary",)),
    )(du1, du1, u0, u0, alin, agate, w32, *(ex.srcs if ex else []))
    return outs[:4], list(outs[4:])


def _bwd_qk(dqa, dka, qn, kn, rq, rk, fgb, qg512, kg512, e512, et512, tm):
    S = qn.shape[0]
    nt = S // tm

    def body(dqa_ref, dka_ref, qn_ref, kn_ref, rq_ref, rk_ref, fgb_ref, qg_ref, kg_ref, e_ref, et_ref,
             dq_ref, dk_ref, dfg_ref, accg_ref, accb_ref, carry_ref):
        @pl.when(pl.program_id(0) == 0)
        def _():
            carry_ref[...] = jnp.zeros_like(carry_ref)
            accg_ref[...] = jnp.zeros_like(accg_ref)
            accb_ref[...] = jnp.zeros_like(accb_ref)

        lane = _lane((tm, LANES))
        sel_a = lane < DH
        df = jnp.zeros((tm, LANES), F32)
        for h in range(NH):
            col = dqa_ref[h][:, 64:65] - dka_ref[h][:, 67:68]
            df = jnp.where(lane == h, col, df)
        tri = (lax.broadcasted_iota(jnp.int32, (tm, tm), 0) <= lax.broadcasted_iota(jnp.int32, (tm, tm), 1)
               ).astype(F32).astype(BF16)
        dlf = _dot3(tri, df) + carry_ref[0:1, :]
        carry_ref[...] = jnp.broadcast_to(dlf[0:1, :], carry_ref.shape)
        dfg = jnp.where(lane < NH, dlf * _sigmoid(-fgb_ref[...]), 0.0)
        dfg_ref[...] = dfg.astype(BF16)
        accb_ref[...] += _colsum8(dfg)

        def norm_bwd(src_ref, n_ref, r_ref, g_ref, scale, slot):
            pairs = []
            for p in range(NH // 2):
                b = pltpu.roll(src_ref[2 * p + 1], 64, 1)
                pairs.append(jnp.where(sel_a, src_ref[2 * p], b))
            dh = jnp.concatenate(pairs, axis=1) * scale
            tn = n_ref[...].astype(F32)
            accg_ref[slot] += _colsum8(dh * tn)
            dn = dh * g_ref[...]
            mean = _dot2(dn * tn, e_ref[...]) * (1.0 / DH)
            corr = _dot2(mean, et_ref[...])
            rf = _dot2(r_ref[...], et_ref[...])
            return (rf * (dn - tn * corr)).astype(BF16)

        dq_ref[...] = norm_bwd(dqa_ref, qn_ref, rq_ref, qg_ref, DH ** -0.5, 0)
        dk_ref[...] = norm_bwd(dka_ref, kn_ref, rk_ref, kg_ref, 1.0, 1)

    rev = lambda n: pl.BlockSpec((tm, n), lambda i: (nt - 1 - i, 0))
    hm = pl.BlockSpec((NH, tm, LANES), lambda i: (0, nt - 1 - i, 0))
    dq, dk, dfg, accg, accb = pl.pallas_call(
        body, name="bwd_qk", grid=(nt,),
        out_shape=(_sds((S, AW), BF16), _sds((S, AW), BF16), _sds((S, LANES), BF16), _sds((2, 8, AW), F32),
                   _sds((8, LANES), F32)),
        in_specs=[hm, hm, rev(AW), rev(AW), rev(LANES), rev(LANES), rev(LANES), _const((1, AW)), _const((1, AW)),
                  _const((AW, LANES)), _const((LANES, AW))],
        out_specs=(rev(AW), rev(AW), rev(LANES), _const((2, 8, AW)), _const((8, LANES))),
        scratch_shapes=[pltpu.VMEM((8, LANES), F32)], compiler_params=_cp(("arbitrary",)),
    )(dqa, dka, qn, kn, rq, rk, fgb, qg512, kg512, e512, et512)
    return dq, dk, dfg, accg, accb


def _bwd_in(dq, dk, dv, dalin, dagate, dfg, w_in_p, x, dx2, mod8, n1g, tm, ex=None):
    S = x.shape[0]

    def body(dq_ref, dk_ref, dv_ref, dal_ref, dag_ref, dfg_ref, w_ref, x_ref, dx2_ref, mod_ref, n1g_ref,
             dx_ref, acc_ref):
        @pl.when(pl.program_id(0) == 0)
        def _():
            acc_ref[...] = jnp.zeros_like(acc_ref)

        def part(ref, a, b):
            return lax.dot_general(ref[...], w_ref[:, a:b], NT, preferred_element_type=F32)

        dh = (part(dq_ref, 0, 512) + part(dk_ref, 512, 1024) + part(dv_ref, 1024, 1536) + part(dal_ref, 1536, 2048)
              + part(dag_ref, 2048, 2560) + part(dfg_ref, 2560, NP))
        xv = x_ref[...]
        r1 = lax.rsqrt(jnp.mean(xv * xv, axis=-1, keepdims=True) + EPS)
        xn = xv * r1
        gain = n1g_ref[...] * (1.0 + mod_ref[1:2, :])
        t = dh * xn
        acc_ref[0] += _colsum8(dh)
        acc_ref[1] += _colsum8(t * n1g_ref[...])
        acc_ref[2] += _colsum8(t * (1.0 + mod_ref[1:2, :]))
        dxn = dh * gain
        dx_ref[...] = dx2_ref[...] + r1 * (dxn - xn * jnp.mean(dxn * xn, axis=-1, keepdims=True))

    outs = pl.pallas_call(
        _hosted(body, ex, 11, 2, 0, S // tm), name="bwd_in", grid=(S // tm,),
        out_shape=(_sds((S, D), F32), _sds((3, 8, D), F32), *(ex.out_shapes if ex else [])),
        in_specs=[_rows(tm, AW), _rows(tm, AW), _rows(tm, AW), _rows(tm, CW), _rows(tm, CW), _rows(tm, LANES),
                  _const((D, NP)), _rows(tm, D), _rows(tm, D), _const((8, D)), _const((1, D)),
                  *(ex.in_specs if ex else [])],
        out_specs=(_rows(tm, D), _const((3, 8, D)), *(ex.out_specs if ex else [])),
        scratch_shapes=ex.scratch if ex else [],
        compiler_params=_cp(("arbitrary",)),
    )(dq, dk, dv, dalin, dagate, dfg, w_in_p, x, dx2, mod8, n1g, *(ex.srcs if ex else []))
    return outs[:2], list(outs[2:])


def _adam(w, g, m, v):
    m_new = B1 * m + (1.0 - B1) * g
    v_new = B2 * v + (1.0 - B2) * (g * g)
    m_hat = m_new / (1.0 - B1 ** STEP)
    v_hat = v_new / (1.0 - B2 ** STEP)
    delta = -LR * (m_hat / (jnp.sqrt(v_hat) + AEPS) + WD * w)
    return delta, m_new, v_new


def _reduce_adamw(slots, w, m, v, name, tr=256):
    ns, R, C = slots.shape
    tr = tr if R % tr == 0 else R

    def body(s_ref, w_ref, m_ref, v_ref, g_ref, d_ref, mo_ref, vo_ref):
        g = s_ref[0].astype(F32)
        for k in range(1, ns):
            g = g + s_ref[k].astype(F32)
        g_ref[...] = g
        d_ref[...], mo_ref[...], vo_ref[...] = _adam(w_ref[...], g, m_ref[...], v_ref[...])

    blk = pl.BlockSpec((tr, C), lambda i: (i, 0))
    return pl.pallas_call(
        body, name=name, grid=(R // tr,), out_shape=tuple(_sds((R, C), F32) for _ in range(4)),
        in_specs=[pl.BlockSpec((ns, tr, C), lambda i: (0, i, 0)), blk, blk, blk], out_specs=(blk, blk, blk, blk),
        compiler_params=_cp(("parallel",)),
    )(slots, w, m, v)


def _pair_adamw(slots, w, m, v, name, tr=256):
    ns, R, C = slots.shape
    tr = tr if R % tr == 0 else R
    nt = R // tr

    def body(s_ref, w_ref, m_ref, v_ref, g_ref, d_ref, mo_ref, vo_ref, mine_ref, theirs_ref, send_sems, recv_sems):
        i = pl.program_id(0)
        part = s_ref[0].astype(F32)
        for k in range(1, ns):
            part = part + s_ref[k].astype(F32)
        mine_ref[i] = part
        swap = pltpu.make_async_remote_copy(
            src_ref=mine_ref.at[i], dst_ref=theirs_ref.at[i], send_sem=send_sems.at[i], recv_sem=recv_sems.at[i],
            device_id=(lax.axis_index("x"), lax.axis_index("y"), 1 - lax.axis_index("c")),
            device_id_type=pl.DeviceIdType.MESH)
        swap.start()
        swap.wait()
        g = part + theirs_ref[i]
        g_ref[...] = g
        d_ref[...], mo_ref[...], vo_ref[...] = _adam(w_ref[...], g, m_ref[...], v_ref[...])

    blk = pl.BlockSpec((tr, C), lambda i: (i, 0))
    return pl.pallas_call(
        body, name=name, grid=(nt,), out_shape=tuple(_sds((R, C), F32) for _ in range(4)),
        in_specs=[pl.BlockSpec((ns, tr, C), lambda i: (0, i, 0)), blk, blk, blk], out_specs=(blk, blk, blk, blk),
        scratch_shapes=[pltpu.VMEM((nt, tr, C), F32), pltpu.VMEM((nt, tr, C), F32),
                        pltpu.SemaphoreType.DMA((nt,)), pltpu.SemaphoreType.DMA((nt,))],
        compiler_params=_cp(("arbitrary",)),
    )(slots, w, m, v)


def _ada_adamw(sct, dmod, w, m, v):
    R, C = w.shape
    tr, bc = 256, 512

    def body(sct_ref, dm_ref, w_ref, m_ref, v_ref, g_ref, d_ref, mo_ref, vo_ref):
        g = sct_ref[:, 0:1] * dm_ref[0:1, :]
        for b in range(1, N_DEV):
            g = g + sct_ref[:, b:b + 1] * dm_ref[b:b + 1, :]
        g_ref[...] = g
        d_ref[...], mo_ref[...], vo_ref[...] = _adam(w_ref[...], g, m_ref[...], v_ref[...])

    blk = pl.BlockSpec((tr, bc), lambda i, j: (i, j))
    return pl.pallas_call(
        body, name="ada_adamw", grid=(R // tr, C // bc), out_shape=tuple(_sds((R, C), F32) for _ in range(4)),
        in_specs=[pl.BlockSpec((tr, N_DEV), lambda i, j: (i, 0)), pl.BlockSpec((N_DEV, bc), lambda i, j: (0, j)),
                  blk, blk, blk],
        out_specs=(blk, blk, blk, blk), compiler_params=_cp(("parallel", "parallel")),
    )(sct, dmod, w, m, v)


def _small_reduce(slots, fold):
    def body(s_ref, f_ref, o_ref):
        tot = s_ref[0:1, :]
        for k in range(1, N_DEV):
            tot = tot + s_ref[k:k + 1, :]
        o_ref[:, 0:6144] = tot[:, 0:6144]
        o_ref[:, 6144:7168] = tot[:, 6144:7168]
        for t, src in enumerate((8192, 8704)):
            v8 = jnp.broadcast_to(tot[:, src:src + AW], (8, AW))
            o_ref[:, 7168 + t * LANES:7168 + (t + 1) * LANES] = jnp.dot(
                v8, f_ref[...], precision=HI, preferred_element_type=F32)[0:1, :]
        o_ref[:, 7424:7552] = tot[:, 9216:9344]
        o_ref[:, 7552:10112] = tot[:, 9344:11904]
        o_ref[:, 10112:SMALL_OUT] = tot[:, 7168:8192]

    return pl.pallas_call(
        body, name="small_reduce", out_shape=_sds((1, SMALL_OUT), F32),
        in_specs=[pl.BlockSpec(memory_space=pltpu.VMEM), pl.BlockSpec(memory_space=pltpu.VMEM)],
        out_specs=pl.BlockSpec(memory_space=pltpu.VMEM),
    )(slots, fold)


def _perm_in(w):
    pad = jnp.zeros((w.shape[0], NP - 2568), w.dtype)
    return jnp.concatenate([w[:, :1536], w[:, 1544:2568], w[:, 1536:1544], pad], axis=1)


def _pad_lanes(vec, n=LANES):
    return jnp.pad(vec, ((0, 0), (0, n - vec.shape[1])))


def kernel(x, c, w_ada, b_ada, norm1_g, w_in, q_norm_g, k_norm_g, b_f, conv_w, conv_b, conv_ln_g, conv_ln_b, beta_attn, beta_conv, w_out, norm2_g, w_ff1, w_ff2, loss_target, m_w_ada, m_b_ada, m_norm1_g, m_w_in, m_q_norm_g, m_k_norm_g, m_b_f, m_conv_w, m_conv_b, m_conv_ln_g, m_conv_ln_b, m_beta_attn, m_beta_conv, m_w_out, m_norm2_g, m_w_ff1, m_w_ff2, v_w_ada, v_b_ada, v_norm1_g, v_w_in, v_q_norm_g, v_k_norm_g, v_b_f, v_conv_w, v_conv_b, v_conv_ln_g, v_conv_ln_b, v_beta_attn, v_beta_conv, v_w_out, v_norm2_g, v_w_ff1, v_w_ff2):
    S = x.shape[1]
    tm = min(256, S)
    tw = min(512, S)
    tq = min(512, S // 2)
    xs, tgt = x[0], loss_target[0]
    chip = 2 * lax.axis_index("x") + lax.axis_index("y")
    e512, et512 = _head_sum_mats()

    conv_w32 = jnp.pad(conv_w[0], ((0, 1), (0, 0)))
    c_all, g_in = _exchange([(c, "bcast8"), (w_in[0].astype(BF16), "chip4")], "gather_in")
    later_weights = _Exchange([(w_out[0].astype(BF16), "chip4"), (w_ff1[0].astype(BF16), "chip4"),
                               (w_ff2[0].astype(BF16), "chip4"), (conv_w32, "chip4")])
    c_all = c_all.reshape(N_DEV, D)
    w_in_p = _perm_in(jnp.transpose(g_in, (1, 0, 2)).reshape(D, 2568))

    b_shard = lax.dynamic_slice(b_ada, (0, chip * 1536), (1, 1536))
    mod_rows, sc_all = _mod_shard(c_all, w_ada[0], b_shard)
    (mod_slots,) = _exchange([(mod_rows.reshape(N_DEV, 1, 1536), "all8")], "scatter_mod")
    mod = mod_slots.reshape(4, 2, 1536)[:, 0, :].reshape(6, D)
    mod8 = jnp.pad(mod, ((0, 2), (0, 0)))

    qg512 = jnp.tile(q_norm_g, (1, NH))
    kg512 = jnp.tile(k_norm_g, (1, NH))
    bf128 = _pad_lanes(b_f)

    (h1, qh, kh, vb, qn, kn, rq, rk, fgb, alin, agate, u0), (g_out, g_ff1, g_ff2, g_cw) = _fwd_in(
        xs, mod8, norm1_g, w_in_p, e512, et512, qg512, kg512, bf128, tw, ex=later_weights)
    w_out_f = g_out.reshape(D, D)
    w1, w2 = g_ff1, g_ff2
    cw32 = jnp.transpose(g_cw, (1, 0, 2)).reshape(HALO, CW)
    bd = _logit_bound(q_norm_g, k_norm_g)
    qa, ka, va, fcum = _fwd_decay(fgb, qh, kh, vb, _shift(bd), tm)
    fs, fe = _skip_tables(fcum, tq)
    o_attn, lser = _attn_fwd(qa, ka, va, fs, fe, bd, tq)
    u1, mc = _fwd_conv(u0, cw32, conv_b, conv_ln_g, conv_ln_b, beta_conv, tm)
    merged, ob, x2, h2 = _fwd_out(o_attn, mc, xs, mod8, norm2_g, beta_attn, w_out_f, tw)
    r, dy, loss8, dg2 = _fwd_ffn(h2, w1, w2, x2, tgt, mod8, tw)

    df2, df1, dh2 = _bwd_ffn(dy, mod8, r, w1, w2, tw)
    gw_ff2 = _wgrad(r, df2, "wgrad_ff2", square_a=True)
    gw_ff1 = _wgrad(h2, df1, "wgrad_ff1", col_pieces=True)
    ff_grads = _Exchange([(gw_ff1, "chip4p"), (gw_ff2.reshape(4, D, D), "chip4p")])
    (dx2, do, doa, du1, acc_d, acc_h, dr), (p_ff1, p_ff2) = _bwd_mid(
        dh2, dy, x2, ob, o_attn, u1, mod8, norm2_g, beta_attn, beta_conv, conv_ln_g, conv_ln_b, w_out_f, e512, tm, tq,
        ex=ff_grads)
    gw_out = _wgrad(merged, do, "wgrad_out")
    dqa, dka, dv = _attn_bwd(qa, ka, vb, doa, lser, dr, fs, fe, bd, tq)
    out_grads = _Exchange([(gw_out.reshape(4, 256, D), "chip4p")])
    (dalin, dagate, dcw, dcb), (p_out,) = _bwd_conv(du1, u0, alin, agate, cw32, tm, ex=out_grads)
    dq, dk, dfg, accg, accb = _bwd_qk(dqa, dka, qn, kn, rq, rk, fgb, qg512, kg512, e512, et512, tm)
    gw_in_p = _wgrad_in(h1, [dq, dk, dv, dalin, dagate, dfg])
    gw_in = jnp.concatenate([gw_in_p[:, :1536], gw_in_p[:, 2560:2568], gw_in_p[:, 1536:2560]], axis=1)
    s8 = lambda a: jnp.sum(a, axis=-2)
    gcw = s8(dcw)
    in_grads = _Exchange([(jnp.transpose(gw_in.reshape(D, 4, 642), (1, 0, 2)), "chip4p"),
                          (jnp.transpose(gcw.reshape(HALO, 4, LANES), (1, 0, 2)), "chip4p")])
    (grad_x, acc1), (p_in, p_cw) = _bwd_in(dq, dk, dv, dalin, dagate, dfg, w_in_p, xs, dx2, mod8, norm1_g, tw,
                                           ex=in_grads)

    a1, ad, ah = s8(acc1), s8(acc_d), s8(acc_h)
    small = jnp.concatenate(
        [a1[0], a1[1], ad[3], ad[0], ad[1], s8(dg2),
         a1[2], ad[2], s8(accg).reshape(-1), s8(accb), s8(dcb), ah[2], ah[3], ah[0], ah[1]]).reshape(1, SMALL_IN)
    (small_s,) = _exchange([(small, "bcast8")], "gather_small")
    small_s = small_s.reshape(N_DEV, SMALL_IN)

    g_in_, d_in, nm_in, nv_in = _pair_adamw(p_in, w_in[0], m_w_in[0], v_w_in[0], "adamw_in")
    g_out_, d_out, nm_out, nv_out = _pair_adamw(p_out, w_out[0], m_w_out[0], v_w_out[0], "adamw_out")
    g_f1, d_f1, nm_f1, nv_f1 = _pair_adamw(p_ff1, w_ff1[0], m_w_ff1[0], v_w_ff1[0], "adamw_ff1")
    g_f2, d_f2, nm_f2, nv_f2 = _pair_adamw(p_ff2, w_ff2[0], m_w_ff2[0], v_w_ff2[0], "adamw_ff2")
    pad_row = lambda a, fill: jnp.pad(a[0], ((0, 1), (0, 0)), constant_values=fill)
    g_cw_, d_cw, nm_cw, nv_cw = (a[:KC] for a in _pair_adamw(
        p_cw, pad_row(conv_w, 0.0), pad_row(m_conv_w, 0.0), pad_row(v_conv_w, 1.0), "adamw_conv_w"))
    dmod_shard = lax.dynamic_slice(small_s[:, :6 * D], (0, chip * 1536), (N_DEV, 1536))
    g_ada, d_ada, nm_ada, nv_ada = _ada_adamw(sc_all.T, dmod_shard, w_ada[0], m_w_ada[0], v_w_ada[0])

    fold = np.zeros((AW, LANES), np.float32)
    fold[np.arange(AW), np.arange(AW) % DH] = 1.0
    g_small = _small_reduce(small_s, jnp.asarray(fold))
    smalls = [b_ada, norm1_g, q_norm_g, k_norm_g, b_f, conv_b, conv_ln_g, conv_ln_b, beta_attn, beta_conv, norm2_g]
    m_smalls = [m_b_ada, m_norm1_g, m_q_norm_g, m_k_norm_g, m_b_f, m_conv_b, m_conv_ln_g, m_conv_ln_b, m_beta_attn,
                m_beta_conv, m_norm2_g]
    v_smalls = [v_b_ada, v_norm1_g, v_q_norm_g, v_k_norm_g, v_b_f, v_conv_b, v_conv_ln_g, v_conv_ln_b, v_beta_attn,
                v_beta_conv, v_norm2_g]
    widths = [a.shape[1] for a in smalls]
    padded = [-(-n // LANES) * LANES for n in widths]
    pack = lambda arrs, fill: jnp.concatenate(
        [jnp.pad(a, ((0, 0), (0, p - a.shape[1])), constant_values=fill) for a, p in zip(arrs, padded)], axis=1)
    outs_small = _reduce_adamw(g_small.reshape(1, 1, SMALL_OUT), pack(smalls, 0.0), pack(m_smalls, 0.0),
                               pack(v_smalls, 1.0), "adamw_small")
    offs = np.concatenate([[0], np.cumsum(padded)])

    def unpack(a):
        return [a[:, int(o):int(o) + n] for o, n in zip(offs[:-1], widths)]

    gs, ds, ms, vs = (unpack(a) for a in outs_small)

    loss = lax.psum(loss8[0, 0], ("x", "y", "c"))
    big = {"w_ada": (g_ada, d_ada, nm_ada, nv_ada), "w_in": (g_in_, d_in, nm_in, nv_in),
           "conv_w": (g_cw_, d_cw, nm_cw, nv_cw), "w_out": (g_out_, d_out, nm_out, nv_out),
           "w_ff1": (g_f1, d_f1, nm_f1, nv_f1), "w_ff2": (g_f2, d_f2, nm_f2, nv_f2)}
    small_names = ["b_ada", "norm1_g", "q_norm_g", "k_norm_g", "b_f", "conv_b", "conv_ln_g", "conv_ln_b", "beta_attn",
                   "beta_conv", "norm2_g"]
    order = ["w_ada", "b_ada", "norm1_g", "w_in", "q_norm_g", "k_norm_g", "b_f", "conv_w", "conv_b", "conv_ln_g",
             "conv_ln_b", "beta_attn", "beta_conv", "w_out", "norm2_g", "w_ff1", "w_ff2"]

    def leaf(name, which):
        if name in big:
            return big[name][which][None]
        return (gs, ds, ms, vs)[which][small_names.index(name)]

    return (loss, grad_x[None], *[leaf(n, 0) for n in order], *[leaf(n, 1) for n in order],
            *[leaf(n, 2) for n in order], *[leaf(n, 3) for n in order])
```

```python
import functools

import numpy as np
import jax
import jax.numpy as jnp
from jax import lax
from jax.experimental import pallas as pl
from jax.experimental.pallas import tpu as pltpu

F32, BF16 = jnp.float32, jnp.bfloat16
HI = lax.Precision.HIGHEST
D = 1024
AW = 512
CW = 512
NH = 8
DH = 64
KC = 31
DFF = 4096
NP = 2688
EPS = 1e-6
NEG = -1e30
LANES = 128
VMEM_LIMIT = 56 * 2**20
NT = (((1,), (1,)), ((), ()))
TN = (((0,), (0,)), ((), ()))
LR, B1, B2, AEPS, WD, STEP = 0.001, 0.9, 0.999, 1e-08, 0.01, 10
N_DEV = 8
SMALL_IN = 11904
SMALL_OUT = 11136


def _cp(sem=None, vmem=VMEM_LIMIT):
    kw = dict(vmem_limit_bytes=vmem)
    if sem is not None:
        kw["dimension_semantics"] = sem
    return pltpu.CompilerParams(**kw)


def _rows(tm, n):
    return pl.BlockSpec((tm, n), lambda i: (i, 0))


def _const(shape):
    nd = len(shape)
    return pl.BlockSpec(shape, lambda *_: (0,) * nd)


def _sds(shape, dt):
    return jax.ShapeDtypeStruct(shape, dt)


def _lane(shape):
    return lax.broadcasted_iota(jnp.int32, shape, len(shape) - 1)


def _sigmoid(x):
    return 1.0 / (1.0 + jnp.exp(-x))


class _Exchange:
    MASKS = {"chip4": (2, 4, 6), "chip4p": (2, 4, 6), "all8": (1, 2, 3, 4, 5, 6, 7), "bcast8": (1, 2, 3, 4, 5, 6, 7)}

    def __init__(self, items):
        self.srcs = [s for s, _ in items]
        self.kinds = [k for _, k in items]
        self.n = len(items)
        self.out_shapes = []
        for s, k in items:
            shape = {"all8": (N_DEV,) + s.shape[1:], "bcast8": (N_DEV,) + s.shape, "chip4": (4,) + s.shape,
                     "chip4p": (4,) + s.shape[1:]}[k]
            self.out_shapes.append(_sds(shape, s.dtype))
        self.sem_index = {}
        for t, k in enumerate(self.kinds):
            for m in self.MASKS[k]:
                self.sem_index[(t, m)] = len(self.sem_index)
        n_sem = len(self.sem_index)
        self.scratch = [pltpu.SemaphoreType.DMA((n_sem,)), pltpu.SemaphoreType.DMA((n_sem,)),
                        pltpu.SemaphoreType.DMA((self.n,))]
        self.in_specs = [pl.BlockSpec(memory_space=pl.ANY)] * self.n
        self.out_specs = [pl.BlockSpec(memory_space=pl.ANY)] * self.n

    def copies(self, src_refs, dst_refs, send_sems, recv_sems, local_sems):
        x, y, c = lax.axis_index("x"), lax.axis_index("y"), lax.axis_index("c")
        my_id = 4 * x + 2 * y + c
        my_chip = 2 * x + y

        def piece(t, dev_id, chip):
            k = self.kinds[t]
            return src_refs[t].at[dev_id] if k == "all8" else src_refs[t].at[chip] if k == "chip4p" else src_refs[t]

        out = []
        for t in range(self.n):
            slot = dst_refs[t].at[my_chip if self.kinds[t] in ("chip4", "chip4p") else my_id]
            out.append(pltpu.make_async_copy(piece(t, my_id, my_chip), slot, local_sems.at[t]))
            for m in self.MASKS[self.kinds[t]]:
                px = 1 - x if m & 4 else x
                py = 1 - y if m & 2 else y
                pc = 1 - c if m & 1 else c
                s = self.sem_index[(t, m)]
                out.append(pltpu.make_async_remote_copy(
                    src_ref=piece(t, 4 * px + 2 * py + pc, 2 * px + py), dst_ref=slot,
                    send_sem=send_sems.at[s], recv_sem=recv_sems.at[s],
                    device_id=(px, py, pc), device_id_type=pl.DeviceIdType.MESH))
        return out


def _hosted(body, ex, n_in, n_out, n_scr, n_steps):
    if ex is None:
        return body

    def wrapped(*refs):
        ins, xin = refs[:n_in], refs[n_in:n_in + ex.n]
        o0 = n_in + ex.n
        outs, xout = refs[o0:o0 + n_out], refs[o0 + n_out:o0 + n_out + ex.n]
        s0 = o0 + n_out + ex.n
        scr, sems = refs[s0:s0 + n_scr], refs[s0 + n_scr:]

        @pl.when(pl.program_id(0) == 0)
        def _():
            for cp in ex.copies(xin, xout, *sems):
                cp.start()

        body(*ins, *outs, *scr)

        @pl.when(pl.program_id(0) == n_steps - 1)
        def _():
            for cp in ex.copies(xin, xout, *sems):
                cp.wait()

    return wrapped


def _exchange(items, name):
    ex = _Exchange(items)
    n = ex.n

    def body(*refs):
        copies = ex.copies(refs[:n], refs[n:2 * n], *refs[2 * n:])
        for cp in copies:
            cp.start()
        for cp in copies:
            cp.wait()

    outs = pl.pallas_call(
        body, name=name, out_shape=tuple(ex.out_shapes), in_specs=ex.in_specs, out_specs=tuple(ex.out_specs),
        scratch_shapes=ex.scratch,
    )(*ex.srcs)
    return list(outs)


def _mod_shard(c_all, w_ada, b_shard):
    n = w_ada.shape[1]

    def body(c_ref, w_ref, b_ref, o_ref, sc_ref):
        cv = c_ref[...]
        sc = cv * _sigmoid(cv)
        sc_ref[...] = sc
        o_ref[...] = jnp.dot(sc, w_ref[...], precision=HI, preferred_element_type=F32) + b_ref[...]

    bn = 512
    return pl.pallas_call(
        body, name="mod_shard", out_shape=(_sds((N_DEV, n), F32), _sds((N_DEV, D), F32)), grid=(n // bn,),
        in_specs=[_const((N_DEV, D)), pl.BlockSpec((D, bn), lambda j: (0, j)), pl.BlockSpec((1, bn), lambda j: (0, j))],
        out_specs=(pl.BlockSpec((N_DEV, bn), lambda j: (0, j)), _const((N_DEV, D))),
        compiler_params=_cp(("arbitrary",)),
    )(c_all, w_ada, b_shard)


def _head_sum_mats():
    e = np.zeros((AW, LANES), np.float32)
    for h in range(NH):
        e[h * DH:(h + 1) * DH, h] = 1.0
    return jnp.asarray(e, BF16), jnp.asarray(e.T.copy(), BF16)


def _dot2(x, w):
    hi = x.astype(BF16)
    lo = (x - hi.astype(F32)).astype(BF16)
    return jnp.dot(hi, w, preferred_element_type=F32) + jnp.dot(lo, w, preferred_element_type=F32)


def _fwd_in(x, mod8, n1g, w_in_p, e512, et512, qg512, kg512, bf128, tm, ex=None):
    S = x.shape[0]

    def body(x_ref, mod_ref, n1g_ref, w_ref, e_ref, et_ref, qg_ref, kg_ref, bf_ref,
             h1_ref, qh_ref, kh_ref, v_ref, qn_ref, kn_ref, rq_ref, rk_ref, fgb_ref, alin_ref, agate_ref, u0_ref):
        xv = x_ref[...]
        r1 = lax.rsqrt(jnp.mean(xv * xv, axis=-1, keepdims=True) + EPS)
        h = (xv * r1) * (n1g_ref[...] * (1.0 + mod_ref[1:2, :])) + mod_ref[0:1, :]
        hb = h.astype(BF16)
        h1_ref[...] = hb

        def seg(a, b):
            return jnp.dot(hb, w_ref[:, a:b], preferred_element_type=F32)

        def headnorm(t, g_ref, scale, n_ref, r_ref, o_ref):
            ss = _dot2(t * t, e_ref[...])
            r = lax.rsqrt(ss * (1.0 / DH) + EPS)
            tn = t * _dot2(r, et_ref[...])
            n_ref[...] = tn.astype(BF16)
            r_ref[...] = r
            o_ref[...] = (tn * (g_ref[...] * scale)).astype(BF16)

        headnorm(seg(0, 512), qg_ref, DH ** -0.5, qn_ref, rq_ref, qh_ref)
        headnorm(seg(512, 1024), kg_ref, 1.0, kn_ref, rk_ref, kh_ref)
        v_ref[...] = seg(1024, 1536).astype(BF16)
        alin = seg(1536, 2048)
        agate = seg(2048, 2560)
        alin_ref[...] = alin.astype(BF16)
        agate_ref[...] = agate.astype(BF16)
        u0_ref[...] = alin * _sigmoid(agate)
        fgb_ref[...] = seg(2560, NP) + bf_ref[...]

    bf = lambda: _sds((S, AW), BF16)
    xs = ex.srcs if ex else []
    outs = pl.pallas_call(
        _hosted(body, ex, 9, 12, 0, S // tm), name="fwd_in", grid=(S // tm,),
        out_shape=(_sds((S, D), BF16), bf(), bf(), bf(), bf(), bf(), _sds((S, LANES), F32), _sds((S, LANES), F32),
                   _sds((S, LANES), F32), bf(), bf(), _sds((S, CW), F32), *(ex.out_shapes if ex else [])),
        in_specs=[_rows(tm, D), _const((8, D)), _const((1, D)), _const((D, NP)), _const((AW, LANES)), _const((LANES, AW)),
                  _const((1, AW)), _const((1, AW)), _const((1, LANES)), *(ex.in_specs if ex else [])],
        out_specs=(_rows(tm, D), _rows(tm, AW), _rows(tm, AW), _rows(tm, AW), _rows(tm, AW), _rows(tm, AW),
                   _rows(tm, LANES), _rows(tm, LANES), _rows(tm, LANES), _rows(tm, AW), _rows(tm, AW), _rows(tm, CW),
                   *(ex.out_specs if ex else [])),
        scratch_shapes=ex.scratch if ex else [],
        compiler_params=_cp(("arbitrary",)),
    )(x, mod8, n1g, w_in_p, e512, et512, qg512, kg512, bf128, *xs)
    return outs[:12], list(outs[12:])


def _split3(f):
    f1 = f.astype(BF16).astype(F32)
    f2 = (f - f1).astype(BF16).astype(F32)
    return f1, f2, f - f1 - f2


def _dot3(w, x):
    return sum(jnp.dot(w, piece.astype(BF16), preferred_element_type=F32) for piece in _split3(x))


def _fwd_decay(fgb, qh, kh, vb, shift, tm, ex=None):
    S = fgb.shape[0]

    def body(shift_ref, fgb_ref, qh_ref, kh_ref, vb_ref, qa_ref, ka_ref, va_ref, f_ref, carry_ref):
        @pl.when(pl.program_id(0) == 0)
        def _():
            carry_ref[...] = jnp.zeros_like(carry_ref)

        fb = fgb_ref[...]
        lf = jnp.minimum(fb, 0.0) - jnp.log1p(jnp.exp(-jnp.abs(fb)))
        tri = (lax.broadcasted_iota(jnp.int32, (tm, tm), 0) >= lax.broadcasted_iota(jnp.int32, (tm, tm), 1)
               ).astype(F32).astype(BF16)
        cs = _dot3(tri, lf) + carry_ref[0:1, :]
        f_ref[...] = cs
        carry_ref[...] = jnp.broadcast_to(cs[tm - 1:tm, :], carry_ref.shape)
        lane = _lane((tm, LANES))
        s1, s2, s3 = _split3(jnp.zeros((tm, LANES), F32) - shift_ref[0, 0])
        tail_q = jnp.where((lane >= 67) & (lane < 70), 1.0,
                           jnp.where(lane == 70, s1, jnp.where(lane == 71, s2, jnp.where(lane == 72, s3, 0.0))))
        tail_k = jnp.where(((lane >= 64) & (lane < 67)) | ((lane >= 70) & (lane < 73)), 1.0, 0.0)
        tail_v = jnp.where(lane == DH, 1.0, 0.0)
        for p in range(NH // 2):
            qp = qh_ref[:, p * LANES:(p + 1) * LANES].astype(F32)
            kp = kh_ref[:, p * LANES:(p + 1) * LANES].astype(F32)
            vp = vb_ref[:, p * LANES:(p + 1) * LANES].astype(F32)
            for hh in range(2):
                h = 2 * p + hh
                f1, f2, f3 = _split3(cs[:, h:h + 1])
                qb = qp if hh == 0 else pltpu.roll(qp, 64, 1)
                kb = kp if hh == 0 else pltpu.roll(kp, 64, 1)
                vh = vp if hh == 0 else pltpu.roll(vp, 64, 1)
                augq = jnp.where(lane == 64, f1, jnp.where(lane == 65, f2, jnp.where(lane == 66, f3, tail_q)))
                augk = jnp.where(lane == 67, -f1, jnp.where(lane == 68, -f2, jnp.where(lane == 69, -f3, tail_k)))
                qa_ref[h] = jnp.where(lane < DH, qb, augq).astype(BF16)
                ka_ref[h] = jnp.where(lane < DH, kb, augk).astype(BF16)
                va_ref[h] = jnp.where(lane < DH, vh, tail_v).astype(BF16)

    hm = pl.BlockSpec((NH, tm, LANES), lambda i: (0, i, 0))
    hms = _sds((NH, S, LANES), BF16)
    outs = pl.pallas_call(
        _hosted(body, ex, 5, 4, 1, S // tm), name="fwd_decay", grid=(S // tm,),
        out_shape=(hms, hms, hms, _sds((S, LANES), F32), *(ex.out_shapes if ex else [])),
        in_specs=[SMEM_SPEC, _rows(tm, LANES), _rows(tm, AW), _rows(tm, AW), _rows(tm, AW),
                  *(ex.in_specs if ex else [])],
        out_specs=(hm, hm, hm, _rows(tm, LANES), *(ex.out_specs if ex else [])),
        scratch_shapes=[pltpu.VMEM((8, LANES), F32), *(ex.scratch if ex else [])],
        compiler_params=_cp(("arbitrary",)),
    )(shift, fgb, qh, kh, vb, *(ex.srcs if ex else []))
    return outs[:4], list(outs[4:])


SKIP = -106.0


def _block_loops(first, step, needed, run):
    def both(j):
        return jnp.logical_and(needed(0, j), needed(1, j))

    def walk(heads):
        def go(j):
            run(j, heads)
            return j + step
        return go

    j = lax.while_loop(both, walk((0, 1)), first)
    lax.while_loop(functools.partial(needed, 0), walk((0,)), j)
    lax.while_loop(functools.partial(needed, 1), walk((1,)), j)


def _logit_bound(qg, kg):
    return (2.0 * 1.03 * DH ** 0.5 * jnp.max(jnp.abs(qg)) * jnp.max(jnp.abs(kg))).reshape(1, 1)


def _skip_tables(f, tq):
    return f[0::tq, :NH].T, f[tq - 1::tq, :NH].T


SMEM_SPEC = pl.BlockSpec(memory_space=pltpu.SMEM)


def _causal_rect(rows, cols, col0):
    return (lax.broadcasted_iota(jnp.int32, (rows, cols), 0)
            >= lax.broadcasted_iota(jnp.int32, (rows, cols), 1) + col0)


def _chunk(tq):
    return tq


SHIFT_MAX = 60.0


def _shift(bd):
    return jnp.where(bd <= SHIFT_MAX, 0.5 * bd, 0.0)


def _attn_fwd(qa, ka, va, fs, fe, bd, tq):
    S = qa.shape[1]
    nq = S // tq

    def body(fs_ref, fe_ref, bd_ref, qa_ref, ka_ref, va_ref, o_ref, lser_ref, m_ref, acc_ref):
        pr, i = pl.program_id(0), pl.program_id(1)
        sel_a = _lane((tq, LANES)) < DH
        acc_ref[...] = jnp.zeros_like(acc_ref)

        def logits(j, hh, masked):
            start = pl.multiple_of(j * tq, tq)
            s = lax.dot_general(qa_ref[hh], ka_ref[hh, pl.ds(start, tq), :], NT, preferred_element_type=F32)
            if masked:
                s = jnp.where(_causal_rect(tq, tq, 0), s, NEG)
            return s, va_ref[hh, pl.ds(start, tq), :]

        def shifted_step(j, heads, masked=False):
            for hh in heads:
                s, vb = logits(j, hh, masked)
                acc_ref[hh] += jnp.dot(jnp.exp(s).astype(BF16), vb, preferred_element_type=F32)

        def online_step(j, heads, masked=False):
            for hh in heads:
                s, vb = logits(j, hh, masked)
                m_prev = m_ref[hh]
                m_new = jnp.maximum(m_prev, jnp.max(s, axis=1, keepdims=True))
                p = jnp.exp(s - jnp.tile(m_new, (1, tq // LANES)))
                m_ref[hh] = m_new
                acc_ref[hh] = jnp.exp(m_prev - m_new) * acc_ref[hh] + jnp.dot(p.astype(BF16), vb,
                                                                              preferred_element_type=F32)

        def needed(hh, j):
            top = fs_ref[2 * pr + hh, i] + bd_ref[0, 0]
            return jnp.logical_and(j >= 0, top - fe_ref[2 * pr + hh, jnp.maximum(j, 0)] >= SKIP)

        @pl.when(bd_ref[0, 0] <= SHIFT_MAX)
        def _():
            m_ref[...] = jnp.zeros_like(m_ref)
            shifted_step(i, (0, 1), masked=True)
            _block_loops(i - 1, -1, needed, shifted_step)

        @pl.when(bd_ref[0, 0] > SHIFT_MAX)
        def _():
            m_ref[...] = jnp.full(m_ref.shape, NEG, F32)
            online_step(i, (0, 1), masked=True)
            _block_loops(i - 1, -1, needed, online_step)

        outs, lses = [], []
        for hh in range(2):
            acc = acc_ref[hh]
            row_sum = jnp.broadcast_to(acc[:, DH:DH + 1], (tq, LANES))
            outs.append(acc / row_sum)
            lses.append(m_ref[hh] + jnp.log(row_sum))
        o_ref[...] = jnp.where(sel_a, outs[0], pltpu.roll(outs[1], 64, 1))
        row = lax.broadcasted_iota(jnp.int32, (8, tq), 0)
        lser_ref[0, 0] = jnp.where(row == 0, lses[0].T[0:8, :], lses[1].T[0:8, :])

    return pl.pallas_call(
        body, name="attn_fwd", grid=(NH // 2, nq),
        out_shape=(_sds((S, AW), F32), _sds((NH // 2, nq, 8, tq), F32)),
        in_specs=[SMEM_SPEC, SMEM_SPEC, SMEM_SPEC,
                  pl.BlockSpec((2, tq, LANES), lambda p, i: (p, i, 0)),
                  pl.BlockSpec((2, S, LANES), lambda p, i: (p, 0, 0)),
                  pl.BlockSpec((2, S, LANES), lambda p, i: (p, 0, 0))],
        out_specs=(pl.BlockSpec((tq, LANES), lambda p, i: (i, p)),
                   pl.BlockSpec((1, 1, 8, tq), lambda p, i: (p, i, 0, 0))),
        scratch_shapes=[pltpu.VMEM((2, tq, LANES), F32), pltpu.VMEM((2, tq, LANES), F32)],
        compiler_params=_cp(("parallel", "parallel")),
    )(fs, fe, bd, qa, ka, va)


HALO = 32
CHUNK_ROWS = 64


def _halo_prev(tm):
    return pl.BlockSpec((HALO, CW), lambda i: (jnp.maximum(i * (tm // HALO) - 1, 0), 0))


SUBLANES = 8
SHIFT_ROWS = 24


def _shifted_copies(ext_ref, sh_ref, tm):
    for k in range(1, SUBLANES):
        sh_ref[k - 1, 0:tm + SHIFT_ROWS, :] = ext_ref[k:k + tm + SHIFT_ROWS, :]


def _ext_rows(ext_ref, sh_ref, o):
    k = o % SUBLANES
    if k == 0:
        return ext_ref[o:o + CHUNK_ROWS, :]
    return sh_ref[k - 1, o - k:o - k + CHUNK_ROWS, :]


def _fwd_conv(u0, w32, cb, lng, lnb, beta_c, tm, ex=None):
    S = u0.shape[0]

    def body(cur_ref, prev_ref, w_ref, cb_ref, lng_ref, lnb_ref, beta_ref, u1_ref, mc_ref, ext_ref, sh_ref):
        i = pl.program_id(0)
        ext_ref[0:HALO, :] = jnp.where(i == 0, 0.0, prev_ref[...])
        ext_ref[HALO:, :] = cur_ref[...]
        _shifted_copies(ext_ref, sh_ref, tm)
        for r0 in range(0, tm, CHUNK_ROWS):
            acc = jnp.zeros((CHUNK_ROWS, CW), F32) + cb_ref[...]
            for j in range(KC):
                acc = acc + w_ref[j:j + 1, :] * _ext_rows(ext_ref, sh_ref, r0 + 2 + j)
            u1_ref[r0:r0 + CHUNK_ROWS, :] = acc
        u1 = u1_ref[...]
        mu = jnp.mean(u1, axis=-1, keepdims=True)
        d = u1 - mu
        rstd = lax.rsqrt(jnp.mean(d * d, axis=-1, keepdims=True) + EPS)
        u2 = d * rstd * lng_ref[...] + lnb_ref[...]
        u3 = u2 * _sigmoid(u2)
        rc = lax.rsqrt(jnp.mean(u3 * u3, axis=-1, keepdims=True) + EPS)
        mc_ref[...] = (u3 * rc * beta_ref[...]).astype(BF16)

    outs = pl.pallas_call(
        _hosted(body, ex, 7, 2, 2, S // tm), name="fwd_conv", grid=(S // tm,),
        out_shape=(_sds((S, CW), F32), _sds((S, CW), BF16), *(ex.out_shapes if ex else [])),
        in_specs=[_rows(tm, CW), _halo_prev(tm), _const((HALO, CW)), _const((1, CW)), _const((1, CW)), _const((1, CW)),
                  _const((1, CW)), *(ex.in_specs if ex else [])],
        out_specs=(_rows(tm, CW), _rows(tm, CW), *(ex.out_specs if ex else [])),
        scratch_shapes=[pltpu.VMEM((tm + HALO, CW), F32), pltpu.VMEM((SUBLANES - 1, tm + HALO, CW), F32),
                        *(ex.scratch if ex else [])],
        compiler_params=_cp(("arbitrary",)),
    )(u0, u0, w32, cb, lng, lnb, beta_c, *(ex.srcs if ex else []))
    return outs[:2], list(outs[2:])


def _fwd_out(o_attn, mc, x, mod8, n2g, beta_a, w_out, tm):
    S = x.shape[0]

    def body(o_ref, mc_ref, x_ref, mod_ref, n2g_ref, beta_ref, w_ref, mg_ref, ob_ref, x2_ref, h2_ref):
        ov = o_ref[...]
        ra = lax.rsqrt(jnp.mean(ov * ov, axis=-1, keepdims=True) + EPS)
        ma = (ov * ra * beta_ref[...]).astype(BF16)
        mcv = mc_ref[...]
        mg_ref[:, 0:AW] = ma
        mg_ref[:, AW:D] = mcv
        o = (jnp.dot(ma, w_ref[0:AW, :], preferred_element_type=F32)
             + jnp.dot(mcv, w_ref[AW:D, :], preferred_element_type=F32))
        ob_ref[...] = o.astype(BF16)
        x2 = x_ref[...] + mod_ref[2:3, :] * o
        x2_ref[...] = x2
        r2 = lax.rsqrt(jnp.mean(x2 * x2, axis=-1, keepdims=True) + EPS)
        h2_ref[...] = ((x2 * r2) * (n2g_ref[...] * (1.0 + mod_ref[4:5, :])) + mod_ref[3:4, :]).astype(BF16)

    return pl.pallas_call(
        body, name="fwd_out", grid=(S // tm,),
        out_shape=(_sds((S, D), BF16), _sds((S, D), BF16), _sds((S, D), F32), _sds((S, D), BF16)),
        in_specs=[_rows(tm, AW), _rows(tm, CW), _rows(tm, D), _const((8, D)), _const((1, D)), _const((1, AW)),
                  _const((D, D))],
        out_specs=(_rows(tm, D), _rows(tm, D), _rows(tm, D), _rows(tm, D)),
        compiler_params=_cp(("parallel",)),
    )(o_attn, mc, x, mod8, n2g, beta_a, w_out)


def _fwd_ffn(h2, w1, w2, x2, tgt, mod8, tm):
    S = h2.shape[0]
    nk = w1.shape[0]
    bf = w1.shape[2]

    def body(h2_ref, w1_ref, w2_ref, x2_ref, tgt_ref, mod_ref, r_ref, dy_ref, loss_ref, dg2_ref):
        @pl.when(pl.program_id(0) == 0)
        def _():
            loss_ref[...] = jnp.zeros_like(loss_ref)
            dg2_ref[...] = jnp.zeros_like(dg2_ref)

        f2 = None
        for k in range(nk):
            r = jnp.maximum(jnp.dot(h2_ref[...], w1_ref[k], preferred_element_type=F32), 0.0)
            r_ref[:, k * bf:(k + 1) * bf] = r.astype(BF16)
            part = jnp.dot((r * r).astype(BF16), w2_ref[k], preferred_element_type=F32)
            f2 = part if f2 is None else f2 + part
        e = x2_ref[...] + mod_ref[5:6, :] * f2 - tgt_ref[...]
        dy = e * (1.0 / D)
        dy_ref[...] = dy
        loss_ref[...] += 0.5 * jnp.sum(jnp.sum(e * dy, axis=1, keepdims=True), axis=0, keepdims=True)
        dg2_ref[...] += jnp.sum((dy * f2).reshape(tm // 8, 8, D), axis=0)

    once = pl.Buffered(1)
    return pl.pallas_call(
        body, name="fwd_ffn", grid=(S // tm,),
        out_shape=(_sds((S, DFF), BF16), _sds((S, D), F32), _sds((8, LANES), F32), _sds((8, D), F32)),
        in_specs=[_rows(tm, D), pl.BlockSpec((nk, D, bf), lambda i: (0, 0, 0), pipeline_mode=once),
                  pl.BlockSpec((nk, bf, D), lambda i: (0, 0, 0), pipeline_mode=once), _rows(tm, D), _rows(tm, D),
                  _const((8, D))],
        out_specs=(_rows(tm, DFF), _rows(tm, D), _const((8, LANES)), _const((8, D))),
        compiler_params=_cp(("arbitrary",)),
    )(h2, w1, w2, x2, tgt, mod8)


def _bwd_ffn(dy, mod8, r, w1, w2, tm):
    S = dy.shape[0]
    nk = w1.shape[0]
    bf = w1.shape[2]

    def body(dy_ref, mod_ref, r_ref, w1_ref, w2_ref, df2_ref, df1_ref, dh2_ref):
        df2 = (dy_ref[...] * mod_ref[5:6, :]).astype(BF16)
        df2_ref[...] = df2
        dh2 = None
        for k in range(nk):
            da = lax.dot_general(df2, w2_ref[k], NT, preferred_element_type=F32)
            df1 = (da * (2.0 * r_ref[:, k * bf:(k + 1) * bf].astype(F32))).astype(BF16)
            df1_ref[:, k * bf:(k + 1) * bf] = df1
            part = lax.dot_general(df1, w1_ref[k], NT, preferred_element_type=F32)
            dh2 = part if dh2 is None else dh2 + part
        dh2_ref[...] = dh2

    once = pl.Buffered(1)
    return pl.pallas_call(
        body, name="bwd_ffn", grid=(S // tm,),
        out_shape=(_sds((S, D), BF16), _sds((S, DFF), BF16), _sds((S, D), F32)),
        in_specs=[_rows(tm, D), _const((8, D)), _rows(tm, DFF),
                  pl.BlockSpec((nk, D, bf), lambda i: (0, 0, 0), pipeline_mode=once),
                  pl.BlockSpec((nk, bf, D), lambda i: (0, 0, 0), pipeline_mode=once)],
        out_specs=(_rows(tm, D), _rows(tm, DFF), _rows(tm, D)),
        compiler_params=_cp(("parallel",)),
    )(dy, mod8, r, w1, w2)


def _wgrad(a, b, name, square_a=False, col_pieces=False, tk=1024, bm=1024, bn=1024):
    S, M = a.shape
    N = b.shape[1]
    bm, bn, tk = min(bm, M), min(bn, N), min(tk, S)
    nk = S // tk

    def body(a_ref, b_ref, o_ref, acc_ref):
        av = a_ref[...]
        if square_a:
            af = av.astype(F32)
            av = (af * af).astype(BF16)
        part = lax.dot_general(av, b_ref[...], TN, preferred_element_type=F32)

        @pl.when(pl.program_id(2) == 0)
        def _():
            acc_ref[...] = part

        @pl.when(pl.program_id(2) > 0)
        def _():
            acc_ref[...] += part

        @pl.when(pl.program_id(2) == nk - 1)
        def _():
            if col_pieces:
                o_ref[0] = acc_ref[...].astype(BF16)
            else:
                o_ref[...] = acc_ref[...].astype(BF16)

    if col_pieces:
        out_shape, out_spec = _sds((N // bn, M, bn), BF16), pl.BlockSpec((1, bm, bn), lambda mi, ni, k: (ni, mi, 0))
    else:
        out_shape, out_spec = _sds((M, N), BF16), pl.BlockSpec((bm, bn), lambda mi, ni, k: (mi, ni))
    return pl.pallas_call(
        body, name=name, grid=(M // bm, N // bn, nk), out_shape=out_shape,
        in_specs=[pl.BlockSpec((tk, bm), lambda mi, ni, k: (k, mi)), pl.BlockSpec((tk, bn), lambda mi, ni, k: (k, ni))],
        out_specs=out_spec, scratch_shapes=[pltpu.VMEM((bm, bn), F32)],
        compiler_params=_cp(("parallel", "parallel", "arbitrary")),
    )(a, b)


def _wgrad_in(h1, pieces, tk=1024):
    S = h1.shape[0]
    tk = min(tk, S)
    widths = [p.shape[1] for p in pieces]
    offs = [sum(widths[:t]) for t in range(len(widths))]

    def body(a_ref, *refs):
        o_ref, acc_ref = refs[-2:]

        @pl.when(pl.program_id(0) == 0)
        def _():
            acc_ref[...] = jnp.zeros_like(acc_ref)

        for b_ref, off, w in zip(refs[:-2], offs, widths):
            acc_ref[:, off:off + w] += lax.dot_general(a_ref[...], b_ref[...], TN, preferred_element_type=F32)

        @pl.when(pl.program_id(0) == S // tk - 1)
        def _():
            o_ref[...] = acc_ref[...].astype(BF16)

    return pl.pallas_call(
        body, name="wgrad_in", grid=(S // tk,), out_shape=_sds((D, NP), BF16),
        in_specs=[_rows(tk, D)] + [_rows(tk, w) for w in widths], out_specs=_const((D, NP)),
        scratch_shapes=[pltpu.VMEM((D, NP), F32)], compiler_params=_cp(("arbitrary",)),
    )(h1, *pieces)


def _colsum8(t):
    return jnp.sum(t.reshape(t.shape[0] // 8, 8, t.shape[1]), axis=0)


def _bwd_mid(dh2, dy, x2, ob, o_attn, u1, mod8, n2g, beta_a, beta_c, lng, lnb, w_out, e512, tm, tq, ex=None):
    S = dy.shape[0]

    def body(dh2_ref, dy_ref, x2_ref, ob_ref, oa_ref, u1_ref, mod_ref, n2g_ref, ba_ref, bc_ref, lng_ref, lnb_ref, w_ref,
             e_ref, dx2_ref, do_ref, doa_ref, du1_ref, acc_d_ref, acc_h_ref, dr_ref):
        @pl.when(pl.program_id(0) == 0)
        def _():
            acc_d_ref[...] = jnp.zeros_like(acc_d_ref)
            acc_h_ref[...] = jnp.zeros_like(acc_h_ref)

        x2 = x2_ref[...]
        dh2 = dh2_ref[...]
        r2 = lax.rsqrt(jnp.mean(x2 * x2, axis=-1, keepdims=True) + EPS)
        xn2 = x2 * r2
        gain = n2g_ref[...] * (1.0 + mod_ref[4:5, :])
        dxn = dh2 * gain
        dx2 = dy_ref[...] + r2 * (dxn - xn2 * jnp.mean(dxn * xn2, axis=-1, keepdims=True))
        dx2_ref[...] = dx2
        t = dh2 * xn2
        acc_d_ref[0] += _colsum8(dh2)
        acc_d_ref[1] += _colsum8(t * n2g_ref[...])
        acc_d_ref[2] += _colsum8(t * (1.0 + mod_ref[4:5, :]))
        acc_d_ref[3] += _colsum8(dx2 * ob_ref[...].astype(F32))
        do = (dx2 * mod_ref[2:3, :]).astype(BF16)
        do_ref[...] = do
        dma = lax.dot_general(do, w_ref[0:AW, :], NT, preferred_element_type=F32)
        dmc = lax.dot_general(do, w_ref[AW:D, :], NT, preferred_element_type=F32)
        ov = oa_ref[...]
        ra = lax.rsqrt(jnp.mean(ov * ov, axis=-1, keepdims=True) + EPS)
        on = ov * ra
        acc_h_ref[0] += _colsum8(dma * on)
        don = dma * ba_ref[...]
        doa = (ra * (don - on * jnp.mean(don * on, axis=-1, keepdims=True))).astype(BF16)
        doa_ref[...] = doa
        delta_t = _dot2(doa.astype(F32) * ov, e_ref[...]).T
        for p in range(NH // 2):
            dr_ref[p, 0] = delta_t[2 * p:2 * p + 8, :]
        u1 = u1_ref[...]
        mu = jnp.mean(u1, axis=-1, keepdims=True)
        d = u1 - mu
        rstd = lax.rsqrt(jnp.mean(d * d, axis=-1, keepdims=True) + EPS)
        uh = d * rstd
        u2 = uh * lng_ref[...] + lnb_ref[...]
        sg = _sigmoid(u2)
        u3 = u2 * sg
        rc = lax.rsqrt(jnp.mean(u3 * u3, axis=-1, keepdims=True) + EPS)
        u3n = u3 * rc
        acc_h_ref[1] += _colsum8(dmc * u3n)
        du3n = dmc * bc_ref[...]
        du3 = rc * (du3n - u3n * jnp.mean(du3n * u3n, axis=-1, keepdims=True))
        du2 = du3 * (sg * (1.0 + u2 * (1.0 - sg)))
        acc_h_ref[2] += _colsum8(du2 * uh)
        acc_h_ref[3] += _colsum8(du2)
        duh = du2 * lng_ref[...]
        du1_ref[...] = rstd * (duh - jnp.mean(duh, axis=-1, keepdims=True)
                               - uh * jnp.mean(duh * uh, axis=-1, keepdims=True))

    per = tq // tm
    outs = pl.pallas_call(
        _hosted(body, ex, 14, 7, 0, S // tm), name="bwd_mid", grid=(S // tm,),
        out_shape=(_sds((S, D), F32), _sds((S, D), BF16), _sds((S, AW), BF16), _sds((S, CW), F32),
                   _sds((4, 8, D), F32), _sds((4, 8, AW), F32), _sds((NH // 2, S // tq, 8, tq), F32),
                   *(ex.out_shapes if ex else [])),
        in_specs=[_rows(tm, D), _rows(tm, D), _rows(tm, D), _rows(tm, D), _rows(tm, AW), _rows(tm, CW), _const((8, D)),
                  _const((1, D)), _const((1, AW)), _const((1, CW)), _const((1, CW)), _const((1, CW)), _const((D, D)),
                  _const((AW, LANES)), *(ex.in_specs if ex else [])],
        out_specs=(_rows(tm, D), _rows(tm, D), _rows(tm, AW), _rows(tm, CW), _const((4, 8, D)), _const((4, 8, AW)),
                   pl.BlockSpec((NH // 2, 1, 8, tm), lambda i: (0, i // per, 0, i % per)),
                   *(ex.out_specs if ex else [])),
        scratch_shapes=ex.scratch if ex else [],
        compiler_params=_cp(("arbitrary",)),
    )(dh2, dy, x2, ob, o_attn, u1, mod8, n2g, beta_a, beta_c, lng, lnb, w_out, e512, *(ex.srcs if ex else []))
    return outs[:7], list(outs[7:])


def _attn_bwd_dq(qa, ka, v, do, o_attn, lsec, fs, fe, bd, tq):
    S = qa.shape[1]
    nq = S // tq
    tc = _chunk(tq)

    def body(fs_ref, fe_ref, bd_ref, qa_ref, ka_ref, v_ref, do_ref, o_ref, lse_ref, dqa_ref, dr_ref, acc_ref):
        pr, i = pl.program_id(0), pl.program_id(1)
        sel_a = _lane((tq, LANES)) < DH
        dov = do_ref[...]
        prod = dov.astype(F32) * o_ref[...]
        zero = jnp.zeros_like(prod)
        deltas = [jnp.broadcast_to(jnp.sum(jnp.where(sel_a, prod, zero), axis=1, keepdims=True), (tq, LANES)),
                  jnp.broadcast_to(jnp.sum(jnp.where(sel_a, zero, prod), axis=1, keepdims=True), (tq, LANES))]
        zb = jnp.zeros_like(dov)
        dos = [jnp.where(sel_a, dov, zb), jnp.where(sel_a, zb, dov)]
        acc_ref[...] = jnp.zeros_like(acc_ref)

        def kv_step(j, heads, masked=False):
            for c0 in range(0, tq, tc):
                start = pl.multiple_of(j * tq + c0, tc)
                vb = v_ref[pl.ds(start, tc), :]
                for hh in heads:
                    kb = ka_ref[hh, pl.ds(start, tc), :]
                    s = lax.dot_general(qa_ref[hh], kb, NT, preferred_element_type=F32)
                    p = jnp.exp(s - jnp.tile(lse_ref[hh], (1, tc // LANES)))
                    if masked:
                        p = jnp.where(_causal_rect(tq, tc, c0), p, 0.0)
                    dp = lax.dot_general(dos[hh], vb, NT, preferred_element_type=F32)
                    ds = p * (dp - jnp.tile(deltas[hh], (1, tc // LANES)))
                    acc_ref[hh] += jnp.dot(ds.astype(BF16), kb, preferred_element_type=F32)

        def needed(hh, j):
            top = fs_ref[2 * pr + hh, i] + bd_ref[0, 0]
            return jnp.logical_and(j >= 0, top - fe_ref[2 * pr + hh, jnp.maximum(j, 0)] >= SKIP)

        kv_step(i, (0, 1), masked=True)
        _block_loops(i - 1, -1, needed, kv_step)
        dqa_ref[...] = acc_ref[...]
        row = lax.broadcasted_iota(jnp.int32, (8, tq), 0)
        da = deltas[0].T[0:8, :]
        db = deltas[1].T[0:8, :]
        dr_ref[0, 0] = jnp.where(row == 0, da, db)

    return pl.pallas_call(
        body, name="attn_bwd_dq", grid=(NH // 2, nq),
        out_shape=(_sds((NH, S, LANES), F32), _sds((NH // 2, nq, 8, tq), F32)),
        in_specs=[SMEM_SPEC, SMEM_SPEC, SMEM_SPEC,
                  pl.BlockSpec((2, tq, LANES), lambda p, i: (p, i, 0)),
                  pl.BlockSpec((2, S, LANES), lambda p, i: (p, 0, 0)),
                  pl.BlockSpec((S, LANES), lambda p, i: (0, p)),
                  pl.BlockSpec((tq, LANES), lambda p, i: (i, p)),
                  pl.BlockSpec((tq, LANES), lambda p, i: (i, p)),
                  pl.BlockSpec((2, tq, LANES), lambda p, i: (p, i, 0))],
        out_specs=(pl.BlockSpec((2, tq, LANES), lambda p, i: (p, i, 0)),
                   pl.BlockSpec((1, 1, 8, tq), lambda p, i: (p, i, 0, 0))),
        scratch_shapes=[pltpu.VMEM((2, tq, LANES), F32)],
        compiler_params=_cp(("parallel", "parallel")),
    )(fs, fe, bd, qa, ka, v, do, o_attn, lsec)


def _attn_bwd_dkv(qa, ka, v, do, lser, dr, fs, fe, bd, tq):
    S = qa.shape[1]
    nq = S // tq
    tc = _chunk(tq)

    def body(fs_ref, fe_ref, bd_ref, ka_ref, v_ref, qa_ref, do_ref, lse_ref, dr_ref, dka_ref, dv_ref, acck_ref,
             accv_ref):
        pr, j = pl.program_id(0), pl.program_id(1)
        sel_a = _lane((tq, LANES)) < DH
        vv = v_ref[...]
        zb = jnp.zeros_like(vv)
        vs = [jnp.where(sel_a, vv, zb), jnp.where(sel_a, zb, vv)]
        acck_ref[...] = jnp.zeros_like(acck_ref)
        accv_ref[...] = jnp.zeros_like(accv_ref)

        def q_step(i, heads, masked=False):
            lse8 = lse_ref[0, i]
            dr8 = dr_ref[0, i]
            for c0 in range(0, tq, tc):
                start = pl.multiple_of(i * tq + c0, tc)
                dob = do_ref[pl.ds(start, tc), :]
                for hh in heads:
                    qb = qa_ref[hh, pl.ds(start, tc), :]
                    st = lax.dot_general(ka_ref[hh], qb, NT, preferred_element_type=F32)
                    pt = jnp.exp(st - lse8[hh:hh + 1, c0:c0 + tc])
                    if masked:
                        keep = (lax.broadcasted_iota(jnp.int32, (tq, tc), 0)
                                <= lax.broadcasted_iota(jnp.int32, (tq, tc), 1) + c0)
                        pt = jnp.where(keep, pt, 0.0)
                    accv_ref[hh] += jnp.dot(pt.astype(BF16), dob, preferred_element_type=F32)
                    dpt = lax.dot_general(vs[hh], dob, NT, preferred_element_type=F32)
                    dst = pt * (dpt - dr8[hh:hh + 1, c0:c0 + tc])
                    acck_ref[hh] += jnp.dot(dst.astype(BF16), qb, preferred_element_type=F32)

        def needed(hh, i):
            top = fs_ref[2 * pr + hh, jnp.minimum(i, nq - 1)] + bd_ref[0, 0]
            return jnp.logical_and(i < nq, top - fe_ref[2 * pr + hh, j] >= SKIP)

        q_step(j, (0, 1), masked=True)
        _block_loops(j + 1, 1, needed, q_step)
        dka_ref[...] = acck_ref[...]
        dv_ref[...] = jnp.where(sel_a, accv_ref[0], accv_ref[1]).astype(BF16)

    return pl.pallas_call(
        body, name="attn_bwd_dkv", grid=(NH // 2, nq),
        out_shape=(_sds((NH, S, LANES), F32), _sds((S, AW), BF16)),
        in_specs=[SMEM_SPEC, SMEM_SPEC, SMEM_SPEC,
                  pl.BlockSpec((2, tq, LANES), lambda p, j: (p, j, 0)),
                  pl.BlockSpec((tq, LANES), lambda p, j: (j, p)),
                  pl.BlockSpec((2, S, LANES), lambda p, j: (p, 0, 0)),
                  pl.BlockSpec((S, LANES), lambda p, j: (0, p)),
                  pl.BlockSpec((1, nq, 8, tq), lambda p, j: (p, 0, 0, 0)),
                  pl.BlockSpec((1, nq, 8, tq), lambda p, j: (p, 0, 0, 0))],
        out_specs=(pl.BlockSpec((2, tq, LANES), lambda p, j: (p, j, 0)),
                   pl.BlockSpec((tq, LANES), lambda p, j: (j, p))),
        scratch_shapes=[pltpu.VMEM((2, tq, LANES), F32), pltpu.VMEM((2, tq, LANES), F32)],
        compiler_params=_cp(("parallel", "parallel")),
    )(fs, fe, bd, ka, v, qa, do, lser, dr)


def _attn_bwd(qa, ka, v, do, lser, dr, fs, fe, bd, tq):
    S = qa.shape[1]
    nq = S // tq

    def body(fs_ref, fe_ref, bd_ref, ka_ref, v_ref, qa_ref, do_ref, lse_ref, dr_ref, dqa_hbm, dka_ref, dv_ref,
             accq_ref, acck_ref, accv_ref, out_sem):
        pr, j = pl.program_id(0), pl.program_id(1)
        sel_a = _lane((tq, LANES)) < DH
        vv = v_ref[...]
        zb = jnp.zeros_like(vv)
        vs = [jnp.where(sel_a, vv, zb), jnp.where(sel_a, zb, vv)]
        acck_ref[...] = jnp.zeros_like(acck_ref)
        accv_ref[...] = jnp.zeros_like(accv_ref)

        @pl.when(j == 0)
        def _():
            accq_ref[...] = jnp.zeros_like(accq_ref)

        def q_step(i, heads, masked=False):
            start = pl.multiple_of(i * tq, tq)
            dob = do_ref[pl.ds(start, tq), :]
            lse8 = lse_ref[0, i]
            dr8 = dr_ref[0, i]
            for hh in heads:
                qb = qa_ref[hh, pl.ds(start, tq), :]
                kb = ka_ref[hh]
                st = lax.dot_general(kb, qb, NT, preferred_element_type=F32)
                pt = jnp.exp(st - lse8[hh:hh + 1, :])
                if masked:
                    keep = (lax.broadcasted_iota(jnp.int32, (tq, tq), 0)
                            <= lax.broadcasted_iota(jnp.int32, (tq, tq), 1))
                    pt = jnp.where(keep, pt, 0.0)
                accv_ref[hh] += jnp.dot(pt.astype(BF16), dob, preferred_element_type=F32)
                dpt = lax.dot_general(vs[hh], dob, NT, preferred_element_type=F32)
                dst = (pt * (dpt - dr8[hh:hh + 1, :])).astype(BF16)
                acck_ref[hh] += jnp.dot(dst, qb, preferred_element_type=F32)
                accq_ref[hh, pl.ds(start, tq), :] += lax.dot_general(dst, kb, TN, preferred_element_type=F32)

        def needed(hh, i):
            top = fs_ref[2 * pr + hh, jnp.minimum(i, nq - 1)] + bd_ref[0, 0]
            return jnp.logical_and(i < nq, top - fe_ref[2 * pr + hh, j] >= SKIP)

        q_step(j, (0, 1), masked=True)
        _block_loops(j + 1, 1, needed, q_step)
        dka_ref[...] = acck_ref[...]
        dv_ref[...] = jnp.where(sel_a, accv_ref[0], accv_ref[1]).astype(BF16)

        @pl.when(j == nq - 1)
        def _():
            out = pltpu.make_async_copy(accq_ref, dqa_hbm.at[pl.ds(2 * pr, 2)], out_sem)
            out.start()
            out.wait()

    once = pl.Buffered(1)
    return pl.pallas_call(
        body, name="attn_bwd", grid=(NH // 2, nq),
        out_shape=(_sds((NH, S, LANES), F32), _sds((NH, S, LANES), F32), _sds((S, AW), BF16)),
        in_specs=[SMEM_SPEC, SMEM_SPEC, SMEM_SPEC,
                  pl.BlockSpec((2, tq, LANES), lambda p, j: (p, j, 0)),
                  pl.BlockSpec((tq, LANES), lambda p, j: (j, p)),
                  pl.BlockSpec((2, S, LANES), lambda p, j: (p, 0, 0), pipeline_mode=once),
                  pl.BlockSpec((S, LANES), lambda p, j: (0, p), pipeline_mode=once),
                  pl.BlockSpec((1, nq, 8, tq), lambda p, j: (p, 0, 0, 0)),
                  pl.BlockSpec((1, nq, 8, tq), lambda p, j: (p, 0, 0, 0))],
        out_specs=(pl.BlockSpec(memory_space=pl.ANY),
                   pl.BlockSpec((2, tq, LANES), lambda p, j: (p, j, 0)),
                   pl.BlockSpec((tq, LANES), lambda p, j: (j, p))),
        scratch_shapes=[pltpu.VMEM((2, S, LANES), F32), pltpu.VMEM((2, tq, LANES), F32),
                        pltpu.VMEM((2, tq, LANES), F32), pltpu.SemaphoreType.DMA],
        compiler_params=_cp(("arbitrary", "arbitrary")),
    )(fs, fe, bd, ka, v, qa, do, lser, dr)


def _bwd_conv(du1, u0, alin, agate, w32, tm, ex=None):
    S = du1.shape[0]
    nt = S // tm

    def body(du_ref, dun_ref, u0_ref, u0p_ref, alin_ref, agate_ref, w_ref,
             dalin_ref, dagate_ref, dw_ref, db_ref, extd_ref, extu_ref, du0_ref, shd_ref, shu_ref):
        i = pl.program_id(0)

        @pl.when(i == 0)
        def _():
            dw_ref[...] = jnp.zeros_like(dw_ref)
            db_ref[...] = jnp.zeros_like(db_ref)

        extd_ref[0:tm, :] = du_ref[...]
        extd_ref[tm:, :] = jnp.where(i == nt - 1, 0.0, dun_ref[...])
        extu_ref[0:HALO, :] = jnp.where(i == 0, 0.0, u0p_ref[...])
        extu_ref[HALO:, :] = u0_ref[...]
        _shifted_copies(extd_ref, shd_ref, tm)
        _shifted_copies(extu_ref, shu_ref, tm)
        db_ref[...] += _colsum8(du_ref[...])
        for r0 in range(0, tm, CHUNK_ROWS):
            duc = du_ref[r0:r0 + CHUNK_ROWS, :]
            acc = jnp.zeros((CHUNK_ROWS, CW), F32)
            for j in range(KC):
                acc = acc + w_ref[j:j + 1, :] * _ext_rows(extd_ref, shd_ref, r0 + 30 - j)
                dw_ref[j] += _colsum8(duc * _ext_rows(extu_ref, shu_ref, r0 + 2 + j))
            du0_ref[r0:r0 + CHUNK_ROWS, :] = acc
        du0 = du0_ref[...]
        al = alin_ref[...].astype(F32)
        sg = _sigmoid(agate_ref[...].astype(F32))
        dalin_ref[...] = (du0 * sg).astype(BF16)
        dagate_ref[...] = (du0 * al * sg * (1.0 - sg)).astype(BF16)

    nxt = pl.BlockSpec((HALO, CW), lambda i: (jnp.minimum((i + 1) * (tm // HALO), S // HALO - 1), 0))
    outs = pl.pallas_call(
        _hosted(body, ex, 7, 4, 5, nt), name="bwd_conv", grid=(nt,),
        out_shape=(_sds((S, CW), BF16), _sds((S, CW), BF16), _sds((HALO, 8, CW), F32), _sds((8, CW), F32),
                   *(ex.out_shapes if ex else [])),
        in_specs=[_rows(tm, CW), nxt, _rows(tm, CW), _halo_prev(tm), _rows(tm, CW), _rows(tm, CW), _const((HALO, CW)),
                  *(ex.in_specs if ex else [])],
        out_specs=(_rows(tm, CW), _rows(tm, CW), _const((HALO, 8, CW)), _const((8, CW)), *(ex.out_specs if ex else [])),
        scratch_shapes=[pltpu.VMEM((tm + HALO, CW), F32), pltpu.VMEM((tm + HALO, CW), F32), pltpu.VMEM((tm, CW), F32),
                        pltpu.VMEM((SUBLANES - 1, tm + HALO, CW), F32), pltpu.VMEM((SUBLANES - 1, tm + HALO, CW), F32),
                        *(ex.scratch if ex else [])],
        compiler_params=_cp(("arbitrary",)),
    )(du1, du1, u0, u0, alin, agate, w32, *(ex.srcs if ex else []))
    return outs[:4], list(outs[4:])


def _bwd_qk(dqa, dka, qn, kn, rq, rk, fgb, qg512, kg512, e512, et512, tm):
    S = qn.shape[0]
    nt = S // tm

    def body(dqa_ref, dka_ref, qn_ref, kn_ref, rq_ref, rk_ref, fgb_ref, qg_ref, kg_ref, e_ref, et_ref,
             dq_ref, dk_ref, dfg_ref, accg_ref, accb_ref, carry_ref):
        @pl.when(pl.program_id(0) == 0)
        def _():
            carry_ref[...] = jnp.zeros_like(carry_ref)
            accg_ref[...] = jnp.zeros_like(accg_ref)
            accb_ref[...] = jnp.zeros_like(accb_ref)

        lane = _lane((tm, LANES))
        sel_a = lane < DH
        df = jnp.zeros((tm, LANES), F32)
        for h in range(NH):
            col = dqa_ref[h][:, 64:65] - dka_ref[h][:, 67:68]
            df = jnp.where(lane == h, col, df)
        tri = (lax.broadcasted_iota(jnp.int32, (tm, tm), 0) <= lax.broadcasted_iota(jnp.int32, (tm, tm), 1)
               ).astype(F32).astype(BF16)
        dlf = _dot3(tri, df) + carry_ref[0:1, :]
        carry_ref[...] = jnp.broadcast_to(dlf[0:1, :], carry_ref.shape)
        dfg = jnp.where(lane < NH, dlf * _sigmoid(-fgb_ref[...]), 0.0)
        dfg_ref[...] = dfg.astype(BF16)
        accb_ref[...] += _colsum8(dfg)

        def norm_bwd(src_ref, n_ref, r_ref, g_ref, scale, slot):
            pairs = []
            for p in range(NH // 2):
                b = pltpu.roll(src_ref[2 * p + 1], 64, 1)
                pairs.append(jnp.where(sel_a, src_ref[2 * p], b))
            dh = jnp.concatenate(pairs, axis=1) * scale
            tn = n_ref[...].astype(F32)
            accg_ref[slot] += _colsum8(dh * tn)
            dn = dh * g_ref[...]
            mean = _dot2(dn * tn, e_ref[...]) * (1.0 / DH)
            corr = _dot2(mean, et_ref[...])
            rf = _dot2(r_ref[...], et_ref[...])
            return (rf * (dn - tn * corr)).astype(BF16)

        dq_ref[...] = norm_bwd(dqa_ref, qn_ref, rq_ref, qg_ref, DH ** -0.5, 0)
        dk_ref[...] = norm_bwd(dka_ref, kn_ref, rk_ref, kg_ref, 1.0, 1)

    rev = lambda n: pl.BlockSpec((tm, n), lambda i: (nt - 1 - i, 0))
    hm = pl.BlockSpec((NH, tm, LANES), lambda i: (0, nt - 1 - i, 0))
    dq, dk, dfg, accg, accb = pl.pallas_call(
        body, name="bwd_qk", grid=(nt,),
        out_shape=(_sds((S, AW), BF16), _sds((S, AW), BF16), _sds((S, LANES), BF16), _sds((2, 8, AW), F32),
                   _sds((8, LANES), F32)),
        in_specs=[hm, hm, rev(AW), rev(AW), rev(LANES), rev(LANES), rev(LANES), _const((1, AW)), _const((1, AW)),
                  _const((AW, LANES)), _const((LANES, AW))],
        out_specs=(rev(AW), rev(AW), rev(LANES), _const((2, 8, AW)), _const((8, LANES))),
        scratch_shapes=[pltpu.VMEM((8, LANES), F32)], compiler_params=_cp(("arbitrary",)),
    )(dqa, dka, qn, kn, rq, rk, fgb, qg512, kg512, e512, et512)
    return dq, dk, dfg, accg, accb


def _bwd_in(dq, dk, dv, dalin, dagate, dfg, w_in_p, x, dx2, mod8, n1g, tm, ex=None):
    S = x.shape[0]

    def body(dq_ref, dk_ref, dv_ref, dal_ref, dag_ref, dfg_ref, w_ref, x_ref, dx2_ref, mod_ref, n1g_ref,
             dx_ref, acc_ref):
        @pl.when(pl.program_id(0) == 0)
        def _():
            acc_ref[...] = jnp.zeros_like(acc_ref)

        def part(ref, a, b):
            return lax.dot_general(ref[...], w_ref[:, a:b], NT, preferred_element_type=F32)

        dh = (part(dq_ref, 0, 512) + part(dk_ref, 512, 1024) + part(dv_ref, 1024, 1536) + part(dal_ref, 1536, 2048)
              + part(dag_ref, 2048, 2560) + part(dfg_ref, 2560, NP))
        xv = x_ref[...]
        r1 = lax.rsqrt(jnp.mean(xv * xv, axis=-1, keepdims=True) + EPS)
        xn = xv * r1
        gain = n1g_ref[...] * (1.0 + mod_ref[1:2, :])
        t = dh * xn
        acc_ref[0] += _colsum8(dh)
        acc_ref[1] += _colsum8(t * n1g_ref[...])
        acc_ref[2] += _colsum8(t * (1.0 + mod_ref[1:2, :]))
        dxn = dh * gain
        dx_ref[...] = dx2_ref[...] + r1 * (dxn - xn * jnp.mean(dxn * xn, axis=-1, keepdims=True))

    outs = pl.pallas_call(
        _hosted(body, ex, 11, 2, 0, S // tm), name="bwd_in", grid=(S // tm,),
        out_shape=(_sds((S, D), F32), _sds((3, 8, D), F32), *(ex.out_shapes if ex else [])),
        in_specs=[_rows(tm, AW), _rows(tm, AW), _rows(tm, AW), _rows(tm, CW), _rows(tm, CW), _rows(tm, LANES),
                  _const((D, NP)), _rows(tm, D), _rows(tm, D), _const((8, D)), _const((1, D)),
                  *(ex.in_specs if ex else [])],
        out_specs=(_rows(tm, D), _const((3, 8, D)), *(ex.out_specs if ex else [])),
        scratch_shapes=ex.scratch if ex else [],
        compiler_params=_cp(("arbitrary",)),
    )(dq, dk, dv, dalin, dagate, dfg, w_in_p, x, dx2, mod8, n1g, *(ex.srcs if ex else []))
    return outs[:2], list(outs[2:])


def _adam(w, g, m, v):
    m_new = B1 * m + (1.0 - B1) * g
    v_new = B2 * v + (1.0 - B2) * (g * g)
    m_hat = m_new / (1.0 - B1 ** STEP)
    v_hat = v_new / (1.0 - B2 ** STEP)
    delta = -LR * (m_hat / (jnp.sqrt(v_hat) + AEPS) + WD * w)
    return delta, m_new, v_new


def _reduce_adamw(slots, w, m, v, name, tr=256):
    ns, R, C = slots.shape
    tr = tr if R % tr == 0 else R

    def body(s_ref, w_ref, m_ref, v_ref, g_ref, d_ref, mo_ref, vo_ref):
        g = s_ref[0].astype(F32)
        for k in range(1, ns):
            g = g + s_ref[k].astype(F32)
        g_ref[...] = g
        d_ref[...], mo_ref[...], vo_ref[...] = _adam(w_ref[...], g, m_ref[...], v_ref[...])

    blk = pl.BlockSpec((tr, C), lambda i: (i, 0))
    return pl.pallas_call(
        body, name=name, grid=(R // tr,), out_shape=tuple(_sds((R, C), F32) for _ in range(4)),
        in_specs=[pl.BlockSpec((ns, tr, C), lambda i: (0, i, 0)), blk, blk, blk], out_specs=(blk, blk, blk, blk),
        compiler_params=_cp(("parallel",)),
    )(slots, w, m, v)


def _pair_adamw(slots, w, m, v, name, tr=256):
    ns, R, C = slots.shape
    tr = tr if R % tr == 0 else R
    nt = R // tr

    def body(s_ref, w_ref, m_ref, v_ref, g_ref, d_ref, mo_ref, vo_ref, mine_ref, theirs_ref, send_sems, recv_sems):
        i = pl.program_id(0)
        part = s_ref[0].astype(F32)
        for k in range(1, ns):
            part = part + s_ref[k].astype(F32)
        mine_ref[i] = part
        swap = pltpu.make_async_remote_copy(
            src_ref=mine_ref.at[i], dst_ref=theirs_ref.at[i], send_sem=send_sems.at[i], recv_sem=recv_sems.at[i],
            device_id=(lax.axis_index("x"), lax.axis_index("y"), 1 - lax.axis_index("c")),
            device_id_type=pl.DeviceIdType.MESH)
        swap.start()
        swap.wait()
        g = part + theirs_ref[i]
        g_ref[...] = g
        d_ref[...], mo_ref[...], vo_ref[...] = _adam(w_ref[...], g, m_ref[...], v_ref[...])

    blk = pl.BlockSpec((tr, C), lambda i: (i, 0))
    return pl.pallas_call(
        body, name=name, grid=(nt,), out_shape=tuple(_sds((R, C), F32) for _ in range(4)),
        in_specs=[pl.BlockSpec((ns, tr, C), lambda i: (0, i, 0)), blk, blk, blk], out_specs=(blk, blk, blk, blk),
        scratch_shapes=[pltpu.VMEM((nt, tr, C), F32), pltpu.VMEM((nt, tr, C), F32),
                        pltpu.SemaphoreType.DMA((nt,)), pltpu.SemaphoreType.DMA((nt,))],
        compiler_params=_cp(("arbitrary",)),
    )(slots, w, m, v)


def _ada_adamw(sct, dmod, w, m, v):
    R, C = w.shape
    tr, bc = 256, 512

    def body(sct_ref, dm_ref, w_ref, m_ref, v_ref, g_ref, d_ref, mo_ref, vo_ref):
        g = sct_ref[:, 0:1] * dm_ref[0:1, :]
        for b in range(1, N_DEV):
            g = g + sct_ref[:, b:b + 1] * dm_ref[b:b + 1, :]
        g_ref[...] = g
        d_ref[...], mo_ref[...], vo_ref[...] = _adam(w_ref[...], g, m_ref[...], v_ref[...])

    blk = pl.BlockSpec((tr, bc), lambda i, j: (i, j))
    return pl.pallas_call(
        body, name="ada_adamw", grid=(R // tr, C // bc), out_shape=tuple(_sds((R, C), F32) for _ in range(4)),
        in_specs=[pl.BlockSpec((tr, N_DEV), lambda i, j: (i, 0)), pl.BlockSpec((N_DEV, bc), lambda i, j: (0, j)),
                  blk, blk, blk],
        out_specs=(blk, blk, blk, blk), compiler_params=_cp(("parallel", "parallel")),
    )(sct, dmod, w, m, v)


PACK = {"dmod": 0, "norm1_g": 6144, "norm2_g": 7168, "q_norm_g": 8192, "k_norm_g": 8704, "b_f": 9216, "conv_b": 9344,
        "conv_ln_g": 9856, "conv_ln_b": 10368, "beta_attn": 10880, "beta_conv": 11392}
SMALL_NAMES = ["b_ada", "norm1_g", "q_norm_g", "k_norm_g", "b_f", "conv_b", "conv_ln_g", "conv_ln_b", "beta_attn",
               "beta_conv", "norm2_g"]


def _pack_small(acc1, acc_d, acc_h, dg2, accg, accb, dcb):
    def body(a1_ref, ad_ref, ah_ref, dg2_ref, ag_ref, ab_ref, cb_ref, o_ref):
        def put(off, rows):
            o_ref[:, off:off + rows.shape[1]] = jnp.sum(rows, axis=0, keepdims=True)

        for t, rows in enumerate((a1_ref[0], a1_ref[1], ad_ref[3], ad_ref[0], ad_ref[1], dg2_ref[...])):
            put(PACK["dmod"] + t * D, rows)
        put(PACK["norm1_g"], a1_ref[2])
        put(PACK["norm2_g"], ad_ref[2])
        put(PACK["q_norm_g"], ag_ref[0])
        put(PACK["k_norm_g"], ag_ref[1])
        put(PACK["b_f"], ab_ref[...])
        put(PACK["conv_b"], cb_ref[...])
        put(PACK["conv_ln_g"], ah_ref[2])
        put(PACK["conv_ln_b"], ah_ref[3])
        put(PACK["beta_attn"], ah_ref[0])
        put(PACK["beta_conv"], ah_ref[1])

    vm = pl.BlockSpec(memory_space=pltpu.VMEM)
    return pl.pallas_call(body, name="pack_small", out_shape=_sds((1, SMALL_IN), F32), in_specs=[vm] * 7, out_specs=vm,
                          )(acc1, acc_d, acc_h, dg2, accg, accb, dcb)


def _small_adamw(slots, fold, ws, ms, vs):
    n = len(ws)
    widths = [w.shape[1] for w in ws]

    def body(s_ref, f_ref, *refs):
        w_refs, m_refs, v_refs = refs[:n], refs[n:2 * n], refs[2 * n:3 * n]
        outs = refs[3 * n:]
        tot = s_ref[0:1, :]
        for k in range(1, N_DEV):
            tot = tot + s_ref[k:k + 1, :]

        def grad(name, width):
            if name == "b_ada":
                return tot[:, 0:6 * D]
            seg = tot[:, PACK[name]:PACK[name] + max(width, LANES)]
            if name in ("q_norm_g", "k_norm_g"):
                seg = jnp.dot(jnp.broadcast_to(tot[:, PACK[name]:PACK[name] + AW], (8, AW)), f_ref[...], precision=HI,
                              preferred_element_type=F32)[0:1, :]
            return seg[:, 0:width]

        for t, (name, width) in enumerate(zip(SMALL_NAMES, widths)):
            g = grad(name, width)
            d, m_new, v_new = _adam(w_refs[t][...], g, m_refs[t][...], v_refs[t][...])
            outs[t][...] = g
            outs[n + t][...] = d
            outs[2 * n + t][...] = m_new
            outs[3 * n + t][...] = v_new

    vm = pl.BlockSpec(memory_space=pltpu.VMEM)
    outs = pl.pallas_call(
        body, name="adamw_small", out_shape=tuple(_sds((1, w), F32) for _ in range(4) for w in widths),
        in_specs=[vm] * (2 + 3 * n), out_specs=tuple(vm for _ in range(4 * n)),
    )(slots, fold, *ws, *ms, *vs)
    return [list(outs[k * n:(k + 1) * n]) for k in range(4)]


def _perm_in(w):
    pad = jnp.zeros((w.shape[0], NP - 2568), w.dtype)
    return jnp.concatenate([w[:, :1536], w[:, 1544:2568], w[:, 1536:1544], pad], axis=1)


def _pad_lanes(vec, n=LANES):
    return jnp.pad(vec, ((0, 0), (0, n - vec.shape[1])))


def kernel(x, c, w_ada, b_ada, norm1_g, w_in, q_norm_g, k_norm_g, b_f, conv_w, conv_b, conv_ln_g, conv_ln_b, beta_attn, beta_conv, w_out, norm2_g, w_ff1, w_ff2, loss_target, m_w_ada, m_b_ada, m_norm1_g, m_w_in, m_q_norm_g, m_k_norm_g, m_b_f, m_conv_w, m_conv_b, m_conv_ln_g, m_conv_ln_b, m_beta_attn, m_beta_conv, m_w_out, m_norm2_g, m_w_ff1, m_w_ff2, v_w_ada, v_b_ada, v_norm1_g, v_w_in, v_q_norm_g, v_k_norm_g, v_b_f, v_conv_w, v_conv_b, v_conv_ln_g, v_conv_ln_b, v_beta_attn, v_beta_conv, v_w_out, v_norm2_g, v_w_ff1, v_w_ff2):
    S = x.shape[1]
    tm = min(256, S)
    tw = min(512, S)
    tq = min(512, S // 2)
    xs, tgt = x[0], loss_target[0]
    chip = 2 * lax.axis_index("x") + lax.axis_index("y")
    e512, et512 = _head_sum_mats()

    conv_w32 = jnp.pad(conv_w[0], ((0, 1), (0, 0)))
    c_all, g_in = _exchange([(c, "bcast8"), (w_in[0].astype(BF16), "chip4")], "gather_in")
    later_weights = _Exchange([(w_out[0].astype(BF16), "chip4"), (conv_w32, "chip4")])
    c_all = c_all.reshape(N_DEV, D)
    w_in_p = _perm_in(jnp.transpose(g_in, (1, 0, 2)).reshape(D, 2568))

    b_shard = lax.dynamic_slice(b_ada, (0, chip * 1536), (1, 1536))
    mod_rows, sc_all = _mod_shard(c_all, w_ada[0], b_shard)
    (mod_slots,) = _exchange([(mod_rows.reshape(N_DEV, 1, 1536), "all8")], "scatter_mod")
    mod = mod_slots.reshape(4, 2, 1536)[:, 0, :].reshape(6, D)
    mod8 = jnp.pad(mod, ((0, 2), (0, 0)))

    qg512 = jnp.tile(q_norm_g, (1, NH))
    kg512 = jnp.tile(k_norm_g, (1, NH))
    bf128 = _pad_lanes(b_f)

    (h1, qh, kh, vb, qn, kn, rq, rk, fgb, alin, agate, u0), (g_out, g_cw) = _fwd_in(
        xs, mod8, norm1_g, w_in_p, e512, et512, qg512, kg512, bf128, tw, ex=later_weights)
    w_out_f = g_out.reshape(D, D)
    cw32 = jnp.transpose(g_cw, (1, 0, 2)).reshape(HALO, CW)
    bd = _logit_bound(q_norm_g, k_norm_g)
    (qa, ka, va, fcum), (w1,) = _fwd_decay(fgb, qh, kh, vb, _shift(bd), tm,
                                           ex=_Exchange([(w_ff1[0].astype(BF16), "chip4")]))
    fs, fe = _skip_tables(fcum, tq)
    o_attn, lser = _attn_fwd(qa, ka, va, fs, fe, bd, tq)
    (u1, mc), (w2,) = _fwd_conv(u0, cw32, conv_b, conv_ln_g, conv_ln_b, beta_conv, tm,
                                ex=_Exchange([(w_ff2[0].astype(BF16), "chip4")]))
    merged, ob, x2, h2 = _fwd_out(o_attn, mc, xs, mod8, norm2_g, beta_attn, w_out_f, tw)
    r, dy, loss8, dg2 = _fwd_ffn(h2, w1, w2, x2, tgt, mod8, tw)

    df2, df1, dh2 = _bwd_ffn(dy, mod8, r, w1, w2, tw)
    gw_ff2 = _wgrad(r, df2, "wgrad_ff2", square_a=True)
    gw_ff1 = _wgrad(h2, df1, "wgrad_ff1", col_pieces=True)
    ff_grads = _Exchange([(gw_ff1, "chip4p"), (gw_ff2.reshape(4, D, D), "chip4p")])
    (dx2, do, doa, du1, acc_d, acc_h, dr), (p_ff1, p_ff2) = _bwd_mid(
        dh2, dy, x2, ob, o_attn, u1, mod8, norm2_g, beta_attn, beta_conv, conv_ln_g, conv_ln_b, w_out_f, e512, tm, tq,
        ex=ff_grads)
    gw_out = _wgrad(merged, do, "wgrad_out")
    dqa, dka, dv = _attn_bwd(qa, ka, vb, doa, lser, dr, fs, fe, bd, tq)
    out_grads = _Exchange([(gw_out.reshape(4, 256, D), "chip4p")])
    (dalin, dagate, dcw, dcb), (p_out,) = _bwd_conv(du1, u0, alin, agate, cw32, tm, ex=out_grads)
    dq, dk, dfg, accg, accb = _bwd_qk(dqa, dka, qn, kn, rq, rk, fgb, qg512, kg512, e512, et512, tm)
    gw_in_p = _wgrad_in(h1, [dq, dk, dv, dalin, dagate, dfg])
    gw_in = jnp.concatenate([gw_in_p[:, :1536], gw_in_p[:, 2560:2568], gw_in_p[:, 1536:2560]], axis=1)
    s8 = lambda a: jnp.sum(a, axis=-2)
    gcw = s8(dcw)
    in_grads = _Exchange([(jnp.transpose(gw_in.reshape(D, 4, 642), (1, 0, 2)), "chip4p"),
                          (jnp.transpose(gcw.reshape(HALO, 4, LANES), (1, 0, 2)), "chip4p")])
    (grad_x, acc1), (p_in, p_cw) = _bwd_in(dq, dk, dv, dalin, dagate, dfg, w_in_p, xs, dx2, mod8, norm1_g, tw,
                                           ex=in_grads)

    small = _pack_small(acc1, acc_d, acc_h, dg2, accg, accb, dcb)
    (small_s,) = _exchange([(small, "bcast8")], "gather_small")
    small_s = small_s.reshape(N_DEV, SMALL_IN)

    g_in_, d_in, nm_in, nv_in = _pair_adamw(p_in, w_in[0], m_w_in[0], v_w_in[0], "adamw_in")
    g_out_, d_out, nm_out, nv_out = _pair_adamw(p_out, w_out[0], m_w_out[0], v_w_out[0], "adamw_out")
    g_f1, d_f1, nm_f1, nv_f1 = _pair_adamw(p_ff1, w_ff1[0], m_w_ff1[0], v_w_ff1[0], "adamw_ff1")
    g_f2, d_f2, nm_f2, nv_f2 = _pair_adamw(p_ff2, w_ff2[0], m_w_ff2[0], v_w_ff2[0], "adamw_ff2")
    pad_row = lambda a, fill: jnp.pad(a[0], ((0, 1), (0, 0)), constant_values=fill)
    g_cw_, d_cw, nm_cw, nv_cw = (a[:KC] for a in _pair_adamw(
        p_cw, pad_row(conv_w, 0.0), pad_row(m_conv_w, 0.0), pad_row(v_conv_w, 1.0), "adamw_conv_w"))
    dmod_shard = lax.dynamic_slice(small_s[:, :6 * D], (0, chip * 1536), (N_DEV, 1536))
    g_ada, d_ada, nm_ada, nv_ada = _ada_adamw(sc_all.T, dmod_shard, w_ada[0], m_w_ada[0], v_w_ada[0])

    fold = np.zeros((AW, LANES), np.float32)
    fold[np.arange(AW), np.arange(AW) % DH] = 1.0
    smalls = [b_ada, norm1_g, q_norm_g, k_norm_g, b_f, conv_b, conv_ln_g, conv_ln_b, beta_attn, beta_conv, norm2_g]
    m_smalls = [m_b_ada, m_norm1_g, m_q_norm_g, m_k_norm_g, m_b_f, m_conv_b, m_conv_ln_g, m_conv_ln_b, m_beta_attn,
                m_beta_conv, m_norm2_g]
    v_smalls = [v_b_ada, v_norm1_g, v_q_norm_g, v_k_norm_g, v_b_f, v_conv_b, v_conv_ln_g, v_conv_ln_b, v_beta_attn,
                v_beta_conv, v_norm2_g]
    gs, ds, ms, vs = _small_adamw(small_s, jnp.asarray(fold), smalls, m_smalls, v_smalls)

    loss = lax.psum(loss8[0, 0], ("x", "y", "c"))
    big = {"w_ada": (g_ada, d_ada, nm_ada, nv_ada), "w_in": (g_in_, d_in, nm_in, nv_in),
           "conv_w": (g_cw_, d_cw, nm_cw, nv_cw), "w_out": (g_out_, d_out, nm_out, nv_out),
           "w_ff1": (g_f1, d_f1, nm_f1, nv_f1), "w_ff2": (g_f2, d_f2, nm_f2, nv_f2)}
    order =["w_ada", "b_ada", "norm1_g", "w_in", "q_norm_g", "k_norm_g", "b_f", "conv_w", "conv_b", "conv_ln_g",
             "conv_ln_b", "beta_attn", "beta_conv", "w_out", "norm2_g", "w_ff1", "w_ff2"]

    def leaf(name, which):
        if name in big:
            return big[name][which][None]
        return (gs, ds, ms, vs)[which][SMALL_NAMES.index(name)]

    return (loss, grad_x[None], *[leaf(n, 0) for n in order], *[leaf(n, 1) for n in order],
            *[leaf(n, 2) for n in order], *[leaf(n, 3) for n in order])
```

```python
import functools

import numpy as np
import jax
import jax.numpy as jnp
from jax import lax
from jax.experimental import pallas as pl
from jax.experimental.pallas import tpu as pltpu

F32, BF16 = jnp.float32, jnp.bfloat16
HI = lax.Precision.HIGHEST
D = 1024
AW = 512
CW = 512
NH = 8
DH = 64
KC = 31
DFF = 4096
NP = 2688
EPS = 1e-6
NEG = -1e30
LANES = 128
VMEM_LIMIT = 56 * 2**20
NT = (((1,), (1,)), ((), ()))
TN = (((0,), (0,)), ((), ()))
LR, B1, B2, AEPS, WD, STEP = 0.001, 0.9, 0.999, 1e-08, 0.01, 10
N_DEV = 8
SMALL_IN = 11904


def _cp(sem=None, vmem=VMEM_LIMIT):
    kw = dict(vmem_limit_bytes=vmem)
    if sem is not None:
        kw["dimension_semantics"] = sem
    return pltpu.CompilerParams(**kw)


def _rows(tm, n):
    return pl.BlockSpec((tm, n), lambda i: (i, 0))


def _const(shape):
    nd = len(shape)
    return pl.BlockSpec(shape, lambda *_: (0,) * nd)


def _sds(shape, dt):
    return jax.ShapeDtypeStruct(shape, dt)


def _lane(shape):
    return lax.broadcasted_iota(jnp.int32, shape, len(shape) - 1)


def _sigmoid(x):
    return 1.0 / (1.0 + jnp.exp(-x))


class _Exchange:
    MASKS = {"chip4": (2, 4, 6), "chip4p": (2, 4, 6), "all8": (1, 2, 3, 4, 5, 6, 7), "bcast8": (1, 2, 3, 4, 5, 6, 7)}

    def __init__(self, items):
        self.srcs = [s for s, _ in items]
        self.kinds = [k for _, k in items]
        self.n = len(items)
        self.out_shapes = []
        for s, k in items:
            shape = {"all8": (N_DEV,) + s.shape[1:], "bcast8": (N_DEV,) + s.shape, "chip4": (4,) + s.shape,
                     "chip4p": (4,) + s.shape[1:]}[k]
            self.out_shapes.append(_sds(shape, s.dtype))
        self.sem_index = {}
        for t, k in enumerate(self.kinds):
            for m in self.MASKS[k]:
                self.sem_index[(t, m)] = len(self.sem_index)
        n_sem = len(self.sem_index)
        self.scratch = [pltpu.SemaphoreType.DMA((n_sem,)), pltpu.SemaphoreType.DMA((n_sem,)),
                        pltpu.SemaphoreType.DMA((self.n,))]
        self.in_specs = [pl.BlockSpec(memory_space=pl.ANY)] * self.n
        self.out_specs = [pl.BlockSpec(memory_space=pl.ANY)] * self.n

    def copies(self, src_refs, dst_refs, send_sems, recv_sems, local_sems):
        x, y, c = lax.axis_index("x"), lax.axis_index("y"), lax.axis_index("c")
        my_id = 4 * x + 2 * y + c
        my_chip = 2 * x + y

        def piece(t, dev_id, chip):
            k = self.kinds[t]
            return src_refs[t].at[dev_id] if k == "all8" else src_refs[t].at[chip] if k == "chip4p" else src_refs[t]

        out = []
        for t in range(self.n):
            slot = dst_refs[t].at[my_chip if self.kinds[t] in ("chip4", "chip4p") else my_id]
            out.append(pltpu.make_async_copy(piece(t, my_id, my_chip), slot, local_sems.at[t]))
            for m in self.MASKS[self.kinds[t]]:
                px = 1 - x if m & 4 else x
                py = 1 - y if m & 2 else y
                pc = 1 - c if m & 1 else c
                s = self.sem_index[(t, m)]
                out.append(pltpu.make_async_remote_copy(
                    src_ref=piece(t, 4 * px + 2 * py + pc, 2 * px + py), dst_ref=slot,
                    send_sem=send_sems.at[s], recv_sem=recv_sems.at[s],
                    device_id=(px, py, pc), device_id_type=pl.DeviceIdType.MESH))
        return out


def _hosted(body, ex, n_in, n_out, n_scr, n_steps):
    if ex is None:
        return body

    def wrapped(*refs):
        ins, xin = refs[:n_in], refs[n_in:n_in + ex.n]
        o0 = n_in + ex.n
        outs, xout = refs[o0:o0 + n_out], refs[o0 + n_out:o0 + n_out + ex.n]
        s0 = o0 + n_out + ex.n
        scr, sems = refs[s0:s0 + n_scr], refs[s0 + n_scr:]

        @pl.when(pl.program_id(0) == 0)
        def _():
            for cp in ex.copies(xin, xout, *sems):
                cp.start()

        body(*ins, *outs, *scr)

        @pl.when(pl.program_id(0) == n_steps - 1)
        def _():
            for cp in ex.copies(xin, xout, *sems):
                cp.wait()

    return wrapped


def _exchange(items, name):
    ex = _Exchange(items)
    n = ex.n

    def body(*refs):
        copies = ex.copies(refs[:n], refs[n:2 * n], *refs[2 * n:])
        for cp in copies:
            cp.start()
        for cp in copies:
            cp.wait()

    outs = pl.pallas_call(
        body, name=name, out_shape=tuple(ex.out_shapes), in_specs=ex.in_specs, out_specs=tuple(ex.out_specs),
        scratch_shapes=ex.scratch,
    )(*ex.srcs)
    return list(outs)


def _mod_shard(c_all, w_ada, b_shard):
    n = w_ada.shape[1]

    def body(c_ref, w_ref, b_ref, o_ref, sc_ref):
        cv = c_ref[...]
        sc = cv * _sigmoid(cv)
        sc_ref[...] = sc
        o_ref[...] = jnp.dot(sc, w_ref[...], precision=HI, preferred_element_type=F32) + b_ref[...]

    bn = 512
    return pl.pallas_call(
        body, name="mod_shard", out_shape=(_sds((N_DEV, n), F32), _sds((N_DEV, D), F32)), grid=(n // bn,),
        in_specs=[_const((N_DEV, D)), pl.BlockSpec((D, bn), lambda j: (0, j)), pl.BlockSpec((1, bn), lambda j: (0, j))],
        out_specs=(pl.BlockSpec((N_DEV, bn), lambda j: (0, j)), _const((N_DEV, D))),
        compiler_params=_cp(("arbitrary",)),
    )(c_all, w_ada, b_shard)


def _head_sum_mats():
    e = np.zeros((AW, LANES), np.float32)
    for h in range(NH):
        e[h * DH:(h + 1) * DH, h] = 1.0
    return jnp.asarray(e, BF16), jnp.asarray(e.T.copy(), BF16)


def _dot2(x, w):
    hi = x.astype(BF16)
    lo = (x - hi.astype(F32)).astype(BF16)
    return jnp.dot(hi, w, preferred_element_type=F32) + jnp.dot(lo, w, preferred_element_type=F32)


def _fwd_in(x, mod8, n1g, w_in_p, e512, et512, qg512, kg512, bf128, tm, ex=None):
    S = x.shape[0]

    def body(x_ref, mod_ref, n1g_ref, w_ref, e_ref, et_ref, qg_ref, kg_ref, bf_ref,
             h1_ref, qh_ref, kh_ref, v_ref, qn_ref, kn_ref, rq_ref, rk_ref, fgb_ref, alin_ref, agate_ref, u0_ref):
        xv = x_ref[...]
        r1 = lax.rsqrt(jnp.mean(xv * xv, axis=-1, keepdims=True) + EPS)
        h = (xv * r1) * (n1g_ref[...] * (1.0 + mod_ref[1:2, :])) + mod_ref[0:1, :]
        hb = h.astype(BF16)
        h1_ref[...] = hb

        def seg(a, b):
            return jnp.dot(hb, w_ref[:, a:b], preferred_element_type=F32)

        def headnorm(t, g_ref, scale, n_ref, r_ref, o_ref):
            ss = _dot2(t * t, e_ref[...])
            r = lax.rsqrt(ss * (1.0 / DH) + EPS)
            tn = t * _dot2(r, et_ref[...])
            n_ref[...] = tn.astype(BF16)
            r_ref[...] = r
            o_ref[...] = (tn * (g_ref[...] * scale)).astype(BF16)

        headnorm(seg(0, 512), qg_ref, DH ** -0.5, qn_ref, rq_ref, qh_ref)
        headnorm(seg(512, 1024), kg_ref, 1.0, kn_ref, rk_ref, kh_ref)
        v_ref[...] = seg(1024, 1536).astype(BF16)
        alin = seg(1536, 2048)
        agate = seg(2048, 2560)
        alin_ref[...] = alin.astype(BF16)
        agate_ref[...] = agate.astype(BF16)
        u0_ref[...] = alin * _sigmoid(agate)
        fgb_ref[...] = seg(2560, NP) + bf_ref[...]

    bf = lambda: _sds((S, AW), BF16)
    xs = ex.srcs if ex else []
    outs = pl.pallas_call(
        _hosted(body, ex, 9, 12, 0, S // tm), name="fwd_in", grid=(S // tm,),
        out_shape=(_sds((S, D), BF16), bf(), bf(), bf(), bf(), bf(), _sds((S, LANES), F32), _sds((S, LANES), F32),
                   _sds((S, LANES), F32), bf(), bf(), _sds((S, CW), F32), *(ex.out_shapes if ex else [])),
        in_specs=[_rows(tm, D), _const((8, D)), _const((1, D)), _const((D, NP)), _const((AW, LANES)), _const((LANES, AW)),
                  _const((1, AW)), _const((1, AW)), _const((1, LANES)), *(ex.in_specs if ex else [])],
        out_specs=(_rows(tm, D), _rows(tm, AW), _rows(tm, AW), _rows(tm, AW), _rows(tm, AW), _rows(tm, AW),
                   _rows(tm, LANES), _rows(tm, LANES), _rows(tm, LANES), _rows(tm, AW), _rows(tm, AW), _rows(tm, CW),
                   *(ex.out_specs if ex else [])),
        scratch_shapes=ex.scratch if ex else [],
        compiler_params=_cp(("arbitrary",)),
    )(x, mod8, n1g, w_in_p, e512, et512, qg512, kg512, bf128, *xs)
    return outs[:12], list(outs[12:])


def _split3(f):
    f1 = f.astype(BF16).astype(F32)
    f2 = (f - f1).astype(BF16).astype(F32)
    return f1, f2, f - f1 - f2


def _dot3(w, x):
    return sum(jnp.dot(w, piece.astype(BF16), preferred_element_type=F32) for piece in _split3(x))


def _fwd_decay(fgb, qh, kh, vb, shift, tm, ex=None):
    S = fgb.shape[0]

    def body(shift_ref, fgb_ref, qh_ref, kh_ref, vb_ref, qa_ref, ka_ref, va_ref, f_ref, carry_ref):
        @pl.when(pl.program_id(0) == 0)
        def _():
            carry_ref[...] = jnp.zeros_like(carry_ref)

        fb = fgb_ref[...]
        lf = jnp.minimum(fb, 0.0) - jnp.log1p(jnp.exp(-jnp.abs(fb)))
        tri = (lax.broadcasted_iota(jnp.int32, (tm, tm), 0) >= lax.broadcasted_iota(jnp.int32, (tm, tm), 1)
               ).astype(F32).astype(BF16)
        cs = _dot3(tri, lf) + carry_ref[0:1, :]
        f_ref[...] = cs
        carry_ref[...] = jnp.broadcast_to(cs[tm - 1:tm, :], carry_ref.shape)
        lane = _lane((tm, LANES))
        s1, s2, s3 = _split3(jnp.zeros((tm, LANES), F32) - shift_ref[0, 0])
        tail_q = jnp.where((lane >= 67) & (lane < 70), 1.0,
                           jnp.where(lane == 70, s1, jnp.where(lane == 71, s2, jnp.where(lane == 72, s3, 0.0))))
        tail_k = jnp.where(((lane >= 64) & (lane < 67)) | ((lane >= 70) & (lane < 73)), 1.0, 0.0)
        tail_v = jnp.where(lane == DH, 1.0, 0.0)
        for p in range(NH // 2):
            qp = qh_ref[:, p * LANES:(p + 1) * LANES].astype(F32)
            kp = kh_ref[:, p * LANES:(p + 1) * LANES].astype(F32)
            vp = vb_ref[:, p * LANES:(p + 1) * LANES].astype(F32)
            for hh in range(2):
                h = 2 * p + hh
                f1, f2, f3 = _split3(cs[:, h:h + 1])
                qb = qp if hh == 0 else pltpu.roll(qp, 64, 1)
                kb = kp if hh == 0 else pltpu.roll(kp, 64, 1)
                vh = vp if hh == 0 else pltpu.roll(vp, 64, 1)
                augq = jnp.where(lane == 64, f1, jnp.where(lane == 65, f2, jnp.where(lane == 66, f3, tail_q)))
                augk = jnp.where(lane == 67, -f1, jnp.where(lane == 68, -f2, jnp.where(lane == 69, -f3, tail_k)))
                qa_ref[h] = jnp.where(lane < DH, qb, augq).astype(BF16)
                ka_ref[h] = jnp.where(lane < DH, kb, augk).astype(BF16)
                va_ref[h] = jnp.where(lane < DH, vh, tail_v).astype(BF16)

    hm = pl.BlockSpec((NH, tm, LANES), lambda i: (0, i, 0))
    hms = _sds((NH, S, LANES), BF16)
    outs = pl.pallas_call(
        _hosted(body, ex, 5, 4, 1, S // tm), name="fwd_decay", grid=(S // tm,),
        out_shape=(hms, hms, hms, _sds((S, LANES), F32), *(ex.out_shapes if ex else [])),
        in_specs=[SMEM_SPEC, _rows(tm, LANES), _rows(tm, AW), _rows(tm, AW), _rows(tm, AW),
                  *(ex.in_specs if ex else [])],
        out_specs=(hm, hm, hm, _rows(tm, LANES), *(ex.out_specs if ex else [])),
        scratch_shapes=[pltpu.VMEM((8, LANES), F32), *(ex.scratch if ex else [])],
        compiler_params=_cp(("arbitrary",)),
    )(shift, fgb, qh, kh, vb, *(ex.srcs if ex else []))
    return outs[:4], list(outs[4:])


SKIP = -106.0


def _block_loops(first, step, needed, run):
    def both(j):
        return jnp.logical_and(needed(0, j), needed(1, j))

    def walk(heads):
        def go(j):
            run(j, heads)
            return j + step
        return go

    j = lax.while_loop(both, walk((0, 1)), first)
    lax.while_loop(functools.partial(needed, 0), walk((0,)), j)
    lax.while_loop(functools.partial(needed, 1), walk((1,)), j)


def _logit_bound(qg, kg):
    return (2.0 * 1.03 * DH ** 0.5 * jnp.max(jnp.abs(qg)) * jnp.max(jnp.abs(kg))).reshape(1, 1)


def _skip_tables(f, tq):
    return f[0::tq, :NH].T, f[tq - 1::tq, :NH].T


SMEM_SPEC = pl.BlockSpec(memory_space=pltpu.SMEM)


def _causal_rect(rows, cols, col0):
    return (lax.broadcasted_iota(jnp.int32, (rows, cols), 0)
            >= lax.broadcasted_iota(jnp.int32, (rows, cols), 1) + col0)


SHIFT_MAX = 60.0


def _shift(bd):
    return jnp.where(bd <= SHIFT_MAX, 0.5 * bd, 0.0)


def _attn_fwd(qa, ka, va, fs, fe, bd, tq):
    S = qa.shape[1]
    nq = S // tq

    def body(fs_ref, fe_ref, bd_ref, qa_ref, ka_ref, va_ref, o_ref, lser_ref, m_ref, acc_ref):
        pr, i = pl.program_id(0), pl.program_id(1)
        sel_a = _lane((tq, LANES)) < DH
        acc_ref[...] = jnp.zeros_like(acc_ref)

        def logits(j, hh, masked):
            start = pl.multiple_of(j * tq, tq)
            s = lax.dot_general(qa_ref[hh], ka_ref[hh, pl.ds(start, tq), :], NT, preferred_element_type=F32)
            if masked:
                s = jnp.where(_causal_rect(tq, tq, 0), s, NEG)
            return s, va_ref[hh, pl.ds(start, tq), :]

        def shifted_step(j, heads, masked=False):
            for hh in heads:
                s, vb = logits(j, hh, masked)
                acc_ref[hh] += jnp.dot(jnp.exp(s).astype(BF16), vb, preferred_element_type=F32)

        def online_step(j, heads, masked=False):
            for hh in heads:
                s, vb = logits(j, hh, masked)
                m_prev = m_ref[hh]
                m_new = jnp.maximum(m_prev, jnp.max(s, axis=1, keepdims=True))
                p = jnp.exp(s - jnp.tile(m_new, (1, tq // LANES)))
                m_ref[hh] = m_new
                acc_ref[hh] = jnp.exp(m_prev - m_new) * acc_ref[hh] + jnp.dot(p.astype(BF16), vb,
                                                                              preferred_element_type=F32)

        def needed(hh, j):
            top = fs_ref[2 * pr + hh, i] + bd_ref[0, 0]
            return jnp.logical_and(j >= 0, top - fe_ref[2 * pr + hh, jnp.maximum(j, 0)] >= SKIP)

        @pl.when(bd_ref[0, 0] <= SHIFT_MAX)
        def _():
            m_ref[...] = jnp.zeros_like(m_ref)
            shifted_step(i, (0, 1), masked=True)
            _block_loops(i - 1, -1, needed, shifted_step)

        @pl.when(bd_ref[0, 0] > SHIFT_MAX)
        def _():
            m_ref[...] = jnp.full(m_ref.shape, NEG, F32)
            online_step(i, (0, 1), masked=True)
            _block_loops(i - 1, -1, needed, online_step)

        outs, lses = [], []
        for hh in range(2):
            acc = acc_ref[hh]
            row_sum = jnp.broadcast_to(acc[:, DH:DH + 1], (tq, LANES))
            outs.append(acc / row_sum)
            lses.append(m_ref[hh] + jnp.log(row_sum))
        o_ref[...] = jnp.where(sel_a, outs[0], pltpu.roll(outs[1], 64, 1))
        row = lax.broadcasted_iota(jnp.int32, (8, tq), 0)
        lser_ref[0, 0] = jnp.where(row == 0, lses[0].T[0:8, :], lses[1].T[0:8, :])

    return pl.pallas_call(
        body, name="attn_fwd", grid=(NH // 2, nq),
        out_shape=(_sds((S, AW), F32), _sds((NH // 2, nq, 8, tq), F32)),
        in_specs=[SMEM_SPEC, SMEM_SPEC, SMEM_SPEC,
                  pl.BlockSpec((2, tq, LANES), lambda p, i: (p, i, 0)),
                  pl.BlockSpec((2, S, LANES), lambda p, i: (p, 0, 0)),
                  pl.BlockSpec((2, S, LANES), lambda p, i: (p, 0, 0))],
        out_specs=(pl.BlockSpec((tq, LANES), lambda p, i: (i, p)),
                   pl.BlockSpec((1, 1, 8, tq), lambda p, i: (p, i, 0, 0))),
        scratch_shapes=[pltpu.VMEM((2, tq, LANES), F32), pltpu.VMEM((2, tq, LANES), F32)],
        compiler_params=_cp(("parallel", "parallel")),
    )(fs, fe, bd, qa, ka, va)


HALO = 32
FWD_CHUNK = 64
BWD_CHUNK = 32


def _halo_prev(tm):
    return pl.BlockSpec((HALO, CW), lambda i: (jnp.maximum(i * (tm // HALO) - 1, 0), 0))


SUBLANES = 8
SHIFT_ROWS = 24


def _shifted_copies(ext_ref, sh_ref, tm):
    for k in range(1, SUBLANES):
        sh_ref[k - 1, 0:tm + SHIFT_ROWS, :] = ext_ref[k:k + tm + SHIFT_ROWS, :]


def _tap_windows(ext_ref, sh_ref, r0, offset, rows, shared):
    for k in range(SUBLANES):
        taps = sorted((offset(j), j) for j in range(KC) if offset(j) % SUBLANES == k)
        for group in ([taps] if shared and taps else [[t] for t in taps]):
            lo, hi = r0 + group[0][0] - k, r0 + group[-1][0] - k + rows
            window = ext_ref[lo:hi, :] if k == 0 else sh_ref[k - 1, lo:hi, :]
            for off, j in group:
                at = r0 + off - k - lo
                yield j, window[at:at + rows, :]


def _fwd_conv(u0, w32, cb, lng, lnb, beta_c, tm, ex=None):
    S = u0.shape[0]

    def body(cur_ref, prev_ref, w_ref, cb_ref, lng_ref, lnb_ref, beta_ref, u1_ref, mc_ref, ext_ref, sh_ref):
        i = pl.program_id(0)
        ext_ref[0:HALO, :] = jnp.where(i == 0, 0.0, prev_ref[...])
        ext_ref[HALO:, :] = cur_ref[...]
        _shifted_copies(ext_ref, sh_ref, tm)
        for r0 in range(0, tm, FWD_CHUNK):
            acc = jnp.zeros((FWD_CHUNK, CW), F32) + cb_ref[...]
            for j, rows in _tap_windows(ext_ref, sh_ref, r0, lambda j: 2 + j, FWD_CHUNK, shared=False):
                acc = acc + w_ref[j:j + 1, :] * rows
            u1_ref[r0:r0 + FWD_CHUNK, :] = acc
        u1 = u1_ref[...]
        mu = jnp.mean(u1, axis=-1, keepdims=True)
        d = u1 - mu
        rstd = lax.rsqrt(jnp.mean(d * d, axis=-1, keepdims=True) + EPS)
        u2 = d * rstd * lng_ref[...] + lnb_ref[...]
        u3 = u2 * _sigmoid(u2)
        rc = lax.rsqrt(jnp.mean(u3 * u3, axis=-1, keepdims=True) + EPS)
        mc_ref[...] = (u3 * rc * beta_ref[...]).astype(BF16)

    outs = pl.pallas_call(
        _hosted(body, ex, 7, 2, 2, S // tm), name="fwd_conv", grid=(S // tm,),
        out_shape=(_sds((S, CW), F32), _sds((S, CW), BF16), *(ex.out_shapes if ex else [])),
        in_specs=[_rows(tm, CW), _halo_prev(tm), _const((HALO, CW)), _const((1, CW)), _const((1, CW)), _const((1, CW)),
                  _const((1, CW)), *(ex.in_specs if ex else [])],
        out_specs=(_rows(tm, CW), _rows(tm, CW), *(ex.out_specs if ex else [])),
        scratch_shapes=[pltpu.VMEM((tm + HALO, CW), F32), pltpu.VMEM((SUBLANES - 1, tm + HALO, CW), F32),
                        *(ex.scratch if ex else [])],
        compiler_params=_cp(("arbitrary",)),
    )(u0, u0, w32, cb, lng, lnb, beta_c, *(ex.srcs if ex else []))
    return outs[:2], list(outs[2:])


def _fwd_out(o_attn, mc, x, mod8, n2g, beta_a, w_out, tm):
    S = x.shape[0]

    def body(o_ref, mc_ref, x_ref, mod_ref, n2g_ref, beta_ref, w_ref, mg_ref, ob_ref, x2_ref, h2_ref):
        ov = o_ref[...]
        ra = lax.rsqrt(jnp.mean(ov * ov, axis=-1, keepdims=True) + EPS)
        ma = (ov * ra * beta_ref[...]).astype(BF16)
        mcv = mc_ref[...]
        mg_ref[:, 0:AW] = ma
        mg_ref[:, AW:D] = mcv
        o = (jnp.dot(ma, w_ref[0:AW, :], preferred_element_type=F32)
             + jnp.dot(mcv, w_ref[AW:D, :], preferred_element_type=F32))
        ob_ref[...] = o.astype(BF16)
        x2 = x_ref[...] + mod_ref[2:3, :] * o
        x2_ref[...] = x2
        r2 = lax.rsqrt(jnp.mean(x2 * x2, axis=-1, keepdims=True) + EPS)
        h2_ref[...] = ((x2 * r2) * (n2g_ref[...] * (1.0 + mod_ref[4:5, :])) + mod_ref[3:4, :]).astype(BF16)

    return pl.pallas_call(
        body, name="fwd_out", grid=(S // tm,),
        out_shape=(_sds((S, D), BF16), _sds((S, D), BF16), _sds((S, D), F32), _sds((S, D), BF16)),
        in_specs=[_rows(tm, AW), _rows(tm, CW), _rows(tm, D), _const((8, D)), _const((1, D)), _const((1, AW)),
                  _const((D, D))],
        out_specs=(_rows(tm, D), _rows(tm, D), _rows(tm, D), _rows(tm, D)),
        compiler_params=_cp(("parallel",)),
    )(o_attn, mc, x, mod8, n2g, beta_a, w_out)


def _fwd_ffn(h2, w1, w2, x2, tgt, mod8, tm):
    S = h2.shape[0]
    nk = w1.shape[0]
    bf = w1.shape[2]

    def body(h2_ref, w1_ref, w2_ref, x2_ref, tgt_ref, mod_ref, r_ref, dy_ref, loss_ref, dg2_ref):
        @pl.when(pl.program_id(0) == 0)
        def _():
            loss_ref[...] = jnp.zeros_like(loss_ref)
            dg2_ref[...] = jnp.zeros_like(dg2_ref)

        f2 = None
        for k in range(nk):
            r = jnp.maximum(jnp.dot(h2_ref[...], w1_ref[k], preferred_element_type=F32), 0.0)
            r_ref[:, k * bf:(k + 1) * bf] = r.astype(BF16)
            part = jnp.dot((r * r).astype(BF16), w2_ref[k], preferred_element_type=F32)
            f2 = part if f2 is None else f2 + part
        e = x2_ref[...] + mod_ref[5:6, :] * f2 - tgt_ref[...]
        dy = e * (1.0 / D)
        dy_ref[...] = dy
        loss_ref[...] += 0.5 * jnp.sum(jnp.sum(e * dy, axis=1, keepdims=True), axis=0, keepdims=True)
        dg2_ref[...] += jnp.sum((dy * f2).reshape(tm // 8, 8, D), axis=0)

    once = pl.Buffered(1)
    return pl.pallas_call(
        body, name="fwd_ffn", grid=(S // tm,),
        out_shape=(_sds((S, DFF), BF16), _sds((S, D), F32), _sds((8, LANES), F32), _sds((8, D), F32)),
        in_specs=[_rows(tm, D), pl.BlockSpec((nk, D, bf), lambda i: (0, 0, 0), pipeline_mode=once),
                  pl.BlockSpec((nk, bf, D), lambda i: (0, 0, 0), pipeline_mode=once), _rows(tm, D), _rows(tm, D),
                  _const((8, D))],
        out_specs=(_rows(tm, DFF), _rows(tm, D), _const((8, LANES)), _const((8, D))),
        compiler_params=_cp(("arbitrary",)),
    )(h2, w1, w2, x2, tgt, mod8)


def _bwd_ffn(dy, mod8, r, w1, w2, tm):
    S = dy.shape[0]
    nk = w1.shape[0]
    bf = w1.shape[2]

    def body(dy_ref, mod_ref, r_ref, w1_ref, w2_ref, df2_ref, df1_ref, dh2_ref):
        df2 = (dy_ref[...] * mod_ref[5:6, :]).astype(BF16)
        df2_ref[...] = df2
        dh2 = None
        for k in range(nk):
            da = lax.dot_general(df2, w2_ref[k], NT, preferred_element_type=F32)
            df1 = (da * (2.0 * r_ref[:, k * bf:(k + 1) * bf].astype(F32))).astype(BF16)
            df1_ref[:, k * bf:(k + 1) * bf] = df1
            part = lax.dot_general(df1, w1_ref[k], NT, preferred_element_type=F32)
            dh2 = part if dh2 is None else dh2 + part
        dh2_ref[...] = dh2

    once = pl.Buffered(1)
    return pl.pallas_call(
        body, name="bwd_ffn", grid=(S // tm,),
        out_shape=(_sds((S, D), BF16), _sds((S, DFF), BF16), _sds((S, D), F32)),
        in_specs=[_rows(tm, D), _const((8, D)), _rows(tm, DFF),
                  pl.BlockSpec((nk, D, bf), lambda i: (0, 0, 0), pipeline_mode=once),
                  pl.BlockSpec((nk, bf, D), lambda i: (0, 0, 0), pipeline_mode=once)],
        out_specs=(_rows(tm, D), _rows(tm, DFF), _rows(tm, D)),
        compiler_params=_cp(("parallel",)),
    )(dy, mod8, r, w1, w2)


def _wgrad(a, b, name, square_a=False, col_pieces=False, tk=1024, bm=1024, bn=1024):
    S, M = a.shape
    N = b.shape[1]
    bm, bn, tk = min(bm, M), min(bn, N), min(tk, S)
    nk = S // tk

    def body(a_ref, b_ref, o_ref, acc_ref):
        av = a_ref[...]
        if square_a:
            af = av.astype(F32)
            av = (af * af).astype(BF16)
        part = lax.dot_general(av, b_ref[...], TN, preferred_element_type=F32)

        @pl.when(pl.program_id(2) == 0)
        def _():
            acc_ref[...] = part

        @pl.when(pl.program_id(2) > 0)
        def _():
            acc_ref[...] += part

        @pl.when(pl.program_id(2) == nk - 1)
        def _():
            if col_pieces:
                o_ref[0] = acc_ref[...].astype(BF16)
            else:
                o_ref[...] = acc_ref[...].astype(BF16)

    if col_pieces:
        out_shape, out_spec = _sds((N // bn, M, bn), BF16), pl.BlockSpec((1, bm, bn), lambda mi, ni, k: (ni, mi, 0))
    else:
        out_shape, out_spec = _sds((M, N), BF16), pl.BlockSpec((bm, bn), lambda mi, ni, k: (mi, ni))
    return pl.pallas_call(
        body, name=name, grid=(M // bm, N // bn, nk), out_shape=out_shape,
        in_specs=[pl.BlockSpec((tk, bm), lambda mi, ni, k: (k, mi)), pl.BlockSpec((tk, bn), lambda mi, ni, k: (k, ni))],
        out_specs=out_spec, scratch_shapes=[pltpu.VMEM((bm, bn), F32)],
        compiler_params=_cp(("parallel", "parallel", "arbitrary")),
    )(a, b)


def _wgrad_in(h1, pieces, tk=1024):
    S = h1.shape[0]
    tk = min(tk, S)
    widths = [p.shape[1] for p in pieces]
    offs = [sum(widths[:t]) for t in range(len(widths))]

    def body(a_ref, *refs):
        o_ref, acc_ref = refs[-2:]

        @pl.when(pl.program_id(0) == 0)
        def _():
            acc_ref[...] = jnp.zeros_like(acc_ref)

        for b_ref, off, w in zip(refs[:-2], offs, widths):
            acc_ref[:, off:off + w] += lax.dot_general(a_ref[...], b_ref[...], TN, preferred_element_type=F32)

        @pl.when(pl.program_id(0) == S // tk - 1)
        def _():
            o_ref[...] = acc_ref[...].astype(BF16)

    return pl.pallas_call(
        body, name="wgrad_in", grid=(S // tk,), out_shape=_sds((D, NP), BF16),
        in_specs=[_rows(tk, D)] + [_rows(tk, w) for w in widths], out_specs=_const((D, NP)),
        scratch_shapes=[pltpu.VMEM((D, NP), F32)], compiler_params=_cp(("arbitrary",)),
    )(h1, *pieces)


def _colsum8(t):
    return jnp.sum(t.reshape(t.shape[0] // 8, 8, t.shape[1]), axis=0)


def _bwd_mid(dh2, dy, x2, ob, o_attn, u1, mod8, n2g, beta_a, beta_c, lng, lnb, w_out, e512, tm, tq, ex=None):
    S = dy.shape[0]

    def body(dh2_ref, dy_ref, x2_ref, ob_ref, oa_ref, u1_ref, mod_ref, n2g_ref, ba_ref, bc_ref, lng_ref, lnb_ref, w_ref,
             e_ref, dx2_ref, do_ref, doa_ref, du1_ref, acc_d_ref, acc_h_ref, dr_ref):
        @pl.when(pl.program_id(0) == 0)
        def _():
            acc_d_ref[...] = jnp.zeros_like(acc_d_ref)
            acc_h_ref[...] = jnp.zeros_like(acc_h_ref)

        x2 = x2_ref[...]
        dh2 = dh2_ref[...]
        r2 = lax.rsqrt(jnp.mean(x2 * x2, axis=-1, keepdims=True) + EPS)
        xn2 = x2 * r2
        gain = n2g_ref[...] * (1.0 + mod_ref[4:5, :])
        dxn = dh2 * gain
        dx2 = dy_ref[...] + r2 * (dxn - xn2 * jnp.mean(dxn * xn2, axis=-1, keepdims=True))
        dx2_ref[...] = dx2
        t = dh2 * xn2
        acc_d_ref[0] += _colsum8(dh2)
        acc_d_ref[1] += _colsum8(t * n2g_ref[...])
        acc_d_ref[2] += _colsum8(t * (1.0 + mod_ref[4:5, :]))
        acc_d_ref[3] += _colsum8(dx2 * ob_ref[...].astype(F32))
        do = (dx2 * mod_ref[2:3, :]).astype(BF16)
        do_ref[...] = do
        dma = lax.dot_general(do, w_ref[0:AW, :], NT, preferred_element_type=F32)
        dmc = lax.dot_general(do, w_ref[AW:D, :], NT, preferred_element_type=F32)
        ov = oa_ref[...]
        ra = lax.rsqrt(jnp.mean(ov * ov, axis=-1, keepdims=True) + EPS)
        on = ov * ra
        acc_h_ref[0] += _colsum8(dma * on)
        don = dma * ba_ref[...]
        doa = (ra * (don - on * jnp.mean(don * on, axis=-1, keepdims=True))).astype(BF16)
        doa_ref[...] = doa
        delta_t = _dot2(doa.astype(F32) * ov, e_ref[...]).T
        for p in range(NH // 2):
            dr_ref[p, 0] = delta_t[2 * p:2 * p + 8, :]
        u1 = u1_ref[...]
        mu = jnp.mean(u1, axis=-1, keepdims=True)
        d = u1 - mu
        rstd = lax.rsqrt(jnp.mean(d * d, axis=-1, keepdims=True) + EPS)
        uh = d * rstd
        u2 = uh * lng_ref[...] + lnb_ref[...]
        sg = _sigmoid(u2)
        u3 = u2 * sg
        rc = lax.rsqrt(jnp.mean(u3 * u3, axis=-1, keepdims=True) + EPS)
        u3n = u3 * rc
        acc_h_ref[1] += _colsum8(dmc * u3n)
        du3n = dmc * bc_ref[...]
        du3 = rc * (du3n - u3n * jnp.mean(du3n * u3n, axis=-1, keepdims=True))
        du2 = du3 * (sg * (1.0 + u2 * (1.0 - sg)))
        acc_h_ref[2] += _colsum8(du2 * uh)
        acc_h_ref[3] += _colsum8(du2)
        duh = du2 * lng_ref[...]
        du1_ref[...] = rstd * (duh - jnp.mean(duh, axis=-1, keepdims=True)
                               - uh * jnp.mean(duh * uh, axis=-1, keepdims=True))

    per = tq // tm
    outs = pl.pallas_call(
        _hosted(body, ex, 14, 7, 0, S // tm), name="bwd_mid", grid=(S // tm,),
        out_shape=(_sds((S, D), F32), _sds((S, D), BF16), _sds((S, AW), BF16), _sds((S, CW), F32),
                   _sds((4, 8, D), F32), _sds((4, 8, AW), F32), _sds((NH // 2, S // tq, 8, tq), F32),
                   *(ex.out_shapes if ex else [])),
        in_specs=[_rows(tm, D), _rows(tm, D), _rows(tm, D), _rows(tm, D), _rows(tm, AW), _rows(tm, CW), _const((8, D)),
                  _const((1, D)), _const((1, AW)), _const((1, CW)), _const((1, CW)), _const((1, CW)), _const((D, D)),
                  _const((AW, LANES)), *(ex.in_specs if ex else [])],
        out_specs=(_rows(tm, D), _rows(tm, D), _rows(tm, AW), _rows(tm, CW), _const((4, 8, D)), _const((4, 8, AW)),
                   pl.BlockSpec((NH // 2, 1, 8, tm), lambda i: (0, i // per, 0, i % per)),
                   *(ex.out_specs if ex else [])),
        scratch_shapes=ex.scratch if ex else [],
        compiler_params=_cp(("arbitrary",)),
    )(dh2, dy, x2, ob, o_attn, u1, mod8, n2g, beta_a, beta_c, lng, lnb, w_out, e512, *(ex.srcs if ex else []))
    return outs[:7], list(outs[7:])


def _attn_bwd(qa, ka, v, do, lser, dr, fs, fe, bd, tq):
    S = qa.shape[1]
    nq = S // tq

    def body(fs_ref, fe_ref, bd_ref, ka_ref, v_ref, qa_ref, do_ref, lse_ref, dr_ref, dqa_hbm, dka_ref, dv_ref,
             accq_ref, acck_ref, accv_ref, out_sem):
        pr, j = pl.program_id(0), pl.program_id(1)
        sel_a = _lane((tq, LANES)) < DH
        vv = v_ref[...]
        zb = jnp.zeros_like(vv)
        vs = [jnp.where(sel_a, vv, zb), jnp.where(sel_a, zb, vv)]
        acck_ref[...] = jnp.zeros_like(acck_ref)
        accv_ref[...] = jnp.zeros_like(accv_ref)

        @pl.when(j == 0)
        def _():
            accq_ref[...] = jnp.zeros_like(accq_ref)

        def q_step(i, heads, masked=False):
            start = pl.multiple_of(i * tq, tq)
            dob = do_ref[pl.ds(start, tq), :]
            lse8 = lse_ref[0, i]
            dr8 = dr_ref[0, i]
            for hh in heads:
                qb = qa_ref[hh, pl.ds(start, tq), :]
                kb = ka_ref[hh]
                st = lax.dot_general(kb, qb, NT, preferred_element_type=F32)
                pt = jnp.exp(st - lse8[hh:hh + 1, :])
                if masked:
                    keep = (lax.broadcasted_iota(jnp.int32, (tq, tq), 0)
                            <= lax.broadcasted_iota(jnp.int32, (tq, tq), 1))
                    pt = jnp.where(keep, pt, 0.0)
                accv_ref[hh] += jnp.dot(pt.astype(BF16), dob, preferred_element_type=F32)
                dpt = lax.dot_general(vs[hh], dob, NT, preferred_element_type=F32)
                dst = (pt * (dpt - dr8[hh:hh + 1, :])).astype(BF16)
                acck_ref[hh] += jnp.dot(dst, qb, preferred_element_type=F32)
                accq_ref[hh, pl.ds(start, tq), :] += lax.dot_general(dst, kb, TN, preferred_element_type=F32)

        def needed(hh, i):
            top = fs_ref[2 * pr + hh, jnp.minimum(i, nq - 1)] + bd_ref[0, 0]
            return jnp.logical_and(i < nq, top - fe_ref[2 * pr + hh, j] >= SKIP)

        q_step(j, (0, 1), masked=True)
        _block_loops(j + 1, 1, needed, q_step)
        dka_ref[...] = acck_ref[...]
        dv_ref[...] = jnp.where(sel_a, accv_ref[0], accv_ref[1]).astype(BF16)

        @pl.when(j == nq - 1)
        def _():
            out = pltpu.make_async_copy(accq_ref, dqa_hbm.at[pl.ds(2 * pr, 2)], out_sem)
            out.start()
            out.wait()

    once = pl.Buffered(1)
    return pl.pallas_call(
        body, name="attn_bwd", grid=(NH // 2, nq),
        out_shape=(_sds((NH, S, LANES), F32), _sds((NH, S, LANES), F32), _sds((S, AW), BF16)),
        in_specs=[SMEM_SPEC, SMEM_SPEC, SMEM_SPEC,
                  pl.BlockSpec((2, tq, LANES), lambda p, j: (p, j, 0)),
                  pl.BlockSpec((tq, LANES), lambda p, j: (j, p)),
                  pl.BlockSpec((2, S, LANES), lambda p, j: (p, 0, 0), pipeline_mode=once),
                  pl.BlockSpec((S, LANES), lambda p, j: (0, p), pipeline_mode=once),
                  pl.BlockSpec((1, nq, 8, tq), lambda p, j: (p, 0, 0, 0)),
                  pl.BlockSpec((1, nq, 8, tq), lambda p, j: (p, 0, 0, 0))],
        out_specs=(pl.BlockSpec(memory_space=pl.ANY),
                   pl.BlockSpec((2, tq, LANES), lambda p, j: (p, j, 0)),
                   pl.BlockSpec((tq, LANES), lambda p, j: (j, p))),
        scratch_shapes=[pltpu.VMEM((2, S, LANES), F32), pltpu.VMEM((2, tq, LANES), F32),
                        pltpu.VMEM((2, tq, LANES), F32), pltpu.SemaphoreType.DMA],
        compiler_params=_cp(("arbitrary", "arbitrary")),
    )(fs, fe, bd, ka, v, qa, do, lser, dr)


def _bwd_conv(du1, u0, alin, agate, w32, tm, ex=None):
    S = du1.shape[0]
    nt = S // tm

    def body(du_ref, dun_ref, u0_ref, u0p_ref, alin_ref, agate_ref, w_ref,
             dalin_ref, dagate_ref, dw_ref, db_ref, extd_ref, extu_ref, du0_ref, shd_ref, shu_ref):
        i = pl.program_id(0)

        @pl.when(i == 0)
        def _():
            dw_ref[...] = jnp.zeros_like(dw_ref)
            db_ref[...] = jnp.zeros_like(db_ref)

        extd_ref[0:tm, :] = du_ref[...]
        extd_ref[tm:, :] = jnp.where(i == nt - 1, 0.0, dun_ref[...])
        extu_ref[0:HALO, :] = jnp.where(i == 0, 0.0, u0p_ref[...])
        extu_ref[HALO:, :] = u0_ref[...]
        _shifted_copies(extd_ref, shd_ref, tm)
        _shifted_copies(extu_ref, shu_ref, tm)
        db_ref[...] += _colsum8(du_ref[...])
        for r0 in range(0, tm, BWD_CHUNK):
            acc = jnp.zeros((BWD_CHUNK, CW), F32)
            for j, rows in _tap_windows(extd_ref, shd_ref, r0, lambda j: 30 - j, BWD_CHUNK, shared=True):
                acc = acc + w_ref[j:j + 1, :] * rows
            du0_ref[r0:r0 + BWD_CHUNK, :] = acc
            duc = du_ref[r0:r0 + BWD_CHUNK, :]
            for j, rows in _tap_windows(extu_ref, shu_ref, r0, lambda j: 2 + j, BWD_CHUNK, shared=True):
                dw_ref[j] += _colsum8(duc * rows)
        du0 = du0_ref[...]
        al = alin_ref[...].astype(F32)
        sg = _sigmoid(agate_ref[...].astype(F32))
        dalin_ref[...] = (du0 * sg).astype(BF16)
        dagate_ref[...] = (du0 * al * sg * (1.0 - sg)).astype(BF16)

    nxt = pl.BlockSpec((HALO, CW), lambda i: (jnp.minimum((i + 1) * (tm // HALO), S // HALO - 1), 0))
    outs = pl.pallas_call(
        _hosted(body, ex, 7, 4, 5, nt), name="bwd_conv", grid=(nt,),
        out_shape=(_sds((S, CW), BF16), _sds((S, CW), BF16), _sds((HALO, 8, CW), F32), _sds((8, CW), F32),
                   *(ex.out_shapes if ex else [])),
        in_specs=[_rows(tm, CW), nxt, _rows(tm, CW), _halo_prev(tm), _rows(tm, CW), _rows(tm, CW), _const((HALO, CW)),
                  *(ex.in_specs if ex else [])],
        out_specs=(_rows(tm, CW), _rows(tm, CW), _const((HALO, 8, CW)), _const((8, CW)), *(ex.out_specs if ex else [])),
        scratch_shapes=[pltpu.VMEM((tm + HALO, CW), F32), pltpu.VMEM((tm + HALO, CW), F32), pltpu.VMEM((tm, CW), F32),
                        pltpu.VMEM((SUBLANES - 1, tm + HALO, CW), F32), pltpu.VMEM((SUBLANES - 1, tm + HALO, CW), F32),
                        *(ex.scratch if ex else [])],
        compiler_params=_cp(("arbitrary",)),
    )(du1, du1, u0, u0, alin, agate, w32, *(ex.srcs if ex else []))
    return outs[:4], list(outs[4:])


def _bwd_qk(dqa, dka, qn, kn, rq, rk, fgb, qg512, kg512, e512, et512, tm):
    S = qn.shape[0]
    nt = S // tm

    def body(dqa_ref, dka_ref, qn_ref, kn_ref, rq_ref, rk_ref, fgb_ref, qg_ref, kg_ref, e_ref, et_ref,
             dq_ref, dk_ref, dfg_ref, accg_ref, accb_ref, carry_ref):
        @pl.when(pl.program_id(0) == 0)
        def _():
            carry_ref[...] = jnp.zeros_like(carry_ref)
            accg_ref[...] = jnp.zeros_like(accg_ref)
            accb_ref[...] = jnp.zeros_like(accb_ref)

        lane = _lane((tm, LANES))
        sel_a = lane < DH
        df = jnp.zeros((tm, LANES), F32)
        for h in range(NH):
            col = dqa_ref[h][:, 64:65] - dka_ref[h][:, 67:68]
            df = jnp.where(lane == h, col, df)
        tri = (lax.broadcasted_iota(jnp.int32, (tm, tm), 0) <= lax.broadcasted_iota(jnp.int32, (tm, tm), 1)
               ).astype(F32).astype(BF16)
        dlf = _dot3(tri, df) + carry_ref[0:1, :]
        carry_ref[...] = jnp.broadcast_to(dlf[0:1, :], carry_ref.shape)
        dfg = jnp.where(lane < NH, dlf * _sigmoid(-fgb_ref[...]), 0.0)
        dfg_ref[...] = dfg.astype(BF16)
        accb_ref[...] += _colsum8(dfg)

        def norm_bwd(src_ref, n_ref, r_ref, g_ref, scale, slot):
            pairs = []
            for p in range(NH // 2):
                b = pltpu.roll(src_ref[2 * p + 1], 64, 1)
                pairs.append(jnp.where(sel_a, src_ref[2 * p], b))
            dh = jnp.concatenate(pairs, axis=1) * scale
            tn = n_ref[...].astype(F32)
            accg_ref[slot] += _colsum8(dh * tn)
            dn = dh * g_ref[...]
            mean = _dot2(dn * tn, e_ref[...]) * (1.0 / DH)
            corr = _dot2(mean, et_ref[...])
            rf = _dot2(r_ref[...], et_ref[...])
            return (rf * (dn - tn * corr)).astype(BF16)

        dq_ref[...] = norm_bwd(dqa_ref, qn_ref, rq_ref, qg_ref, DH ** -0.5, 0)
        dk_ref[...] = norm_bwd(dka_ref, kn_ref, rk_ref, kg_ref, 1.0, 1)

    rev = lambda n: pl.BlockSpec((tm, n), lambda i: (nt - 1 - i, 0))
    hm = pl.BlockSpec((NH, tm, LANES), lambda i: (0, nt - 1 - i, 0))
    dq, dk, dfg, accg, accb = pl.pallas_call(
        body, name="bwd_qk", grid=(nt,),
        out_shape=(_sds((S, AW), BF16), _sds((S, AW), BF16), _sds((S, LANES), BF16), _sds((2, 8, AW), F32),
                   _sds((8, LANES), F32)),
        in_specs=[hm, hm, rev(AW), rev(AW), rev(LANES), rev(LANES), rev(LANES), _const((1, AW)), _const((1, AW)),
                  _const((AW, LANES)), _const((LANES, AW))],
        out_specs=(rev(AW), rev(AW), rev(LANES), _const((2, 8, AW)), _const((8, LANES))),
        scratch_shapes=[pltpu.VMEM((8, LANES), F32)], compiler_params=_cp(("arbitrary",)),
    )(dqa, dka, qn, kn, rq, rk, fgb, qg512, kg512, e512, et512)
    return dq, dk, dfg, accg, accb


def _bwd_in(dq, dk, dv, dalin, dagate, dfg, w_in_p, x, dx2, mod8, n1g, tm, ex=None):
    S = x.shape[0]

    def body(dq_ref, dk_ref, dv_ref, dal_ref, dag_ref, dfg_ref, w_ref, x_ref, dx2_ref, mod_ref, n1g_ref,
             dx_ref, acc_ref):
        @pl.when(pl.program_id(0) == 0)
        def _():
            acc_ref[...] = jnp.zeros_like(acc_ref)

        def part(ref, a, b):
            return lax.dot_general(ref[...], w_ref[:, a:b], NT, preferred_element_type=F32)

        dh = (part(dq_ref, 0, 512) + part(dk_ref, 512, 1024) + part(dv_ref, 1024, 1536) + part(dal_ref, 1536, 2048)
              + part(dag_ref, 2048, 2560) + part(dfg_ref, 2560, NP))
        xv = x_ref[...]
        r1 = lax.rsqrt(jnp.mean(xv * xv, axis=-1, keepdims=True) + EPS)
        xn = xv * r1
        gain = n1g_ref[...] * (1.0 + mod_ref[1:2, :])
        t = dh * xn
        acc_ref[0] += _colsum8(dh)
        acc_ref[1] += _colsum8(t * n1g_ref[...])
        acc_ref[2] += _colsum8(t * (1.0 + mod_ref[1:2, :]))
        dxn = dh * gain
        dx_ref[...] = dx2_ref[...] + r1 * (dxn - xn * jnp.mean(dxn * xn, axis=-1, keepdims=True))

    outs = pl.pallas_call(
        _hosted(body, ex, 11, 2, 0, S // tm), name="bwd_in", grid=(S // tm,),
        out_shape=(_sds((S, D), F32), _sds((3, 8, D), F32), *(ex.out_shapes if ex else [])),
        in_specs=[_rows(tm, AW), _rows(tm, AW), _rows(tm, AW), _rows(tm, CW), _rows(tm, CW), _rows(tm, LANES),
                  _const((D, NP)), _rows(tm, D), _rows(tm, D), _const((8, D)), _const((1, D)),
                  *(ex.in_specs if ex else [])],
        out_specs=(_rows(tm, D), _const((3, 8, D)), *(ex.out_specs if ex else [])),
        scratch_shapes=ex.scratch if ex else [],
        compiler_params=_cp(("arbitrary",)),
    )(dq, dk, dv, dalin, dagate, dfg, w_in_p, x, dx2, mod8, n1g, *(ex.srcs if ex else []))
    return outs[:2], list(outs[2:])


def _adam(w, g, m, v):
    m_new = B1 * m + (1.0 - B1) * g
    v_new = B2 * v + (1.0 - B2) * (g * g)
    m_hat = m_new / (1.0 - B1 ** STEP)
    v_hat = v_new / (1.0 - B2 ** STEP)
    delta = -LR * (m_hat / (jnp.sqrt(v_hat) + AEPS) + WD * w)
    return delta, m_new, v_new


def _pair_adamw(slots, w, m, v, name, tr=256):
    ns, R, C = slots.shape
    tr = tr if R % tr == 0 else R
    nt = R // tr

    def body(s_ref, w_ref, m_ref, v_ref, g_ref, d_ref, mo_ref, vo_ref, mine_ref, theirs_ref, send_sems, recv_sems):
        i = pl.program_id(0)
        part = s_ref[0].astype(F32)
        for k in range(1, ns):
            part = part + s_ref[k].astype(F32)
        mine_ref[i] = part
        swap = pltpu.make_async_remote_copy(
            src_ref=mine_ref.at[i], dst_ref=theirs_ref.at[i], send_sem=send_sems.at[i], recv_sem=recv_sems.at[i],
            device_id=(lax.axis_index("x"), lax.axis_index("y"), 1 - lax.axis_index("c")),
            device_id_type=pl.DeviceIdType.MESH)
        swap.start()
        swap.wait()
        g = part + theirs_ref[i]
        g_ref[...] = g
        d_ref[...], mo_ref[...], vo_ref[...] = _adam(w_ref[...], g, m_ref[...], v_ref[...])

    blk = pl.BlockSpec((tr, C), lambda i: (i, 0))
    return pl.pallas_call(
        body, name=name, grid=(nt,), out_shape=tuple(_sds((R, C), F32) for _ in range(4)),
        in_specs=[pl.BlockSpec((ns, tr, C), lambda i: (0, i, 0)), blk, blk, blk], out_specs=(blk, blk, blk, blk),
        scratch_shapes=[pltpu.VMEM((nt, tr, C), F32), pltpu.VMEM((nt, tr, C), F32),
                        pltpu.SemaphoreType.DMA((nt,)), pltpu.SemaphoreType.DMA((nt,))],
        compiler_params=_cp(("arbitrary",)),
    )(slots, w, m, v)


def _ada_adamw(sct, dmod, w, m, v):
    R, C = w.shape
    tr, bc = 256, 512

    def body(sct_ref, dm_ref, w_ref, m_ref, v_ref, g_ref, d_ref, mo_ref, vo_ref):
        g = sct_ref[:, 0:1] * dm_ref[0:1, :]
        for b in range(1, N_DEV):
            g = g + sct_ref[:, b:b + 1] * dm_ref[b:b + 1, :]
        g_ref[...] = g
        d_ref[...], mo_ref[...], vo_ref[...] = _adam(w_ref[...], g, m_ref[...], v_ref[...])

    blk = pl.BlockSpec((tr, bc), lambda i, j: (i, j))
    return pl.pallas_call(
        body, name="ada_adamw", grid=(R // tr, C // bc), out_shape=tuple(_sds((R, C), F32) for _ in range(4)),
        in_specs=[pl.BlockSpec((tr, N_DEV), lambda i, j: (i, 0)), pl.BlockSpec((N_DEV, bc), lambda i, j: (0, j)),
                  blk, blk, blk],
        out_specs=(blk, blk, blk, blk), compiler_params=_cp(("parallel", "parallel")),
    )(sct, dmod, w, m, v)


PACK = {"dmod": 0, "norm1_g": 6144, "norm2_g": 7168, "q_norm_g": 8192, "k_norm_g": 8704, "b_f": 9216, "conv_b": 9344,
        "conv_ln_g": 9856, "conv_ln_b": 10368, "beta_attn": 10880, "beta_conv": 11392}
SMALL_NAMES = ["b_ada", "norm1_g", "q_norm_g", "k_norm_g", "b_f", "conv_b", "conv_ln_g", "conv_ln_b", "beta_attn",
               "beta_conv", "norm2_g"]


def _pack_small(acc1, acc_d, acc_h, dg2, accg, accb, dcb):
    def body(a1_ref, ad_ref, ah_ref, dg2_ref, ag_ref, ab_ref, cb_ref, o_ref):
        def put(off, rows):
            o_ref[:, off:off + rows.shape[1]] = jnp.sum(rows, axis=0, keepdims=True)

        for t, rows in enumerate((a1_ref[0], a1_ref[1], ad_ref[3], ad_ref[0], ad_ref[1], dg2_ref[...])):
            put(PACK["dmod"] + t * D, rows)
        put(PACK["norm1_g"], a1_ref[2])
        put(PACK["norm2_g"], ad_ref[2])
        put(PACK["q_norm_g"], ag_ref[0])
        put(PACK["k_norm_g"], ag_ref[1])
        put(PACK["b_f"], ab_ref[...])
        put(PACK["conv_b"], cb_ref[...])
        put(PACK["conv_ln_g"], ah_ref[2])
        put(PACK["conv_ln_b"], ah_ref[3])
        put(PACK["beta_attn"], ah_ref[0])
        put(PACK["beta_conv"], ah_ref[1])

    vm = pl.BlockSpec(memory_space=pltpu.VMEM)
    return pl.pallas_call(body, name="pack_small", out_shape=_sds((1, SMALL_IN), F32), in_specs=[vm] * 7, out_specs=vm,
                          )(acc1, acc_d, acc_h, dg2, accg, accb, dcb)


def _small_adamw(slots, fold, ws, ms, vs):
    n = len(ws)
    widths = [w.shape[1] for w in ws]

    def body(s_ref, f_ref, *refs):
        w_refs, m_refs, v_refs = refs[:n], refs[n:2 * n], refs[2 * n:3 * n]
        outs = refs[3 * n:]
        tot = s_ref[0:1, :]
        for k in range(1, N_DEV):
            tot = tot + s_ref[k:k + 1, :]

        def grad(name, width):
            if name == "b_ada":
                return tot[:, 0:6 * D]
            seg = tot[:, PACK[name]:PACK[name] + max(width, LANES)]
            if name in ("q_norm_g", "k_norm_g"):
                seg = jnp.dot(jnp.broadcast_to(tot[:, PACK[name]:PACK[name] + AW], (8, AW)), f_ref[...], precision=HI,
                              preferred_element_type=F32)[0:1, :]
            return seg[:, 0:width]

        for t, (name, width) in enumerate(zip(SMALL_NAMES, widths)):
            g = grad(name, width)
            d, m_new, v_new = _adam(w_refs[t][...], g, m_refs[t][...], v_refs[t][...])
            outs[t][...] = g
            outs[n + t][...] = d
            outs[2 * n + t][...] = m_new
            outs[3 * n + t][...] = v_new

    vm = pl.BlockSpec(memory_space=pltpu.VMEM)
    outs = pl.pallas_call(
        body, name="adamw_small", out_shape=tuple(_sds((1, w), F32) for _ in range(4) for w in widths),
        in_specs=[vm] * (2 + 3 * n), out_specs=tuple(vm for _ in range(4 * n)),
    )(slots, fold, *ws, *ms, *vs)
    return [list(outs[k * n:(k + 1) * n]) for k in range(4)]


def _perm_in(w):
    pad = jnp.zeros((w.shape[0], NP - 2568), w.dtype)
    return jnp.concatenate([w[:, :1536], w[:, 1544:2568], w[:, 1536:1544], pad], axis=1)


def _pad_lanes(vec, n=LANES):
    return jnp.pad(vec, ((0, 0), (0, n - vec.shape[1])))


def kernel(x, c, w_ada, b_ada, norm1_g, w_in, q_norm_g, k_norm_g, b_f, conv_w, conv_b, conv_ln_g, conv_ln_b, beta_attn, beta_conv, w_out, norm2_g, w_ff1, w_ff2, loss_target, m_w_ada, m_b_ada, m_norm1_g, m_w_in, m_q_norm_g, m_k_norm_g, m_b_f, m_conv_w, m_conv_b, m_conv_ln_g, m_conv_ln_b, m_beta_attn, m_beta_conv, m_w_out, m_norm2_g, m_w_ff1, m_w_ff2, v_w_ada, v_b_ada, v_norm1_g, v_w_in, v_q_norm_g, v_k_norm_g, v_b_f, v_conv_w, v_conv_b, v_conv_ln_g, v_conv_ln_b, v_beta_attn, v_beta_conv, v_w_out, v_norm2_g, v_w_ff1, v_w_ff2):
    S = x.shape[1]
    tm = min(256, S)
    tw = min(512, S)
    tq = min(512, S // 2)
    xs, tgt = x[0], loss_target[0]
    chip = 2 * lax.axis_index("x") + lax.axis_index("y")
    e512, et512 = _head_sum_mats()

    conv_w32 = jnp.pad(conv_w[0], ((0, 1), (0, 0)))
    c_all, g_in = _exchange([(c, "bcast8"), (w_in[0].astype(BF16), "chip4")], "gather_in")
    later_weights = _Exchange([(w_out[0].astype(BF16), "chip4"), (conv_w32, "chip4")])
    c_all = c_all.reshape(N_DEV, D)
    w_in_p = _perm_in(jnp.transpose(g_in, (1, 0, 2)).reshape(D, 2568))

    b_shard = lax.dynamic_slice(b_ada, (0, chip * 1536), (1, 1536))
    mod_rows, sc_all = _mod_shard(c_all, w_ada[0], b_shard)
    (mod_slots,) = _exchange([(mod_rows.reshape(N_DEV, 1, 1536), "all8")], "scatter_mod")
    mod = mod_slots.reshape(4, 2, 1536)[:, 0, :].reshape(6, D)
    mod8 = jnp.pad(mod, ((0, 2), (0, 0)))

    qg512 = jnp.tile(q_norm_g, (1, NH))
    kg512 = jnp.tile(k_norm_g, (1, NH))
    bf128 = _pad_lanes(b_f)

    (h1, qh, kh, vb, qn, kn, rq, rk, fgb, alin, agate, u0), (g_out, g_cw) = _fwd_in(
        xs, mod8, norm1_g, w_in_p, e512, et512, qg512, kg512, bf128, tw, ex=later_weights)
    w_out_f = g_out.reshape(D, D)
    cw32 = jnp.transpose(g_cw, (1, 0, 2)).reshape(HALO, CW)
    bd = _logit_bound(q_norm_g, k_norm_g)
    (qa, ka, va, fcum), (w1,) = _fwd_decay(fgb, qh, kh, vb, _shift(bd), tm,
                                           ex=_Exchange([(w_ff1[0].astype(BF16), "chip4")]))
    fs, fe = _skip_tables(fcum, tq)
    o_attn, lser = _attn_fwd(qa, ka, va, fs, fe, bd, tq)
    (u1, mc), (w2,) = _fwd_conv(u0, cw32, conv_b, conv_ln_g, conv_ln_b, beta_conv, tm,
                                ex=_Exchange([(w_ff2[0].astype(BF16), "chip4")]))
    merged, ob, x2, h2 = _fwd_out(o_attn, mc, xs, mod8, norm2_g, beta_attn, w_out_f, tw)
    r, dy, loss8, dg2 = _fwd_ffn(h2, w1, w2, x2, tgt, mod8, tw)

    df2, df1, dh2 = _bwd_ffn(dy, mod8, r, w1, w2, tw)
    gw_ff2 = _wgrad(r, df2, "wgrad_ff2", square_a=True)
    gw_ff1 = _wgrad(h2, df1, "wgrad_ff1", col_pieces=True)
    ff_grads = _Exchange([(gw_ff1, "chip4p"), (gw_ff2.reshape(4, D, D), "chip4p")])
    (dx2, do, doa, du1, acc_d, acc_h, dr), (p_ff1, p_ff2) = _bwd_mid(
        dh2, dy, x2, ob, o_attn, u1, mod8, norm2_g, beta_attn, beta_conv, conv_ln_g, conv_ln_b, w_out_f, e512, tm, tq,
        ex=ff_grads)
    gw_out = _wgrad(merged, do, "wgrad_out")
    dqa, dka, dv = _attn_bwd(qa, ka, vb, doa, lser, dr, fs, fe, bd, tq)
    out_grads = _Exchange([(gw_out.reshape(4, 256, D), "chip4p")])
    (dalin, dagate, dcw, dcb), (p_out,) = _bwd_conv(du1, u0, alin, agate, cw32, tm, ex=out_grads)
    dq, dk, dfg, accg, accb = _bwd_qk(dqa, dka, qn, kn, rq, rk, fgb, qg512, kg512, e512, et512, tm)
    gw_in_p = _wgrad_in(h1, [dq, dk, dv, dalin, dagate, dfg])
    gw_in = jnp.concatenate([gw_in_p[:, :1536], gw_in_p[:, 2560:2568], gw_in_p[:, 1536:2560]], axis=1)
    s8 = lambda a: jnp.sum(a, axis=-2)
    gcw = s8(dcw)
    in_grads = _Exchange([(jnp.transpose(gw_in.reshape(D, 4, 642), (1, 0, 2)), "chip4p"),
                          (jnp.transpose(gcw.reshape(HALO, 4, LANES), (1, 0, 2)), "chip4p")])
    (grad_x, acc1), (p_in, p_cw) = _bwd_in(dq, dk, dv, dalin, dagate, dfg, w_in_p, xs, dx2, mod8, norm1_g, tw,
                                           ex=in_grads)

    small = _pack_small(acc1, acc_d, acc_h, dg2, accg, accb, dcb)
    (small_s,) = _exchange([(small, "bcast8")], "gather_small")
    small_s = small_s.reshape(N_DEV, SMALL_IN)

    g_in_, d_in, nm_in, nv_in = _pair_adamw(p_in, w_in[0], m_w_in[0], v_w_in[0], "adamw_in")
    g_out_, d_out, nm_out, nv_out = _pair_adamw(p_out, w_out[0], m_w_out[0], v_w_out[0], "adamw_out")
    g_f1, d_f1, nm_f1, nv_f1 = _pair_adamw(p_ff1, w_ff1[0], m_w_ff1[0], v_w_ff1[0], "adamw_ff1")
    g_f2, d_f2, nm_f2, nv_f2 = _pair_adamw(p_ff2, w_ff2[0], m_w_ff2[0], v_w_ff2[0], "adamw_ff2")
    pad_row = lambda a, fill: jnp.pad(a[0], ((0, 1), (0, 0)), constant_values=fill)
    g_cw_, d_cw, nm_cw, nv_cw = (a[:KC] for a in _pair_adamw(
        p_cw, pad_row(conv_w, 0.0), pad_row(m_conv_w, 0.0), pad_row(v_conv_w, 1.0), "adamw_conv_w"))
    dmod_shard = lax.dynamic_slice(small_s[:, :6 * D], (0, chip * 1536), (N_DEV, 1536))
    g_ada, d_ada, nm_ada, nv_ada = _ada_adamw(sc_all.T, dmod_shard, w_ada[0], m_w_ada[0], v_w_ada[0])

    fold = np.zeros((AW, LANES), np.float32)
    fold[np.arange(AW), np.arange(AW) % DH] = 1.0
    smalls = [b_ada, norm1_g, q_norm_g, k_norm_g, b_f, conv_b, conv_ln_g, conv_ln_b, beta_attn, beta_conv, norm2_g]
    m_smalls = [m_b_ada, m_norm1_g, m_q_norm_g, m_k_norm_g, m_b_f, m_conv_b, m_conv_ln_g, m_conv_ln_b, m_beta_attn,
                m_beta_conv, m_norm2_g]
    v_smalls = [v_b_ada, v_norm1_g, v_q_norm_g, v_k_norm_g, v_b_f, v_conv_b, v_conv_ln_g, v_conv_ln_b, v_beta_attn,
                v_beta_conv, v_norm2_g]
    gs, ds, ms, vs = _small_adamw(small_s, jnp.asarray(fold), smalls, m_smalls, v_smalls)

    loss = lax.psum(loss8[0, 0], ("x", "y", "c"))
    big = {"w_ada": (g_ada, d_ada, nm_ada, nv_ada), "w_in": (g_in_, d_in, nm_in, nv_in),
           "conv_w": (g_cw_, d_cw, nm_cw, nv_cw), "w_out": (g_out_, d_out, nm_out, nv_out),
           "w_ff1": (g_f1, d_f1, nm_f1, nv_f1), "w_ff2": (g_f2, d_f2, nm_f2, nv_f2)}
    order =["w_ada", "b_ada", "norm1_g", "w_in", "q_norm_g", "k_norm_g", "b_f", "conv_w", "conv_b", "conv_ln_g",
             "conv_ln_b", "beta_attn", "beta_conv", "w_out", "norm2_g", "w_ff1", "w_ff2"]

    def leaf(name, which):
        if name in big:
            return big[name][which][None]
        return (gs, ds, ms, vs)[which][SMALL_NAMES.index(name)]

    return (loss, grad_x[None], *[leaf(n, 0) for n in order], *[leaf(n, 1) for n in order],
            *[leaf(n, 2) for n in order], *[leaf(n, 3) for n in order])
```

```python
import functools

import numpy as np
import jax
import jax.numpy as jnp
from jax import lax
from jax.experimental import pallas as pl
from jax.experimental.pallas import tpu as pltpu

F32, BF16 = jnp.float32, jnp.bfloat16
HI = lax.Precision.HIGHEST
D = 1024
AW = 512
CW = 512
NH = 8
DH = 64
KC = 31
DFF = 4096
NP = 2688
EPS = 1e-6
NEG = -1e30
LANES = 128
VMEM_LIMIT = 56 * 2**20
NT = (((1,), (1,)), ((), ()))
TN = (((0,), (0,)), ((), ()))
LR, B1, B2, AEPS, WD, STEP = 0.001, 0.9, 0.999, 1e-08, 0.01, 10
N_DEV = 8
SMALL_IN = 11904


def _cp(sem=None, vmem=VMEM_LIMIT):
    kw = dict(vmem_limit_bytes=vmem)
    if sem is not None:
        kw["dimension_semantics"] = sem
    return pltpu.CompilerParams(**kw)


def _rows(tm, n):
    return pl.BlockSpec((tm, n), lambda i: (i, 0))


def _const(shape):
    nd = len(shape)
    return pl.BlockSpec(shape, lambda *_: (0,) * nd)


def _sds(shape, dt):
    return jax.ShapeDtypeStruct(shape, dt)


def _lane(shape):
    return lax.broadcasted_iota(jnp.int32, shape, len(shape) - 1)


def _sigmoid(x):
    return 1.0 / (1.0 + jnp.exp(-x))


class _Exchange:
    MASKS = {"chip4": (2, 4, 6), "chip4p": (2, 4, 6), "all8": (1, 2, 3, 4, 5, 6, 7), "bcast8": (1, 2, 3, 4, 5, 6, 7)}

    def __init__(self, items):
        self.srcs = [s for s, _ in items]
        self.kinds = [k for _, k in items]
        self.n = len(items)
        self.out_shapes = []
        for s, k in items:
            shape = {"all8": (N_DEV,) + s.shape[1:], "bcast8": (N_DEV,) + s.shape, "chip4": (4,) + s.shape,
                     "chip4p": (4,) + s.shape[1:]}[k]
            self.out_shapes.append(_sds(shape, s.dtype))
        self.sem_index = {}
        for t, k in enumerate(self.kinds):
            for m in self.MASKS[k]:
                self.sem_index[(t, m)] = len(self.sem_index)
        n_sem = len(self.sem_index)
        self.scratch = [pltpu.SemaphoreType.DMA((n_sem,)), pltpu.SemaphoreType.DMA((n_sem,)),
                        pltpu.SemaphoreType.DMA((self.n,))]
        self.in_specs = [pl.BlockSpec(memory_space=pl.ANY)] * self.n
        self.out_specs = [pl.BlockSpec(memory_space=pl.ANY)] * self.n

    def copies(self, src_refs, dst_refs, send_sems, recv_sems, local_sems):
        x, y, c = lax.axis_index("x"), lax.axis_index("y"), lax.axis_index("c")
        my_id = 4 * x + 2 * y + c
        my_chip = 2 * x + y

        def piece(t, dev_id, chip):
            k = self.kinds[t]
            return src_refs[t].at[dev_id] if k == "all8" else src_refs[t].at[chip] if k == "chip4p" else src_refs[t]

        out = []
        for t in range(self.n):
            slot = dst_refs[t].at[my_chip if self.kinds[t] in ("chip4", "chip4p") else my_id]
            out.append(pltpu.make_async_copy(piece(t, my_id, my_chip), slot, local_sems.at[t]))
            for m in self.MASKS[self.kinds[t]]:
                px = 1 - x if m & 4 else x
                py = 1 - y if m & 2 else y
                pc = 1 - c if m & 1 else c
                s = self.sem_index[(t, m)]
                out.append(pltpu.make_async_remote_copy(
                    src_ref=piece(t, 4 * px + 2 * py + pc, 2 * px + py), dst_ref=slot,
                    send_sem=send_sems.at[s], recv_sem=recv_sems.at[s],
                    device_id=(px, py, pc), device_id_type=pl.DeviceIdType.MESH))
        return out


def _hosted(body, ex, n_in, n_out, n_scr, n_steps):
    if ex is None:
        return body

    def wrapped(*refs):
        ins, xin = refs[:n_in], refs[n_in:n_in + ex.n]
        o0 = n_in + ex.n
        outs, xout = refs[o0:o0 + n_out], refs[o0 + n_out:o0 + n_out + ex.n]
        s0 = o0 + n_out + ex.n
        scr, sems = refs[s0:s0 + n_scr], refs[s0 + n_scr:]

        @pl.when(pl.program_id(0) == 0)
        def _():
            for cp in ex.copies(xin, xout, *sems):
                cp.start()

        body(*ins, *outs, *scr)

        @pl.when(pl.program_id(0) == n_steps - 1)
        def _():
            for cp in ex.copies(xin, xout, *sems):
                cp.wait()

    return wrapped


def _exchange(items, name):
    ex = _Exchange(items)
    n = ex.n

    def body(*refs):
        copies = ex.copies(refs[:n], refs[n:2 * n], *refs[2 * n:])
        for cp in copies:
            cp.start()
        for cp in copies:
            cp.wait()

    outs = pl.pallas_call(
        body, name=name, out_shape=tuple(ex.out_shapes), in_specs=ex.in_specs, out_specs=tuple(ex.out_specs),
        scratch_shapes=ex.scratch,
    )(*ex.srcs)
    return list(outs)


def _mod_shard(c_all, w_ada, b_shard):
    n = w_ada.shape[1]

    def body(c_ref, w_ref, b_ref, o_ref, sc_ref):
        cv = c_ref[...]
        sc = cv * _sigmoid(cv)
        sc_ref[...] = sc
        o_ref[...] = jnp.dot(sc, w_ref[...], precision=HI, preferred_element_type=F32) + b_ref[...]

    bn = 512
    return pl.pallas_call(
        body, name="mod_shard", out_shape=(_sds((N_DEV, n), F32), _sds((N_DEV, D), F32)), grid=(n // bn,),
        in_specs=[_const((N_DEV, D)), pl.BlockSpec((D, bn), lambda j: (0, j)), pl.BlockSpec((1, bn), lambda j: (0, j))],
        out_specs=(pl.BlockSpec((N_DEV, bn), lambda j: (0, j)), _const((N_DEV, D))),
        compiler_params=_cp(("arbitrary",)),
    )(c_all, w_ada, b_shard)


def _head_sum_mats():
    e = np.zeros((AW, LANES), np.float32)
    for h in range(NH):
        e[h * DH:(h + 1) * DH, h] = 1.0
    return jnp.asarray(e, BF16), jnp.asarray(e.T.copy(), BF16)


def _dot2(x, w):
    hi = x.astype(BF16)
    lo = (x - hi.astype(F32)).astype(BF16)
    return jnp.dot(hi, w, preferred_element_type=F32) + jnp.dot(lo, w, preferred_element_type=F32)


def _fwd_in(x, mod8, n1g, w_in_p, e512, et512, qg512, kg512, bf128, tm, ex=None):
    S = x.shape[0]

    def body(x_ref, mod_ref, n1g_ref, w_ref, e_ref, et_ref, qg_ref, kg_ref, bf_ref,
             h1_ref, qh_ref, kh_ref, v_ref, qn_ref, kn_ref, rq_ref, rk_ref, fgb_ref, alin_ref, agate_ref, u0_ref):
        xv = x_ref[...]
        r1 = lax.rsqrt(jnp.mean(xv * xv, axis=-1, keepdims=True) + EPS)
        h = (xv * r1) * (n1g_ref[...] * (1.0 + mod_ref[1:2, :])) + mod_ref[0:1, :]
        hb = h.astype(BF16)
        h1_ref[...] = hb

        def seg(a, b):
            return jnp.dot(hb, w_ref[:, a:b], preferred_element_type=F32)

        def headnorm(t, g_ref, scale, n_ref, r_ref, o_ref):
            ss = _dot2(t * t, e_ref[...])
            r = lax.rsqrt(ss * (1.0 / DH) + EPS)
            tn = t * _dot2(r, et_ref[...])
            n_ref[...] = tn.astype(BF16)
            r_ref[...] = r
            o_ref[...] = (tn * (g_ref[...] * scale)).astype(BF16)

        headnorm(seg(0, 512), qg_ref, DH ** -0.5, qn_ref, rq_ref, qh_ref)
        headnorm(seg(512, 1024), kg_ref, 1.0, kn_ref, rk_ref, kh_ref)
        v_ref[...] = seg(1024, 1536).astype(BF16)
        alin = seg(1536, 2048)
        agate = seg(2048, 2560)
        alin_ref[...] = alin.astype(BF16)
        agate_ref[...] = agate.astype(BF16)
        u0_ref[...] = alin * _sigmoid(agate)
        fgb_ref[...] = seg(2560, NP) + bf_ref[...]

    bf = lambda: _sds((S, AW), BF16)
    xs = ex.srcs if ex else []
    outs = pl.pallas_call(
        _hosted(body, ex, 9, 12, 0, S // tm), name="fwd_in", grid=(S // tm,),
        out_shape=(_sds((S, D), BF16), bf(), bf(), bf(), bf(), bf(), _sds((S, LANES), F32), _sds((S, LANES), F32),
                   _sds((S, LANES), F32), bf(), bf(), _sds((S, CW), F32), *(ex.out_shapes if ex else [])),
        in_specs=[_rows(tm, D), _const((8, D)), _const((1, D)), _const((D, NP)), _const((AW, LANES)), _const((LANES, AW)),
                  _const((1, AW)), _const((1, AW)), _const((1, LANES)), *(ex.in_specs if ex else [])],
        out_specs=(_rows(tm, D), _rows(tm, AW), _rows(tm, AW), _rows(tm, AW), _rows(tm, AW), _rows(tm, AW),
                   _rows(tm, LANES), _rows(tm, LANES), _rows(tm, LANES), _rows(tm, AW), _rows(tm, AW), _rows(tm, CW),
                   *(ex.out_specs if ex else [])),
        scratch_shapes=ex.scratch if ex else [],
        compiler_params=_cp(("arbitrary",)),
    )(x, mod8, n1g, w_in_p, e512, et512, qg512, kg512, bf128, *xs)
    return outs[:12], list(outs[12:])


def _split3(f):
    f1 = f.astype(BF16).astype(F32)
    f2 = (f - f1).astype(BF16).astype(F32)
    return f1, f2, f - f1 - f2


def _dot3(w, x):
    return sum(jnp.dot(w, piece.astype(BF16), preferred_element_type=F32) for piece in _split3(x))


def _fwd_decay(fgb, qh, kh, vb, shift, tm, ex=None):
    S = fgb.shape[0]

    def body(shift_ref, fgb_ref, qh_ref, kh_ref, vb_ref, qa_ref, ka_ref, va_ref, f_ref, carry_ref):
        @pl.when(pl.program_id(0) == 0)
        def _():
            carry_ref[...] = jnp.zeros_like(carry_ref)

        fb = fgb_ref[...]
        lf = jnp.minimum(fb, 0.0) - jnp.log1p(jnp.exp(-jnp.abs(fb)))
        tri = (lax.broadcasted_iota(jnp.int32, (tm, tm), 0) >= lax.broadcasted_iota(jnp.int32, (tm, tm), 1)
               ).astype(F32).astype(BF16)
        cs = _dot3(tri, lf) + carry_ref[0:1, :]
        f_ref[...] = cs
        carry_ref[...] = jnp.broadcast_to(cs[tm - 1:tm, :], carry_ref.shape)
        lane = _lane((tm, LANES))
        s1, s2, s3 = _split3(jnp.zeros((tm, LANES), F32) - shift_ref[0, 0])
        tail_q = jnp.where((lane >= 67) & (lane < 70), 1.0,
                           jnp.where(lane == 70, s1, jnp.where(lane == 71, s2, jnp.where(lane == 72, s3, 0.0))))
        tail_k = jnp.where(((lane >= 64) & (lane < 67)) | ((lane >= 70) & (lane < 73)), 1.0, 0.0)
        tail_v = jnp.where(lane == DH, 1.0, 0.0)
        for p in range(NH // 2):
            qp = qh_ref[:, p * LANES:(p + 1) * LANES].astype(F32)
            kp = kh_ref[:, p * LANES:(p + 1) * LANES].astype(F32)
            vp = vb_ref[:, p * LANES:(p + 1) * LANES].astype(F32)
            for hh in range(2):
                h = 2 * p + hh
                f1, f2, f3 = _split3(cs[:, h:h + 1])
                qb = qp if hh == 0 else pltpu.roll(qp, 64, 1)
                kb = kp if hh == 0 else pltpu.roll(kp, 64, 1)
                vh = vp if hh == 0 else pltpu.roll(vp, 64, 1)
                augq = jnp.where(lane == 64, f1, jnp.where(lane == 65, f2, jnp.where(lane == 66, f3, tail_q)))
                augk = jnp.where(lane == 67, -f1, jnp.where(lane == 68, -f2, jnp.where(lane == 69, -f3, tail_k)))
                qa_ref[h] = jnp.where(lane < DH, qb, augq).astype(BF16)
                ka_ref[h] = jnp.where(lane < DH, kb, augk).astype(BF16)
                va_ref[h] = jnp.where(lane < DH, vh, tail_v).astype(BF16)

    hm = pl.BlockSpec((NH, tm, LANES), lambda i: (0, i, 0))
    hms = _sds((NH, S, LANES), BF16)
    outs = pl.pallas_call(
        _hosted(body, ex, 5, 4, 1, S // tm), name="fwd_decay", grid=(S // tm,),
        out_shape=(hms, hms, hms, _sds((S, LANES), F32), *(ex.out_shapes if ex else [])),
        in_specs=[SMEM_SPEC, _rows(tm, LANES), _rows(tm, AW), _rows(tm, AW), _rows(tm, AW),
                  *(ex.in_specs if ex else [])],
        out_specs=(hm, hm, hm, _rows(tm, LANES), *(ex.out_specs if ex else [])),
        scratch_shapes=[pltpu.VMEM((8, LANES), F32), *(ex.scratch if ex else [])],
        compiler_params=_cp(("arbitrary",)),
    )(shift, fgb, qh, kh, vb, *(ex.srcs if ex else []))
    return outs[:4], list(outs[4:])


SKIP = -106.0
SKIP_P = -88.0


def _block_loops(first, step, needed, run):
    def both(j):
        return jnp.logical_and(needed(0, j), needed(1, j))

    def walk(heads):
        def go(j):
            run(j, heads)
            return j + step
        return go

    j = lax.while_loop(both, walk((0, 1)), first)
    lax.while_loop(functools.partial(needed, 0), walk((0,)), j)
    lax.while_loop(functools.partial(needed, 1), walk((1,)), j)


def _logit_bound(qg, kg):
    return (2.0 * 1.03 * DH ** 0.5 * jnp.max(jnp.abs(qg)) * jnp.max(jnp.abs(kg))).reshape(1, 1)


def _skip_tables(f, tq):
    return f[0::tq, :NH].T, f[tq - 1::tq, :NH].T


SMEM_SPEC = pl.BlockSpec(memory_space=pltpu.SMEM)


def _causal_rect(rows, cols, col0):
    return (lax.broadcasted_iota(jnp.int32, (rows, cols), 0)
            >= lax.broadcasted_iota(jnp.int32, (rows, cols), 1) + col0)


SHIFT_MAX = 60.0


def _shift(bd):
    return jnp.where(bd <= SHIFT_MAX, 0.5 * bd, 0.0)


def _attn_fwd(qa, ka, va, fs, fe, bd, tq):
    S = qa.shape[1]
    nq = S // tq

    def body(fs_ref, fe_ref, bd_ref, qa_ref, ka_ref, va_ref, o_ref, lser_ref, m_ref, acc_ref):
        pr, i = pl.program_id(0), pl.program_id(1)
        sel_a = _lane((tq, LANES)) < DH
        acc_ref[...] = jnp.zeros_like(acc_ref)

        def logits(j, hh, masked):
            start = pl.multiple_of(j * tq, tq)
            s = lax.dot_general(qa_ref[hh], ka_ref[hh, pl.ds(start, tq), :], NT, preferred_element_type=F32)
            if masked:
                s = jnp.where(_causal_rect(tq, tq, 0), s, NEG)
            return s, va_ref[hh, pl.ds(start, tq), :]

        def shifted_step(j, heads, masked=False):
            for hh in heads:
                s, vb = logits(j, hh, masked)
                acc_ref[hh] += jnp.dot(jnp.exp(s).astype(BF16), vb, preferred_element_type=F32)

        def online_step(j, heads, masked=False):
            for hh in heads:
                s, vb = logits(j, hh, masked)
                m_prev = m_ref[hh]
                m_new = jnp.maximum(m_prev, jnp.max(s, axis=1, keepdims=True))
                p = jnp.exp(s - jnp.tile(m_new, (1, tq // LANES)))
                m_ref[hh] = m_new
                acc_ref[hh] = jnp.exp(m_prev - m_new) * acc_ref[hh] + jnp.dot(p.astype(BF16), vb,
                                                                              preferred_element_type=F32)

        def needed(slack, hh, j):
            top = fs_ref[2 * pr + hh, i] + slack
            return jnp.logical_and(j >= 0, top - fe_ref[2 * pr + hh, jnp.maximum(j, 0)] >= SKIP)

        @pl.when(bd_ref[0, 0] <= SHIFT_MAX)
        def _():
            m_ref[...] = jnp.zeros_like(m_ref)
            shifted_step(i, (0, 1), masked=True)
            _block_loops(i - 1, -1, functools.partial(needed, 0.0), shifted_step)

        @pl.when(bd_ref[0, 0] > SHIFT_MAX)
        def _():
            m_ref[...] = jnp.full(m_ref.shape, NEG, F32)
            online_step(i, (0, 1), masked=True)
            _block_loops(i - 1, -1, functools.partial(needed, bd_ref[0, 0]), online_step)

        outs, lses = [], []
        for hh in range(2):
            acc = acc_ref[hh]
            row_sum = jnp.broadcast_to(acc[:, DH:DH + 1], (tq, LANES))
            outs.append(acc / row_sum)
            lses.append(m_ref[hh] + jnp.log(row_sum))
        o_ref[...] = jnp.where(sel_a, outs[0], pltpu.roll(outs[1], 64, 1))
        row = lax.broadcasted_iota(jnp.int32, (8, tq), 0)
        lser_ref[0, 0] = jnp.where(row == 0, lses[0].T[0:8, :], lses[1].T[0:8, :])

    return pl.pallas_call(
        body, name="attn_fwd", grid=(NH // 2, nq),
        out_shape=(_sds((S, AW), F32), _sds((NH // 2, nq, 8, tq), F32)),
        in_specs=[SMEM_SPEC, SMEM_SPEC, SMEM_SPEC,
                  pl.BlockSpec((2, tq, LANES), lambda p, i: (p, i, 0)),
                  pl.BlockSpec((2, S, LANES), lambda p, i: (p, 0, 0)),
                  pl.BlockSpec((2, S, LANES), lambda p, i: (p, 0, 0))],
        out_specs=(pl.BlockSpec((tq, LANES), lambda p, i: (i, p)),
                   pl.BlockSpec((1, 1, 8, tq), lambda p, i: (p, i, 0, 0))),
        scratch_shapes=[pltpu.VMEM((2, tq, LANES), F32), pltpu.VMEM((2, tq, LANES), F32)],
        compiler_params=_cp(("parallel", "parallel")),
    )(fs, fe, bd, qa, ka, va)


HALO = 32
FWD_CHUNK = 64
BWD_CHUNK = 32


def _halo_prev(tm):
    return pl.BlockSpec((HALO, CW), lambda i: (jnp.maximum(i * (tm // HALO) - 1, 0), 0))


SUBLANES = 8
SHIFT_ROWS = 24


def _shifted_copies(ext_ref, sh_ref, tm):
    for k in range(1, SUBLANES):
        sh_ref[k - 1, 0:tm + SHIFT_ROWS, :] = ext_ref[k:k + tm + SHIFT_ROWS, :]


def _tap_windows(ext_ref, sh_ref, r0, offset, rows, shared):
    for k in range(SUBLANES):
        taps = sorted((offset(j), j) for j in range(KC) if offset(j) % SUBLANES == k)
        for group in ([taps] if shared and taps else [[t] for t in taps]):
            lo, hi = r0 + group[0][0] - k, r0 + group[-1][0] - k + rows
            window = ext_ref[lo:hi, :] if k == 0 else sh_ref[k - 1, lo:hi, :]
            for off, j in group:
                at = r0 + off - k - lo
                yield j, window[at:at + rows, :]


def _fwd_conv(u0, w32, cb, lng, lnb, beta_c, tm, ex=None):
    S = u0.shape[0]

    def body(cur_ref, prev_ref, w_ref, cb_ref, lng_ref, lnb_ref, beta_ref, u1_ref, mc_ref, ext_ref, sh_ref):
        i = pl.program_id(0)
        ext_ref[0:HALO, :] = jnp.where(i == 0, 0.0, prev_ref[...])
        ext_ref[HALO:, :] = cur_ref[...]
        _shifted_copies(ext_ref, sh_ref, tm)
        for r0 in range(0, tm, FWD_CHUNK):
            acc = jnp.zeros((FWD_CHUNK, CW), F32) + cb_ref[...]
            for j, rows in _tap_windows(ext_ref, sh_ref, r0, lambda j: 2 + j, FWD_CHUNK, shared=False):
                acc = acc + w_ref[j:j + 1, :] * rows
            u1_ref[r0:r0 + FWD_CHUNK, :] = acc
        u1 = u1_ref[...]
        mu = jnp.mean(u1, axis=-1, keepdims=True)
        d = u1 - mu
        rstd = lax.rsqrt(jnp.mean(d * d, axis=-1, keepdims=True) + EPS)
        u2 = d * rstd * lng_ref[...] + lnb_ref[...]
        u3 = u2 * _sigmoid(u2)
        rc = lax.rsqrt(jnp.mean(u3 * u3, axis=-1, keepdims=True) + EPS)
        mc_ref[...] = (u3 * rc * beta_ref[...]).astype(BF16)

    outs = pl.pallas_call(
        _hosted(body, ex, 7, 2, 2, S // tm), name="fwd_conv", grid=(S // tm,),
        out_shape=(_sds((S, CW), F32), _sds((S, CW), BF16), *(ex.out_shapes if ex else [])),
        in_specs=[_rows(tm, CW), _halo_prev(tm), _const((HALO, CW)), _const((1, CW)), _const((1, CW)), _const((1, CW)),
                  _const((1, CW)), *(ex.in_specs if ex else [])],
        out_specs=(_rows(tm, CW), _rows(tm, CW), *(ex.out_specs if ex else [])),
        scratch_shapes=[pltpu.VMEM((tm + HALO, CW), F32), pltpu.VMEM((SUBLANES - 1, tm + HALO, CW), F32),
                        *(ex.scratch if ex else [])],
        compiler_params=_cp(("arbitrary",)),
    )(u0, u0, w32, cb, lng, lnb, beta_c, *(ex.srcs if ex else []))
    return outs[:2], list(outs[2:])


def _fwd_out(o_attn, mc, x, mod8, n2g, beta_a, w_out, tm):
    S = x.shape[0]

    def body(o_ref, mc_ref, x_ref, mod_ref, n2g_ref, beta_ref, w_ref, mg_ref, ob_ref, x2_ref, h2_ref):
        ov = o_ref[...]
        ra = lax.rsqrt(jnp.mean(ov * ov, axis=-1, keepdims=True) + EPS)
        ma = (ov * ra * beta_ref[...]).astype(BF16)
        mcv = mc_ref[...]
        mg_ref[:, 0:AW] = ma
        mg_ref[:, AW:D] = mcv
        o = (jnp.dot(ma, w_ref[0:AW, :], preferred_element_type=F32)
             + jnp.dot(mcv, w_ref[AW:D, :], preferred_element_type=F32))
        ob_ref[...] = o.astype(BF16)
        x2 = x_ref[...] + mod_ref[2:3, :] * o
        x2_ref[...] = x2
        r2 = lax.rsqrt(jnp.mean(x2 * x2, axis=-1, keepdims=True) + EPS)
        h2_ref[...] = ((x2 * r2) * (n2g_ref[...] * (1.0 + mod_ref[4:5, :])) + mod_ref[3:4, :]).astype(BF16)

    return pl.pallas_call(
        body, name="fwd_out", grid=(S // tm,),
        out_shape=(_sds((S, D), BF16), _sds((S, D), BF16), _sds((S, D), F32), _sds((S, D), BF16)),
        in_specs=[_rows(tm, AW), _rows(tm, CW), _rows(tm, D), _const((8, D)), _const((1, D)), _const((1, AW)),
                  _const((D, D))],
        out_specs=(_rows(tm, D), _rows(tm, D), _rows(tm, D), _rows(tm, D)),
        compiler_params=_cp(("parallel",)),
    )(o_attn, mc, x, mod8, n2g, beta_a, w_out)


def _fwd_ffn(h2, w1, w2, x2, tgt, mod8, tm):
    S = h2.shape[0]
    nk = w1.shape[0]
    bf = w1.shape[2]

    def body(h2_ref, w1_ref, w2_ref, x2_ref, tgt_ref, mod_ref, r_ref, dy_ref, loss_ref, dg2_ref):
        @pl.when(pl.program_id(0) == 0)
        def _():
            loss_ref[...] = jnp.zeros_like(loss_ref)
            dg2_ref[...] = jnp.zeros_like(dg2_ref)

        f2 = None
        for k in range(nk):
            r = jnp.maximum(jnp.dot(h2_ref[...], w1_ref[k], preferred_element_type=F32), 0.0)
            r_ref[:, k * bf:(k + 1) * bf] = r.astype(BF16)
            part = jnp.dot((r * r).astype(BF16), w2_ref[k], preferred_element_type=F32)
            f2 = part if f2 is None else f2 + part
        e = x2_ref[...] + mod_ref[5:6, :] * f2 - tgt_ref[...]
        dy = e * (1.0 / D)
        dy_ref[...] = dy
        loss_ref[...] += 0.5 * jnp.sum(jnp.sum(e * dy, axis=1, keepdims=True), axis=0, keepdims=True)
        dg2_ref[...] += jnp.sum((dy * f2).reshape(tm // 8, 8, D), axis=0)

    once = pl.Buffered(1)
    return pl.pallas_call(
        body, name="fwd_ffn", grid=(S // tm,),
        out_shape=(_sds((S, DFF), BF16), _sds((S, D), F32), _sds((8, LANES), F32), _sds((8, D), F32)),
        in_specs=[_rows(tm, D), pl.BlockSpec((nk, D, bf), lambda i: (0, 0, 0), pipeline_mode=once),
                  pl.BlockSpec((nk, bf, D), lambda i: (0, 0, 0), pipeline_mode=once), _rows(tm, D), _rows(tm, D),
                  _const((8, D))],
        out_specs=(_rows(tm, DFF), _rows(tm, D), _const((8, LANES)), _const((8, D))),
        compiler_params=_cp(("arbitrary",)),
    )(h2, w1, w2, x2, tgt, mod8)


def _bwd_ffn(dy, mod8, r, w1, w2, tm):
    S = dy.shape[0]
    nk = w1.shape[0]
    bf = w1.shape[2]

    def body(dy_ref, mod_ref, r_ref, w1_ref, w2_ref, df2_ref, df1_ref, dh2_ref):
        df2 = (dy_ref[...] * mod_ref[5:6, :]).astype(BF16)
        df2_ref[...] = df2
        dh2 = None
        for k in range(nk):
            da = lax.dot_general(df2, w2_ref[k], NT, preferred_element_type=F32)
            df1 = (da * (2.0 * r_ref[:, k * bf:(k + 1) * bf].astype(F32))).astype(BF16)
            df1_ref[:, k * bf:(k + 1) * bf] = df1
            part = lax.dot_general(df1, w1_ref[k], NT, preferred_element_type=F32)
            dh2 = part if dh2 is None else dh2 + part
        dh2_ref[...] = dh2

    once = pl.Buffered(1)
    return pl.pallas_call(
        body, name="bwd_ffn", grid=(S // tm,),
        out_shape=(_sds((S, D), BF16), _sds((S, DFF), BF16), _sds((S, D), F32)),
        in_specs=[_rows(tm, D), _const((8, D)), _rows(tm, DFF),
                  pl.BlockSpec((nk, D, bf), lambda i: (0, 0, 0), pipeline_mode=once),
                  pl.BlockSpec((nk, bf, D), lambda i: (0, 0, 0), pipeline_mode=once)],
        out_specs=(_rows(tm, D), _rows(tm, DFF), _rows(tm, D)),
        compiler_params=_cp(("parallel",)),
    )(dy, mod8, r, w1, w2)


def _token_tile(S, want):
    while S % want:
        want //= 2
    return want


def _wgrad(a, b, name, square_a=False, col_pieces=False, tk=1024, bm=1024, bn=1024):
    S, M = a.shape
    N = b.shape[1]
    bm, bn, tk = min(bm, M), min(bn, N), _token_tile(S, tk)
    nk = S // tk

    def body(a_ref, b_ref, o_ref, acc_ref):
        av = a_ref[...]
        if square_a:
            af = av.astype(F32)
            av = (af * af).astype(BF16)
        part = lax.dot_general(av, b_ref[...], TN, preferred_element_type=F32)

        @pl.when(pl.program_id(2) == 0)
        def _():
            acc_ref[...] = part

        @pl.when(pl.program_id(2) > 0)
        def _():
            acc_ref[...] += part

        @pl.when(pl.program_id(2) == nk - 1)
        def _():
            if col_pieces:
                o_ref[0] = acc_ref[...].astype(BF16)
            else:
                o_ref[...] = acc_ref[...].astype(BF16)

    if col_pieces:
        out_shape, out_spec = _sds((N // bn, M, bn), BF16), pl.BlockSpec((1, bm, bn), lambda mi, ni, k: (ni, mi, 0))
    else:
        out_shape, out_spec = _sds((M, N), BF16), pl.BlockSpec((bm, bn), lambda mi, ni, k: (mi, ni))
    return pl.pallas_call(
        body, name=name, grid=(M // bm, N // bn, nk), out_shape=out_shape,
        in_specs=[pl.BlockSpec((tk, bm), lambda mi, ni, k: (k, mi)), pl.BlockSpec((tk, bn), lambda mi, ni, k: (k, ni))],
        out_specs=out_spec, scratch_shapes=[pltpu.VMEM((bm, bn), F32)],
        compiler_params=_cp(("parallel", "parallel", "arbitrary")),
    )(a, b)


def _wgrad_in(h1, pieces, tk=1024):
    S = h1.shape[0]
    tk = _token_tile(S, tk)
    widths = [p.shape[1] for p in pieces]
    offs = [sum(widths[:t]) for t in range(len(widths))]

    def body(a_ref, *refs):
        o_ref, acc_ref = refs[-2:]

        @pl.when(pl.program_id(0) == 0)
        def _():
            acc_ref[...] = jnp.zeros_like(acc_ref)

        for b_ref, off, w in zip(refs[:-2], offs, widths):
            acc_ref[:, off:off + w] += lax.dot_general(a_ref[...], b_ref[...], TN, preferred_element_type=F32)

        @pl.when(pl.program_id(0) == S // tk - 1)
        def _():
            o_ref[...] = acc_ref[...].astype(BF16)

    return pl.pallas_call(
        body, name="wgrad_in", grid=(S // tk,), out_shape=_sds((D, NP), BF16),
        in_specs=[_rows(tk, D)] + [_rows(tk, w) for w in widths], out_specs=_const((D, NP)),
        scratch_shapes=[pltpu.VMEM((D, NP), F32)], compiler_params=_cp(("arbitrary",)),
    )(h1, *pieces)


def _colsum8(t):
    return jnp.sum(t.reshape(t.shape[0] // 8, 8, t.shape[1]), axis=0)


def _bwd_mid(dh2, dy, x2, ob, o_attn, u1, mod8, n2g, beta_a, beta_c, lng, lnb, w_out, e512, tm, tq, ex=None):
    S = dy.shape[0]

    def body(dh2_ref, dy_ref, x2_ref, ob_ref, oa_ref, u1_ref, mod_ref, n2g_ref, ba_ref, bc_ref, lng_ref, lnb_ref, w_ref,
             e_ref, dx2_ref, do_ref, doa_ref, du1_ref, acc_d_ref, acc_h_ref, dr_ref):
        @pl.when(pl.program_id(0) == 0)
        def _():
            acc_d_ref[...] = jnp.zeros_like(acc_d_ref)
            acc_h_ref[...] = jnp.zeros_like(acc_h_ref)

        x2 = x2_ref[...]
        dh2 = dh2_ref[...]
        r2 = lax.rsqrt(jnp.mean(x2 * x2, axis=-1, keepdims=True) + EPS)
        xn2 = x2 * r2
        gain = n2g_ref[...] * (1.0 + mod_ref[4:5, :])
        dxn = dh2 * gain
        dx2 = dy_ref[...] + r2 * (dxn - xn2 * jnp.mean(dxn * xn2, axis=-1, keepdims=True))
        dx2_ref[...] = dx2
        t = dh2 * xn2
        acc_d_ref[0] += _colsum8(dh2)
        acc_d_ref[1] += _colsum8(t * n2g_ref[...])
        acc_d_ref[2] += _colsum8(t * (1.0 + mod_ref[4:5, :]))
        acc_d_ref[3] += _colsum8(dx2 * ob_ref[...].astype(F32))
        do = (dx2 * mod_ref[2:3, :]).astype(BF16)
        do_ref[...] = do
        dma = lax.dot_general(do, w_ref[0:AW, :], NT, preferred_element_type=F32)
        dmc = lax.dot_general(do, w_ref[AW:D, :], NT, preferred_element_type=F32)
        ov = oa_ref[...]
        ra = lax.rsqrt(jnp.mean(ov * ov, axis=-1, keepdims=True) + EPS)
        on = ov * ra
        acc_h_ref[0] += _colsum8(dma * on)
        don = dma * ba_ref[...]
        doa = (ra * (don - on * jnp.mean(don * on, axis=-1, keepdims=True))).astype(BF16)
        doa_ref[...] = doa
        delta_t = _dot2(doa.astype(F32) * ov, e_ref[...]).T
        for p in range(NH // 2):
            dr_ref[p, 0] = delta_t[2 * p:2 * p + 8, :]
        u1 = u1_ref[...]
        mu = jnp.mean(u1, axis=-1, keepdims=True)
        d = u1 - mu
        rstd = lax.rsqrt(jnp.mean(d * d, axis=-1, keepdims=True) + EPS)
        uh = d * rstd
        u2 = uh * lng_ref[...] + lnb_ref[...]
        sg = _sigmoid(u2)
        u3 = u2 * sg
        rc = lax.rsqrt(jnp.mean(u3 * u3, axis=-1, keepdims=True) + EPS)
        u3n = u3 * rc
        acc_h_ref[1] += _colsum8(dmc * u3n)
        du3n = dmc * bc_ref[...]
        du3 = rc * (du3n - u3n * jnp.mean(du3n * u3n, axis=-1, keepdims=True))
        du2 = du3 * (sg * (1.0 + u2 * (1.0 - sg)))
        acc_h_ref[2] += _colsum8(du2 * uh)
        acc_h_ref[3] += _colsum8(du2)
        duh = du2 * lng_ref[...]
        du1_ref[...] = rstd * (duh - jnp.mean(duh, axis=-1, keepdims=True)
                               - uh * jnp.mean(duh * uh, axis=-1, keepdims=True))

    per = tq // tm
    outs = pl.pallas_call(
        _hosted(body, ex, 14, 7, 0, S // tm), name="bwd_mid", grid=(S // tm,),
        out_shape=(_sds((S, D), F32), _sds((S, D), BF16), _sds((S, AW), BF16), _sds((S, CW), F32),
                   _sds((4, 8, D), F32), _sds((4, 8, AW), F32), _sds((NH // 2, S // tq, 8, tq), F32),
                   *(ex.out_shapes if ex else [])),
        in_specs=[_rows(tm, D), _rows(tm, D), _rows(tm, D), _rows(tm, D), _rows(tm, AW), _rows(tm, CW), _const((8, D)),
                  _const((1, D)), _const((1, AW)), _const((1, CW)), _const((1, CW)), _const((1, CW)), _const((D, D)),
                  _const((AW, LANES)), *(ex.in_specs if ex else [])],
        out_specs=(_rows(tm, D), _rows(tm, D), _rows(tm, AW), _rows(tm, CW), _const((4, 8, D)), _const((4, 8, AW)),
                   pl.BlockSpec((NH // 2, 1, 8, tm), lambda i: (0, i // per, 0, i % per)),
                   *(ex.out_specs if ex else [])),
        scratch_shapes=ex.scratch if ex else [],
        compiler_params=_cp(("arbitrary",)),
    )(dh2, dy, x2, ob, o_attn, u1, mod8, n2g, beta_a, beta_c, lng, lnb, w_out, e512, *(ex.srcs if ex else []))
    return outs[:7], list(outs[7:])


def _attn_bwd(qa, ka, v, do, lser, dr, fs, fe, bd, tq):
    S = qa.shape[1]
    nq = S // tq

    def body(fs_ref, fe_ref, bd_ref, ka_ref, v_ref, qa_ref, do_ref, lse_ref, dr_ref, dqa_hbm, dka_ref, dv_ref,
             accq_ref, acck_ref, accv_ref, out_sem):
        pr, j = pl.program_id(0), pl.program_id(1)
        sel_a = _lane((tq, LANES)) < DH
        vv = v_ref[...]
        zb = jnp.zeros_like(vv)
        vs = [jnp.where(sel_a, vv, zb), jnp.where(sel_a, zb, vv)]
        acck_ref[...] = jnp.zeros_like(acck_ref)
        accv_ref[...] = jnp.zeros_like(accv_ref)

        @pl.when(j == 0)
        def _():
            accq_ref[...] = jnp.zeros_like(accq_ref)

        def q_step(i, heads, masked=False):
            start = pl.multiple_of(i * tq, tq)
            dob = do_ref[pl.ds(start, tq), :]
            lse8 = lse_ref[0, i]
            dr8 = dr_ref[0, i]
            for hh in heads:
                qb = qa_ref[hh, pl.ds(start, tq), :]
                kb = ka_ref[hh]
                st = lax.dot_general(kb, qb, NT, preferred_element_type=F32)
                pt = jnp.exp(st - lse8[hh:hh + 1, :])
                if masked:
                    keep = (lax.broadcasted_iota(jnp.int32, (tq, tq), 0)
                            <= lax.broadcasted_iota(jnp.int32, (tq, tq), 1))
                    pt = jnp.where(keep, pt, 0.0)
                accv_ref[hh] += jnp.dot(pt.astype(BF16), dob, preferred_element_type=F32)
                dpt = lax.dot_general(vs[hh], dob, NT, preferred_element_type=F32)
                dst = (pt * (dpt - dr8[hh:hh + 1, :])).astype(BF16)
                acck_ref[hh] += jnp.dot(dst, qb, preferred_element_type=F32)
                accq_ref[hh, pl.ds(start, tq), :] += lax.dot_general(dst, kb, TN, preferred_element_type=F32)

        def needed(hh, i):
            top = fs_ref[2 * pr + hh, jnp.minimum(i, nq - 1)] + bd_ref[0, 0]
            return jnp.logical_and(i < nq, top - fe_ref[2 * pr + hh, j] >= SKIP_P)

        q_step(j, (0, 1), masked=True)
        _block_loops(j + 1, 1, needed, q_step)
        dka_ref[...] = acck_ref[...]
        dv_ref[...] = jnp.where(sel_a, accv_ref[0], accv_ref[1]).astype(BF16)

        @pl.when(j == nq - 1)
        def _():
            out = pltpu.make_async_copy(accq_ref, dqa_hbm.at[pl.ds(2 * pr, 2)], out_sem)
            out.start()
            out.wait()

    once = pl.Buffered(1)
    return pl.pallas_call(
        body, name="attn_bwd", grid=(NH // 2, nq),
        out_shape=(_sds((NH, S, LANES), F32), _sds((NH, S, LANES), F32), _sds((S, AW), BF16)),
        in_specs=[SMEM_SPEC, SMEM_SPEC, SMEM_SPEC,
                  pl.BlockSpec((2, tq, LANES), lambda p, j: (p, j, 0)),
                  pl.BlockSpec((tq, LANES), lambda p, j: (j, p)),
                  pl.BlockSpec((2, S, LANES), lambda p, j: (p, 0, 0), pipeline_mode=once),
                  pl.BlockSpec((S, LANES), lambda p, j: (0, p), pipeline_mode=once),
                  pl.BlockSpec((1, nq, 8, tq), lambda p, j: (p, 0, 0, 0)),
                  pl.BlockSpec((1, nq, 8, tq), lambda p, j: (p, 0, 0, 0))],
        out_specs=(pl.BlockSpec(memory_space=pl.ANY),
                   pl.BlockSpec((2, tq, LANES), lambda p, j: (p, j, 0)),
                   pl.BlockSpec((tq, LANES), lambda p, j: (j, p))),
        scratch_shapes=[pltpu.VMEM((2, S, LANES), F32), pltpu.VMEM((2, tq, LANES), F32),
                        pltpu.VMEM((2, tq, LANES), F32), pltpu.SemaphoreType.DMA],
        compiler_params=_cp(("arbitrary", "arbitrary")),
    )(fs, fe, bd, ka, v, qa, do, lser, dr)


def _bwd_conv(du1, u0, alin, agate, w32, tm, ex=None):
    S = du1.shape[0]
    nt = S // tm

    def body(du_ref, dun_ref, u0_ref, u0p_ref, alin_ref, agate_ref, w_ref,
             dalin_ref, dagate_ref, dw_ref, db_ref, extd_ref, extu_ref, du0_ref, shd_ref, shu_ref):
        i = pl.program_id(0)

        @pl.when(i == 0)
        def _():
            dw_ref[...] = jnp.zeros_like(dw_ref)
            db_ref[...] = jnp.zeros_like(db_ref)

        extd_ref[0:tm, :] = du_ref[...]
        extd_ref[tm:, :] = jnp.where(i == nt - 1, 0.0, dun_ref[...])
        extu_ref[0:HALO, :] = jnp.where(i == 0, 0.0, u0p_ref[...])
        extu_ref[HALO:, :] = u0_ref[...]
        _shifted_copies(extd_ref, shd_ref, tm)
        _shifted_copies(extu_ref, shu_ref, tm)
        db_ref[...] += _colsum8(du_ref[...])
        for r0 in range(0, tm, BWD_CHUNK):
            acc = jnp.zeros((BWD_CHUNK, CW), F32)
            for j, rows in _tap_windows(extd_ref, shd_ref, r0, lambda j: 30 - j, BWD_CHUNK, shared=True):
                acc = acc + w_ref[j:j + 1, :] * rows
            du0_ref[r0:r0 + BWD_CHUNK, :] = acc
            duc = du_ref[r0:r0 + BWD_CHUNK, :]
            for j, rows in _tap_windows(extu_ref, shu_ref, r0, lambda j: 2 + j, BWD_CHUNK, shared=True):
                dw_ref[j] += _colsum8(duc * rows)
        du0 = du0_ref[...]
        al = alin_ref[...].astype(F32)
        sg = _sigmoid(agate_ref[...].astype(F32))
        dalin_ref[...] = (du0 * sg).astype(BF16)
        dagate_ref[...] = (du0 * al * sg * (1.0 - sg)).astype(BF16)

    nxt = pl.BlockSpec((HALO, CW), lambda i: (jnp.minimum((i + 1) * (tm // HALO), S // HALO - 1), 0))
    outs = pl.pallas_call(
        _hosted(body, ex, 7, 4, 5, nt), name="bwd_conv", grid=(nt,),
        out_shape=(_sds((S, CW), BF16), _sds((S, CW), BF16), _sds((HALO, 8, CW), F32), _sds((8, CW), F32),
                   *(ex.out_shapes if ex else [])),
        in_specs=[_rows(tm, CW), nxt, _rows(tm, CW), _halo_prev(tm), _rows(tm, CW), _rows(tm, CW), _const((HALO, CW)),
                  *(ex.in_specs if ex else [])],
        out_specs=(_rows(tm, CW), _rows(tm, CW), _const((HALO, 8, CW)), _const((8, CW)), *(ex.out_specs if ex else [])),
        scratch_shapes=[pltpu.VMEM((tm + HALO, CW), F32), pltpu.VMEM((tm + HALO, CW), F32), pltpu.VMEM((tm, CW), F32),
                        pltpu.VMEM((SUBLANES - 1, tm + HALO, CW), F32), pltpu.VMEM((SUBLANES - 1, tm + HALO, CW), F32),
                        *(ex.scratch if ex else [])],
        compiler_params=_cp(("arbitrary",)),
    )(du1, du1, u0, u0, alin, agate, w32, *(ex.srcs if ex else []))
    return outs[:4], list(outs[4:])


def _bwd_qk(dqa, dka, qn, kn, rq, rk, fgb, qg512, kg512, e512, et512, tm):
    S = qn.shape[0]
    nt = S // tm

    def body(dqa_ref, dka_ref, qn_ref, kn_ref, rq_ref, rk_ref, fgb_ref, qg_ref, kg_ref, e_ref, et_ref,
             dq_ref, dk_ref, dfg_ref, accg_ref, accb_ref, carry_ref):
        @pl.when(pl.program_id(0) == 0)
        def _():
            carry_ref[...] = jnp.zeros_like(carry_ref)
            accg_ref[...] = jnp.zeros_like(accg_ref)
            accb_ref[...] = jnp.zeros_like(accb_ref)

        lane = _lane((tm, LANES))
        sel_a = lane < DH
        df = jnp.zeros((tm, LANES), F32)
        for h in range(NH):
            col = dqa_ref[h][:, 64:65] - dka_ref[h][:, 67:68]
            df = jnp.where(lane == h, col, df)
        tri = (lax.broadcasted_iota(jnp.int32, (tm, tm), 0) <= lax.broadcasted_iota(jnp.int32, (tm, tm), 1)
               ).astype(F32).astype(BF16)
        dlf = _dot3(tri, df) + carry_ref[0:1, :]
        carry_ref[...] = jnp.broadcast_to(dlf[0:1, :], carry_ref.shape)
        dfg = jnp.where(lane < NH, dlf * _sigmoid(-fgb_ref[...]), 0.0)
        dfg_ref[...] = dfg.astype(BF16)
        accb_ref[...] += _colsum8(dfg)

        def norm_bwd(src_ref, n_ref, r_ref, g_ref, scale, slot):
            pairs = []
            for p in range(NH // 2):
                b = pltpu.roll(src_ref[2 * p + 1], 64, 1)
                pairs.append(jnp.where(sel_a, src_ref[2 * p], b))
            dh = jnp.concatenate(pairs, axis=1) * scale
            tn = n_ref[...].astype(F32)
            accg_ref[slot] += _colsum8(dh * tn)
            dn = dh * g_ref[...]
            mean = _dot2(dn * tn, e_ref[...]) * (1.0 / DH)
            corr = _dot2(mean, et_ref[...])
            rf = _dot2(r_ref[...], et_ref[...])
            return (rf * (dn - tn * corr)).astype(BF16)

        dq_ref[...] = norm_bwd(dqa_ref, qn_ref, rq_ref, qg_ref, DH ** -0.5, 0)
        dk_ref[...] = norm_bwd(dka_ref, kn_ref, rk_ref, kg_ref, 1.0, 1)

    rev = lambda n: pl.BlockSpec((tm, n), lambda i: (nt - 1 - i, 0))
    hm = pl.BlockSpec((NH, tm, LANES), lambda i: (0, nt - 1 - i, 0))
    dq, dk, dfg, accg, accb = pl.pallas_call(
        body, name="bwd_qk", grid=(nt,),
        out_shape=(_sds((S, AW), BF16), _sds((S, AW), BF16), _sds((S, LANES), BF16), _sds((2, 8, AW), F32),
                   _sds((8, LANES), F32)),
        in_specs=[hm, hm, rev(AW), rev(AW), rev(LANES), rev(LANES), rev(LANES), _const((1, AW)), _const((1, AW)),
                  _const((AW, LANES)), _const((LANES, AW))],
        out_specs=(rev(AW), rev(AW), rev(LANES), _const((2, 8, AW)), _const((8, LANES))),
        scratch_shapes=[pltpu.VMEM((8, LANES), F32)], compiler_params=_cp(("arbitrary",)),
    )(dqa, dka, qn, kn, rq, rk, fgb, qg512, kg512, e512, et512)
    return dq, dk, dfg, accg, accb


def _bwd_in(dq, dk, dv, dalin, dagate, dfg, w_in_p, x, dx2, mod8, n1g, tm, ex=None):
    S = x.shape[0]

    def body(dq_ref, dk_ref, dv_ref, dal_ref, dag_ref, dfg_ref, w_ref, x_ref, dx2_ref, mod_ref, n1g_ref,
             dx_ref, acc_ref):
        @pl.when(pl.program_id(0) == 0)
        def _():
            acc_ref[...] = jnp.zeros_like(acc_ref)

        def part(ref, a, b):
            return lax.dot_general(ref[...], w_ref[:, a:b], NT, preferred_element_type=F32)

        dh = (part(dq_ref, 0, 512) + part(dk_ref, 512, 1024) + part(dv_ref, 1024, 1536) + part(dal_ref, 1536, 2048)
              + part(dag_ref, 2048, 2560) + part(dfg_ref, 2560, NP))
        xv = x_ref[...]
        r1 = lax.rsqrt(jnp.mean(xv * xv, axis=-1, keepdims=True) + EPS)
        xn = xv * r1
        gain = n1g_ref[...] * (1.0 + mod_ref[1:2, :])
        t = dh * xn
        acc_ref[0] += _colsum8(dh)
        acc_ref[1] += _colsum8(t * n1g_ref[...])
        acc_ref[2] += _colsum8(t * (1.0 + mod_ref[1:2, :]))
        dxn = dh * gain
        dx_ref[...] = dx2_ref[...] + r1 * (dxn - xn * jnp.mean(dxn * xn, axis=-1, keepdims=True))

    outs = pl.pallas_call(
        _hosted(body, ex, 11, 2, 0, S // tm), name="bwd_in", grid=(S // tm,),
        out_shape=(_sds((S, D), F32), _sds((3, 8, D), F32), *(ex.out_shapes if ex else [])),
        in_specs=[_rows(tm, AW), _rows(tm, AW), _rows(tm, AW), _rows(tm, CW), _rows(tm, CW), _rows(tm, LANES),
                  _const((D, NP)), _rows(tm, D), _rows(tm, D), _const((8, D)), _const((1, D)),
                  *(ex.in_specs if ex else [])],
        out_specs=(_rows(tm, D), _const((3, 8, D)), *(ex.out_specs if ex else [])),
        scratch_shapes=ex.scratch if ex else [],
        compiler_params=_cp(("arbitrary",)),
    )(dq, dk, dv, dalin, dagate, dfg, w_in_p, x, dx2, mod8, n1g, *(ex.srcs if ex else []))
    return outs[:2], list(outs[2:])


def _adam(w, g, m, v):
    m_new = B1 * m + (1.0 - B1) * g
    v_new = B2 * v + (1.0 - B2) * (g * g)
    m_hat = m_new / (1.0 - B1 ** STEP)
    v_hat = v_new / (1.0 - B2 ** STEP)
    delta = -LR * (m_hat / (jnp.sqrt(v_hat) + AEPS) + WD * w)
    return delta, m_new, v_new


def _pair_adamw(slots, w, m, v, name, tr=256):
    ns, R, C = slots.shape
    tr = tr if R % tr == 0 else R
    nt = R // tr

    def body(s_ref, w_ref, m_ref, v_ref, g_ref, d_ref, mo_ref, vo_ref, mine_ref, theirs_ref, send_sems, recv_sems):
        i = pl.program_id(0)
        part = s_ref[0].astype(F32)
        for k in range(1, ns):
            part = part + s_ref[k].astype(F32)
        mine_ref[i] = part
        swap = pltpu.make_async_remote_copy(
            src_ref=mine_ref.at[i], dst_ref=theirs_ref.at[i], send_sem=send_sems.at[i], recv_sem=recv_sems.at[i],
            device_id=(lax.axis_index("x"), lax.axis_index("y"), 1 - lax.axis_index("c")),
            device_id_type=pl.DeviceIdType.MESH)
        swap.start()
        swap.wait()
        g = part + theirs_ref[i]
        g_ref[...] = g
        d_ref[...], mo_ref[...], vo_ref[...] = _adam(w_ref[...], g, m_ref[...], v_ref[...])

    blk = pl.BlockSpec((tr, C), lambda i: (i, 0))
    return pl.pallas_call(
        body, name=name, grid=(nt,), out_shape=tuple(_sds((R, C), F32) for _ in range(4)),
        in_specs=[pl.BlockSpec((ns, tr, C), lambda i: (0, i, 0)), blk, blk, blk], out_specs=(blk, blk, blk, blk),
        scratch_shapes=[pltpu.VMEM((nt, tr, C), F32), pltpu.VMEM((nt, tr, C), F32),
                        pltpu.SemaphoreType.DMA((nt,)), pltpu.SemaphoreType.DMA((nt,))],
        compiler_params=_cp(("arbitrary",)),
    )(slots, w, m, v)


def _ada_adamw(sct, dmod, w, m, v):
    R, C = w.shape
    tr, bc = 256, 512

    def body(sct_ref, dm_ref, w_ref, m_ref, v_ref, g_ref, d_ref, mo_ref, vo_ref):
        g = sct_ref[:, 0:1] * dm_ref[0:1, :]
        for b in range(1, N_DEV):
            g = g + sct_ref[:, b:b + 1] * dm_ref[b:b + 1, :]
        g_ref[...] = g
        d_ref[...], mo_ref[...], vo_ref[...] = _adam(w_ref[...], g, m_ref[...], v_ref[...])

    blk = pl.BlockSpec((tr, bc), lambda i, j: (i, j))
    return pl.pallas_call(
        body, name="ada_adamw", grid=(R // tr, C // bc), out_shape=tuple(_sds((R, C), F32) for _ in range(4)),
        in_specs=[pl.BlockSpec((tr, N_DEV), lambda i, j: (i, 0)), pl.BlockSpec((N_DEV, bc), lambda i, j: (0, j)),
                  blk, blk, blk],
        out_specs=(blk, blk, blk, blk), compiler_params=_cp(("parallel", "parallel")),
    )(sct, dmod, w, m, v)


PACK = {"dmod": 0, "norm1_g": 6144, "norm2_g": 7168, "q_norm_g": 8192, "k_norm_g": 8704, "b_f": 9216, "conv_b": 9344,
        "conv_ln_g": 9856, "conv_ln_b": 10368, "beta_attn": 10880, "beta_conv": 11392}
SMALL_NAMES = ["b_ada", "norm1_g", "q_norm_g", "k_norm_g", "b_f", "conv_b", "conv_ln_g", "conv_ln_b", "beta_attn",
               "beta_conv", "norm2_g"]


def _pack_small(acc1, acc_d, acc_h, dg2, accg, accb, dcb):
    def body(a1_ref, ad_ref, ah_ref, dg2_ref, ag_ref, ab_ref, cb_ref, o_ref):
        def put(off, rows):
            o_ref[:, off:off + rows.shape[1]] = jnp.sum(rows, axis=0, keepdims=True)

        for t, rows in enumerate((a1_ref[0], a1_ref[1], ad_ref[3], ad_ref[0], ad_ref[1], dg2_ref[...])):
            put(PACK["dmod"] + t * D, rows)
        put(PACK["norm1_g"], a1_ref[2])
        put(PACK["norm2_g"], ad_ref[2])
        put(PACK["q_norm_g"], ag_ref[0])
        put(PACK["k_norm_g"], ag_ref[1])
        put(PACK["b_f"], ab_ref[...])
        put(PACK["conv_b"], cb_ref[...])
        put(PACK["conv_ln_g"], ah_ref[2])
        put(PACK["conv_ln_b"], ah_ref[3])
        put(PACK["beta_attn"], ah_ref[0])
        put(PACK["beta_conv"], ah_ref[1])

    vm = pl.BlockSpec(memory_space=pltpu.VMEM)
    return pl.pallas_call(body, name="pack_small", out_shape=_sds((1, SMALL_IN), F32), in_specs=[vm] * 7, out_specs=vm,
                          )(acc1, acc_d, acc_h, dg2, accg, accb, dcb)


def _small_adamw(slots, fold, ws, ms, vs):
    n = len(ws)
    widths = [w.shape[1] for w in ws]

    def body(s_ref, f_ref, *refs):
        w_refs, m_refs, v_refs = refs[:n], refs[n:2 * n], refs[2 * n:3 * n]
        outs = refs[3 * n:]
        tot = s_ref[0:1, :]
        for k in range(1, N_DEV):
            tot = tot + s_ref[k:k + 1, :]

        def grad(name, width):
            if name == "b_ada":
                return tot[:, 0:6 * D]
            seg = tot[:, PACK[name]:PACK[name] + max(width, LANES)]
            if name in ("q_norm_g", "k_norm_g"):
                seg = jnp.dot(jnp.broadcast_to(tot[:, PACK[name]:PACK[name] + AW], (8, AW)), f_ref[...], precision=HI,
                              preferred_element_type=F32)[0:1, :]
            return seg[:, 0:width]

        for t, (name, width) in enumerate(zip(SMALL_NAMES, widths)):
            g = grad(name, width)
            d, m_new, v_new = _adam(w_refs[t][...], g, m_refs[t][...], v_refs[t][...])
            outs[t][...] = g
            outs[n + t][...] = d
            outs[2 * n + t][...] = m_new
            outs[3 * n + t][...] = v_new

    vm = pl.BlockSpec(memory_space=pltpu.VMEM)
    outs = pl.pallas_call(
        body, name="adamw_small", out_shape=tuple(_sds((1, w), F32) for _ in range(4) for w in widths),
        in_specs=[vm] * (2 + 3 * n), out_specs=tuple(vm for _ in range(4 * n)),
    )(slots, fold, *ws, *ms, *vs)
    return [list(outs[k * n:(k + 1) * n]) for k in range(4)]


def _perm_in(w):
    pad = jnp.zeros((w.shape[0], NP - 2568), w.dtype)
    return jnp.concatenate([w[:, :1536], w[:, 1544:2568], w[:, 1536:1544], pad], axis=1)


def _pad_lanes(vec, n=LANES):
    return jnp.pad(vec, ((0, 0), (0, n - vec.shape[1])))


def kernel(x, c, w_ada, b_ada, norm1_g, w_in, q_norm_g, k_norm_g, b_f, conv_w, conv_b, conv_ln_g, conv_ln_b, beta_attn, beta_conv, w_out, norm2_g, w_ff1, w_ff2, loss_target, m_w_ada, m_b_ada, m_norm1_g, m_w_in, m_q_norm_g, m_k_norm_g, m_b_f, m_conv_w, m_conv_b, m_conv_ln_g, m_conv_ln_b, m_beta_attn, m_beta_conv, m_w_out, m_norm2_g, m_w_ff1, m_w_ff2, v_w_ada, v_b_ada, v_norm1_g, v_w_in, v_q_norm_g, v_k_norm_g, v_b_f, v_conv_w, v_conv_b, v_conv_ln_g, v_conv_ln_b, v_beta_attn, v_beta_conv, v_w_out, v_norm2_g, v_w_ff1, v_w_ff2):
    S = x.shape[1]
    tm = min(256, S)
    tw = min(512, S)
    tq = min(512, S // 2)
    xs, tgt = x[0], loss_target[0]
    chip = 2 * lax.axis_index("x") + lax.axis_index("y")
    e512, et512 = _head_sum_mats()

    conv_w32 = jnp.pad(conv_w[0], ((0, 1), (0, 0)))
    c_all, g_in = _exchange([(c, "bcast8"), (w_in[0].astype(BF16), "chip4")], "gather_in")
    later_weights = _Exchange([(w_out[0].astype(BF16), "chip4"), (conv_w32, "chip4")])
    c_all = c_all.reshape(N_DEV, D)
    w_in_p = _perm_in(jnp.transpose(g_in, (1, 0, 2)).reshape(D, 2568))

    b_shard = lax.dynamic_slice(b_ada, (0, chip * 1536), (1, 1536))
    mod_rows, sc_all = _mod_shard(c_all, w_ada[0], b_shard)
    (mod_slots,) = _exchange([(mod_rows.reshape(N_DEV, 1, 1536), "all8")], "scatter_mod")
    mod = mod_slots.reshape(4, 2, 1536)[:, 0, :].reshape(6, D)
    mod8 = jnp.pad(mod, ((0, 2), (0, 0)))

    qg512 = jnp.tile(q_norm_g, (1, NH))
    kg512 = jnp.tile(k_norm_g, (1, NH))
    bf128 = _pad_lanes(b_f)

    (h1, qh, kh, vb, qn, kn, rq, rk, fgb, alin, agate, u0), (g_out, g_cw) = _fwd_in(
        xs, mod8, norm1_g, w_in_p, e512, et512, qg512, kg512, bf128, tw, ex=later_weights)
    w_out_f = g_out.reshape(D, D)
    cw32 = jnp.transpose(g_cw, (1, 0, 2)).reshape(HALO, CW)
    bd = _logit_bound(q_norm_g, k_norm_g)
    (qa, ka, va, fcum), (w1,) = _fwd_decay(fgb, qh, kh, vb, _shift(bd), tm,
                                           ex=_Exchange([(w_ff1[0].astype(BF16), "chip4")]))
    fs, fe = _skip_tables(fcum, tq)
    o_attn, lser = _attn_fwd(qa, ka, va, fs, fe, bd, tq)
    (u1, mc), (w2,) = _fwd_conv(u0, cw32, conv_b, conv_ln_g, conv_ln_b, beta_conv, tm,
                                ex=_Exchange([(w_ff2[0].astype(BF16), "chip4")]))
    merged, ob, x2, h2 = _fwd_out(o_attn, mc, xs, mod8, norm2_g, beta_attn, w_out_f, tw)
    r, dy, loss8, dg2 = _fwd_ffn(h2, w1, w2, x2, tgt, mod8, tw)

    df2, df1, dh2 = _bwd_ffn(dy, mod8, r, w1, w2, tw)
    gw_ff2 = _wgrad(r, df2, "wgrad_ff2", square_a=True)
    gw_ff1 = _wgrad(h2, df1, "wgrad_ff1", col_pieces=True)
    (dx2, do, doa, du1, acc_d, acc_h, dr), (p_ff1,) = _bwd_mid(
        dh2, dy, x2, ob, o_attn, u1, mod8, norm2_g, beta_attn, beta_conv, conv_ln_g, conv_ln_b, w_out_f, e512, tm, tq,
        ex=_Exchange([(gw_ff1, "chip4p")]))
    gw_out = _wgrad(merged, do, "wgrad_out")
    dqa, dka, dv = _attn_bwd(qa, ka, vb, doa, lser, dr, fs, fe, bd, tq)
    (dalin, dagate, dcw, dcb), (p_out, p_ff2) = _bwd_conv(
        du1, u0, alin, agate, cw32, tm,
        ex=_Exchange([(gw_out.reshape(4, 256, D), "chip4p"), (gw_ff2.reshape(4, D, D), "chip4p")]))
    dq, dk, dfg, accg, accb = _bwd_qk(dqa, dka, qn, kn, rq, rk, fgb, qg512, kg512, e512, et512, tm)
    gw_in_p = _wgrad_in(h1, [dq, dk, dv, dalin, dagate, dfg])
    gw_in = jnp.concatenate([gw_in_p[:, :1536], gw_in_p[:, 2560:2568], gw_in_p[:, 1536:2560]], axis=1)
    s8 = lambda a: jnp.sum(a, axis=-2)
    gcw = s8(dcw)
    in_grads = _Exchange([(jnp.transpose(gw_in.reshape(D, 4, 642), (1, 0, 2)), "chip4p"),
                          (jnp.transpose(gcw.reshape(HALO, 4, LANES), (1, 0, 2)), "chip4p")])
    (grad_x, acc1), (p_in, p_cw) = _bwd_in(dq, dk, dv, dalin, dagate, dfg, w_in_p, xs, dx2, mod8, norm1_g, tw,
                                           ex=in_grads)

    small = _pack_small(acc1, acc_d, acc_h, dg2, accg, accb, dcb)
    (small_s,) = _exchange([(small, "bcast8")], "gather_small")
    small_s = small_s.reshape(N_DEV, SMALL_IN)

    g_in_, d_in, nm_in, nv_in = _pair_adamw(p_in, w_in[0], m_w_in[0], v_w_in[0], "adamw_in")
    g_out_, d_out, nm_out, nv_out = _pair_adamw(p_out, w_out[0], m_w_out[0], v_w_out[0], "adamw_out")
    g_f1, d_f1, nm_f1, nv_f1 = _pair_adamw(p_ff1, w_ff1[0], m_w_ff1[0], v_w_ff1[0], "adamw_ff1")
    g_f2, d_f2, nm_f2, nv_f2 = _pair_adamw(p_ff2, w_ff2[0], m_w_ff2[0], v_w_ff2[0], "adamw_ff2")
    pad_row = lambda a, fill: jnp.pad(a[0], ((0, 1), (0, 0)), constant_values=fill)
    g_cw_, d_cw, nm_cw, nv_cw = (a[:KC] for a in _pair_adamw(
        p_cw, pad_row(conv_w, 0.0), pad_row(m_conv_w, 0.0), pad_row(v_conv_w, 1.0), "adamw_conv_w"))
    dmod_shard = lax.dynamic_slice(small_s[:, :6 * D], (0, chip * 1536), (N_DEV, 1536))
    g_ada, d_ada, nm_ada, nv_ada = _ada_adamw(sc_all.T, dmod_shard, w_ada[0], m_w_ada[0], v_w_ada[0])

    fold = np.zeros((AW, LANES), np.float32)
    fold[np.arange(AW), np.arange(AW) % DH] = 1.0
    smalls = [b_ada, norm1_g, q_norm_g, k_norm_g, b_f, conv_b, conv_ln_g, conv_ln_b, beta_attn, beta_conv, norm2_g]
    m_smalls = [m_b_ada, m_norm1_g, m_q_norm_g, m_k_norm_g, m_b_f, m_conv_b, m_conv_ln_g, m_conv_ln_b, m_beta_attn,
                m_beta_conv, m_norm2_g]
    v_smalls = [v_b_ada, v_norm1_g, v_q_norm_g, v_k_norm_g, v_b_f, v_conv_b, v_conv_ln_g, v_conv_ln_b, v_beta_attn,
                v_beta_conv, v_norm2_g]
    gs, ds, ms, vs = _small_adamw(small_s, jnp.asarray(fold), smalls, m_smalls, v_smalls)

    loss = lax.psum(loss8[0, 0], ("x", "y", "c"))
    big = {"w_ada": (g_ada, d_ada, nm_ada, nv_ada), "w_in": (g_in_, d_in, nm_in, nv_in),
           "conv_w": (g_cw_, d_cw, nm_cw, nv_cw), "w_out": (g_out_, d_out, nm_out, nv_out),
           "w_ff1": (g_f1, d_f1, nm_f1, nv_f1), "w_ff2": (g_f2, d_f2, nm_f2, nv_f2)}
    order =["w_ada", "b_ada", "norm1_g", "w_in", "q_norm_g", "k_norm_g", "b_f", "conv_w", "conv_b", "conv_ln_g",
             "conv_ln_b", "beta_attn", "beta_conv", "w_out", "norm2_g", "w_ff1", "w_ff2"]

    def leaf(name, which):
        if name in big:
            return big[name][which][None]
        return (gs, ds, ms, vs)[which][SMALL_NAMES.index(name)]

    return (loss, grad_x[None], *[leaf(n, 0) for n in order], *[leaf(n, 1) for n in order],
            *[leaf(n, 2) for n in order], *[leaf(n, 3) for n in order])
```

```python
import functools

import numpy as np
import jax
import jax.numpy as jnp
from jax import lax
from jax.experimental import pallas as pl
from jax.experimental.pallas import tpu as pltpu

F32, BF16 = jnp.float32, jnp.bfloat16
HI = lax.Precision.HIGHEST
D = 1024
AW = 512
CW = 512
NH = 8
DH = 64
KC = 31
DFF = 4096
NP = 2688
EPS = 1e-6
NEG = -1e30
LANES = 128
VMEM_LIMIT = 56 * 2**20
NT = (((1,), (1,)), ((), ()))
TN = (((0,), (0,)), ((), ()))
LR, B1, B2, AEPS, WD, STEP = 0.001, 0.9, 0.999, 1e-08, 0.01, 10
N_DEV = 8
SMALL_IN = 12032


def _cp(sem=None, vmem=VMEM_LIMIT):
    kw = dict(vmem_limit_bytes=vmem)
    if sem is not None:
        kw["dimension_semantics"] = sem
    return pltpu.CompilerParams(**kw)


def _rows(tm, n):
    return pl.BlockSpec((tm, n), lambda i: (i, 0))


def _const(shape):
    nd = len(shape)
    return pl.BlockSpec(shape, lambda *_: (0,) * nd)


def _sds(shape, dt):
    return jax.ShapeDtypeStruct(shape, dt)


def _lane(shape):
    return lax.broadcasted_iota(jnp.int32, shape, len(shape) - 1)


def _sigmoid(x):
    return 1.0 / (1.0 + jnp.exp(-x))


class _Exchange:
    MASKS = {"chip4": (2, 4, 6), "chip4p": (2, 4, 6), "all8": (1, 2, 3, 4, 5, 6, 7), "bcast8": (1, 2, 3, 4, 5, 6, 7)}

    def __init__(self, items):
        self.srcs = [s for s, _ in items]
        self.kinds = [k for _, k in items]
        self.n = len(items)
        self.out_shapes = []
        for s, k in items:
            shape = {"all8": (N_DEV,) + s.shape[1:], "bcast8": (N_DEV,) + s.shape, "chip4": (4,) + s.shape,
                     "chip4p": (4,) + s.shape[1:]}[k]
            self.out_shapes.append(_sds(shape, s.dtype))
        self.sem_index = {}
        for t, k in enumerate(self.kinds):
            for m in self.MASKS[k]:
                self.sem_index[(t, m)] = len(self.sem_index)
        n_sem = len(self.sem_index)
        self.scratch = [pltpu.SemaphoreType.DMA((n_sem,)), pltpu.SemaphoreType.DMA((n_sem,)),
                        pltpu.SemaphoreType.DMA((self.n,))]
        self.in_specs = [pl.BlockSpec(memory_space=pl.ANY)] * self.n
        self.out_specs = [pl.BlockSpec(memory_space=pl.ANY)] * self.n

    def copies(self, src_refs, dst_refs, send_sems, recv_sems, local_sems):
        x, y, c = lax.axis_index("x"), lax.axis_index("y"), lax.axis_index("c")
        my_id = 4 * x + 2 * y + c
        my_chip = 2 * x + y

        def piece(t, dev_id, chip):
            k = self.kinds[t]
            return src_refs[t].at[dev_id] if k == "all8" else src_refs[t].at[chip] if k == "chip4p" else src_refs[t]

        out = []
        for t in range(self.n):
            slot = dst_refs[t].at[my_chip if self.kinds[t] in ("chip4", "chip4p") else my_id]
            out.append(pltpu.make_async_copy(piece(t, my_id, my_chip), slot, local_sems.at[t]))
            for m in self.MASKS[self.kinds[t]]:
                px = 1 - x if m & 4 else x
                py = 1 - y if m & 2 else y
                pc = 1 - c if m & 1 else c
                s = self.sem_index[(t, m)]
                out.append(pltpu.make_async_remote_copy(
                    src_ref=piece(t, 4 * px + 2 * py + pc, 2 * px + py), dst_ref=slot,
                    send_sem=send_sems.at[s], recv_sem=recv_sems.at[s],
                    device_id=(px, py, pc), device_id_type=pl.DeviceIdType.MESH))
        return out


def _hosted(body, ex, n_in, n_out, n_scr, n_steps):
    if ex is None:
        return body

    def wrapped(*refs):
        ins, xin = refs[:n_in], refs[n_in:n_in + ex.n]
        o0 = n_in + ex.n
        outs, xout = refs[o0:o0 + n_out], refs[o0 + n_out:o0 + n_out + ex.n]
        s0 = o0 + n_out + ex.n
        scr, sems = refs[s0:s0 + n_scr], refs[s0 + n_scr:]

        @pl.when(pl.program_id(0) == 0)
        def _():
            for cp in ex.copies(xin, xout, *sems):
                cp.start()

        body(*ins, *outs, *scr)

        @pl.when(pl.program_id(0) == n_steps - 1)
        def _():
            for cp in ex.copies(xin, xout, *sems):
                cp.wait()

    return wrapped


def _exchange(items, name):
    ex = _Exchange(items)
    n = ex.n

    def body(*refs):
        copies = ex.copies(refs[:n], refs[n:2 * n], *refs[2 * n:])
        for cp in copies:
            cp.start()
        for cp in copies:
            cp.wait()

    outs = pl.pallas_call(
        body, name=name, out_shape=tuple(ex.out_shapes), in_specs=ex.in_specs, out_specs=tuple(ex.out_specs),
        scratch_shapes=ex.scratch,
    )(*ex.srcs)
    return list(outs)


def _mod_shard(c_all, w_ada, b_shard):
    n = w_ada.shape[1]

    def body(c_ref, w_ref, b_ref, o_ref, sc_ref):
        cv = c_ref[...]
        sc = cv * _sigmoid(cv)
        sc_ref[...] = sc
        o_ref[...] = jnp.dot(sc, w_ref[...], precision=HI, preferred_element_type=F32) + b_ref[...]

    bn = 512
    return pl.pallas_call(
        body, name="mod_shard", out_shape=(_sds((N_DEV, n), F32), _sds((N_DEV, D), F32)), grid=(n // bn,),
        in_specs=[_const((N_DEV, D)), pl.BlockSpec((D, bn), lambda j: (0, j)), pl.BlockSpec((1, bn), lambda j: (0, j))],
        out_specs=(pl.BlockSpec((N_DEV, bn), lambda j: (0, j)), _const((N_DEV, D))),
        compiler_params=_cp(("arbitrary",)),
    )(c_all, w_ada, b_shard)


def _head_sum_mats():
    e = np.zeros((AW, LANES), np.float32)
    for h in range(NH):
        e[h * DH:(h + 1) * DH, h] = 1.0
    return jnp.asarray(e, BF16), jnp.asarray(e.T.copy(), BF16)


def _dot2(x, w):
    hi = x.astype(BF16)
    lo = (x - hi.astype(F32)).astype(BF16)
    return jnp.dot(hi, w, preferred_element_type=F32) + jnp.dot(lo, w, preferred_element_type=F32)


def _fwd_in(x, mod8, n1g, w_in_p, e512, et512, qg512, kg512, bf128, tm, ex=None):
    S = x.shape[0]

    def body(x_ref, mod_ref, n1g_ref, w_ref, e_ref, et_ref, qg_ref, kg_ref, bf_ref,
             h1_ref, qh_ref, kh_ref, v_ref, qn_ref, kn_ref, rq_ref, rk_ref, fgb_ref, alin_ref, agate_ref, u0_ref):
        xv = x_ref[...]
        r1 = lax.rsqrt(jnp.mean(xv * xv, axis=-1, keepdims=True) + EPS)
        h = (xv * r1) * (n1g_ref[...] * (1.0 + mod_ref[1:2, :])) + mod_ref[0:1, :]
        hb = h.astype(BF16)
        h1_ref[...] = hb

        def seg(a, b):
            return jnp.dot(hb, w_ref[:, a:b], preferred_element_type=F32)

        def headnorm(t, g_ref, scale, n_ref, r_ref, o_ref):
            ss = _dot2(t * t, e_ref[...])
            r = lax.rsqrt(ss * (1.0 / DH) + EPS)
            tn = t * _dot2(r, et_ref[...])
            n_ref[...] = tn.astype(BF16)
            r_ref[...] = r
            o_ref[...] = (tn * (g_ref[...] * scale)).astype(BF16)

        headnorm(seg(0, 512), qg_ref, DH ** -0.5, qn_ref, rq_ref, qh_ref)
        headnorm(seg(512, 1024), kg_ref, 1.0, kn_ref, rk_ref, kh_ref)
        v_ref[...] = seg(1024, 1536).astype(BF16)
        alin = seg(1536, 2048)
        agate = seg(2048, 2560)
        alin_ref[...] = alin.astype(BF16)
        agate_ref[...] = agate.astype(BF16)
        u0_ref[...] = alin * _sigmoid(agate)
        fgb_ref[...] = seg(2560, NP) + bf_ref[...]

    bf = lambda: _sds((S, AW), BF16)
    xs = ex.srcs if ex else []
    outs = pl.pallas_call(
        _hosted(body, ex, 9, 12, 0, S // tm), name="fwd_in", grid=(S // tm,),
        out_shape=(_sds((S, D), BF16), bf(), bf(), bf(), bf(), bf(), _sds((S, LANES), F32), _sds((S, LANES), F32),
                   _sds((S, LANES), F32), bf(), bf(), _sds((S, CW), F32), *(ex.out_shapes if ex else [])),
        in_specs=[_rows(tm, D), _const((8, D)), _const((1, D)), _const((D, NP)), _const((AW, LANES)), _const((LANES, AW)),
                  _const((1, AW)), _const((1, AW)), _const((1, LANES)), *(ex.in_specs if ex else [])],
        out_specs=(_rows(tm, D), _rows(tm, AW), _rows(tm, AW), _rows(tm, AW), _rows(tm, AW), _rows(tm, AW),
                   _rows(tm, LANES), _rows(tm, LANES), _rows(tm, LANES), _rows(tm, AW), _rows(tm, AW), _rows(tm, CW),
                   *(ex.out_specs if ex else [])),
        scratch_shapes=ex.scratch if ex else [],
        compiler_params=_cp(("arbitrary",)),
    )(x, mod8, n1g, w_in_p, e512, et512, qg512, kg512, bf128, *xs)
    return outs[:12], list(outs[12:])


def _split3(f):
    f1 = f.astype(BF16).astype(F32)
    f2 = (f - f1).astype(BF16).astype(F32)
    return f1, f2, f - f1 - f2


def _dot3(w, x):
    return sum(jnp.dot(w, piece.astype(BF16), preferred_element_type=F32) for piece in _split3(x))


def _fwd_decay(fgb, qh, kh, vb, shift, tm, ex=None):
    S = fgb.shape[0]

    def body(shift_ref, fgb_ref, qh_ref, kh_ref, vb_ref, qa_ref, ka_ref, va_ref, f_ref, carry_ref):
        @pl.when(pl.program_id(0) == 0)
        def _():
            carry_ref[...] = jnp.zeros_like(carry_ref)

        fb = fgb_ref[...]
        lf = jnp.minimum(fb, 0.0) - jnp.log1p(jnp.exp(-jnp.abs(fb)))
        tri = (lax.broadcasted_iota(jnp.int32, (tm, tm), 0) >= lax.broadcasted_iota(jnp.int32, (tm, tm), 1)
               ).astype(F32).astype(BF16)
        cs = _dot3(tri, lf) + carry_ref[0:1, :]
        f_ref[...] = cs
        carry_ref[...] = jnp.broadcast_to(cs[tm - 1:tm, :], carry_ref.shape)
        lane = _lane((tm, LANES))
        s1, s2, s3 = _split3(jnp.zeros((tm, LANES), F32) - shift_ref[0, 0])
        tail_q = jnp.where((lane >= 67) & (lane < 70), 1.0,
                           jnp.where(lane == 70, s1, jnp.where(lane == 71, s2, jnp.where(lane == 72, s3, 0.0))))
        tail_k = jnp.where(((lane >= 64) & (lane < 67)) | ((lane >= 70) & (lane < 73)), 1.0, 0.0)
        tail_v = jnp.where(lane == DH, 1.0, 0.0)
        for p in range(NH // 2):
            qp = qh_ref[:, p * LANES:(p + 1) * LANES].astype(F32)
            kp = kh_ref[:, p * LANES:(p + 1) * LANES].astype(F32)
            vp = vb_ref[:, p * LANES:(p + 1) * LANES].astype(F32)
            for hh in range(2):
                h = 2 * p + hh
                f1, f2, f3 = _split3(cs[:, h:h + 1])
                qb = qp if hh == 0 else pltpu.roll(qp, 64, 1)
                kb = kp if hh == 0 else pltpu.roll(kp, 64, 1)
                vh = vp if hh == 0 else pltpu.roll(vp, 64, 1)
                augq = jnp.where(lane == 64, f1, jnp.where(lane == 65, f2, jnp.where(lane == 66, f3, tail_q)))
                augk = jnp.where(lane == 67, -f1, jnp.where(lane == 68, -f2, jnp.where(lane == 69, -f3, tail_k)))
                qa_ref[h] = jnp.where(lane < DH, qb, augq).astype(BF16)
                ka_ref[h] = jnp.where(lane < DH, kb, augk).astype(BF16)
                va_ref[h] = jnp.where(lane < DH, vh, tail_v).astype(BF16)

    hm = pl.BlockSpec((NH, tm, LANES), lambda i: (0, i, 0))
    hms = _sds((NH, S, LANES), BF16)
    outs = pl.pallas_call(
        _hosted(body, ex, 5, 4, 1, S // tm), name="fwd_decay", grid=(S // tm,),
        out_shape=(hms, hms, hms, _sds((S, LANES), F32), *(ex.out_shapes if ex else [])),
        in_specs=[SMEM_SPEC, _rows(tm, LANES), _rows(tm, AW), _rows(tm, AW), _rows(tm, AW),
                  *(ex.in_specs if ex else [])],
        out_specs=(hm, hm, hm, _rows(tm, LANES), *(ex.out_specs if ex else [])),
        scratch_shapes=[pltpu.VMEM((8, LANES), F32), *(ex.scratch if ex else [])],
        compiler_params=_cp(("arbitrary",)),
    )(shift, fgb, qh, kh, vb, *(ex.srcs if ex else []))
    return outs[:4], list(outs[4:])


SKIP = -106.0
SKIP_P = -88.0


def _block_loops(first, step, needed, run):
    def both(j):
        return jnp.logical_and(needed(0, j), needed(1, j))

    def walk(heads):
        def go(j):
            run(j, heads)
            return j + step
        return go

    j = lax.while_loop(both, walk((0, 1)), first)
    lax.while_loop(functools.partial(needed, 0), walk((0,)), j)
    lax.while_loop(functools.partial(needed, 1), walk((1,)), j)


def _logit_bound(qg, kg):
    return (2.0 * 1.03 * DH ** 0.5 * jnp.max(jnp.abs(qg)) * jnp.max(jnp.abs(kg))).reshape(1, 1)


def _skip_tables(f, tq):
    return f[0::tq, :NH].T, f[tq - 1::tq, :NH].T


SMEM_SPEC = pl.BlockSpec(memory_space=pltpu.SMEM)


def _causal_rect(rows, cols, col0):
    return (lax.broadcasted_iota(jnp.int32, (rows, cols), 0)
            >= lax.broadcasted_iota(jnp.int32, (rows, cols), 1) + col0)


SHIFT_MAX = 60.0


def _shift(bd):
    return jnp.where(bd <= SHIFT_MAX, 0.5 * bd, 0.0)


def _attn_fwd(qa, ka, va, fs, fe, bd, tq):
    S = qa.shape[1]
    nq = S // tq

    def body(fs_ref, fe_ref, bd_ref, qa_ref, ka_ref, va_ref, o_ref, lser_ref, m_ref, acc_ref):
        pr, i = pl.program_id(0), pl.program_id(1)
        sel_a = _lane((tq, LANES)) < DH
        acc_ref[...] = jnp.zeros_like(acc_ref)

        def logits(j, hh, masked):
            start = pl.multiple_of(j * tq, tq)
            s = lax.dot_general(qa_ref[hh], ka_ref[hh, pl.ds(start, tq), :], NT, preferred_element_type=F32)
            if masked:
                s = jnp.where(_causal_rect(tq, tq, 0), s, NEG)
            return s, va_ref[hh, pl.ds(start, tq), :]

        def shifted_step(j, heads, masked=False):
            for hh in heads:
                s, vb = logits(j, hh, masked)
                acc_ref[hh] += jnp.dot(jnp.exp(s).astype(BF16), vb, preferred_element_type=F32)

        def online_step(j, heads, masked=False):
            for hh in heads:
                s, vb = logits(j, hh, masked)
                m_prev = m_ref[hh]
                m_new = jnp.maximum(m_prev, jnp.max(s, axis=1, keepdims=True))
                p = jnp.exp(s - jnp.tile(m_new, (1, tq // LANES)))
                m_ref[hh] = m_new
                acc_ref[hh] = jnp.exp(m_prev - m_new) * acc_ref[hh] + jnp.dot(p.astype(BF16), vb,
                                                                              preferred_element_type=F32)

        def needed(slack, hh, j):
            top = fs_ref[2 * pr + hh, i] + slack
            return jnp.logical_and(j >= 0, top - fe_ref[2 * pr + hh, jnp.maximum(j, 0)] >= SKIP)

        @pl.when(bd_ref[0, 0] <= SHIFT_MAX)
        def _():
            m_ref[...] = jnp.zeros_like(m_ref)
            shifted_step(i, (0, 1), masked=True)
            _block_loops(i - 1, -1, functools.partial(needed, 0.0), shifted_step)

        @pl.when(bd_ref[0, 0] > SHIFT_MAX)
        def _():
            m_ref[...] = jnp.full(m_ref.shape, NEG, F32)
            online_step(i, (0, 1), masked=True)
            _block_loops(i - 1, -1, functools.partial(needed, bd_ref[0, 0]), online_step)

        outs, lses = [], []
        for hh in range(2):
            acc = acc_ref[hh]
            row_sum = jnp.broadcast_to(acc[:, DH:DH + 1], (tq, LANES))
            outs.append(acc / row_sum)
            lses.append(m_ref[hh] + jnp.log(row_sum))
        o_ref[...] = jnp.where(sel_a, outs[0], pltpu.roll(outs[1], 64, 1))
        row = lax.broadcasted_iota(jnp.int32, (8, tq), 0)
        lser_ref[0, 0] = jnp.where(row == 0, lses[0].T[0:8, :], lses[1].T[0:8, :])

    return pl.pallas_call(
        body, name="attn_fwd", grid=(NH // 2, nq),
        out_shape=(_sds((S, AW), F32), _sds((NH // 2, nq, 8, tq), F32)),
        in_specs=[SMEM_SPEC, SMEM_SPEC, SMEM_SPEC,
                  pl.BlockSpec((2, tq, LANES), lambda p, i: (p, i, 0)),
                  pl.BlockSpec((2, S, LANES), lambda p, i: (p, 0, 0)),
                  pl.BlockSpec((2, S, LANES), lambda p, i: (p, 0, 0))],
        out_specs=(pl.BlockSpec((tq, LANES), lambda p, i: (i, p)),
                   pl.BlockSpec((1, 1, 8, tq), lambda p, i: (p, i, 0, 0))),
        scratch_shapes=[pltpu.VMEM((2, tq, LANES), F32), pltpu.VMEM((2, tq, LANES), F32)],
        compiler_params=_cp(("parallel", "parallel")),
    )(fs, fe, bd, qa, ka, va)


HALO = 32
FWD_CHUNK = 64
BWD_CHUNK = 32


def _halo_prev(tm):
    return pl.BlockSpec((HALO, CW), lambda i: (jnp.maximum(i * (tm // HALO) - 1, 0), 0))


SUBLANES = 8
SHIFT_ROWS = 24


def _shifted_copies(ext_ref, sh_ref, tm):
    for k in range(1, SUBLANES):
        sh_ref[k - 1, 0:tm + SHIFT_ROWS, :] = ext_ref[k:k + tm + SHIFT_ROWS, :]


def _tap_windows(ext_ref, sh_ref, r0, offset, rows, shared):
    for k in range(SUBLANES):
        taps = sorted((offset(j), j) for j in range(KC) if offset(j) % SUBLANES == k)
        for group in ([taps] if shared and taps else [[t] for t in taps]):
            lo, hi = r0 + group[0][0] - k, r0 + group[-1][0] - k + rows
            window = ext_ref[lo:hi, :] if k == 0 else sh_ref[k - 1, lo:hi, :]
            for off, j in group:
                at = r0 + off - k - lo
                yield j, window[at:at + rows, :]


def _fwd_conv(u0, w32, cb, lng, lnb, beta_c, tm, ex=None):
    S = u0.shape[0]

    def body(cur_ref, prev_ref, w_ref, cb_ref, lng_ref, lnb_ref, beta_ref, u1_ref, mc_ref, ext_ref, sh_ref):
        i = pl.program_id(0)
        ext_ref[0:HALO, :] = jnp.where(i == 0, 0.0, prev_ref[...])
        ext_ref[HALO:, :] = cur_ref[...]
        _shifted_copies(ext_ref, sh_ref, tm)
        for r0 in range(0, tm, FWD_CHUNK):
            acc = jnp.zeros((FWD_CHUNK, CW), F32) + cb_ref[...]
            for j, rows in _tap_windows(ext_ref, sh_ref, r0, lambda j: 2 + j, FWD_CHUNK, shared=False):
                acc = acc + w_ref[j:j + 1, :] * rows
            u1_ref[r0:r0 + FWD_CHUNK, :] = acc
        u1 = u1_ref[...]
        mu = jnp.mean(u1, axis=-1, keepdims=True)
        d = u1 - mu
        rstd = lax.rsqrt(jnp.mean(d * d, axis=-1, keepdims=True) + EPS)
        u2 = d * rstd * lng_ref[...] + lnb_ref[...]
        u3 = u2 * _sigmoid(u2)
        rc = lax.rsqrt(jnp.mean(u3 * u3, axis=-1, keepdims=True) + EPS)
        mc_ref[...] = (u3 * rc * beta_ref[...]).astype(BF16)

    outs = pl.pallas_call(
        _hosted(body, ex, 7, 2, 2, S // tm), name="fwd_conv", grid=(S // tm,),
        out_shape=(_sds((S, CW), F32), _sds((S, CW), BF16), *(ex.out_shapes if ex else [])),
        in_specs=[_rows(tm, CW), _halo_prev(tm), _const((HALO, CW)), _const((1, CW)), _const((1, CW)), _const((1, CW)),
                  _const((1, CW)), *(ex.in_specs if ex else [])],
        out_specs=(_rows(tm, CW), _rows(tm, CW), *(ex.out_specs if ex else [])),
        scratch_shapes=[pltpu.VMEM((tm + HALO, CW), F32), pltpu.VMEM((SUBLANES - 1, tm + HALO, CW), F32),
                        *(ex.scratch if ex else [])],
        compiler_params=_cp(("arbitrary",)),
    )(u0, u0, w32, cb, lng, lnb, beta_c, *(ex.srcs if ex else []))
    return outs[:2], list(outs[2:])


def _fwd_out(o_attn, mc, x, mod8, n2g, beta_a, w_out, tm):
    S = x.shape[0]

    def body(o_ref, mc_ref, x_ref, mod_ref, n2g_ref, beta_ref, w_ref, mg_ref, ob_ref, x2_ref, h2_ref):
        ov = o_ref[...]
        ra = lax.rsqrt(jnp.mean(ov * ov, axis=-1, keepdims=True) + EPS)
        ma = (ov * ra * beta_ref[...]).astype(BF16)
        mcv = mc_ref[...]
        mg_ref[:, 0:AW] = ma
        mg_ref[:, AW:D] = mcv
        o = (jnp.dot(ma, w_ref[0:AW, :], preferred_element_type=F32)
             + jnp.dot(mcv, w_ref[AW:D, :], preferred_element_type=F32))
        ob_ref[...] = o.astype(BF16)
        x2 = x_ref[...] + mod_ref[2:3, :] * o
        x2_ref[...] = x2
        r2 = lax.rsqrt(jnp.mean(x2 * x2, axis=-1, keepdims=True) + EPS)
        h2_ref[...] = ((x2 * r2) * (n2g_ref[...] * (1.0 + mod_ref[4:5, :])) + mod_ref[3:4, :]).astype(BF16)

    return pl.pallas_call(
        body, name="fwd_out", grid=(S // tm,),
        out_shape=(_sds((S, D), BF16), _sds((S, D), BF16), _sds((S, D), F32), _sds((S, D), BF16)),
        in_specs=[_rows(tm, AW), _rows(tm, CW), _rows(tm, D), _const((8, D)), _const((1, D)), _const((1, AW)),
                  _const((D, D))],
        out_specs=(_rows(tm, D), _rows(tm, D), _rows(tm, D), _rows(tm, D)),
        compiler_params=_cp(("parallel",)),
    )(o_attn, mc, x, mod8, n2g, beta_a, w_out)


def _fwd_ffn(h2, w1, w2, x2, tgt, mod8, tm):
    S = h2.shape[0]
    nk = w1.shape[0]
    bf = w1.shape[2]

    def body(h2_ref, w1_ref, w2_ref, x2_ref, tgt_ref, mod_ref, r_ref, dy_ref, loss_ref, dg2_ref):
        @pl.when(pl.program_id(0) == 0)
        def _():
            loss_ref[...] = jnp.zeros_like(loss_ref)
            dg2_ref[...] = jnp.zeros_like(dg2_ref)

        f2 = None
        for k in range(nk):
            r = jnp.maximum(jnp.dot(h2_ref[...], w1_ref[k], preferred_element_type=F32), 0.0)
            r_ref[:, k * bf:(k + 1) * bf] = r.astype(BF16)
            part = jnp.dot((r * r).astype(BF16), w2_ref[k], preferred_element_type=F32)
            f2 = part if f2 is None else f2 + part
        e = x2_ref[...] + mod_ref[5:6, :] * f2 - tgt_ref[...]
        dy = e * (1.0 / D)
        dy_ref[...] = dy
        loss_ref[...] += 0.5 * jnp.sum(jnp.sum(e * dy, axis=1, keepdims=True), axis=0, keepdims=True)
        dg2_ref[...] += jnp.sum((dy * f2).reshape(tm // 8, 8, D), axis=0)

    once = pl.Buffered(1)
    return pl.pallas_call(
        body, name="fwd_ffn", grid=(S // tm,),
        out_shape=(_sds((S, DFF), BF16), _sds((S, D), F32), _sds((8, LANES), F32), _sds((8, D), F32)),
        in_specs=[_rows(tm, D), pl.BlockSpec((nk, D, bf), lambda i: (0, 0, 0), pipeline_mode=once),
                  pl.BlockSpec((nk, bf, D), lambda i: (0, 0, 0), pipeline_mode=once), _rows(tm, D), _rows(tm, D),
                  _const((8, D))],
        out_specs=(_rows(tm, DFF), _rows(tm, D), _const((8, LANES)), _const((8, D))),
        compiler_params=_cp(("arbitrary",)),
    )(h2, w1, w2, x2, tgt, mod8)


def _bwd_ffn(dy, mod8, r, w1, w2, tm):
    S = dy.shape[0]
    nk = w1.shape[0]
    bf = w1.shape[2]

    def body(dy_ref, mod_ref, r_ref, w1_ref, w2_ref, df2_ref, df1_ref, dh2_ref):
        df2 = (dy_ref[...] * mod_ref[5:6, :]).astype(BF16)
        df2_ref[...] = df2
        dh2 = None
        for k in range(nk):
            da = lax.dot_general(df2, w2_ref[k], NT, preferred_element_type=F32)
            df1 = (da * (2.0 * r_ref[:, k * bf:(k + 1) * bf].astype(F32))).astype(BF16)
            df1_ref[:, k * bf:(k + 1) * bf] = df1
            part = lax.dot_general(df1, w1_ref[k], NT, preferred_element_type=F32)
            dh2 = part if dh2 is None else dh2 + part
        dh2_ref[...] = dh2

    once = pl.Buffered(1)
    return pl.pallas_call(
        body, name="bwd_ffn", grid=(S // tm,),
        out_shape=(_sds((S, D), BF16), _sds((S, DFF), BF16), _sds((S, D), F32)),
        in_specs=[_rows(tm, D), _const((8, D)), _rows(tm, DFF),
                  pl.BlockSpec((nk, D, bf), lambda i: (0, 0, 0), pipeline_mode=once),
                  pl.BlockSpec((nk, bf, D), lambda i: (0, 0, 0), pipeline_mode=once)],
        out_specs=(_rows(tm, D), _rows(tm, DFF), _rows(tm, D)),
        compiler_params=_cp(("parallel",)),
    )(dy, mod8, r, w1, w2)


def _token_tile(S, want):
    while S % want:
        want //= 2
    return want


def _wgrad(a, b, name, square_a=False, col_pieces=False, tk=2048, bm=1024, bn=1024):
    S, M = a.shape
    N = b.shape[1]
    bm, bn, tk = min(bm, M), min(bn, N), _token_tile(S, tk)
    nk = S // tk

    def body(a_ref, b_ref, o_ref, acc_ref):
        av = a_ref[...]
        if square_a:
            af = av.astype(F32)
            av = (af * af).astype(BF16)
        part = lax.dot_general(av, b_ref[...], TN, preferred_element_type=F32)

        @pl.when(pl.program_id(2) == 0)
        def _():
            acc_ref[...] = part

        @pl.when(pl.program_id(2) > 0)
        def _():
            acc_ref[...] += part

        @pl.when(pl.program_id(2) == nk - 1)
        def _():
            if col_pieces:
                o_ref[0] = acc_ref[...].astype(BF16)
            else:
                o_ref[...] = acc_ref[...].astype(BF16)

    if col_pieces:
        out_shape, out_spec = _sds((N // bn, M, bn), BF16), pl.BlockSpec((1, bm, bn), lambda mi, ni, k: (ni, mi, 0))
    else:
        out_shape, out_spec = _sds((M, N), BF16), pl.BlockSpec((bm, bn), lambda mi, ni, k: (mi, ni))
    return pl.pallas_call(
        body, name=name, grid=(M // bm, N // bn, nk), out_shape=out_shape,
        in_specs=[pl.BlockSpec((tk, bm), lambda mi, ni, k: (k, mi)), pl.BlockSpec((tk, bn), lambda mi, ni, k: (k, ni))],
        out_specs=out_spec, scratch_shapes=[pltpu.VMEM((bm, bn), F32)],
        compiler_params=_cp(("parallel", "parallel", "arbitrary")),
    )(a, b)


def _wgrad_in(h1, pieces, tk=1024):
    S = h1.shape[0]
    tk = _token_tile(S, tk)
    widths = [p.shape[1] for p in pieces]
    offs = [sum(widths[:t]) for t in range(len(widths))]

    def body(a_ref, *refs):
        o_ref, acc_ref = refs[-2:]

        @pl.when(pl.program_id(0) == 0)
        def _():
            acc_ref[...] = jnp.zeros_like(acc_ref)

        for b_ref, off, w in zip(refs[:-2], offs, widths):
            acc_ref[:, off:off + w] += lax.dot_general(a_ref[...], b_ref[...], TN, preferred_element_type=F32)

        @pl.when(pl.program_id(0) == S // tk - 1)
        def _():
            o_ref[...] = acc_ref[...].astype(BF16)

    return pl.pallas_call(
        body, name="wgrad_in", grid=(S // tk,), out_shape=_sds((D, NP), BF16),
        in_specs=[_rows(tk, D)] + [_rows(tk, w) for w in widths], out_specs=_const((D, NP)),
        scratch_shapes=[pltpu.VMEM((D, NP), F32)], compiler_params=_cp(("arbitrary",)),
    )(h1, *pieces)


def _colsum8(t):
    return jnp.sum(t.reshape(t.shape[0] // 8, 8, t.shape[1]), axis=0)


def _bwd_mid(dh2, dy, x2, ob, o_attn, u1, mod8, n2g, beta_a, beta_c, lng, lnb, w_out, e512, tm, tq, ex=None):
    S = dy.shape[0]

    def body(dh2_ref, dy_ref, x2_ref, ob_ref, oa_ref, u1_ref, mod_ref, n2g_ref, ba_ref, bc_ref, lng_ref, lnb_ref, w_ref,
             e_ref, dx2_ref, do_ref, doa_ref, du1_ref, acc_d_ref, acc_h_ref, dr_ref):
        @pl.when(pl.program_id(0) == 0)
        def _():
            acc_d_ref[...] = jnp.zeros_like(acc_d_ref)
            acc_h_ref[...] = jnp.zeros_like(acc_h_ref)

        x2 = x2_ref[...]
        dh2 = dh2_ref[...]
        r2 = lax.rsqrt(jnp.mean(x2 * x2, axis=-1, keepdims=True) + EPS)
        xn2 = x2 * r2
        gain = n2g_ref[...] * (1.0 + mod_ref[4:5, :])
        dxn = dh2 * gain
        dx2 = dy_ref[...] + r2 * (dxn - xn2 * jnp.mean(dxn * xn2, axis=-1, keepdims=True))
        dx2_ref[...] = dx2
        t = dh2 * xn2
        acc_d_ref[0] += _colsum8(dh2)
        acc_d_ref[1] += _colsum8(t * n2g_ref[...])
        acc_d_ref[2] += _colsum8(t * (1.0 + mod_ref[4:5, :]))
        acc_d_ref[3] += _colsum8(dx2 * ob_ref[...].astype(F32))
        do = (dx2 * mod_ref[2:3, :]).astype(BF16)
        do_ref[...] = do
        dma = lax.dot_general(do, w_ref[0:AW, :], NT, preferred_element_type=F32)
        dmc = lax.dot_general(do, w_ref[AW:D, :], NT, preferred_element_type=F32)
        ov = oa_ref[...]
        ra = lax.rsqrt(jnp.mean(ov * ov, axis=-1, keepdims=True) + EPS)
        on = ov * ra
        acc_h_ref[0] += _colsum8(dma * on)
        don = dma * ba_ref[...]
        doa = (ra * (don - on * jnp.mean(don * on, axis=-1, keepdims=True))).astype(BF16)
        doa_ref[...] = doa
        delta_t = _dot2(doa.astype(F32) * ov, e_ref[...]).T
        for p in range(NH // 2):
            dr_ref[p, 0] = delta_t[2 * p:2 * p + 8, :]
        u1 = u1_ref[...]
        mu = jnp.mean(u1, axis=-1, keepdims=True)
        d = u1 - mu
        rstd = lax.rsqrt(jnp.mean(d * d, axis=-1, keepdims=True) + EPS)
        uh = d * rstd
        u2 = uh * lng_ref[...] + lnb_ref[...]
        sg = _sigmoid(u2)
        u3 = u2 * sg
        rc = lax.rsqrt(jnp.mean(u3 * u3, axis=-1, keepdims=True) + EPS)
        u3n = u3 * rc
        acc_h_ref[1] += _colsum8(dmc * u3n)
        du3n = dmc * bc_ref[...]
        du3 = rc * (du3n - u3n * jnp.mean(du3n * u3n, axis=-1, keepdims=True))
        du2 = du3 * (sg * (1.0 + u2 * (1.0 - sg)))
        acc_h_ref[2] += _colsum8(du2 * uh)
        acc_h_ref[3] += _colsum8(du2)
        duh = du2 * lng_ref[...]
        du1_ref[...] = rstd * (duh - jnp.mean(duh, axis=-1, keepdims=True)
                               - uh * jnp.mean(duh * uh, axis=-1, keepdims=True))

    per = tq // tm
    outs = pl.pallas_call(
        _hosted(body, ex, 14, 7, 0, S // tm), name="bwd_mid", grid=(S // tm,),
        out_shape=(_sds((S, D), F32), _sds((S, D), BF16), _sds((S, AW), BF16), _sds((S, CW), F32),
                   _sds((4, 8, D), F32), _sds((4, 8, AW), F32), _sds((NH // 2, S // tq, 8, tq), F32),
                   *(ex.out_shapes if ex else [])),
        in_specs=[_rows(tm, D), _rows(tm, D), _rows(tm, D), _rows(tm, D), _rows(tm, AW), _rows(tm, CW), _const((8, D)),
                  _const((1, D)), _const((1, AW)), _const((1, CW)), _const((1, CW)), _const((1, CW)), _const((D, D)),
                  _const((AW, LANES)), *(ex.in_specs if ex else [])],
        out_specs=(_rows(tm, D), _rows(tm, D), _rows(tm, AW), _rows(tm, CW), _const((4, 8, D)), _const((4, 8, AW)),
                   pl.BlockSpec((NH // 2, 1, 8, tm), lambda i: (0, i // per, 0, i % per)),
                   *(ex.out_specs if ex else [])),
        scratch_shapes=ex.scratch if ex else [],
        compiler_params=_cp(("arbitrary",)),
    )(dh2, dy, x2, ob, o_attn, u1, mod8, n2g, beta_a, beta_c, lng, lnb, w_out, e512, *(ex.srcs if ex else []))
    return outs[:7], list(outs[7:])


def _attn_bwd(qa, ka, v, do, lser, dr, fs, fe, bd, tq):
    S = qa.shape[1]
    nq = S // tq

    def body(fs_ref, fe_ref, bd_ref, ka_ref, v_ref, qa_ref, do_ref, lse_ref, dr_ref, dqa_hbm, dka_ref, dv_ref,
             accq_ref, acck_ref, accv_ref, out_sem):
        pr, j = pl.program_id(0), pl.program_id(1)
        sel_a = _lane((tq, LANES)) < DH
        vv = v_ref[...]
        zb = jnp.zeros_like(vv)
        vs = [jnp.where(sel_a, vv, zb), jnp.where(sel_a, zb, vv)]
        acck_ref[...] = jnp.zeros_like(acck_ref)
        accv_ref[...] = jnp.zeros_like(accv_ref)

        @pl.when(j == 0)
        def _():
            accq_ref[...] = jnp.zeros_like(accq_ref)

        def q_step(i, heads, masked=False):
            start = pl.multiple_of(i * tq, tq)
            dob = do_ref[pl.ds(start, tq), :]
            lse8 = lse_ref[0, i]
            dr8 = dr_ref[0, i]
            for hh in heads:
                qb = qa_ref[hh, pl.ds(start, tq), :]
                kb = ka_ref[hh]
                st = lax.dot_general(kb, qb, NT, preferred_element_type=F32)
                pt = jnp.exp(st - lse8[hh:hh + 1, :])
                if masked:
                    keep = (lax.broadcasted_iota(jnp.int32, (tq, tq), 0)
                            <= lax.broadcasted_iota(jnp.int32, (tq, tq), 1))
                    pt = jnp.where(keep, pt, 0.0)
                accv_ref[hh] += jnp.dot(pt.astype(BF16), dob, preferred_element_type=F32)
                dpt = lax.dot_general(vs[hh], dob, NT, preferred_element_type=F32)
                dst = (pt * (dpt - dr8[hh:hh + 1, :])).astype(BF16)
                acck_ref[hh] += jnp.dot(dst, qb, preferred_element_type=F32)
                accq_ref[hh, pl.ds(start, tq), :] += lax.dot_general(dst, kb, TN, preferred_element_type=F32)

        def needed(hh, i):
            top = fs_ref[2 * pr + hh, jnp.minimum(i, nq - 1)] + bd_ref[0, 0]
            return jnp.logical_and(i < nq, top - fe_ref[2 * pr + hh, j] >= SKIP_P)

        q_step(j, (0, 1), masked=True)
        _block_loops(j + 1, 1, needed, q_step)
        dka_ref[...] = acck_ref[...]
        dv_ref[...] = jnp.where(sel_a, accv_ref[0], accv_ref[1]).astype(BF16)

        @pl.when(j == nq - 1)
        def _():
            out = pltpu.make_async_copy(accq_ref, dqa_hbm.at[pl.ds(2 * pr, 2)], out_sem)
            out.start()
            out.wait()

    once = pl.Buffered(1)
    return pl.pallas_call(
        body, name="attn_bwd", grid=(NH // 2, nq),
        out_shape=(_sds((NH, S, LANES), F32), _sds((NH, S, LANES), F32), _sds((S, AW), BF16)),
        in_specs=[SMEM_SPEC, SMEM_SPEC, SMEM_SPEC,
                  pl.BlockSpec((2, tq, LANES), lambda p, j: (p, j, 0)),
                  pl.BlockSpec((tq, LANES), lambda p, j: (j, p)),
                  pl.BlockSpec((2, S, LANES), lambda p, j: (p, 0, 0), pipeline_mode=once),
                  pl.BlockSpec((S, LANES), lambda p, j: (0, p), pipeline_mode=once),
                  pl.BlockSpec((1, nq, 8, tq), lambda p, j: (p, 0, 0, 0)),
                  pl.BlockSpec((1, nq, 8, tq), lambda p, j: (p, 0, 0, 0))],
        out_specs=(pl.BlockSpec(memory_space=pl.ANY),
                   pl.BlockSpec((2, tq, LANES), lambda p, j: (p, j, 0)),
                   pl.BlockSpec((tq, LANES), lambda p, j: (j, p))),
        scratch_shapes=[pltpu.VMEM((2, S, LANES), F32), pltpu.VMEM((2, tq, LANES), F32),
                        pltpu.VMEM((2, tq, LANES), F32), pltpu.SemaphoreType.DMA],
        compiler_params=_cp(("arbitrary", "arbitrary")),
    )(fs, fe, bd, ka, v, qa, do, lser, dr)


def _bwd_conv(du1, u0, alin, agate, w32, tm, ex=None):
    S = du1.shape[0]
    nt = S // tm

    def body(du_ref, dun_ref, u0_ref, u0p_ref, alin_ref, agate_ref, w_ref,
             dalin_ref, dagate_ref, dw_ref, db_ref, extd_ref, extu_ref, du0_ref, shd_ref, shu_ref):
        i = pl.program_id(0)

        @pl.when(i == 0)
        def _():
            dw_ref[...] = jnp.zeros_like(dw_ref)
            db_ref[...] = jnp.zeros_like(db_ref)

        extd_ref[0:tm, :] = du_ref[...]
        extd_ref[tm:, :] = jnp.where(i == nt - 1, 0.0, dun_ref[...])
        extu_ref[0:HALO, :] = jnp.where(i == 0, 0.0, u0p_ref[...])
        extu_ref[HALO:, :] = u0_ref[...]
        _shifted_copies(extd_ref, shd_ref, tm)
        _shifted_copies(extu_ref, shu_ref, tm)
        db_ref[...] += _colsum8(du_ref[...])
        for r0 in range(0, tm, BWD_CHUNK):
            acc = jnp.zeros((BWD_CHUNK, CW), F32)
            for j, rows in _tap_windows(extd_ref, shd_ref, r0, lambda j: 30 - j, BWD_CHUNK, shared=True):
                acc = acc + w_ref[j:j + 1, :] * rows
            du0_ref[r0:r0 + BWD_CHUNK, :] = acc
            duc = du_ref[r0:r0 + BWD_CHUNK, :]
            for j, rows in _tap_windows(extu_ref, shu_ref, r0, lambda j: 2 + j, BWD_CHUNK, shared=True):
                dw_ref[j] += _colsum8(duc * rows)
        du0 = du0_ref[...]
        al = alin_ref[...].astype(F32)
        sg = _sigmoid(agate_ref[...].astype(F32))
        dalin_ref[...] = (du0 * sg).astype(BF16)
        dagate_ref[...] = (du0 * al * sg * (1.0 - sg)).astype(BF16)

    nxt = pl.BlockSpec((HALO, CW), lambda i: (jnp.minimum((i + 1) * (tm // HALO), S // HALO - 1), 0))
    outs = pl.pallas_call(
        _hosted(body, ex, 7, 4, 5, nt), name="bwd_conv", grid=(nt,),
        out_shape=(_sds((S, CW), BF16), _sds((S, CW), BF16), _sds((HALO, 8, CW), F32), _sds((8, CW), F32),
                   *(ex.out_shapes if ex else [])),
        in_specs=[_rows(tm, CW), nxt, _rows(tm, CW), _halo_prev(tm), _rows(tm, CW), _rows(tm, CW), _const((HALO, CW)),
                  *(ex.in_specs if ex else [])],
        out_specs=(_rows(tm, CW), _rows(tm, CW), _const((HALO, 8, CW)), _const((8, CW)), *(ex.out_specs if ex else [])),
        scratch_shapes=[pltpu.VMEM((tm + HALO, CW), F32), pltpu.VMEM((tm + HALO, CW), F32), pltpu.VMEM((tm, CW), F32),
                        pltpu.VMEM((SUBLANES - 1, tm + HALO, CW), F32), pltpu.VMEM((SUBLANES - 1, tm + HALO, CW), F32),
                        *(ex.scratch if ex else [])],
        compiler_params=_cp(("arbitrary",)),
    )(du1, du1, u0, u0, alin, agate, w32, *(ex.srcs if ex else []))
    return outs[:4], list(outs[4:])


def _bwd_qk(dqa, dka, qn, kn, rq, rk, fgb, qg512, kg512, e512, et512, tm):
    S = qn.shape[0]
    nt = S // tm

    def body(dqa_ref, dka_ref, qn_ref, kn_ref, rq_ref, rk_ref, fgb_ref, qg_ref, kg_ref, e_ref, et_ref,
             dq_ref, dk_ref, dfg_ref, accg_ref, accb_ref, carry_ref):
        @pl.when(pl.program_id(0) == 0)
        def _():
            carry_ref[...] = jnp.zeros_like(carry_ref)
            accg_ref[...] = jnp.zeros_like(accg_ref)
            accb_ref[...] = jnp.zeros_like(accb_ref)

        lane = _lane((tm, LANES))
        sel_a = lane < DH
        df = jnp.zeros((tm, LANES), F32)
        for h in range(NH):
            col = dqa_ref[h][:, 64:65] - dka_ref[h][:, 67:68]
            df = jnp.where(lane == h, col, df)
        tri = (lax.broadcasted_iota(jnp.int32, (tm, tm), 0) <= lax.broadcasted_iota(jnp.int32, (tm, tm), 1)
               ).astype(F32).astype(BF16)
        dlf = _dot3(tri, df) + carry_ref[0:1, :]
        carry_ref[...] = jnp.broadcast_to(dlf[0:1, :], carry_ref.shape)
        dfg = jnp.where(lane < NH, dlf * _sigmoid(-fgb_ref[...]), 0.0)
        dfg_ref[...] = dfg.astype(BF16)
        accb_ref[...] += _colsum8(dfg)

        def norm_bwd(src_ref, n_ref, r_ref, g_ref, scale, slot):
            pairs = []
            for p in range(NH // 2):
                b = pltpu.roll(src_ref[2 * p + 1], 64, 1)
                pairs.append(jnp.where(sel_a, src_ref[2 * p], b))
            dh = jnp.concatenate(pairs, axis=1) * scale
            tn = n_ref[...].astype(F32)
            accg_ref[slot] += _colsum8(dh * tn)
            dn = dh * g_ref[...]
            mean = _dot2(dn * tn, e_ref[...]) * (1.0 / DH)
            corr = _dot2(mean, et_ref[...])
            rf = _dot2(r_ref[...], et_ref[...])
            return (rf * (dn - tn * corr)).astype(BF16)

        dq_ref[...] = norm_bwd(dqa_ref, qn_ref, rq_ref, qg_ref, DH ** -0.5, 0)
        dk_ref[...] = norm_bwd(dka_ref, kn_ref, rk_ref, kg_ref, 1.0, 1)

    rev = lambda n: pl.BlockSpec((tm, n), lambda i: (nt - 1 - i, 0))
    hm = pl.BlockSpec((NH, tm, LANES), lambda i: (0, nt - 1 - i, 0))
    dq, dk, dfg, accg, accb = pl.pallas_call(
        body, name="bwd_qk", grid=(nt,),
        out_shape=(_sds((S, AW), BF16), _sds((S, AW), BF16), _sds((S, LANES), BF16), _sds((2, 8, AW), F32),
                   _sds((8, LANES), F32)),
        in_specs=[hm, hm, rev(AW), rev(AW), rev(LANES), rev(LANES), rev(LANES), _const((1, AW)), _const((1, AW)),
                  _const((AW, LANES)), _const((LANES, AW))],
        out_specs=(rev(AW), rev(AW), rev(LANES), _const((2, 8, AW)), _const((8, LANES))),
        scratch_shapes=[pltpu.VMEM((8, LANES), F32)], compiler_params=_cp(("arbitrary",)),
    )(dqa, dka, qn, kn, rq, rk, fgb, qg512, kg512, e512, et512)
    return dq, dk, dfg, accg, accb


def _bwd_in(dq, dk, dv, dalin, dagate, dfg, w_in_p, x, dx2, mod8, n1g, tm, ex=None):
    S = x.shape[0]

    def body(dq_ref, dk_ref, dv_ref, dal_ref, dag_ref, dfg_ref, w_ref, x_ref, dx2_ref, mod_ref, n1g_ref,
             dx_ref, acc_ref):
        @pl.when(pl.program_id(0) == 0)
        def _():
            acc_ref[...] = jnp.zeros_like(acc_ref)

        def part(ref, a, b):
            return lax.dot_general(ref[...], w_ref[:, a:b], NT, preferred_element_type=F32)

        dh = (part(dq_ref, 0, 512) + part(dk_ref, 512, 1024) + part(dv_ref, 1024, 1536) + part(dal_ref, 1536, 2048)
              + part(dag_ref, 2048, 2560) + part(dfg_ref, 2560, NP))
        xv = x_ref[...]
        r1 = lax.rsqrt(jnp.mean(xv * xv, axis=-1, keepdims=True) + EPS)
        xn = xv * r1
        gain = n1g_ref[...] * (1.0 + mod_ref[1:2, :])
        t = dh * xn
        acc_ref[0] += _colsum8(dh)
        acc_ref[1] += _colsum8(t * n1g_ref[...])
        acc_ref[2] += _colsum8(t * (1.0 + mod_ref[1:2, :]))
        dxn = dh * gain
        dx_ref[...] = dx2_ref[...] + r1 * (dxn - xn * jnp.mean(dxn * xn, axis=-1, keepdims=True))

    outs = pl.pallas_call(
        _hosted(body, ex, 11, 2, 0, S // tm), name="bwd_in", grid=(S // tm,),
        out_shape=(_sds((S, D), F32), _sds((3, 8, D), F32), *(ex.out_shapes if ex else [])),
        in_specs=[_rows(tm, AW), _rows(tm, AW), _rows(tm, AW), _rows(tm, CW), _rows(tm, CW), _rows(tm, LANES),
                  _const((D, NP)), _rows(tm, D), _rows(tm, D), _const((8, D)), _const((1, D)),
                  *(ex.in_specs if ex else [])],
        out_specs=(_rows(tm, D), _const((3, 8, D)), *(ex.out_specs if ex else [])),
        scratch_shapes=ex.scratch if ex else [],
        compiler_params=_cp(("arbitrary",)),
    )(dq, dk, dv, dalin, dagate, dfg, w_in_p, x, dx2, mod8, n1g, *(ex.srcs if ex else []))
    return outs[:2], list(outs[2:])


def _adam(w, g, m, v):
    m_new = B1 * m + (1.0 - B1) * g
    v_new = B2 * v + (1.0 - B2) * (g * g)
    m_hat = m_new / (1.0 - B1 ** STEP)
    v_hat = v_new / (1.0 - B2 ** STEP)
    delta = -LR * (m_hat / (jnp.sqrt(v_hat) + AEPS) + WD * w)
    return delta, m_new, v_new


def _pair_adamw(slots, w, m, v, name, tr=256):
    ns, R, C = slots.shape
    tr = tr if R % tr == 0 else R
    nt = R // tr

    def body(s_ref, w_ref, m_ref, v_ref, g_ref, d_ref, mo_ref, vo_ref, mine_ref, theirs_ref, send_sems, recv_sems):
        i = pl.program_id(0)
        part = s_ref[0].astype(F32)
        for k in range(1, ns):
            part = part + s_ref[k].astype(F32)
        mine_ref[i] = part
        swap = pltpu.make_async_remote_copy(
            src_ref=mine_ref.at[i], dst_ref=theirs_ref.at[i], send_sem=send_sems.at[i], recv_sem=recv_sems.at[i],
            device_id=(lax.axis_index("x"), lax.axis_index("y"), 1 - lax.axis_index("c")),
            device_id_type=pl.DeviceIdType.MESH)
        swap.start()
        swap.wait()
        g = part + theirs_ref[i]
        g_ref[...] = g
        d_ref[...], mo_ref[...], vo_ref[...] = _adam(w_ref[...], g, m_ref[...], v_ref[...])

    blk = pl.BlockSpec((tr, C), lambda i: (i, 0))
    return pl.pallas_call(
        body, name=name, grid=(nt,), out_shape=tuple(_sds((R, C), F32) for _ in range(4)),
        in_specs=[pl.BlockSpec((ns, tr, C), lambda i: (0, i, 0)), blk, blk, blk], out_specs=(blk, blk, blk, blk),
        scratch_shapes=[pltpu.VMEM((nt, tr, C), F32), pltpu.VMEM((nt, tr, C), F32),
                        pltpu.SemaphoreType.DMA((nt,)), pltpu.SemaphoreType.DMA((nt,))],
        compiler_params=_cp(("arbitrary",)),
    )(slots, w, m, v)


def _ada_adamw(sct, dmod, w, m, v):
    R, C = w.shape
    tr, bc = 256, 512

    def body(sct_ref, dm_ref, w_ref, m_ref, v_ref, g_ref, d_ref, mo_ref, vo_ref):
        g = sct_ref[:, 0:1] * dm_ref[0:1, :]
        for b in range(1, N_DEV):
            g = g + sct_ref[:, b:b + 1] * dm_ref[b:b + 1, :]
        g_ref[...] = g
        d_ref[...], mo_ref[...], vo_ref[...] = _adam(w_ref[...], g, m_ref[...], v_ref[...])

    blk = pl.BlockSpec((tr, bc), lambda i, j: (i, j))
    return pl.pallas_call(
        body, name="ada_adamw", grid=(R // tr, C // bc), out_shape=tuple(_sds((R, C), F32) for _ in range(4)),
        in_specs=[pl.BlockSpec((tr, N_DEV), lambda i, j: (i, 0)), pl.BlockSpec((N_DEV, bc), lambda i, j: (0, j)),
                  blk, blk, blk],
        out_specs=(blk, blk, blk, blk), compiler_params=_cp(("parallel", "parallel")),
    )(sct, dmod, w, m, v)


PACK = {"dmod": 0, "norm1_g": 6144, "norm2_g": 7168, "q_norm_g": 8192, "k_norm_g": 8704, "b_f": 9216, "conv_b": 9344,
        "conv_ln_g": 9856, "conv_ln_b": 10368, "beta_attn": 10880, "beta_conv": 11392, "loss": 11904}
SMALL_NAMES = ["b_ada", "norm1_g", "q_norm_g", "k_norm_g", "b_f", "conv_b", "conv_ln_g", "conv_ln_b", "beta_attn",
               "beta_conv", "norm2_g"]


def _pack_small(acc1, acc_d, acc_h, dg2, accg, accb, dcb, loss8):
    def body(a1_ref, ad_ref, ah_ref, dg2_ref, ag_ref, ab_ref, cb_ref, loss_ref, o_ref):
        def put(off, rows):
            o_ref[:, off:off + rows.shape[1]] = jnp.sum(rows, axis=0, keepdims=True)

        for t, rows in enumerate((a1_ref[0], a1_ref[1], ad_ref[3], ad_ref[0], ad_ref[1], dg2_ref[...])):
            put(PACK["dmod"] + t * D, rows)
        put(PACK["norm1_g"], a1_ref[2])
        put(PACK["norm2_g"], ad_ref[2])
        put(PACK["q_norm_g"], ag_ref[0])
        put(PACK["k_norm_g"], ag_ref[1])
        put(PACK["b_f"], ab_ref[...])
        put(PACK["conv_b"], cb_ref[...])
        put(PACK["conv_ln_g"], ah_ref[2])
        put(PACK["conv_ln_b"], ah_ref[3])
        put(PACK["beta_attn"], ah_ref[0])
        put(PACK["beta_conv"], ah_ref[1])
        o_ref[:, PACK["loss"]:PACK["loss"] + LANES] = loss_ref[0:1, :]

    vm = pl.BlockSpec(memory_space=pltpu.VMEM)
    return pl.pallas_call(body, name="pack_small", out_shape=_sds((1, SMALL_IN), F32), in_specs=[vm] * 8, out_specs=vm,
                          )(acc1, acc_d, acc_h, dg2, accg, accb, dcb, loss8)


def _small_adamw(slots, fold, ws, ms, vs):
    n = len(ws)
    widths = [w.shape[1] for w in ws]

    def body(s_ref, f_ref, *refs):
        w_refs, m_refs, v_refs = refs[:n], refs[n:2 * n], refs[2 * n:3 * n]
        outs = refs[3 * n:]
        tot = s_ref[0:1, :]
        for k in range(1, N_DEV):
            tot = tot + s_ref[k:k + 1, :]

        def grad(name, width):
            if name == "b_ada":
                return tot[:, 0:6 * D]
            seg = tot[:, PACK[name]:PACK[name] + max(width, LANES)]
            if name in ("q_norm_g", "k_norm_g"):
                seg = jnp.dot(jnp.broadcast_to(tot[:, PACK[name]:PACK[name] + AW], (8, AW)), f_ref[...], precision=HI,
                              preferred_element_type=F32)[0:1, :]
            return seg[:, 0:width]

        for t, (name, width) in enumerate(zip(SMALL_NAMES, widths)):
            g = grad(name, width)
            d, m_new, v_new = _adam(w_refs[t][...], g, m_refs[t][...], v_refs[t][...])
            outs[t][...] = g
            outs[n + t][...] = d
            outs[2 * n + t][...] = m_new
            outs[3 * n + t][...] = v_new
        outs[4 * n][...] = tot[:, PACK["loss"]:PACK["loss"] + LANES]

    vm = pl.BlockSpec(memory_space=pltpu.VMEM)
    outs = pl.pallas_call(
        body, name="adamw_small",
        out_shape=(*(_sds((1, w), F32) for _ in range(4) for w in widths), _sds((1, LANES), F32)),
        in_specs=[vm] * (2 + 3 * n), out_specs=tuple(vm for _ in range(4 * n + 1)),
    )(slots, fold, *ws, *ms, *vs)
    return [list(outs[k * n:(k + 1) * n]) for k in range(4)], outs[4 * n][0, 0]


def _perm_in(w):
    pad = jnp.zeros((w.shape[0], NP - 2568), w.dtype)
    return jnp.concatenate([w[:, :1536], w[:, 1544:2568], w[:, 1536:1544], pad], axis=1)


def _pad_lanes(vec, n=LANES):
    return jnp.pad(vec, ((0, 0), (0, n - vec.shape[1])))


def kernel(x, c, w_ada, b_ada, norm1_g, w_in, q_norm_g, k_norm_g, b_f, conv_w, conv_b, conv_ln_g, conv_ln_b, beta_attn, beta_conv, w_out, norm2_g, w_ff1, w_ff2, loss_target, m_w_ada, m_b_ada, m_norm1_g, m_w_in, m_q_norm_g, m_k_norm_g, m_b_f, m_conv_w, m_conv_b, m_conv_ln_g, m_conv_ln_b, m_beta_attn, m_beta_conv, m_w_out, m_norm2_g, m_w_ff1, m_w_ff2, v_w_ada, v_b_ada, v_norm1_g, v_w_in, v_q_norm_g, v_k_norm_g, v_b_f, v_conv_w, v_conv_b, v_conv_ln_g, v_conv_ln_b, v_beta_attn, v_beta_conv, v_w_out, v_norm2_g, v_w_ff1, v_w_ff2):
    S = x.shape[1]
    tm = min(256, S)
    tw = min(512, S)
    tq = min(512, S // 2)
    xs, tgt = x[0], loss_target[0]
    chip = 2 * lax.axis_index("x") + lax.axis_index("y")
    e512, et512 = _head_sum_mats()

    conv_w32 = jnp.pad(conv_w[0], ((0, 1), (0, 0)))
    c_all, g_in = _exchange([(c, "bcast8"), (w_in[0].astype(BF16), "chip4")], "gather_in")
    later_weights = _Exchange([(w_out[0].astype(BF16), "chip4"), (conv_w32, "chip4")])
    c_all = c_all.reshape(N_DEV, D)
    w_in_p = _perm_in(jnp.transpose(g_in, (1, 0, 2)).reshape(D, 2568))

    b_shard = lax.dynamic_slice(b_ada, (0, chip * 1536), (1, 1536))
    mod_rows, sc_all = _mod_shard(c_all, w_ada[0], b_shard)
    (mod_slots,) = _exchange([(mod_rows.reshape(N_DEV, 1, 1536), "all8")], "scatter_mod")
    mod = mod_slots.reshape(4, 2, 1536)[:, 0, :].reshape(6, D)
    mod8 = jnp.pad(mod, ((0, 2), (0, 0)))

    qg512 = jnp.tile(q_norm_g, (1, NH))
    kg512 = jnp.tile(k_norm_g, (1, NH))
    bf128 = _pad_lanes(b_f)

    (h1, qh, kh, vb, qn, kn, rq, rk, fgb, alin, agate, u0), (g_out, g_cw) = _fwd_in(
        xs, mod8, norm1_g, w_in_p, e512, et512, qg512, kg512, bf128, tw, ex=later_weights)
    w_out_f = g_out.reshape(D, D)
    cw32 = jnp.transpose(g_cw, (1, 0, 2)).reshape(HALO, CW)
    bd = _logit_bound(q_norm_g, k_norm_g)
    (qa, ka, va, fcum), (w1,) = _fwd_decay(fgb, qh, kh, vb, _shift(bd), tm,
                                           ex=_Exchange([(w_ff1[0].astype(BF16), "chip4")]))
    fs, fe = _skip_tables(fcum, tq)
    o_attn, lser = _attn_fwd(qa, ka, va, fs, fe, bd, tq)
    (u1, mc), (w2,) = _fwd_conv(u0, cw32, conv_b, conv_ln_g, conv_ln_b, beta_conv, tm,
                                ex=_Exchange([(w_ff2[0].astype(BF16), "chip4")]))
    merged, ob, x2, h2 = _fwd_out(o_attn, mc, xs, mod8, norm2_g, beta_attn, w_out_f, tw)
    r, dy, loss8, dg2 = _fwd_ffn(h2, w1, w2, x2, tgt, mod8, tw)

    df2, df1, dh2 = _bwd_ffn(dy, mod8, r, w1, w2, tw)
    gw_ff2 = _wgrad(r, df2, "wgrad_ff2", square_a=True)
    gw_ff1 = _wgrad(h2, df1, "wgrad_ff1", col_pieces=True)
    (dx2, do, doa, du1, acc_d, acc_h, dr), (p_ff1,) = _bwd_mid(
        dh2, dy, x2, ob, o_attn, u1, mod8, norm2_g, beta_attn, beta_conv, conv_ln_g, conv_ln_b, w_out_f, e512, tm, tq,
        ex=_Exchange([(gw_ff1, "chip4p")]))
    gw_out = _wgrad(merged, do, "wgrad_out")
    dqa, dka, dv = _attn_bwd(qa, ka, vb, doa, lser, dr, fs, fe, bd, tq)
    (dalin, dagate, dcw, dcb), (p_out, p_ff2) = _bwd_conv(
        du1, u0, alin, agate, cw32, tm,
        ex=_Exchange([(gw_out.reshape(4, 256, D), "chip4p"), (gw_ff2.reshape(4, D, D), "chip4p")]))
    dq, dk, dfg, accg, accb = _bwd_qk(dqa, dka, qn, kn, rq, rk, fgb, qg512, kg512, e512, et512, tm)
    gw_in_p = _wgrad_in(h1, [dq, dk, dv, dalin, dagate, dfg])
    gw_in = jnp.concatenate([gw_in_p[:, :1536], gw_in_p[:, 2560:2568], gw_in_p[:, 1536:2560]], axis=1)
    s8 = lambda a: jnp.sum(a, axis=-2)
    gcw = s8(dcw)
    in_grads = _Exchange([(jnp.transpose(gw_in.reshape(D, 4, 642), (1, 0, 2)), "chip4p"),
                          (jnp.transpose(gcw.reshape(HALO, 4, LANES), (1, 0, 2)), "chip4p")])
    (grad_x, acc1), (p_in, p_cw) = _bwd_in(dq, dk, dv, dalin, dagate, dfg, w_in_p, xs, dx2, mod8, norm1_g, tw,
                                           ex=in_grads)

    small = _pack_small(acc1, acc_d, acc_h, dg2, accg, accb, dcb, loss8)
    (small_s,) = _exchange([(small, "bcast8")], "gather_small")
    small_s = small_s.reshape(N_DEV, SMALL_IN)

    g_in_, d_in, nm_in, nv_in = _pair_adamw(p_in, w_in[0], m_w_in[0], v_w_in[0], "adamw_in")
    g_out_, d_out, nm_out, nv_out = _pair_adamw(p_out, w_out[0], m_w_out[0], v_w_out[0], "adamw_out")
    g_f1, d_f1, nm_f1, nv_f1 = _pair_adamw(p_ff1, w_ff1[0], m_w_ff1[0], v_w_ff1[0], "adamw_ff1")
    g_f2, d_f2, nm_f2, nv_f2 = _pair_adamw(p_ff2, w_ff2[0], m_w_ff2[0], v_w_ff2[0], "adamw_ff2")
    pad_row = lambda a, fill: jnp.pad(a[0], ((0, 1), (0, 0)), constant_values=fill)
    g_cw_, d_cw, nm_cw, nv_cw = (a[:KC] for a in _pair_adamw(
        p_cw, pad_row(conv_w, 0.0), pad_row(m_conv_w, 0.0), pad_row(v_conv_w, 1.0), "adamw_conv_w"))
    dmod_shard = lax.dynamic_slice(small_s[:, :6 * D], (0, chip * 1536), (N_DEV, 1536))
    g_ada, d_ada, nm_ada, nv_ada = _ada_adamw(sc_all.T, dmod_shard, w_ada[0], m_w_ada[0], v_w_ada[0])

    fold = np.zeros((AW, LANES), np.float32)
    fold[np.arange(AW), np.arange(AW) % DH] = 1.0
    smalls = [b_ada, norm1_g, q_norm_g, k_norm_g, b_f, conv_b, conv_ln_g, conv_ln_b, beta_attn, beta_conv, norm2_g]
    m_smalls = [m_b_ada, m_norm1_g, m_q_norm_g, m_k_norm_g, m_b_f, m_conv_b, m_conv_ln_g, m_conv_ln_b, m_beta_attn,
                m_beta_conv, m_norm2_g]
    v_smalls = [v_b_ada, v_norm1_g, v_q_norm_g, v_k_norm_g, v_b_f, v_conv_b, v_conv_ln_g, v_conv_ln_b, v_beta_attn,
                v_beta_conv, v_norm2_g]
    (gs, ds, ms, vs), loss = _small_adamw(small_s, jnp.asarray(fold), smalls, m_smalls, v_smalls)

    big ={"w_ada": (g_ada, d_ada, nm_ada, nv_ada), "w_in": (g_in_, d_in, nm_in, nv_in),
           "conv_w": (g_cw_, d_cw, nm_cw, nv_cw), "w_out": (g_out_, d_out, nm_out, nv_out),
           "w_ff1": (g_f1, d_f1, nm_f1, nv_f1), "w_ff2": (g_f2, d_f2, nm_f2, nv_f2)}
    order =["w_ada", "b_ada", "norm1_g", "w_in", "q_norm_g", "k_norm_g", "b_f", "conv_w", "conv_b", "conv_ln_g",
             "conv_ln_b", "beta_attn", "beta_conv", "w_out", "norm2_g", "w_ff1", "w_ff2"]

    def leaf(name, which):
        if name in big:
            return big[name][which][None]
        return (gs, ds, ms, vs)[which][SMALL_NAMES.index(name)]

    return (loss, grad_x[None], *[leaf(n, 0) for n in order], *[leaf(n, 1) for n in order],
            *[leaf(n, 2) for n in order], *[leaf(n, 3) for n in order])
```

```python
import functools

import numpy as np
import jax
import jax.numpy as jnp
from jax import lax
from jax.experimental import pallas as pl
from jax.experimental.pallas import tpu as pltpu

F32, BF16 = jnp.float32, jnp.bfloat16
HI = lax.Precision.HIGHEST
D = 1024
AW = 512
CW = 512
NH = 8
DH = 64
KC = 31
DFF = 4096
NP = 2688
EPS = 1e-6
NEG = -1e30
LANES = 128
VMEM_LIMIT = 56 * 2**20
NT = (((1,), (1,)), ((), ()))
TN = (((0,), (0,)), ((), ()))
LR, B1, B2, AEPS, WD, STEP = 0.001, 0.9, 0.999, 1e-08, 0.01, 10
N_DEV = 8
SMALL_IN = 12032


def _cp(sem=None, vmem=VMEM_LIMIT):
    kw = dict(vmem_limit_bytes=vmem)
    if sem is not None:
        kw["dimension_semantics"] = sem
    return pltpu.CompilerParams(**kw)


def _rows(tm, n):
    return pl.BlockSpec((tm, n), lambda i: (i, 0))


def _const(shape):
    nd = len(shape)
    return pl.BlockSpec(shape, lambda *_: (0,) * nd)


def _sds(shape, dt):
    return jax.ShapeDtypeStruct(shape, dt)


def _lane(shape):
    return lax.broadcasted_iota(jnp.int32, shape, len(shape) - 1)


def _sigmoid(x):
    return 1.0 / (1.0 + jnp.exp(-x))


class _Exchange:
    MASKS = {"chip4": (2, 4, 6), "chip4p": (2, 4, 6), "all8": (1, 2, 3, 4, 5, 6, 7), "bcast8": (1, 2, 3, 4, 5, 6, 7)}

    def __init__(self, items):
        self.srcs = [s for s, _ in items]
        self.kinds = [k for _, k in items]
        self.n = len(items)
        self.out_shapes = []
        for s, k in items:
            shape = {"all8": (N_DEV,) + s.shape[1:], "bcast8": (N_DEV,) + s.shape, "chip4": (4,) + s.shape,
                     "chip4p": (4,) + s.shape[1:]}[k]
            self.out_shapes.append(_sds(shape, s.dtype))
        self.sem_index = {}
        for t, k in enumerate(self.kinds):
            for m in self.MASKS[k]:
                self.sem_index[(t, m)] = len(self.sem_index)
        n_sem = len(self.sem_index)
        self.scratch = [pltpu.SemaphoreType.DMA((n_sem,)), pltpu.SemaphoreType.DMA((n_sem,)),
                        pltpu.SemaphoreType.DMA((self.n,))]
        self.in_specs = [pl.BlockSpec(memory_space=pl.ANY)] * self.n
        self.out_specs = [pl.BlockSpec(memory_space=pl.ANY)] * self.n

    def copies(self, src_refs, dst_refs, send_sems, recv_sems, local_sems):
        x, y, c = lax.axis_index("x"), lax.axis_index("y"), lax.axis_index("c")
        my_id = 4 * x + 2 * y + c
        my_chip = 2 * x + y

        def piece(t, dev_id, chip):
            k = self.kinds[t]
            return src_refs[t].at[dev_id] if k == "all8" else src_refs[t].at[chip] if k == "chip4p" else src_refs[t]

        out = []
        for t in range(self.n):
            slot = dst_refs[t].at[my_chip if self.kinds[t] in ("chip4", "chip4p") else my_id]
            out.append(pltpu.make_async_copy(piece(t, my_id, my_chip), slot, local_sems.at[t]))
            for m in self.MASKS[self.kinds[t]]:
                px = 1 - x if m & 4 else x
                py = 1 - y if m & 2 else y
                pc = 1 - c if m & 1 else c
                s = self.sem_index[(t, m)]
                out.append(pltpu.make_async_remote_copy(
                    src_ref=piece(t, 4 * px + 2 * py + pc, 2 * px + py), dst_ref=slot,
                    send_sem=send_sems.at[s], recv_sem=recv_sems.at[s],
                    device_id=(px, py, pc), device_id_type=pl.DeviceIdType.MESH))
        return out


def _hosted(body, ex, n_in, n_out, n_scr, n_steps):
    if ex is None:
        return body

    def wrapped(*refs):
        ins, xin = refs[:n_in], refs[n_in:n_in + ex.n]
        o0 = n_in + ex.n
        outs, xout = refs[o0:o0 + n_out], refs[o0 + n_out:o0 + n_out + ex.n]
        s0 = o0 + n_out + ex.n
        scr, sems = refs[s0:s0 + n_scr], refs[s0 + n_scr:]

        @pl.when(pl.program_id(0) == 0)
        def _():
            for cp in ex.copies(xin, xout, *sems):
                cp.start()

        body(*ins, *outs, *scr)

        @pl.when(pl.program_id(0) == n_steps - 1)
        def _():
            for cp in ex.copies(xin, xout, *sems):
                cp.wait()

    return wrapped


def _exchange(items, name):
    ex = _Exchange(items)
    n = ex.n

    def body(*refs):
        copies = ex.copies(refs[:n], refs[n:2 * n], *refs[2 * n:])
        for cp in copies:
            cp.start()
        for cp in copies:
            cp.wait()

    outs = pl.pallas_call(
        body, name=name, out_shape=tuple(ex.out_shapes), in_specs=ex.in_specs, out_specs=tuple(ex.out_specs),
        scratch_shapes=ex.scratch,
    )(*ex.srcs)
    return list(outs)


def _mod_shard(c_all, w_ada, b_shard):
    n = w_ada.shape[1]

    def body(c_ref, w_ref, b_ref, o_ref, sc_ref):
        cv = c_ref[...]
        sc = cv * _sigmoid(cv)
        sc_ref[...] = sc
        o_ref[...] = jnp.dot(sc, w_ref[...], precision=HI, preferred_element_type=F32) + b_ref[...]

    bn = 512
    return pl.pallas_call(
        body, name="mod_shard", out_shape=(_sds((N_DEV, n), F32), _sds((N_DEV, D), F32)), grid=(n // bn,),
        in_specs=[_const((N_DEV, D)), pl.BlockSpec((D, bn), lambda j: (0, j)), pl.BlockSpec((1, bn), lambda j: (0, j))],
        out_specs=(pl.BlockSpec((N_DEV, bn), lambda j: (0, j)), _const((N_DEV, D))),
        compiler_params=_cp(("arbitrary",)),
    )(c_all, w_ada, b_shard)


def _head_sum_mats():
    e = np.zeros((AW, LANES), np.float32)
    for h in range(NH):
        e[h * DH:(h + 1) * DH, h] = 1.0
    return jnp.asarray(e, BF16), jnp.asarray(e.T.copy(), BF16)


def _dot2(x, w):
    hi = x.astype(BF16)
    lo = (x - hi.astype(F32)).astype(BF16)
    return jnp.dot(hi, w, preferred_element_type=F32) + jnp.dot(lo, w, preferred_element_type=F32)


def _fwd_in(x, mod8, n1g, w_in_p, e512, et512, qg512, kg512, bf128, tm, ex=None):
    S = x.shape[0]

    def body(x_ref, mod_ref, n1g_ref, w_ref, e_ref, et_ref, qg_ref, kg_ref, bf_ref,
             h1_ref, qh_ref, kh_ref, v_ref, qn_ref, kn_ref, rq_ref, rk_ref, fgb_ref, alin_ref, agate_ref, u0_ref):
        xv = x_ref[...]
        r1 = lax.rsqrt(jnp.mean(xv * xv, axis=-1, keepdims=True) + EPS)
        h = (xv * r1) * (n1g_ref[...] * (1.0 + mod_ref[1:2, :])) + mod_ref[0:1, :]
        hb = h.astype(BF16)
        h1_ref[...] = hb

        def seg(a, b):
            return jnp.dot(hb, w_ref[:, a:b], preferred_element_type=F32)

        def headnorm(t, g_ref, scale, n_ref, r_ref, o_ref):
            ss = _dot2(t * t, e_ref[...])
            r = lax.rsqrt(ss * (1.0 / DH) + EPS)
            tn = t * _dot2(r, et_ref[...])
            n_ref[...] = tn.astype(BF16)
            r_ref[...] = r
            o_ref[...] = (tn * (g_ref[...] * scale)).astype(BF16)

        headnorm(seg(0, 512), qg_ref, DH ** -0.5, qn_ref, rq_ref, qh_ref)
        headnorm(seg(512, 1024), kg_ref, 1.0, kn_ref, rk_ref, kh_ref)
        v_ref[...] = seg(1024, 1536).astype(BF16)
        alin = seg(1536, 2048)
        agate = seg(2048, 2560)
        alin_ref[...] = alin.astype(BF16)
        agate_ref[...] = agate.astype(BF16)
        u0_ref[...] = alin * _sigmoid(agate)
        fgb_ref[...] = seg(2560, NP) + bf_ref[...]

    bf = lambda: _sds((S, AW), BF16)
    xs = ex.srcs if ex else []
    outs = pl.pallas_call(
        _hosted(body, ex, 9, 12, 0, S // tm), name="fwd_in", grid=(S // tm,),
        out_shape=(_sds((S, D), BF16), bf(), bf(), bf(), bf(), bf(), _sds((S, LANES), F32), _sds((S, LANES), F32),
                   _sds((S, LANES), F32), bf(), bf(), _sds((S, CW), F32), *(ex.out_shapes if ex else [])),
        in_specs=[_rows(tm, D), _const((8, D)), _const((1, D)), _const((D, NP)), _const((AW, LANES)), _const((LANES, AW)),
                  _const((1, AW)), _const((1, AW)), _const((1, LANES)), *(ex.in_specs if ex else [])],
        out_specs=(_rows(tm, D), _rows(tm, AW), _rows(tm, AW), _rows(tm, AW), _rows(tm, AW), _rows(tm, AW),
                   _rows(tm, LANES), _rows(tm, LANES), _rows(tm, LANES), _rows(tm, AW), _rows(tm, AW), _rows(tm, CW),
                   *(ex.out_specs if ex else [])),
        scratch_shapes=ex.scratch if ex else [],
        compiler_params=_cp(("arbitrary",)),
    )(x, mod8, n1g, w_in_p, e512, et512, qg512, kg512, bf128, *xs)
    return outs[:12], list(outs[12:])


def _split3(f):
    f1 = f.astype(BF16).astype(F32)
    f2 = (f - f1).astype(BF16).astype(F32)
    return f1, f2, f - f1 - f2


def _dot3(w, x):
    return sum(jnp.dot(w, piece.astype(BF16), preferred_element_type=F32) for piece in _split3(x))


def _fwd_decay(fgb, qh, kh, vb, shift, tm, ex=None):
    S = fgb.shape[0]

    def body(shift_ref, fgb_ref, qh_ref, kh_ref, vb_ref, qa_ref, ka_ref, va_ref, f_ref, carry_ref):
        @pl.when(pl.program_id(0) == 0)
        def _():
            carry_ref[...] = jnp.zeros_like(carry_ref)

        fb = fgb_ref[...]
        lf = jnp.minimum(fb, 0.0) - jnp.log1p(jnp.exp(-jnp.abs(fb)))
        tri = (lax.broadcasted_iota(jnp.int32, (tm, tm), 0) >= lax.broadcasted_iota(jnp.int32, (tm, tm), 1)
               ).astype(F32).astype(BF16)
        cs = _dot3(tri, lf) + carry_ref[0:1, :]
        f_ref[...] = cs
        carry_ref[...] = jnp.broadcast_to(cs[tm - 1:tm, :], carry_ref.shape)
        lane = _lane((tm, LANES))
        s1, s2, s3 = _split3(jnp.zeros((tm, LANES), F32) - shift_ref[0, 0])
        tail_q = jnp.where((lane >= 67) & (lane < 70), 1.0,
                           jnp.where(lane == 70, s1, jnp.where(lane == 71, s2, jnp.where(lane == 72, s3, 0.0))))
        tail_k = jnp.where(((lane >= 64) & (lane < 67)) | ((lane >= 70) & (lane < 73)), 1.0, 0.0)
        tail_v = jnp.where(lane == DH, 1.0, 0.0)
        for p in range(NH // 2):
            qp = qh_ref[:, p * LANES:(p + 1) * LANES].astype(F32)
            kp = kh_ref[:, p * LANES:(p + 1) * LANES].astype(F32)
            vp = vb_ref[:, p * LANES:(p + 1) * LANES].astype(F32)
            for hh in range(2):
                h = 2 * p + hh
                f1, f2, f3 = _split3(cs[:, h:h + 1])
                qb = qp if hh == 0 else pltpu.roll(qp, 64, 1)
                kb = kp if hh == 0 else pltpu.roll(kp, 64, 1)
                vh = vp if hh == 0 else pltpu.roll(vp, 64, 1)
                augq = jnp.where(lane == 64, f1, jnp.where(lane == 65, f2, jnp.where(lane == 66, f3, tail_q)))
                augk = jnp.where(lane == 67, -f1, jnp.where(lane == 68, -f2, jnp.where(lane == 69, -f3, tail_k)))
                qa_ref[h] = jnp.where(lane < DH, qb, augq).astype(BF16)
                ka_ref[h] = jnp.where(lane < DH, kb, augk).astype(BF16)
                va_ref[h] = jnp.where(lane < DH, vh, tail_v).astype(BF16)

    hm = pl.BlockSpec((NH, tm, LANES), lambda i: (0, i, 0))
    hms = _sds((NH, S, LANES), BF16)
    outs = pl.pallas_call(
        _hosted(body, ex, 5, 4, 1, S // tm), name="fwd_decay", grid=(S // tm,),
        out_shape=(hms, hms, hms, _sds((S, LANES), F32), *(ex.out_shapes if ex else [])),
        in_specs=[SMEM_SPEC, _rows(tm, LANES), _rows(tm, AW), _rows(tm, AW), _rows(tm, AW),
                  *(ex.in_specs if ex else [])],
        out_specs=(hm, hm, hm, _rows(tm, LANES), *(ex.out_specs if ex else [])),
        scratch_shapes=[pltpu.VMEM((8, LANES), F32), *(ex.scratch if ex else [])],
        compiler_params=_cp(("arbitrary",)),
    )(shift, fgb, qh, kh, vb, *(ex.srcs if ex else []))
    return outs[:4], list(outs[4:])


SKIP = -106.0
SKIP_P = -88.0


def _block_loops(first, step, needed, run):
    def both(j):
        return jnp.logical_and(needed(0, j), needed(1, j))

    def walk(heads):
        def go(j):
            run(j, heads)
            return j + step
        return go

    j = lax.while_loop(both, walk((0, 1)), first)
    lax.while_loop(functools.partial(needed, 0), walk((0,)), j)
    lax.while_loop(functools.partial(needed, 1), walk((1,)), j)


def _logit_bound(qg, kg):
    return (2.0 * 1.03 * DH ** 0.5 * jnp.max(jnp.abs(qg)) * jnp.max(jnp.abs(kg))).reshape(1, 1)


def _skip_tables(f, tq):
    return f[0::tq, :NH].T, f[tq - 1::tq, :NH].T


SMEM_SPEC = pl.BlockSpec(memory_space=pltpu.SMEM)


def _causal_rect(rows, cols, col0):
    return (lax.broadcasted_iota(jnp.int32, (rows, cols), 0)
            >= lax.broadcasted_iota(jnp.int32, (rows, cols), 1) + col0)


SHIFT_MAX = 60.0


def _shift(bd):
    return jnp.where(bd <= SHIFT_MAX, 0.5 * bd, 0.0)


def _attn_fwd(qa, ka, va, fs, fe, bd, tq):
    S = qa.shape[1]
    nq = S // tq

    def body(fs_ref, fe_ref, bd_ref, qa_ref, ka_ref, va_ref, o_ref, lser_ref, m_ref, acc_ref):
        pr, i = pl.program_id(0), pl.program_id(1)
        sel_a = _lane((tq, LANES)) < DH
        acc_ref[...] = jnp.zeros_like(acc_ref)

        def logits(j, hh, masked):
            start = pl.multiple_of(j * tq, tq)
            s = lax.dot_general(qa_ref[hh], ka_ref[hh, pl.ds(start, tq), :], NT, preferred_element_type=F32)
            if masked:
                s = jnp.where(_causal_rect(tq, tq, 0), s, NEG)
            return s, va_ref[hh, pl.ds(start, tq), :]

        def shifted_step(j, heads, masked=False):
            for hh in heads:
                s, vb = logits(j, hh, masked)
                acc_ref[hh] += jnp.dot(jnp.exp(s).astype(BF16), vb, preferred_element_type=F32)

        def online_step(j, heads, masked=False):
            for hh in heads:
                s, vb = logits(j, hh, masked)
                m_prev = m_ref[hh]
                m_new = jnp.maximum(m_prev, jnp.max(s, axis=1, keepdims=True))
                p = jnp.exp(s - jnp.tile(m_new, (1, tq // LANES)))
                m_ref[hh] = m_new
                acc_ref[hh] = jnp.exp(m_prev - m_new) * acc_ref[hh] + jnp.dot(p.astype(BF16), vb,
                                                                              preferred_element_type=F32)

        def needed(slack, hh, j):
            top = fs_ref[2 * pr + hh, i] + slack
            return jnp.logical_and(j >= 0, top - fe_ref[2 * pr + hh, jnp.maximum(j, 0)] >= SKIP)

        def shifted_diagonal():
            h = tq // 2
            base = pl.multiple_of(i * tq, tq)
            for hh in (0, 1):
                s0 = lax.dot_general(qa_ref[hh], ka_ref[hh, pl.ds(base, h), :], NT, preferred_element_type=F32)
                s0 = jnp.where(_causal_rect(tq, h, 0), s0, NEG)
                acc_ref[hh] += jnp.dot(jnp.exp(s0).astype(BF16), va_ref[hh, pl.ds(base, h), :],
                                       preferred_element_type=F32)
                newer = pl.multiple_of(base + h, h)
                s1 = lax.dot_general(qa_ref[hh, h:tq, :], ka_ref[hh, pl.ds(newer, h), :], NT,
                                     preferred_element_type=F32)
                s1 = jnp.where(_causal_rect(h, h, 0), s1, NEG)
                acc_ref[hh, h:tq, :] += jnp.dot(jnp.exp(s1).astype(BF16), va_ref[hh, pl.ds(newer, h), :],
                                                preferred_element_type=F32)

        @pl.when(bd_ref[0, 0] <= SHIFT_MAX)
        def _():
            m_ref[...] = jnp.zeros_like(m_ref)
            shifted_diagonal()
            _block_loops(i - 1, -1, functools.partial(needed, 0.0), shifted_step)

        @pl.when(bd_ref[0, 0] > SHIFT_MAX)
        def _():
            m_ref[...] = jnp.full(m_ref.shape, NEG, F32)
            online_step(i, (0, 1), masked=True)
            _block_loops(i - 1, -1, functools.partial(needed, bd_ref[0, 0]), online_step)

        outs, lses = [], []
        for hh in range(2):
            acc = acc_ref[hh]
            row_sum = jnp.broadcast_to(acc[:, DH:DH + 1], (tq, LANES))
            outs.append(acc / row_sum)
            lses.append(m_ref[hh] + jnp.log(row_sum))
        o_ref[...] = jnp.where(sel_a, outs[0], pltpu.roll(outs[1], 64, 1))
        row = lax.broadcasted_iota(jnp.int32, (8, tq), 0)
        lser_ref[0, 0] = jnp.where(row == 0, lses[0].T[0:8, :], lses[1].T[0:8, :])

    return pl.pallas_call(
        body, name="attn_fwd", grid=(NH // 2, nq),
        out_shape=(_sds((S, AW), F32), _sds((NH // 2, nq, 8, tq), F32)),
        in_specs=[SMEM_SPEC, SMEM_SPEC, SMEM_SPEC,
                  pl.BlockSpec((2, tq, LANES), lambda p, i: (p, i, 0)),
                  pl.BlockSpec((2, S, LANES), lambda p, i: (p, 0, 0)),
                  pl.BlockSpec((2, S, LANES), lambda p, i: (p, 0, 0))],
        out_specs=(pl.BlockSpec((tq, LANES), lambda p, i: (i, p)),
                   pl.BlockSpec((1, 1, 8, tq), lambda p, i: (p, i, 0, 0))),
        scratch_shapes=[pltpu.VMEM((2, tq, LANES), F32), pltpu.VMEM((2, tq, LANES), F32)],
        compiler_params=_cp(("parallel", "parallel")),
    )(fs, fe, bd, qa, ka, va)


HALO = 32
FWD_CHUNK = 64
BWD_CHUNK = 32


def _halo_prev(tm):
    return pl.BlockSpec((HALO, CW), lambda i: (jnp.maximum(i * (tm // HALO) - 1, 0), 0))


SUBLANES = 8
SHIFT_ROWS = 24


def _shifted_copies(ext_ref, sh_ref, tm):
    for k in range(1, SUBLANES):
        sh_ref[k - 1, 0:tm + SHIFT_ROWS, :] = ext_ref[k:k + tm + SHIFT_ROWS, :]


def _tap_windows(ext_ref, sh_ref, r0, offset, rows, shared):
    for k in range(SUBLANES):
        taps = sorted((offset(j), j) for j in range(KC) if offset(j) % SUBLANES == k)
        for group in ([taps] if shared and taps else [[t] for t in taps]):
            lo, hi = r0 + group[0][0] - k, r0 + group[-1][0] - k + rows
            window = ext_ref[lo:hi, :] if k == 0 else sh_ref[k - 1, lo:hi, :]
            for off, j in group:
                at = r0 + off - k - lo
                yield j, window[at:at + rows, :]


def _fwd_conv(u0, w32, cb, lng, lnb, beta_c, tm, ex=None):
    S = u0.shape[0]

    def body(cur_ref, prev_ref, w_ref, cb_ref, lng_ref, lnb_ref, beta_ref, u1_ref, mc_ref, ext_ref, sh_ref):
        i = pl.program_id(0)
        ext_ref[0:HALO, :] = jnp.where(i == 0, 0.0, prev_ref[...])
        ext_ref[HALO:, :] = cur_ref[...]
        _shifted_copies(ext_ref, sh_ref, tm)
        for r0 in range(0, tm, FWD_CHUNK):
            acc = jnp.zeros((FWD_CHUNK, CW), F32) + cb_ref[...]
            for j, rows in _tap_windows(ext_ref, sh_ref, r0, lambda j: 2 + j, FWD_CHUNK, shared=False):
                acc = acc + w_ref[j:j + 1, :] * rows
            u1_ref[r0:r0 + FWD_CHUNK, :] = acc
        u1 = u1_ref[...]
        mu = jnp.mean(u1, axis=-1, keepdims=True)
        d = u1 - mu
        rstd = lax.rsqrt(jnp.mean(d * d, axis=-1, keepdims=True) + EPS)
        u2 = d * rstd * lng_ref[...] + lnb_ref[...]
        u3 = u2 * _sigmoid(u2)
        rc = lax.rsqrt(jnp.mean(u3 * u3, axis=-1, keepdims=True) + EPS)
        mc_ref[...] = (u3 * rc * beta_ref[...]).astype(BF16)

    outs = pl.pallas_call(
        _hosted(body, ex, 7, 2, 2, S // tm), name="fwd_conv", grid=(S // tm,),
        out_shape=(_sds((S, CW), F32), _sds((S, CW), BF16), *(ex.out_shapes if ex else [])),
        in_specs=[_rows(tm, CW), _halo_prev(tm), _const((HALO, CW)), _const((1, CW)), _const((1, CW)), _const((1, CW)),
                  _const((1, CW)), *(ex.in_specs if ex else [])],
        out_specs=(_rows(tm, CW), _rows(tm, CW), *(ex.out_specs if ex else [])),
        scratch_shapes=[pltpu.VMEM((tm + HALO, CW), F32), pltpu.VMEM((SUBLANES - 1, tm + HALO, CW), F32),
                        *(ex.scratch if ex else [])],
        compiler_params=_cp(("arbitrary",)),
    )(u0, u0, w32, cb, lng, lnb, beta_c, *(ex.srcs if ex else []))
    return outs[:2], list(outs[2:])


def _fwd_out(o_attn, mc, x, mod8, n2g, beta_a, w_out, tm):
    S = x.shape[0]

    def body(o_ref, mc_ref, x_ref, mod_ref, n2g_ref, beta_ref, w_ref, mg_ref, ob_ref, x2_ref, h2_ref):
        ov = o_ref[...]
        ra = lax.rsqrt(jnp.mean(ov * ov, axis=-1, keepdims=True) + EPS)
        ma = (ov * ra * beta_ref[...]).astype(BF16)
        mcv = mc_ref[...]
        mg_ref[:, 0:AW] = ma
        mg_ref[:, AW:D] = mcv
        o = (jnp.dot(ma, w_ref[0:AW, :], preferred_element_type=F32)
             + jnp.dot(mcv, w_ref[AW:D, :], preferred_element_type=F32))
        ob_ref[...] = o.astype(BF16)
        x2 = x_ref[...] + mod_ref[2:3, :] * o
        x2_ref[...] = x2
        r2 = lax.rsqrt(jnp.mean(x2 * x2, axis=-1, keepdims=True) + EPS)
        h2_ref[...] = ((x2 * r2) * (n2g_ref[...] * (1.0 + mod_ref[4:5, :])) + mod_ref[3:4, :]).astype(BF16)

    return pl.pallas_call(
        body, name="fwd_out", grid=(S // tm,),
        out_shape=(_sds((S, D), BF16), _sds((S, D), BF16), _sds((S, D), F32), _sds((S, D), BF16)),
        in_specs=[_rows(tm, AW), _rows(tm, CW), _rows(tm, D), _const((8, D)), _const((1, D)), _const((1, AW)),
                  _const((D, D))],
        out_specs=(_rows(tm, D), _rows(tm, D), _rows(tm, D), _rows(tm, D)),
        compiler_params=_cp(("parallel",)),
    )(o_attn, mc, x, mod8, n2g, beta_a, w_out)


def _fwd_ffn(h2, w1, w2, x2, tgt, mod8, tm):
    S = h2.shape[0]
    nk = w1.shape[0]
    bf = w1.shape[2]

    def body(h2_ref, w1_ref, w2_ref, x2_ref, tgt_ref, mod_ref, r_ref, dy_ref, loss_ref, dg2_ref):
        @pl.when(pl.program_id(0) == 0)
        def _():
            loss_ref[...] = jnp.zeros_like(loss_ref)
            dg2_ref[...] = jnp.zeros_like(dg2_ref)

        f2 = None
        for k in range(nk):
            r = jnp.maximum(jnp.dot(h2_ref[...], w1_ref[k], preferred_element_type=F32), 0.0)
            r_ref[:, k * bf:(k + 1) * bf] = r.astype(BF16)
            part = jnp.dot((r * r).astype(BF16), w2_ref[k], preferred_element_type=F32)
            f2 = part if f2 is None else f2 + part
        e = x2_ref[...] + mod_ref[5:6, :] * f2 - tgt_ref[...]
        dy = e * (1.0 / D)
        dy_ref[...] = dy
        loss_ref[...] += 0.5 * jnp.sum(jnp.sum(e * dy, axis=1, keepdims=True), axis=0, keepdims=True)
        dg2_ref[...] += jnp.sum((dy * f2).reshape(tm // 8, 8, D), axis=0)

    once = pl.Buffered(1)
    return pl.pallas_call(
        body, name="fwd_ffn", grid=(S // tm,),
        out_shape=(_sds((S, DFF), BF16), _sds((S, D), F32), _sds((8, LANES), F32), _sds((8, D), F32)),
        in_specs=[_rows(tm, D), pl.BlockSpec((nk, D, bf), lambda i: (0, 0, 0), pipeline_mode=once),
                  pl.BlockSpec((nk, bf, D), lambda i: (0, 0, 0), pipeline_mode=once), _rows(tm, D), _rows(tm, D),
                  _const((8, D))],
        out_specs=(_rows(tm, DFF), _rows(tm, D), _const((8, LANES)), _const((8, D))),
        compiler_params=_cp(("arbitrary",)),
    )(h2, w1, w2, x2, tgt, mod8)


def _bwd_ffn(dy, mod8, r, w1, w2, tm):
    S = dy.shape[0]
    nk = w1.shape[0]
    bf = w1.shape[2]

    def body(dy_ref, mod_ref, r_ref, w1_ref, w2_ref, df2_ref, df1_ref, dh2_ref):
        df2 = (dy_ref[...] * mod_ref[5:6, :]).astype(BF16)
        df2_ref[...] = df2
        dh2 = None
        for k in range(nk):
            da = lax.dot_general(df2, w2_ref[k], NT, preferred_element_type=F32)
            df1 = (da * (2.0 * r_ref[:, k * bf:(k + 1) * bf].astype(F32))).astype(BF16)
            df1_ref[:, k * bf:(k + 1) * bf] = df1
            part = lax.dot_general(df1, w1_ref[k], NT, preferred_element_type=F32)
            dh2 = part if dh2 is None else dh2 + part
        dh2_ref[...] = dh2

    once = pl.Buffered(1)
    return pl.pallas_call(
        body, name="bwd_ffn", grid=(S // tm,),
        out_shape=(_sds((S, D), BF16), _sds((S, DFF), BF16), _sds((S, D), F32)),
        in_specs=[_rows(tm, D), _const((8, D)), _rows(tm, DFF),
                  pl.BlockSpec((nk, D, bf), lambda i: (0, 0, 0), pipeline_mode=once),
                  pl.BlockSpec((nk, bf, D), lambda i: (0, 0, 0), pipeline_mode=once)],
        out_specs=(_rows(tm, D), _rows(tm, DFF), _rows(tm, D)),
        compiler_params=_cp(("parallel",)),
    )(dy, mod8, r, w1, w2)


def _token_tile(S, want):
    while S % want:
        want //= 2
    return want


def _wgrad(a, b, name, square_a=False, col_pieces=False, tk=2048, bm=1024, bn=1024):
    S, M = a.shape
    N = b.shape[1]
    bm, bn, tk = min(bm, M), min(bn, N), _token_tile(S, tk)
    nk = S // tk

    def body(a_ref, b_ref, o_ref, acc_ref):
        av = a_ref[...]
        if square_a:
            af = av.astype(F32)
            av = (af * af).astype(BF16)
        part = lax.dot_general(av, b_ref[...], TN, preferred_element_type=F32)

        @pl.when(pl.program_id(2) == 0)
        def _():
            acc_ref[...] = part

        @pl.when(pl.program_id(2) > 0)
        def _():
            acc_ref[...] += part

        @pl.when(pl.program_id(2) == nk - 1)
        def _():
            if col_pieces:
                o_ref[0] = acc_ref[...].astype(BF16)
            else:
                o_ref[...] = acc_ref[...].astype(BF16)

    if col_pieces:
        out_shape, out_spec = _sds((N // bn, M, bn), BF16), pl.BlockSpec((1, bm, bn), lambda mi, ni, k: (ni, mi, 0))
    else:
        out_shape, out_spec = _sds((M, N), BF16), pl.BlockSpec((bm, bn), lambda mi, ni, k: (mi, ni))
    return pl.pallas_call(
        body, name=name, grid=(M // bm, N // bn, nk), out_shape=out_shape,
        in_specs=[pl.BlockSpec((tk, bm), lambda mi, ni, k: (k, mi)), pl.BlockSpec((tk, bn), lambda mi, ni, k: (k, ni))],
        out_specs=out_spec, scratch_shapes=[pltpu.VMEM((bm, bn), F32)],
        compiler_params=_cp(("parallel", "parallel", "arbitrary")),
    )(a, b)


def _wgrad_in(h1, pieces, tk=1024):
    S = h1.shape[0]
    tk = _token_tile(S, tk)
    widths = [p.shape[1] for p in pieces]
    offs = [sum(widths[:t]) for t in range(len(widths))]

    def body(a_ref, *refs):
        o_ref, acc_ref = refs[-2:]

        @pl.when(pl.program_id(0) == 0)
        def _():
            acc_ref[...] = jnp.zeros_like(acc_ref)

        for b_ref, off, w in zip(refs[:-2], offs, widths):
            acc_ref[:, off:off + w] += lax.dot_general(a_ref[...], b_ref[...], TN, preferred_element_type=F32)

        @pl.when(pl.program_id(0) == S // tk - 1)
        def _():
            o_ref[...] = acc_ref[...].astype(BF16)

    return pl.pallas_call(
        body, name="wgrad_in", grid=(S // tk,), out_shape=_sds((D, NP), BF16),
        in_specs=[_rows(tk, D)] + [_rows(tk, w) for w in widths], out_specs=_const((D, NP)),
        scratch_shapes=[pltpu.VMEM((D, NP), F32)], compiler_params=_cp(("arbitrary",)),
    )(h1, *pieces)


def _colsum8(t):
    return jnp.sum(t.reshape(t.shape[0] // 8, 8, t.shape[1]), axis=0)


def _bwd_mid(dh2, dy, x2, ob, o_attn, u1, mod8, n2g, beta_a, beta_c, lng, lnb, w_out, e512, tm, tq, ex=None):
    S = dy.shape[0]

    def body(dh2_ref, dy_ref, x2_ref, ob_ref, oa_ref, u1_ref, mod_ref, n2g_ref, ba_ref, bc_ref, lng_ref, lnb_ref, w_ref,
             e_ref, dx2_ref, do_ref, doa_ref, du1_ref, acc_d_ref, acc_h_ref, dr_ref):
        @pl.when(pl.program_id(0) == 0)
        def _():
            acc_d_ref[...] = jnp.zeros_like(acc_d_ref)
            acc_h_ref[...] = jnp.zeros_like(acc_h_ref)

        x2 = x2_ref[...]
        dh2 = dh2_ref[...]
        r2 = lax.rsqrt(jnp.mean(x2 * x2, axis=-1, keepdims=True) + EPS)
        xn2 = x2 * r2
        gain = n2g_ref[...] * (1.0 + mod_ref[4:5, :])
        dxn = dh2 * gain
        dx2 = dy_ref[...] + r2 * (dxn - xn2 * jnp.mean(dxn * xn2, axis=-1, keepdims=True))
        dx2_ref[...] = dx2
        t = dh2 * xn2
        acc_d_ref[0] += _colsum8(dh2)
        acc_d_ref[1] += _colsum8(t * n2g_ref[...])
        acc_d_ref[2] += _colsum8(t * (1.0 + mod_ref[4:5, :]))
        acc_d_ref[3] += _colsum8(dx2 * ob_ref[...].astype(F32))
        do = (dx2 * mod_ref[2:3, :]).astype(BF16)
        do_ref[...] = do
        dma = lax.dot_general(do, w_ref[0:AW, :], NT, preferred_element_type=F32)
        dmc = lax.dot_general(do, w_ref[AW:D, :], NT, preferred_element_type=F32)
        ov = oa_ref[...]
        ra = lax.rsqrt(jnp.mean(ov * ov, axis=-1, keepdims=True) + EPS)
        on = ov * ra
        acc_h_ref[0] += _colsum8(dma * on)
        don = dma * ba_ref[...]
        doa = (ra * (don - on * jnp.mean(don * on, axis=-1, keepdims=True))).astype(BF16)
        doa_ref[...] = doa
        delta_t = _dot2(doa.astype(F32) * ov, e_ref[...]).T
        for p in range(NH // 2):
            dr_ref[p, 0] = delta_t[2 * p:2 * p + 8, :]
        u1 = u1_ref[...]
        mu = jnp.mean(u1, axis=-1, keepdims=True)
        d = u1 - mu
        rstd = lax.rsqrt(jnp.mean(d * d, axis=-1, keepdims=True) + EPS)
        uh = d * rstd
        u2 = uh * lng_ref[...] + lnb_ref[...]
        sg = _sigmoid(u2)
        u3 = u2 * sg
        rc = lax.rsqrt(jnp.mean(u3 * u3, axis=-1, keepdims=True) + EPS)
        u3n = u3 * rc
        acc_h_ref[1] += _colsum8(dmc * u3n)
        du3n = dmc * bc_ref[...]
        du3 = rc * (du3n - u3n * jnp.mean(du3n * u3n, axis=-1, keepdims=True))
        du2 = du3 * (sg * (1.0 + u2 * (1.0 - sg)))
        acc_h_ref[2] += _colsum8(du2 * uh)
        acc_h_ref[3] += _colsum8(du2)
        duh = du2 * lng_ref[...]
        du1_ref[...] = rstd * (duh - jnp.mean(duh, axis=-1, keepdims=True)
                               - uh * jnp.mean(duh * uh, axis=-1, keepdims=True))

    per = tq // tm
    outs = pl.pallas_call(
        _hosted(body, ex, 14, 7, 0, S // tm), name="bwd_mid", grid=(S // tm,),
        out_shape=(_sds((S, D), F32), _sds((S, D), BF16), _sds((S, AW), BF16), _sds((S, CW), F32),
                   _sds((4, 8, D), F32), _sds((4, 8, AW), F32), _sds((NH // 2, S // tq, 8, tq), F32),
                   *(ex.out_shapes if ex else [])),
        in_specs=[_rows(tm, D), _rows(tm, D), _rows(tm, D), _rows(tm, D), _rows(tm, AW), _rows(tm, CW), _const((8, D)),
                  _const((1, D)), _const((1, AW)), _const((1, CW)), _const((1, CW)), _const((1, CW)), _const((D, D)),
                  _const((AW, LANES)), *(ex.in_specs if ex else [])],
        out_specs=(_rows(tm, D), _rows(tm, D), _rows(tm, AW), _rows(tm, CW), _const((4, 8, D)), _const((4, 8, AW)),
                   pl.BlockSpec((NH // 2, 1, 8, tm), lambda i: (0, i // per, 0, i % per)),
                   *(ex.out_specs if ex else [])),
        scratch_shapes=ex.scratch if ex else [],
        compiler_params=_cp(("arbitrary",)),
    )(dh2, dy, x2, ob, o_attn, u1, mod8, n2g, beta_a, beta_c, lng, lnb, w_out, e512, *(ex.srcs if ex else []))
    return outs[:7], list(outs[7:])


def _attn_bwd(qa, ka, v, do, lser, dr, fs, fe, bd, tq):
    S = qa.shape[1]
    nq = S // tq

    def body(fs_ref, fe_ref, bd_ref, ka_ref, v_ref, qa_ref, do_ref, lse_ref, dr_ref, dqa_hbm, dka_ref, dv_ref,
             accq_ref, acck_ref, accv_ref, out_sem):
        pr, j = pl.program_id(0), pl.program_id(1)
        sel_a = _lane((tq, LANES)) < DH
        vv = v_ref[...]
        zb = jnp.zeros_like(vv)
        vs = [jnp.where(sel_a, vv, zb), jnp.where(sel_a, zb, vv)]
        acck_ref[...] = jnp.zeros_like(acck_ref)
        accv_ref[...] = jnp.zeros_like(accv_ref)

        @pl.when(j == 0)
        def _():
            accq_ref[...] = jnp.zeros_like(accq_ref)

        def q_step(i, heads, masked=False):
            start = pl.multiple_of(i * tq, tq)
            dob = do_ref[pl.ds(start, tq), :]
            lse8 = lse_ref[0, i]
            dr8 = dr_ref[0, i]
            for hh in heads:
                qb = qa_ref[hh, pl.ds(start, tq), :]
                kb = ka_ref[hh]
                st = lax.dot_general(kb, qb, NT, preferred_element_type=F32)
                pt = jnp.exp(st - lse8[hh:hh + 1, :])
                if masked:
                    keep = (lax.broadcasted_iota(jnp.int32, (tq, tq), 0)
                            <= lax.broadcasted_iota(jnp.int32, (tq, tq), 1))
                    pt = jnp.where(keep, pt, 0.0)
                accv_ref[hh] += jnp.dot(pt.astype(BF16), dob, preferred_element_type=F32)
                dpt = lax.dot_general(vs[hh], dob, NT, preferred_element_type=F32)
                dst = (pt * (dpt - dr8[hh:hh + 1, :])).astype(BF16)
                acck_ref[hh] += jnp.dot(dst, qb, preferred_element_type=F32)
                accq_ref[hh, pl.ds(start, tq), :] += lax.dot_general(dst, kb, TN, preferred_element_type=F32)

        def needed(hh, i):
            top = fs_ref[2 * pr + hh, jnp.minimum(i, nq - 1)] + bd_ref[0, 0]
            return jnp.logical_and(i < nq, top - fe_ref[2 * pr + hh, j] >= SKIP_P)

        def diagonal():
            h = tq // 2
            start = pl.multiple_of(j * tq, tq)
            dob = do_ref[pl.ds(start, tq), :]
            lse8 = lse_ref[0, j]
            dr8 = dr_ref[0, j]
            keep = lambda n: (lax.broadcasted_iota(jnp.int32, (h, n), 0) <= lax.broadcasted_iota(jnp.int32, (h, n), 1))
            for hh in (0, 1):
                qb = qa_ref[hh, pl.ds(start, tq), :]
                kb = ka_ref[hh]
                for k0, q0 in ((0, 0), (h, h)):
                    kp, vp, qp, dop = kb[k0:k0 + h], vs[hh][k0:k0 + h], qb[q0:], dob[q0:]
                    st = lax.dot_general(kp, qp, NT, preferred_element_type=F32)
                    pt = jnp.where(keep(tq - q0), jnp.exp(st - lse8[hh:hh + 1, q0:]), 0.0)
                    accv_ref[hh, k0:k0 + h, :] += jnp.dot(pt.astype(BF16), dop, preferred_element_type=F32)
                    dpt = lax.dot_general(vp, dop, NT, preferred_element_type=F32)
                    dst = (pt * (dpt - dr8[hh:hh + 1, q0:])).astype(BF16)
                    acck_ref[hh, k0:k0 + h, :] += jnp.dot(dst, qp, preferred_element_type=F32)
                    rows = pl.ds(pl.multiple_of(start + q0, h), tq - q0)
                    accq_ref[hh, rows, :] += lax.dot_general(dst, kp, TN, preferred_element_type=F32)

        diagonal()
        _block_loops(j + 1, 1, needed, q_step)
        dka_ref[...] = acck_ref[...]
        dv_ref[...] = jnp.where(sel_a, accv_ref[0], accv_ref[1]).astype(BF16)

        @pl.when(j == nq - 1)
        def _():
            out = pltpu.make_async_copy(accq_ref, dqa_hbm.at[pl.ds(2 * pr, 2)], out_sem)
            out.start()
            out.wait()

    once = pl.Buffered(1)
    return pl.pallas_call(
        body, name="attn_bwd", grid=(NH // 2, nq),
        out_shape=(_sds((NH, S, LANES), F32), _sds((NH, S, LANES), F32), _sds((S, AW), BF16)),
        in_specs=[SMEM_SPEC, SMEM_SPEC, SMEM_SPEC,
                  pl.BlockSpec((2, tq, LANES), lambda p, j: (p, j, 0)),
                  pl.BlockSpec((tq, LANES), lambda p, j: (j, p)),
                  pl.BlockSpec((2, S, LANES), lambda p, j: (p, 0, 0), pipeline_mode=once),
                  pl.BlockSpec((S, LANES), lambda p, j: (0, p), pipeline_mode=once),
                  pl.BlockSpec((1, nq, 8, tq), lambda p, j: (p, 0, 0, 0)),
                  pl.BlockSpec((1, nq, 8, tq), lambda p, j: (p, 0, 0, 0))],
        out_specs=(pl.BlockSpec(memory_space=pl.ANY),
                   pl.BlockSpec((2, tq, LANES), lambda p, j: (p, j, 0)),
                   pl.BlockSpec((tq, LANES), lambda p, j: (j, p))),
        scratch_shapes=[pltpu.VMEM((2, S, LANES), F32), pltpu.VMEM((2, tq, LANES), F32),
                        pltpu.VMEM((2, tq, LANES), F32), pltpu.SemaphoreType.DMA],
        compiler_params=_cp(("arbitrary", "arbitrary")),
    )(fs, fe, bd, ka, v, qa, do, lser, dr)


def _bwd_conv(du1, u0, alin, agate, w32, tm, ex=None):
    S = du1.shape[0]
    nt = S // tm

    def body(du_ref, dun_ref, u0_ref, u0p_ref, alin_ref, agate_ref, w_ref,
             dalin_ref, dagate_ref, dw_ref, db_ref, extd_ref, extu_ref, du0_ref, shd_ref, shu_ref):
        i = pl.program_id(0)

        @pl.when(i == 0)
        def _():
            dw_ref[...] = jnp.zeros_like(dw_ref)
            db_ref[...] = jnp.zeros_like(db_ref)

        extd_ref[0:tm, :] = du_ref[...]
        extd_ref[tm:, :] = jnp.where(i == nt - 1, 0.0, dun_ref[...])
        extu_ref[0:HALO, :] = jnp.where(i == 0, 0.0, u0p_ref[...])
        extu_ref[HALO:, :] = u0_ref[...]
        _shifted_copies(extd_ref, shd_ref, tm)
        _shifted_copies(extu_ref, shu_ref, tm)
        db_ref[...] += _colsum8(du_ref[...])
        for r0 in range(0, tm, BWD_CHUNK):
            acc = jnp.zeros((BWD_CHUNK, CW), F32)
            for j, rows in _tap_windows(extd_ref, shd_ref, r0, lambda j: 30 - j, BWD_CHUNK, shared=True):
                acc = acc + w_ref[j:j + 1, :] * rows
            du0_ref[r0:r0 + BWD_CHUNK, :] = acc
            duc = du_ref[r0:r0 + BWD_CHUNK, :]
            for j, rows in _tap_windows(extu_ref, shu_ref, r0, lambda j: 2 + j, BWD_CHUNK, shared=True):
                dw_ref[j] += _colsum8(duc * rows)
        du0 = du0_ref[...]
        al = alin_ref[...].astype(F32)
        sg = _sigmoid(agate_ref[...].astype(F32))
        dalin_ref[...] = (du0 * sg).astype(BF16)
        dagate_ref[...] = (du0 * al * sg * (1.0 - sg)).astype(BF16)

    nxt = pl.BlockSpec((HALO, CW), lambda i: (jnp.minimum((i + 1) * (tm // HALO), S // HALO - 1), 0))
    outs = pl.pallas_call(
        _hosted(body, ex, 7, 4, 5, nt), name="bwd_conv", grid=(nt,),
        out_shape=(_sds((S, CW), BF16), _sds((S, CW), BF16), _sds((HALO, 8, CW), F32), _sds((8, CW), F32),
                   *(ex.out_shapes if ex else [])),
        in_specs=[_rows(tm, CW), nxt, _rows(tm, CW), _halo_prev(tm), _rows(tm, CW), _rows(tm, CW), _const((HALO, CW)),
                  *(ex.in_specs if ex else [])],
        out_specs=(_rows(tm, CW), _rows(tm, CW), _const((HALO, 8, CW)), _const((8, CW)), *(ex.out_specs if ex else [])),
        scratch_shapes=[pltpu.VMEM((tm + HALO, CW), F32), pltpu.VMEM((tm + HALO, CW), F32), pltpu.VMEM((tm, CW), F32),
                        pltpu.VMEM((SUBLANES - 1, tm + HALO, CW), F32), pltpu.VMEM((SUBLANES - 1, tm + HALO, CW), F32),
                        *(ex.scratch if ex else [])],
        compiler_params=_cp(("arbitrary",)),
    )(du1, du1, u0, u0, alin, agate, w32, *(ex.srcs if ex else []))
    return outs[:4], list(outs[4:])


def _bwd_qk(dqa, dka, qn, kn, rq, rk, fgb, qg512, kg512, e512, et512, tm):
    S = qn.shape[0]
    nt = S // tm

    def body(dqa_ref, dka_ref, qn_ref, kn_ref, rq_ref, rk_ref, fgb_ref, qg_ref, kg_ref, e_ref, et_ref,
             dq_ref, dk_ref, dfg_ref, accg_ref, accb_ref, carry_ref):
        @pl.when(pl.program_id(0) == 0)
        def _():
            carry_ref[...] = jnp.zeros_like(carry_ref)
            accg_ref[...] = jnp.zeros_like(accg_ref)
            accb_ref[...] = jnp.zeros_like(accb_ref)

        lane = _lane((tm, LANES))
        sel_a = lane < DH
        df = jnp.zeros((tm, LANES), F32)
        for h in range(NH):
            col = dqa_ref[h][:, 64:65] - dka_ref[h][:, 67:68]
            df = jnp.where(lane == h, col, df)
        tri = (lax.broadcasted_iota(jnp.int32, (tm, tm), 0) <= lax.broadcasted_iota(jnp.int32, (tm, tm), 1)
               ).astype(F32).astype(BF16)
        dlf = _dot3(tri, df) + carry_ref[0:1, :]
        carry_ref[...] = jnp.broadcast_to(dlf[0:1, :], carry_ref.shape)
        dfg = jnp.where(lane < NH, dlf * _sigmoid(-fgb_ref[...]), 0.0)
        dfg_ref[...] = dfg.astype(BF16)
        accb_ref[...] += _colsum8(dfg)

        def norm_bwd(src_ref, n_ref, r_ref, g_ref, scale, slot):
            pairs = []
            for p in range(NH // 2):
                b = pltpu.roll(src_ref[2 * p + 1], 64, 1)
                pairs.append(jnp.where(sel_a, src_ref[2 * p], b))
            dh = jnp.concatenate(pairs, axis=1) * scale
            tn = n_ref[...].astype(F32)
            accg_ref[slot] += _colsum8(dh * tn)
            dn = dh * g_ref[...]
            mean = _dot2(dn * tn, e_ref[...]) * (1.0 / DH)
            corr = _dot2(mean, et_ref[...])
            rf = _dot2(r_ref[...], et_ref[...])
            return (rf * (dn - tn * corr)).astype(BF16)

        dq_ref[...] = norm_bwd(dqa_ref, qn_ref, rq_ref, qg_ref, DH ** -0.5, 0)
        dk_ref[...] = norm_bwd(dka_ref, kn_ref, rk_ref, kg_ref, 1.0, 1)

    rev = lambda n: pl.BlockSpec((tm, n), lambda i: (nt - 1 - i, 0))
    hm = pl.BlockSpec((NH, tm, LANES), lambda i: (0, nt - 1 - i, 0))
    dq, dk, dfg, accg, accb = pl.pallas_call(
        body, name="bwd_qk", grid=(nt,),
        out_shape=(_sds((S, AW), BF16), _sds((S, AW), BF16), _sds((S, LANES), BF16), _sds((2, 8, AW), F32),
                   _sds((8, LANES), F32)),
        in_specs=[hm, hm, rev(AW), rev(AW), rev(LANES), rev(LANES), rev(LANES), _const((1, AW)), _const((1, AW)),
                  _const((AW, LANES)), _const((LANES, AW))],
        out_specs=(rev(AW), rev(AW), rev(LANES), _const((2, 8, AW)), _const((8, LANES))),
        scratch_shapes=[pltpu.VMEM((8, LANES), F32)], compiler_params=_cp(("arbitrary",)),
    )(dqa, dka, qn, kn, rq, rk, fgb, qg512, kg512, e512, et512)
    return dq, dk, dfg, accg, accb


def _bwd_in(dq, dk, dv, dalin, dagate, dfg, w_in_p, x, dx2, mod8, n1g, tm, ex=None):
    S = x.shape[0]

    def body(dq_ref, dk_ref, dv_ref, dal_ref, dag_ref, dfg_ref, w_ref, x_ref, dx2_ref, mod_ref, n1g_ref,
             dx_ref, acc_ref):
        @pl.when(pl.program_id(0) == 0)
        def _():
            acc_ref[...] = jnp.zeros_like(acc_ref)

        def part(ref, a, b):
            return lax.dot_general(ref[...], w_ref[:, a:b], NT, preferred_element_type=F32)

        dh = (part(dq_ref, 0, 512) + part(dk_ref, 512, 1024) + part(dv_ref, 1024, 1536) + part(dal_ref, 1536, 2048)
              + part(dag_ref, 2048, 2560) + part(dfg_ref, 2560, NP))
        xv = x_ref[...]
        r1 = lax.rsqrt(jnp.mean(xv * xv, axis=-1, keepdims=True) + EPS)
        xn = xv * r1
        gain = n1g_ref[...] * (1.0 + mod_ref[1:2, :])
        t = dh * xn
        acc_ref[0] += _colsum8(dh)
        acc_ref[1] += _colsum8(t * n1g_ref[...])
        acc_ref[2] += _colsum8(t * (1.0 + mod_ref[1:2, :]))
        dxn = dh * gain
        dx_ref[...] = dx2_ref[...] + r1 * (dxn - xn * jnp.mean(dxn * xn, axis=-1, keepdims=True))

    outs = pl.pallas_call(
        _hosted(body, ex, 11, 2, 0, S // tm), name="bwd_in", grid=(S // tm,),
        out_shape=(_sds((S, D), F32), _sds((3, 8, D), F32), *(ex.out_shapes if ex else [])),
        in_specs=[_rows(tm, AW), _rows(tm, AW), _rows(tm, AW), _rows(tm, CW), _rows(tm, CW), _rows(tm, LANES),
                  _const((D, NP)), _rows(tm, D), _rows(tm, D), _const((8, D)), _const((1, D)),
                  *(ex.in_specs if ex else [])],
        out_specs=(_rows(tm, D), _const((3, 8, D)), *(ex.out_specs if ex else [])),
        scratch_shapes=ex.scratch if ex else [],
        compiler_params=_cp(("arbitrary",)),
    )(dq, dk, dv, dalin, dagate, dfg, w_in_p, x, dx2, mod8, n1g, *(ex.srcs if ex else []))
    return outs[:2], list(outs[2:])


def _adam(w, g, m, v):
    m_new = B1 * m + (1.0 - B1) * g
    v_new = B2 * v + (1.0 - B2) * (g * g)
    m_hat = m_new / (1.0 - B1 ** STEP)
    v_hat = v_new / (1.0 - B2 ** STEP)
    delta = -LR * (m_hat / (jnp.sqrt(v_hat) + AEPS) + WD * w)
    return delta, m_new, v_new


def _pair_adamw(slots, w, m, v, name, tr=256):
    ns, R, C = slots.shape
    tr = tr if R % tr == 0 else R
    nt = R // tr

    def body(s_ref, w_ref, m_ref, v_ref, g_ref, d_ref, mo_ref, vo_ref, mine_ref, theirs_ref, send_sems, recv_sems):
        i = pl.program_id(0)
        part = s_ref[0].astype(F32)
        for k in range(1, ns):
            part = part + s_ref[k].astype(F32)
        mine_ref[i] = part
        swap = pltpu.make_async_remote_copy(
            src_ref=mine_ref.at[i], dst_ref=theirs_ref.at[i], send_sem=send_sems.at[i], recv_sem=recv_sems.at[i],
            device_id=(lax.axis_index("x"), lax.axis_index("y"), 1 - lax.axis_index("c")),
            device_id_type=pl.DeviceIdType.MESH)
        swap.start()
        swap.wait()
        g = part + theirs_ref[i]
        g_ref[...] = g
        d_ref[...], mo_ref[...], vo_ref[...] = _adam(w_ref[...], g, m_ref[...], v_ref[...])

    blk = pl.BlockSpec((tr, C), lambda i: (i, 0))
    return pl.pallas_call(
        body, name=name, grid=(nt,), out_shape=tuple(_sds((R, C), F32) for _ in range(4)),
        in_specs=[pl.BlockSpec((ns, tr, C), lambda i: (0, i, 0)), blk, blk, blk], out_specs=(blk, blk, blk, blk),
        scratch_shapes=[pltpu.VMEM((nt, tr, C), F32), pltpu.VMEM((nt, tr, C), F32),
                        pltpu.SemaphoreType.DMA((nt,)), pltpu.SemaphoreType.DMA((nt,))],
        compiler_params=_cp(("arbitrary",)),
    )(slots, w, m, v)


def _ada_adamw(sct, dmod, w, m, v):
    R, C = w.shape
    tr, bc = 256, 512

    def body(sct_ref, dm_ref, w_ref, m_ref, v_ref, g_ref, d_ref, mo_ref, vo_ref):
        g = sct_ref[:, 0:1] * dm_ref[0:1, :]
        for b in range(1, N_DEV):
            g = g + sct_ref[:, b:b + 1] * dm_ref[b:b + 1, :]
        g_ref[...] = g
        d_ref[...], mo_ref[...], vo_ref[...] = _adam(w_ref[...], g, m_ref[...], v_ref[...])

    blk = pl.BlockSpec((tr, bc), lambda i, j: (i, j))
    return pl.pallas_call(
        body, name="ada_adamw", grid=(R // tr, C // bc), out_shape=tuple(_sds((R, C), F32) for _ in range(4)),
        in_specs=[pl.BlockSpec((tr, N_DEV), lambda i, j: (i, 0)), pl.BlockSpec((N_DEV, bc), lambda i, j: (0, j)),
                  blk, blk, blk],
        out_specs=(blk, blk, blk, blk), compiler_params=_cp(("parallel", "parallel")),
    )(sct, dmod, w, m, v)


PACK = {"dmod": 0, "norm1_g": 6144, "norm2_g": 7168, "q_norm_g": 8192, "k_norm_g": 8704, "b_f": 9216, "conv_b": 9344,
        "conv_ln_g": 9856, "conv_ln_b": 10368, "beta_attn": 10880, "beta_conv": 11392, "loss": 11904}
SMALL_NAMES = ["b_ada", "norm1_g", "q_norm_g", "k_norm_g", "b_f", "conv_b", "conv_ln_g", "conv_ln_b", "beta_attn",
               "beta_conv", "norm2_g"]


def _pack_small(acc1, acc_d, acc_h, dg2, accg, accb, dcb, loss8):
    def body(a1_ref, ad_ref, ah_ref, dg2_ref, ag_ref, ab_ref, cb_ref, loss_ref, o_ref):
        def put(off, rows):
            o_ref[:, off:off + rows.shape[1]] = jnp.sum(rows, axis=0, keepdims=True)

        for t, rows in enumerate((a1_ref[0], a1_ref[1], ad_ref[3], ad_ref[0], ad_ref[1], dg2_ref[...])):
            put(PACK["dmod"] + t * D, rows)
        put(PACK["norm1_g"], a1_ref[2])
        put(PACK["norm2_g"], ad_ref[2])
        put(PACK["q_norm_g"], ag_ref[0])
        put(PACK["k_norm_g"], ag_ref[1])
        put(PACK["b_f"], ab_ref[...])
        put(PACK["conv_b"], cb_ref[...])
        put(PACK["conv_ln_g"], ah_ref[2])
        put(PACK["conv_ln_b"], ah_ref[3])
        put(PACK["beta_attn"], ah_ref[0])
        put(PACK["beta_conv"], ah_ref[1])
        o_ref[:, PACK["loss"]:PACK["loss"] + LANES] = loss_ref[0:1, :]

    vm = pl.BlockSpec(memory_space=pltpu.VMEM)
    return pl.pallas_call(body, name="pack_small", out_shape=_sds((1, SMALL_IN), F32), in_specs=[vm] * 8, out_specs=vm,
                          )(acc1, acc_d, acc_h, dg2, accg, accb, dcb, loss8)


def _small_adamw(slots, fold, ws, ms, vs):
    n = len(ws)
    widths = [w.shape[1] for w in ws]

    def body(s_ref, f_ref, *refs):
        w_refs, m_refs, v_refs = refs[:n], refs[n:2 * n], refs[2 * n:3 * n]
        outs = refs[3 * n:]
        tot = s_ref[0:1, :]
        for k in range(1, N_DEV):
            tot = tot + s_ref[k:k + 1, :]

        def grad(name, width):
            if name == "b_ada":
                return tot[:, 0:6 * D]
            seg = tot[:, PACK[name]:PACK[name] + max(width, LANES)]
            if name in ("q_norm_g", "k_norm_g"):
                seg = jnp.dot(jnp.broadcast_to(tot[:, PACK[name]:PACK[name] + AW], (8, AW)), f_ref[...], precision=HI,
                              preferred_element_type=F32)[0:1, :]
            return seg[:, 0:width]

        for t, (name, width) in enumerate(zip(SMALL_NAMES, widths)):
            g = grad(name, width)
            d, m_new, v_new = _adam(w_refs[t][...], g, m_refs[t][...], v_refs[t][...])
            outs[t][...] = g
            outs[n + t][...] = d
            outs[2 * n + t][...] = m_new
            outs[3 * n + t][...] = v_new
        outs[4 * n][...] = tot[:, PACK["loss"]:PACK["loss"] + LANES]

    vm = pl.BlockSpec(memory_space=pltpu.VMEM)
    outs = pl.pallas_call(
        body, name="adamw_small",
        out_shape=(*(_sds((1, w), F32) for _ in range(4) for w in widths), _sds((1, LANES), F32)),
        in_specs=[vm] * (2 + 3 * n), out_specs=tuple(vm for _ in range(4 * n + 1)),
    )(slots, fold, *ws, *ms, *vs)
    return [list(outs[k * n:(k + 1) * n]) for k in range(4)], outs[4 * n][0, 0]


def _perm_in(w):
    pad = jnp.zeros((w.shape[0], NP - 2568), w.dtype)
    return jnp.concatenate([w[:, :1536], w[:, 1544:2568], w[:, 1536:1544], pad], axis=1)


def _pad_lanes(vec, n=LANES):
    return jnp.pad(vec, ((0, 0), (0, n - vec.shape[1])))


def kernel(x, c, w_ada, b_ada, norm1_g, w_in, q_norm_g, k_norm_g, b_f, conv_w, conv_b, conv_ln_g, conv_ln_b, beta_attn, beta_conv, w_out, norm2_g, w_ff1, w_ff2, loss_target, m_w_ada, m_b_ada, m_norm1_g, m_w_in, m_q_norm_g, m_k_norm_g, m_b_f, m_conv_w, m_conv_b, m_conv_ln_g, m_conv_ln_b, m_beta_attn, m_beta_conv, m_w_out, m_norm2_g, m_w_ff1, m_w_ff2, v_w_ada, v_b_ada, v_norm1_g, v_w_in, v_q_norm_g, v_k_norm_g, v_b_f, v_conv_w, v_conv_b, v_conv_ln_g, v_conv_ln_b, v_beta_attn, v_beta_conv, v_w_out, v_norm2_g, v_w_ff1, v_w_ff2):
    S = x.shape[1]
    tm = min(256, S)
    tw = min(512, S)
    tq = min(512, S // 2)
    xs, tgt = x[0], loss_target[0]
    chip = 2 * lax.axis_index("x") + lax.axis_index("y")
    e512, et512 = _head_sum_mats()

    conv_w32 = jnp.pad(conv_w[0], ((0, 1), (0, 0)))
    c_all, g_in = _exchange([(c, "bcast8"), (w_in[0].astype(BF16), "chip4")], "gather_in")
    later_weights = _Exchange([(w_out[0].astype(BF16), "chip4"), (conv_w32, "chip4")])
    c_all = c_all.reshape(N_DEV, D)
    w_in_p = _perm_in(jnp.transpose(g_in, (1, 0, 2)).reshape(D, 2568))

    b_shard = lax.dynamic_slice(b_ada, (0, chip * 1536), (1, 1536))
    mod_rows, sc_all = _mod_shard(c_all, w_ada[0], b_shard)
    (mod_slots,) = _exchange([(mod_rows.reshape(N_DEV, 1, 1536), "all8")], "scatter_mod")
    mod = mod_slots.reshape(4, 2, 1536)[:, 0, :].reshape(6, D)
    mod8 = jnp.pad(mod, ((0, 2), (0, 0)))

    qg512 = jnp.tile(q_norm_g, (1, NH))
    kg512 = jnp.tile(k_norm_g, (1, NH))
    bf128 = _pad_lanes(b_f)

    (h1, qh, kh, vb, qn, kn, rq, rk, fgb, alin, agate, u0), (g_out, g_cw) = _fwd_in(
        xs, mod8, norm1_g, w_in_p, e512, et512, qg512, kg512, bf128, tw, ex=later_weights)
    w_out_f = g_out.reshape(D, D)
    cw32 = jnp.transpose(g_cw, (1, 0, 2)).reshape(HALO, CW)
    bd = _logit_bound(q_norm_g, k_norm_g)
    (qa, ka, va, fcum), (w1,) = _fwd_decay(fgb, qh, kh, vb, _shift(bd), tm,
                                           ex=_Exchange([(w_ff1[0].astype(BF16), "chip4")]))
    fs, fe = _skip_tables(fcum, tq)
    o_attn, lser = _attn_fwd(qa, ka, va, fs, fe, bd, tq)
    (u1, mc), (w2,) = _fwd_conv(u0, cw32, conv_b, conv_ln_g, conv_ln_b, beta_conv, tm,
                                ex=_Exchange([(w_ff2[0].astype(BF16), "chip4")]))
    merged, ob, x2, h2 = _fwd_out(o_attn, mc, xs, mod8, norm2_g, beta_attn, w_out_f, tw)
    r, dy, loss8, dg2 = _fwd_ffn(h2, w1, w2, x2, tgt, mod8, tw)

    df2, df1, dh2 = _bwd_ffn(dy, mod8, r, w1, w2, tw)
    gw_ff2 = _wgrad(r, df2, "wgrad_ff2", square_a=True)
    gw_ff1 = _wgrad(h2, df1, "wgrad_ff1", col_pieces=True)
    (dx2, do, doa, du1, acc_d, acc_h, dr), (p_ff1,) = _bwd_mid(
        dh2, dy, x2, ob, o_attn, u1, mod8, norm2_g, beta_attn, beta_conv, conv_ln_g, conv_ln_b, w_out_f, e512, tm, tq,
        ex=_Exchange([(gw_ff1, "chip4p")]))
    gw_out = _wgrad(merged, do, "wgrad_out")
    dqa, dka, dv = _attn_bwd(qa, ka, vb, doa, lser, dr, fs, fe, bd, tq)
    (dalin, dagate, dcw, dcb), (p_out, p_ff2) = _bwd_conv(
        du1, u0, alin, agate, cw32, tm,
        ex=_Exchange([(gw_out.reshape(4, 256, D), "chip4p"), (gw_ff2.reshape(4, D, D), "chip4p")]))
    dq, dk, dfg, accg, accb = _bwd_qk(dqa, dka, qn, kn, rq, rk, fgb, qg512, kg512, e512, et512, tm)
    gw_in_p = _wgrad_in(h1, [dq, dk, dv, dalin, dagate, dfg])
    gw_in = jnp.concatenate([gw_in_p[:, :1536], gw_in_p[:, 2560:2568], gw_in_p[:, 1536:2560]], axis=1)
    s8 = lambda a: jnp.sum(a, axis=-2)
    gcw = s8(dcw)
    in_grads = _Exchange([(jnp.transpose(gw_in.reshape(D, 4, 642), (1, 0, 2)), "chip4p"),
                          (jnp.transpose(gcw.reshape(HALO, 4, LANES), (1, 0, 2)), "chip4p")])
    (grad_x, acc1), (p_in, p_cw) = _bwd_in(dq, dk, dv, dalin, dagate, dfg, w_in_p, xs, dx2, mod8, norm1_g, tw,
                                           ex=in_grads)

    small = _pack_small(acc1, acc_d, acc_h, dg2, accg, accb, dcb, loss8)
    (small_s,) = _exchange([(small, "bcast8")], "gather_small")
    small_s = small_s.reshape(N_DEV, SMALL_IN)

    g_in_, d_in, nm_in, nv_in = _pair_adamw(p_in, w_in[0], m_w_in[0], v_w_in[0], "adamw_in")
    g_out_, d_out, nm_out, nv_out = _pair_adamw(p_out, w_out[0], m_w_out[0], v_w_out[0], "adamw_out")
    g_f1, d_f1, nm_f1, nv_f1 = _pair_adamw(p_ff1, w_ff1[0], m_w_ff1[0], v_w_ff1[0], "adamw_ff1")
    g_f2, d_f2, nm_f2, nv_f2 = _pair_adamw(p_ff2, w_ff2[0], m_w_ff2[0], v_w_ff2[0], "adamw_ff2")
    pad_row = lambda a, fill: jnp.pad(a[0], ((0, 1), (0, 0)), constant_values=fill)
    g_cw_, d_cw, nm_cw, nv_cw = (a[:KC] for a in _pair_adamw(
        p_cw, pad_row(conv_w, 0.0), pad_row(m_conv_w, 0.0), pad_row(v_conv_w, 1.0), "adamw_conv_w"))
    dmod_shard = lax.dynamic_slice(small_s[:, :6 * D], (0, chip * 1536), (N_DEV, 1536))
    g_ada, d_ada, nm_ada, nv_ada = _ada_adamw(sc_all.T, dmod_shard, w_ada[0], m_w_ada[0], v_w_ada[0])

    fold = np.zeros((AW, LANES), np.float32)
    fold[np.arange(AW), np.arange(AW) % DH] = 1.0
    smalls = [b_ada, norm1_g, q_norm_g, k_norm_g, b_f, conv_b, conv_ln_g, conv_ln_b, beta_attn, beta_conv, norm2_g]
    m_smalls = [m_b_ada, m_norm1_g, m_q_norm_g, m_k_norm_g, m_b_f, m_conv_b, m_conv_ln_g, m_conv_ln_b, m_beta_attn,
                m_beta_conv, m_norm2_g]
    v_smalls = [v_b_ada, v_norm1_g, v_q_norm_g, v_k_norm_g, v_b_f, v_conv_b, v_conv_ln_g, v_conv_ln_b, v_beta_attn,
                v_beta_conv, v_norm2_g]
    (gs, ds, ms, vs), loss = _small_adamw(small_s, jnp.asarray(fold), smalls, m_smalls, v_smalls)

    big ={"w_ada": (g_ada, d_ada, nm_ada, nv_ada), "w_in": (g_in_, d_in, nm_in, nv_in),
           "conv_w": (g_cw_, d_cw, nm_cw, nv_cw), "w_out": (g_out_, d_out, nm_out, nv_out),
           "w_ff1": (g_f1, d_f1, nm_f1, nv_f1), "w_ff2": (g_f2, d_f2, nm_f2, nv_f2)}
    order =["w_ada", "b_ada", "norm1_g", "w_in", "q_norm_g", "k_norm_g", "b_f", "conv_w", "conv_b", "conv_ln_g",
             "conv_ln_b", "beta_attn", "beta_conv", "w_out", "norm2_g", "w_ff1", "w_ff2"]

    def leaf(name, which):
        if name in big:
            return big[name][which][None]
        return (gs, ds, ms, vs)[which][SMALL_NAMES.index(name)]

    return (loss, grad_x[None], *[leaf(n, 0) for n in order], *[leaf(n, 1) for n in order],
            *[leaf(n, 2) for n in order], *[leaf(n, 3) for n in order])
```

```python
import functools

import numpy as np
import jax
import jax.numpy as jnp
from jax import lax
from jax.experimental import pallas as pl
from jax.experimental.pallas import tpu as pltpu

F32, BF16 = jnp.float32, jnp.bfloat16
HI = lax.Precision.HIGHEST
D = 1024
AW = 512
CW = 512
NH = 8
DH = 64
KC = 31
DFF = 4096
NP = 2688
EPS = 1e-6
NEG = -1e30
LANES = 128
VMEM_LIMIT = 56 * 2**20
NT = (((1,), (1,)), ((), ()))
TN = (((0,), (0,)), ((), ()))
LR, B1, B2, AEPS, WD, STEP = 0.001, 0.9, 0.999, 1e-08, 0.01, 10
N_DEV = 8
SMALL_IN = 12032


def _cp(sem=None, vmem=VMEM_LIMIT):
    kw = dict(vmem_limit_bytes=vmem)
    if sem is not None:
        kw["dimension_semantics"] = sem
    return pltpu.CompilerParams(**kw)


def _rows(tm, n):
    return pl.BlockSpec((tm, n), lambda i: (i, 0))


def _const(shape):
    nd = len(shape)
    return pl.BlockSpec(shape, lambda *_: (0,) * nd)


def _sds(shape, dt):
    return jax.ShapeDtypeStruct(shape, dt)


def _lane(shape):
    return lax.broadcasted_iota(jnp.int32, shape, len(shape) - 1)


def _sigmoid(x):
    return 1.0 / (1.0 + jnp.exp(-x))


class _Exchange:
    MASKS = {"chip4": (2, 4, 6), "chip4p": (2, 4, 6), "all8": (1, 2, 3, 4, 5, 6, 7), "bcast8": (1, 2, 3, 4, 5, 6, 7)}

    def __init__(self, items):
        self.srcs = [s for s, _ in items]
        self.kinds = [k for _, k in items]
        self.n = len(items)
        self.out_shapes = []
        for s, k in items:
            shape = {"all8": (N_DEV,) + s.shape[1:], "bcast8": (N_DEV,) + s.shape, "chip4": (4,) + s.shape,
                     "chip4p": (4,) + s.shape[1:]}[k]
            self.out_shapes.append(_sds(shape, s.dtype))
        self.sem_index = {}
        for t, k in enumerate(self.kinds):
            for m in self.MASKS[k]:
                self.sem_index[(t, m)] = len(self.sem_index)
        n_sem = len(self.sem_index)
        self.scratch = [pltpu.SemaphoreType.DMA((n_sem,)), pltpu.SemaphoreType.DMA((n_sem,)),
                        pltpu.SemaphoreType.DMA((self.n,))]
        self.in_specs = [pl.BlockSpec(memory_space=pl.ANY)] * self.n
        self.out_specs = [pl.BlockSpec(memory_space=pl.ANY)] * self.n

    def copies(self, src_refs, dst_refs, send_sems, recv_sems, local_sems):
        x, y, c = lax.axis_index("x"), lax.axis_index("y"), lax.axis_index("c")
        my_id = 4 * x + 2 * y + c
        my_chip = 2 * x + y

        def piece(t, dev_id, chip):
            k = self.kinds[t]
            return src_refs[t].at[dev_id] if k == "all8" else src_refs[t].at[chip] if k == "chip4p" else src_refs[t]

        out = []
        for t in range(self.n):
            slot = dst_refs[t].at[my_chip if self.kinds[t] in ("chip4", "chip4p") else my_id]
            out.append(pltpu.make_async_copy(piece(t, my_id, my_chip), slot, local_sems.at[t]))
            for m in self.MASKS[self.kinds[t]]:
                px = 1 - x if m & 4 else x
                py = 1 - y if m & 2 else y
                pc = 1 - c if m & 1 else c
                s = self.sem_index[(t, m)]
                out.append(pltpu.make_async_remote_copy(
                    src_ref=piece(t, 4 * px + 2 * py + pc, 2 * px + py), dst_ref=slot,
                    send_sem=send_sems.at[s], recv_sem=recv_sems.at[s],
                    device_id=(px, py, pc), device_id_type=pl.DeviceIdType.MESH))
        return out


def _hosted(body, ex, n_in, n_out, n_scr, n_steps):
    if ex is None:
        return body

    def wrapped(*refs):
        ins, xin = refs[:n_in], refs[n_in:n_in + ex.n]
        o0 = n_in + ex.n
        outs, xout = refs[o0:o0 + n_out], refs[o0 + n_out:o0 + n_out + ex.n]
        s0 = o0 + n_out + ex.n
        scr, sems = refs[s0:s0 + n_scr], refs[s0 + n_scr:]

        @pl.when(pl.program_id(0) == 0)
        def _():
            for cp in ex.copies(xin, xout, *sems):
                cp.start()

        body(*ins, *outs, *scr)

        @pl.when(pl.program_id(0) == n_steps - 1)
        def _():
            for cp in ex.copies(xin, xout, *sems):
                cp.wait()

    return wrapped


def _exchange(items, name):
    ex = _Exchange(items)
    n = ex.n

    def body(*refs):
        copies = ex.copies(refs[:n], refs[n:2 * n], *refs[2 * n:])
        for cp in copies:
            cp.start()
        for cp in copies:
            cp.wait()

    outs = pl.pallas_call(
        body, name=name, out_shape=tuple(ex.out_shapes), in_specs=ex.in_specs, out_specs=tuple(ex.out_specs),
        scratch_shapes=ex.scratch,
    )(*ex.srcs)
    return list(outs)


def _mod_shard(c_all, w_ada, b_shard):
    n = w_ada.shape[1]

    def body(c_ref, w_ref, b_ref, o_ref, sc_ref):
        cv = c_ref[...]
        sc = cv * _sigmoid(cv)
        sc_ref[...] = sc
        o_ref[...] = jnp.dot(sc, w_ref[...], precision=HI, preferred_element_type=F32) + b_ref[...]

    bn = 512
    return pl.pallas_call(
        body, name="mod_shard", out_shape=(_sds((N_DEV, n), F32), _sds((N_DEV, D), F32)), grid=(n // bn,),
        in_specs=[_const((N_DEV, D)), pl.BlockSpec((D, bn), lambda j: (0, j)), pl.BlockSpec((1, bn), lambda j: (0, j))],
        out_specs=(pl.BlockSpec((N_DEV, bn), lambda j: (0, j)), _const((N_DEV, D))),
        compiler_params=_cp(("arbitrary",)),
    )(c_all, w_ada, b_shard)


def _head_sum_mats():
    e = np.zeros((AW, LANES), np.float32)
    for h in range(NH):
        e[h * DH:(h + 1) * DH, h] = 1.0
    return jnp.asarray(e, BF16), jnp.asarray(e.T.copy(), BF16)


def _dot2(x, w):
    hi = x.astype(BF16)
    lo = (x - hi.astype(F32)).astype(BF16)
    return jnp.dot(hi, w, preferred_element_type=F32) + jnp.dot(lo, w, preferred_element_type=F32)


def _fwd_in(x, mod8, n1g, w_in_p, e512, et512, qg512, kg512, bf128, tm, ex=None):
    S = x.shape[0]

    def body(x_ref, mod_ref, n1g_ref, w_ref, e_ref, et_ref, qg_ref, kg_ref, bf_ref,
             h1_ref, qh_ref, kh_ref, v_ref, qn_ref, kn_ref, rq_ref, rk_ref, fgb_ref, alin_ref, agate_ref, u0_ref):
        xv = x_ref[...]
        r1 = lax.rsqrt(jnp.mean(xv * xv, axis=-1, keepdims=True) + EPS)
        h = (xv * r1) * (n1g_ref[...] * (1.0 + mod_ref[1:2, :])) + mod_ref[0:1, :]
        hb = h.astype(BF16)
        h1_ref[...] = hb

        def seg(a, b):
            return jnp.dot(hb, w_ref[:, a:b], preferred_element_type=F32)

        def headnorm(t, g_ref, scale, n_ref, r_ref, o_ref):
            ss = _dot2(t * t, e_ref[...])
            r = lax.rsqrt(ss * (1.0 / DH) + EPS)
            tn = t * _dot2(r, et_ref[...])
            n_ref[...] = tn.astype(BF16)
            r_ref[...] = r
            o_ref[...] = (tn * (g_ref[...] * scale)).astype(BF16)

        headnorm(seg(0, 512), qg_ref, DH ** -0.5, qn_ref, rq_ref, qh_ref)
        headnorm(seg(512, 1024), kg_ref, 1.0, kn_ref, rk_ref, kh_ref)
        v_ref[...] = seg(1024, 1536).astype(BF16)
        alin = seg(1536, 2048)
        agate = seg(2048, 2560)
        alin_ref[...] = alin.astype(BF16)
        agate_ref[...] = agate.astype(BF16)
        u0_ref[...] = alin * _sigmoid(agate)
        fgb_ref[...] = seg(2560, NP) + bf_ref[...]

    bf = lambda: _sds((S, AW), BF16)
    xs = ex.srcs if ex else []
    outs = pl.pallas_call(
        _hosted(body, ex, 9, 12, 0, S // tm), name="fwd_in", grid=(S // tm,),
        out_shape=(_sds((S, D), BF16), bf(), bf(), bf(), bf(), bf(), _sds((S, LANES), F32), _sds((S, LANES), F32),
                   _sds((S, LANES), F32), bf(), bf(), _sds((S, CW), F32), *(ex.out_shapes if ex else [])),
        in_specs=[_rows(tm, D), _const((8, D)), _const((1, D)), _const((D, NP)), _const((AW, LANES)), _const((LANES, AW)),
                  _const((1, AW)), _const((1, AW)), _const((1, LANES)), *(ex.in_specs if ex else [])],
        out_specs=(_rows(tm, D), _rows(tm, AW), _rows(tm, AW), _rows(tm, AW), _rows(tm, AW), _rows(tm, AW),
                   _rows(tm, LANES), _rows(tm, LANES), _rows(tm, LANES), _rows(tm, AW), _rows(tm, AW), _rows(tm, CW),
                   *(ex.out_specs if ex else [])),
        scratch_shapes=ex.scratch if ex else [],
        compiler_params=_cp(("arbitrary",)),
    )(x, mod8, n1g, w_in_p, e512, et512, qg512, kg512, bf128, *xs)
    return outs[:12], list(outs[12:])


def _split3(f):
    f1 = f.astype(BF16).astype(F32)
    f2 = (f - f1).astype(BF16).astype(F32)
    return f1, f2, f - f1 - f2


def _dot3(w, x):
    return sum(jnp.dot(w, piece.astype(BF16), preferred_element_type=F32) for piece in _split3(x))


def _fwd_decay(fgb, qh, kh, vb, shift, tm, ex=None):
    S = fgb.shape[0]

    def body(shift_ref, fgb_ref, qh_ref, kh_ref, vb_ref, qa_ref, ka_ref, va_ref, f_ref, carry_ref):
        @pl.when(pl.program_id(0) == 0)
        def _():
            carry_ref[...] = jnp.zeros_like(carry_ref)

        fb = fgb_ref[...]
        lf = jnp.minimum(fb, 0.0) - jnp.log1p(jnp.exp(-jnp.abs(fb)))
        tri = (lax.broadcasted_iota(jnp.int32, (tm, tm), 0) >= lax.broadcasted_iota(jnp.int32, (tm, tm), 1)
               ).astype(F32).astype(BF16)
        cs = _dot3(tri, lf) + carry_ref[0:1, :]
        f_ref[...] = cs
        carry_ref[...] = jnp.broadcast_to(cs[tm - 1:tm, :], carry_ref.shape)
        lane = _lane((tm, LANES))
        s1, s2, s3 = _split3(jnp.zeros((tm, LANES), F32) - shift_ref[0, 0])
        tail_q = jnp.where((lane >= 67) & (lane < 70), 1.0,
                           jnp.where(lane == 70, s1, jnp.where(lane == 71, s2, jnp.where(lane == 72, s3, 0.0))))
        tail_k = jnp.where(((lane >= 64) & (lane < 67)) | ((lane >= 70) & (lane < 73)), 1.0, 0.0)
        tail_v = jnp.where(lane == DH, 1.0, 0.0)
        for p in range(NH // 2):
            qp = qh_ref[:, p * LANES:(p + 1) * LANES].astype(F32)
            kp = kh_ref[:, p * LANES:(p + 1) * LANES].astype(F32)
            vp = vb_ref[:, p * LANES:(p + 1) * LANES].astype(F32)
            for hh in range(2):
                h = 2 * p + hh
                f1, f2, f3 = _split3(cs[:, h:h + 1])
                qb = qp if hh == 0 else pltpu.roll(qp, 64, 1)
                kb = kp if hh == 0 else pltpu.roll(kp, 64, 1)
                vh = vp if hh == 0 else pltpu.roll(vp, 64, 1)
                augq = jnp.where(lane == 64, f1, jnp.where(lane == 65, f2, jnp.where(lane == 66, f3, tail_q)))
                augk = jnp.where(lane == 67, -f1, jnp.where(lane == 68, -f2, jnp.where(lane == 69, -f3, tail_k)))
                qa_ref[h] = jnp.where(lane < DH, qb, augq).astype(BF16)
                ka_ref[h] = jnp.where(lane < DH, kb, augk).astype(BF16)
                va_ref[h] = jnp.where(lane < DH, vh, tail_v).astype(BF16)

    hm = pl.BlockSpec((NH, tm, LANES), lambda i: (0, i, 0))
    hms = _sds((NH, S, LANES), BF16)
    outs = pl.pallas_call(
        _hosted(body, ex, 5, 4, 1, S // tm), name="fwd_decay", grid=(S // tm,),
        out_shape=(hms, hms, hms, _sds((S, LANES), F32), *(ex.out_shapes if ex else [])),
        in_specs=[SMEM_SPEC, _rows(tm, LANES), _rows(tm, AW), _rows(tm, AW), _rows(tm, AW),
                  *(ex.in_specs if ex else [])],
        out_specs=(hm, hm, hm, _rows(tm, LANES), *(ex.out_specs if ex else [])),
        scratch_shapes=[pltpu.VMEM((8, LANES), F32), *(ex.scratch if ex else [])],
        compiler_params=_cp(("arbitrary",)),
    )(shift, fgb, qh, kh, vb, *(ex.srcs if ex else []))
    return outs[:4], list(outs[4:])


SKIP = -106.0
SKIP_P = -88.0


def _block_loops(first, step, needed, run):
    def both(j):
        return jnp.logical_and(needed(0, j), needed(1, j))

    def walk(heads):
        def go(j):
            run(j, heads)
            return j + step
        return go

    j = lax.while_loop(both, walk((0, 1)), first)
    lax.while_loop(functools.partial(needed, 0), walk((0,)), j)
    lax.while_loop(functools.partial(needed, 1), walk((1,)), j)


def _logit_bound(qg, kg):
    return (2.0 * 1.03 * DH ** 0.5 * jnp.max(jnp.abs(qg)) * jnp.max(jnp.abs(kg))).reshape(1, 1)


def _skip_tables(f, tq):
    return f[0::tq, :NH].T, f[tq - 1::tq, :NH].T


SMEM_SPEC = pl.BlockSpec(memory_space=pltpu.SMEM)


def _causal_rect(rows, cols, col0):
    return (lax.broadcasted_iota(jnp.int32, (rows, cols), 0)
            >= lax.broadcasted_iota(jnp.int32, (rows, cols), 1) + col0)


SHIFT_MAX = 60.0


def _shift(bd):
    return jnp.where(bd <= SHIFT_MAX, 0.5 * bd, 0.0)


def _attn_fwd(qa, ka, va, fs, fe, bd, tq):
    S = qa.shape[1]
    nq = S // tq

    def body(fs_ref, fe_ref, bd_ref, qa_ref, ka_ref, va_ref, o_ref, lser_ref, m_ref, acc_ref):
        pr, i = pl.program_id(0), pl.program_id(1)
        sel_a = _lane((tq, LANES)) < DH
        acc_ref[...] = jnp.zeros_like(acc_ref)

        def logits(j, hh, masked):
            start = pl.multiple_of(j * tq, tq)
            s = lax.dot_general(qa_ref[hh], ka_ref[hh, pl.ds(start, tq), :], NT, preferred_element_type=F32)
            if masked:
                s = jnp.where(_causal_rect(tq, tq, 0), s, NEG)
            return s, va_ref[hh, pl.ds(start, tq), :]

        def shifted_step(j, heads, masked=False):
            for hh in heads:
                s, vb = logits(j, hh, masked)
                acc_ref[hh] += jnp.dot(jnp.exp(s).astype(BF16), vb, preferred_element_type=F32)

        def online_step(j, heads, masked=False):
            for hh in heads:
                s, vb = logits(j, hh, masked)
                m_prev = m_ref[hh]
                m_new = jnp.maximum(m_prev, jnp.max(s, axis=1, keepdims=True))
                p = jnp.exp(s - jnp.tile(m_new, (1, tq // LANES)))
                m_ref[hh] = m_new
                acc_ref[hh] = jnp.exp(m_prev - m_new) * acc_ref[hh] + jnp.dot(p.astype(BF16), vb,
                                                                              preferred_element_type=F32)

        def needed(slack, hh, j):
            top = fs_ref[2 * pr + hh, i] + slack
            return jnp.logical_and(j >= 0, top - fe_ref[2 * pr + hh, jnp.maximum(j, 0)] >= SKIP)

        def shifted_diagonal():
            h = tq // 2
            base = pl.multiple_of(i * tq, tq)
            for hh in (0, 1):
                s0 = lax.dot_general(qa_ref[hh], ka_ref[hh, pl.ds(base, h), :], NT, preferred_element_type=F32)
                s0 = jnp.where(_causal_rect(tq, h, 0), s0, NEG)
                acc_ref[hh] += jnp.dot(jnp.exp(s0).astype(BF16), va_ref[hh, pl.ds(base, h), :],
                                       preferred_element_type=F32)
                newer = pl.multiple_of(base + h, h)
                s1 = lax.dot_general(qa_ref[hh, h:tq, :], ka_ref[hh, pl.ds(newer, h), :], NT,
                                     preferred_element_type=F32)
                s1 = jnp.where(_causal_rect(h, h, 0), s1, NEG)
                acc_ref[hh, h:tq, :] += jnp.dot(jnp.exp(s1).astype(BF16), va_ref[hh, pl.ds(newer, h), :],
                                                preferred_element_type=F32)

        @pl.when(bd_ref[0, 0] <= SHIFT_MAX)
        def _():
            m_ref[...] = jnp.zeros_like(m_ref)
            shifted_diagonal()
            _block_loops(i - 1, -1, functools.partial(needed, 0.0), shifted_step)

        @pl.when(bd_ref[0, 0] > SHIFT_MAX)
        def _():
            m_ref[...] = jnp.full(m_ref.shape, NEG, F32)
            online_step(i, (0, 1), masked=True)
            _block_loops(i - 1, -1, functools.partial(needed, bd_ref[0, 0]), online_step)

        outs, lses = [], []
        for hh in range(2):
            acc = acc_ref[hh]
            row_sum = jnp.broadcast_to(acc[:, DH:DH + 1], (tq, LANES))
            outs.append(acc / row_sum)
            lses.append(m_ref[hh] + jnp.log(row_sum))
        o_ref[...] = jnp.where(sel_a, outs[0], pltpu.roll(outs[1], 64, 1))
        row = lax.broadcasted_iota(jnp.int32, (8, tq), 0)
        lser_ref[0, 0] = jnp.where(row == 0, lses[0].T[0:8, :], lses[1].T[0:8, :])

    return pl.pallas_call(
        body, name="attn_fwd", grid=(NH // 2, nq),
        out_shape=(_sds((S, AW), F32), _sds((NH // 2, nq, 8, tq), F32)),
        in_specs=[SMEM_SPEC, SMEM_SPEC, SMEM_SPEC,
                  pl.BlockSpec((2, tq, LANES), lambda p, i: (p, i, 0)),
                  pl.BlockSpec((2, S, LANES), lambda p, i: (p, 0, 0)),
                  pl.BlockSpec((2, S, LANES), lambda p, i: (p, 0, 0))],
        out_specs=(pl.BlockSpec((tq, LANES), lambda p, i: (i, p)),
                   pl.BlockSpec((1, 1, 8, tq), lambda p, i: (p, i, 0, 0))),
        scratch_shapes=[pltpu.VMEM((2, tq, LANES), F32), pltpu.VMEM((2, tq, LANES), F32)],
        compiler_params=_cp(("parallel", "parallel")),
    )(fs, fe, bd, qa, ka, va)


HALO = 32
FWD_CHUNK = 64
BWD_CHUNK = 32


def _halo_prev(tm):
    return pl.BlockSpec((HALO, CW), lambda i: (jnp.maximum(i * (tm // HALO) - 1, 0), 0))


SUBLANES = 8
SHIFT_ROWS = 24


def _shifted_copies(ext_ref, sh_ref, tm):
    for k in range(1, SUBLANES):
        sh_ref[k - 1, 0:tm + SHIFT_ROWS, :] = ext_ref[k:k + tm + SHIFT_ROWS, :]


def _tap_windows(ext_ref, sh_ref, r0, offset, rows, shared):
    for k in range(SUBLANES):
        taps = sorted((offset(j), j) for j in range(KC) if offset(j) % SUBLANES == k)
        for group in ([taps] if shared and taps else [[t] for t in taps]):
            lo, hi = r0 + group[0][0] - k, r0 + group[-1][0] - k + rows
            window = ext_ref[lo:hi, :] if k == 0 else sh_ref[k - 1, lo:hi, :]
            for off, j in group:
                at = r0 + off - k - lo
                yield j, window[at:at + rows, :]


def _fwd_conv(u0, w32, cb, lng, lnb, beta_c, tm, ex=None):
    S = u0.shape[0]

    def body(cur_ref, prev_ref, w_ref, cb_ref, lng_ref, lnb_ref, beta_ref, u1_ref, mc_ref, ext_ref, sh_ref):
        i = pl.program_id(0)
        ext_ref[0:HALO, :] = jnp.where(i == 0, 0.0, prev_ref[...])
        ext_ref[HALO:, :] = cur_ref[...]
        _shifted_copies(ext_ref, sh_ref, tm)
        for r0 in range(0, tm, FWD_CHUNK):
            acc = jnp.zeros((FWD_CHUNK, CW), F32) + cb_ref[...]
            for j, rows in _tap_windows(ext_ref, sh_ref, r0, lambda j: 2 + j, FWD_CHUNK, shared=False):
                acc = acc + w_ref[j:j + 1, :] * rows
            u1_ref[r0:r0 + FWD_CHUNK, :] = acc
        u1 = u1_ref[...]
        mu = jnp.mean(u1, axis=-1, keepdims=True)
        d = u1 - mu
        rstd = lax.rsqrt(jnp.mean(d * d, axis=-1, keepdims=True) + EPS)
        u2 = d * rstd * lng_ref[...] + lnb_ref[...]
        u3 = u2 * _sigmoid(u2)
        rc = lax.rsqrt(jnp.mean(u3 * u3, axis=-1, keepdims=True) + EPS)
        mc_ref[...] = (u3 * rc * beta_ref[...]).astype(BF16)

    outs = pl.pallas_call(
        _hosted(body, ex, 7, 2, 2, S // tm), name="fwd_conv", grid=(S // tm,),
        out_shape=(_sds((S, CW), F32), _sds((S, CW), BF16), *(ex.out_shapes if ex else [])),
        in_specs=[_rows(tm, CW), _halo_prev(tm), _const((HALO, CW)), _const((1, CW)), _const((1, CW)), _const((1, CW)),
                  _const((1, CW)), *(ex.in_specs if ex else [])],
        out_specs=(_rows(tm, CW), _rows(tm, CW), *(ex.out_specs if ex else [])),
        scratch_shapes=[pltpu.VMEM((tm + HALO, CW), F32), pltpu.VMEM((SUBLANES - 1, tm + HALO, CW), F32),
                        *(ex.scratch if ex else [])],
        compiler_params=_cp(("arbitrary",)),
    )(u0, u0, w32, cb, lng, lnb, beta_c, *(ex.srcs if ex else []))
    return outs[:2], list(outs[2:])


def _fwd_out(o_attn, mc, x, mod8, n2g, beta_a, w_out, tm):
    S = x.shape[0]

    def body(o_ref, mc_ref, x_ref, mod_ref, n2g_ref, beta_ref, w_ref, mg_ref, ob_ref, x2_ref, h2_ref):
        ov = o_ref[...]
        ra = lax.rsqrt(jnp.mean(ov * ov, axis=-1, keepdims=True) + EPS)
        ma = (ov * ra * beta_ref[...]).astype(BF16)
        mcv = mc_ref[...]
        mg_ref[:, 0:AW] = ma
        mg_ref[:, AW:D] = mcv
        o = (jnp.dot(ma, w_ref[0:AW, :], preferred_element_type=F32)
             + jnp.dot(mcv, w_ref[AW:D, :], preferred_element_type=F32))
        ob_ref[...] = o.astype(BF16)
        x2 = x_ref[...] + mod_ref[2:3, :] * o
        x2_ref[...] = x2
        r2 = lax.rsqrt(jnp.mean(x2 * x2, axis=-1, keepdims=True) + EPS)
        h2_ref[...] = ((x2 * r2) * (n2g_ref[...] * (1.0 + mod_ref[4:5, :])) + mod_ref[3:4, :]).astype(BF16)

    return pl.pallas_call(
        body, name="fwd_out", grid=(S // tm,),
        out_shape=(_sds((S, D), BF16), _sds((S, D), BF16), _sds((S, D), F32), _sds((S, D), BF16)),
        in_specs=[_rows(tm, AW), _rows(tm, CW), _rows(tm, D), _const((8, D)), _const((1, D)), _const((1, AW)),
                  _const((D, D))],
        out_specs=(_rows(tm, D), _rows(tm, D), _rows(tm, D), _rows(tm, D)),
        compiler_params=_cp(("parallel",)),
    )(o_attn, mc, x, mod8, n2g, beta_a, w_out)


def _fwd_ffn(h2, w1, w2, x2, tgt, mod8, tm):
    S = h2.shape[0]
    nk = w1.shape[0]
    bf = w1.shape[2]

    def body(h2_ref, w1_ref, w2_ref, x2_ref, tgt_ref, mod_ref, r_ref, dy_ref, loss_ref, dg2_ref):
        @pl.when(pl.program_id(0) == 0)
        def _():
            loss_ref[...] = jnp.zeros_like(loss_ref)
            dg2_ref[...] = jnp.zeros_like(dg2_ref)

        f2 = None
        for k in range(nk):
            r = jnp.maximum(jnp.dot(h2_ref[...], w1_ref[k], preferred_element_type=F32), 0.0)
            r_ref[:, k * bf:(k + 1) * bf] = r.astype(BF16)
            part = jnp.dot((r * r).astype(BF16), w2_ref[k], preferred_element_type=F32)
            f2 = part if f2 is None else f2 + part
        e = x2_ref[...] + mod_ref[5:6, :] * f2 - tgt_ref[...]
        dy = e * (1.0 / D)
        dy_ref[...] = dy
        loss_ref[...] += 0.5 * jnp.sum(jnp.sum(e * dy, axis=1, keepdims=True), axis=0, keepdims=True)
        dg2_ref[...] += jnp.sum((dy * f2).reshape(tm // 8, 8, D), axis=0)

    once = pl.Buffered(1)
    return pl.pallas_call(
        body, name="fwd_ffn", grid=(S // tm,),
        out_shape=(_sds((S, DFF), BF16), _sds((S, D), F32), _sds((8, LANES), F32), _sds((8, D), F32)),
        in_specs=[_rows(tm, D), pl.BlockSpec((nk, D, bf), lambda i: (0, 0, 0), pipeline_mode=once),
                  pl.BlockSpec((nk, bf, D), lambda i: (0, 0, 0), pipeline_mode=once), _rows(tm, D), _rows(tm, D),
                  _const((8, D))],
        out_specs=(_rows(tm, DFF), _rows(tm, D), _const((8, LANES)), _const((8, D))),
        compiler_params=_cp(("arbitrary",)),
    )(h2, w1, w2, x2, tgt, mod8)


def _bwd_ffn(dy, mod8, r, w1, w2, tm):
    S = dy.shape[0]
    nk = w1.shape[0]
    bf = w1.shape[2]

    def body(dy_ref, mod_ref, r_ref, w1_ref, w2_ref, df2_ref, df1_ref, dh2_ref):
        df2 = (dy_ref[...] * mod_ref[5:6, :]).astype(BF16)
        df2_ref[...] = df2
        dh2 = None
        for k in range(nk):
            da = lax.dot_general(df2, w2_ref[k], NT, preferred_element_type=F32)
            df1 = (da * (2.0 * r_ref[:, k * bf:(k + 1) * bf].astype(F32))).astype(BF16)
            df1_ref[:, k * bf:(k + 1) * bf] = df1
            part = lax.dot_general(df1, w1_ref[k], NT, preferred_element_type=F32)
            dh2 = part if dh2 is None else dh2 + part
        dh2_ref[...] = dh2

    once = pl.Buffered(1)
    return pl.pallas_call(
        body, name="bwd_ffn", grid=(S // tm,),
        out_shape=(_sds((S, D), BF16), _sds((S, DFF), BF16), _sds((S, D), F32)),
        in_specs=[_rows(tm, D), _const((8, D)), _rows(tm, DFF),
                  pl.BlockSpec((nk, D, bf), lambda i: (0, 0, 0), pipeline_mode=once),
                  pl.BlockSpec((nk, bf, D), lambda i: (0, 0, 0), pipeline_mode=once)],
        out_specs=(_rows(tm, D), _rows(tm, DFF), _rows(tm, D)),
        compiler_params=_cp(("parallel",)),
    )(dy, mod8, r, w1, w2)


def _token_tile(S, want):
    while S % want:
        want //= 2
    return want


def _wgrad(a, b, name, square_a=False, col_pieces=False, tk=2048, bm=1024, bn=1024):
    S, M = a.shape
    N = b.shape[1]
    bm, bn, tk = min(bm, M), min(bn, N), _token_tile(S, tk)
    nk = S // tk

    def body(a_ref, b_ref, o_ref, acc_ref):
        av = a_ref[...]
        if square_a:
            af = av.astype(F32)
            av = (af * af).astype(BF16)
        part = lax.dot_general(av, b_ref[...], TN, preferred_element_type=F32)

        @pl.when(pl.program_id(2) == 0)
        def _():
            acc_ref[...] = part

        @pl.when(pl.program_id(2) > 0)
        def _():
            acc_ref[...] += part

        @pl.when(pl.program_id(2) == nk - 1)
        def _():
            if col_pieces:
                o_ref[0] = acc_ref[...].astype(BF16)
            else:
                o_ref[...] = acc_ref[...].astype(BF16)

    if col_pieces:
        out_shape, out_spec = _sds((N // bn, M, bn), BF16), pl.BlockSpec((1, bm, bn), lambda mi, ni, k: (ni, mi, 0))
    else:
        out_shape, out_spec = _sds((M, N), BF16), pl.BlockSpec((bm, bn), lambda mi, ni, k: (mi, ni))
    return pl.pallas_call(
        body, name=name, grid=(M // bm, N // bn, nk), out_shape=out_shape,
        in_specs=[pl.BlockSpec((tk, bm), lambda mi, ni, k: (k, mi)), pl.BlockSpec((tk, bn), lambda mi, ni, k: (k, ni))],
        out_specs=out_spec, scratch_shapes=[pltpu.VMEM((bm, bn), F32)],
        compiler_params=_cp(("parallel", "parallel", "arbitrary")),
    )(a, b)


def _wgrad_in(h1, pieces, tk=1024):
    S = h1.shape[0]
    tk = _token_tile(S, tk)
    widths = [p.shape[1] for p in pieces]
    offs = [sum(widths[:t]) for t in range(len(widths))]

    def body(a_ref, *refs):
        o_ref, acc_ref = refs[-2:]

        @pl.when(pl.program_id(0) == 0)
        def _():
            acc_ref[...] = jnp.zeros_like(acc_ref)

        for b_ref, off, w in zip(refs[:-2], offs, widths):
            acc_ref[:, off:off + w] += lax.dot_general(a_ref[...], b_ref[...], TN, preferred_element_type=F32)

        @pl.when(pl.program_id(0) == S // tk - 1)
        def _():
            o_ref[...] = acc_ref[...].astype(BF16)

    return pl.pallas_call(
        body, name="wgrad_in", grid=(S // tk,), out_shape=_sds((D, NP), BF16),
        in_specs=[_rows(tk, D)] + [_rows(tk, w) for w in widths], out_specs=_const((D, NP)),
        scratch_shapes=[pltpu.VMEM((D, NP), F32)], compiler_params=_cp(("arbitrary",)),
    )(h1, *pieces)


def _colsum8(t):
    return jnp.sum(t.reshape(t.shape[0] // 8, 8, t.shape[1]), axis=0)


def _bwd_mid(dh2, dy, x2, ob, o_attn, u1, mod8, n2g, beta_a, beta_c, lng, lnb, w_out, e512, tm, tq, ex=None):
    S = dy.shape[0]

    def body(dh2_ref, dy_ref, x2_ref, ob_ref, oa_ref, u1_ref, mod_ref, n2g_ref, ba_ref, bc_ref, lng_ref, lnb_ref, w_ref,
             e_ref, dx2_ref, do_ref, doa_ref, du1_ref, acc_d_ref, acc_h_ref, dr_ref):
        @pl.when(pl.program_id(0) == 0)
        def _():
            acc_d_ref[...] = jnp.zeros_like(acc_d_ref)
            acc_h_ref[...] = jnp.zeros_like(acc_h_ref)

        x2 = x2_ref[...]
        dh2 = dh2_ref[...]
        r2 = lax.rsqrt(jnp.mean(x2 * x2, axis=-1, keepdims=True) + EPS)
        xn2 = x2 * r2
        gain = n2g_ref[...] * (1.0 + mod_ref[4:5, :])
        dxn = dh2 * gain
        dx2 = dy_ref[...] + r2 * (dxn - xn2 * jnp.mean(dxn * xn2, axis=-1, keepdims=True))
        dx2_ref[...] = dx2
        t = dh2 * xn2
        acc_d_ref[0] += _colsum8(dh2)
        acc_d_ref[1] += _colsum8(t * n2g_ref[...])
        acc_d_ref[2] += _colsum8(t * (1.0 + mod_ref[4:5, :]))
        acc_d_ref[3] += _colsum8(dx2 * ob_ref[...].astype(F32))
        do = (dx2 * mod_ref[2:3, :]).astype(BF16)
        do_ref[...] = do
        dma = lax.dot_general(do, w_ref[0:AW, :], NT, preferred_element_type=F32)
        dmc = lax.dot_general(do, w_ref[AW:D, :], NT, preferred_element_type=F32)
        ov = oa_ref[...]
        ra = lax.rsqrt(jnp.mean(ov * ov, axis=-1, keepdims=True) + EPS)
        on = ov * ra
        acc_h_ref[0] += _colsum8(dma * on)
        don = dma * ba_ref[...]
        doa = (ra * (don - on * jnp.mean(don * on, axis=-1, keepdims=True))).astype(BF16)
        doa_ref[...] = doa
        delta_t = _dot2(doa.astype(F32) * ov, e_ref[...]).T
        for p in range(NH // 2):
            dr_ref[p, 0] = delta_t[2 * p:2 * p + 8, :]
        u1 = u1_ref[...]
        mu = jnp.mean(u1, axis=-1, keepdims=True)
        d = u1 - mu
        rstd = lax.rsqrt(jnp.mean(d * d, axis=-1, keepdims=True) + EPS)
        uh = d * rstd
        u2 = uh * lng_ref[...] + lnb_ref[...]
        sg = _sigmoid(u2)
        u3 = u2 * sg
        rc = lax.rsqrt(jnp.mean(u3 * u3, axis=-1, keepdims=True) + EPS)
        u3n = u3 * rc
        acc_h_ref[1] += _colsum8(dmc * u3n)
        du3n = dmc * bc_ref[...]
        du3 = rc * (du3n - u3n * jnp.mean(du3n * u3n, axis=-1, keepdims=True))
        du2 = du3 * (sg * (1.0 + u2 * (1.0 - sg)))
        acc_h_ref[2] += _colsum8(du2 * uh)
        acc_h_ref[3] += _colsum8(du2)
        duh = du2 * lng_ref[...]
        du1_ref[...] = rstd * (duh - jnp.mean(duh, axis=-1, keepdims=True)
                               - uh * jnp.mean(duh * uh, axis=-1, keepdims=True))

    per = tq // tm
    outs = pl.pallas_call(
        _hosted(body, ex, 14, 7, 0, S // tm), name="bwd_mid", grid=(S // tm,),
        out_shape=(_sds((S, D), F32), _sds((S, D), BF16), _sds((S, AW), BF16), _sds((S, CW), F32),
                   _sds((4, 8, D), F32), _sds((4, 8, AW), F32), _sds((NH // 2, S // tq, 8, tq), F32),
                   *(ex.out_shapes if ex else [])),
        in_specs=[_rows(tm, D), _rows(tm, D), _rows(tm, D), _rows(tm, D), _rows(tm, AW), _rows(tm, CW), _const((8, D)),
                  _const((1, D)), _const((1, AW)), _const((1, CW)), _const((1, CW)), _const((1, CW)), _const((D, D)),
                  _const((AW, LANES)), *(ex.in_specs if ex else [])],
        out_specs=(_rows(tm, D), _rows(tm, D), _rows(tm, AW), _rows(tm, CW), _const((4, 8, D)), _const((4, 8, AW)),
                   pl.BlockSpec((NH // 2, 1, 8, tm), lambda i: (0, i // per, 0, i % per)),
                   *(ex.out_specs if ex else [])),
        scratch_shapes=ex.scratch if ex else [],
        compiler_params=_cp(("arbitrary",)),
    )(dh2, dy, x2, ob, o_attn, u1, mod8, n2g, beta_a, beta_c, lng, lnb, w_out, e512, *(ex.srcs if ex else []))
    return outs[:7], list(outs[7:])


def _attn_bwd(qa, ka, v, do, lser, dr, fs, fe, bd, tq):
    S = qa.shape[1]
    nq = S // tq

    def body(fs_ref, fe_ref, bd_ref, ka_ref, v_ref, qa_ref, do_ref, lse_ref, dr_ref, dqa_hbm, dka_ref, dv_ref,
             accq_ref, acck_ref, accv_ref, out_sem):
        pr, j = pl.program_id(0), pl.program_id(1)
        sel_a = _lane((tq, LANES)) < DH
        vv = v_ref[...]
        zb = jnp.zeros_like(vv)
        vs = [jnp.where(sel_a, vv, zb), jnp.where(sel_a, zb, vv)]
        acck_ref[...] = jnp.zeros_like(acck_ref)
        accv_ref[...] = jnp.zeros_like(accv_ref)

        @pl.when(j == 0)
        def _():
            accq_ref[...] = jnp.zeros_like(accq_ref)

        def q_step(i, heads, masked=False):
            start = pl.multiple_of(i * tq, tq)
            dob = do_ref[pl.ds(start, tq), :]
            lse8 = lse_ref[0, i]
            dr8 = dr_ref[0, i]
            for hh in heads:
                qb = qa_ref[hh, pl.ds(start, tq), :]
                kb = ka_ref[hh]
                st = lax.dot_general(kb, qb, NT, preferred_element_type=F32)
                pt = jnp.exp(st - lse8[hh:hh + 1, :])
                if masked:
                    keep = (lax.broadcasted_iota(jnp.int32, (tq, tq), 0)
                            <= lax.broadcasted_iota(jnp.int32, (tq, tq), 1))
                    pt = jnp.where(keep, pt, 0.0)
                accv_ref[hh] += jnp.dot(pt.astype(BF16), dob, preferred_element_type=F32)
                dpt = lax.dot_general(vs[hh], dob, NT, preferred_element_type=F32)
                dst = (pt * (dpt - dr8[hh:hh + 1, :])).astype(BF16)
                acck_ref[hh] += jnp.dot(dst, qb, preferred_element_type=F32)
                accq_ref[hh, pl.ds(start, tq), :] += lax.dot_general(dst, kb, TN, preferred_element_type=F32)

        def needed(hh, i):
            top = fs_ref[2 * pr + hh, jnp.minimum(i, nq - 1)] + bd_ref[0, 0]
            return jnp.logical_and(i < nq, top - fe_ref[2 * pr + hh, j] >= SKIP_P)

        q_step(j, (0, 1), masked=True)
        _block_loops(j + 1, 1, needed, q_step)
        dka_ref[...] = acck_ref[...]
        dv_ref[...] = jnp.where(sel_a, accv_ref[0], accv_ref[1]).astype(BF16)

        @pl.when(j == nq - 1)
        def _():
            out = pltpu.make_async_copy(accq_ref, dqa_hbm.at[pl.ds(2 * pr, 2)], out_sem)
            out.start()
            out.wait()

    once = pl.Buffered(1)
    return pl.pallas_call(
        body, name="attn_bwd", grid=(NH // 2, nq),
        out_shape=(_sds((NH, S, LANES), F32), _sds((NH, S, LANES), F32), _sds((S, AW), BF16)),
        in_specs=[SMEM_SPEC, SMEM_SPEC, SMEM_SPEC,
                  pl.BlockSpec((2, tq, LANES), lambda p, j: (p, j, 0)),
                  pl.BlockSpec((tq, LANES), lambda p, j: (j, p)),
                  pl.BlockSpec((2, S, LANES), lambda p, j: (p, 0, 0), pipeline_mode=once),
                  pl.BlockSpec((S, LANES), lambda p, j: (0, p), pipeline_mode=once),
                  pl.BlockSpec((1, nq, 8, tq), lambda p, j: (p, 0, 0, 0)),
                  pl.BlockSpec((1, nq, 8, tq), lambda p, j: (p, 0, 0, 0))],
        out_specs=(pl.BlockSpec(memory_space=pl.ANY),
                   pl.BlockSpec((2, tq, LANES), lambda p, j: (p, j, 0)),
                   pl.BlockSpec((tq, LANES), lambda p, j: (j, p))),
        scratch_shapes=[pltpu.VMEM((2, S, LANES), F32), pltpu.VMEM((2, tq, LANES), F32),
                        pltpu.VMEM((2, tq, LANES), F32), pltpu.SemaphoreType.DMA],
        compiler_params=_cp(("arbitrary", "arbitrary")),
    )(fs, fe, bd, ka, v, qa, do, lser, dr)


def _bwd_conv(du1, u0, alin, agate, w32, tm, ex=None):
    S = du1.shape[0]
    nt = S // tm

    def body(du_ref, dun_ref, u0_ref, u0p_ref, alin_ref, agate_ref, w_ref,
             dalin_ref, dagate_ref, dw_ref, db_ref, extd_ref, extu_ref, du0_ref, shd_ref, shu_ref):
        i = pl.program_id(0)

        @pl.when(i == 0)
        def _():
            dw_ref[...] = jnp.zeros_like(dw_ref)
            db_ref[...] = jnp.zeros_like(db_ref)

        extd_ref[0:tm, :] = du_ref[...]
        extd_ref[tm:, :] = jnp.where(i == nt - 1, 0.0, dun_ref[...])
        extu_ref[0:HALO, :] = jnp.where(i == 0, 0.0, u0p_ref[...])
        extu_ref[HALO:, :] = u0_ref[...]
        _shifted_copies(extd_ref, shd_ref, tm)
        _shifted_copies(extu_ref, shu_ref, tm)
        db_ref[...] += _colsum8(du_ref[...])
        for r0 in range(0, tm, BWD_CHUNK):
            acc = jnp.zeros((BWD_CHUNK, CW), F32)
            for j, rows in _tap_windows(extd_ref, shd_ref, r0, lambda j: 30 - j, BWD_CHUNK, shared=True):
                acc = acc + w_ref[j:j + 1, :] * rows
            du0_ref[r0:r0 + BWD_CHUNK, :] = acc
            duc = du_ref[r0:r0 + BWD_CHUNK, :]
            for j, rows in _tap_windows(extu_ref, shu_ref, r0, lambda j: 2 + j, BWD_CHUNK, shared=True):
                dw_ref[j] += _colsum8(duc * rows)
        du0 = du0_ref[...]
        al = alin_ref[...].astype(F32)
        sg = _sigmoid(agate_ref[...].astype(F32))
        dalin_ref[...] = (du0 * sg).astype(BF16)
        dagate_ref[...] = (du0 * al * sg * (1.0 - sg)).astype(BF16)

    nxt = pl.BlockSpec((HALO, CW), lambda i: (jnp.minimum((i + 1) * (tm // HALO), S // HALO - 1), 0))
    outs = pl.pallas_call(
        _hosted(body, ex, 7, 4, 5, nt), name="bwd_conv", grid=(nt,),
        out_shape=(_sds((S, CW), BF16), _sds((S, CW), BF16), _sds((HALO, 8, CW), F32), _sds((8, CW), F32),
                   *(ex.out_shapes if ex else [])),
        in_specs=[_rows(tm, CW), nxt, _rows(tm, CW), _halo_prev(tm), _rows(tm, CW), _rows(tm, CW), _const((HALO, CW)),
                  *(ex.in_specs if ex else [])],
        out_specs=(_rows(tm, CW), _rows(tm, CW), _const((HALO, 8, CW)), _const((8, CW)), *(ex.out_specs if ex else [])),
        scratch_shapes=[pltpu.VMEM((tm + HALO, CW), F32), pltpu.VMEM((tm + HALO, CW), F32), pltpu.VMEM((tm, CW), F32),
                        pltpu.VMEM((SUBLANES - 1, tm + HALO, CW), F32), pltpu.VMEM((SUBLANES - 1, tm + HALO, CW), F32),
                        *(ex.scratch if ex else [])],
        compiler_params=_cp(("arbitrary",)),
    )(du1, du1, u0, u0, alin, agate, w32, *(ex.srcs if ex else []))
    return outs[:4], list(outs[4:])


def _bwd_qk(dqa, dka, qn, kn, rq, rk, fgb, qg512, kg512, e512, et512, tm):
    S = qn.shape[0]
    nt = S // tm

    def body(dqa_ref, dka_ref, qn_ref, kn_ref, rq_ref, rk_ref, fgb_ref, qg_ref, kg_ref, e_ref, et_ref,
             dq_ref, dk_ref, dfg_ref, accg_ref, accb_ref, carry_ref):
        @pl.when(pl.program_id(0) == 0)
        def _():
            carry_ref[...] = jnp.zeros_like(carry_ref)
            accg_ref[...] = jnp.zeros_like(accg_ref)
            accb_ref[...] = jnp.zeros_like(accb_ref)

        lane = _lane((tm, LANES))
        sel_a = lane < DH
        df = jnp.zeros((tm, LANES), F32)
        for h in range(NH):
            col = dqa_ref[h][:, 64:65] - dka_ref[h][:, 67:68]
            df = jnp.where(lane == h, col, df)
        tri = (lax.broadcasted_iota(jnp.int32, (tm, tm), 0) <= lax.broadcasted_iota(jnp.int32, (tm, tm), 1)
               ).astype(F32).astype(BF16)
        dlf = _dot3(tri, df) + carry_ref[0:1, :]
        carry_ref[...] = jnp.broadcast_to(dlf[0:1, :], carry_ref.shape)
        dfg = jnp.where(lane < NH, dlf * _sigmoid(-fgb_ref[...]), 0.0)
        dfg_ref[...] = dfg.astype(BF16)
        accb_ref[...] += _colsum8(dfg)

        def norm_bwd(src_ref, n_ref, r_ref, g_ref, scale, slot):
            pairs = []
            for p in range(NH // 2):
                b = pltpu.roll(src_ref[2 * p + 1], 64, 1)
                pairs.append(jnp.where(sel_a, src_ref[2 * p], b))
            dh = jnp.concatenate(pairs, axis=1) * scale
            tn = n_ref[...].astype(F32)
            accg_ref[slot] += _colsum8(dh * tn)
            dn = dh * g_ref[...]
            mean = _dot2(dn * tn, e_ref[...]) * (1.0 / DH)
            corr = _dot2(mean, et_ref[...])
            rf = _dot2(r_ref[...], et_ref[...])
            return (rf * (dn - tn * corr)).astype(BF16)

        dq_ref[...] = norm_bwd(dqa_ref, qn_ref, rq_ref, qg_ref, DH ** -0.5, 0)
        dk_ref[...] = norm_bwd(dka_ref, kn_ref, rk_ref, kg_ref, 1.0, 1)

    rev = lambda n: pl.BlockSpec((tm, n), lambda i: (nt - 1 - i, 0))
    hm = pl.BlockSpec((NH, tm, LANES), lambda i: (0, nt - 1 - i, 0))
    dq, dk, dfg, accg, accb = pl.pallas_call(
        body, name="bwd_qk", grid=(nt,),
        out_shape=(_sds((S, AW), BF16), _sds((S, AW), BF16), _sds((S, LANES), BF16), _sds((2, 8, AW), F32),
                   _sds((8, LANES), F32)),
        in_specs=[hm, hm, rev(AW), rev(AW), rev(LANES), rev(LANES), rev(LANES), _const((1, AW)), _const((1, AW)),
                  _const((AW, LANES)), _const((LANES, AW))],
        out_specs=(rev(AW), rev(AW), rev(LANES), _const((2, 8, AW)), _const((8, LANES))),
        scratch_shapes=[pltpu.VMEM((8, LANES), F32)], compiler_params=_cp(("arbitrary",)),
    )(dqa, dka, qn, kn, rq, rk, fgb, qg512, kg512, e512, et512)
    return dq, dk, dfg, accg, accb


def _bwd_in(dq, dk, dv, dalin, dagate, dfg, w_in_p, x, dx2, mod8, n1g, tm, ex=None):
    S = x.shape[0]

    def body(dq_ref, dk_ref, dv_ref, dal_ref, dag_ref, dfg_ref, w_ref, x_ref, dx2_ref, mod_ref, n1g_ref,
             dx_ref, acc_ref):
        @pl.when(pl.program_id(0) == 0)
        def _():
            acc_ref[...] = jnp.zeros_like(acc_ref)

        def part(ref, a, b):
            return lax.dot_general(ref[...], w_ref[:, a:b], NT, preferred_element_type=F32)

        dh = (part(dq_ref, 0, 512) + part(dk_ref, 512, 1024) + part(dv_ref, 1024, 1536) + part(dal_ref, 1536, 2048)
              + part(dag_ref, 2048, 2560) + part(dfg_ref, 2560, NP))
        xv = x_ref[...]
        r1 = lax.rsqrt(jnp.mean(xv * xv, axis=-1, keepdims=True) + EPS)
        xn = xv * r1
        gain = n1g_ref[...] * (1.0 + mod_ref[1:2, :])
        t = dh * xn
        acc_ref[0] += _colsum8(dh)
        acc_ref[1] += _colsum8(t * n1g_ref[...])
        acc_ref[2] += _colsum8(t * (1.0 + mod_ref[1:2, :]))
        dxn = dh * gain
        dx_ref[...] = dx2_ref[...] + r1 * (dxn - xn * jnp.mean(dxn * xn, axis=-1, keepdims=True))

    outs = pl.pallas_call(
        _hosted(body, ex, 11, 2, 0, S // tm), name="bwd_in", grid=(S // tm,),
        out_shape=(_sds((S, D), F32), _sds((3, 8, D), F32), *(ex.out_shapes if ex else [])),
        in_specs=[_rows(tm, AW), _rows(tm, AW), _rows(tm, AW), _rows(tm, CW), _rows(tm, CW), _rows(tm, LANES),
                  _const((D, NP)), _rows(tm, D), _rows(tm, D), _const((8, D)), _const((1, D)),
                  *(ex.in_specs if ex else [])],
        out_specs=(_rows(tm, D), _const((3, 8, D)), *(ex.out_specs if ex else [])),
        scratch_shapes=ex.scratch if ex else [],
        compiler_params=_cp(("arbitrary",)),
    )(dq, dk, dv, dalin, dagate, dfg, w_in_p, x, dx2, mod8, n1g, *(ex.srcs if ex else []))
    return outs[:2], list(outs[2:])


def _adam(w, g, m, v):
    m_new = B1 * m + (1.0 - B1) * g
    v_new = B2 * v + (1.0 - B2) * (g * g)
    m_hat = m_new / (1.0 - B1 ** STEP)
    v_hat = v_new / (1.0 - B2 ** STEP)
    delta = -LR * (m_hat / (jnp.sqrt(v_hat) + AEPS) + WD * w)
    return delta, m_new, v_new


def _pair_adamw(slots, w, m, v, name, tr=256):
    ns, R, C = slots.shape
    tr = tr if R % tr == 0 else R
    nt = R // tr

    def body(s_ref, w_ref, m_ref, v_ref, g_ref, d_ref, mo_ref, vo_ref, mine_ref, theirs_ref, send_sems, recv_sems):
        i = pl.program_id(0)
        part = s_ref[0].astype(F32)
        for k in range(1, ns):
            part = part + s_ref[k].astype(F32)
        mine_ref[i] = part
        swap = pltpu.make_async_remote_copy(
            src_ref=mine_ref.at[i], dst_ref=theirs_ref.at[i], send_sem=send_sems.at[i], recv_sem=recv_sems.at[i],
            device_id=(lax.axis_index("x"), lax.axis_index("y"), 1 - lax.axis_index("c")),
            device_id_type=pl.DeviceIdType.MESH)
        swap.start()
        swap.wait()
        g = part + theirs_ref[i]
        g_ref[...] = g
        d_ref[...], mo_ref[...], vo_ref[...] = _adam(w_ref[...], g, m_ref[...], v_ref[...])

    blk = pl.BlockSpec((tr, C), lambda i: (i, 0))
    return pl.pallas_call(
        body, name=name, grid=(nt,), out_shape=tuple(_sds((R, C), F32) for _ in range(4)),
        in_specs=[pl.BlockSpec((ns, tr, C), lambda i: (0, i, 0)), blk, blk, blk], out_specs=(blk, blk, blk, blk),
        scratch_shapes=[pltpu.VMEM((nt, tr, C), F32), pltpu.VMEM((nt, tr, C), F32),
                        pltpu.SemaphoreType.DMA((nt,)), pltpu.SemaphoreType.DMA((nt,))],
        compiler_params=_cp(("arbitrary",)),
    )(slots, w, m, v)


def _ada_adamw(sct, dmod, w, m, v):
    R, C = w.shape
    tr, bc = 256, 512

    def body(sct_ref, dm_ref, w_ref, m_ref, v_ref, g_ref, d_ref, mo_ref, vo_ref):
        g = sct_ref[:, 0:1] * dm_ref[0:1, :]
        for b in range(1, N_DEV):
            g = g + sct_ref[:, b:b + 1] * dm_ref[b:b + 1, :]
        g_ref[...] = g
        d_ref[...], mo_ref[...], vo_ref[...] = _adam(w_ref[...], g, m_ref[...], v_ref[...])

    blk = pl.BlockSpec((tr, bc), lambda i, j: (i, j))
    return pl.pallas_call(
        body, name="ada_adamw", grid=(R // tr, C // bc), out_shape=tuple(_sds((R, C), F32) for _ in range(4)),
        in_specs=[pl.BlockSpec((tr, N_DEV), lambda i, j: (i, 0)), pl.BlockSpec((N_DEV, bc), lambda i, j: (0, j)),
                  blk, blk, blk],
        out_specs=(blk, blk, blk, blk), compiler_params=_cp(("parallel", "parallel")),
    )(sct, dmod, w, m, v)


PACK = {"dmod": 0, "norm1_g": 6144, "norm2_g": 7168, "q_norm_g": 8192, "k_norm_g": 8704, "b_f": 9216, "conv_b": 9344,
        "conv_ln_g": 9856, "conv_ln_b": 10368, "beta_attn": 10880, "beta_conv": 11392, "loss": 11904}
SMALL_NAMES = ["b_ada", "norm1_g", "q_norm_g", "k_norm_g", "b_f", "conv_b", "conv_ln_g", "conv_ln_b", "beta_attn",
               "beta_conv", "norm2_g"]


def _pack_small(acc1, acc_d, acc_h, dg2, accg, accb, dcb, loss8):
    def body(a1_ref, ad_ref, ah_ref, dg2_ref, ag_ref, ab_ref, cb_ref, loss_ref, o_ref):
        def put(off, rows):
            o_ref[:, off:off + rows.shape[1]] = jnp.sum(rows, axis=0, keepdims=True)

        for t, rows in enumerate((a1_ref[0], a1_ref[1], ad_ref[3], ad_ref[0], ad_ref[1], dg2_ref[...])):
            put(PACK["dmod"] + t * D, rows)
        put(PACK["norm1_g"], a1_ref[2])
        put(PACK["norm2_g"], ad_ref[2])
        put(PACK["q_norm_g"], ag_ref[0])
        put(PACK["k_norm_g"], ag_ref[1])
        put(PACK["b_f"], ab_ref[...])
        put(PACK["conv_b"], cb_ref[...])
        put(PACK["conv_ln_g"], ah_ref[2])
        put(PACK["conv_ln_b"], ah_ref[3])
        put(PACK["beta_attn"], ah_ref[0])
        put(PACK["beta_conv"], ah_ref[1])
        o_ref[:, PACK["loss"]:PACK["loss"] + LANES] = loss_ref[0:1, :]

    vm = pl.BlockSpec(memory_space=pltpu.VMEM)
    return pl.pallas_call(body, name="pack_small", out_shape=_sds((1, SMALL_IN), F32), in_specs=[vm] * 8, out_specs=vm,
                          )(acc1, acc_d, acc_h, dg2, accg, accb, dcb, loss8)


def _small_adamw(slots, fold, ws, ms, vs):
    n = len(ws)
    widths = [w.shape[1] for w in ws]

    def body(s_ref, f_ref, *refs):
        w_refs, m_refs, v_refs = refs[:n], refs[n:2 * n], refs[2 * n:3 * n]
        outs = refs[3 * n:]
        tot = s_ref[0:1, :]
        for k in range(1, N_DEV):
            tot = tot + s_ref[k:k + 1, :]

        def grad(name, width):
            if name == "b_ada":
                return tot[:, 0:6 * D]
            seg = tot[:, PACK[name]:PACK[name] + max(width, LANES)]
            if name in ("q_norm_g", "k_norm_g"):
                seg = jnp.dot(jnp.broadcast_to(tot[:, PACK[name]:PACK[name] + AW], (8, AW)), f_ref[...], precision=HI,
                              preferred_element_type=F32)[0:1, :]
            return seg[:, 0:width]

        for t, (name, width) in enumerate(zip(SMALL_NAMES, widths)):
            g = grad(name, width)
            d, m_new, v_new = _adam(w_refs[t][...], g, m_refs[t][...], v_refs[t][...])
            outs[t][...] = g
            outs[n + t][...] = d
            outs[2 * n + t][...] = m_new
            outs[3 * n + t][...] = v_new
        outs[4 * n][...] = tot[:, PACK["loss"]:PACK["loss"] + LANES]

    vm = pl.BlockSpec(memory_space=pltpu.VMEM)
    outs = pl.pallas_call(
        body, name="adamw_small",
        out_shape=(*(_sds((1, w), F32) for _ in range(4) for w in widths), _sds((1, LANES), F32)),
        in_specs=[vm] * (2 + 3 * n), out_specs=tuple(vm for _ in range(4 * n + 1)),
    )(slots, fold, *ws, *ms, *vs)
    return [list(outs[k * n:(k + 1) * n]) for k in range(4)], outs[4 * n][0, 0]


def _perm_in(w):
    pad = jnp.zeros((w.shape[0], NP - 2568), w.dtype)
    return jnp.concatenate([w[:, :1536], w[:, 1544:2568], w[:, 1536:1544], pad], axis=1)


def _pad_lanes(vec, n=LANES):
    return jnp.pad(vec, ((0, 0), (0, n - vec.shape[1])))


def kernel(x, c, w_ada, b_ada, norm1_g, w_in, q_norm_g, k_norm_g, b_f, conv_w, conv_b, conv_ln_g, conv_ln_b, beta_attn, beta_conv, w_out, norm2_g, w_ff1, w_ff2, loss_target, m_w_ada, m_b_ada, m_norm1_g, m_w_in, m_q_norm_g, m_k_norm_g, m_b_f, m_conv_w, m_conv_b, m_conv_ln_g, m_conv_ln_b, m_beta_attn, m_beta_conv, m_w_out, m_norm2_g, m_w_ff1, m_w_ff2, v_w_ada, v_b_ada, v_norm1_g, v_w_in, v_q_norm_g, v_k_norm_g, v_b_f, v_conv_w, v_conv_b, v_conv_ln_g, v_conv_ln_b, v_beta_attn, v_beta_conv, v_w_out, v_norm2_g, v_w_ff1, v_w_ff2):
    S = x.shape[1]
    tm = min(256, S)
    tw = min(512, S)
    tq = min(512, S // 2)
    xs, tgt = x[0], loss_target[0]
    chip = 2 * lax.axis_index("x") + lax.axis_index("y")
    e512, et512 = _head_sum_mats()

    conv_w32 = jnp.pad(conv_w[0], ((0, 1), (0, 0)))
    c_all, g_in = _exchange([(c, "bcast8"), (w_in[0].astype(BF16), "chip4")], "gather_in")
    later_weights = _Exchange([(w_out[0].astype(BF16), "chip4"), (conv_w32, "chip4")])
    c_all = c_all.reshape(N_DEV, D)
    w_in_p = _perm_in(jnp.transpose(g_in, (1, 0, 2)).reshape(D, 2568))

    b_shard = lax.dynamic_slice(b_ada, (0, chip * 1536), (1, 1536))
    mod_rows, sc_all = _mod_shard(c_all, w_ada[0], b_shard)
    (mod_slots,) = _exchange([(mod_rows.reshape(N_DEV, 1, 1536), "all8")], "scatter_mod")
    mod = mod_slots.reshape(4, 2, 1536)[:, 0, :].reshape(6, D)
    mod8 = jnp.pad(mod, ((0, 2), (0, 0)))

    qg512 = jnp.tile(q_norm_g, (1, NH))
    kg512 = jnp.tile(k_norm_g, (1, NH))
    bf128 = _pad_lanes(b_f)

    (h1, qh, kh, vb, qn, kn, rq, rk, fgb, alin, agate, u0), (g_out, g_cw) = _fwd_in(
        xs, mod8, norm1_g, w_in_p, e512, et512, qg512, kg512, bf128, tw, ex=later_weights)
    w_out_f = g_out.reshape(D, D)
    cw32 = jnp.transpose(g_cw, (1, 0, 2)).reshape(HALO, CW)
    bd = _logit_bound(q_norm_g, k_norm_g)
    (qa, ka, va, fcum), (w1,) = _fwd_decay(fgb, qh, kh, vb, _shift(bd), tw,
                                           ex=_Exchange([(w_ff1[0].astype(BF16), "chip4")]))
    fs, fe = _skip_tables(fcum, tq)
    o_attn, lser = _attn_fwd(qa, ka, va, fs, fe, bd, tq)
    (u1, mc), (w2,) = _fwd_conv(u0, cw32, conv_b, conv_ln_g, conv_ln_b, beta_conv, tm,
                                ex=_Exchange([(w_ff2[0].astype(BF16), "chip4")]))
    merged, ob, x2, h2 = _fwd_out(o_attn, mc, xs, mod8, norm2_g, beta_attn, w_out_f, tw)
    r, dy, loss8, dg2 = _fwd_ffn(h2, w1, w2, x2, tgt, mod8, tw)

    df2, df1, dh2 = _bwd_ffn(dy, mod8, r, w1, w2, tw)
    gw_ff2 = _wgrad(r, df2, "wgrad_ff2", square_a=True)
    gw_ff1 = _wgrad(h2, df1, "wgrad_ff1", col_pieces=True)
    (dx2, do, doa, du1, acc_d, acc_h, dr), (p_ff1,) = _bwd_mid(
        dh2, dy, x2, ob, o_attn, u1, mod8, norm2_g, beta_attn, beta_conv, conv_ln_g, conv_ln_b, w_out_f, e512, tm, tq,
        ex=_Exchange([(gw_ff1, "chip4p")]))
    gw_out = _wgrad(merged, do, "wgrad_out")
    dqa, dka, dv = _attn_bwd(qa, ka, vb, doa, lser, dr, fs, fe, bd, tq)
    (dalin, dagate, dcw, dcb), (p_out, p_ff2) = _bwd_conv(
        du1, u0, alin, agate, cw32, tm,
        ex=_Exchange([(gw_out.reshape(4, 256, D), "chip4p"), (gw_ff2.reshape(4, D, D), "chip4p")]))
    dq, dk, dfg, accg, accb = _bwd_qk(dqa, dka, qn, kn, rq, rk, fgb, qg512, kg512, e512, et512, tw)
    gw_in_p = _wgrad_in(h1, [dq, dk, dv, dalin, dagate, dfg])
    gw_in = jnp.concatenate([gw_in_p[:, :1536], gw_in_p[:, 2560:2568], gw_in_p[:, 1536:2560]], axis=1)
    s8 = lambda a: jnp.sum(a, axis=-2)
    gcw = s8(dcw)
    in_grads = _Exchange([(jnp.transpose(gw_in.reshape(D, 4, 642), (1, 0, 2)), "chip4p"),
                          (jnp.transpose(gcw.reshape(HALO, 4, LANES), (1, 0, 2)), "chip4p")])
    (grad_x, acc1), (p_in, p_cw) = _bwd_in(dq, dk, dv, dalin, dagate, dfg, w_in_p, xs, dx2, mod8, norm1_g, tw,
                                           ex=in_grads)

    small = _pack_small(acc1, acc_d, acc_h, dg2, accg, accb, dcb, loss8)
    (small_s,) = _exchange([(small, "bcast8")], "gather_small")
    small_s = small_s.reshape(N_DEV, SMALL_IN)

    g_in_, d_in, nm_in, nv_in = _pair_adamw(p_in, w_in[0], m_w_in[0], v_w_in[0], "adamw_in")
    g_out_, d_out, nm_out, nv_out = _pair_adamw(p_out, w_out[0], m_w_out[0], v_w_out[0], "adamw_out")
    g_f1, d_f1, nm_f1, nv_f1 = _pair_adamw(p_ff1, w_ff1[0], m_w_ff1[0], v_w_ff1[0], "adamw_ff1")
    g_f2, d_f2, nm_f2, nv_f2 = _pair_adamw(p_ff2, w_ff2[0], m_w_ff2[0], v_w_ff2[0], "adamw_ff2")
    pad_row = lambda a, fill: jnp.pad(a[0], ((0, 1), (0, 0)), constant_values=fill)
    g_cw_, d_cw, nm_cw, nv_cw = (a[:KC] for a in _pair_adamw(
        p_cw, pad_row(conv_w, 0.0), pad_row(m_conv_w, 0.0), pad_row(v_conv_w, 1.0), "adamw_conv_w"))
    dmod_shard = lax.dynamic_slice(small_s[:, :6 * D], (0, chip * 1536), (N_DEV, 1536))
    g_ada, d_ada, nm_ada, nv_ada = _ada_adamw(sc_all.T, dmod_shard, w_ada[0], m_w_ada[0], v_w_ada[0])

    fold = np.zeros((AW, LANES), np.float32)
    fold[np.arange(AW), np.arange(AW) % DH] = 1.0
    smalls = [b_ada, norm1_g, q_norm_g, k_norm_g, b_f, conv_b, conv_ln_g, conv_ln_b, beta_attn, beta_conv, norm2_g]
    m_smalls = [m_b_ada, m_norm1_g, m_q_norm_g, m_k_norm_g, m_b_f, m_conv_b, m_conv_ln_g, m_conv_ln_b, m_beta_attn,
                m_beta_conv, m_norm2_g]
    v_smalls = [v_b_ada, v_norm1_g, v_q_norm_g, v_k_norm_g, v_b_f, v_conv_b, v_conv_ln_g, v_conv_ln_b, v_beta_attn,
                v_beta_conv, v_norm2_g]
    (gs, ds, ms, vs), loss = _small_adamw(small_s, jnp.asarray(fold), smalls, m_smalls, v_smalls)

    big ={"w_ada": (g_ada, d_ada, nm_ada, nv_ada), "w_in": (g_in_, d_in, nm_in, nv_in),
           "conv_w": (g_cw_, d_cw, nm_cw, nv_cw), "w_out": (g_out_, d_out, nm_out, nv_out),
           "w_ff1": (g_f1, d_f1, nm_f1, nv_f1), "w_ff2": (g_f2, d_f2, nm_f2, nv_f2)}
    order =["w_ada", "b_ada", "norm1_g", "w_in", "q_norm_g", "k_norm_g", "b_f", "conv_w", "conv_b", "conv_ln_g",
             "conv_ln_b", "beta_attn", "beta_conv", "w_out", "norm2_g", "w_ff1", "w_ff2"]

    def leaf(name, which):
        if name in big:
            return big[name][which][None]
        return (gs, ds, ms, vs)[which][SMALL_NAMES.index(name)]

    return (loss, grad_x[None], *[leaf(n, 0) for n in order], *[leaf(n, 1) for n in order],
            *[leaf(n, 2) for n in order], *[leaf(n, 3) for n in order])
```

```python
import functools

import numpy as np
import jax
import jax.numpy as jnp
from jax import lax
from jax.experimental import pallas as pl
from jax.experimental.pallas import tpu as pltpu

F32, BF16 = jnp.float32, jnp.bfloat16
HI = lax.Precision.HIGHEST
D = 1024
AW = 512
CW = 512
NH = 8
DH = 64
KC = 31
DFF = 4096
NP = 2688
EPS = 1e-6
NEG = -1e30
LANES = 128
VMEM_LIMIT = 56 * 2**20
NT = (((1,), (1,)), ((), ()))
TN = (((0,), (0,)), ((), ()))
LR, B1, B2, AEPS, WD, STEP = 0.001, 0.9, 0.999, 1e-08, 0.01, 10
N_DEV = 8
SMALL_IN = 12032


def _cp(sem=None, vmem=VMEM_LIMIT):
    kw = dict(vmem_limit_bytes=vmem)
    if sem is not None:
        kw["dimension_semantics"] = sem
    return pltpu.CompilerParams(**kw)


def _rows(tm, n):
    return pl.BlockSpec((tm, n), lambda i: (i, 0))


def _const(shape):
    nd = len(shape)
    return pl.BlockSpec(shape, lambda *_: (0,) * nd)


def _sds(shape, dt):
    return jax.ShapeDtypeStruct(shape, dt)


def _lane(shape):
    return lax.broadcasted_iota(jnp.int32, shape, len(shape) - 1)


def _sigmoid(x):
    return 1.0 / (1.0 + jnp.exp(-x))


class _Exchange:
    MASKS = {"chip4": (2, 4, 6), "chip4p": (2, 4, 6), "all8": (1, 2, 3, 4, 5, 6, 7), "bcast8": (1, 2, 3, 4, 5, 6, 7)}

    def __init__(self, items):
        self.srcs = [s for s, _ in items]
        self.kinds = [k for _, k in items]
        self.n = len(items)
        self.out_shapes = []
        for s, k in items:
            shape = {"all8": (N_DEV,) + s.shape[1:], "bcast8": (N_DEV,) + s.shape, "chip4": (4,) + s.shape,
                     "chip4p": (4,) + s.shape[1:]}[k]
            self.out_shapes.append(_sds(shape, s.dtype))
        self.sem_index = {}
        for t, k in enumerate(self.kinds):
            for m in self.MASKS[k]:
                self.sem_index[(t, m)] = len(self.sem_index)
        n_sem = len(self.sem_index)
        self.scratch = [pltpu.SemaphoreType.DMA((n_sem,)), pltpu.SemaphoreType.DMA((n_sem,)),
                        pltpu.SemaphoreType.DMA((self.n,))]
        self.in_specs = [pl.BlockSpec(memory_space=pl.ANY)] * self.n
        self.out_specs = [pl.BlockSpec(memory_space=pl.ANY)] * self.n

    def copies(self, src_refs, dst_refs, send_sems, recv_sems, local_sems):
        x, y, c = lax.axis_index("x"), lax.axis_index("y"), lax.axis_index("c")
        my_id = 4 * x + 2 * y + c
        my_chip = 2 * x + y

        def piece(t, dev_id, chip):
            k = self.kinds[t]
            return src_refs[t].at[dev_id] if k == "all8" else src_refs[t].at[chip] if k == "chip4p" else src_refs[t]

        out = []
        for t in range(self.n):
            slot = dst_refs[t].at[my_chip if self.kinds[t] in ("chip4", "chip4p") else my_id]
            out.append(pltpu.make_async_copy(piece(t, my_id, my_chip), slot, local_sems.at[t]))
            for m in self.MASKS[self.kinds[t]]:
                px = 1 - x if m & 4 else x
                py = 1 - y if m & 2 else y
                pc = 1 - c if m & 1 else c
                s = self.sem_index[(t, m)]
                out.append(pltpu.make_async_remote_copy(
                    src_ref=piece(t, 4 * px + 2 * py + pc, 2 * px + py), dst_ref=slot,
                    send_sem=send_sems.at[s], recv_sem=recv_sems.at[s],
                    device_id=(px, py, pc), device_id_type=pl.DeviceIdType.MESH))
        return out


def _hosted(body, ex, n_in, n_out, n_scr, n_steps):
    if ex is None:
        return body

    def wrapped(*refs):
        ins, xin = refs[:n_in], refs[n_in:n_in + ex.n]
        o0 = n_in + ex.n
        outs, xout = refs[o0:o0 + n_out], refs[o0 + n_out:o0 + n_out + ex.n]
        s0 = o0 + n_out + ex.n
        scr, sems = refs[s0:s0 + n_scr], refs[s0 + n_scr:]

        @pl.when(pl.program_id(0) == 0)
        def _():
            for cp in ex.copies(xin, xout, *sems):
                cp.start()

        body(*ins, *outs, *scr)

        @pl.when(pl.program_id(0) == n_steps - 1)
        def _():
            for cp in ex.copies(xin, xout, *sems):
                cp.wait()

    return wrapped


def _exchange(items, name):
    ex = _Exchange(items)
    n = ex.n

    def body(*refs):
        copies = ex.copies(refs[:n], refs[n:2 * n], *refs[2 * n:])
        for cp in copies:
            cp.start()
        for cp in copies:
            cp.wait()

    outs = pl.pallas_call(
        body, name=name, out_shape=tuple(ex.out_shapes), in_specs=ex.in_specs, out_specs=tuple(ex.out_specs),
        scratch_shapes=ex.scratch,
    )(*ex.srcs)
    return list(outs)


def _mod_shard(c_all, w_ada, b_shard):
    n = w_ada.shape[1]

    def body(c_ref, w_ref, b_ref, o_ref, sc_ref):
        cv = c_ref[...]
        sc = cv * _sigmoid(cv)
        sc_ref[...] = sc
        o_ref[...] = jnp.dot(sc, w_ref[...], precision=HI, preferred_element_type=F32) + b_ref[...]

    bn = 512
    return pl.pallas_call(
        body, name="mod_shard", out_shape=(_sds((N_DEV, n), F32), _sds((N_DEV, D), F32)), grid=(n // bn,),
        in_specs=[_const((N_DEV, D)), pl.BlockSpec((D, bn), lambda j: (0, j)), pl.BlockSpec((1, bn), lambda j: (0, j))],
        out_specs=(pl.BlockSpec((N_DEV, bn), lambda j: (0, j)), _const((N_DEV, D))),
        compiler_params=_cp(("arbitrary",)),
    )(c_all, w_ada, b_shard)


def _head_sum_mats():
    e = np.zeros((AW, LANES), np.float32)
    for h in range(NH):
        e[h * DH:(h + 1) * DH, h] = 1.0
    return jnp.asarray(e, BF16), jnp.asarray(e.T.copy(), BF16)


def _dot2(x, w):
    hi = x.astype(BF16)
    lo = (x - hi.astype(F32)).astype(BF16)
    return jnp.dot(hi, w, preferred_element_type=F32) + jnp.dot(lo, w, preferred_element_type=F32)


def _fwd_in(x, mod8, n1g, w_in_p, e512, et512, qg512, kg512, bf128, tm, ex=None):
    S = x.shape[0]

    def body(x_ref, mod_ref, n1g_ref, w_ref, e_ref, et_ref, qg_ref, kg_ref, bf_ref,
             h1_ref, qh_ref, kh_ref, v_ref, qn_ref, kn_ref, rq_ref, rk_ref, fgb_ref, alin_ref, agate_ref, u0_ref):
        xv = x_ref[...]
        r1 = lax.rsqrt(jnp.mean(xv * xv, axis=-1, keepdims=True) + EPS)
        h = (xv * r1) * (n1g_ref[...] * (1.0 + mod_ref[1:2, :])) + mod_ref[0:1, :]
        hb = h.astype(BF16)
        h1_ref[...] = hb

        def seg(a, b):
            return jnp.dot(hb, w_ref[:, a:b], preferred_element_type=F32)

        def headnorm(t, g_ref, scale, n_ref, r_ref, o_ref):
            ss = _dot2(t * t, e_ref[...])
            r = lax.rsqrt(ss * (1.0 / DH) + EPS)
            tn = t * _dot2(r, et_ref[...])
            n_ref[...] = tn.astype(BF16)
            r_ref[...] = r
            o_ref[...] = (tn * (g_ref[...] * scale)).astype(BF16)

        headnorm(seg(0, 512), qg_ref, DH ** -0.5, qn_ref, rq_ref, qh_ref)
        headnorm(seg(512, 1024), kg_ref, 1.0, kn_ref, rk_ref, kh_ref)
        v_ref[...] = seg(1024, 1536).astype(BF16)
        alin = seg(1536, 2048)
        agate = seg(2048, 2560)
        alin_ref[...] = alin.astype(BF16)
        agate_ref[...] = agate.astype(BF16)
        u0_ref[...] = alin * _sigmoid(agate)
        fgb_ref[...] = seg(2560, NP) + bf_ref[...]

    bf = lambda: _sds((S, AW), BF16)
    xs = ex.srcs if ex else []
    outs = pl.pallas_call(
        _hosted(body, ex, 9, 12, 0, S // tm), name="fwd_in", grid=(S // tm,),
        out_shape=(_sds((S, D), BF16), bf(), bf(), bf(), bf(), bf(), _sds((S, LANES), F32), _sds((S, LANES), F32),
                   _sds((S, LANES), F32), bf(), bf(), _sds((S, CW), F32), *(ex.out_shapes if ex else [])),
        in_specs=[_rows(tm, D), _const((8, D)), _const((1, D)), _const((D, NP)), _const((AW, LANES)), _const((LANES, AW)),
                  _const((1, AW)), _const((1, AW)), _const((1, LANES)), *(ex.in_specs if ex else [])],
        out_specs=(_rows(tm, D), _rows(tm, AW), _rows(tm, AW), _rows(tm, AW), _rows(tm, AW), _rows(tm, AW),
                   _rows(tm, LANES), _rows(tm, LANES), _rows(tm, LANES), _rows(tm, AW), _rows(tm, AW), _rows(tm, CW),
                   *(ex.out_specs if ex else [])),
        scratch_shapes=ex.scratch if ex else [],
        compiler_params=_cp(("arbitrary",)),
    )(x, mod8, n1g, w_in_p, e512, et512, qg512, kg512, bf128, *xs)
    return outs[:12], list(outs[12:])


def _split3(f):
    f1 = f.astype(BF16).astype(F32)
    f2 = (f - f1).astype(BF16).astype(F32)
    return f1, f2, f - f1 - f2


def _dot3(w, x):
    return sum(jnp.dot(w, piece.astype(BF16), preferred_element_type=F32) for piece in _split3(x))


def _fwd_decay(fgb, qh, kh, vb, shift, tm, ex=None):
    S = fgb.shape[0]

    def body(shift_ref, fgb_ref, qh_ref, kh_ref, vb_ref, qa_ref, ka_ref, va_ref, f_ref, carry_ref):
        @pl.when(pl.program_id(0) == 0)
        def _():
            carry_ref[...] = jnp.zeros_like(carry_ref)

        fb = fgb_ref[...]
        lf = jnp.minimum(fb, 0.0) - jnp.log1p(jnp.exp(-jnp.abs(fb)))
        tri = (lax.broadcasted_iota(jnp.int32, (tm, tm), 0) >= lax.broadcasted_iota(jnp.int32, (tm, tm), 1)
               ).astype(F32).astype(BF16)
        cs = _dot3(tri, lf) + carry_ref[0:1, :]
        f_ref[...] = cs
        carry_ref[...] = jnp.broadcast_to(cs[tm - 1:tm, :], carry_ref.shape)
        lane = _lane((tm, LANES))
        s1, s2, s3 = _split3(jnp.zeros((tm, LANES), F32) - shift_ref[0, 0])
        tail_q = jnp.where((lane >= 67) & (lane < 70), 1.0,
                           jnp.where(lane == 70, s1, jnp.where(lane == 71, s2, jnp.where(lane == 72, s3, 0.0))))
        tail_k = jnp.where(((lane >= 64) & (lane < 67)) | ((lane >= 70) & (lane < 73)), 1.0, 0.0)
        tail_v = jnp.where(lane == DH, 1.0, 0.0)
        for p in range(NH // 2):
            qp = qh_ref[:, p * LANES:(p + 1) * LANES].astype(F32)
            kp = kh_ref[:, p * LANES:(p + 1) * LANES].astype(F32)
            vp = vb_ref[:, p * LANES:(p + 1) * LANES].astype(F32)
            for hh in range(2):
                h = 2 * p + hh
                f1, f2, f3 = _split3(cs[:, h:h + 1])
                qb = qp if hh == 0 else pltpu.roll(qp, 64, 1)
                kb = kp if hh == 0 else pltpu.roll(kp, 64, 1)
                vh = vp if hh == 0 else pltpu.roll(vp, 64, 1)
                augq = jnp.where(lane == 64, f1, jnp.where(lane == 65, f2, jnp.where(lane == 66, f3, tail_q)))
                augk = jnp.where(lane == 67, -f1, jnp.where(lane == 68, -f2, jnp.where(lane == 69, -f3, tail_k)))
                qa_ref[h] = jnp.where(lane < DH, qb, augq).astype(BF16)
                ka_ref[h] = jnp.where(lane < DH, kb, augk).astype(BF16)
                va_ref[h] = jnp.where(lane < DH, vh, tail_v).astype(BF16)

    hm = pl.BlockSpec((NH, tm, LANES), lambda i: (0, i, 0))
    hms = _sds((NH, S, LANES), BF16)
    outs = pl.pallas_call(
        _hosted(body, ex, 5, 4, 1, S // tm), name="fwd_decay", grid=(S // tm,),
        out_shape=(hms, hms, hms, _sds((S, LANES), F32), *(ex.out_shapes if ex else [])),
        in_specs=[SMEM_SPEC, _rows(tm, LANES), _rows(tm, AW), _rows(tm, AW), _rows(tm, AW),
                  *(ex.in_specs if ex else [])],
        out_specs=(hm, hm, hm, _rows(tm, LANES), *(ex.out_specs if ex else [])),
        scratch_shapes=[pltpu.VMEM((8, LANES), F32), *(ex.scratch if ex else [])],
        compiler_params=_cp(("arbitrary",)),
    )(shift, fgb, qh, kh, vb, *(ex.srcs if ex else []))
    return outs[:4], list(outs[4:])


SKIP = -106.0
SKIP_P = -88.0


def _block_loops(first, step, needed, run):
    def both(j):
        return jnp.logical_and(needed(0, j), needed(1, j))

    def walk(heads):
        def go(j):
            run(j, heads)
            return j + step
        return go

    j = lax.while_loop(both, walk((0, 1)), first)
    lax.while_loop(functools.partial(needed, 0), walk((0,)), j)
    lax.while_loop(functools.partial(needed, 1), walk((1,)), j)


def _logit_bound(qg, kg):
    return (2.0 * 1.03 * DH ** 0.5 * jnp.max(jnp.abs(qg)) * jnp.max(jnp.abs(kg))).reshape(1, 1)


def _skip_tables(f, tq):
    return f[0::tq, :NH].T, f[tq - 1::tq, :NH].T


SMEM_SPEC = pl.BlockSpec(memory_space=pltpu.SMEM)


def _causal_rect(rows, cols, col0):
    return (lax.broadcasted_iota(jnp.int32, (rows, cols), 0)
            >= lax.broadcasted_iota(jnp.int32, (rows, cols), 1) + col0)


SHIFT_MAX = 60.0


def _shift(bd):
    return jnp.where(bd <= SHIFT_MAX, 0.5 * bd, 0.0)


def _attn_fwd(qa, ka, va, fs, fe, bd, tq):
    S = qa.shape[1]
    nq = S // tq

    def body(fs_ref, fe_ref, bd_ref, qa_ref, ka_ref, va_ref, o_ref, lser_ref, m_ref, acc_ref):
        pr, i = pl.program_id(0), pl.program_id(1)
        sel_a = _lane((tq, LANES)) < DH
        acc_ref[...] = jnp.zeros_like(acc_ref)

        def logits(j, hh, masked):
            start = pl.multiple_of(j * tq, tq)
            s = lax.dot_general(qa_ref[hh], ka_ref[hh, pl.ds(start, tq), :], NT, preferred_element_type=F32)
            if masked:
                s = jnp.where(_causal_rect(tq, tq, 0), s, NEG)
            return s, va_ref[hh, pl.ds(start, tq), :]

        def shifted_step(j, heads, masked=False):
            for hh in heads:
                s, vb = logits(j, hh, masked)
                acc_ref[hh] += jnp.dot(jnp.exp(s).astype(BF16), vb, preferred_element_type=F32)

        def online_step(j, heads, masked=False):
            for hh in heads:
                s, vb = logits(j, hh, masked)
                m_prev = m_ref[hh]
                m_new = jnp.maximum(m_prev, jnp.max(s, axis=1, keepdims=True))
                p = jnp.exp(s - jnp.tile(m_new, (1, tq // LANES)))
                m_ref[hh] = m_new
                acc_ref[hh] = jnp.exp(m_prev - m_new) * acc_ref[hh] + jnp.dot(p.astype(BF16), vb,
                                                                              preferred_element_type=F32)

        def needed(slack, hh, j):
            top = fs_ref[2 * pr + hh, i] + slack
            return jnp.logical_and(j >= 0, top - fe_ref[2 * pr + hh, jnp.maximum(j, 0)] >= SKIP)

        def shifted_diagonal():
            h = tq // 2
            base = pl.multiple_of(i * tq, tq)
            for hh in (0, 1):
                s0 = lax.dot_general(qa_ref[hh], ka_ref[hh, pl.ds(base, h), :], NT, preferred_element_type=F32)
                s0 = jnp.where(_causal_rect(tq, h, 0), s0, NEG)
                acc_ref[hh] += jnp.dot(jnp.exp(s0).astype(BF16), va_ref[hh, pl.ds(base, h), :],
                                       preferred_element_type=F32)
                newer = pl.multiple_of(base + h, h)
                s1 = lax.dot_general(qa_ref[hh, h:tq, :], ka_ref[hh, pl.ds(newer, h), :], NT,
                                     preferred_element_type=F32)
                s1 = jnp.where(_causal_rect(h, h, 0), s1, NEG)
                acc_ref[hh, h:tq, :] += jnp.dot(jnp.exp(s1).astype(BF16), va_ref[hh, pl.ds(newer, h), :],
                                                preferred_element_type=F32)

        @pl.when(bd_ref[0, 0] <= SHIFT_MAX)
        def _():
            m_ref[...] = jnp.zeros_like(m_ref)
            shifted_diagonal()
            _block_loops(i - 1, -1, functools.partial(needed, 0.0), shifted_step)

        @pl.when(bd_ref[0, 0] > SHIFT_MAX)
        def _():
            m_ref[...] = jnp.full(m_ref.shape, NEG, F32)
            online_step(i, (0, 1), masked=True)
            _block_loops(i - 1, -1, functools.partial(needed, bd_ref[0, 0]), online_step)

        outs, lses = [], []
        for hh in range(2):
            acc = acc_ref[hh]
            row_sum = jnp.broadcast_to(acc[:, DH:DH + 1], (tq, LANES))
            outs.append(acc / row_sum)
            lses.append(m_ref[hh] + jnp.log(row_sum))
        o_ref[...] = jnp.where(sel_a, outs[0], pltpu.roll(outs[1], 64, 1))
        row = lax.broadcasted_iota(jnp.int32, (8, tq), 0)
        lser_ref[0, 0] = jnp.where(row == 0, lses[0].T[0:8, :], lses[1].T[0:8, :])

    return pl.pallas_call(
        body, name="attn_fwd", grid=(NH // 2, nq),
        out_shape=(_sds((S, AW), F32), _sds((NH // 2, nq, 8, tq), F32)),
        in_specs=[SMEM_SPEC, SMEM_SPEC, SMEM_SPEC,
                  pl.BlockSpec((2, tq, LANES), lambda p, i: (p, i, 0)),
                  pl.BlockSpec((2, S, LANES), lambda p, i: (p, 0, 0)),
                  pl.BlockSpec((2, S, LANES), lambda p, i: (p, 0, 0))],
        out_specs=(pl.BlockSpec((tq, LANES), lambda p, i: (i, p)),
                   pl.BlockSpec((1, 1, 8, tq), lambda p, i: (p, i, 0, 0))),
        scratch_shapes=[pltpu.VMEM((2, tq, LANES), F32), pltpu.VMEM((2, tq, LANES), F32)],
        compiler_params=_cp(("parallel", "parallel")),
    )(fs, fe, bd, qa, ka, va)


HALO = 32
FWD_CHUNK = 64
BWD_CHUNK = 32


def _halo_prev(tm):
    return pl.BlockSpec((HALO, CW), lambda i: (jnp.maximum(i * (tm // HALO) - 1, 0), 0))


SUBLANES = 8
SHIFT_ROWS = 24


def _shifted_copies(ext_ref, sh_ref, tm):
    for k in range(1, SUBLANES):
        sh_ref[k - 1, 0:tm + SHIFT_ROWS, :] = ext_ref[k:k + tm + SHIFT_ROWS, :]


def _tap_windows(ext_ref, sh_ref, r0, offset, rows, shared):
    for k in range(SUBLANES):
        taps = sorted((offset(j), j) for j in range(KC) if offset(j) % SUBLANES == k)
        for group in ([taps] if shared and taps else [[t] for t in taps]):
            lo, hi = r0 + group[0][0] - k, r0 + group[-1][0] - k + rows
            window = ext_ref[lo:hi, :] if k == 0 else sh_ref[k - 1, lo:hi, :]
            for off, j in group:
                at = r0 + off - k - lo
                yield j, window[at:at + rows, :]


def _fwd_conv(u0, w32, cb, lng, lnb, beta_c, tm, ex=None):
    S = u0.shape[0]

    def body(cur_ref, prev_ref, w_ref, cb_ref, lng_ref, lnb_ref, beta_ref, u1_ref, mc_ref, ext_ref, sh_ref):
        i = pl.program_id(0)
        ext_ref[0:HALO, :] = jnp.where(i == 0, 0.0, prev_ref[...])
        ext_ref[HALO:, :] = cur_ref[...]
        _shifted_copies(ext_ref, sh_ref, tm)
        for r0 in range(0, tm, FWD_CHUNK):
            acc = jnp.zeros((FWD_CHUNK, CW), F32) + cb_ref[...]
            for j, rows in _tap_windows(ext_ref, sh_ref, r0, lambda j: 2 + j, FWD_CHUNK, shared=False):
                acc = acc + w_ref[j:j + 1, :] * rows
            u1_ref[r0:r0 + FWD_CHUNK, :] = acc
        u1 = u1_ref[...]
        mu = jnp.mean(u1, axis=-1, keepdims=True)
        d = u1 - mu
        rstd = lax.rsqrt(jnp.mean(d * d, axis=-1, keepdims=True) + EPS)
        u2 = d * rstd * lng_ref[...] + lnb_ref[...]
        u3 = u2 * _sigmoid(u2)
        rc = lax.rsqrt(jnp.mean(u3 * u3, axis=-1, keepdims=True) + EPS)
        mc_ref[...] = (u3 * rc * beta_ref[...]).astype(BF16)

    outs = pl.pallas_call(
        _hosted(body, ex, 7, 2, 2, S // tm), name="fwd_conv", grid=(S // tm,),
        out_shape=(_sds((S, CW), F32), _sds((S, CW), BF16), *(ex.out_shapes if ex else [])),
        in_specs=[_rows(tm, CW), _halo_prev(tm), _const((HALO, CW)), _const((1, CW)), _const((1, CW)), _const((1, CW)),
                  _const((1, CW)), *(ex.in_specs if ex else [])],
        out_specs=(_rows(tm, CW), _rows(tm, CW), *(ex.out_specs if ex else [])),
        scratch_shapes=[pltpu.VMEM((tm + HALO, CW), F32), pltpu.VMEM((SUBLANES - 1, tm + HALO, CW), F32),
                        *(ex.scratch if ex else [])],
        compiler_params=_cp(("arbitrary",)),
    )(u0, u0, w32, cb, lng, lnb, beta_c, *(ex.srcs if ex else []))
    return outs[:2], list(outs[2:])


def _fwd_out(o_attn, mc, x, mod8, n2g, beta_a, w_out, tm):
    S = x.shape[0]

    def body(o_ref, mc_ref, x_ref, mod_ref, n2g_ref, beta_ref, w_ref, mg_ref, ob_ref, x2_ref, h2_ref):
        ov = o_ref[...]
        ra = lax.rsqrt(jnp.mean(ov * ov, axis=-1, keepdims=True) + EPS)
        ma = (ov * ra * beta_ref[...]).astype(BF16)
        mcv = mc_ref[...]
        mg_ref[:, 0:AW] = ma
        mg_ref[:, AW:D] = mcv
        o = (jnp.dot(ma, w_ref[0:AW, :], preferred_element_type=F32)
             + jnp.dot(mcv, w_ref[AW:D, :], preferred_element_type=F32))
        ob_ref[...] = o.astype(BF16)
        x2 = x_ref[...] + mod_ref[2:3, :] * o
        x2_ref[...] = x2
        r2 = lax.rsqrt(jnp.mean(x2 * x2, axis=-1, keepdims=True) + EPS)
        h2_ref[...] = ((x2 * r2) * (n2g_ref[...] * (1.0 + mod_ref[4:5, :])) + mod_ref[3:4, :]).astype(BF16)

    return pl.pallas_call(
        body, name="fwd_out", grid=(S // tm,),
        out_shape=(_sds((S, D), BF16), _sds((S, D), BF16), _sds((S, D), F32), _sds((S, D), BF16)),
        in_specs=[_rows(tm, AW), _rows(tm, CW), _rows(tm, D), _const((8, D)), _const((1, D)), _const((1, AW)),
                  _const((D, D))],
        out_specs=(_rows(tm, D), _rows(tm, D), _rows(tm, D), _rows(tm, D)),
        compiler_params=_cp(("parallel",)),
    )(o_attn, mc, x, mod8, n2g, beta_a, w_out)


def _fwd_ffn(h2, w1, w2, x2, tgt, mod8, tm):
    S = h2.shape[0]
    nk = w1.shape[0]
    bf = w1.shape[2]

    def body(h2_ref, w1_ref, w2_ref, x2_ref, tgt_ref, mod_ref, r_ref, dy_ref, loss_ref, dg2_ref):
        @pl.when(pl.program_id(0) == 0)
        def _():
            loss_ref[...] = jnp.zeros_like(loss_ref)
            dg2_ref[...] = jnp.zeros_like(dg2_ref)

        f2 = None
        for k in range(nk):
            r = jnp.maximum(jnp.dot(h2_ref[...], w1_ref[k], preferred_element_type=F32), 0.0)
            r_ref[:, k * bf:(k + 1) * bf] = r.astype(BF16)
            part = jnp.dot((r * r).astype(BF16), w2_ref[k], preferred_element_type=F32)
            f2 = part if f2 is None else f2 + part
        e = x2_ref[...] + mod_ref[5:6, :] * f2 - tgt_ref[...]
        dy = e * (1.0 / D)
        dy_ref[...] = dy
        loss_ref[...] += 0.5 * jnp.sum(jnp.sum(e * dy, axis=1, keepdims=True), axis=0, keepdims=True)
        dg2_ref[...] += jnp.sum((dy * f2).reshape(tm // 8, 8, D), axis=0)

    once = pl.Buffered(1)
    return pl.pallas_call(
        body, name="fwd_ffn", grid=(S // tm,),
        out_shape=(_sds((S, DFF), BF16), _sds((S, D), F32), _sds((8, LANES), F32), _sds((8, D), F32)),
        in_specs=[_rows(tm, D), pl.BlockSpec((nk, D, bf), lambda i: (0, 0, 0), pipeline_mode=once),
                  pl.BlockSpec((nk, bf, D), lambda i: (0, 0, 0), pipeline_mode=once), _rows(tm, D), _rows(tm, D),
                  _const((8, D))],
        out_specs=(_rows(tm, DFF), _rows(tm, D), _const((8, LANES)), _const((8, D))),
        compiler_params=_cp(("arbitrary",)),
    )(h2, w1, w2, x2, tgt, mod8)


def _bwd_ffn(dy, mod8, r, w1, w2, tm):
    S = dy.shape[0]
    nk = w1.shape[0]
    bf = w1.shape[2]

    def body(dy_ref, mod_ref, r_ref, w1_ref, w2_ref, df2_ref, df1_ref, dh2_ref):
        df2 = (dy_ref[...] * mod_ref[5:6, :]).astype(BF16)
        df2_ref[...] = df2
        dh2 = None
        for k in range(nk):
            da = lax.dot_general(df2, w2_ref[k], NT, preferred_element_type=F32)
            df1 = (da * (2.0 * r_ref[:, k * bf:(k + 1) * bf].astype(F32))).astype(BF16)
            df1_ref[:, k * bf:(k + 1) * bf] = df1
            part = lax.dot_general(df1, w1_ref[k], NT, preferred_element_type=F32)
            dh2 = part if dh2 is None else dh2 + part
        dh2_ref[...] = dh2

    once = pl.Buffered(1)
    return pl.pallas_call(
        body, name="bwd_ffn", grid=(S // tm,),
        out_shape=(_sds((S, D), BF16), _sds((S, DFF), BF16), _sds((S, D), F32)),
        in_specs=[_rows(tm, D), _const((8, D)), _rows(tm, DFF),
                  pl.BlockSpec((nk, D, bf), lambda i: (0, 0, 0), pipeline_mode=once),
                  pl.BlockSpec((nk, bf, D), lambda i: (0, 0, 0), pipeline_mode=once)],
        out_specs=(_rows(tm, D), _rows(tm, DFF), _rows(tm, D)),
        compiler_params=_cp(("parallel",)),
    )(dy, mod8, r, w1, w2)


def _token_tile(S, want):
    while S % want:
        want //= 2
    return want


def _wgrad(a, b, name, square_a=False, col_pieces=False, tk=2048, bm=1024, bn=1024):
    S, M = a.shape
    N = b.shape[1]
    bm, bn, tk = min(bm, M), min(bn, N), _token_tile(S, tk)
    nk = S // tk

    def body(a_ref, b_ref, o_ref, acc_ref):
        av = a_ref[...]
        if square_a:
            af = av.astype(F32)
            av = (af * af).astype(BF16)
        part = lax.dot_general(av, b_ref[...], TN, preferred_element_type=F32)

        @pl.when(pl.program_id(2) == 0)
        def _():
            acc_ref[...] = part

        @pl.when(pl.program_id(2) > 0)
        def _():
            acc_ref[...] += part

        @pl.when(pl.program_id(2) == nk - 1)
        def _():
            if col_pieces:
                o_ref[0] = acc_ref[...].astype(BF16)
            else:
                o_ref[...] = acc_ref[...].astype(BF16)

    if col_pieces:
        out_shape, out_spec = _sds((N // bn, M, bn), BF16), pl.BlockSpec((1, bm, bn), lambda mi, ni, k: (ni, mi, 0))
    else:
        out_shape, out_spec = _sds((M, N), BF16), pl.BlockSpec((bm, bn), lambda mi, ni, k: (mi, ni))
    return pl.pallas_call(
        body, name=name, grid=(M // bm, N // bn, nk), out_shape=out_shape,
        in_specs=[pl.BlockSpec((tk, bm), lambda mi, ni, k: (k, mi)), pl.BlockSpec((tk, bn), lambda mi, ni, k: (k, ni))],
        out_specs=out_spec, scratch_shapes=[pltpu.VMEM((bm, bn), F32)],
        compiler_params=_cp(("parallel", "parallel", "arbitrary")),
    )(a, b)


def _wgrad_in(h1, pieces, tk=1024):
    S = h1.shape[0]
    tk = _token_tile(S, tk)
    widths = [p.shape[1] for p in pieces]
    offs = [sum(widths[:t]) for t in range(len(widths))]

    def body(a_ref, *refs):
        o_ref, acc_ref = refs[-2:]

        @pl.when(pl.program_id(0) == 0)
        def _():
            acc_ref[...] = jnp.zeros_like(acc_ref)

        for b_ref, off, w in zip(refs[:-2], offs, widths):
            acc_ref[:, off:off + w] += lax.dot_general(a_ref[...], b_ref[...], TN, preferred_element_type=F32)

        @pl.when(pl.program_id(0) == S // tk - 1)
        def _():
            o_ref[...] = acc_ref[...].astype(BF16)

    return pl.pallas_call(
        body, name="wgrad_in", grid=(S // tk,), out_shape=_sds((D, NP), BF16),
        in_specs=[_rows(tk, D)] + [_rows(tk, w) for w in widths], out_specs=_const((D, NP)),
        scratch_shapes=[pltpu.VMEM((D, NP), F32)], compiler_params=_cp(("arbitrary",)),
    )(h1, *pieces)


def _colsum8(t):
    return jnp.sum(t.reshape(t.shape[0] // 8, 8, t.shape[1]), axis=0)


def _bwd_mid(dh2, dy, x2, ob, o_attn, u1, mod8, n2g, beta_a, beta_c, lng, lnb, w_out, e512, tm, tq, ex=None):
    S = dy.shape[0]

    def body(dh2_ref, dy_ref, x2_ref, ob_ref, oa_ref, u1_ref, mod_ref, n2g_ref, ba_ref, bc_ref, lng_ref, lnb_ref, w_ref,
             e_ref, dx2_ref, do_ref, doa_ref, du1_ref, acc_d_ref, acc_h_ref, dr_ref):
        @pl.when(pl.program_id(0) == 0)
        def _():
            acc_d_ref[...] = jnp.zeros_like(acc_d_ref)
            acc_h_ref[...] = jnp.zeros_like(acc_h_ref)

        x2 = x2_ref[...]
        dh2 = dh2_ref[...]
        r2 = lax.rsqrt(jnp.mean(x2 * x2, axis=-1, keepdims=True) + EPS)
        xn2 = x2 * r2
        gain = n2g_ref[...] * (1.0 + mod_ref[4:5, :])
        dxn = dh2 * gain
        dx2 = dy_ref[...] + r2 * (dxn - xn2 * jnp.mean(dxn * xn2, axis=-1, keepdims=True))
        dx2_ref[...] = dx2
        t = dh2 * xn2
        acc_d_ref[0] += _colsum8(dh2)
        acc_d_ref[1] += _colsum8(t * n2g_ref[...])
        acc_d_ref[2] += _colsum8(t * (1.0 + mod_ref[4:5, :]))
        acc_d_ref[3] += _colsum8(dx2 * ob_ref[...].astype(F32))
        do = (dx2 * mod_ref[2:3, :]).astype(BF16)
        do_ref[...] = do
        dma = lax.dot_general(do, w_ref[0:AW, :], NT, preferred_element_type=F32)
        dmc = lax.dot_general(do, w_ref[AW:D, :], NT, preferred_element_type=F32)
        ov = oa_ref[...]
        ra = lax.rsqrt(jnp.mean(ov * ov, axis=-1, keepdims=True) + EPS)
        on = ov * ra
        acc_h_ref[0] += _colsum8(dma * on)
        don = dma * ba_ref[...]
        doa = (ra * (don - on * jnp.mean(don * on, axis=-1, keepdims=True))).astype(BF16)
        doa_ref[...] = doa
        delta_t = _dot2(doa.astype(F32) * ov, e_ref[...]).T
        for p in range(NH // 2):
            dr_ref[p, 0] = delta_t[2 * p:2 * p + 8, :]
        u1 = u1_ref[...]
        mu = jnp.mean(u1, axis=-1, keepdims=True)
        d = u1 - mu
        rstd = lax.rsqrt(jnp.mean(d * d, axis=-1, keepdims=True) + EPS)
        uh = d * rstd
        u2 = uh * lng_ref[...] + lnb_ref[...]
        sg = _sigmoid(u2)
        u3 = u2 * sg
        rc = lax.rsqrt(jnp.mean(u3 * u3, axis=-1, keepdims=True) + EPS)
        u3n = u3 * rc
        acc_h_ref[1] += _colsum8(dmc * u3n)
        du3n = dmc * bc_ref[...]
        du3 = rc * (du3n - u3n * jnp.mean(du3n * u3n, axis=-1, keepdims=True))
        du2 = du3 * (sg * (1.0 + u2 * (1.0 - sg)))
        acc_h_ref[2] += _colsum8(du2 * uh)
        acc_h_ref[3] += _colsum8(du2)
        duh = du2 * lng_ref[...]
        du1_ref[...] = rstd * (duh - jnp.mean(duh, axis=-1, keepdims=True)
                               - uh * jnp.mean(duh * uh, axis=-1, keepdims=True))

    per = tq // tm
    outs = pl.pallas_call(
        _hosted(body, ex, 14, 7, 0, S // tm), name="bwd_mid", grid=(S // tm,),
        out_shape=(_sds((S, D), F32), _sds((S, D), BF16), _sds((S, AW), BF16), _sds((S, CW), F32),
                   _sds((4, 8, D), F32), _sds((4, 8, AW), F32), _sds((NH // 2, S // tq, 8, tq), F32),
                   *(ex.out_shapes if ex else [])),
        in_specs=[_rows(tm, D), _rows(tm, D), _rows(tm, D), _rows(tm, D), _rows(tm, AW), _rows(tm, CW), _const((8, D)),
                  _const((1, D)), _const((1, AW)), _const((1, CW)), _const((1, CW)), _const((1, CW)), _const((D, D)),
                  _const((AW, LANES)), *(ex.in_specs if ex else [])],
        out_specs=(_rows(tm, D), _rows(tm, D), _rows(tm, AW), _rows(tm, CW), _const((4, 8, D)), _const((4, 8, AW)),
                   pl.BlockSpec((NH // 2, 1, 8, tm), lambda i: (0, i // per, 0, i % per)),
                   *(ex.out_specs if ex else [])),
        scratch_shapes=ex.scratch if ex else [],
        compiler_params=_cp(("arbitrary",)),
    )(dh2, dy, x2, ob, o_attn, u1, mod8, n2g, beta_a, beta_c, lng, lnb, w_out, e512, *(ex.srcs if ex else []))
    return outs[:7], list(outs[7:])


def _attn_bwd(qa, ka, v, do, lser, dr, fs, fe, bd, tq):
    S = qa.shape[1]
    nq = S // tq

    def body(fs_ref, fe_ref, bd_ref, ka_ref, v_ref, qa_ref, do_ref, lse_ref, dr_ref, dqa_hbm, dka_ref, dv_ref,
             accq_ref, acck_ref, accv_ref, out_sem):
        pr, j = pl.program_id(0), pl.program_id(1)
        sel_a = _lane((tq, LANES)) < DH
        vv = v_ref[...]
        zb = jnp.zeros_like(vv)
        vs = [jnp.where(sel_a, vv, zb), jnp.where(sel_a, zb, vv)]
        acck_ref[...] = jnp.zeros_like(acck_ref)
        accv_ref[...] = jnp.zeros_like(accv_ref)

        @pl.when(j == 0)
        def _():
            accq_ref[...] = jnp.zeros_like(accq_ref)

        def q_step(i, heads, masked=False):
            start = pl.multiple_of(i * tq, tq)
            dob = do_ref[pl.ds(start, tq), :]
            lse8 = lse_ref[0, i]
            dr8 = dr_ref[0, i]
            for hh in heads:
                qb = qa_ref[hh, pl.ds(start, tq), :]
                kb = ka_ref[hh]
                st = lax.dot_general(kb, qb, NT, preferred_element_type=F32)
                pt = jnp.exp(st - lse8[hh:hh + 1, :])
                if masked:
                    keep = (lax.broadcasted_iota(jnp.int32, (tq, tq), 0)
                            <= lax.broadcasted_iota(jnp.int32, (tq, tq), 1))
                    pt = jnp.where(keep, pt, 0.0)
                accv_ref[hh] += jnp.dot(pt.astype(BF16), dob, preferred_element_type=F32)
                dpt = lax.dot_general(vs[hh], dob, NT, preferred_element_type=F32)
                dst = (pt * (dpt - dr8[hh:hh + 1, :])).astype(BF16)
                acck_ref[hh] += jnp.dot(dst, qb, preferred_element_type=F32)
                accq_ref[hh, pl.ds(start, tq), :] += lax.dot_general(dst, kb, TN, preferred_element_type=F32)

        def needed(hh, i):
            top = fs_ref[2 * pr + hh, jnp.minimum(i, nq - 1)] + bd_ref[0, 0]
            return jnp.logical_and(i < nq, top - fe_ref[2 * pr + hh, j] >= SKIP_P)

        q_step(j, (0, 1), masked=True)
        _block_loops(j + 1, 1, needed, q_step)
        dka_ref[...] = acck_ref[...]
        dv_ref[...] = jnp.where(sel_a, accv_ref[0], accv_ref[1]).astype(BF16)

        @pl.when(j == nq - 1)
        def _():
            out = pltpu.make_async_copy(accq_ref, dqa_hbm.at[pl.ds(2 * pr, 2)], out_sem)
            out.start()
            out.wait()

    once = pl.Buffered(1)
    return pl.pallas_call(
        body, name="attn_bwd", grid=(NH // 2, nq),
        out_shape=(_sds((NH, S, LANES), F32), _sds((NH, S, LANES), F32), _sds((S, AW), BF16)),
        in_specs=[SMEM_SPEC, SMEM_SPEC, SMEM_SPEC,
                  pl.BlockSpec((2, tq, LANES), lambda p, j: (p, j, 0)),
                  pl.BlockSpec((tq, LANES), lambda p, j: (j, p)),
                  pl.BlockSpec((2, S, LANES), lambda p, j: (p, 0, 0), pipeline_mode=once),
                  pl.BlockSpec((S, LANES), lambda p, j: (0, p), pipeline_mode=once),
                  pl.BlockSpec((1, nq, 8, tq), lambda p, j: (p, 0, 0, 0)),
                  pl.BlockSpec((1, nq, 8, tq), lambda p, j: (p, 0, 0, 0))],
        out_specs=(pl.BlockSpec(memory_space=pl.ANY),
                   pl.BlockSpec((2, tq, LANES), lambda p, j: (p, j, 0)),
                   pl.BlockSpec((tq, LANES), lambda p, j: (j, p))),
        scratch_shapes=[pltpu.VMEM((2, S, LANES), F32), pltpu.VMEM((2, tq, LANES), F32),
                        pltpu.VMEM((2, tq, LANES), F32), pltpu.SemaphoreType.DMA],
        compiler_params=_cp(("arbitrary", "arbitrary")),
    )(fs, fe, bd, ka, v, qa, do, lser, dr)


def _bwd_conv(du1, u0, alin, agate, w32, tm, ex=None):
    S = du1.shape[0]
    nt = S // tm

    def body(du_ref, dun_ref, u0_ref, u0p_ref, alin_ref, agate_ref, w_ref,
             dalin_ref, dagate_ref, dw_ref, db_ref, extd_ref, extu_ref, du0_ref, shd_ref, shu_ref):
        i = pl.program_id(0)

        @pl.when(i == 0)
        def _():
            dw_ref[...] = jnp.zeros_like(dw_ref)
            db_ref[...] = jnp.zeros_like(db_ref)

        extd_ref[0:tm, :] = du_ref[...]
        extd_ref[tm:, :] = jnp.where(i == nt - 1, 0.0, dun_ref[...])
        extu_ref[0:HALO, :] = jnp.where(i == 0, 0.0, u0p_ref[...])
        extu_ref[HALO:, :] = u0_ref[...]
        _shifted_copies(extd_ref, shd_ref, tm)
        _shifted_copies(extu_ref, shu_ref, tm)
        db_ref[...] += _colsum8(du_ref[...])
        for r0 in range(0, tm, BWD_CHUNK):
            acc = jnp.zeros((BWD_CHUNK, CW), F32)
            for j, rows in _tap_windows(extd_ref, shd_ref, r0, lambda j: 30 - j, BWD_CHUNK, shared=True):
                acc = acc + w_ref[j:j + 1, :] * rows
            du0_ref[r0:r0 + BWD_CHUNK, :] = acc
            duc = du_ref[r0:r0 + BWD_CHUNK, :]
            for j, rows in _tap_windows(extu_ref, shu_ref, r0, lambda j: 2 + j, BWD_CHUNK, shared=True):
                dw_ref[j] += _colsum8(duc * rows)
        du0 = du0_ref[...]
        al = alin_ref[...].astype(F32)
        sg = _sigmoid(agate_ref[...].astype(F32))
        dalin_ref[...] = (du0 * sg).astype(BF16)
        dagate_ref[...] = (du0 * al * sg * (1.0 - sg)).astype(BF16)

    nxt = pl.BlockSpec((HALO, CW), lambda i: (jnp.minimum((i + 1) * (tm // HALO), S // HALO - 1), 0))
    outs = pl.pallas_call(
        _hosted(body, ex, 7, 4, 5, nt), name="bwd_conv", grid=(nt,),
        out_shape=(_sds((S, CW), BF16), _sds((S, CW), BF16), _sds((HALO, 8, CW), F32), _sds((8, CW), F32),
                   *(ex.out_shapes if ex else [])),
        in_specs=[_rows(tm, CW), nxt, _rows(tm, CW), _halo_prev(tm), _rows(tm, CW), _rows(tm, CW), _const((HALO, CW)),
                  *(ex.in_specs if ex else [])],
        out_specs=(_rows(tm, CW), _rows(tm, CW), _const((HALO, 8, CW)), _const((8, CW)), *(ex.out_specs if ex else [])),
        scratch_shapes=[pltpu.VMEM((tm + HALO, CW), F32), pltpu.VMEM((tm + HALO, CW), F32), pltpu.VMEM((tm, CW), F32),
                        pltpu.VMEM((SUBLANES - 1, tm + HALO, CW), F32), pltpu.VMEM((SUBLANES - 1, tm + HALO, CW), F32),
                        *(ex.scratch if ex else [])],
        compiler_params=_cp(("arbitrary",)),
    )(du1, du1, u0, u0, alin, agate, w32, *(ex.srcs if ex else []))
    return outs[:4], list(outs[4:])


def _bwd_qk(dqa, dka, qn, kn, rq, rk, fgb, qg512, kg512, e512, et512, tm):
    S = qn.shape[0]
    nt = S // tm

    def body(dqa_ref, dka_ref, qn_ref, kn_ref, rq_ref, rk_ref, fgb_ref, qg_ref, kg_ref, e_ref, et_ref,
             dq_ref, dk_ref, dfg_ref, accg_ref, accb_ref, carry_ref):
        @pl.when(pl.program_id(0) == 0)
        def _():
            carry_ref[...] = jnp.zeros_like(carry_ref)
            accg_ref[...] = jnp.zeros_like(accg_ref)
            accb_ref[...] = jnp.zeros_like(accb_ref)

        lane = _lane((tm, LANES))
        sel_a = lane < DH
        df = jnp.zeros((tm, LANES), F32)
        for h in range(NH):
            col = dqa_ref[h][:, 64:65] - dka_ref[h][:, 67:68]
            df = jnp.where(lane == h, col, df)
        tri = (lax.broadcasted_iota(jnp.int32, (tm, tm), 0) <= lax.broadcasted_iota(jnp.int32, (tm, tm), 1)
               ).astype(F32).astype(BF16)
        dlf = _dot3(tri, df) + carry_ref[0:1, :]
        carry_ref[...] = jnp.broadcast_to(dlf[0:1, :], carry_ref.shape)
        dfg = jnp.where(lane < NH, dlf * _sigmoid(-fgb_ref[...]), 0.0)
        dfg_ref[...] = dfg.astype(BF16)
        accb_ref[...] += _colsum8(dfg)

        def norm_bwd(src_ref, n_ref, r_ref, g_ref, scale, slot):
            pairs = []
            for p in range(NH // 2):
                b = pltpu.roll(src_ref[2 * p + 1], 64, 1)
                pairs.append(jnp.where(sel_a, src_ref[2 * p], b))
            dh = jnp.concatenate(pairs, axis=1) * scale
            tn = n_ref[...].astype(F32)
            accg_ref[slot] += _colsum8(dh * tn)
            dn = dh * g_ref[...]
            mean = _dot2(dn * tn, e_ref[...]) * (1.0 / DH)
            corr = _dot2(mean, et_ref[...])
            rf = _dot2(r_ref[...], et_ref[...])
            return (rf * (dn - tn * corr)).astype(BF16)

        dq_ref[...] = norm_bwd(dqa_ref, qn_ref, rq_ref, qg_ref, DH ** -0.5, 0)
        dk_ref[...] = norm_bwd(dka_ref, kn_ref, rk_ref, kg_ref, 1.0, 1)

    rev = lambda n: pl.BlockSpec((tm, n), lambda i: (nt - 1 - i, 0))
    hm = pl.BlockSpec((NH, tm, LANES), lambda i: (0, nt - 1 - i, 0))
    dq, dk, dfg, accg, accb = pl.pallas_call(
        body, name="bwd_qk", grid=(nt,),
        out_shape=(_sds((S, AW), BF16), _sds((S, AW), BF16), _sds((S, LANES), BF16), _sds((2, 8, AW), F32),
                   _sds((8, LANES), F32)),
        in_specs=[hm, hm, rev(AW), rev(AW), rev(LANES), rev(LANES), rev(LANES), _const((1, AW)), _const((1, AW)),
                  _const((AW, LANES)), _const((LANES, AW))],
        out_specs=(rev(AW), rev(AW), rev(LANES), _const((2, 8, AW)), _const((8, LANES))),
        scratch_shapes=[pltpu.VMEM((8, LANES), F32)], compiler_params=_cp(("arbitrary",)),
    )(dqa, dka, qn, kn, rq, rk, fgb, qg512, kg512, e512, et512)
    return dq, dk, dfg, accg, accb


def _bwd_in(dq, dk, dv, dalin, dagate, dfg, w_in_p, x, dx2, mod8, n1g, tm, ex=None):
    S = x.shape[0]

    def body(dq_ref, dk_ref, dv_ref, dal_ref, dag_ref, dfg_ref, w_ref, x_ref, dx2_ref, mod_ref, n1g_ref,
             dx_ref, acc_ref):
        @pl.when(pl.program_id(0) == 0)
        def _():
            acc_ref[...] = jnp.zeros_like(acc_ref)

        def part(ref, a, b):
            return lax.dot_general(ref[...], w_ref[:, a:b], NT, preferred_element_type=F32)

        dh = (part(dq_ref, 0, 512) + part(dk_ref, 512, 1024) + part(dv_ref, 1024, 1536) + part(dal_ref, 1536, 2048)
              + part(dag_ref, 2048, 2560) + part(dfg_ref, 2560, NP))
        xv = x_ref[...]
        r1 = lax.rsqrt(jnp.mean(xv * xv, axis=-1, keepdims=True) + EPS)
        xn = xv * r1
        gain = n1g_ref[...] * (1.0 + mod_ref[1:2, :])
        t = dh * xn
        acc_ref[0] += _colsum8(dh)
        acc_ref[1] += _colsum8(t * n1g_ref[...])
        acc_ref[2] += _colsum8(t * (1.0 + mod_ref[1:2, :]))
        dxn = dh * gain
        dx_ref[...] = dx2_ref[...] + r1 * (dxn - xn * jnp.mean(dxn * xn, axis=-1, keepdims=True))

    outs = pl.pallas_call(
        _hosted(body, ex, 11, 2, 0, S // tm), name="bwd_in", grid=(S // tm,),
        out_shape=(_sds((S, D), F32), _sds((3, 8, D), F32), *(ex.out_shapes if ex else [])),
        in_specs=[_rows(tm, AW), _rows(tm, AW), _rows(tm, AW), _rows(tm, CW), _rows(tm, CW), _rows(tm, LANES),
                  _const((D, NP)), _rows(tm, D), _rows(tm, D), _const((8, D)), _const((1, D)),
                  *(ex.in_specs if ex else [])],
        out_specs=(_rows(tm, D), _const((3, 8, D)), *(ex.out_specs if ex else [])),
        scratch_shapes=ex.scratch if ex else [],
        compiler_params=_cp(("arbitrary",)),
    )(dq, dk, dv, dalin, dagate, dfg, w_in_p, x, dx2, mod8, n1g, *(ex.srcs if ex else []))
    return outs[:2], list(outs[2:])


def _adam(w, g, m, v):
    m_new = B1 * m + (1.0 - B1) * g
    v_new = B2 * v + (1.0 - B2) * (g * g)
    m_hat = m_new / (1.0 - B1 ** STEP)
    v_hat = v_new / (1.0 - B2 ** STEP)
    delta = -LR * (m_hat / (jnp.sqrt(v_hat) + AEPS) + WD * w)
    return delta, m_new, v_new


def _pair_adamw(slots, w, m, v, name, tr=256):
    ns, R, C = slots.shape
    tr = tr if R % tr == 0 else R
    nt = R // tr

    def body(s_ref, w_ref, m_ref, v_ref, g_ref, d_ref, mo_ref, vo_ref, mine_ref, theirs_ref, send_sems, recv_sems):
        i = pl.program_id(0)
        part = s_ref[0].astype(F32)
        for k in range(1, ns):
            part = part + s_ref[k].astype(F32)
        mine_ref[i] = part
        swap = pltpu.make_async_remote_copy(
            src_ref=mine_ref.at[i], dst_ref=theirs_ref.at[i], send_sem=send_sems.at[i], recv_sem=recv_sems.at[i],
            device_id=(lax.axis_index("x"), lax.axis_index("y"), 1 - lax.axis_index("c")),
            device_id_type=pl.DeviceIdType.MESH)
        swap.start()
        swap.wait()
        g = part + theirs_ref[i]
        g_ref[...] = g
        d_ref[...], mo_ref[...], vo_ref[...] = _adam(w_ref[...], g, m_ref[...], v_ref[...])

    blk = pl.BlockSpec((tr, C), lambda i: (i, 0))
    return pl.pallas_call(
        body, name=name, grid=(nt,), out_shape=tuple(_sds((R, C), F32) for _ in range(4)),
        in_specs=[pl.BlockSpec((ns, tr, C), lambda i: (0, i, 0)), blk, blk, blk], out_specs=(blk, blk, blk, blk),
        scratch_shapes=[pltpu.VMEM((nt, tr, C), F32), pltpu.VMEM((nt, tr, C), F32),
                        pltpu.SemaphoreType.DMA((nt,)), pltpu.SemaphoreType.DMA((nt,))],
        compiler_params=_cp(("arbitrary",)),
    )(slots, w, m, v)


def _ada_adamw(sct, dmod, w, m, v):
    R, C = w.shape
    tr, bc = 256, 512

    def body(sct_ref, dm_ref, w_ref, m_ref, v_ref, g_ref, d_ref, mo_ref, vo_ref):
        g = sct_ref[:, 0:1] * dm_ref[0:1, :]
        for b in range(1, N_DEV):
            g = g + sct_ref[:, b:b + 1] * dm_ref[b:b + 1, :]
        g_ref[...] = g
        d_ref[...], mo_ref[...], vo_ref[...] = _adam(w_ref[...], g, m_ref[...], v_ref[...])

    blk = pl.BlockSpec((tr, bc), lambda i, j: (i, j))
    return pl.pallas_call(
        body, name="ada_adamw", grid=(R // tr, C // bc), out_shape=tuple(_sds((R, C), F32) for _ in range(4)),
        in_specs=[pl.BlockSpec((tr, N_DEV), lambda i, j: (i, 0)), pl.BlockSpec((N_DEV, bc), lambda i, j: (0, j)),
                  blk, blk, blk],
        out_specs=(blk, blk, blk, blk), compiler_params=_cp(("parallel", "parallel")),
    )(sct, dmod, w, m, v)


PACK = {"dmod": 0, "norm1_g": 6144, "norm2_g": 7168, "q_norm_g": 8192, "k_norm_g": 8704, "b_f": 9216, "conv_b": 9344,
        "conv_ln_g": 9856, "conv_ln_b": 10368, "beta_attn": 10880, "beta_conv": 11392, "loss": 11904}
SMALL_NAMES = ["b_ada", "norm1_g", "q_norm_g", "k_norm_g", "b_f", "conv_b", "conv_ln_g", "conv_ln_b", "beta_attn",
               "beta_conv", "norm2_g"]


def _pack_small(acc1, acc_d, acc_h, dg2, accg, accb, dcb, loss8):
    def body(a1_ref, ad_ref, ah_ref, dg2_ref, ag_ref, ab_ref, cb_ref, loss_ref, o_ref):
        def put(off, rows):
            o_ref[:, off:off + rows.shape[1]] = jnp.sum(rows, axis=0, keepdims=True)

        for t, rows in enumerate((a1_ref[0], a1_ref[1], ad_ref[3], ad_ref[0], ad_ref[1], dg2_ref[...])):
            put(PACK["dmod"] + t * D, rows)
        put(PACK["norm1_g"], a1_ref[2])
        put(PACK["norm2_g"], ad_ref[2])
        put(PACK["q_norm_g"], ag_ref[0])
        put(PACK["k_norm_g"], ag_ref[1])
        put(PACK["b_f"], ab_ref[...])
        put(PACK["conv_b"], cb_ref[...])
        put(PACK["conv_ln_g"], ah_ref[2])
        put(PACK["conv_ln_b"], ah_ref[3])
        put(PACK["beta_attn"], ah_ref[0])
        put(PACK["beta_conv"], ah_ref[1])
        o_ref[:, PACK["loss"]:PACK["loss"] + LANES] = loss_ref[0:1, :]

    vm = pl.BlockSpec(memory_space=pltpu.VMEM)
    return pl.pallas_call(body, name="pack_small", out_shape=_sds((1, SMALL_IN), F32), in_specs=[vm] * 8, out_specs=vm,
                          )(acc1, acc_d, acc_h, dg2, accg, accb, dcb, loss8)


def _small_adamw(slots, fold, ws, ms, vs):
    n = len(ws)
    widths = [w.shape[1] for w in ws]

    def body(s_ref, f_ref, *refs):
        w_refs, m_refs, v_refs = refs[:n], refs[n:2 * n], refs[2 * n:3 * n]
        outs = refs[3 * n:]
        tot = s_ref[0:1, :]
        for k in range(1, N_DEV):
            tot = tot + s_ref[k:k + 1, :]

        def grad(name, width):
            if name == "b_ada":
                return tot[:, 0:6 * D]
            seg = tot[:, PACK[name]:PACK[name] + max(width, LANES)]
            if name in ("q_norm_g", "k_norm_g"):
                seg = jnp.dot(jnp.broadcast_to(tot[:, PACK[name]:PACK[name] + AW], (8, AW)), f_ref[...], precision=HI,
                              preferred_element_type=F32)[0:1, :]
            return seg[:, 0:width]

        for t, (name, width) in enumerate(zip(SMALL_NAMES, widths)):
            g = grad(name, width)
            d, m_new, v_new = _adam(w_refs[t][...], g, m_refs[t][...], v_refs[t][...])
            outs[t][...] = g
            outs[n + t][...] = d
            outs[2 * n + t][...] = m_new
            outs[3 * n + t][...] = v_new
        outs[4 * n][...] = tot[:, PACK["loss"]:PACK["loss"] + LANES]

    vm = pl.BlockSpec(memory_space=pltpu.VMEM)
    outs = pl.pallas_call(
        body, name="adamw_small",
        out_shape=(*(_sds((1, w), F32) for _ in range(4) for w in widths), _sds((1, LANES), F32)),
        in_specs=[vm] * (2 + 3 * n), out_specs=tuple(vm for _ in range(4 * n + 1)),
    )(slots, fold, *ws, *ms, *vs)
    return [list(outs[k * n:(k + 1) * n]) for k in range(4)], outs[4 * n][0, 0]


def _perm_in(w):
    pad = jnp.zeros((w.shape[0], NP - 2568), w.dtype)
    return jnp.concatenate([w[:, :1536], w[:, 1544:2568], w[:, 1536:1544], pad], axis=1)


def _pad_lanes(vec, n=LANES):
    return jnp.pad(vec, ((0, 0), (0, n - vec.shape[1])))


def kernel(x, c, w_ada, b_ada, norm1_g, w_in, q_norm_g, k_norm_g, b_f, conv_w, conv_b, conv_ln_g, conv_ln_b, beta_attn, beta_conv, w_out, norm2_g, w_ff1, w_ff2, loss_target, m_w_ada, m_b_ada, m_norm1_g, m_w_in, m_q_norm_g, m_k_norm_g, m_b_f, m_conv_w, m_conv_b, m_conv_ln_g, m_conv_ln_b, m_beta_attn, m_beta_conv, m_w_out, m_norm2_g, m_w_ff1, m_w_ff2, v_w_ada, v_b_ada, v_norm1_g, v_w_in, v_q_norm_g, v_k_norm_g, v_b_f, v_conv_w, v_conv_b, v_conv_ln_g, v_conv_ln_b, v_beta_attn, v_beta_conv, v_w_out, v_norm2_g, v_w_ff1, v_w_ff2):
    S = x.shape[1]
    tm = min(256, S)
    tw = min(512, S)
    tq = min(512, S // 2)
    xs, tgt = x[0], loss_target[0]
    chip = 2 * lax.axis_index("x") + lax.axis_index("y")
    e512, et512 = _head_sum_mats()

    conv_w32 = jnp.pad(conv_w[0], ((0, 1), (0, 0)))
    c_all, g_in = _exchange([(c, "bcast8"), (w_in[0].astype(BF16), "chip4")], "gather_in")
    later_weights = _Exchange([(w_out[0].astype(BF16), "chip4"), (conv_w32, "chip4")])
    c_all = c_all.reshape(N_DEV, D)
    w_in_p = _perm_in(jnp.transpose(g_in, (1, 0, 2)).reshape(D, 2568))

    b_shard = lax.dynamic_slice(b_ada, (0, chip * 1536), (1, 1536))
    mod_rows, sc_all = _mod_shard(c_all, w_ada[0], b_shard)
    (mod_slots,) = _exchange([(mod_rows.reshape(N_DEV, 1, 1536), "all8")], "scatter_mod")
    mod = mod_slots.reshape(4, 2, 1536)[:, 0, :].reshape(6, D)
    mod8 = jnp.pad(mod, ((0, 2), (0, 0)))

    qg512 = jnp.tile(q_norm_g, (1, NH))
    kg512 = jnp.tile(k_norm_g, (1, NH))
    bf128 = _pad_lanes(b_f)

    (h1, qh, kh, vb, qn, kn, rq, rk, fgb, alin, agate, u0), (g_out, g_cw) = _fwd_in(
        xs, mod8, norm1_g, w_in_p, e512, et512, qg512, kg512, bf128, tw, ex=later_weights)
    w_out_f = g_out.reshape(D, D)
    cw32 = jnp.transpose(g_cw, (1, 0, 2)).reshape(HALO, CW)
    bd = _logit_bound(q_norm_g, k_norm_g)
    (qa, ka, va, fcum), (w1,) = _fwd_decay(fgb, qh, kh, vb, _shift(bd), tw,
                                           ex=_Exchange([(w_ff1[0].astype(BF16), "chip4")]))
    fs, fe = _skip_tables(fcum, tq)
    o_attn, lser = _attn_fwd(qa, ka, va, fs, fe, bd, tq)
    (u1, mc), (w2,) = _fwd_conv(u0, cw32, conv_b, conv_ln_g, conv_ln_b, beta_conv, tw,
                                ex=_Exchange([(w_ff2[0].astype(BF16), "chip4")]))
    merged, ob, x2, h2 = _fwd_out(o_attn, mc, xs, mod8, norm2_g, beta_attn, w_out_f, tw)
    r, dy, loss8, dg2 = _fwd_ffn(h2, w1, w2, x2, tgt, mod8, tw)

    df2, df1, dh2 = _bwd_ffn(dy, mod8, r, w1, w2, tw)
    gw_ff2 = _wgrad(r, df2, "wgrad_ff2", square_a=True)
    gw_ff1 = _wgrad(h2, df1, "wgrad_ff1", col_pieces=True)
    (dx2, do, doa, du1, acc_d, acc_h, dr), (p_ff1,) = _bwd_mid(
        dh2, dy, x2, ob, o_attn, u1, mod8, norm2_g, beta_attn, beta_conv, conv_ln_g, conv_ln_b, w_out_f, e512, tm, tq,
        ex=_Exchange([(gw_ff1, "chip4p")]))
    gw_out = _wgrad(merged, do, "wgrad_out")
    dqa, dka, dv = _attn_bwd(qa, ka, vb, doa, lser, dr, fs, fe, bd, tq)
    (dalin, dagate, dcw, dcb), (p_out, p_ff2) = _bwd_conv(
        du1, u0, alin, agate, cw32, tw,
        ex=_Exchange([(gw_out.reshape(4, 256, D), "chip4p"), (gw_ff2.reshape(4, D, D), "chip4p")]))
    dq, dk, dfg, accg, accb = _bwd_qk(dqa, dka, qn, kn, rq, rk, fgb, qg512, kg512, e512, et512, tw)
    gw_in_p = _wgrad_in(h1, [dq, dk, dv, dalin, dagate, dfg])
    gw_in = jnp.concatenate([gw_in_p[:, :1536], gw_in_p[:, 2560:2568], gw_in_p[:, 1536:2560]], axis=1)
    s8 = lambda a: jnp.sum(a, axis=-2)
    gcw = s8(dcw)
    in_grads = _Exchange([(jnp.transpose(gw_in.reshape(D, 4, 642), (1, 0, 2)), "chip4p"),
                          (jnp.transpose(gcw.reshape(HALO, 4, LANES), (1, 0, 2)), "chip4p")])
    (grad_x, acc1), (p_in, p_cw) = _bwd_in(dq, dk, dv, dalin, dagate, dfg, w_in_p, xs, dx2, mod8, norm1_g, tw,
                                           ex=in_grads)

    small = _pack_small(acc1, acc_d, acc_h, dg2, accg, accb, dcb, loss8)
    (small_s,) = _exchange([(small, "bcast8")], "gather_small")
    small_s = small_s.reshape(N_DEV, SMALL_IN)

    g_in_, d_in, nm_in, nv_in = _pair_adamw(p_in, w_in[0], m_w_in[0], v_w_in[0], "adamw_in")
    g_out_, d_out, nm_out, nv_out = _pair_adamw(p_out, w_out[0], m_w_out[0], v_w_out[0], "adamw_out")
    g_f1, d_f1, nm_f1, nv_f1 = _pair_adamw(p_ff1, w_ff1[0], m_w_ff1[0], v_w_ff1[0], "adamw_ff1")
    g_f2, d_f2, nm_f2, nv_f2 = _pair_adamw(p_ff2, w_ff2[0], m_w_ff2[0], v_w_ff2[0], "adamw_ff2")
    pad_row = lambda a, fill: jnp.pad(a[0], ((0, 1), (0, 0)), constant_values=fill)
    g_cw_, d_cw, nm_cw, nv_cw = (a[:KC] for a in _pair_adamw(
        p_cw, pad_row(conv_w, 0.0), pad_row(m_conv_w, 0.0), pad_row(v_conv_w, 1.0), "adamw_conv_w"))
    dmod_shard = lax.dynamic_slice(small_s[:, :6 * D], (0, chip * 1536), (N_DEV, 1536))
    g_ada, d_ada, nm_ada, nv_ada = _ada_adamw(sc_all.T, dmod_shard, w_ada[0], m_w_ada[0], v_w_ada[0])

    fold = np.zeros((AW, LANES), np.float32)
    fold[np.arange(AW), np.arange(AW) % DH] = 1.0
    smalls = [b_ada, norm1_g, q_norm_g, k_norm_g, b_f, conv_b, conv_ln_g, conv_ln_b, beta_attn, beta_conv, norm2_g]
    m_smalls = [m_b_ada, m_norm1_g, m_q_norm_g, m_k_norm_g, m_b_f, m_conv_b, m_conv_ln_g, m_conv_ln_b, m_beta_attn,
                m_beta_conv, m_norm2_g]
    v_smalls = [v_b_ada, v_norm1_g, v_q_norm_g, v_k_norm_g, v_b_f, v_conv_b, v_conv_ln_g, v_conv_ln_b, v_beta_attn,
                v_beta_conv, v_norm2_g]
    (gs, ds, ms, vs), loss = _small_adamw(small_s, jnp.asarray(fold), smalls, m_smalls, v_smalls)

    big ={"w_ada": (g_ada, d_ada, nm_ada, nv_ada), "w_in": (g_in_, d_in, nm_in, nv_in),
           "conv_w": (g_cw_, d_cw, nm_cw, nv_cw), "w_out": (g_out_, d_out, nm_out, nv_out),
           "w_ff1": (g_f1, d_f1, nm_f1, nv_f1), "w_ff2": (g_f2, d_f2, nm_f2, nv_f2)}
    order =["w_ada", "b_ada", "norm1_g", "w_in", "q_norm_g", "k_norm_g", "b_f", "conv_w", "conv_b", "conv_ln_g",
             "conv_ln_b", "beta_attn", "beta_conv", "w_out", "norm2_g", "w_ff1", "w_ff2"]

    def leaf(name, which):
        if name in big:
            return big[name][which][None]
        return (gs, ds, ms, vs)[which][SMALL_NAMES.index(name)]

    return (loss, grad_x[None], *[leaf(n, 0) for n in order], *[leaf(n, 1) for n in order],
            *[leaf(n, 2) for n in order], *[leaf(n, 3) for n in order])
```

```python
import functools

import numpy as np
import jax
import jax.numpy as jnp
from jax import lax
from jax.experimental import pallas as pl
from jax.experimental.pallas import tpu as pltpu

F32, BF16 = jnp.float32, jnp.bfloat16
HI = lax.Precision.HIGHEST
D = 1024
AW = 512
CW = 512
NH = 8
DH = 64
KC = 31
DFF = 4096
NP = 2688
EPS = 1e-6
NEG = -1e30
LANES = 128
VMEM_LIMIT = 56 * 2**20
NT = (((1,), (1,)), ((), ()))
TN = (((0,), (0,)), ((), ()))
LR, B1, B2, AEPS, WD, STEP = 0.001, 0.9, 0.999, 1e-08, 0.01, 10
N_DEV = 8
SMALL_IN = 12032


def _cp(sem=None, vmem=VMEM_LIMIT):
    kw = dict(vmem_limit_bytes=vmem)
    if sem is not None:
        kw["dimension_semantics"] = sem
    return pltpu.CompilerParams(**kw)


def _rows(tm, n):
    return pl.BlockSpec((tm, n), lambda i: (i, 0))


def _const(shape):
    nd = len(shape)
    return pl.BlockSpec(shape, lambda *_: (0,) * nd)


def _sds(shape, dt):
    return jax.ShapeDtypeStruct(shape, dt)


def _lane(shape):
    return lax.broadcasted_iota(jnp.int32, shape, len(shape) - 1)


def _sigmoid(x):
    return 1.0 / (1.0 + jnp.exp(-x))


class _Exchange:
    MASKS = {"chip4": (2, 4, 6), "chip4p": (2, 4, 6), "all8": (1, 2, 3, 4, 5, 6, 7), "bcast8": (1, 2, 3, 4, 5, 6, 7)}

    def __init__(self, items):
        self.srcs = [s for s, _ in items]
        self.kinds = [k for _, k in items]
        self.n = len(items)
        self.out_shapes = []
        for s, k in items:
            shape = {"all8": (N_DEV,) + s.shape[1:], "bcast8": (N_DEV,) + s.shape, "chip4": (4,) + s.shape,
                     "chip4p": (4,) + s.shape[1:]}[k]
            self.out_shapes.append(_sds(shape, s.dtype))
        self.sem_index = {}
        for t, k in enumerate(self.kinds):
            for m in self.MASKS[k]:
                self.sem_index[(t, m)] = len(self.sem_index)
        n_sem = len(self.sem_index)
        self.scratch = [pltpu.SemaphoreType.DMA((n_sem,)), pltpu.SemaphoreType.DMA((n_sem,)),
                        pltpu.SemaphoreType.DMA((self.n,))]
        self.in_specs = [pl.BlockSpec(memory_space=pl.ANY)] * self.n
        self.out_specs = [pl.BlockSpec(memory_space=pl.ANY)] * self.n

    def copies(self, src_refs, dst_refs, send_sems, recv_sems, local_sems):
        x, y, c = lax.axis_index("x"), lax.axis_index("y"), lax.axis_index("c")
        my_id = 4 * x + 2 * y + c
        my_chip = 2 * x + y

        def piece(t, dev_id, chip):
            k = self.kinds[t]
            return src_refs[t].at[dev_id] if k == "all8" else src_refs[t].at[chip] if k == "chip4p" else src_refs[t]

        out = []
        for t in range(self.n):
            slot = dst_refs[t].at[my_chip if self.kinds[t] in ("chip4", "chip4p") else my_id]
            out.append(pltpu.make_async_copy(piece(t, my_id, my_chip), slot, local_sems.at[t]))
            for m in self.MASKS[self.kinds[t]]:
                px = 1 - x if m & 4 else x
                py = 1 - y if m & 2 else y
                pc = 1 - c if m & 1 else c
                s = self.sem_index[(t, m)]
                out.append(pltpu.make_async_remote_copy(
                    src_ref=piece(t, 4 * px + 2 * py + pc, 2 * px + py), dst_ref=slot,
                    send_sem=send_sems.at[s], recv_sem=recv_sems.at[s],
                    device_id=(px, py, pc), device_id_type=pl.DeviceIdType.MESH))
        return out


def _hosted(body, ex, n_in, n_out, n_scr, n_steps):
    if ex is None:
        return body

    def wrapped(*refs):
        ins, xin = refs[:n_in], refs[n_in:n_in + ex.n]
        o0 = n_in + ex.n
        outs, xout = refs[o0:o0 + n_out], refs[o0 + n_out:o0 + n_out + ex.n]
        s0 = o0 + n_out + ex.n
        scr, sems = refs[s0:s0 + n_scr], refs[s0 + n_scr:]

        @pl.when(pl.program_id(0) == 0)
        def _():
            for cp in ex.copies(xin, xout, *sems):
                cp.start()

        body(*ins, *outs, *scr)

        @pl.when(pl.program_id(0) == n_steps - 1)
        def _():
            for cp in ex.copies(xin, xout, *sems):
                cp.wait()

    return wrapped


def _exchange(items, name):
    ex = _Exchange(items)
    n = ex.n

    def body(*refs):
        copies = ex.copies(refs[:n], refs[n:2 * n], *refs[2 * n:])
        for cp in copies:
            cp.start()
        for cp in copies:
            cp.wait()

    outs = pl.pallas_call(
        body, name=name, out_shape=tuple(ex.out_shapes), in_specs=ex.in_specs, out_specs=tuple(ex.out_specs),
        scratch_shapes=ex.scratch,
    )(*ex.srcs)
    return list(outs)


def _mod_shard(c_all, w_ada, b_shard):
    n = w_ada.shape[1]

    def body(c_ref, w_ref, b_ref, o_ref, sc_ref):
        cv = c_ref[...]
        sc = cv * _sigmoid(cv)
        sc_ref[...] = sc
        o_ref[...] = jnp.dot(sc, w_ref[...], precision=HI, preferred_element_type=F32) + b_ref[...]

    bn = 512
    return pl.pallas_call(
        body, name="mod_shard", out_shape=(_sds((N_DEV, n), F32), _sds((N_DEV, D), F32)), grid=(n // bn,),
        in_specs=[_const((N_DEV, D)), pl.BlockSpec((D, bn), lambda j: (0, j)), pl.BlockSpec((1, bn), lambda j: (0, j))],
        out_specs=(pl.BlockSpec((N_DEV, bn), lambda j: (0, j)), _const((N_DEV, D))),
        compiler_params=_cp(("arbitrary",)),
    )(c_all, w_ada, b_shard)


def _head_sum_mats():
    e = np.zeros((AW, LANES), np.float32)
    for h in range(NH):
        e[h * DH:(h + 1) * DH, h] = 1.0
    return jnp.asarray(e, BF16), jnp.asarray(e.T.copy(), BF16)


def _dot2(x, w):
    hi = x.astype(BF16)
    lo = (x - hi.astype(F32)).astype(BF16)
    return jnp.dot(hi, w, preferred_element_type=F32) + jnp.dot(lo, w, preferred_element_type=F32)


def _fwd_in(x, mod8, n1g, w_in_p, e512, et512, qg512, kg512, bf128, tm, ex=None):
    S = x.shape[0]

    def body(x_ref, mod_ref, n1g_ref, w_ref, e_ref, et_ref, qg_ref, kg_ref, bf_ref,
             h1_ref, qh_ref, kh_ref, v_ref, qn_ref, kn_ref, rq_ref, rk_ref, fgb_ref, alin_ref, agate_ref, u0_ref):
        xv = x_ref[...]
        r1 = lax.rsqrt(jnp.mean(xv * xv, axis=-1, keepdims=True) + EPS)
        h = (xv * r1) * (n1g_ref[...] * (1.0 + mod_ref[1:2, :])) + mod_ref[0:1, :]
        hb = h.astype(BF16)
        h1_ref[...] = hb

        def seg(a, b):
            return jnp.dot(hb, w_ref[:, a:b], preferred_element_type=F32)

        def headnorm(t, g_ref, scale, n_ref, r_ref, o_ref):
            ss = _dot2(t * t, e_ref[...])
            r = lax.rsqrt(ss * (1.0 / DH) + EPS)
            tn = t * _dot2(r, et_ref[...])
            n_ref[...] = tn.astype(BF16)
            r_ref[...] = r
            o_ref[...] = (tn * (g_ref[...] * scale)).astype(BF16)

        headnorm(seg(0, 512), qg_ref, DH ** -0.5, qn_ref, rq_ref, qh_ref)
        headnorm(seg(512, 1024), kg_ref, 1.0, kn_ref, rk_ref, kh_ref)
        v_ref[...] = seg(1024, 1536).astype(BF16)
        alin = seg(1536, 2048)
        agate = seg(2048, 2560)
        alin_ref[...] = alin.astype(BF16)
        agate_ref[...] = agate.astype(BF16)
        u0_ref[...] = alin * _sigmoid(agate)
        fgb_ref[...] = seg(2560, NP) + bf_ref[...]

    bf = lambda: _sds((S, AW), BF16)
    xs = ex.srcs if ex else []
    outs = pl.pallas_call(
        _hosted(body, ex, 9, 12, 0, S // tm), name="fwd_in", grid=(S // tm,),
        out_shape=(_sds((S, D), BF16), bf(), bf(), bf(), bf(), bf(), _sds((S, LANES), F32), _sds((S, LANES), F32),
                   _sds((S, LANES), F32), bf(), bf(), _sds((S, CW), F32), *(ex.out_shapes if ex else [])),
        in_specs=[_rows(tm, D), _const((8, D)), _const((1, D)), _const((D, NP)), _const((AW, LANES)), _const((LANES, AW)),
                  _const((1, AW)), _const((1, AW)), _const((1, LANES)), *(ex.in_specs if ex else [])],
        out_specs=(_rows(tm, D), _rows(tm, AW), _rows(tm, AW), _rows(tm, AW), _rows(tm, AW), _rows(tm, AW),
                   _rows(tm, LANES), _rows(tm, LANES), _rows(tm, LANES), _rows(tm, AW), _rows(tm, AW), _rows(tm, CW),
                   *(ex.out_specs if ex else [])),
        scratch_shapes=ex.scratch if ex else [],
        compiler_params=_cp(("arbitrary",)),
    )(x, mod8, n1g, w_in_p, e512, et512, qg512, kg512, bf128, *xs)
    return outs[:12], list(outs[12:])


def _split3(f):
    f1 = f.astype(BF16).astype(F32)
    f2 = (f - f1).astype(BF16).astype(F32)
    return f1, f2, f - f1 - f2


def _dot3(w, x):
    return sum(jnp.dot(w, piece.astype(BF16), preferred_element_type=F32) for piece in _split3(x))


def _fwd_decay(fgb, qh, kh, vb, shift, tm, ex=None):
    S = fgb.shape[0]

    def body(shift_ref, fgb_ref, qh_ref, kh_ref, vb_ref, qa_ref, ka_ref, va_ref, f_ref, carry_ref):
        @pl.when(pl.program_id(0) == 0)
        def _():
            carry_ref[...] = jnp.zeros_like(carry_ref)

        fb = fgb_ref[...]
        lf = jnp.minimum(fb, 0.0) - jnp.log1p(jnp.exp(-jnp.abs(fb)))
        tri = (lax.broadcasted_iota(jnp.int32, (tm, tm), 0) >= lax.broadcasted_iota(jnp.int32, (tm, tm), 1)
               ).astype(F32).astype(BF16)
        cs = _dot3(tri, lf) + carry_ref[0:1, :]
        f_ref[...] = cs
        carry_ref[...] = jnp.broadcast_to(cs[tm - 1:tm, :], carry_ref.shape)
        lane = _lane((tm, LANES))
        s1, s2, s3 = _split3(jnp.zeros((tm, LANES), F32) - shift_ref[0, 0])
        tail_q = jnp.where((lane >= 67) & (lane < 70), 1.0,
                           jnp.where(lane == 70, s1, jnp.where(lane == 71, s2, jnp.where(lane == 72, s3, 0.0))))
        tail_k = jnp.where(((lane >= 64) & (lane < 67)) | ((lane >= 70) & (lane < 73)), 1.0, 0.0)
        tail_v = jnp.where(lane == DH, 1.0, 0.0)
        for p in range(NH // 2):
            qp = qh_ref[:, p * LANES:(p + 1) * LANES].astype(F32)
            kp = kh_ref[:, p * LANES:(p + 1) * LANES].astype(F32)
            vp = vb_ref[:, p * LANES:(p + 1) * LANES].astype(F32)
            for hh in range(2):
                h = 2 * p + hh
                f1, f2, f3 = _split3(cs[:, h:h + 1])
                qb = qp if hh == 0 else pltpu.roll(qp, 64, 1)
                kb = kp if hh == 0 else pltpu.roll(kp, 64, 1)
                vh = vp if hh == 0 else pltpu.roll(vp, 64, 1)
                augq = jnp.where(lane == 64, f1, jnp.where(lane == 65, f2, jnp.where(lane == 66, f3, tail_q)))
                augk = jnp.where(lane == 67, -f1, jnp.where(lane == 68, -f2, jnp.where(lane == 69, -f3, tail_k)))
                qa_ref[h] = jnp.where(lane < DH, qb, augq).astype(BF16)
                ka_ref[h] = jnp.where(lane < DH, kb, augk).astype(BF16)
                va_ref[h] = jnp.where(lane < DH, vh, tail_v).astype(BF16)

    hm = pl.BlockSpec((NH, tm, LANES), lambda i: (0, i, 0))
    hms = _sds((NH, S, LANES), BF16)
    outs = pl.pallas_call(
        _hosted(body, ex, 5, 4, 1, S // tm), name="fwd_decay", grid=(S // tm,),
        out_shape=(hms, hms, hms, _sds((S, LANES), F32), *(ex.out_shapes if ex else [])),
        in_specs=[SMEM_SPEC, _rows(tm, LANES), _rows(tm, AW), _rows(tm, AW), _rows(tm, AW),
                  *(ex.in_specs if ex else [])],
        out_specs=(hm, hm, hm, _rows(tm, LANES), *(ex.out_specs if ex else [])),
        scratch_shapes=[pltpu.VMEM((8, LANES), F32), *(ex.scratch if ex else [])],
        compiler_params=_cp(("arbitrary",)),
    )(shift, fgb, qh, kh, vb, *(ex.srcs if ex else []))
    return outs[:4], list(outs[4:])


SKIP = -106.0
SKIP_P = -88.0


def _block_loops(first, step, needed, run):
    def both(j):
        return jnp.logical_and(needed(0, j), needed(1, j))

    def walk(heads):
        def go(j):
            run(j, heads)
            return j + step
        return go

    j = lax.while_loop(both, walk((0, 1)), first)
    lax.while_loop(functools.partial(needed, 0), walk((0,)), j)
    lax.while_loop(functools.partial(needed, 1), walk((1,)), j)


def _logit_bound(qg, kg):
    return (2.0 * 1.03 * DH ** 0.5 * jnp.max(jnp.abs(qg)) * jnp.max(jnp.abs(kg))).reshape(1, 1)


def _skip_tables(f, tq):
    return f[0::tq, :NH].T, f[tq - 1::tq, :NH].T


SMEM_SPEC = pl.BlockSpec(memory_space=pltpu.SMEM)


def _causal_rect(rows, cols, col0):
    return (lax.broadcasted_iota(jnp.int32, (rows, cols), 0)
            >= lax.broadcasted_iota(jnp.int32, (rows, cols), 1) + col0)


SHIFT_MAX = 60.0


def _shift(bd):
    return jnp.where(bd <= SHIFT_MAX, 0.5 * bd, 0.0)


def _attn_fwd(qa, ka, va, fs, fe, bd, tq):
    S = qa.shape[1]
    nq = S // tq

    def body(fs_ref, fe_ref, bd_ref, qa_ref, ka_ref, va_ref, o_ref, lser_ref, m_ref, acc_ref):
        pr, i = pl.program_id(0), pl.program_id(1)
        sel_a = _lane((tq, LANES)) < DH
        acc_ref[...] = jnp.zeros_like(acc_ref)

        def logits(j, hh, masked):
            start = pl.multiple_of(j * tq, tq)
            s = lax.dot_general(qa_ref[hh], ka_ref[hh, pl.ds(start, tq), :], NT, preferred_element_type=F32)
            if masked:
                s = jnp.where(_causal_rect(tq, tq, 0), s, NEG)
            return s, va_ref[hh, pl.ds(start, tq), :]

        def shifted_step(j, heads, masked=False):
            for hh in heads:
                s, vb = logits(j, hh, masked)
                acc_ref[hh] += jnp.dot(jnp.exp(s).astype(BF16), vb, preferred_element_type=F32)

        def online_step(j, heads, masked=False):
            for hh in heads:
                s, vb = logits(j, hh, masked)
                m_prev = m_ref[hh]
                m_new = jnp.maximum(m_prev, jnp.max(s, axis=1, keepdims=True))
                p = jnp.exp(s - jnp.tile(m_new, (1, tq // LANES)))
                m_ref[hh] = m_new
                acc_ref[hh] = jnp.exp(m_prev - m_new) * acc_ref[hh] + jnp.dot(p.astype(BF16), vb,
                                                                              preferred_element_type=F32)

        def needed(slack, hh, j):
            top = fs_ref[2 * pr + hh, i] + slack
            return jnp.logical_and(j >= 0, top - fe_ref[2 * pr + hh, jnp.maximum(j, 0)] >= SKIP)

        def shifted_diagonal():
            h = tq // 2
            base = pl.multiple_of(i * tq, tq)
            for hh in (0, 1):
                s0 = lax.dot_general(qa_ref[hh], ka_ref[hh, pl.ds(base, h), :], NT, preferred_element_type=F32)
                s0 = jnp.where(_causal_rect(tq, h, 0), s0, NEG)
                acc_ref[hh] += jnp.dot(jnp.exp(s0).astype(BF16), va_ref[hh, pl.ds(base, h), :],
                                       preferred_element_type=F32)
                newer = pl.multiple_of(base + h, h)
                s1 = lax.dot_general(qa_ref[hh, h:tq, :], ka_ref[hh, pl.ds(newer, h), :], NT,
                                     preferred_element_type=F32)
                s1 = jnp.where(_causal_rect(h, h, 0), s1, NEG)
                acc_ref[hh, h:tq, :] += jnp.dot(jnp.exp(s1).astype(BF16), va_ref[hh, pl.ds(newer, h), :],
                                                preferred_element_type=F32)

        @pl.when(bd_ref[0, 0] <= SHIFT_MAX)
        def _():
            m_ref[...] = jnp.zeros_like(m_ref)
            shifted_diagonal()
            _block_loops(i - 1, -1, functools.partial(needed, 0.0), shifted_step)

        @pl.when(bd_ref[0, 0] > SHIFT_MAX)
        def _():
            m_ref[...] = jnp.full(m_ref.shape, NEG, F32)
            online_step(i, (0, 1), masked=True)
            _block_loops(i - 1, -1, functools.partial(needed, bd_ref[0, 0]), online_step)

        outs, lses = [], []
        for hh in range(2):
            acc = acc_ref[hh]
            row_sum = jnp.broadcast_to(acc[:, DH:DH + 1], (tq, LANES))
            outs.append(acc / row_sum)
            lses.append(m_ref[hh] + jnp.log(row_sum))
        o_ref[...] = jnp.where(sel_a, outs[0], pltpu.roll(outs[1], 64, 1))
        row = lax.broadcasted_iota(jnp.int32, (8, tq), 0)
        lser_ref[0, 0] = jnp.where(row == 0, lses[0].T[0:8, :], lses[1].T[0:8, :])

    return pl.pallas_call(
        body, name="attn_fwd", grid=(NH // 2, nq),
        out_shape=(_sds((S, AW), F32), _sds((NH // 2, nq, 8, tq), F32)),
        in_specs=[SMEM_SPEC, SMEM_SPEC, SMEM_SPEC,
                  pl.BlockSpec((2, tq, LANES), lambda p, i: (p, i, 0)),
                  pl.BlockSpec((2, S, LANES), lambda p, i: (p, 0, 0)),
                  pl.BlockSpec((2, S, LANES), lambda p, i: (p, 0, 0))],
        out_specs=(pl.BlockSpec((tq, LANES), lambda p, i: (i, p)),
                   pl.BlockSpec((1, 1, 8, tq), lambda p, i: (p, i, 0, 0))),
        scratch_shapes=[pltpu.VMEM((2, tq, LANES), F32), pltpu.VMEM((2, tq, LANES), F32)],
        compiler_params=_cp(("parallel", "parallel")),
    )(fs, fe, bd, qa, ka, va)


HALO = 32
FWD_CHUNK = 64
BWD_CHUNK = 32


def _halo_prev(tm):
    return pl.BlockSpec((HALO, CW), lambda i: (jnp.maximum(i * (tm // HALO) - 1, 0), 0))


SUBLANES = 8
SHIFT_ROWS = 24


def _shifted_copies(ext_ref, sh_ref, tm):
    for k in range(1, SUBLANES):
        sh_ref[k - 1, 0:tm + SHIFT_ROWS, :] = ext_ref[k:k + tm + SHIFT_ROWS, :]


def _tap_windows(ext_ref, sh_ref, r0, offset, rows, shared):
    for k in range(SUBLANES):
        taps = sorted((offset(j), j) for j in range(KC) if offset(j) % SUBLANES == k)
        for group in ([taps] if shared and taps else [[t] for t in taps]):
            lo, hi = r0 + group[0][0] - k, r0 + group[-1][0] - k + rows
            window = ext_ref[lo:hi, :] if k == 0 else sh_ref[k - 1, lo:hi, :]
            for off, j in group:
                at = r0 + off - k - lo
                yield j, window[at:at + rows, :]


def _fwd_conv(u0, w32, cb, lng, lnb, beta_c, tm, ex=None):
    S = u0.shape[0]

    def body(cur_ref, prev_ref, w_ref, cb_ref, lng_ref, lnb_ref, beta_ref, u1_ref, mc_ref, ext_ref, sh_ref):
        i = pl.program_id(0)
        ext_ref[0:HALO, :] = jnp.where(i == 0, 0.0, prev_ref[...])
        ext_ref[HALO:, :] = cur_ref[...]
        _shifted_copies(ext_ref, sh_ref, tm)
        for r0 in range(0, tm, FWD_CHUNK):
            acc = jnp.zeros((FWD_CHUNK, CW), F32) + cb_ref[...]
            for j, rows in _tap_windows(ext_ref, sh_ref, r0, lambda j: 2 + j, FWD_CHUNK, shared=False):
                acc = acc + w_ref[j:j + 1, :] * rows
            u1_ref[r0:r0 + FWD_CHUNK, :] = acc
        u1 = u1_ref[...]
        mu = jnp.mean(u1, axis=-1, keepdims=True)
        d = u1 - mu
        rstd = lax.rsqrt(jnp.mean(d * d, axis=-1, keepdims=True) + EPS)
        u2 = d * rstd * lng_ref[...] + lnb_ref[...]
        u3 = u2 * _sigmoid(u2)
        rc = lax.rsqrt(jnp.mean(u3 * u3, axis=-1, keepdims=True) + EPS)
        mc_ref[...] = (u3 * rc * beta_ref[...]).astype(BF16)

    outs = pl.pallas_call(
        _hosted(body, ex, 7, 2, 2, S // tm), name="fwd_conv", grid=(S // tm,),
        out_shape=(_sds((S, CW), F32), _sds((S, CW), BF16), *(ex.out_shapes if ex else [])),
        in_specs=[_rows(tm, CW), _halo_prev(tm), _const((HALO, CW)), _const((1, CW)), _const((1, CW)), _const((1, CW)),
                  _const((1, CW)), *(ex.in_specs if ex else [])],
        out_specs=(_rows(tm, CW), _rows(tm, CW), *(ex.out_specs if ex else [])),
        scratch_shapes=[pltpu.VMEM((tm + HALO, CW), F32), pltpu.VMEM((SUBLANES - 1, tm + HALO, CW), F32),
                        *(ex.scratch if ex else [])],
        compiler_params=_cp(("arbitrary",)),
    )(u0, u0, w32, cb, lng, lnb, beta_c, *(ex.srcs if ex else []))
    return outs[:2], list(outs[2:])


def _fwd_out(o_attn, mc, x, mod8, n2g, beta_a, w_out, tm):
    S = x.shape[0]

    def body(o_ref, mc_ref, x_ref, mod_ref, n2g_ref, beta_ref, w_ref, mg_ref, ob_ref, x2_ref, h2_ref):
        ov = o_ref[...]
        ra = lax.rsqrt(jnp.mean(ov * ov, axis=-1, keepdims=True) + EPS)
        ma = (ov * ra * beta_ref[...]).astype(BF16)
        mcv = mc_ref[...]
        mg_ref[:, 0:AW] = ma
        mg_ref[:, AW:D] = mcv
        o = (jnp.dot(ma, w_ref[0:AW, :], preferred_element_type=F32)
             + jnp.dot(mcv, w_ref[AW:D, :], preferred_element_type=F32))
        ob_ref[...] = o.astype(BF16)
        x2 = x_ref[...] + mod_ref[2:3, :] * o
        x2_ref[...] = x2
        r2 = lax.rsqrt(jnp.mean(x2 * x2, axis=-1, keepdims=True) + EPS)
        h2_ref[...] = ((x2 * r2) * (n2g_ref[...] * (1.0 + mod_ref[4:5, :])) + mod_ref[3:4, :]).astype(BF16)

    return pl.pallas_call(
        body, name="fwd_out", grid=(S // tm,),
        out_shape=(_sds((S, D), BF16), _sds((S, D), BF16), _sds((S, D), F32), _sds((S, D), BF16)),
        in_specs=[_rows(tm, AW), _rows(tm, CW), _rows(tm, D), _const((8, D)), _const((1, D)), _const((1, AW)),
                  _const((D, D))],
        out_specs=(_rows(tm, D), _rows(tm, D), _rows(tm, D), _rows(tm, D)),
        compiler_params=_cp(("parallel",)),
    )(o_attn, mc, x, mod8, n2g, beta_a, w_out)


def _fwd_ffn(h2, w1, w2, x2, tgt, mod8, tm):
    S = h2.shape[0]
    nk = w1.shape[0]
    bf = w1.shape[2]

    def body(h2_ref, w1_ref, w2_ref, x2_ref, tgt_ref, mod_ref, r_ref, dy_ref, loss_ref, dg2_ref):
        @pl.when(pl.program_id(0) == 0)
        def _():
            loss_ref[...] = jnp.zeros_like(loss_ref)
            dg2_ref[...] = jnp.zeros_like(dg2_ref)

        f2 = None
        for k in range(nk):
            r = jnp.maximum(jnp.dot(h2_ref[...], w1_ref[k], preferred_element_type=F32), 0.0)
            r_ref[:, k * bf:(k + 1) * bf] = r.astype(BF16)
            part = jnp.dot((r * r).astype(BF16), w2_ref[k], preferred_element_type=F32)
            f2 = part if f2 is None else f2 + part
        e = x2_ref[...] + mod_ref[5:6, :] * f2 - tgt_ref[...]
        dy = e * (1.0 / D)
        dy_ref[...] = dy
        loss_ref[...] += 0.5 * jnp.sum(jnp.sum(e * dy, axis=1, keepdims=True), axis=0, keepdims=True)
        dg2_ref[...] += jnp.sum((dy * f2).reshape(tm // 8, 8, D), axis=0)

    once = pl.Buffered(1)
    return pl.pallas_call(
        body, name="fwd_ffn", grid=(S // tm,),
        out_shape=(_sds((S, DFF), BF16), _sds((S, D), F32), _sds((8, LANES), F32), _sds((8, D), F32)),
        in_specs=[_rows(tm, D), pl.BlockSpec((nk, D, bf), lambda i: (0, 0, 0), pipeline_mode=once),
                  pl.BlockSpec((nk, bf, D), lambda i: (0, 0, 0), pipeline_mode=once), _rows(tm, D), _rows(tm, D),
                  _const((8, D))],
        out_specs=(_rows(tm, DFF), _rows(tm, D), _const((8, LANES)), _const((8, D))),
        compiler_params=_cp(("arbitrary",)),
    )(h2, w1, w2, x2, tgt, mod8)


def _bwd_ffn(dy, mod8, r, w1, w2, tm):
    S = dy.shape[0]
    nk = w1.shape[0]
    bf = w1.shape[2]

    def body(dy_ref, mod_ref, r_ref, w1_ref, w2_ref, df2_ref, df1_ref, dh2_ref):
        df2 = (dy_ref[...] * mod_ref[5:6, :]).astype(BF16)
        df2_ref[...] = df2
        dh2 = None
        for k in range(nk):
            da = lax.dot_general(df2, w2_ref[k], NT, preferred_element_type=F32)
            df1 = (da * (2.0 * r_ref[:, k * bf:(k + 1) * bf].astype(F32))).astype(BF16)
            df1_ref[:, k * bf:(k + 1) * bf] = df1
            part = lax.dot_general(df1, w1_ref[k], NT, preferred_element_type=F32)
            dh2 = part if dh2 is None else dh2 + part
        dh2_ref[...] = dh2

    once = pl.Buffered(1)
    return pl.pallas_call(
        body, name="bwd_ffn", grid=(S // tm,),
        out_shape=(_sds((S, D), BF16), _sds((S, DFF), BF16), _sds((S, D), F32)),
        in_specs=[_rows(tm, D), _const((8, D)), _rows(tm, DFF),
                  pl.BlockSpec((nk, D, bf), lambda i: (0, 0, 0), pipeline_mode=once),
                  pl.BlockSpec((nk, bf, D), lambda i: (0, 0, 0), pipeline_mode=once)],
        out_specs=(_rows(tm, D), _rows(tm, DFF), _rows(tm, D)),
        compiler_params=_cp(("parallel",)),
    )(dy, mod8, r, w1, w2)


def _token_tile(S, want):
    while S % want:
        want //= 2
    return want


def _wgrad(a, b, name, square_a=False, col_pieces=False, tk=2048, bm=1024, bn=1024):
    S, M = a.shape
    N = b.shape[1]
    bm, bn, tk = min(bm, M), min(bn, N), _token_tile(S, tk)
    nk = S // tk

    def body(a_ref, b_ref, o_ref, acc_ref):
        av = a_ref[...]
        if square_a:
            af = av.astype(F32)
            av = (af * af).astype(BF16)
        part = lax.dot_general(av, b_ref[...], TN, preferred_element_type=F32)

        @pl.when(pl.program_id(2) == 0)
        def _():
            acc_ref[...] = part

        @pl.when(pl.program_id(2) > 0)
        def _():
            acc_ref[...] += part

        @pl.when(pl.program_id(2) == nk - 1)
        def _():
            if col_pieces:
                o_ref[0] = acc_ref[...].astype(BF16)
            else:
                o_ref[...] = acc_ref[...].astype(BF16)

    if col_pieces:
        out_shape, out_spec = _sds((N // bn, M, bn), BF16), pl.BlockSpec((1, bm, bn), lambda mi, ni, k: (ni, mi, 0))
    else:
        out_shape, out_spec = _sds((M, N), BF16), pl.BlockSpec((bm, bn), lambda mi, ni, k: (mi, ni))
    return pl.pallas_call(
        body, name=name, grid=(M // bm, N // bn, nk), out_shape=out_shape,
        in_specs=[pl.BlockSpec((tk, bm), lambda mi, ni, k: (k, mi)), pl.BlockSpec((tk, bn), lambda mi, ni, k: (k, ni))],
        out_specs=out_spec, scratch_shapes=[pltpu.VMEM((bm, bn), F32)],
        compiler_params=_cp(("parallel", "parallel", "arbitrary")),
    )(a, b)


def _wgrad_in(h1, pieces, tk=1024):
    S = h1.shape[0]
    tk = _token_tile(S, tk)
    widths = [p.shape[1] for p in pieces]
    offs = [sum(widths[:t]) for t in range(len(widths))]

    def body(a_ref, *refs):
        o_ref, acc_ref = refs[-2:]

        @pl.when(pl.program_id(0) == 0)
        def _():
            acc_ref[...] = jnp.zeros_like(acc_ref)

        for b_ref, off, w in zip(refs[:-2], offs, widths):
            acc_ref[:, off:off + w] += lax.dot_general(a_ref[...], b_ref[...], TN, preferred_element_type=F32)

        @pl.when(pl.program_id(0) == S // tk - 1)
        def _():
            o_ref[...] = acc_ref[...].astype(BF16)

    return pl.pallas_call(
        body, name="wgrad_in", grid=(S // tk,), out_shape=_sds((D, NP), BF16),
        in_specs=[_rows(tk, D)] + [_rows(tk, w) for w in widths], out_specs=_const((D, NP)),
        scratch_shapes=[pltpu.VMEM((D, NP), F32)], compiler_params=_cp(("arbitrary",)),
    )(h1, *pieces)


def _colsum8(t):
    return jnp.sum(t.reshape(t.shape[0] // 8, 8, t.shape[1]), axis=0)


def _bwd_mid(dh2, dy, x2, ob, o_attn, u1, mod8, n2g, beta_a, beta_c, lng, lnb, w_out, e512, tm, tq, ex=None):
    S = dy.shape[0]

    def body(dh2_ref, dy_ref, x2_ref, ob_ref, oa_ref, u1_ref, mod_ref, n2g_ref, ba_ref, bc_ref, lng_ref, lnb_ref, w_ref,
             e_ref, dx2_ref, do_ref, doa_ref, du1_ref, acc_d_ref, acc_h_ref, dr_ref):
        @pl.when(pl.program_id(0) == 0)
        def _():
            acc_d_ref[...] = jnp.zeros_like(acc_d_ref)
            acc_h_ref[...] = jnp.zeros_like(acc_h_ref)

        x2 = x2_ref[...]
        dh2 = dh2_ref[...]
        r2 = lax.rsqrt(jnp.mean(x2 * x2, axis=-1, keepdims=True) + EPS)
        xn2 = x2 * r2
        gain = n2g_ref[...] * (1.0 + mod_ref[4:5, :])
        dxn = dh2 * gain
        dx2 = dy_ref[...] + r2 * (dxn - xn2 * jnp.mean(dxn * xn2, axis=-1, keepdims=True))
        dx2_ref[...] = dx2
        t = dh2 * xn2
        acc_d_ref[0] += _colsum8(dh2)
        acc_d_ref[1] += _colsum8(t * n2g_ref[...])
        acc_d_ref[2] += _colsum8(t * (1.0 + mod_ref[4:5, :]))
        acc_d_ref[3] += _colsum8(dx2 * ob_ref[...].astype(F32))
        do = (dx2 * mod_ref[2:3, :]).astype(BF16)
        do_ref[...] = do
        dma = lax.dot_general(do, w_ref[0:AW, :], NT, preferred_element_type=F32)
        dmc = lax.dot_general(do, w_ref[AW:D, :], NT, preferred_element_type=F32)
        ov = oa_ref[...]
        ra = lax.rsqrt(jnp.mean(ov * ov, axis=-1, keepdims=True) + EPS)
        on = ov * ra
        acc_h_ref[0] += _colsum8(dma * on)
        don = dma * ba_ref[...]
        doa = (ra * (don - on * jnp.mean(don * on, axis=-1, keepdims=True))).astype(BF16)
        doa_ref[...] = doa
        delta_t = _dot2(doa.astype(F32) * ov, e_ref[...]).T
        for p in range(NH // 2):
            dr_ref[p, 0] = delta_t[2 * p:2 * p + 8, :]
        u1 = u1_ref[...]
        mu = jnp.mean(u1, axis=-1, keepdims=True)
        d = u1 - mu
        rstd = lax.rsqrt(jnp.mean(d * d, axis=-1, keepdims=True) + EPS)
        uh = d * rstd
        u2 = uh * lng_ref[...] + lnb_ref[...]
        sg = _sigmoid(u2)
        u3 = u2 * sg
        rc = lax.rsqrt(jnp.mean(u3 * u3, axis=-1, keepdims=True) + EPS)
        u3n = u3 * rc
        acc_h_ref[1] += _colsum8(dmc * u3n)
        du3n = dmc * bc_ref[...]
        du3 = rc * (du3n - u3n * jnp.mean(du3n * u3n, axis=-1, keepdims=True))
        du2 = du3 * (sg * (1.0 + u2 * (1.0 - sg)))
        acc_h_ref[2] += _colsum8(du2 * uh)
        acc_h_ref[3] += _colsum8(du2)
        duh = du2 * lng_ref[...]
        du1_ref[...] = rstd * (duh - jnp.mean(duh, axis=-1, keepdims=True)
                               - uh * jnp.mean(duh * uh, axis=-1, keepdims=True))

    per = tq // tm
    outs = pl.pallas_call(
        _hosted(body, ex, 14, 7, 0, S // tm), name="bwd_mid", grid=(S // tm,),
        out_shape=(_sds((S, D), F32), _sds((S, D), BF16), _sds((S, AW), BF16), _sds((S, CW), F32),
                   _sds((4, 8, D), F32), _sds((4, 8, AW), F32), _sds((NH // 2, S // tq, 8, tq), F32),
                   *(ex.out_shapes if ex else [])),
        in_specs=[_rows(tm, D), _rows(tm, D), _rows(tm, D), _rows(tm, D), _rows(tm, AW), _rows(tm, CW), _const((8, D)),
                  _const((1, D)), _const((1, AW)), _const((1, CW)), _const((1, CW)), _const((1, CW)), _const((D, D)),
                  _const((AW, LANES)), *(ex.in_specs if ex else [])],
        out_specs=(_rows(tm, D), _rows(tm, D), _rows(tm, AW), _rows(tm, CW), _const((4, 8, D)), _const((4, 8, AW)),
                   pl.BlockSpec((NH // 2, 1, 8, tm), lambda i: (0, i // per, 0, i % per)),
                   *(ex.out_specs if ex else [])),
        scratch_shapes=ex.scratch if ex else [],
        compiler_params=_cp(("arbitrary",)),
    )(dh2, dy, x2, ob, o_attn, u1, mod8, n2g, beta_a, beta_c, lng, lnb, w_out, e512, *(ex.srcs if ex else []))
    return outs[:7], list(outs[7:])


def _attn_bwd(qa, ka, v, do, lser, dr, fs, fe, bd, tq):
    S = qa.shape[1]
    nq = S // tq

    def body(fs_ref, fe_ref, bd_ref, ka_ref, v_ref, qa_ref, do_ref, lse_ref, dr_ref, dqa_hbm, dka_ref, dv_ref,
             accq_ref, acck_ref, accv_ref, out_sem):
        pr, j = pl.program_id(0), pl.program_id(1)
        sel_a = _lane((tq, LANES)) < DH
        vv = v_ref[...]
        zb = jnp.zeros_like(vv)
        vs = [jnp.where(sel_a, vv, zb), jnp.where(sel_a, zb, vv)]
        acck_ref[...] = jnp.zeros_like(acck_ref)
        accv_ref[...] = jnp.zeros_like(accv_ref)

        @pl.when(j == 0)
        def _():
            accq_ref[...] = jnp.zeros_like(accq_ref)

        def q_step(i, heads, masked=False):
            start = pl.multiple_of(i * tq, tq)
            dob = do_ref[pl.ds(start, tq), :]
            lse8 = lse_ref[0, i]
            dr8 = dr_ref[0, i]
            for hh in heads:
                qb = qa_ref[hh, pl.ds(start, tq), :]
                kb = ka_ref[hh]
                st = lax.dot_general(kb, qb, NT, preferred_element_type=F32)
                pt = jnp.exp(st - lse8[hh:hh + 1, :])
                if masked:
                    keep = (lax.broadcasted_iota(jnp.int32, (tq, tq), 0)
                            <= lax.broadcasted_iota(jnp.int32, (tq, tq), 1))
                    pt = jnp.where(keep, pt, 0.0)
                accv_ref[hh] += jnp.dot(pt.astype(BF16), dob, preferred_element_type=F32)
                dpt = lax.dot_general(vs[hh], dob, NT, preferred_element_type=F32)
                dst = (pt * (dpt - dr8[hh:hh + 1, :])).astype(BF16)
                acck_ref[hh] += jnp.dot(dst, qb, preferred_element_type=F32)
                accq_ref[hh, pl.ds(start, tq), :] += lax.dot_general(dst, kb, TN, preferred_element_type=F32)

        def needed(hh, i):
            top = fs_ref[2 * pr + hh, jnp.minimum(i, nq - 1)] + bd_ref[0, 0]
            return jnp.logical_and(i < nq, top - fe_ref[2 * pr + hh, j] >= SKIP_P)

        q_step(j, (0, 1), masked=True)
        _block_loops(j + 1, 1, needed, q_step)
        dka_ref[...] = acck_ref[...]
        dv_ref[...] = jnp.where(sel_a, accv_ref[0], accv_ref[1]).astype(BF16)

        @pl.when(j == nq - 1)
        def _():
            out = pltpu.make_async_copy(accq_ref, dqa_hbm.at[pl.ds(2 * pr, 2)], out_sem)
            out.start()
            out.wait()

    once = pl.Buffered(1)
    return pl.pallas_call(
        body, name="attn_bwd", grid=(NH // 2, nq),
        out_shape=(_sds((NH, S, LANES), F32), _sds((NH, S, LANES), F32), _sds((S, AW), BF16)),
        in_specs=[SMEM_SPEC, SMEM_SPEC, SMEM_SPEC,
                  pl.BlockSpec((2, tq, LANES), lambda p, j: (p, j, 0)),
                  pl.BlockSpec((tq, LANES), lambda p, j: (j, p)),
                  pl.BlockSpec((2, S, LANES), lambda p, j: (p, 0, 0), pipeline_mode=once),
                  pl.BlockSpec((S, LANES), lambda p, j: (0, p), pipeline_mode=once),
                  pl.BlockSpec((1, nq, 8, tq), lambda p, j: (p, 0, 0, 0)),
                  pl.BlockSpec((1, nq, 8, tq), lambda p, j: (p, 0, 0, 0))],
        out_specs=(pl.BlockSpec(memory_space=pl.ANY),
                   pl.BlockSpec((2, tq, LANES), lambda p, j: (p, j, 0)),
                   pl.BlockSpec((tq, LANES), lambda p, j: (j, p))),
        scratch_shapes=[pltpu.VMEM((2, S, LANES), F32), pltpu.VMEM((2, tq, LANES), F32),
                        pltpu.VMEM((2, tq, LANES), F32), pltpu.SemaphoreType.DMA],
        compiler_params=_cp(("arbitrary", "arbitrary")),
    )(fs, fe, bd, ka, v, qa, do, lser, dr)


def _bwd_conv(du1, u0, alin, agate, w32, tm, ex=None):
    S = du1.shape[0]
    nt = S // tm

    def body(du_ref, dun_ref, u0_ref, u0p_ref, alin_ref, agate_ref, w_ref,
             dalin_ref, dagate_ref, dw_ref, db_ref, extd_ref, extu_ref, du0_ref, shd_ref, shu_ref):
        i = pl.program_id(0)

        @pl.when(i == 0)
        def _():
            dw_ref[...] = jnp.zeros_like(dw_ref)
            db_ref[...] = jnp.zeros_like(db_ref)

        extd_ref[0:tm, :] = du_ref[...]
        extd_ref[tm:, :] = jnp.where(i == nt - 1, 0.0, dun_ref[...])
        extu_ref[0:HALO, :] = jnp.where(i == 0, 0.0, u0p_ref[...])
        extu_ref[HALO:, :] = u0_ref[...]
        _shifted_copies(extd_ref, shd_ref, tm)
        _shifted_copies(extu_ref, shu_ref, tm)
        db_ref[...] += _colsum8(du_ref[...])
        for r0 in range(0, tm, BWD_CHUNK):
            acc = jnp.zeros((BWD_CHUNK, CW), F32)
            for j, rows in _tap_windows(extd_ref, shd_ref, r0, lambda j: 30 - j, BWD_CHUNK, shared=True):
                acc = acc + w_ref[j:j + 1, :] * rows
            du0_ref[r0:r0 + BWD_CHUNK, :] = acc
            duc = du_ref[r0:r0 + BWD_CHUNK, :]
            for j, rows in _tap_windows(extu_ref, shu_ref, r0, lambda j: 2 + j, BWD_CHUNK, shared=True):
                dw_ref[j] += _colsum8(duc * rows)
        du0 = du0_ref[...]
        al = alin_ref[...].astype(F32)
        sg = _sigmoid(agate_ref[...].astype(F32))
        dalin_ref[...] = (du0 * sg).astype(BF16)
        dagate_ref[...] = (du0 * al * sg * (1.0 - sg)).astype(BF16)

    nxt = pl.BlockSpec((HALO, CW), lambda i: (jnp.minimum((i + 1) * (tm // HALO), S // HALO - 1), 0))
    outs = pl.pallas_call(
        _hosted(body, ex, 7, 4, 5, nt), name="bwd_conv", grid=(nt,),
        out_shape=(_sds((S, CW), BF16), _sds((S, CW), BF16), _sds((HALO, 8, CW), F32), _sds((8, CW), F32),
                   *(ex.out_shapes if ex else [])),
        in_specs=[_rows(tm, CW), nxt, _rows(tm, CW), _halo_prev(tm), _rows(tm, CW), _rows(tm, CW), _const((HALO, CW)),
                  *(ex.in_specs if ex else [])],
        out_specs=(_rows(tm, CW), _rows(tm, CW), _const((HALO, 8, CW)), _const((8, CW)), *(ex.out_specs if ex else [])),
        scratch_shapes=[pltpu.VMEM((tm + HALO, CW), F32), pltpu.VMEM((tm + HALO, CW), F32), pltpu.VMEM((tm, CW), F32),
                        pltpu.VMEM((SUBLANES - 1, tm + HALO, CW), F32), pltpu.VMEM((SUBLANES - 1, tm + HALO, CW), F32),
                        *(ex.scratch if ex else [])],
        compiler_params=_cp(("arbitrary",)),
    )(du1, du1, u0, u0, alin, agate, w32, *(ex.srcs if ex else []))
    return outs[:4], list(outs[4:])


def _bwd_qk(dqa, dka, qn, kn, rq, rk, fgb, qg512, kg512, e512, et512, tm):
    S = qn.shape[0]
    nt = S // tm

    def body(dqa_ref, dka_ref, qn_ref, kn_ref, rq_ref, rk_ref, fgb_ref, qg_ref, kg_ref, e_ref, et_ref,
             dq_ref, dk_ref, dfg_ref, accg_ref, accb_ref, carry_ref):
        @pl.when(pl.program_id(0) == 0)
        def _():
            carry_ref[...] = jnp.zeros_like(carry_ref)
            accg_ref[...] = jnp.zeros_like(accg_ref)
            accb_ref[...] = jnp.zeros_like(accb_ref)

        lane = _lane((tm, LANES))
        sel_a = lane < DH
        df = jnp.zeros((tm, LANES), F32)
        for h in range(NH):
            col = dqa_ref[h][:, 64:65] - dka_ref[h][:, 67:68]
            df = jnp.where(lane == h, col, df)
        tri = (lax.broadcasted_iota(jnp.int32, (tm, tm), 0) <= lax.broadcasted_iota(jnp.int32, (tm, tm), 1)
               ).astype(F32).astype(BF16)
        dlf = _dot3(tri, df) + carry_ref[0:1, :]
        carry_ref[...] = jnp.broadcast_to(dlf[0:1, :], carry_ref.shape)
        dfg = jnp.where(lane < NH, dlf * _sigmoid(-fgb_ref[...]), 0.0)
        dfg_ref[...] = dfg.astype(BF16)
        accb_ref[...] += _colsum8(dfg)

        def norm_bwd(src_ref, n_ref, r_ref, g_ref, scale, slot):
            pairs = []
            for p in range(NH // 2):
                b = pltpu.roll(src_ref[2 * p + 1], 64, 1)
                pairs.append(jnp.where(sel_a, src_ref[2 * p], b))
            dh = jnp.concatenate(pairs, axis=1) * scale
            tn = n_ref[...].astype(F32)
            accg_ref[slot] += _colsum8(dh * tn)
            dn = dh * g_ref[...]
            mean = _dot2(dn * tn, e_ref[...]) * (1.0 / DH)
            corr = _dot2(mean, et_ref[...])
            rf = _dot2(r_ref[...], et_ref[...])
            return (rf * (dn - tn * corr)).astype(BF16)

        dq_ref[...] = norm_bwd(dqa_ref, qn_ref, rq_ref, qg_ref, DH ** -0.5, 0)
        dk_ref[...] = norm_bwd(dka_ref, kn_ref, rk_ref, kg_ref, 1.0, 1)

    rev = lambda n: pl.BlockSpec((tm, n), lambda i: (nt - 1 - i, 0))
    hm = pl.BlockSpec((NH, tm, LANES), lambda i: (0, nt - 1 - i, 0))
    dq, dk, dfg, accg, accb = pl.pallas_call(
        body, name="bwd_qk", grid=(nt,),
        out_shape=(_sds((S, AW), BF16), _sds((S, AW), BF16), _sds((S, LANES), BF16), _sds((2, 8, AW), F32),
                   _sds((8, LANES), F32)),
        in_specs=[hm, hm, rev(AW), rev(AW), rev(LANES), rev(LANES), rev(LANES), _const((1, AW)), _const((1, AW)),
                  _const((AW, LANES)), _const((LANES, AW))],
        out_specs=(rev(AW), rev(AW), rev(LANES), _const((2, 8, AW)), _const((8, LANES))),
        scratch_shapes=[pltpu.VMEM((8, LANES), F32)], compiler_params=_cp(("arbitrary",)),
    )(dqa, dka, qn, kn, rq, rk, fgb, qg512, kg512, e512, et512)
    return dq, dk, dfg, accg, accb


def _bwd_in(dq, dk, dv, dalin, dagate, dfg, w_in_p, x, dx2, mod8, n1g, tm, ex=None):
    S = x.shape[0]

    def body(dq_ref, dk_ref, dv_ref, dal_ref, dag_ref, dfg_ref, w_ref, x_ref, dx2_ref, mod_ref, n1g_ref,
             dx_ref, acc_ref):
        @pl.when(pl.program_id(0) == 0)
        def _():
            acc_ref[...] = jnp.zeros_like(acc_ref)

        def part(ref, a, b):
            return lax.dot_general(ref[...], w_ref[:, a:b], NT, preferred_element_type=F32)

        dh = (part(dq_ref, 0, 512) + part(dk_ref, 512, 1024) + part(dv_ref, 1024, 1536) + part(dal_ref, 1536, 2048)
              + part(dag_ref, 2048, 2560) + part(dfg_ref, 2560, NP))
        xv = x_ref[...]
        r1 = lax.rsqrt(jnp.mean(xv * xv, axis=-1, keepdims=True) + EPS)
        xn = xv * r1
        gain = n1g_ref[...] * (1.0 + mod_ref[1:2, :])
        t = dh * xn
        acc_ref[0] += _colsum8(dh)
        acc_ref[1] += _colsum8(t * n1g_ref[...])
        acc_ref[2] += _colsum8(t * (1.0 + mod_ref[1:2, :]))
        dxn = dh * gain
        dx_ref[...] = dx2_ref[...] + r1 * (dxn - xn * jnp.mean(dxn * xn, axis=-1, keepdims=True))

    outs = pl.pallas_call(
        _hosted(body, ex, 11, 2, 0, S // tm), name="bwd_in", grid=(S // tm,),
        out_shape=(_sds((S, D), F32), _sds((3, 8, D), F32), *(ex.out_shapes if ex else [])),
        in_specs=[_rows(tm, AW), _rows(tm, AW), _rows(tm, AW), _rows(tm, CW), _rows(tm, CW), _rows(tm, LANES),
                  _const((D, NP)), _rows(tm, D), _rows(tm, D), _const((8, D)), _const((1, D)),
                  *(ex.in_specs if ex else [])],
        out_specs=(_rows(tm, D), _const((3, 8, D)), *(ex.out_specs if ex else [])),
        scratch_shapes=ex.scratch if ex else [],
        compiler_params=_cp(("arbitrary",)),
    )(dq, dk, dv, dalin, dagate, dfg, w_in_p, x, dx2, mod8, n1g, *(ex.srcs if ex else []))
    return outs[:2], list(outs[2:])


def _adam(w, g, m, v):
    m_new = B1 * m + (1.0 - B1) * g
    v_new = B2 * v + (1.0 - B2) * (g * g)
    m_hat = m_new / (1.0 - B1 ** STEP)
    v_hat = v_new / (1.0 - B2 ** STEP)
    delta = -LR * (m_hat / (jnp.sqrt(v_hat) + AEPS) + WD * w)
    return delta, m_new, v_new


def _pair_adamw(slots, w, m, v, name, tr=512):
    ns, R, C = slots.shape
    tr = tr if R % tr == 0 else R
    nt = R // tr

    def body(s_ref, w_ref, m_ref, v_ref, g_ref, d_ref, mo_ref, vo_ref, mine_ref, theirs_ref, send_sems, recv_sems):
        i = pl.program_id(0)
        part = s_ref[0].astype(F32)
        for k in range(1, ns):
            part = part + s_ref[k].astype(F32)
        mine_ref[i] = part
        swap = pltpu.make_async_remote_copy(
            src_ref=mine_ref.at[i], dst_ref=theirs_ref.at[i], send_sem=send_sems.at[i], recv_sem=recv_sems.at[i],
            device_id=(lax.axis_index("x"), lax.axis_index("y"), 1 - lax.axis_index("c")),
            device_id_type=pl.DeviceIdType.MESH)
        swap.start()
        swap.wait()
        g = part + theirs_ref[i]
        g_ref[...] = g
        d_ref[...], mo_ref[...], vo_ref[...] = _adam(w_ref[...], g, m_ref[...], v_ref[...])

    blk = pl.BlockSpec((tr, C), lambda i: (i, 0))
    return pl.pallas_call(
        body, name=name, grid=(nt,), out_shape=tuple(_sds((R, C), F32) for _ in range(4)),
        in_specs=[pl.BlockSpec((ns, tr, C), lambda i: (0, i, 0)), blk, blk, blk], out_specs=(blk, blk, blk, blk),
        scratch_shapes=[pltpu.VMEM((nt, tr, C), F32), pltpu.VMEM((nt, tr, C), F32),
                        pltpu.SemaphoreType.DMA((nt,)), pltpu.SemaphoreType.DMA((nt,))],
        compiler_params=_cp(("arbitrary",)),
    )(slots, w, m, v)


def _ada_adamw(sct, dmod, w, m, v):
    R, C = w.shape
    tr, bc = 256, 512

    def body(sct_ref, dm_ref, w_ref, m_ref, v_ref, g_ref, d_ref, mo_ref, vo_ref):
        g = sct_ref[:, 0:1] * dm_ref[0:1, :]
        for b in range(1, N_DEV):
            g = g + sct_ref[:, b:b + 1] * dm_ref[b:b + 1, :]
        g_ref[...] = g
        d_ref[...], mo_ref[...], vo_ref[...] = _adam(w_ref[...], g, m_ref[...], v_ref[...])

    blk = pl.BlockSpec((tr, bc), lambda i, j: (i, j))
    return pl.pallas_call(
        body, name="ada_adamw", grid=(R // tr, C // bc), out_shape=tuple(_sds((R, C), F32) for _ in range(4)),
        in_specs=[pl.BlockSpec((tr, N_DEV), lambda i, j: (i, 0)), pl.BlockSpec((N_DEV, bc), lambda i, j: (0, j)),
                  blk, blk, blk],
        out_specs=(blk, blk, blk, blk), compiler_params=_cp(("parallel", "parallel")),
    )(sct, dmod, w, m, v)


PACK = {"dmod": 0, "norm1_g": 6144, "norm2_g": 7168, "q_norm_g": 8192, "k_norm_g": 8704, "b_f": 9216, "conv_b": 9344,
        "conv_ln_g": 9856, "conv_ln_b": 10368, "beta_attn": 10880, "beta_conv": 11392, "loss": 11904}
SMALL_NAMES = ["b_ada", "norm1_g", "q_norm_g", "k_norm_g", "b_f", "conv_b", "conv_ln_g", "conv_ln_b", "beta_attn",
               "beta_conv", "norm2_g"]


def _pack_small(acc1, acc_d, acc_h, dg2, accg, accb, dcb, loss8):
    def body(a1_ref, ad_ref, ah_ref, dg2_ref, ag_ref, ab_ref, cb_ref, loss_ref, o_ref):
        def put(off, rows):
            o_ref[:, off:off + rows.shape[1]] = jnp.sum(rows, axis=0, keepdims=True)

        for t, rows in enumerate((a1_ref[0], a1_ref[1], ad_ref[3], ad_ref[0], ad_ref[1], dg2_ref[...])):
            put(PACK["dmod"] + t * D, rows)
        put(PACK["norm1_g"], a1_ref[2])
        put(PACK["norm2_g"], ad_ref[2])
        put(PACK["q_norm_g"], ag_ref[0])
        put(PACK["k_norm_g"], ag_ref[1])
        put(PACK["b_f"], ab_ref[...])
        put(PACK["conv_b"], cb_ref[...])
        put(PACK["conv_ln_g"], ah_ref[2])
        put(PACK["conv_ln_b"], ah_ref[3])
        put(PACK["beta_attn"], ah_ref[0])
        put(PACK["beta_conv"], ah_ref[1])
        o_ref[:, PACK["loss"]:PACK["loss"] + LANES] = loss_ref[0:1, :]

    vm = pl.BlockSpec(memory_space=pltpu.VMEM)
    return pl.pallas_call(body, name="pack_small", out_shape=_sds((1, SMALL_IN), F32), in_specs=[vm] * 8, out_specs=vm,
                          )(acc1, acc_d, acc_h, dg2, accg, accb, dcb, loss8)


def _small_adamw(slots, fold, ws, ms, vs):
    n = len(ws)
    widths = [w.shape[1] for w in ws]

    def body(s_ref, f_ref, *refs):
        w_refs, m_refs, v_refs = refs[:n], refs[n:2 * n], refs[2 * n:3 * n]
        outs = refs[3 * n:]
        tot = s_ref[0:1, :]
        for k in range(1, N_DEV):
            tot = tot + s_ref[k:k + 1, :]

        def grad(name, width):
            if name == "b_ada":
                return tot[:, 0:6 * D]
            seg = tot[:, PACK[name]:PACK[name] + max(width, LANES)]
            if name in ("q_norm_g", "k_norm_g"):
                seg = jnp.dot(jnp.broadcast_to(tot[:, PACK[name]:PACK[name] + AW], (8, AW)), f_ref[...], precision=HI,
                              preferred_element_type=F32)[0:1, :]
            return seg[:, 0:width]

        for t, (name, width) in enumerate(zip(SMALL_NAMES, widths)):
            g = grad(name, width)
            d, m_new, v_new = _adam(w_refs[t][...], g, m_refs[t][...], v_refs[t][...])
            outs[t][...] = g
            outs[n + t][...] = d
            outs[2 * n + t][...] = m_new
            outs[3 * n + t][...] = v_new
        outs[4 * n][...] = tot[:, PACK["loss"]:PACK["loss"] + LANES]

    vm = pl.BlockSpec(memory_space=pltpu.VMEM)
    outs = pl.pallas_call(
        body, name="adamw_small",
        out_shape=(*(_sds((1, w), F32) for _ in range(4) for w in widths), _sds((1, LANES), F32)),
        in_specs=[vm] * (2 + 3 * n), out_specs=tuple(vm for _ in range(4 * n + 1)),
    )(slots, fold, *ws, *ms, *vs)
    return [list(outs[k * n:(k + 1) * n]) for k in range(4)], outs[4 * n][0, 0]


def _perm_in(w):
    pad = jnp.zeros((w.shape[0], NP - 2568), w.dtype)
    return jnp.concatenate([w[:, :1536], w[:, 1544:2568], w[:, 1536:1544], pad], axis=1)


def _pad_lanes(vec, n=LANES):
    return jnp.pad(vec, ((0, 0), (0, n - vec.shape[1])))


def kernel(x, c, w_ada, b_ada, norm1_g, w_in, q_norm_g, k_norm_g, b_f, conv_w, conv_b, conv_ln_g, conv_ln_b, beta_attn, beta_conv, w_out, norm2_g, w_ff1, w_ff2, loss_target, m_w_ada, m_b_ada, m_norm1_g, m_w_in, m_q_norm_g, m_k_norm_g, m_b_f, m_conv_w, m_conv_b, m_conv_ln_g, m_conv_ln_b, m_beta_attn, m_beta_conv, m_w_out, m_norm2_g, m_w_ff1, m_w_ff2, v_w_ada, v_b_ada, v_norm1_g, v_w_in, v_q_norm_g, v_k_norm_g, v_b_f, v_conv_w, v_conv_b, v_conv_ln_g, v_conv_ln_b, v_beta_attn, v_beta_conv, v_w_out, v_norm2_g, v_w_ff1, v_w_ff2):
    S = x.shape[1]
    tm = min(256, S)
    tw = min(512, S)
    tq = min(512, S // 2)
    xs, tgt = x[0], loss_target[0]
    chip = 2 * lax.axis_index("x") + lax.axis_index("y")
    e512, et512 = _head_sum_mats()

    conv_w32 = jnp.pad(conv_w[0], ((0, 1), (0, 0)))
    c_all, g_in = _exchange([(c, "bcast8"), (w_in[0].astype(BF16), "chip4")], "gather_in")
    later_weights = _Exchange([(w_out[0].astype(BF16), "chip4"), (conv_w32, "chip4")])
    c_all = c_all.reshape(N_DEV, D)
    w_in_p = _perm_in(jnp.transpose(g_in, (1, 0, 2)).reshape(D, 2568))

    b_shard = lax.dynamic_slice(b_ada, (0, chip * 1536), (1, 1536))
    mod_rows, sc_all = _mod_shard(c_all, w_ada[0], b_shard)
    (mod_slots,) = _exchange([(mod_rows.reshape(N_DEV, 1, 1536), "all8")], "scatter_mod")
    mod = mod_slots.reshape(4, 2, 1536)[:, 0, :].reshape(6, D)
    mod8 = jnp.pad(mod, ((0, 2), (0, 0)))

    qg512 = jnp.tile(q_norm_g, (1, NH))
    kg512 = jnp.tile(k_norm_g, (1, NH))
    bf128 = _pad_lanes(b_f)

    (h1, qh, kh, vb, qn, kn, rq, rk, fgb, alin, agate, u0), (g_out, g_cw) = _fwd_in(
        xs, mod8, norm1_g, w_in_p, e512, et512, qg512, kg512, bf128, tw, ex=later_weights)
    w_out_f = g_out.reshape(D, D)
    cw32 = jnp.transpose(g_cw, (1, 0, 2)).reshape(HALO, CW)
    bd = _logit_bound(q_norm_g, k_norm_g)
    (qa, ka, va, fcum), (w1,) = _fwd_decay(fgb, qh, kh, vb, _shift(bd), tw,
                                           ex=_Exchange([(w_ff1[0].astype(BF16), "chip4")]))
    fs, fe = _skip_tables(fcum, tq)
    o_attn, lser = _attn_fwd(qa, ka, va, fs, fe, bd, tq)
    (u1, mc), (w2,) = _fwd_conv(u0, cw32, conv_b, conv_ln_g, conv_ln_b, beta_conv, tw,
                                ex=_Exchange([(w_ff2[0].astype(BF16), "chip4")]))
    merged, ob, x2, h2 = _fwd_out(o_attn, mc, xs, mod8, norm2_g, beta_attn, w_out_f, tw)
    r, dy, loss8, dg2 = _fwd_ffn(h2, w1, w2, x2, tgt, mod8, tw)

    df2, df1, dh2 = _bwd_ffn(dy, mod8, r, w1, w2, tw)
    gw_ff2 = _wgrad(r, df2, "wgrad_ff2", square_a=True)
    gw_ff1 = _wgrad(h2, df1, "wgrad_ff1", col_pieces=True)
    (dx2, do, doa, du1, acc_d, acc_h, dr), (p_ff1,) = _bwd_mid(
        dh2, dy, x2, ob, o_attn, u1, mod8, norm2_g, beta_attn, beta_conv, conv_ln_g, conv_ln_b, w_out_f, e512, tm, tq,
        ex=_Exchange([(gw_ff1, "chip4p")]))
    gw_out = _wgrad(merged, do, "wgrad_out")
    dqa, dka, dv = _attn_bwd(qa, ka, vb, doa, lser, dr, fs, fe, bd, tq)
    (dalin, dagate, dcw, dcb), (p_out, p_ff2) = _bwd_conv(
        du1, u0, alin, agate, cw32, tw,
        ex=_Exchange([(gw_out.reshape(4, 256, D), "chip4p"), (gw_ff2.reshape(4, D, D), "chip4p")]))
    dq, dk, dfg, accg, accb = _bwd_qk(dqa, dka, qn, kn, rq, rk, fgb, qg512, kg512, e512, et512, tw)
    gw_in_p = _wgrad_in(h1, [dq, dk, dv, dalin, dagate, dfg])
    gw_in = jnp.concatenate([gw_in_p[:, :1536], gw_in_p[:, 2560:2568], gw_in_p[:, 1536:2560]], axis=1)
    s8 = lambda a: jnp.sum(a, axis=-2)
    gcw = s8(dcw)
    in_grads = _Exchange([(jnp.transpose(gw_in.reshape(D, 4, 642), (1, 0, 2)), "chip4p"),
                          (jnp.transpose(gcw.reshape(HALO, 4, LANES), (1, 0, 2)), "chip4p")])
    (grad_x, acc1), (p_in, p_cw) = _bwd_in(dq, dk, dv, dalin, dagate, dfg, w_in_p, xs, dx2, mod8, norm1_g, tw,
                                           ex=in_grads)

    small = _pack_small(acc1, acc_d, acc_h, dg2, accg, accb, dcb, loss8)
    (small_s,) = _exchange([(small, "bcast8")], "gather_small")
    small_s = small_s.reshape(N_DEV, SMALL_IN)

    g_in_, d_in, nm_in, nv_in = _pair_adamw(p_in, w_in[0], m_w_in[0], v_w_in[0], "adamw_in")
    g_out_, d_out, nm_out, nv_out = _pair_adamw(p_out, w_out[0], m_w_out[0], v_w_out[0], "adamw_out")
    g_f1, d_f1, nm_f1, nv_f1 = _pair_adamw(p_ff1, w_ff1[0], m_w_ff1[0], v_w_ff1[0], "adamw_ff1")
    g_f2, d_f2, nm_f2, nv_f2 = _pair_adamw(p_ff2, w_ff2[0], m_w_ff2[0], v_w_ff2[0], "adamw_ff2")
    pad_row = lambda a, fill: jnp.pad(a[0], ((0, 1), (0, 0)), constant_values=fill)
    g_cw_, d_cw, nm_cw, nv_cw = (a[:KC] for a in _pair_adamw(
        p_cw, pad_row(conv_w, 0.0), pad_row(m_conv_w, 0.0), pad_row(v_conv_w, 1.0), "adamw_conv_w"))
    dmod_shard = lax.dynamic_slice(small_s[:, :6 * D], (0, chip * 1536), (N_DEV, 1536))
    g_ada, d_ada, nm_ada, nv_ada = _ada_adamw(sc_all.T, dmod_shard, w_ada[0], m_w_ada[0], v_w_ada[0])

    fold = np.zeros((AW, LANES), np.float32)
    fold[np.arange(AW), np.arange(AW) % DH] = 1.0
    smalls = [b_ada, norm1_g, q_norm_g, k_norm_g, b_f, conv_b, conv_ln_g, conv_ln_b, beta_attn, beta_conv, norm2_g]
    m_smalls = [m_b_ada, m_norm1_g, m_q_norm_g, m_k_norm_g, m_b_f, m_conv_b, m_conv_ln_g, m_conv_ln_b, m_beta_attn,
                m_beta_conv, m_norm2_g]
    v_smalls = [v_b_ada, v_norm1_g, v_q_norm_g, v_k_norm_g, v_b_f, v_conv_b, v_conv_ln_g, v_conv_ln_b, v_beta_attn,
                v_beta_conv, v_norm2_g]
    (gs, ds, ms, vs), loss = _small_adamw(small_s, jnp.asarray(fold), smalls, m_smalls, v_smalls)

    big ={"w_ada": (g_ada, d_ada, nm_ada, nv_ada), "w_in": (g_in_, d_in, nm_in, nv_in),
           "conv_w": (g_cw_, d_cw, nm_cw, nv_cw), "w_out": (g_out_, d_out, nm_out, nv_out),
           "w_ff1": (g_f1, d_f1, nm_f1, nv_f1), "w_ff2": (g_f2, d_f2, nm_f2, nv_f2)}
    order =["w_ada", "b_ada", "norm1_g", "w_in", "q_norm_g", "k_norm_g", "b_f", "conv_w", "conv_b", "conv_ln_g",
             "conv_ln_b", "beta_attn", "beta_conv", "w_out", "norm2_g", "w_ff1", "w_ff2"]

    def leaf(name, which):
        if name in big:
            return big[name][which][None]
        return (gs, ds, ms, vs)[which][SMALL_NAMES.index(name)]

    return (loss, grad_x[None], *[leaf(n, 0) for n in order], *[leaf(n, 1) for n in order],
            *[leaf(n, 2) for n in order], *[leaf(n, 3) for n in order])
```
